```python
import math
import jax, jax.numpy as jnp
from jax import lax
import numpy as np

D_MODEL = 1024
BATCH = 8
SEQ = 2048
DEPTH = 1

MEM_LEN = 256
HEAD_DIM = 64
ATTN_WIDTH = D_MODEL // 2
N_Q_HEADS = ATTN_WIDTH // HEAD_DIM
N_KV_HEADS = N_Q_HEADS // 4
Q_PER_KV = N_Q_HEADS // N_KV_HEADS
WINDOW = 128
BLOCK = 128
CONV_CH = D_MODEL // 4
CONV_WIDTH = 31
MEM_WIDTH = D_MODEL // 4
N_MEM_HEADS = MEM_WIDTH // HEAD_DIM
MIX_WIDTH = ATTN_WIDTH + CONV_CH + MEM_WIDTH
ROPE_THETA = 500000.0
ROPE_DIM = HEAD_DIM // 4
D_FF = 256 * math.ceil(8 * D_MODEL / 3 / 256)
EPS = 1e-6

Q_COLS = N_Q_HEADS * HEAD_DIM
KV_COLS = N_KV_HEADS * HEAD_DIM
GLU_COLS = 2 * CONV_CH
MQ_COLS = N_MEM_HEADS * HEAD_DIM
IN_COLS = Q_COLS + 2 * KV_COLS + GLU_COLS + MQ_COLS
SPLITS = [Q_COLS, Q_COLS + KV_COLS, Q_COLS + 2 * KV_COLS, Q_COLS + 2 * KV_COLS + GLU_COLS]

kernel_name = "hymba_conformer_swa_sink_memory_layer"


def rms_norm(x, g):
    xf = x.astype(jnp.float32)
    y = xf * lax.rsqrt(jnp.mean(xf * xf, axis=-1, keepdims=True) + EPS)
    return (y * g.astype(jnp.float32)).astype(x.dtype)


def layer_norm(x, g, b):
    xf = x.astype(jnp.float32)
    mu = jnp.mean(xf, axis=-1, keepdims=True)
    var = jnp.mean(jnp.square(xf - mu), axis=-1, keepdims=True)
    y = (xf - mu) * lax.rsqrt(var + EPS)
    return (y * g.astype(jnp.float32) + b.astype(jnp.float32)).astype(x.dtype)


def swiglu(x, w_gate, w_up, w_down):
    return (jax.nn.silu(x @ w_gate) * (x @ w_up)) @ w_down


def rope_partial(x, positions):
    half = ROPE_DIM // 2
    inv_freq = ROPE_THETA ** (-jnp.arange(half, dtype=jnp.float32) / half)
    ang = positions.astype(jnp.float32)[..., None] * inv_freq
    cos = jnp.cos(ang)[:, :, None, :]
    sin = jnp.sin(ang)[:, :, None, :]
    xr = x[..., :ROPE_DIM].astype(jnp.float32)
    x1, x2 = xr[..., :half], xr[..., half:]
    rot = jnp.concatenate([x1 * cos - x2 * sin, x2 * cos + x1 * sin], axis=-1).astype(x.dtype)
    return jnp.concatenate([rot, x[..., ROPE_DIM:]], axis=-1)


def sliding_window_attention(q, k, v, sinks):
    B, T = q.shape[0], q.shape[1]
    nb = T // BLOCK
    qb = q.reshape(B, nb, BLOCK, N_KV_HEADS, Q_PER_KV, HEAD_DIM)

    def with_prev(a):
        a = a.reshape(B, nb, BLOCK, N_KV_HEADS, HEAD_DIM)
        prev = jnp.pad(a, ((0, 0), (1, 0), (0, 0), (0, 0), (0, 0)))[:, :-1]
        return jnp.concatenate([prev, a], axis=2)

    kb, vb = with_prev(k), with_prev(v)
    s = jnp.einsum('bnqgrd,bnkgd->bngrqk', qb, kb).astype(jnp.float32) * (HEAD_DIM ** -0.5)
    qi = jnp.arange(BLOCK)[:, None] + BLOCK
    ki = jnp.arange(2 * BLOCK)[None, :]
    rel = qi - ki
    band = (rel >= 0) & (rel < WINDOW)
    blk = jnp.arange(nb)[:, None, None]
    valid = band[None] & ((blk > 0) | (ki >= BLOCK)[None])
    s = jnp.where(valid[None, :, None, None], s, -jnp.inf)
    sink = sinks.astype(jnp.float32).reshape(N_KV_HEADS, Q_PER_KV)[None, None, :, :, None, None]
    m = jnp.maximum(jnp.max(s, axis=-1, keepdims=True), sink)
    p = jnp.exp(s - m)
    denom = jnp.sum(p, axis=-1, keepdims=True) + jnp.exp(sink - m)
    w = (p / denom).astype(v.dtype)
    o = jnp.einsum('bngrqk,bnkgd->bnqgrd', w, vb)
    return o.reshape(B, T, N_Q_HEADS * HEAD_DIM)


def memory_cross_attention(q, k, v):
    B, T = q.shape[0], q.shape[1]
    s = jnp.einsum('bthd,bmhd->bhtm', q, k).astype(jnp.float32) * (HEAD_DIM ** -0.5)
    w = jax.nn.softmax(s, axis=-1).astype(v.dtype)
    o = jnp.einsum('bhtm,bmhd->bthd', w, v)
    return o.reshape(B, T, N_MEM_HEADS * HEAD_DIM)


def conformer_conv(u, w_dw, b_dw, g_ln, b_ln):
    a, gate = jnp.split(u, 2, axis=-1)
    h = a * jax.nn.sigmoid(gate)
    h = lax.conv_general_dilated(
        h, w_dw[:, None, :].astype(h.dtype), window_strides=(1,),
        padding=((CONV_WIDTH - 1, 0),), dimension_numbers=('NWC', 'WIO', 'NWC'),
        feature_group_count=CONV_CH) + b_dw
    return jax.nn.silu(layer_norm(h, g_ln, b_ln))


def _fwd_setup_inputs(seed: int = 0) -> dict:
    key = jax.random.key(seed)
    ks = jax.random.split(key, 32)
    f32 = jnp.float32

    def w(k, shape, fan_in):
        return jax.random.normal(k, shape, f32) * (fan_in ** -0.5)

    def gain(k, shape):
        return 1.0 + 0.05 * jax.random.normal(k, shape, f32)

    L = DEPTH
    start = jax.random.randint(ks[2], (BATCH, 1), 0, 1024, dtype=jnp.int32)
    positions = start + jnp.arange(SEQ, dtype=jnp.int32)[None, :]
    return {
        "x": jax.random.normal(ks[0], (BATCH, SEQ, D_MODEL), f32),
        "mem": jax.random.normal(ks[1], (BATCH, MEM_LEN, D_MODEL), f32),
        "positions": positions,
        "g_ffn1": gain(ks[3], (L, D_MODEL)),
        "w_ffn1_gate": w(ks[4], (L, D_MODEL, D_FF), D_MODEL),
        "w_ffn1_up": w(ks[5], (L, D_MODEL, D_FF), D_MODEL),
        "w_ffn1_down": w(ks[6], (L, D_FF, D_MODEL), D_FF),
        "g_mix": gain(ks[7], (L, D_MODEL)),
        "w_in": w(ks[8], (L, D_MODEL, IN_COLS), D_MODEL),
        "g_q": gain(ks[9], (L, HEAD_DIM)),
        "g_k": gain(ks[10], (L, HEAD_DIM)),
        "sinks": 0.5 * jax.random.normal(ks[11], (L, N_Q_HEADS), f32),
        "w_dw": w(ks[12], (L, CONV_WIDTH, CONV_CH), CONV_WIDTH),
        "b_dw": 0.02 * jax.random.normal(ks[13], (L, CONV_CH), f32),
        "g_conv_ln": gain(ks[14], (L, CONV_CH)),
        "b_conv_ln": 0.02 * jax.random.normal(ks[15], (L, CONV_CH), f32),
        "g_mem": gain(ks[16], (L, D_MODEL)),
        "w_mem_kv": w(ks[17], (L, D_MODEL, 2 * MQ_COLS), D_MODEL),
        "g_mq": gain(ks[18], (L, HEAD_DIM)),
        "g_mk": gain(ks[19], (L, HEAD_DIM)),
        "w_out": w(ks[20], (L, MIX_WIDTH, D_MODEL), MIX_WIDTH),
        "g_ffn2": gain(ks[21], (L, D_MODEL)),
        "w_ffn2_gate": w(ks[22], (L, D_MODEL, D_FF), D_MODEL),
        "w_ffn2_up": w(ks[23], (L, D_MODEL, D_FF), D_MODEL),
        "w_ffn2_down": w(ks[24], (L, D_FF, D_MODEL), D_FF),
    }


def _fwd_reference(x, mem, positions, g_ffn1, w_ffn1_gate, w_ffn1_up, w_ffn1_down,
              g_mix, w_in, g_q, g_k, sinks, w_dw, b_dw, g_conv_ln, b_conv_ln,
              g_mem, w_mem_kv, g_mq, g_mk, w_out,
              g_ffn2, w_ffn2_gate, w_ffn2_up, w_ffn2_down):
    B, T = x.shape[0], x.shape[1]
    M = mem.shape[1]
    for l in range(DEPTH):
        x = x + 0.5 * swiglu(rms_norm(x, g_ffn1[l]), w_ffn1_gate[l], w_ffn1_up[l], w_ffn1_down[l])

        h = rms_norm(x, g_mix[l])
        proj = h @ w_in[l]
        q, k, v, u, mq = jnp.split(proj, SPLITS, axis=-1)
        q = q.reshape(B, T, N_Q_HEADS, HEAD_DIM)
        k = k.reshape(B, T, N_KV_HEADS, HEAD_DIM)
        v = v.reshape(B, T, N_KV_HEADS, HEAD_DIM)
        q = rope_partial(rms_norm(q, g_q[l]), positions)
        k = rope_partial(rms_norm(k, g_k[l]), positions)
        y_attn = sliding_window_attention(q, k, v, sinks[l])

        y_conv = conformer_conv(u, w_dw[l], b_dw[l], g_conv_ln[l], b_conv_ln[l])

        mkv = rms_norm(mem, g_mem[l]) @ w_mem_kv[l]
        mk, mv = jnp.split(mkv, 2, axis=-1)
        mq = rms_norm(mq.reshape(B, T, N_MEM_HEADS, HEAD_DIM), g_mq[l])
        mk = rms_norm(mk.reshape(B, M, N_MEM_HEADS, HEAD_DIM), g_mk[l])
        mv = mv.reshape(B, M, N_MEM_HEADS, HEAD_DIM)
        y_mem = memory_cross_attention(mq, mk, mv)

        x = x + jnp.concatenate([y_attn, y_conv, y_mem], axis=-1) @ w_out[l]

        x = x + 0.5 * swiglu(rms_norm(x, g_ffn2[l]), w_ffn2_gate[l], w_ffn2_up[l], w_ffn2_down[l])
    return x


import jax as _jax
import jax.numpy as _jnp

TWIN_FORMAT = 'train_step'
FWD_PARAMS = ['x', 'mem', 'positions', 'g_ffn1', 'w_ffn1_gate', 'w_ffn1_up', 'w_ffn1_down', 'g_mix', 'w_in', 'g_q', 'g_k', 'sinks', 'w_dw', 'b_dw', 'g_conv_ln', 'b_conv_ln', 'g_mem', 'w_mem_kv', 'g_mq', 'g_mk', 'w_out', 'g_ffn2', 'w_ffn2_gate', 'w_ffn2_up', 'w_ffn2_down']
TWIN_WEIGHTS = ['g_ffn1', 'w_ffn1_gate', 'w_ffn1_up', 'w_ffn1_down', 'g_mix', 'w_in', 'g_q', 'g_k', 'sinks', 'w_dw', 'b_dw', 'g_conv_ln', 'b_conv_ln', 'g_mem', 'w_mem_kv', 'g_mq', 'g_mk', 'w_out', 'g_ffn2', 'w_ffn2_gate', 'w_ffn2_up', 'w_ffn2_down']
TWIN_DIFF_INPUT = 'x'
TWIN_INPUTS = ['x', 'mem', 'positions', 'g_ffn1', 'w_ffn1_gate', 'w_ffn1_up', 'w_ffn1_down', 'g_mix', 'w_in', 'g_q', 'g_k', 'sinks', 'w_dw', 'b_dw', 'g_conv_ln', 'b_conv_ln', 'g_mem', 'w_mem_kv', 'g_mq', 'g_mk', 'w_out', 'g_ffn2', 'w_ffn2_gate', 'w_ffn2_up', 'w_ffn2_down', 'loss_target', 'm_g_ffn1', 'm_w_ffn1_gate', 'm_w_ffn1_up', 'm_w_ffn1_down', 'm_g_mix', 'm_w_in', 'm_g_q', 'm_g_k', 'm_sinks', 'm_w_dw', 'm_b_dw', 'm_g_conv_ln', 'm_b_conv_ln', 'm_g_mem', 'm_w_mem_kv', 'm_g_mq', 'm_g_mk', 'm_w_out', 'm_g_ffn2', 'm_w_ffn2_gate', 'm_w_ffn2_up', 'm_w_ffn2_down', 'v_g_ffn1', 'v_w_ffn1_gate', 'v_w_ffn1_up', 'v_w_ffn1_down', 'v_g_mix', 'v_w_in', 'v_g_q', 'v_g_k', 'v_sinks', 'v_w_dw', 'v_b_dw', 'v_g_conv_ln', 'v_b_conv_ln', 'v_g_mem', 'v_w_mem_kv', 'v_g_mq', 'v_g_mk', 'v_w_out', 'v_g_ffn2', 'v_w_ffn2_gate', 'v_w_ffn2_up', 'v_w_ffn2_down']
TWIN_OUTPUTS = ['loss', 'grad_x', 'grad_g_ffn1', 'grad_w_ffn1_gate', 'grad_w_ffn1_up', 'grad_w_ffn1_down', 'grad_g_mix', 'grad_w_in', 'grad_g_q', 'grad_g_k', 'grad_sinks', 'grad_w_dw', 'grad_b_dw', 'grad_g_conv_ln', 'grad_b_conv_ln', 'grad_g_mem', 'grad_w_mem_kv', 'grad_g_mq', 'grad_g_mk', 'grad_w_out', 'grad_g_ffn2', 'grad_w_ffn2_gate', 'grad_w_ffn2_up', 'grad_w_ffn2_down', 'delta_g_ffn1', 'delta_w_ffn1_gate', 'delta_w_ffn1_up', 'delta_w_ffn1_down', 'delta_g_mix', 'delta_w_in', 'delta_g_q', 'delta_g_k', 'delta_sinks', 'delta_w_dw', 'delta_b_dw', 'delta_g_conv_ln', 'delta_b_conv_ln', 'delta_g_mem', 'delta_w_mem_kv', 'delta_g_mq', 'delta_g_mk', 'delta_w_out', 'delta_g_ffn2', 'delta_w_ffn2_gate', 'delta_w_ffn2_up', 'delta_w_ffn2_down', 'new_m_g_ffn1', 'new_m_w_ffn1_gate', 'new_m_w_ffn1_up', 'new_m_w_ffn1_down', 'new_m_g_mix', 'new_m_w_in', 'new_m_g_q', 'new_m_g_k', 'new_m_sinks', 'new_m_w_dw', 'new_m_b_dw', 'new_m_g_conv_ln', 'new_m_b_conv_ln', 'new_m_g_mem', 'new_m_w_mem_kv', 'new_m_g_mq', 'new_m_g_mk', 'new_m_w_out', 'new_m_g_ffn2', 'new_m_w_ffn2_gate', 'new_m_w_ffn2_up', 'new_m_w_ffn2_down', 'new_v_g_ffn1', 'new_v_w_ffn1_gate', 'new_v_w_ffn1_up', 'new_v_w_ffn1_down', 'new_v_g_mix', 'new_v_w_in', 'new_v_g_q', 'new_v_g_k', 'new_v_sinks', 'new_v_w_dw', 'new_v_b_dw', 'new_v_g_conv_ln', 'new_v_b_conv_ln', 'new_v_g_mem', 'new_v_w_mem_kv', 'new_v_g_mq', 'new_v_g_mk', 'new_v_w_out', 'new_v_g_ffn2', 'new_v_w_ffn2_gate', 'new_v_w_ffn2_up', 'new_v_w_ffn2_down']
TWIN_LEAF_KINDS = {'loss': 'loss', 'grad_x': 'grad_x', 'grad_g_ffn1': 'grad_w', 'grad_w_ffn1_gate': 'grad_w', 'grad_w_ffn1_up': 'grad_w', 'grad_w_ffn1_down': 'grad_w', 'grad_g_mix': 'grad_w', 'grad_w_in': 'grad_w', 'grad_g_q': 'grad_w', 'grad_g_k': 'grad_w', 'grad_sinks': 'grad_w', 'grad_w_dw': 'grad_w', 'grad_b_dw': 'grad_w', 'grad_g_conv_ln': 'grad_w', 'grad_b_conv_ln': 'grad_w', 'grad_g_mem': 'grad_w', 'grad_w_mem_kv': 'grad_w', 'grad_g_mq': 'grad_w', 'grad_g_mk': 'grad_w', 'grad_w_out': 'grad_w', 'grad_g_ffn2': 'grad_w', 'grad_w_ffn2_gate': 'grad_w', 'grad_w_ffn2_up': 'grad_w', 'grad_w_ffn2_down': 'grad_w', 'delta_g_ffn1': 'delta_w', 'delta_w_ffn1_gate': 'delta_w', 'delta_w_ffn1_up': 'delta_w', 'delta_w_ffn1_down': 'delta_w', 'delta_g_mix': 'delta_w', 'delta_w_in': 'delta_w', 'delta_g_q': 'delta_w', 'delta_g_k': 'delta_w', 'delta_sinks': 'delta_w', 'delta_w_dw': 'delta_w', 'delta_b_dw': 'delta_w', 'delta_g_conv_ln': 'delta_w', 'delta_b_conv_ln': 'delta_w', 'delta_g_mem': 'delta_w', 'delta_w_mem_kv': 'delta_w', 'delta_g_mq': 'delta_w', 'delta_g_mk': 'delta_w', 'delta_w_out': 'delta_w', 'delta_g_ffn2': 'delta_w', 'delta_w_ffn2_gate': 'delta_w', 'delta_w_ffn2_up': 'delta_w', 'delta_w_ffn2_down': 'delta_w', 'new_m_g_ffn1': 'new_m', 'new_m_w_ffn1_gate': 'new_m', 'new_m_w_ffn1_up': 'new_m', 'new_m_w_ffn1_down': 'new_m', 'new_m_g_mix': 'new_m', 'new_m_w_in': 'new_m', 'new_m_g_q': 'new_m', 'new_m_g_k': 'new_m', 'new_m_sinks': 'new_m', 'new_m_w_dw': 'new_m', 'new_m_b_dw': 'new_m', 'new_m_g_conv_ln': 'new_m', 'new_m_b_conv_ln': 'new_m', 'new_m_g_mem': 'new_m', 'new_m_w_mem_kv': 'new_m', 'new_m_g_mq': 'new_m', 'new_m_g_mk': 'new_m', 'new_m_w_out': 'new_m', 'new_m_g_ffn2': 'new_m', 'new_m_w_ffn2_gate': 'new_m', 'new_m_w_ffn2_up': 'new_m', 'new_m_w_ffn2_down': 'new_m', 'new_v_g_ffn1': 'new_v', 'new_v_w_ffn1_gate': 'new_v', 'new_v_w_ffn1_up': 'new_v', 'new_v_w_ffn1_down': 'new_v', 'new_v_g_mix': 'new_v', 'new_v_w_in': 'new_v', 'new_v_g_q': 'new_v', 'new_v_g_k': 'new_v', 'new_v_sinks': 'new_v', 'new_v_w_dw': 'new_v', 'new_v_b_dw': 'new_v', 'new_v_g_conv_ln': 'new_v', 'new_v_b_conv_ln': 'new_v', 'new_v_g_mem': 'new_v', 'new_v_w_mem_kv': 'new_v', 'new_v_g_mq': 'new_v', 'new_v_g_mk': 'new_v', 'new_v_w_out': 'new_v', 'new_v_g_ffn2': 'new_v', 'new_v_w_ffn2_gate': 'new_v', 'new_v_w_ffn2_up': 'new_v', 'new_v_w_ffn2_down': 'new_v'}


def _forward(args):
    return _fwd_reference(*[args[k] for k in FWD_PARAMS])


def _output_shape():
    out = _jax.eval_shape(lambda: _forward(_fwd_setup_inputs(0)))
    return out.shape, out.dtype

N_MICROBATCH = 1
ADAM_LR = 0.001
ADAM_B1 = 0.9
ADAM_B2 = 0.999
ADAM_EPS = 1e-08
ADAM_WD = 0.01
ADAM_STEP = 10
PER_EXAMPLE_BATCH_AXIS = {'x': 0, 'mem': 0, 'positions': 0, 'loss_target': 0}
SHARED_INPUTS = []
_WEIGHT_DTYPES = {'g_ffn1': _jnp.float32, 'w_ffn1_gate': _jnp.float32, 'w_ffn1_up': _jnp.float32, 'w_ffn1_down': _jnp.float32, 'g_mix': _jnp.float32, 'w_in': _jnp.float32, 'g_q': _jnp.float32, 'g_k': _jnp.float32, 'sinks': _jnp.float32, 'w_dw': _jnp.float32, 'b_dw': _jnp.float32, 'g_conv_ln': _jnp.float32, 'b_conv_ln': _jnp.float32, 'g_mem': _jnp.float32, 'w_mem_kv': _jnp.float32, 'g_mq': _jnp.float32, 'g_mk': _jnp.float32, 'w_out': _jnp.float32, 'g_ffn2': _jnp.float32, 'w_ffn2_gate': _jnp.float32, 'w_ffn2_up': _jnp.float32, 'w_ffn2_down': _jnp.float32}
MOMENT_SCALE = {'g_ffn1': 3.067102e+00, 'w_ffn1_gate': 4.463309e-02, 'w_ffn1_up': 4.783522e-02, 'w_ffn1_down': 7.756223e-02, 'g_mix': 1.499645e-01, 'w_in': 8.033807e-02, 'g_q': 1.782101e+00, 'g_k': 1.786370e+00, 'sinks': 5.202356e-01, 'w_dw': 2.020362e-01, 'b_dw': 2.447662e+00, 'g_conv_ln': 7.061342e+00, 'b_conv_ln': 4.635852e+00, 'g_mem': 5.515174e-02, 'w_mem_kv': 4.567201e-02, 'g_mq': 6.263054e-01, 'g_mk': 6.352758e-01, 'w_out': 1.865729e-01, 'g_ffn2': 3.149304e+00, 'w_ffn2_gate': 5.050532e-02, 'w_ffn2_up': 4.735334e-02, 'w_ffn2_down': 7.581849e-02}


def _to_microbatches(a, axis):
    t = _jnp.moveaxis(a, axis, 0)
    t = t.reshape((N_MICROBATCH, t.shape[0] // N_MICROBATCH) + t.shape[1:])
    return _jnp.moveaxis(t, 1, axis + 1)


def setup_inputs(seed: int = 0) -> dict:
    inp = _fwd_setup_inputs(seed)
    key = _jax.random.fold_in(_jax.random.key(seed), 7919)
    shape, _ = _output_shape()
    out = dict(inp)
    out["loss_target"] = _jax.random.normal(_jax.random.fold_in(key, 0), shape, _jnp.float32)
    for i, name in enumerate(TWIN_WEIGHTS):
        w = inp[name].astype(_jnp.float32)
        if MOMENT_SCALE is None:
            s = _jnp.sqrt(_jnp.mean(_jnp.square(w)) + 1e-30)
        else:
            s = MOMENT_SCALE[name]
        km, kv = _jax.random.split(_jax.random.fold_in(key, i + 1))
        out[name] = w
        out["m_" + name] = s * _jax.random.normal(km, w.shape, _jnp.float32)
        out["v_" + name] = (s * s) * _jax.random.uniform(kv, w.shape, _jnp.float32, 0.5, 1.5)
    if N_MICROBATCH > 1:
        for name, axis in PER_EXAMPLE_BATCH_AXIS.items():
            out[name] = _to_microbatches(out[name], axis)
    return {'x': out['x'], 'mem': out['mem'], 'positions': out['positions'], 'g_ffn1': out['g_ffn1'], 'w_ffn1_gate': out['w_ffn1_gate'], 'w_ffn1_up': out['w_ffn1_up'], 'w_ffn1_down': out['w_ffn1_down'], 'g_mix': out['g_mix'], 'w_in': out['w_in'], 'g_q': out['g_q'], 'g_k': out['g_k'], 'sinks': out['sinks'], 'w_dw': out['w_dw'], 'b_dw': out['b_dw'], 'g_conv_ln': out['g_conv_ln'], 'b_conv_ln': out['b_conv_ln'], 'g_mem': out['g_mem'], 'w_mem_kv': out['w_mem_kv'], 'g_mq': out['g_mq'], 'g_mk': out['g_mk'], 'w_out': out['w_out'], 'g_ffn2': out['g_ffn2'], 'w_ffn2_gate': out['w_ffn2_gate'], 'w_ffn2_up': out['w_ffn2_up'], 'w_ffn2_down': out['w_ffn2_down'], 'loss_target': out['loss_target'], 'm_g_ffn1': out['m_g_ffn1'], 'm_w_ffn1_gate': out['m_w_ffn1_gate'], 'm_w_ffn1_up': out['m_w_ffn1_up'], 'm_w_ffn1_down': out['m_w_ffn1_down'], 'm_g_mix': out['m_g_mix'], 'm_w_in': out['m_w_in'], 'm_g_q': out['m_g_q'], 'm_g_k': out['m_g_k'], 'm_sinks': out['m_sinks'], 'm_w_dw': out['m_w_dw'], 'm_b_dw': out['m_b_dw'], 'm_g_conv_ln': out['m_g_conv_ln'], 'm_b_conv_ln': out['m_b_conv_ln'], 'm_g_mem': out['m_g_mem'], 'm_w_mem_kv': out['m_w_mem_kv'], 'm_g_mq': out['m_g_mq'], 'm_g_mk': out['m_g_mk'], 'm_w_out': out['m_w_out'], 'm_g_ffn2': out['m_g_ffn2'], 'm_w_ffn2_gate': out['m_w_ffn2_gate'], 'm_w_ffn2_up': out['m_w_ffn2_up'], 'm_w_ffn2_down': out['m_w_ffn2_down'], 'v_g_ffn1': out['v_g_ffn1'], 'v_w_ffn1_gate': out['v_w_ffn1_gate'], 'v_w_ffn1_up': out['v_w_ffn1_up'], 'v_w_ffn1_down': out['v_w_ffn1_down'], 'v_g_mix': out['v_g_mix'], 'v_w_in': out['v_w_in'], 'v_g_q': out['v_g_q'], 'v_g_k': out['v_g_k'], 'v_sinks': out['v_sinks'], 'v_w_dw': out['v_w_dw'], 'v_b_dw': out['v_b_dw'], 'v_g_conv_ln': out['v_g_conv_ln'], 'v_b_conv_ln': out['v_b_conv_ln'], 'v_g_mem': out['v_g_mem'], 'v_w_mem_kv': out['v_w_mem_kv'], 'v_g_mq': out['v_g_mq'], 'v_g_mk': out['v_g_mk'], 'v_w_out': out['v_w_out'], 'v_g_ffn2': out['v_g_ffn2'], 'v_w_ffn2_gate': out['v_w_ffn2_gate'], 'v_w_ffn2_up': out['v_w_ffn2_up'], 'v_w_ffn2_down': out['v_w_ffn2_down']}


def _loss(weights, diff, rest, loss_target):
    with _jax.named_scope("forward"):
        args = {**rest, TWIN_DIFF_INPUT: diff, **{k: w.astype(_WEIGHT_DTYPES[k]) for k, w in weights.items()}}
        y = _forward(args)
    with _jax.named_scope("loss_head"):
        err = _jnp.square(y.astype(_jnp.float32) - loss_target)
        return 0.5 * _jnp.sum(_jnp.mean(err, axis=-1)) if err.ndim else 0.5 * err


def _adamw(w, g, m, v):
    m = ADAM_B1 * m + (1.0 - ADAM_B1) * g
    v = ADAM_B2 * v + (1.0 - ADAM_B2) * _jnp.square(g)
    m_hat = m / (1.0 - ADAM_B1 ** ADAM_STEP)
    v_hat = v / (1.0 - ADAM_B2 ** ADAM_STEP)
    delta = -ADAM_LR * (m_hat / (_jnp.sqrt(v_hat) + ADAM_EPS) + ADAM_WD * w)
    return delta, m, v


def reference(x, mem, positions, g_ffn1, w_ffn1_gate, w_ffn1_up, w_ffn1_down, g_mix, w_in, g_q, g_k, sinks, w_dw, b_dw, g_conv_ln, b_conv_ln, g_mem, w_mem_kv, g_mq, g_mk, w_out, g_ffn2, w_ffn2_gate, w_ffn2_up, w_ffn2_down, loss_target, m_g_ffn1, m_w_ffn1_gate, m_w_ffn1_up, m_w_ffn1_down, m_g_mix, m_w_in, m_g_q, m_g_k, m_sinks, m_w_dw, m_b_dw, m_g_conv_ln, m_b_conv_ln, m_g_mem, m_w_mem_kv, m_g_mq, m_g_mk, m_w_out, m_g_ffn2, m_w_ffn2_gate, m_w_ffn2_up, m_w_ffn2_down, v_g_ffn1, v_w_ffn1_gate, v_w_ffn1_up, v_w_ffn1_down, v_g_mix, v_w_in, v_g_q, v_g_k, v_sinks, v_w_dw, v_b_dw, v_g_conv_ln, v_b_conv_ln, v_g_mem, v_w_mem_kv, v_g_mq, v_g_mk, v_w_out, v_g_ffn2, v_w_ffn2_gate, v_w_ffn2_up, v_w_ffn2_down):
    given = dict(x=x, mem=mem, positions=positions, g_ffn1=g_ffn1, w_ffn1_gate=w_ffn1_gate, w_ffn1_up=w_ffn1_up, w_ffn1_down=w_ffn1_down, g_mix=g_mix, w_in=w_in, g_q=g_q, g_k=g_k, sinks=sinks, w_dw=w_dw, b_dw=b_dw, g_conv_ln=g_conv_ln, b_conv_ln=b_conv_ln, g_mem=g_mem, w_mem_kv=w_mem_kv, g_mq=g_mq, g_mk=g_mk, w_out=w_out, g_ffn2=g_ffn2, w_ffn2_gate=w_ffn2_gate, w_ffn2_up=w_ffn2_up, w_ffn2_down=w_ffn2_down, loss_target=loss_target, m_g_ffn1=m_g_ffn1, m_w_ffn1_gate=m_w_ffn1_gate, m_w_ffn1_up=m_w_ffn1_up, m_w_ffn1_down=m_w_ffn1_down, m_g_mix=m_g_mix, m_w_in=m_w_in, m_g_q=m_g_q, m_g_k=m_g_k, m_sinks=m_sinks, m_w_dw=m_w_dw, m_b_dw=m_b_dw, m_g_conv_ln=m_g_conv_ln, m_b_conv_ln=m_b_conv_ln, m_g_mem=m_g_mem, m_w_mem_kv=m_w_mem_kv, m_g_mq=m_g_mq, m_g_mk=m_g_mk, m_w_out=m_w_out, m_g_ffn2=m_g_ffn2, m_w_ffn2_gate=m_w_ffn2_gate, m_w_ffn2_up=m_w_ffn2_up, m_w_ffn2_down=m_w_ffn2_down, v_g_ffn1=v_g_ffn1, v_w_ffn1_gate=v_w_ffn1_gate, v_w_ffn1_up=v_w_ffn1_up, v_w_ffn1_down=v_w_ffn1_down, v_g_mix=v_g_mix, v_w_in=v_w_in, v_g_q=v_g_q, v_g_k=v_g_k, v_sinks=v_sinks, v_w_dw=v_w_dw, v_b_dw=v_b_dw, v_g_conv_ln=v_g_conv_ln, v_b_conv_ln=v_b_conv_ln, v_g_mem=v_g_mem, v_w_mem_kv=v_w_mem_kv, v_g_mq=v_g_mq, v_g_mk=v_g_mk, v_w_out=v_w_out, v_g_ffn2=v_g_ffn2, v_w_ffn2_gate=v_w_ffn2_gate, v_w_ffn2_up=v_w_ffn2_up, v_w_ffn2_down=v_w_ffn2_down)
    weights = {n: given[n] for n in TWIN_WEIGHTS}
    shared = {n: given[n] for n in SHARED_INPUTS}
    per_example = {n: given[n] for n in ['x', 'mem', 'positions']}
    grad_fn = _jax.value_and_grad(_loss, argnums=(0, 1))

    def one_microbatch(ex, loss_target):
        ex = dict(ex)
        diff = ex.pop(TWIN_DIFF_INPUT)
        return grad_fn(weights, diff, {**shared, **ex}, loss_target)

    if N_MICROBATCH == 1:
        loss, (grad_w, grad_x) = one_microbatch(per_example, given["loss_target"])
    else:
        def body(carry, xs):
            loss_sum, grad_sum = carry
            l_k, (gw_k, gx_k) = one_microbatch(xs[0], xs[1])
            with _jax.named_scope("update"):
                return (loss_sum + l_k, _jax.tree.map(_jnp.add, grad_sum, gw_k)), gx_k

        init = (_jnp.zeros((), _jnp.float32), _jax.tree.map(_jnp.zeros_like, weights))
        (loss, grad_w), grad_x = _jax.lax.scan(body, init, (per_example, given["loss_target"]))
    with _jax.named_scope("update"):
        delta_w, new_m, new_v = {}, {}, {}
        for n in TWIN_WEIGHTS:
            delta_w[n], new_m[n], new_v[n] = _adamw(weights[n], grad_w[n], given["m_" + n], given["v_" + n])
    return (loss, grad_x, *[grad_w[n] for n in TWIN_WEIGHTS], *[delta_w[n] for n in TWIN_WEIGHTS],
            *[new_m[n] for n in TWIN_WEIGHTS], *[new_v[n] for n in TWIN_WEIGHTS])
```

```python
import jax
import jax.numpy as jnp
from jax import lax
from jax.experimental import pallas as pl
from jax.experimental.pallas import tpu as pltpu

D_MODEL = 1024
N_DEV = 8
FF_BLOCK = 352
HEAD_DIM = 64
PAIR = 2 * HEAD_DIM
N_Q_HEADS = 8
Q_PER_KV = 4
ATTN_BLOCK = 128
Q_COLS = 512
KV_COLS = 128
CONV_CH = 256
MQ_COLS = 256
IN_COLS = 1536
CONV_WIDTH = 31
CONV_PAD = 32
CONV_CHUNK = 256
N_MEM_HEADS = 4
ROPE_THETA = 500000.0
ROPE_HALF = 8
EPS = 1e-6
SCALE = HEAD_DIM ** -0.5
NEG = -1e30
ADAM_LR, ADAM_B1, ADAM_B2, ADAM_EPS, ADAM_WD, ADAM_STEP = 0.001, 0.9, 0.999, 1e-08, 0.01, 10
VMEM_LIMIT_BYTES = 56 * 1024 * 1024
BF = jnp.bfloat16
F32 = jnp.float32

BIG = ["w_ffn1_gate", "w_ffn1_up", "w_ffn1_down", "w_in", "w_mem_kv", "w_out", "w_ffn2_gate", "w_ffn2_up", "w_ffn2_down"]
SMALL = ["g_ffn1", "g_mix", "g_mem", "g_ffn2", "g_q", "g_k", "g_mq", "g_mk", "sinks", "b_dw", "g_conv_ln", "b_conv_ln"]
WEIGHTS = ["g_ffn1", "w_ffn1_gate", "w_ffn1_up", "w_ffn1_down", "g_mix", "w_in", "g_q", "g_k", "sinks", "w_dw", "b_dw",
           "g_conv_ln", "b_conv_ln", "g_mem", "w_mem_kv", "g_mq", "g_mk", "w_out", "g_ffn2", "w_ffn2_gate", "w_ffn2_up",
           "w_ffn2_down"]
SMALL_ROWS = 8
PACK_ROWS = 16


def _nn(a, b):
    return jnp.dot(a, b, preferred_element_type=F32)


def _nt(a, b):
    return lax.dot_general(a, b, (((1,), (1,)), ((), ())), preferred_element_type=F32)


def _tn(a, b):
    return lax.dot_general(a, b, (((0,), (0,)), ((), ())), preferred_element_type=F32)


def _params(n_grid):
    return pltpu.CompilerParams(dimension_semantics=("arbitrary",) * n_grid, vmem_limit_bytes=VMEM_LIMIT_BYTES)


def _row_rms(xv):
    return lax.rsqrt(jnp.mean(xv * xv, axis=-1, keepdims=True) + EPS)


def _rms_bwd(dh, xv, r, g):
    u = dh * g
    dx = r * u - xv * (r * r * r) * jnp.mean(u * xv, axis=-1, keepdims=True)
    return dx, jnp.sum(dh * xv * r, axis=0, keepdims=True)


def _sum_all(a):
    return jnp.sum(jnp.sum(a, axis=1, keepdims=True), axis=0, keepdims=True)


def _lane_lo(shape):
    return lax.broadcasted_iota(jnp.int32, shape, 1) < HEAD_DIM


def _pair_sum(v):
    lo = _lane_lo(v.shape)
    s_lo = jnp.sum(jnp.where(lo, v, 0.0), axis=-1, keepdims=True)
    s_hi = jnp.sum(jnp.where(lo, 0.0, v), axis=-1, keepdims=True)
    return jnp.where(lo, s_lo, s_hi)


def _pair_rms(xv):
    return lax.rsqrt(_pair_sum(xv * xv) * (1.0 / HEAD_DIM) + EPS)


def _pair_rms_bwd(dxn, xv, r, g):
    u = dxn * g
    dx = r * u - xv * (r * r * r) * (_pair_sum(u * xv) * (1.0 / HEAD_DIM))
    return dx, jnp.sum(dxn * xv * r, axis=0, keepdims=True)


def _rope_mask(shape):
    lane = lax.broadcasted_iota(jnp.int32, shape, 1)
    return ((lane & (HEAD_DIM - 1)) < 2 * ROPE_HALF).astype(F32)


def _partner(v):
    return pltpu.roll(v, ROPE_HALF, 1) + pltpu.roll(v, PAIR - ROPE_HALF, 1)


def _rope(xn, cos_t, sin_t):
    return xn * cos_t + _partner(xn * _rope_mask(xn.shape)) * sin_t


def _rope_t(d, cos_t, sin_t):
    return d * cos_t + _partner(d * sin_t) * _rope_mask(d.shape)


def _rope_tables(positions):
    inv_freq = ROPE_THETA ** (-jnp.arange(ROPE_HALF, dtype=F32) / ROPE_HALF)
    ang = positions.astype(F32)[:, None] * inv_freq
    cos, sin = jnp.cos(ang), jnp.sin(ang)
    t = positions.shape[0]
    cos_h = jnp.concatenate([cos, cos, jnp.ones((t, HEAD_DIM - 2 * ROPE_HALF), F32)], axis=1)
    sin_h = jnp.concatenate([-sin, sin, jnp.zeros((t, HEAD_DIM - 2 * ROPE_HALF), F32)], axis=1)
    return jnp.tile(cos_h, (1, 2)), jnp.tile(sin_h, (1, 2))


def _ffn_fwd(x, g, wg, wu, wd, target, name):
    t_len = x.shape[0]
    tm = min(512, t_len)
    with_loss = target is not None

    def body(*refs):
        if with_loss:
            x_ref, g_ref, wg_ref, wu_ref, wd_ref, t_ref, h_ref, gg_ref, uu_ref, dy_ref, loss_ref, acc = refs
        else:
            x_ref, g_ref, wg_ref, wu_ref, wd_ref, h_ref, gg_ref, uu_ref, xo_ref, acc = refs
        t, j = pl.program_id(0), pl.program_id(1)

        @pl.when(j == 0)
        def _():
            xv = x_ref[...]
            h_ref[...] = (xv * _row_rms(xv) * g_ref[...]).astype(BF)
            acc[...] = jnp.zeros_like(acc)

        h = h_ref[...]
        gate = _nn(h, wg_ref[0])
        up = _nn(h, wu_ref[0])
        gg_ref[0] = gate.astype(BF)
        uu_ref[0] = up.astype(BF)
        act = (gate * jax.nn.sigmoid(gate) * up).astype(BF)
        acc[...] += _nn(act, wd_ref[0])

        @pl.when(j == N_DEV - 1)
        def _():
            xo = x_ref[...] + 0.5 * acc[...]
            if with_loss:
                err = xo - t_ref[...]
                dy_ref[...] = err * (1.0 / D_MODEL)

                @pl.when(t == 0)
                def _():
                    loss_ref[...] = jnp.zeros_like(loss_ref)

                loss_ref[...] += _sum_all(err * err) * (0.5 / D_MODEL)
            else:
                xo_ref[...] = xo

    row = pl.BlockSpec((tm, D_MODEL), lambda t, j: (t, 0))
    vec = pl.BlockSpec((1, D_MODEL), lambda t, j: (0, 0))
    w_in_spec = pl.BlockSpec((1, D_MODEL, FF_BLOCK), lambda t, j: (j, 0, 0))
    w_out_spec = pl.BlockSpec((1, FF_BLOCK, D_MODEL), lambda t, j: (j, 0, 0))
    blk = pl.BlockSpec((1, tm, FF_BLOCK), lambda t, j: (j, t, 0))
    in_specs = [row, vec, w_in_spec, w_in_spec, w_out_spec] + ([row] if with_loss else [])
    out_shape = [jax.ShapeDtypeStruct((t_len, D_MODEL), BF),
                 jax.ShapeDtypeStruct((N_DEV, t_len, FF_BLOCK), BF),
                 jax.ShapeDtypeStruct((N_DEV, t_len, FF_BLOCK), BF),
                 jax.ShapeDtypeStruct((t_len, D_MODEL), F32)]
    out_specs = [row, blk, blk, row]
    if with_loss:
        out_shape.append(jax.ShapeDtypeStruct((1, 128), F32))
        out_specs.append(pl.BlockSpec((1, 128), lambda t, j: (0, 0)))
    args = (x, g, wg, wu, wd) + ((target,) if with_loss else ())
    return pl.pallas_call(body, out_shape=out_shape, grid=(t_len // tm, N_DEV), in_specs=in_specs, out_specs=out_specs,
                          scratch_shapes=[pltpu.VMEM((tm, D_MODEL), F32)], name=name, compiler_params=_params(2))(*args)


def _ffn_bwd_act(dy, x, g, gate, up, wg, wu, wd, name):
    t_len = x.shape[0]
    tm = min(512, t_len)

    def body(dy_ref, x_ref, g_ref, gg_ref, uu_ref, wg_ref, wu_ref, wd_ref,
             dyb_ref, act_ref, dgg_ref, duu_ref, dx_ref, dg_ref, acc):
        t, j = pl.program_id(0), pl.program_id(1)

        @pl.when(j == 0)
        def _():
            dyb_ref[...] = (0.5 * dy_ref[...]).astype(BF)
            acc[...] = jnp.zeros_like(acc)

        d_act = _nt(dyb_ref[...], wd_ref[0])
        gate = gg_ref[0].astype(F32)
        upv = uu_ref[0].astype(F32)
        sig = jax.nn.sigmoid(gate)
        silu = gate * sig
        d_up = (d_act * silu).astype(BF)
        d_gate = (d_act * upv * (sig * (1.0 + gate * (1.0 - sig)))).astype(BF)
        act_ref[0] = (silu * upv).astype(BF)
        dgg_ref[0] = d_gate
        duu_ref[0] = d_up
        acc[...] += _nt(d_gate, wg_ref[0]) + _nt(d_up, wu_ref[0])

        @pl.when(j == N_DEV - 1)
        def _():
            xv = x_ref[...]
            dx, dg = _rms_bwd(acc[...], xv, _row_rms(xv), g_ref[...])
            dx_ref[...] = dy_ref[...] + dx

            @pl.when(t == 0)
            def _():
                dg_ref[...] = jnp.zeros_like(dg_ref)

            dg_ref[...] += dg

    row = pl.BlockSpec((tm, D_MODEL), lambda t, j: (t, 0))
    vec = pl.BlockSpec((1, D_MODEL), lambda t, j: (0, 0))
    w_in_spec = pl.BlockSpec((1, D_MODEL, FF_BLOCK), lambda t, j: (j, 0, 0))
    w_out_spec = pl.BlockSpec((1, FF_BLOCK, D_MODEL), lambda t, j: (j, 0, 0))
    blk = pl.BlockSpec((1, tm, FF_BLOCK), lambda t, j: (j, t, 0))
    blk_shape = jax.ShapeDtypeStruct((N_DEV, t_len, FF_BLOCK), BF)
    return pl.pallas_call(
        body,
        out_shape=[jax.ShapeDtypeStruct((t_len, D_MODEL), BF), blk_shape, blk_shape, blk_shape,
                   jax.ShapeDtypeStruct((t_len, D_MODEL), F32), jax.ShapeDtypeStruct((1, D_MODEL), F32)],
        grid=(t_len // tm, N_DEV), in_specs=[row, row, vec, blk, blk, w_in_spec, w_in_spec, w_out_spec],
        out_specs=[row, blk, blk, blk, row, vec], scratch_shapes=[pltpu.VMEM((tm, D_MODEL), F32)], name=name,
        compiler_params=_params(2))(dy, x, g, gate, up, wg, wu, wd)


def _ffn_bwd_w(h, dyb, act, d_gate, d_up, name):
    t_len = h.shape[0]
    tm = min(512, t_len)
    nt = t_len // tm

    def body(h_ref, dyb_ref, act_ref, dgg_ref, duu_ref, dwg_ref, dwu_ref, dwd_ref, ag, au, ad):
        t = pl.program_id(1)

        @pl.when(t == 0)
        def _():
            ag[...] = jnp.zeros_like(ag)
            au[...] = jnp.zeros_like(au)
            ad[...] = jnp.zeros_like(ad)

        h = h_ref[...]
        ag[...] += _tn(h, dgg_ref[0])
        au[...] += _tn(h, duu_ref[0])
        ad[...] += _tn(act_ref[0], dyb_ref[...])

        @pl.when(t == nt - 1)
        def _():
            dwg_ref[0] = ag[...].astype(BF)
            dwu_ref[0] = au[...].astype(BF)
            dwd_ref[0] = ad[...].astype(BF)

    row = pl.BlockSpec((tm, D_MODEL), lambda j, t: (t, 0))
    blk = pl.BlockSpec((1, tm, FF_BLOCK), lambda j, t: (j, t, 0))
    w_in_spec = pl.BlockSpec((1, D_MODEL, FF_BLOCK), lambda j, t: (j, 0, 0))
    w_out_spec = pl.BlockSpec((1, FF_BLOCK, D_MODEL), lambda j, t: (j, 0, 0))
    return pl.pallas_call(
        body,
        out_shape=[jax.ShapeDtypeStruct((N_DEV, D_MODEL, FF_BLOCK), BF), jax.ShapeDtypeStruct((N_DEV, D_MODEL, FF_BLOCK), BF),
                   jax.ShapeDtypeStruct((N_DEV, FF_BLOCK, D_MODEL), BF)],
        grid=(N_DEV, nt), in_specs=[row, row, blk, blk, blk], out_specs=[w_in_spec, w_in_spec, w_out_spec],
        scratch_shapes=[pltpu.VMEM((D_MODEL, FF_BLOCK), F32), pltpu.VMEM((D_MODEL, FF_BLOCK), F32),
                        pltpu.VMEM((FF_BLOCK, D_MODEL), F32)],
        name=name, compiler_params=_params(2))(h, dyb, act, d_gate, d_up)


def _in_proj_fwd(x, g, w_in, name):
    t_len = x.shape[0]
    tm = min(512, t_len)

    def body(x_ref, g_ref, w_ref, h_ref, p_ref):
        xv = x_ref[...]
        h = (xv * _row_rms(xv) * g_ref[...]).astype(BF)
        h_ref[...] = h
        p_ref[...] = _nn(h, w_ref[...])

    row = pl.BlockSpec((tm, D_MODEL), lambda t: (t, 0))
    return pl.pallas_call(
        body, out_shape=[jax.ShapeDtypeStruct((t_len, D_MODEL), BF), jax.ShapeDtypeStruct((t_len, IN_COLS), F32)],
        grid=(t_len // tm,),
        in_specs=[row, pl.BlockSpec((1, D_MODEL), lambda t: (0, 0)), pl.BlockSpec((D_MODEL, IN_COLS), lambda t: (0, 0))],
        out_specs=[row, pl.BlockSpec((tm, IN_COLS), lambda t: (t, 0))], name=name, compiler_params=_params(1))(x, g, w_in)


def _in_proj_bwd(dq, dk, dv, du, dmq, w_in, h, x, g, dres, name):
    t_len = x.shape[0]
    tm = min(512, t_len)
    nt = t_len // tm
    groups = [(0, Q_COLS), (Q_COLS, KV_COLS), (Q_COLS + KV_COLS, KV_COLS), (Q_COLS + 2 * KV_COLS, 2 * CONV_CH),
              (Q_COLS + 2 * KV_COLS + 2 * CONV_CH, MQ_COLS)]

    def body(dq_ref, dk_ref, dv_ref, du_ref, dmq_ref, w_ref, h_ref, x_ref, g_ref, dres_ref, dx_ref, dg_ref, dw_ref, acc):
        t = pl.program_id(0)

        @pl.when(t == 0)
        def _():
            acc[...] = jnp.zeros_like(acc)
            dg_ref[...] = jnp.zeros_like(dg_ref)

        h = h_ref[...]
        dh = jnp.zeros((tm, D_MODEL), F32)
        for (start, width), ref in zip(groups, (dq_ref, dk_ref, dv_ref, du_ref, dmq_ref)):
            piece = ref[...]
            dh = dh + _nt(piece, w_ref[:, start:start + width])
            acc[:, start:start + width] += _tn(h, piece)
        xv = x_ref[...]
        dx, dg = _rms_bwd(dh, xv, _row_rms(xv), g_ref[...])
        dx_ref[...] = dres_ref[...] + dx
        dg_ref[...] += dg

        @pl.when(t == nt - 1)
        def _():
            dw_ref[...] = acc[...].astype(BF)

    def cols(width):
        return pl.BlockSpec((tm, width), lambda t: (t, 0))

    row = cols(D_MODEL)
    vec = pl.BlockSpec((1, D_MODEL), lambda t: (0, 0))
    full = pl.BlockSpec((D_MODEL, IN_COLS), lambda t: (0, 0))
    return pl.pallas_call(
        body,
        out_shape=[jax.ShapeDtypeStruct((t_len, D_MODEL), F32), jax.ShapeDtypeStruct((1, D_MODEL), F32),
                   jax.ShapeDtypeStruct((D_MODEL, IN_COLS), BF)],
        grid=(nt,),
        in_specs=[cols(Q_COLS), cols(KV_COLS), cols(KV_COLS), cols(2 * CONV_CH), cols(MQ_COLS), full, row, row, vec, row],
        out_specs=[row, vec, full], scratch_shapes=[pltpu.VMEM((D_MODEL, IN_COLS), F32)], name=name,
        compiler_params=_params(1))(dq, dk, dv, du, dmq, w_in, h, x, g, dres)


def _attn_fwd(proj, cos_t, sin_t, gq2, gk2, sinks, name):
    t_len = proj.shape[0]
    nb = t_len // ATTN_BLOCK

    def body(q_ref, kvc_ref, kvp_ref, cq_ref, sq_ref, cp_ref, sp_ref, gq_ref, gk_ref, sk_ref, ya_ref, lse_ref):
        n = pl.program_id(0)
        cq, sq = cq_ref[...], sq_ref[...]
        c_all = jnp.concatenate([cp_ref[...], cq], axis=0)
        s_all = jnp.concatenate([sp_ref[...], sq], axis=0)
        kv = jnp.concatenate([kvp_ref[...], kvc_ref[...]], axis=0)
        k_raw, v = kv[:, :PAIR], kv[:, PAIR:]
        kn = _rope(k_raw * _pair_rms(k_raw) * gk_ref[...], c_all, s_all)
        row = lax.broadcasted_iota(jnp.int32, (ATTN_BLOCK, 2 * ATTN_BLOCK), 0)
        col = lax.broadcasted_iota(jnp.int32, (ATTN_BLOCK, 2 * ATTN_BLOCK), 1)
        rel = row + ATTN_BLOCK - col
        valid = (rel >= 0) & (rel < ATTN_BLOCK) & ((col >= ATTN_BLOCK) | (n > 0))
        for pr in range(N_Q_HEADS // 2):
            q_raw = q_ref[:, pr * PAIR:(pr + 1) * PAIR]
            qn = _rope(q_raw * _pair_rms(q_raw) * gq_ref[...], cq, sq)
            kvh = (2 * pr) // Q_PER_KV
            kh = kn[:, kvh * HEAD_DIM:(kvh + 1) * HEAD_DIM].astype(BF)
            vh = v[:, kvh * HEAD_DIM:(kvh + 1) * HEAD_DIM].astype(BF)
            for hh in range(2):
                hd = 2 * pr + hh
                qh = qn[:, hh * HEAD_DIM:(hh + 1) * HEAD_DIM].astype(BF)
                s = jnp.where(valid, _nt(qh, kh) * SCALE, NEG)
                sink = sk_ref[0:1, hd:hd + 1]
                m = jnp.maximum(jnp.max(s, axis=-1, keepdims=True), sink)
                p = jnp.exp(s - m)
                den = jnp.sum(p, axis=-1, keepdims=True) + jnp.exp(sink - m)
                ya_ref[:, hd * HEAD_DIM:(hd + 1) * HEAD_DIM] = _nn((p / den).astype(BF), vh).astype(BF)
                lse_ref[:, hd:hd + 1] = m + jnp.log(den)

    prev = lambda n: (jnp.maximum(n - 1, 0), 0)
    tab = pl.BlockSpec((ATTN_BLOCK, PAIR), lambda n: (n, 0))
    tab_p = pl.BlockSpec((ATTN_BLOCK, PAIR), prev)
    one = lambda w: pl.BlockSpec((1, w), lambda n: (0, 0))
    return pl.pallas_call(
        body, out_shape=[jax.ShapeDtypeStruct((t_len, Q_COLS), BF), jax.ShapeDtypeStruct((t_len, N_Q_HEADS), F32)],
        grid=(nb,),
        in_specs=[pl.BlockSpec((ATTN_BLOCK, Q_COLS), lambda n: (n, 0)),
                  pl.BlockSpec((ATTN_BLOCK, 2 * KV_COLS), lambda n: (n, 2)),
                  pl.BlockSpec((ATTN_BLOCK, 2 * KV_COLS), lambda n: (jnp.maximum(n - 1, 0), 2)),
                  tab, tab, tab_p, tab_p, one(PAIR), one(PAIR), one(N_Q_HEADS)],
        out_specs=[pl.BlockSpec((ATTN_BLOCK, Q_COLS), lambda n: (n, 0)),
                   pl.BlockSpec((ATTN_BLOCK, N_Q_HEADS), lambda n: (n, 0))],
        name=name, compiler_params=_params(1))(proj, proj, proj, cos_t, sin_t, cos_t, sin_t, gq2, gk2, sinks)


def _attn_bwd_q(proj, cos_t, sin_t, gq2, gk2, sinks, ya, dmix, lse, name):
    t_len = proj.shape[0]
    nb = t_len // ATTN_BLOCK

    def body(q_ref, kvc_ref, kvp_ref, cq_ref, sq_ref, cp_ref, sp_ref, gq_ref, gk_ref, sk_ref, ya_ref, do_ref, lse_ref,
             dq_ref, dgq_ref, dsk_ref, scr):
        n = pl.program_id(0)

        @pl.when(n == 0)
        def _():
            dgq_ref[...] = jnp.zeros_like(dgq_ref)
            dsk_ref[...] = jnp.zeros_like(dsk_ref)

        cq, sq = cq_ref[...], sq_ref[...]
        c_all = jnp.concatenate([cp_ref[...], cq], axis=0)
        s_all = jnp.concatenate([sp_ref[...], sq], axis=0)
        kv = jnp.concatenate([kvp_ref[...], kvc_ref[...]], axis=0)
        k_raw, v = kv[:, :PAIR], kv[:, PAIR:]
        kn = _rope(k_raw * _pair_rms(k_raw) * gk_ref[...], c_all, s_all)
        row = lax.broadcasted_iota(jnp.int32, (ATTN_BLOCK, 2 * ATTN_BLOCK), 0)
        col = lax.broadcasted_iota(jnp.int32, (ATTN_BLOCK, 2 * ATTN_BLOCK), 1)
        rel = row + ATTN_BLOCK - col
        valid = (rel >= 0) & (rel < ATTN_BLOCK) & ((col >= ATTN_BLOCK) | (n > 0))
        gq = gq_ref[...]
        for pr in range(N_Q_HEADS // 2):
            q_raw = q_ref[:, pr * PAIR:(pr + 1) * PAIR]
            rq = _pair_rms(q_raw)
            qn = _rope(q_raw * rq * gq, cq, sq)
            kvh = (2 * pr) // Q_PER_KV
            kh = kn[:, kvh * HEAD_DIM:(kvh + 1) * HEAD_DIM].astype(BF)
            vh = v[:, kvh * HEAD_DIM:(kvh + 1) * HEAD_DIM].astype(BF)
            for hh in range(2):
                hd = 2 * pr + hh
                qh = qn[:, hh * HEAD_DIM:(hh + 1) * HEAD_DIM].astype(BF)
                lse_h = lse_ref[:, hd:hd + 1]
                p = jnp.exp(jnp.where(valid, _nt(qh, kh) * SCALE, NEG) - lse_h)
                d_o = do_ref[:, hd * HEAD_DIM:(hd + 1) * HEAD_DIM]
                o = ya_ref[:, hd * HEAD_DIM:(hd + 1) * HEAD_DIM].astype(F32)
                dsum = jnp.sum(d_o * o, axis=-1, keepdims=True)
                ds = p * (_nt(d_o.astype(BF), vh) - dsum)
                scr[:, hh * HEAD_DIM:(hh + 1) * HEAD_DIM] = _nn(ds.astype(BF), kh) * SCALE
                sink = sk_ref[0:1, hd:hd + 1]
                dsk_ref[0:1, hd:hd + 1] += -jnp.sum(jnp.exp(sink - lse_h) * dsum, axis=0, keepdims=True)
            dx, dg = _pair_rms_bwd(_rope_t(scr[...], cq, sq), q_raw, rq, gq)
            dq_ref[:, pr * PAIR:(pr + 1) * PAIR] = dx.astype(BF)
            dgq_ref[...] += dg

    prev = lambda n: (jnp.maximum(n - 1, 0), 0)
    tab = pl.BlockSpec((ATTN_BLOCK, PAIR), lambda n: (n, 0))
    tab_p = pl.BlockSpec((ATTN_BLOCK, PAIR), prev)
    one = lambda w: pl.BlockSpec((1, w), lambda n: (0, 0))
    qblk = pl.BlockSpec((ATTN_BLOCK, Q_COLS), lambda n: (n, 0))
    return pl.pallas_call(
        body,
        out_shape=[jax.ShapeDtypeStruct((t_len, Q_COLS), BF), jax.ShapeDtypeStruct((1, PAIR), F32),
                   jax.ShapeDtypeStruct((1, N_Q_HEADS), F32)],
        grid=(nb,),
        in_specs=[qblk, pl.BlockSpec((ATTN_BLOCK, 2 * KV_COLS), lambda n: (n, 2)),
                  pl.BlockSpec((ATTN_BLOCK, 2 * KV_COLS), lambda n: (jnp.maximum(n - 1, 0), 2)),
                  tab, tab, tab_p, tab_p, one(PAIR), one(PAIR), one(N_Q_HEADS), qblk, qblk,
                  pl.BlockSpec((ATTN_BLOCK, N_Q_HEADS), lambda n: (n, 0))],
        out_specs=[qblk, one(PAIR), one(N_Q_HEADS)], scratch_shapes=[pltpu.VMEM((ATTN_BLOCK, PAIR), F32)],
        name=name, compiler_params=_params(1))(proj, proj, proj, cos_t, sin_t, cos_t, sin_t, gq2, gk2, sinks, ya, dmix, lse)


def _attn_bwd_kv(proj, cos_t, sin_t, gq2, gk2, ya, dmix, lse, name):
    t_len = proj.shape[0]
    nb = t_len // ATTN_BLOCK

    def body(kv_ref, q0_ref, q1_ref, ck_ref, sk_ref, c1_ref, s1_ref, gq_ref, gk_ref, o0_ref, o1_ref, do0_ref, do1_ref,
             l0_ref, l1_ref, dk_ref, dv_ref, dgk_ref, dkn_scr, dv_scr):
        m = pl.program_id(0)

        @pl.when(m == 0)
        def _():
            dgk_ref[...] = jnp.zeros_like(dgk_ref)

        ck, sk = ck_ref[...], sk_ref[...]
        c_all = jnp.concatenate([ck, c1_ref[...]], axis=0)
        s_all = jnp.concatenate([sk, s1_ref[...]], axis=0)
        kvv = kv_ref[...]
        k_raw, v = kvv[:, :PAIR], kvv[:, PAIR:]
        rk = _pair_rms(k_raw)
        gk = gk_ref[...]
        kn = _rope(k_raw * rk * gk, ck, sk)
        row = lax.broadcasted_iota(jnp.int32, (2 * ATTN_BLOCK, ATTN_BLOCK), 0)
        col = lax.broadcasted_iota(jnp.int32, (2 * ATTN_BLOCK, ATTN_BLOCK), 1)
        valid = ((row < ATTN_BLOCK) & (row >= col)) | ((row >= ATTN_BLOCK) & (row - ATTN_BLOCK < col) & (m < nb - 1))
        lse2 = jnp.concatenate([l0_ref[...], l1_ref[...]], axis=0)
        dkn_scr[...] = jnp.zeros_like(dkn_scr)
        dv_scr[...] = jnp.zeros_like(dv_scr)
        for pr in range(N_Q_HEADS // 2):
            lanes = slice(pr * PAIR, (pr + 1) * PAIR)
            q_raw = jnp.concatenate([q0_ref[:, lanes], q1_ref[:, lanes]], axis=0)
            qn = _rope(q_raw * _pair_rms(q_raw) * gq_ref[...], c_all, s_all)
            d_o2 = jnp.concatenate([do0_ref[:, lanes], do1_ref[:, lanes]], axis=0)
            o2 = jnp.concatenate([o0_ref[:, lanes], o1_ref[:, lanes]], axis=0).astype(F32)
            dsum2 = _pair_sum(d_o2 * o2)
            kvh = (2 * pr) // Q_PER_KV
            khs = slice(kvh * HEAD_DIM, (kvh + 1) * HEAD_DIM)
            kh = kn[:, khs].astype(BF)
            vh = v[:, khs].astype(BF)
            for hh in range(2):
                hd = 2 * pr + hh
                hs = slice(hh * HEAD_DIM, (hh + 1) * HEAD_DIM)
                qh = qn[:, hs].astype(BF)
                d_oh = d_o2[:, hs].astype(BF)
                p = jnp.exp(jnp.where(valid, _nt(qh, kh) * SCALE, NEG) - lse2[:, hd:hd + 1])
                dv_scr[:, khs] += _tn(p.astype(BF), d_oh)
                ds = p * (_nt(d_oh, vh) - dsum2[:, hh * HEAD_DIM:hh * HEAD_DIM + 1])
                dkn_scr[:, khs] += _tn(ds.astype(BF), qh) * SCALE
        dx, dg = _pair_rms_bwd(_rope_t(dkn_scr[...], ck, sk), k_raw, rk, gk)
        dk_ref[...] = dx.astype(BF)
        dv_ref[...] = dv_scr[...].astype(BF)
        dgk_ref[...] += dg

    nxt = lambda m: (jnp.minimum(m + 1, nb - 1), 0)
    cur = lambda m: (m, 0)
    tab = lambda f: pl.BlockSpec((ATTN_BLOCK, PAIR), f)
    qb = lambda f: pl.BlockSpec((ATTN_BLOCK, Q_COLS), f)
    lb = lambda f: pl.BlockSpec((ATTN_BLOCK, N_Q_HEADS), f)
    one = pl.BlockSpec((1, PAIR), lambda m: (0, 0))
    return pl.pallas_call(
        body,
        out_shape=[jax.ShapeDtypeStruct((t_len, KV_COLS), BF), jax.ShapeDtypeStruct((t_len, KV_COLS), BF),
                   jax.ShapeDtypeStruct((1, PAIR), F32)],
        grid=(nb,),
        in_specs=[pl.BlockSpec((ATTN_BLOCK, 2 * KV_COLS), lambda m: (m, 2)), qb(cur), qb(nxt), tab(cur), tab(cur), tab(nxt),
                  tab(nxt), one, one, qb(cur), qb(nxt), qb(cur), qb(nxt), lb(cur), lb(nxt)],
        out_specs=[tab(cur), tab(cur), one],
        scratch_shapes=[pltpu.VMEM((ATTN_BLOCK, PAIR), F32), pltpu.VMEM((ATTN_BLOCK, PAIR), F32)],
        name=name, compiler_params=_params(1))(proj, proj, proj, cos_t, sin_t, cos_t, sin_t, gq2, gk2, ya, ya, dmix, dmix,
                                               lse, lse)


def _conv_taps(blk, w, offset_of):
    acc = jnp.zeros((CONV_CHUNK, CONV_CH), F32)
    for k in range(CONV_WIDTH):
        o = offset_of(k)
        acc = acc + w[k:k + 1, :] * blk[o:o + CONV_CHUNK, :]
    return acc


def _conv_fwd(proj, w_dw, b_dw, g_ln, b_ln, name):
    t_len = proj.shape[0]
    nc = t_len // CONV_CHUNK

    def body(a_ref, gt_ref, w_ref, b_ref, g_ref, bl_ref, yc_ref, cv_ref, pad):
        pad[0:CONV_PAD, :] = jnp.zeros((CONV_PAD, CONV_CH), F32)

        def glu(c, carry):
            rows = pl.ds(pl.multiple_of(c * CONV_CHUNK, CONV_CHUNK), CONV_CHUNK)
            dst = pl.ds(pl.multiple_of(c * CONV_CHUNK + CONV_PAD, CONV_PAD), CONV_CHUNK)
            pad[dst, :] = a_ref[rows, :] * jax.nn.sigmoid(gt_ref[rows, :])
            return carry

        lax.fori_loop(0, nc, glu, 0)
        w = w_ref[...]

        def conv(c, carry):
            base = pl.multiple_of(c * CONV_CHUNK, CONV_CHUNK)
            blk = pad[pl.ds(base, CONV_CHUNK + CONV_PAD), :]
            cv = _conv_taps(blk, w, lambda k: k + CONV_PAD - (CONV_WIDTH - 1)) + b_ref[...]
            mu = jnp.mean(cv, axis=-1, keepdims=True)
            xc = cv - mu
            z = xc * lax.rsqrt(jnp.mean(xc * xc, axis=-1, keepdims=True) + EPS) * g_ref[...] + bl_ref[...]
            rows = pl.ds(base, CONV_CHUNK)
            cv_ref[rows, :] = cv
            yc_ref[rows, :] = (z * jax.nn.sigmoid(z)).astype(BF)
            return carry

        lax.fori_loop(0, nc, conv, 0)

    vec = pl.BlockSpec((1, CONV_CH), lambda i: (0, 0))
    full = pl.BlockSpec((t_len, CONV_CH), lambda i: (0, 0))
    return pl.pallas_call(
        body, out_shape=[jax.ShapeDtypeStruct((t_len, CONV_CH), BF), jax.ShapeDtypeStruct((t_len, CONV_CH), F32)],
        grid=(1,),
        in_specs=[pl.BlockSpec((t_len, CONV_CH), lambda i: (0, 3)), pl.BlockSpec((t_len, CONV_CH), lambda i: (0, 4)),
                  pl.BlockSpec((CONV_WIDTH, CONV_CH), lambda i: (0, 0)), vec, vec, vec],
        out_specs=[full, full], scratch_shapes=[pltpu.VMEM((t_len + CONV_PAD, CONV_CH), F32)], name=name,
        compiler_params=_params(1))(proj, proj, w_dw, b_dw, g_ln, b_ln)


def _conv_bwd(proj, cv, dmix, w_dw, g_ln, b_ln, name):
    t_len = proj.shape[0]
    nc = t_len // CONV_CHUNK

    def body(a_ref, gt_ref, cv_ref, dy_ref, w_ref, g_ref, bl_ref, du_ref, dw_ref, db_ref, dgl_ref, dbl_ref, pad, dpad):
        pad[0:CONV_PAD, :] = jnp.zeros((CONV_PAD, CONV_CH), F32)
        dpad[t_len:t_len + CONV_PAD, :] = jnp.zeros((CONV_PAD, CONV_CH), F32)
        dw_ref[...] = jnp.zeros_like(dw_ref)
        db_ref[...] = jnp.zeros_like(db_ref)
        dgl_ref[...] = jnp.zeros_like(dgl_ref)
        dbl_ref[...] = jnp.zeros_like(dbl_ref)

        def norm_bwd(c, carry):
            base = pl.multiple_of(c * CONV_CHUNK, CONV_CHUNK)
            rows = pl.ds(base, CONV_CHUNK)
            dst = pl.ds(pl.multiple_of(c * CONV_CHUNK + CONV_PAD, CONV_PAD), CONV_CHUNK)
            pad[dst, :] = a_ref[rows, :] * jax.nn.sigmoid(gt_ref[rows, :])
            cvv = cv_ref[rows, :]
            xc = cvv - jnp.mean(cvv, axis=-1, keepdims=True)
            rs = lax.rsqrt(jnp.mean(xc * xc, axis=-1, keepdims=True) + EPS)
            xhat = xc * rs
            z = xhat * g_ref[...] + bl_ref[...]
            sg = jax.nn.sigmoid(z)
            dz = dy_ref[rows, :] * (sg * (1.0 + z * (1.0 - sg)))
            dgl_ref[...] += jnp.sum(dz * xhat, axis=0, keepdims=True)
            dbl_ref[...] += jnp.sum(dz, axis=0, keepdims=True)
            dxh = dz * g_ref[...]
            dcv = rs * (dxh - jnp.mean(dxh, axis=-1, keepdims=True) - xhat * jnp.mean(dxh * xhat, axis=-1, keepdims=True))
            db_ref[...] += jnp.sum(dcv, axis=0, keepdims=True)
            dpad[rows, :] = dcv
            return carry

        lax.fori_loop(0, nc, norm_bwd, 0)
        w = w_ref[...]

        def conv_bwd(c, carry):
            base = pl.multiple_of(c * CONV_CHUNK, CONV_CHUNK)
            rows = pl.ds(base, CONV_CHUNK)
            dblk = dpad[pl.ds(base, CONV_CHUNK + CONV_PAD), :]
            dhc = _conv_taps(dblk, w, lambda k: CONV_WIDTH - 1 - k)
            a = a_ref[rows, :]
            sg = jax.nn.sigmoid(gt_ref[rows, :])
            du_ref[rows, 0:CONV_CH] = (dhc * sg).astype(BF)
            du_ref[rows, CONV_CH:2 * CONV_CH] = (dhc * a * sg * (1.0 - sg)).astype(BF)
            hblk = pad[pl.ds(base, CONV_CHUNK + CONV_PAD), :]
            dcv = dblk[0:CONV_CHUNK, :]
            for k in range(CONV_WIDTH):
                o = k + CONV_PAD - (CONV_WIDTH - 1)
                dw_ref[k:k + 1, :] += jnp.sum(dcv * hblk[o:o + CONV_CHUNK, :], axis=0, keepdims=True)
            return carry

        lax.fori_loop(0, nc, conv_bwd, 0)

    vec = pl.BlockSpec((1, CONV_CH), lambda i: (0, 0))
    full = pl.BlockSpec((t_len, CONV_CH), lambda i: (0, 0))
    wspec = pl.BlockSpec((CONV_WIDTH, CONV_CH), lambda i: (0, 0))
    vshape = jax.ShapeDtypeStruct((1, CONV_CH), F32)
    return pl.pallas_call(
        body,
        out_shape=[jax.ShapeDtypeStruct((t_len, 2 * CONV_CH), BF), jax.ShapeDtypeStruct((CONV_WIDTH, CONV_CH), F32),
                   vshape, vshape, vshape],
        grid=(1,),
        in_specs=[pl.BlockSpec((t_len, CONV_CH), lambda i: (0, 3)), pl.BlockSpec((t_len, CONV_CH), lambda i: (0, 4)), full,
                  pl.BlockSpec((t_len, CONV_CH), lambda i: (0, 2)), wspec, vec, vec],
        out_specs=[pl.BlockSpec((t_len, 2 * CONV_CH), lambda i: (0, 0)), wspec, vec, vec, vec],
        scratch_shapes=[pltpu.VMEM((t_len + CONV_PAD, CONV_CH), F32), pltpu.VMEM((t_len + CONV_PAD, CONV_CH), F32)],
        name=name, compiler_params=_params(1))(proj, proj, cv, dmix, w_dw, g_ln, b_ln)


def _mem_kv_fwd(mem, g, w, name):
    def body(m_ref, g_ref, w_ref, h_ref, kv_ref):
        mv = m_ref[...]
        h = (mv * _row_rms(mv) * g_ref[...]).astype(BF)
        h_ref[...] = h
        kv_ref[...] = _nn(h, w_ref[...])

    m_len = mem.shape[0]
    return pl.pallas_call(
        body, out_shape=[jax.ShapeDtypeStruct((m_len, D_MODEL), BF), jax.ShapeDtypeStruct((m_len, 2 * MQ_COLS), F32)],
        name=name, compiler_params=pltpu.CompilerParams(vmem_limit_bytes=VMEM_LIMIT_BYTES))(mem, g, w)


def _mem_kv_bwd(mem, g, h, w, dkv, name):
    def body(m_ref, g_ref, h_ref, w_ref, dkv_ref, dw_ref, dg_ref):
        dkv_b = dkv_ref[...].astype(BF)
        dw_ref[...] = _tn(h_ref[...], dkv_b).astype(BF)
        mv = m_ref[...]
        dg_ref[...] = jnp.sum(_nt(dkv_b, w_ref[...]) * mv * _row_rms(mv), axis=0, keepdims=True)

    return pl.pallas_call(
        body, out_shape=[jax.ShapeDtypeStruct((D_MODEL, 2 * MQ_COLS), BF), jax.ShapeDtypeStruct((1, D_MODEL), F32)],
        name=name, compiler_params=pltpu.CompilerParams(vmem_limit_bytes=VMEM_LIMIT_BYTES))(mem, g, h, w, dkv)


def _mem_attn_fwd(proj, mkv, gq2, gk2, name):
    t_len = proj.shape[0]
    tm = min(256, t_len)

    def body(q_ref, kv_ref, gq_ref, gk_ref, ym_ref, lse_ref):
        kvv = kv_ref[...]
        for pr in range(N_MEM_HEADS // 2):
            lanes = slice(pr * PAIR, (pr + 1) * PAIR)
            k_raw = kvv[:, lanes]
            kn = k_raw * _pair_rms(k_raw) * gk_ref[...]
            v = kvv[:, MQ_COLS + pr * PAIR:MQ_COLS + (pr + 1) * PAIR]
            q_raw = q_ref[:, lanes]
            qn = q_raw * _pair_rms(q_raw) * gq_ref[...]
            for hh in range(2):
                hd = 2 * pr + hh
                hs = slice(hh * HEAD_DIM, (hh + 1) * HEAD_DIM)
                s = _nt(qn[:, hs].astype(BF), kn[:, hs].astype(BF)) * SCALE
                m = jnp.max(s, axis=-1, keepdims=True)
                p = jnp.exp(s - m)
                den = jnp.sum(p, axis=-1, keepdims=True)
                ym_ref[:, hd * HEAD_DIM:(hd + 1) * HEAD_DIM] = _nn((p / den).astype(BF), v[:, hs].astype(BF)).astype(BF)
                lse_ref[:, hd:hd + 1] = m + jnp.log(den)

    m_len = mkv.shape[0]
    one = pl.BlockSpec((1, PAIR), lambda t: (0, 0))
    return pl.pallas_call(
        body, out_shape=[jax.ShapeDtypeStruct((t_len, MQ_COLS), BF), jax.ShapeDtypeStruct((t_len, N_MEM_HEADS), F32)],
        grid=(t_len // tm,),
        in_specs=[pl.BlockSpec((tm, MQ_COLS), lambda t: (t, 5)), pl.BlockSpec((m_len, 2 * MQ_COLS), lambda t: (0, 0)), one, one],
        out_specs=[pl.BlockSpec((tm, MQ_COLS), lambda t: (t, 0)), pl.BlockSpec((tm, N_MEM_HEADS), lambda t: (t, 0))],
        name=name, compiler_params=_params(1))(proj, mkv, gq2, gk2)


def _mem_attn_bwd(proj, mkv, gq2, gk2, ym, dmix, lse, name):
    t_len = proj.shape[0]
    tm = min(256, t_len)
    nt = t_len // tm
    m_len = mkv.shape[0]

    def body(q_ref, kv_ref, gq_ref, gk_ref, ym_ref, do_ref, lse_ref, dq_ref, dkv_ref, dgq_ref, dgk_ref, dkn_scr, dv_scr, scr):
        t = pl.program_id(0)

        @pl.when(t == 0)
        def _():
            dkn_scr[...] = jnp.zeros_like(dkn_scr)
            dv_scr[...] = jnp.zeros_like(dv_scr)
            dgq_ref[...] = jnp.zeros_like(dgq_ref)

        kvv = kv_ref[...]
        for pr in range(N_MEM_HEADS // 2):
            lanes = slice(pr * PAIR, (pr + 1) * PAIR)
            k_raw = kvv[:, lanes]
            kn = k_raw * _pair_rms(k_raw) * gk_ref[...]
            v = kvv[:, MQ_COLS + pr * PAIR:MQ_COLS + (pr + 1) * PAIR]
            q_raw = q_ref[:, lanes]
            rq = _pair_rms(q_raw)
            qn = q_raw * rq * gq_ref[...]
            d_o = do_ref[:, lanes]
            dsum = _pair_sum(d_o * ym_ref[:, lanes].astype(F32))
            for hh in range(2):
                hd = 2 * pr + hh
                hs = slice(hh * HEAD_DIM, (hh + 1) * HEAD_DIM)
                cols = slice(hd * HEAD_DIM, (hd + 1) * HEAD_DIM)
                qh = qn[:, hs].astype(BF)
                kh = kn[:, hs].astype(BF)
                d_oh = d_o[:, hs].astype(BF)
                p = jnp.exp(_nt(qh, kh) * SCALE - lse_ref[:, hd:hd + 1])
                dv_scr[:, cols] += _tn(p.astype(BF), d_oh)
                ds = (p * (_nt(d_oh, v[:, hs].astype(BF)) - dsum[:, hh * HEAD_DIM:hh * HEAD_DIM + 1])).astype(BF)
                scr[:, hs] = _nn(ds, kh) * SCALE
                dkn_scr[:, cols] += _tn(ds, qh) * SCALE
            dx, dg = _pair_rms_bwd(scr[...], q_raw, rq, gq_ref[...])
            dq_ref[:, lanes] = dx.astype(BF)
            dgq_ref[...] += dg

        @pl.when(t == nt - 1)
        def _():
            dgk = jnp.zeros((1, PAIR), F32)
            for pr in range(N_MEM_HEADS // 2):
                lanes = slice(pr * PAIR, (pr + 1) * PAIR)
                k_raw = kvv[:, lanes]
                dx, dg = _pair_rms_bwd(dkn_scr[:, lanes], k_raw, _pair_rms(k_raw), gk_ref[...])
                dkv_ref[:, lanes] = dx
                dgk = dgk + dg
            dkv_ref[:, MQ_COLS:2 * MQ_COLS] = dv_scr[...]
            dgk_ref[...] = dgk

    one = pl.BlockSpec((1, PAIR), lambda t: (0, 0))
    kvspec = pl.BlockSpec((m_len, 2 * MQ_COLS), lambda t: (0, 0))
    qspec = pl.BlockSpec((tm, MQ_COLS), lambda t: (t, 0))
    return pl.pallas_call(
        body,
        out_shape=[jax.ShapeDtypeStruct((t_len, MQ_COLS), BF), jax.ShapeDtypeStruct((m_len, 2 * MQ_COLS), F32),
                   jax.ShapeDtypeStruct((1, PAIR), F32), jax.ShapeDtypeStruct((1, PAIR), F32)],
        grid=(nt,),
        in_specs=[pl.BlockSpec((tm, MQ_COLS), lambda t: (t, 5)), kvspec, one, one, qspec,
                  pl.BlockSpec((tm, MQ_COLS), lambda t: (t, 3)), pl.BlockSpec((tm, N_MEM_HEADS), lambda t: (t, 0))],
        out_specs=[qspec, kvspec, one, one],
        scratch_shapes=[pltpu.VMEM((m_len, MQ_COLS), F32), pltpu.VMEM((m_len, MQ_COLS), F32), pltpu.VMEM((tm, PAIR), F32)],
        name=name, compiler_params=_params(1))(proj, mkv, gq2, gk2, ym, dmix, lse)


MIX_GROUPS = [(0, Q_COLS), (Q_COLS, CONV_CH), (Q_COLS + CONV_CH, MQ_COLS)]


def _out_proj_fwd(x, ya, yc, ym, w_out, name):
    t_len = x.shape[0]
    tm = min(512, t_len)

    def body(x_ref, ya_ref, yc_ref, ym_ref, w_ref, xo_ref):
        y = x_ref[...]
        for (start, width), ref in zip(MIX_GROUPS, (ya_ref, yc_ref, ym_ref)):
            y = y + _nn(ref[...], w_ref[start:start + width, :])
        xo_ref[...] = y

    cols = lambda w: pl.BlockSpec((tm, w), lambda t: (t, 0))
    return pl.pallas_call(
        body, out_shape=jax.ShapeDtypeStruct((t_len, D_MODEL), F32), grid=(t_len // tm,),
        in_specs=[cols(D_MODEL), cols(Q_COLS), cols(CONV_CH), cols(MQ_COLS),
                  pl.BlockSpec((D_MODEL, D_MODEL), lambda t: (0, 0))],
        out_specs=cols(D_MODEL), name=name, compiler_params=_params(1))(x, ya, yc, ym, w_out)


def _out_proj_bwd(dx, ya, yc, ym, w_out, name):
    t_len = dx.shape[0]
    tm = min(512, t_len)
    nt = t_len // tm

    def body(dx_ref, ya_ref, yc_ref, ym_ref, w_ref, dmix_ref, dw_ref, acc):
        t = pl.program_id(0)

        @pl.when(t == 0)
        def _():
            acc[...] = jnp.zeros_like(acc)

        dxb = dx_ref[...].astype(BF)
        dmix_ref[...] = _nt(dxb, w_ref[...])
        for (start, width), ref in zip(MIX_GROUPS, (ya_ref, yc_ref, ym_ref)):
            acc[start:start + width, :] += _tn(ref[...], dxb)

        @pl.when(t == nt - 1)
        def _():
            dw_ref[...] = acc[...].astype(BF)

    cols = lambda w: pl.BlockSpec((tm, w), lambda t: (t, 0))
    full = pl.BlockSpec((D_MODEL, D_MODEL), lambda t: (0, 0))
    return pl.pallas_call(
        body, out_shape=[jax.ShapeDtypeStruct((t_len, D_MODEL), F32), jax.ShapeDtypeStruct((D_MODEL, D_MODEL), BF)],
        grid=(nt,), in_specs=[cols(D_MODEL), cols(Q_COLS), cols(CONV_CH), cols(MQ_COLS), full],
        out_specs=[cols(D_MODEL), full], scratch_shapes=[pltpu.VMEM((D_MODEL, D_MODEL), F32)], name=name,
        compiler_params=_params(1))(dx, ya, yc, ym, w_out)


def _exchange(gathers, scatters, name):
    ng, ns = len(gathers), len(scatters)
    n = ng + ns

    def body(*refs):
        ins, outs = refs[:n], refs[n:2 * n]
        send_sems, recv_sems, local_sems = refs[2 * n:]
        x, y, c = lax.axis_index("x"), lax.axis_index("y"), lax.axis_index("c")
        me = 4 * x + 2 * y + c

        def peer(k):
            px = 1 - x if k & 4 else x
            py = 1 - y if k & 2 else y
            pc = 1 - c if k & 1 else c
            return (px, py, pc), 4 * px + 2 * py + pc

        def src(w, dest):
            return ins[w] if w < ng else ins[w].at[dest]

        local = [pltpu.make_async_copy(src(w, me), outs[w].at[me], local_sems.at[w]) for w in range(n)]
        for cp in local:
            cp.start()
        sends = []
        for w in range(n):
            for k in range(1, N_DEV):
                dev, idx = peer(k)
                cp = pltpu.make_async_remote_copy(
                    src_ref=src(w, idx), dst_ref=outs[w].at[me], send_sem=send_sems.at[w * (N_DEV - 1) + k - 1],
                    recv_sem=recv_sems.at[w * (N_DEV - 1) + k - 1], device_id=dev, device_id_type=pl.DeviceIdType.MESH)
                cp.start()
                sends.append(cp)
        for w in range(n):
            for k in range(1, N_DEV):
                dev, idx = peer(k)
                pltpu.make_async_remote_copy(
                    src_ref=src(w, idx), dst_ref=outs[w].at[idx], send_sem=send_sems.at[w * (N_DEV - 1) + k - 1],
                    recv_sem=recv_sems.at[w * (N_DEV - 1) + k - 1], device_id=dev,
                    device_id_type=pl.DeviceIdType.MESH).wait_recv()
        for cp in sends:
            cp.wait_send()
        for cp in local:
            cp.wait()

    out_shape = [jax.ShapeDtypeStruct((N_DEV,) + a.shape, a.dtype) for a in gathers]
    out_shape += [jax.ShapeDtypeStruct(a.shape, a.dtype) for a in scatters]
    hbm = pl.BlockSpec(memory_space=pl.ANY)
    return pl.pallas_call(
        body, out_shape=out_shape, in_specs=[hbm] * n, out_specs=[hbm] * n,
        scratch_shapes=[pltpu.SemaphoreType.DMA((n * (N_DEV - 1),)), pltpu.SemaphoreType.DMA((n * (N_DEV - 1),)),
                        pltpu.SemaphoreType.DMA((n,))],
        name=name)(*gathers, *scatters)


def _adamw_math(w, g, m, v):
    m2 = ADAM_B1 * m + (1.0 - ADAM_B1) * g
    v2 = ADAM_B2 * v + (1.0 - ADAM_B2) * (g * g)
    m_hat = m2 / (1.0 - ADAM_B1 ** ADAM_STEP)
    v_hat = v2 / (1.0 - ADAM_B2 ** ADAM_STEP)
    return -ADAM_LR * (m_hat / (jnp.sqrt(v_hat) + ADAM_EPS) + ADAM_WD * w), m2, v2


def _sum_adamw(parts, w, m, v, name):
    rows, cols = w.shape
    tr = rows if rows <= 512 else 256

    def body(p_ref, w_ref, m_ref, v_ref, g_ref, d_ref, m2_ref, v2_ref):
        g = p_ref[0].astype(F32)
        for s in range(1, N_DEV):
            g = g + p_ref[s].astype(F32)
        g_ref[...] = g
        d_ref[...], m2_ref[...], v2_ref[...] = _adamw_math(w_ref[...], g, m_ref[...], v_ref[...])

    blk = pl.BlockSpec((tr, cols), lambda i: (i, 0))
    shape = jax.ShapeDtypeStruct((rows, cols), F32)
    return pl.pallas_call(
        body, out_shape=[shape] * 4, grid=(rows // tr,),
        in_specs=[pl.BlockSpec((N_DEV, tr, cols), lambda i: (0, i, 0)), blk, blk, blk], out_specs=[blk] * 4, name=name,
        compiler_params=_params(1))(parts, w, m, v)


def _slot_sum(parts, name):
    def body(p_ref, g_ref):
        g = p_ref[0]
        for s in range(1, N_DEV):
            g = g + p_ref[s]
        g_ref[...] = g

    return pl.pallas_call(body, out_shape=jax.ShapeDtypeStruct(parts.shape[1:], F32), name=name)(parts)


def _adamw(g, w, m, v, name):
    def body(g_ref, w_ref, m_ref, v_ref, d_ref, m2_ref, v2_ref):
        d_ref[...], m2_ref[...], v2_ref[...] = _adamw_math(w_ref[...], g_ref[...], m_ref[...], v_ref[...])

    shape = jax.ShapeDtypeStruct(w.shape, F32)
    return pl.pallas_call(body, out_shape=[shape] * 3, name=name)(g, w, m, v)


def _pack_small(vals):
    flat = jnp.concatenate([vals[k].reshape(-1) for k in SMALL])
    return jnp.pad(flat, (0, SMALL_ROWS * D_MODEL - flat.shape[0])).reshape(SMALL_ROWS, D_MODEL)


def _unpack_small(packed, like):
    flat, out, at = packed.reshape(-1), {}, 0
    for k in SMALL:
        size = like[k].size
        out[k] = flat[at:at + size].reshape(like[k].shape)
        at += size
    return out


def _local_step(x, mem, positions, target, sm, big):
    cos_t, sin_t = _rope_tables(positions)
    pair = lambda a: jnp.tile(a, (1, 2))
    gq2, gk2, gmq2, gmk2 = pair(sm["g_q"]), pair(sm["g_k"]), pair(sm["g_mq"]), pair(sm["g_mk"])
    w_dw = big["w_dw"]

    h1, gate1, up1, x1 = _ffn_fwd(x, sm["g_ffn1"], big["w_ffn1_gate"], big["w_ffn1_up"], big["w_ffn1_down"], None, "ffn1_fwd")
    h2, proj = _in_proj_fwd(x1, sm["g_mix"], big["w_in"], "in_proj_fwd")
    ya, lse = _attn_fwd(proj, cos_t, sin_t, gq2, gk2, sm["sinks"], "attn_fwd")
    yc, cv = _conv_fwd(proj, w_dw, sm["b_dw"], sm["g_conv_ln"], sm["b_conv_ln"], "conv_fwd")
    hm, mkv = _mem_kv_fwd(mem, sm["g_mem"], big["w_mem_kv"], "mem_kv_fwd")
    ym, lse_m = _mem_attn_fwd(proj, mkv, gmq2, gmk2, "mem_attn_fwd")
    x2 = _out_proj_fwd(x1, ya, yc, ym, big["w_out"], "out_proj_fwd")
    h3, gate2, up2, dy, loss = _ffn_fwd(x2, sm["g_ffn2"], big["w_ffn2_gate"], big["w_ffn2_up"], big["w_ffn2_down"], target,
                                        "ffn2_fwd")

    grads = {}
    dyb2, act2, dgate2, dup2, dx2, grads["g_ffn2"] = _ffn_bwd_act(
        dy, x2, sm["g_ffn2"], gate2, up2, big["w_ffn2_gate"], big["w_ffn2_up"], big["w_ffn2_down"], "ffn2_bwd_act")
    grads["w_ffn2_gate"], grads["w_ffn2_up"], grads["w_ffn2_down"] = _ffn_bwd_w(h3, dyb2, act2, dgate2, dup2, "ffn2_bwd_w")
    dmix, grads["w_out"] = _out_proj_bwd(dx2, ya, yc, ym, big["w_out"], "out_proj_bwd")
    dq, dgq, grads["sinks"] = _attn_bwd_q(proj, cos_t, sin_t, gq2, gk2, sm["sinks"], ya, dmix, lse, "attn_bwd_q")
    dk, dv, dgk = _attn_bwd_kv(proj, cos_t, sin_t, gq2, gk2, ya, dmix, lse, "attn_bwd_kv")
    du, grads["w_dw"], grads["b_dw"], grads["g_conv_ln"], grads["b_conv_ln"] = _conv_bwd(
        proj, cv, dmix, w_dw, sm["g_conv_ln"], sm["b_conv_ln"], "conv_bwd")
    dmq, dmkv, dgmq, dgmk = _mem_attn_bwd(proj, mkv, gmq2, gmk2, ym, dmix, lse_m, "mem_attn_bwd")
    grads["w_mem_kv"], grads["g_mem"] = _mem_kv_bwd(mem, sm["g_mem"], hm, big["w_mem_kv"], dmkv, "mem_kv_bwd")
    dx1, grads["g_mix"], grads["w_in"] = _in_proj_bwd(dq, dk, dv, du, dmq, big["w_in"], h2, x1, sm["g_mix"], dx2, "in_proj_bwd")
    dyb1, act1, dgate1, dup1, dx0, grads["g_ffn1"] = _ffn_bwd_act(
        dx1, x, sm["g_ffn1"], gate1, up1, big["w_ffn1_gate"], big["w_ffn1_up"], big["w_ffn1_down"], "ffn1_bwd_act")
    grads["w_ffn1_gate"], grads["w_ffn1_up"], grads["w_ffn1_down"] = _ffn_bwd_w(h1, dyb1, act1, dgate1, dup1, "ffn1_bwd_w")
    fold = lambda a: a[:, :HEAD_DIM] + a[:, HEAD_DIM:]
    grads["g_q"], grads["g_k"], grads["g_mq"], grads["g_mk"] = fold(dgq), fold(dgk), fold(dgmq), fold(dgmk)
    return loss[0, 0], dx0, grads


def kernel(x, mem, positions, g_ffn1, w_ffn1_gate, w_ffn1_up, w_ffn1_down, g_mix, w_in, g_q, g_k, sinks, w_dw, b_dw, g_conv_ln, b_conv_ln, g_mem, w_mem_kv, g_mq, g_mk, w_out, g_ffn2, w_ffn2_gate, w_ffn2_up, w_ffn2_down, loss_target, m_g_ffn1, m_w_ffn1_gate, m_w_ffn1_up, m_w_ffn1_down, m_g_mix, m_w_in, m_g_q, m_g_k, m_sinks, m_w_dw, m_b_dw, m_g_conv_ln, m_b_conv_ln, m_g_mem, m_w_mem_kv, m_g_mq, m_g_mk, m_w_out, m_g_ffn2, m_w_ffn2_gate, m_w_ffn2_up, m_w_ffn2_down, v_g_ffn1, v_w_ffn1_gate, v_w_ffn1_up, v_w_ffn1_down, v_g_mix, v_w_in, v_g_q, v_g_k, v_sinks, v_w_dw, v_b_dw, v_g_conv_ln, v_b_conv_ln, v_g_mem, v_w_mem_kv, v_g_mq, v_g_mk, v_w_out, v_g_ffn2, v_w_ffn2_gate, v_w_ffn2_up, v_w_ffn2_down):
    w = dict(g_ffn1=g_ffn1, w_ffn1_gate=w_ffn1_gate, w_ffn1_up=w_ffn1_up, w_ffn1_down=w_ffn1_down, g_mix=g_mix, w_in=w_in,
             g_q=g_q, g_k=g_k, sinks=sinks, w_dw=w_dw, b_dw=b_dw, g_conv_ln=g_conv_ln, b_conv_ln=b_conv_ln, g_mem=g_mem,
             w_mem_kv=w_mem_kv, g_mq=g_mq, g_mk=g_mk, w_out=w_out, g_ffn2=g_ffn2, w_ffn2_gate=w_ffn2_gate,
             w_ffn2_up=w_ffn2_up, w_ffn2_down=w_ffn2_down)
    mo = dict(g_ffn1=m_g_ffn1, w_ffn1_gate=m_w_ffn1_gate, w_ffn1_up=m_w_ffn1_up, w_ffn1_down=m_w_ffn1_down, g_mix=m_g_mix,
              w_in=m_w_in, g_q=m_g_q, g_k=m_g_k, sinks=m_sinks, w_dw=m_w_dw, b_dw=m_b_dw, g_conv_ln=m_g_conv_ln,
              b_conv_ln=m_b_conv_ln, g_mem=m_g_mem, w_mem_kv=m_w_mem_kv, g_mq=m_g_mq, g_mk=m_g_mk, w_out=m_w_out,
              g_ffn2=m_g_ffn2, w_ffn2_gate=m_w_ffn2_gate, w_ffn2_up=m_w_ffn2_up, w_ffn2_down=m_w_ffn2_down)
    vo = dict(g_ffn1=v_g_ffn1, w_ffn1_gate=v_w_ffn1_gate, w_ffn1_up=v_w_ffn1_up, w_ffn1_down=v_w_ffn1_down, g_mix=v_g_mix,
              w_in=v_w_in, g_q=v_g_q, g_k=v_g_k, sinks=v_sinks, w_dw=v_w_dw, b_dw=v_b_dw, g_conv_ln=v_g_conv_ln,
              b_conv_ln=v_b_conv_ln, g_mem=v_g_mem, w_mem_kv=v_w_mem_kv, g_mq=v_g_mq, g_mk=v_g_mk, w_out=v_w_out,
              g_ffn2=v_g_ffn2, w_ffn2_gate=v_w_ffn2_gate, w_ffn2_up=v_w_ffn2_up, w_ffn2_down=v_w_ffn2_down)
    me = 4 * lax.axis_index("x") + 2 * lax.axis_index("y") + lax.axis_index("c")

    shards = [w[k][0].astype(BF) for k in BIG] + [w["w_dw"][0]]
    gathered = _exchange(shards, [], "gather_weights")
    big = dict(zip(BIG, gathered[:len(BIG)]))
    big["w_in"] = big["w_in"].transpose(1, 0, 2).reshape(D_MODEL, IN_COLS)
    big["w_mem_kv"] = big["w_mem_kv"].reshape(D_MODEL, 2 * MQ_COLS)
    big["w_out"] = big["w_out"].reshape(D_MODEL, D_MODEL)
    big["w_dw"] = gathered[-1].transpose(1, 0, 2).reshape(CONV_WIDTH, CONV_CH)
    sm = {k: w[k] for k in SMALL}

    loss_local, grad_x, grads = _local_step(x[0], mem[0], positions[0], loss_target[0], sm, big)

    blocks = dict(grads)
    blocks["w_in"] = grads["w_in"].reshape(D_MODEL, N_DEV, IN_COLS // N_DEV).transpose(1, 0, 2)
    blocks["w_mem_kv"] = grads["w_mem_kv"].reshape(N_DEV, D_MODEL // N_DEV, 2 * MQ_COLS)
    blocks["w_out"] = grads["w_out"].reshape(N_DEV, D_MODEL // N_DEV, D_MODEL)
    dw_flat = grads["w_dw"].reshape(-1)
    packed = jnp.concatenate([_pack_small(grads),
                              jnp.pad(dw_flat, (0, SMALL_ROWS * D_MODEL - dw_flat.shape[0])).reshape(SMALL_ROWS, D_MODEL)])
    received = _exchange([packed], [blocks[k] for k in BIG], "scatter_grads")
    small_sum = _slot_sum(received[0], "small_grad_sum")

    out_g, out_d, out_m, out_v = {}, {}, {}, {}
    for k, parts in zip(BIG, received[1:]):
        g, d, m2, v2 = _sum_adamw(parts, w[k][0], mo[k][0], vo[k][0], "adamw_" + k)
        out_g[k], out_d[k], out_m[k], out_v[k] = g[None], d[None], m2[None], v2[None]
    g_small = small_sum[:SMALL_ROWS]
    d, m2, v2 = _adamw(g_small, _pack_small(w), _pack_small(mo), _pack_small(vo), "adamw_small")
    for dst, val in ((out_g, g_small), (out_d, d), (out_m, m2), (out_v, v2)):
        dst.update(_unpack_small(val, sm))
    g_dw = small_sum[SMALL_ROWS:].reshape(-1)[:CONV_WIDTH * CONV_CH].reshape(CONV_WIDTH, CONV_CH)
    g_dw = lax.dynamic_slice_in_dim(g_dw, me * (CONV_CH // N_DEV), CONV_CH // N_DEV, axis=1)
    d, m2, v2 = _adamw(g_dw, w["w_dw"][0], mo["w_dw"][0], vo["w_dw"][0], "adamw_w_dw")
    out_g["w_dw"], out_d["w_dw"], out_m["w_dw"], out_v["w_dw"] = g_dw[None], d[None], m2[None], v2[None]

    loss = lax.psum(loss_local, ("x", "y", "c"))
    return (loss, grad_x[None], *[out_g[k] for k in WEIGHTS], *[out_d[k] for k in WEIGHTS], *[out_m[k] for k in WEIGHTS],
            *[out_v[k] for k in WEIGHTS])
```

```python
import jax
import jax.numpy as jnp
from jax import lax
from jax.experimental import pallas as pl
from jax.experimental.pallas import tpu as pltpu

D_MODEL = 1024
N_DEV = 8
FF_BLOCK = 352
HEAD_DIM = 64
PAIR = 2 * HEAD_DIM
N_Q_HEADS = 8
Q_PER_KV = 4
ATTN_BLOCK = 128
Q_COLS = 512
KV_COLS = 128
CONV_CH = 256
MQ_COLS = 256
IN_COLS = 1536
CONV_WIDTH = 31
CONV_PAD = 32
CONV_CHUNK = 256
N_MEM_HEADS = 4
ROPE_THETA = 500000.0
ROPE_HALF = 8
EPS = 1e-6
SCALE = HEAD_DIM ** -0.5
NEG = -1e30
ADAM_LR, ADAM_B1, ADAM_B2, ADAM_EPS, ADAM_WD, ADAM_STEP = 0.001, 0.9, 0.999, 1e-08, 0.01, 10
VMEM_LIMIT_BYTES = 56 * 1024 * 1024
BF = jnp.bfloat16
F32 = jnp.float32

SMALL = ["g_ffn1", "g_mix", "g_mem", "g_ffn2", "g_q", "g_k", "g_mq", "g_mk", "sinks", "b_dw", "g_conv_ln", "b_conv_ln"]
WEIGHTS = ["g_ffn1", "w_ffn1_gate", "w_ffn1_up", "w_ffn1_down", "g_mix", "w_in", "g_q", "g_k", "sinks", "w_dw", "b_dw",
           "g_conv_ln", "b_conv_ln", "g_mem", "w_mem_kv", "g_mq", "g_mk", "w_out", "g_ffn2", "w_ffn2_gate", "w_ffn2_up",
           "w_ffn2_down"]
SMALL_ROWS = 8
PACK_ROWS = 16


def _nn(a, b):
    return jnp.dot(a, b, preferred_element_type=F32)


def _nt(a, b):
    return lax.dot_general(a, b, (((1,), (1,)), ((), ())), preferred_element_type=F32)


def _tn(a, b):
    return lax.dot_general(a, b, (((0,), (0,)), ((), ())), preferred_element_type=F32)


def _params(n_grid):
    return pltpu.CompilerParams(dimension_semantics=("arbitrary",) * n_grid, vmem_limit_bytes=VMEM_LIMIT_BYTES)


def _ordered_after(body, after):
    if after is None:
        return body, (), []

    def body_after(after_ref, *refs):
        body(*refs)

    return body_after, (after,), [pl.BlockSpec(memory_space=pl.ANY)]


def _row_rms(xv):
    return lax.rsqrt(jnp.mean(xv * xv, axis=-1, keepdims=True) + EPS)


def _rms_bwd(dh, xv, r, g):
    u = dh * g
    dx = r * u - xv * (r * r * r) * jnp.mean(u * xv, axis=-1, keepdims=True)
    return dx, jnp.sum(dh * xv * r, axis=0, keepdims=True)


def _sum_all(a):
    return jnp.sum(jnp.sum(a, axis=1, keepdims=True), axis=0, keepdims=True)


def _lane_lo(shape):
    return lax.broadcasted_iota(jnp.int32, shape, 1) < HEAD_DIM


def _pair_sum(v):
    lo = _lane_lo(v.shape)
    s_lo = jnp.sum(jnp.where(lo, v, 0.0), axis=-1, keepdims=True)
    s_hi = jnp.sum(jnp.where(lo, 0.0, v), axis=-1, keepdims=True)
    return jnp.where(lo, s_lo, s_hi)


def _pair_rms(xv):
    return lax.rsqrt(_pair_sum(xv * xv) * (1.0 / HEAD_DIM) + EPS)


def _pair_rms_bwd(dxn, xv, r, g):
    u = dxn * g
    dx = r * u - xv * (r * r * r) * (_pair_sum(u * xv) * (1.0 / HEAD_DIM))
    return dx, jnp.sum(dxn * xv * r, axis=0, keepdims=True)


def _rope_mask(shape):
    lane = lax.broadcasted_iota(jnp.int32, shape, 1)
    return ((lane & (HEAD_DIM - 1)) < 2 * ROPE_HALF).astype(F32)


def _partner(v):
    return pltpu.roll(v, ROPE_HALF, 1) + pltpu.roll(v, PAIR - ROPE_HALF, 1)


def _rope(xn, cos_t, sin_t):
    return xn * cos_t + _partner(xn * _rope_mask(xn.shape)) * sin_t


def _rope_t(d, cos_t, sin_t):
    return d * cos_t + _partner(d * sin_t) * _rope_mask(d.shape)


def _rope_tables(positions):
    inv_freq = ROPE_THETA ** (-jnp.arange(ROPE_HALF, dtype=F32) / ROPE_HALF)
    ang = positions.astype(F32)[:, None] * inv_freq
    cos, sin = jnp.cos(ang), jnp.sin(ang)
    t = positions.shape[0]
    cos_h = jnp.concatenate([cos, cos, jnp.ones((t, HEAD_DIM - 2 * ROPE_HALF), F32)], axis=1)
    sin_h = jnp.concatenate([-sin, sin, jnp.zeros((t, HEAD_DIM - 2 * ROPE_HALF), F32)], axis=1)
    return jnp.tile(cos_h, (1, 2)), jnp.tile(sin_h, (1, 2))


def _ffn_fwd(x, g, wg, wu, wd, target, name, after=None):
    t_len = x.shape[0]
    tm = min(512, t_len)
    with_loss = target is not None

    def body(*refs):
        if with_loss:
            x_ref, g_ref, wg_ref, wu_ref, wd_ref, t_ref, h_ref, gg_ref, uu_ref, dy_ref, loss_ref, acc = refs
        else:
            x_ref, g_ref, wg_ref, wu_ref, wd_ref, h_ref, gg_ref, uu_ref, xo_ref, acc = refs
        t, j = pl.program_id(0), pl.program_id(1)

        @pl.when(j == 0)
        def _():
            xv = x_ref[...]
            h_ref[...] = (xv * _row_rms(xv) * g_ref[...]).astype(BF)
            acc[...] = jnp.zeros_like(acc)

        h = h_ref[...]
        gate = _nn(h, wg_ref[0])
        up = _nn(h, wu_ref[0])
        gg_ref[0] = gate.astype(BF)
        uu_ref[0] = up.astype(BF)
        act = (gate * jax.nn.sigmoid(gate) * up).astype(BF)
        acc[...] += _nn(act, wd_ref[0])

        @pl.when(j == N_DEV - 1)
        def _():
            xo = x_ref[...] + 0.5 * acc[...]
            if with_loss:
                err = xo - t_ref[...]
                dy_ref[...] = err * (1.0 / D_MODEL)

                @pl.when(t == 0)
                def _():
                    loss_ref[...] = jnp.zeros_like(loss_ref)

                loss_ref[...] += _sum_all(err * err) * (0.5 / D_MODEL)
            else:
                xo_ref[...] = xo

    row = pl.BlockSpec((tm, D_MODEL), lambda t, j: (t, 0))
    vec = pl.BlockSpec((1, D_MODEL), lambda t, j: (0, 0))
    w_in_spec = pl.BlockSpec((1, D_MODEL, FF_BLOCK), lambda t, j: (j, 0, 0))
    w_out_spec = pl.BlockSpec((1, FF_BLOCK, D_MODEL), lambda t, j: (j, 0, 0))
    blk = pl.BlockSpec((1, tm, FF_BLOCK), lambda t, j: (j, t, 0))
    in_specs = [row, vec, w_in_spec, w_in_spec, w_out_spec] + ([row] if with_loss else [])
    out_shape = [jax.ShapeDtypeStruct((t_len, D_MODEL), BF),
                 jax.ShapeDtypeStruct((N_DEV, t_len, FF_BLOCK), BF),
                 jax.ShapeDtypeStruct((N_DEV, t_len, FF_BLOCK), BF),
                 jax.ShapeDtypeStruct((t_len, D_MODEL), F32)]
    out_specs = [row, blk, blk, row]
    if with_loss:
        out_shape.append(jax.ShapeDtypeStruct((1, 128), F32))
        out_specs.append(pl.BlockSpec((1, 128), lambda t, j: (0, 0)))
    args = (x, g, wg, wu, wd) + ((target,) if with_loss else ())
    body, first, first_specs = _ordered_after(body, after)
    return pl.pallas_call(body, out_shape=out_shape, grid=(t_len // tm, N_DEV), in_specs=first_specs + in_specs,
                          out_specs=out_specs, scratch_shapes=[pltpu.VMEM((tm, D_MODEL), F32)], name=name,
                          compiler_params=_params(2))(*first, *args)


def _ffn_bwd_act(dy, x, g, gate, up, wg, wu, wd, name, after=None):
    t_len = x.shape[0]
    tm = min(512, t_len)

    def body(dy_ref, x_ref, g_ref, gg_ref, uu_ref, wg_ref, wu_ref, wd_ref,
             dyb_ref, act_ref, dgg_ref, duu_ref, dx_ref, dg_ref, acc):
        t, j = pl.program_id(0), pl.program_id(1)

        @pl.when(j == 0)
        def _():
            dyb_ref[...] = (0.5 * dy_ref[...]).astype(BF)
            acc[...] = jnp.zeros_like(acc)

        d_act = _nt(dyb_ref[...], wd_ref[0])
        gate = gg_ref[0].astype(F32)
        upv = uu_ref[0].astype(F32)
        sig = jax.nn.sigmoid(gate)
        silu = gate * sig
        d_up = (d_act * silu).astype(BF)
        d_gate = (d_act * upv * (sig * (1.0 + gate * (1.0 - sig)))).astype(BF)
        act_ref[0] = (silu * upv).astype(BF)
        dgg_ref[0] = d_gate
        duu_ref[0] = d_up
        acc[...] += _nt(d_gate, wg_ref[0]) + _nt(d_up, wu_ref[0])

        @pl.when(j == N_DEV - 1)
        def _():
            xv = x_ref[...]
            dx, dg = _rms_bwd(acc[...], xv, _row_rms(xv), g_ref[...])
            dx_ref[...] = dy_ref[...] + dx

            @pl.when(t == 0)
            def _():
                dg_ref[...] = jnp.zeros_like(dg_ref)

            dg_ref[...] += dg

    row = pl.BlockSpec((tm, D_MODEL), lambda t, j: (t, 0))
    vec = pl.BlockSpec((1, D_MODEL), lambda t, j: (0, 0))
    w_in_spec = pl.BlockSpec((1, D_MODEL, FF_BLOCK), lambda t, j: (j, 0, 0))
    w_out_spec = pl.BlockSpec((1, FF_BLOCK, D_MODEL), lambda t, j: (j, 0, 0))
    blk = pl.BlockSpec((1, tm, FF_BLOCK), lambda t, j: (j, t, 0))
    blk_shape = jax.ShapeDtypeStruct((N_DEV, t_len, FF_BLOCK), BF)
    body, first, first_specs = _ordered_after(body, after)
    return pl.pallas_call(
        body,
        out_shape=[jax.ShapeDtypeStruct((t_len, D_MODEL), BF), blk_shape, blk_shape, blk_shape,
                   jax.ShapeDtypeStruct((t_len, D_MODEL), F32), jax.ShapeDtypeStruct((1, D_MODEL), F32)],
        grid=(t_len // tm, N_DEV), in_specs=first_specs + [row, row, vec, blk, blk, w_in_spec, w_in_spec, w_out_spec],
        out_specs=[row, blk, blk, blk, row, vec], scratch_shapes=[pltpu.VMEM((tm, D_MODEL), F32)], name=name,
        compiler_params=_params(2))(*first, dy, x, g, gate, up, wg, wu, wd)


def _ffn_bwd_w(h, dyb, act, d_gate, d_up, name):
    t_len = h.shape[0]
    tm = min(512, t_len)
    nt = t_len // tm

    def body(h_ref, dyb_ref, act_ref, dgg_ref, duu_ref, dwg_ref, dwu_ref, dwd_ref, ag, au, ad):
        t = pl.program_id(1)

        @pl.when(t == 0)
        def _():
            ag[...] = jnp.zeros_like(ag)
            au[...] = jnp.zeros_like(au)
            ad[...] = jnp.zeros_like(ad)

        h = h_ref[...]
        ag[...] += _tn(h, dgg_ref[0])
        au[...] += _tn(h, duu_ref[0])
        ad[...] += _tn(act_ref[0], dyb_ref[...])

        @pl.when(t == nt - 1)
        def _():
            dwg_ref[0] = ag[...].astype(BF)
            dwu_ref[0] = au[...].astype(BF)
            dwd_ref[0] = ad[...].astype(BF)

    row = pl.BlockSpec((tm, D_MODEL), lambda j, t: (t, 0))
    blk = pl.BlockSpec((1, tm, FF_BLOCK), lambda j, t: (j, t, 0))
    w_in_spec = pl.BlockSpec((1, D_MODEL, FF_BLOCK), lambda j, t: (j, 0, 0))
    w_out_spec = pl.BlockSpec((1, FF_BLOCK, D_MODEL), lambda j, t: (j, 0, 0))
    return pl.pallas_call(
        body,
        out_shape=[jax.ShapeDtypeStruct((N_DEV, D_MODEL, FF_BLOCK), BF), jax.ShapeDtypeStruct((N_DEV, D_MODEL, FF_BLOCK), BF),
                   jax.ShapeDtypeStruct((N_DEV, FF_BLOCK, D_MODEL), BF)],
        grid=(N_DEV, nt), in_specs=[row, row, blk, blk, blk], out_specs=[w_in_spec, w_in_spec, w_out_spec],
        scratch_shapes=[pltpu.VMEM((D_MODEL, FF_BLOCK), F32), pltpu.VMEM((D_MODEL, FF_BLOCK), F32),
                        pltpu.VMEM((FF_BLOCK, D_MODEL), F32)],
        name=name, compiler_params=_params(2))(h, dyb, act, d_gate, d_up)


def _in_proj_fwd(x, g, w_in, name):
    t_len = x.shape[0]
    tm = min(512, t_len)

    def body(x_ref, g_ref, w_ref, h_ref, p_ref):
        xv = x_ref[...]
        h = (xv * _row_rms(xv) * g_ref[...]).astype(BF)
        h_ref[...] = h
        p_ref[...] = _nn(h, w_ref[...])

    row = pl.BlockSpec((tm, D_MODEL), lambda t: (t, 0))
    return pl.pallas_call(
        body, out_shape=[jax.ShapeDtypeStruct((t_len, D_MODEL), BF), jax.ShapeDtypeStruct((t_len, IN_COLS), F32)],
        grid=(t_len // tm,),
        in_specs=[row, pl.BlockSpec((1, D_MODEL), lambda t: (0, 0)), pl.BlockSpec((D_MODEL, IN_COLS), lambda t: (0, 0))],
        out_specs=[row, pl.BlockSpec((tm, IN_COLS), lambda t: (t, 0))], name=name, compiler_params=_params(1))(x, g, w_in)


def _in_proj_bwd(dq, dk, dv, du, dmq, w_in, h, x, g, dres, name):
    t_len = x.shape[0]
    tm = min(512, t_len)
    nt = t_len // tm
    groups = [(0, Q_COLS), (Q_COLS, KV_COLS), (Q_COLS + KV_COLS, KV_COLS), (Q_COLS + 2 * KV_COLS, 2 * CONV_CH),
              (Q_COLS + 2 * KV_COLS + 2 * CONV_CH, MQ_COLS)]

    def body(dq_ref, dk_ref, dv_ref, du_ref, dmq_ref, w_ref, h_ref, x_ref, g_ref, dres_ref, dx_ref, dg_ref, dw_ref, acc):
        t = pl.program_id(0)

        @pl.when(t == 0)
        def _():
            acc[...] = jnp.zeros_like(acc)
            dg_ref[...] = jnp.zeros_like(dg_ref)

        h = h_ref[...]
        dh = jnp.zeros((tm, D_MODEL), F32)
        for (start, width), ref in zip(groups, (dq_ref, dk_ref, dv_ref, du_ref, dmq_ref)):
            piece = ref[...]
            dh = dh + _nt(piece, w_ref[:, start:start + width])
            acc[:, start:start + width] += _tn(h, piece)
        xv = x_ref[...]
        dx, dg = _rms_bwd(dh, xv, _row_rms(xv), g_ref[...])
        dx_ref[...] = dres_ref[...] + dx
        dg_ref[...] += dg

        @pl.when(t == nt - 1)
        def _():
            dw_ref[...] = acc[...].astype(BF)

    def cols(width):
        return pl.BlockSpec((tm, width), lambda t: (t, 0))

    row = cols(D_MODEL)
    vec = pl.BlockSpec((1, D_MODEL), lambda t: (0, 0))
    full = pl.BlockSpec((D_MODEL, IN_COLS), lambda t: (0, 0))
    return pl.pallas_call(
        body,
        out_shape=[jax.ShapeDtypeStruct((t_len, D_MODEL), F32), jax.ShapeDtypeStruct((1, D_MODEL), F32),
                   jax.ShapeDtypeStruct((D_MODEL, IN_COLS), BF)],
        grid=(nt,),
        in_specs=[cols(Q_COLS), cols(KV_COLS), cols(KV_COLS), cols(2 * CONV_CH), cols(MQ_COLS), full, row, row, vec, row],
        out_specs=[row, vec, full], scratch_shapes=[pltpu.VMEM((D_MODEL, IN_COLS), F32)], name=name,
        compiler_params=_params(1))(dq, dk, dv, du, dmq, w_in, h, x, g, dres)


def _attn_fwd(proj, cos_t, sin_t, gq2, gk2, sinks, name):
    t_len = proj.shape[0]
    nb = t_len // ATTN_BLOCK

    def body(q_ref, kvc_ref, kvp_ref, cq_ref, sq_ref, cp_ref, sp_ref, gq_ref, gk_ref, sk_ref, ya_ref, lse_ref):
        n = pl.program_id(0)
        cq, sq = cq_ref[...], sq_ref[...]
        c_all = jnp.concatenate([cp_ref[...], cq], axis=0)
        s_all = jnp.concatenate([sp_ref[...], sq], axis=0)
        kv = jnp.concatenate([kvp_ref[...], kvc_ref[...]], axis=0)
        k_raw, v = kv[:, :PAIR], kv[:, PAIR:]
        kn = _rope(k_raw * _pair_rms(k_raw) * gk_ref[...], c_all, s_all)
        row = lax.broadcasted_iota(jnp.int32, (ATTN_BLOCK, 2 * ATTN_BLOCK), 0)
        col = lax.broadcasted_iota(jnp.int32, (ATTN_BLOCK, 2 * ATTN_BLOCK), 1)
        rel = row + ATTN_BLOCK - col
        valid = (rel >= 0) & (rel < ATTN_BLOCK) & ((col >= ATTN_BLOCK) | (n > 0))
        for pr in range(N_Q_HEADS // 2):
            q_raw = q_ref[:, pr * PAIR:(pr + 1) * PAIR]
            qn = _rope(q_raw * _pair_rms(q_raw) * gq_ref[...], cq, sq)
            kvh = (2 * pr) // Q_PER_KV
            kh = kn[:, kvh * HEAD_DIM:(kvh + 1) * HEAD_DIM].astype(BF)
            vh = v[:, kvh * HEAD_DIM:(kvh + 1) * HEAD_DIM].astype(BF)
            for hh in range(2):
                hd = 2 * pr + hh
                qh = qn[:, hh * HEAD_DIM:(hh + 1) * HEAD_DIM].astype(BF)
                s = jnp.where(valid, _nt(qh, kh) * SCALE, NEG)
                sink = sk_ref[0:1, hd:hd + 1]
                m = jnp.maximum(jnp.max(s, axis=-1, keepdims=True), sink)
                p = jnp.exp(s - m)
                den = jnp.sum(p, axis=-1, keepdims=True) + jnp.exp(sink - m)
                ya_ref[:, hd * HEAD_DIM:(hd + 1) * HEAD_DIM] = _nn((p / den).astype(BF), vh).astype(BF)
                lse_ref[:, hd:hd + 1] = m + jnp.log(den)

    prev = lambda n: (jnp.maximum(n - 1, 0), 0)
    tab = pl.BlockSpec((ATTN_BLOCK, PAIR), lambda n: (n, 0))
    tab_p = pl.BlockSpec((ATTN_BLOCK, PAIR), prev)
    one = lambda w: pl.BlockSpec((1, w), lambda n: (0, 0))
    return pl.pallas_call(
        body, out_shape=[jax.ShapeDtypeStruct((t_len, Q_COLS), BF), jax.ShapeDtypeStruct((t_len, N_Q_HEADS), F32)],
        grid=(nb,),
        in_specs=[pl.BlockSpec((ATTN_BLOCK, Q_COLS), lambda n: (n, 0)),
                  pl.BlockSpec((ATTN_BLOCK, 2 * KV_COLS), lambda n: (n, 2)),
                  pl.BlockSpec((ATTN_BLOCK, 2 * KV_COLS), lambda n: (jnp.maximum(n - 1, 0), 2)),
                  tab, tab, tab_p, tab_p, one(PAIR), one(PAIR), one(N_Q_HEADS)],
        out_specs=[pl.BlockSpec((ATTN_BLOCK, Q_COLS), lambda n: (n, 0)),
                   pl.BlockSpec((ATTN_BLOCK, N_Q_HEADS), lambda n: (n, 0))],
        name=name, compiler_params=_params(1))(proj, proj, proj, cos_t, sin_t, cos_t, sin_t, gq2, gk2, sinks)


def _attn_bwd_q(proj, cos_t, sin_t, gq2, gk2, sinks, ya, dmix, lse, name):
    t_len = proj.shape[0]
    nb = t_len // ATTN_BLOCK

    def body(q_ref, kvc_ref, kvp_ref, cq_ref, sq_ref, cp_ref, sp_ref, gq_ref, gk_ref, sk_ref, ya_ref, do_ref, lse_ref,
             dq_ref, dgq_ref, dsk_ref, scr):
        n = pl.program_id(0)

        @pl.when(n == 0)
        def _():
            dgq_ref[...] = jnp.zeros_like(dgq_ref)
            dsk_ref[...] = jnp.zeros_like(dsk_ref)

        cq, sq = cq_ref[...], sq_ref[...]
        c_all = jnp.concatenate([cp_ref[...], cq], axis=0)
        s_all = jnp.concatenate([sp_ref[...], sq], axis=0)
        kv = jnp.concatenate([kvp_ref[...], kvc_ref[...]], axis=0)
        k_raw, v = kv[:, :PAIR], kv[:, PAIR:]
        kn = _rope(k_raw * _pair_rms(k_raw) * gk_ref[...], c_all, s_all)
        row = lax.broadcasted_iota(jnp.int32, (ATTN_BLOCK, 2 * ATTN_BLOCK), 0)
        col = lax.broadcasted_iota(jnp.int32, (ATTN_BLOCK, 2 * ATTN_BLOCK), 1)
        rel = row + ATTN_BLOCK - col
        valid = (rel >= 0) & (rel < ATTN_BLOCK) & ((col >= ATTN_BLOCK) | (n > 0))
        gq = gq_ref[...]
        for pr in range(N_Q_HEADS // 2):
            q_raw = q_ref[:, pr * PAIR:(pr + 1) * PAIR]
            rq = _pair_rms(q_raw)
            qn = _rope(q_raw * rq * gq, cq, sq)
            kvh = (2 * pr) // Q_PER_KV
            kh = kn[:, kvh * HEAD_DIM:(kvh + 1) * HEAD_DIM].astype(BF)
            vh = v[:, kvh * HEAD_DIM:(kvh + 1) * HEAD_DIM].astype(BF)
            for hh in range(2):
                hd = 2 * pr + hh
                qh = qn[:, hh * HEAD_DIM:(hh + 1) * HEAD_DIM].astype(BF)
                lse_h = lse_ref[:, hd:hd + 1]
                p = jnp.exp(jnp.where(valid, _nt(qh, kh) * SCALE, NEG) - lse_h)
                d_o = do_ref[:, hd * HEAD_DIM:(hd + 1) * HEAD_DIM]
                o = ya_ref[:, hd * HEAD_DIM:(hd + 1) * HEAD_DIM].astype(F32)
                dsum = jnp.sum(d_o * o, axis=-1, keepdims=True)
                ds = p * (_nt(d_o.astype(BF), vh) - dsum)
                scr[:, hh * HEAD_DIM:(hh + 1) * HEAD_DIM] = _nn(ds.astype(BF), kh) * SCALE
                sink = sk_ref[0:1, hd:hd + 1]
                dsk_ref[0:1, hd:hd + 1] += -jnp.sum(jnp.exp(sink - lse_h) * dsum, axis=0, keepdims=True)
            dx, dg = _pair_rms_bwd(_rope_t(scr[...], cq, sq), q_raw, rq, gq)
            dq_ref[:, pr * PAIR:(pr + 1) * PAIR] = dx.astype(BF)
            dgq_ref[...] += dg

    prev = lambda n: (jnp.maximum(n - 1, 0), 0)
    tab = pl.BlockSpec((ATTN_BLOCK, PAIR), lambda n: (n, 0))
    tab_p = pl.BlockSpec((ATTN_BLOCK, PAIR), prev)
    one = lambda w: pl.BlockSpec((1, w), lambda n: (0, 0))
    qblk = pl.BlockSpec((ATTN_BLOCK, Q_COLS), lambda n: (n, 0))
    return pl.pallas_call(
        body,
        out_shape=[jax.ShapeDtypeStruct((t_len, Q_COLS), BF), jax.ShapeDtypeStruct((1, PAIR), F32),
                   jax.ShapeDtypeStruct((1, N_Q_HEADS), F32)],
        grid=(nb,),
        in_specs=[qblk, pl.BlockSpec((ATTN_BLOCK, 2 * KV_COLS), lambda n: (n, 2)),
                  pl.BlockSpec((ATTN_BLOCK, 2 * KV_COLS), lambda n: (jnp.maximum(n - 1, 0), 2)),
                  tab, tab, tab_p, tab_p, one(PAIR), one(PAIR), one(N_Q_HEADS), qblk, qblk,
                  pl.BlockSpec((ATTN_BLOCK, N_Q_HEADS), lambda n: (n, 0))],
        out_specs=[qblk, one(PAIR), one(N_Q_HEADS)], scratch_shapes=[pltpu.VMEM((ATTN_BLOCK, PAIR), F32)],
        name=name, compiler_params=_params(1))(proj, proj, proj, cos_t, sin_t, cos_t, sin_t, gq2, gk2, sinks, ya, dmix, lse)


def _attn_bwd_kv(proj, cos_t, sin_t, gq2, gk2, ya, dmix, lse, name):
    t_len = proj.shape[0]
    nb = t_len // ATTN_BLOCK

    def body(kv_ref, q0_ref, q1_ref, ck_ref, sk_ref, c1_ref, s1_ref, gq_ref, gk_ref, o0_ref, o1_ref, do0_ref, do1_ref,
             l0_ref, l1_ref, dk_ref, dv_ref, dgk_ref, dkn_scr, dv_scr):
        m = pl.program_id(0)

        @pl.when(m == 0)
        def _():
            dgk_ref[...] = jnp.zeros_like(dgk_ref)

        ck, sk = ck_ref[...], sk_ref[...]
        c_all = jnp.concatenate([ck, c1_ref[...]], axis=0)
        s_all = jnp.concatenate([sk, s1_ref[...]], axis=0)
        kvv = kv_ref[...]
        k_raw, v = kvv[:, :PAIR], kvv[:, PAIR:]
        rk = _pair_rms(k_raw)
        gk = gk_ref[...]
        kn = _rope(k_raw * rk * gk, ck, sk)
        row = lax.broadcasted_iota(jnp.int32, (2 * ATTN_BLOCK, ATTN_BLOCK), 0)
        col = lax.broadcasted_iota(jnp.int32, (2 * ATTN_BLOCK, ATTN_BLOCK), 1)
        valid = ((row < ATTN_BLOCK) & (row >= col)) | ((row >= ATTN_BLOCK) & (row - ATTN_BLOCK < col) & (m < nb - 1))
        lse2 = jnp.concatenate([l0_ref[...], l1_ref[...]], axis=0)
        dkn_scr[...] = jnp.zeros_like(dkn_scr)
        dv_scr[...] = jnp.zeros_like(dv_scr)
        for pr in range(N_Q_HEADS // 2):
            lanes = slice(pr * PAIR, (pr + 1) * PAIR)
            q_raw = jnp.concatenate([q0_ref[:, lanes], q1_ref[:, lanes]], axis=0)
            qn = _rope(q_raw * _pair_rms(q_raw) * gq_ref[...], c_all, s_all)
            d_o2 = jnp.concatenate([do0_ref[:, lanes], do1_ref[:, lanes]], axis=0)
            o2 = jnp.concatenate([o0_ref[:, lanes], o1_ref[:, lanes]], axis=0).astype(F32)
            dsum2 = _pair_sum(d_o2 * o2)
            kvh = (2 * pr) // Q_PER_KV
            khs = slice(kvh * HEAD_DIM, (kvh + 1) * HEAD_DIM)
            kh = kn[:, khs].astype(BF)
            vh = v[:, khs].astype(BF)
            for hh in range(2):
                hd = 2 * pr + hh
                hs = slice(hh * HEAD_DIM, (hh + 1) * HEAD_DIM)
                qh = qn[:, hs].astype(BF)
                d_oh = d_o2[:, hs].astype(BF)
                p = jnp.exp(jnp.where(valid, _nt(qh, kh) * SCALE, NEG) - lse2[:, hd:hd + 1])
                dv_scr[:, khs] += _tn(p.astype(BF), d_oh)
                ds = p * (_nt(d_oh, vh) - dsum2[:, hh * HEAD_DIM:hh * HEAD_DIM + 1])
                dkn_scr[:, khs] += _tn(ds.astype(BF), qh) * SCALE
        dx, dg = _pair_rms_bwd(_rope_t(dkn_scr[...], ck, sk), k_raw, rk, gk)
        dk_ref[...] = dx.astype(BF)
        dv_ref[...] = dv_scr[...].astype(BF)
        dgk_ref[...] += dg

    nxt = lambda m: (jnp.minimum(m + 1, nb - 1), 0)
    cur = lambda m: (m, 0)
    tab = lambda f: pl.BlockSpec((ATTN_BLOCK, PAIR), f)
    qb = lambda f: pl.BlockSpec((ATTN_BLOCK, Q_COLS), f)
    lb = lambda f: pl.BlockSpec((ATTN_BLOCK, N_Q_HEADS), f)
    one = pl.BlockSpec((1, PAIR), lambda m: (0, 0))
    return pl.pallas_call(
        body,
        out_shape=[jax.ShapeDtypeStruct((t_len, KV_COLS), BF), jax.ShapeDtypeStruct((t_len, KV_COLS), BF),
                   jax.ShapeDtypeStruct((1, PAIR), F32)],
        grid=(nb,),
        in_specs=[pl.BlockSpec((ATTN_BLOCK, 2 * KV_COLS), lambda m: (m, 2)), qb(cur), qb(nxt), tab(cur), tab(cur), tab(nxt),
                  tab(nxt), one, one, qb(cur), qb(nxt), qb(cur), qb(nxt), lb(cur), lb(nxt)],
        out_specs=[tab(cur), tab(cur), one],
        scratch_shapes=[pltpu.VMEM((ATTN_BLOCK, PAIR), F32), pltpu.VMEM((ATTN_BLOCK, PAIR), F32)],
        name=name, compiler_params=_params(1))(proj, proj, proj, cos_t, sin_t, cos_t, sin_t, gq2, gk2, ya, ya, dmix, dmix,
                                               lse, lse)


def _conv_taps(blk, w, offset_of):
    acc = jnp.zeros((CONV_CHUNK, CONV_CH), F32)
    for k in range(CONV_WIDTH):
        o = offset_of(k)
        acc = acc + w[k:k + 1, :] * blk[o:o + CONV_CHUNK, :]
    return acc


def _conv_fwd(proj, w_dw, b_dw, g_ln, b_ln, name):
    t_len = proj.shape[0]
    nc = t_len // CONV_CHUNK

    def body(a_ref, gt_ref, w_ref, b_ref, g_ref, bl_ref, yc_ref, cv_ref, pad):
        pad[0:CONV_PAD, :] = jnp.zeros((CONV_PAD, CONV_CH), F32)

        def glu(c, carry):
            rows = pl.ds(pl.multiple_of(c * CONV_CHUNK, CONV_CHUNK), CONV_CHUNK)
            dst = pl.ds(pl.multiple_of(c * CONV_CHUNK + CONV_PAD, CONV_PAD), CONV_CHUNK)
            pad[dst, :] = a_ref[rows, :] * jax.nn.sigmoid(gt_ref[rows, :])
            return carry

        lax.fori_loop(0, nc, glu, 0)
        w = w_ref[...]

        def conv(c, carry):
            base = pl.multiple_of(c * CONV_CHUNK, CONV_CHUNK)
            blk = pad[pl.ds(base, CONV_CHUNK + CONV_PAD), :]
            cv = _conv_taps(blk, w, lambda k: k + CONV_PAD - (CONV_WIDTH - 1)) + b_ref[...]
            mu = jnp.mean(cv, axis=-1, keepdims=True)
            xc = cv - mu
            z = xc * lax.rsqrt(jnp.mean(xc * xc, axis=-1, keepdims=True) + EPS) * g_ref[...] + bl_ref[...]
            rows = pl.ds(base, CONV_CHUNK)
            cv_ref[rows, :] = cv
            yc_ref[rows, :] = (z * jax.nn.sigmoid(z)).astype(BF)
            return carry

        lax.fori_loop(0, nc, conv, 0)

    vec = pl.BlockSpec((1, CONV_CH), lambda i: (0, 0))
    full = pl.BlockSpec((t_len, CONV_CH), lambda i: (0, 0))
    return pl.pallas_call(
        body, out_shape=[jax.ShapeDtypeStruct((t_len, CONV_CH), BF), jax.ShapeDtypeStruct((t_len, CONV_CH), F32)],
        grid=(1,),
        in_specs=[pl.BlockSpec((t_len, CONV_CH), lambda i: (0, 3)), pl.BlockSpec((t_len, CONV_CH), lambda i: (0, 4)),
                  pl.BlockSpec((CONV_WIDTH, CONV_CH), lambda i: (0, 0)), vec, vec, vec],
        out_specs=[full, full], scratch_shapes=[pltpu.VMEM((t_len + CONV_PAD, CONV_CH), F32)], name=name,
        compiler_params=_params(1))(proj, proj, w_dw, b_dw, g_ln, b_ln)


def _conv_bwd(proj, cv, dmix, w_dw, g_ln, b_ln, name):
    t_len = proj.shape[0]
    nc = t_len // CONV_CHUNK

    def body(a_ref, gt_ref, cv_ref, dy_ref, w_ref, g_ref, bl_ref, du_ref, dw_ref, db_ref, dgl_ref, dbl_ref, pad, dpad):
        pad[0:CONV_PAD, :] = jnp.zeros((CONV_PAD, CONV_CH), F32)
        dpad[t_len:t_len + CONV_PAD, :] = jnp.zeros((CONV_PAD, CONV_CH), F32)
        dw_ref[...] = jnp.zeros_like(dw_ref)
        db_ref[...] = jnp.zeros_like(db_ref)
        dgl_ref[...] = jnp.zeros_like(dgl_ref)
        dbl_ref[...] = jnp.zeros_like(dbl_ref)

        def norm_bwd(c, carry):
            base = pl.multiple_of(c * CONV_CHUNK, CONV_CHUNK)
            rows = pl.ds(base, CONV_CHUNK)
            dst = pl.ds(pl.multiple_of(c * CONV_CHUNK + CONV_PAD, CONV_PAD), CONV_CHUNK)
            pad[dst, :] = a_ref[rows, :] * jax.nn.sigmoid(gt_ref[rows, :])
            cvv = cv_ref[rows, :]
            xc = cvv - jnp.mean(cvv, axis=-1, keepdims=True)
            rs = lax.rsqrt(jnp.mean(xc * xc, axis=-1, keepdims=True) + EPS)
            xhat = xc * rs
            z = xhat * g_ref[...] + bl_ref[...]
            sg = jax.nn.sigmoid(z)
            dz = dy_ref[rows, :] * (sg * (1.0 + z * (1.0 - sg)))
            dgl_ref[...] += jnp.sum(dz * xhat, axis=0, keepdims=True)
            dbl_ref[...] += jnp.sum(dz, axis=0, keepdims=True)
            dxh = dz * g_ref[...]
            dcv = rs * (dxh - jnp.mean(dxh, axis=-1, keepdims=True) - xhat * jnp.mean(dxh * xhat, axis=-1, keepdims=True))
            db_ref[...] += jnp.sum(dcv, axis=0, keepdims=True)
            dpad[rows, :] = dcv
            return carry

        lax.fori_loop(0, nc, norm_bwd, 0)
        w = w_ref[...]

        def conv_bwd(c, carry):
            base = pl.multiple_of(c * CONV_CHUNK, CONV_CHUNK)
            rows = pl.ds(base, CONV_CHUNK)
            dblk = dpad[pl.ds(base, CONV_CHUNK + CONV_PAD), :]
            dhc = _conv_taps(dblk, w, lambda k: CONV_WIDTH - 1 - k)
            a = a_ref[rows, :]
            sg = jax.nn.sigmoid(gt_ref[rows, :])
            du_ref[rows, 0:CONV_CH] = (dhc * sg).astype(BF)
            du_ref[rows, CONV_CH:2 * CONV_CH] = (dhc * a * sg * (1.0 - sg)).astype(BF)
            hblk = pad[pl.ds(base, CONV_CHUNK + CONV_PAD), :]
            dcv = dblk[0:CONV_CHUNK, :]
            for k in range(CONV_WIDTH):
                o = k + CONV_PAD - (CONV_WIDTH - 1)
                dw_ref[k:k + 1, :] += jnp.sum(dcv * hblk[o:o + CONV_CHUNK, :], axis=0, keepdims=True)
            return carry

        lax.fori_loop(0, nc, conv_bwd, 0)

    vec = pl.BlockSpec((1, CONV_CH), lambda i: (0, 0))
    full = pl.BlockSpec((t_len, CONV_CH), lambda i: (0, 0))
    wspec = pl.BlockSpec((CONV_WIDTH, CONV_CH), lambda i: (0, 0))
    vshape = jax.ShapeDtypeStruct((1, CONV_CH), F32)
    return pl.pallas_call(
        body,
        out_shape=[jax.ShapeDtypeStruct((t_len, 2 * CONV_CH), BF), jax.ShapeDtypeStruct((CONV_WIDTH, CONV_CH), F32),
                   vshape, vshape, vshape],
        grid=(1,),
        in_specs=[pl.BlockSpec((t_len, CONV_CH), lambda i: (0, 3)), pl.BlockSpec((t_len, CONV_CH), lambda i: (0, 4)), full,
                  pl.BlockSpec((t_len, CONV_CH), lambda i: (0, 2)), wspec, vec, vec],
        out_specs=[pl.BlockSpec((t_len, 2 * CONV_CH), lambda i: (0, 0)), wspec, vec, vec, vec],
        scratch_shapes=[pltpu.VMEM((t_len + CONV_PAD, CONV_CH), F32), pltpu.VMEM((t_len + CONV_PAD, CONV_CH), F32)],
        name=name, compiler_params=_params(1))(proj, proj, cv, dmix, w_dw, g_ln, b_ln)


def _mem_kv_fwd(mem, g, w, name):
    def body(m_ref, g_ref, w_ref, h_ref, kv_ref):
        mv = m_ref[...]
        h = (mv * _row_rms(mv) * g_ref[...]).astype(BF)
        h_ref[...] = h
        kv_ref[...] = _nn(h, w_ref[...])

    m_len = mem.shape[0]
    return pl.pallas_call(
        body, out_shape=[jax.ShapeDtypeStruct((m_len, D_MODEL), BF), jax.ShapeDtypeStruct((m_len, 2 * MQ_COLS), F32)],
        name=name, compiler_params=pltpu.CompilerParams(vmem_limit_bytes=VMEM_LIMIT_BYTES))(mem, g, w)


def _mem_kv_bwd(mem, g, h, w, dkv, name):
    def body(m_ref, g_ref, h_ref, w_ref, dkv_ref, dw_ref, dg_ref):
        dkv_b = dkv_ref[...].astype(BF)
        dw_ref[...] = _tn(h_ref[...], dkv_b).astype(BF)
        mv = m_ref[...]
        dg_ref[...] = jnp.sum(_nt(dkv_b, w_ref[...]) * mv * _row_rms(mv), axis=0, keepdims=True)

    return pl.pallas_call(
        body, out_shape=[jax.ShapeDtypeStruct((D_MODEL, 2 * MQ_COLS), BF), jax.ShapeDtypeStruct((1, D_MODEL), F32)],
        name=name, compiler_params=pltpu.CompilerParams(vmem_limit_bytes=VMEM_LIMIT_BYTES))(mem, g, h, w, dkv)


def _mem_attn_fwd(proj, mkv, gq2, gk2, name):
    t_len = proj.shape[0]
    tm = min(256, t_len)

    def body(q_ref, kv_ref, gq_ref, gk_ref, ym_ref, lse_ref):
        kvv = kv_ref[...]
        for pr in range(N_MEM_HEADS // 2):
            lanes = slice(pr * PAIR, (pr + 1) * PAIR)
            k_raw = kvv[:, lanes]
            kn = k_raw * _pair_rms(k_raw) * gk_ref[...]
            v = kvv[:, MQ_COLS + pr * PAIR:MQ_COLS + (pr + 1) * PAIR]
            q_raw = q_ref[:, lanes]
            qn = q_raw * _pair_rms(q_raw) * gq_ref[...]
            for hh in range(2):
                hd = 2 * pr + hh
                hs = slice(hh * HEAD_DIM, (hh + 1) * HEAD_DIM)
                s = _nt(qn[:, hs].astype(BF), kn[:, hs].astype(BF)) * SCALE
                m = jnp.max(s, axis=-1, keepdims=True)
                p = jnp.exp(s - m)
                den = jnp.sum(p, axis=-1, keepdims=True)
                ym_ref[:, hd * HEAD_DIM:(hd + 1) * HEAD_DIM] = _nn((p / den).astype(BF), v[:, hs].astype(BF)).astype(BF)
                lse_ref[:, hd:hd + 1] = m + jnp.log(den)

    m_len = mkv.shape[0]
    one = pl.BlockSpec((1, PAIR), lambda t: (0, 0))
    return pl.pallas_call(
        body, out_shape=[jax.ShapeDtypeStruct((t_len, MQ_COLS), BF), jax.ShapeDtypeStruct((t_len, N_MEM_HEADS), F32)],
        grid=(t_len // tm,),
        in_specs=[pl.BlockSpec((tm, MQ_COLS), lambda t: (t, 5)), pl.BlockSpec((m_len, 2 * MQ_COLS), lambda t: (0, 0)), one, one],
        out_specs=[pl.BlockSpec((tm, MQ_COLS), lambda t: (t, 0)), pl.BlockSpec((tm, N_MEM_HEADS), lambda t: (t, 0))],
        name=name, compiler_params=_params(1))(proj, mkv, gq2, gk2)


def _mem_attn_bwd(proj, mkv, gq2, gk2, ym, dmix, lse, name):
    t_len = proj.shape[0]
    tm = min(256, t_len)
    nt = t_len // tm
    m_len = mkv.shape[0]

    def body(q_ref, kv_ref, gq_ref, gk_ref, ym_ref, do_ref, lse_ref, dq_ref, dkv_ref, dgq_ref, dgk_ref, dkn_scr, dv_scr, scr):
        t = pl.program_id(0)

        @pl.when(t == 0)
        def _():
            dkn_scr[...] = jnp.zeros_like(dkn_scr)
            dv_scr[...] = jnp.zeros_like(dv_scr)
            dgq_ref[...] = jnp.zeros_like(dgq_ref)

        kvv = kv_ref[...]
        for pr in range(N_MEM_HEADS // 2):
            lanes = slice(pr * PAIR, (pr + 1) * PAIR)
            k_raw = kvv[:, lanes]
            kn = k_raw * _pair_rms(k_raw) * gk_ref[...]
            v = kvv[:, MQ_COLS + pr * PAIR:MQ_COLS + (pr + 1) * PAIR]
            q_raw = q_ref[:, lanes]
            rq = _pair_rms(q_raw)
            qn = q_raw * rq * gq_ref[...]
            d_o = do_ref[:, lanes]
            dsum = _pair_sum(d_o * ym_ref[:, lanes].astype(F32))
            for hh in range(2):
                hd = 2 * pr + hh
                hs = slice(hh * HEAD_DIM, (hh + 1) * HEAD_DIM)
                cols = slice(hd * HEAD_DIM, (hd + 1) * HEAD_DIM)
                qh = qn[:, hs].astype(BF)
                kh = kn[:, hs].astype(BF)
                d_oh = d_o[:, hs].astype(BF)
                p = jnp.exp(_nt(qh, kh) * SCALE - lse_ref[:, hd:hd + 1])
                dv_scr[:, cols] += _tn(p.astype(BF), d_oh)
                ds = (p * (_nt(d_oh, v[:, hs].astype(BF)) - dsum[:, hh * HEAD_DIM:hh * HEAD_DIM + 1])).astype(BF)
                scr[:, hs] = _nn(ds, kh) * SCALE
                dkn_scr[:, cols] += _tn(ds, qh) * SCALE
            dx, dg = _pair_rms_bwd(scr[...], q_raw, rq, gq_ref[...])
            dq_ref[:, lanes] = dx.astype(BF)
            dgq_ref[...] += dg

        @pl.when(t == nt - 1)
        def _():
            dgk = jnp.zeros((1, PAIR), F32)
            for pr in range(N_MEM_HEADS // 2):
                lanes = slice(pr * PAIR, (pr + 1) * PAIR)
                k_raw = kvv[:, lanes]
                dx, dg = _pair_rms_bwd(dkn_scr[:, lanes], k_raw, _pair_rms(k_raw), gk_ref[...])
                dkv_ref[:, lanes] = dx
                dgk = dgk + dg
            dkv_ref[:, MQ_COLS:2 * MQ_COLS] = dv_scr[...]
            dgk_ref[...] = dgk

    one = pl.BlockSpec((1, PAIR), lambda t: (0, 0))
    kvspec = pl.BlockSpec((m_len, 2 * MQ_COLS), lambda t: (0, 0))
    qspec = pl.BlockSpec((tm, MQ_COLS), lambda t: (t, 0))
    return pl.pallas_call(
        body,
        out_shape=[jax.ShapeDtypeStruct((t_len, MQ_COLS), BF), jax.ShapeDtypeStruct((m_len, 2 * MQ_COLS), F32),
                   jax.ShapeDtypeStruct((1, PAIR), F32), jax.ShapeDtypeStruct((1, PAIR), F32)],
        grid=(nt,),
        in_specs=[pl.BlockSpec((tm, MQ_COLS), lambda t: (t, 5)), kvspec, one, one, qspec,
                  pl.BlockSpec((tm, MQ_COLS), lambda t: (t, 3)), pl.BlockSpec((tm, N_MEM_HEADS), lambda t: (t, 0))],
        out_specs=[qspec, kvspec, one, one],
        scratch_shapes=[pltpu.VMEM((m_len, MQ_COLS), F32), pltpu.VMEM((m_len, MQ_COLS), F32), pltpu.VMEM((tm, PAIR), F32)],
        name=name, compiler_params=_params(1))(proj, mkv, gq2, gk2, ym, dmix, lse)


MIX_GROUPS = [(0, Q_COLS), (Q_COLS, CONV_CH), (Q_COLS + CONV_CH, MQ_COLS)]


def _out_proj_fwd(x, ya, yc, ym, w_out, name):
    t_len = x.shape[0]
    tm = min(512, t_len)

    def body(x_ref, ya_ref, yc_ref, ym_ref, w_ref, xo_ref):
        y = x_ref[...]
        for (start, width), ref in zip(MIX_GROUPS, (ya_ref, yc_ref, ym_ref)):
            y = y + _nn(ref[...], w_ref[start:start + width, :])
        xo_ref[...] = y

    cols = lambda w: pl.BlockSpec((tm, w), lambda t: (t, 0))
    return pl.pallas_call(
        body, out_shape=jax.ShapeDtypeStruct((t_len, D_MODEL), F32), grid=(t_len // tm,),
        in_specs=[cols(D_MODEL), cols(Q_COLS), cols(CONV_CH), cols(MQ_COLS),
                  pl.BlockSpec((D_MODEL, D_MODEL), lambda t: (0, 0))],
        out_specs=cols(D_MODEL), name=name, compiler_params=_params(1))(x, ya, yc, ym, w_out)


def _out_proj_bwd(dx, ya, yc, ym, w_out, name, after=None):
    t_len = dx.shape[0]
    tm = min(512, t_len)
    nt = t_len // tm

    def body(dx_ref, ya_ref, yc_ref, ym_ref, w_ref, dmix_ref, dw_ref, acc):
        t = pl.program_id(0)

        @pl.when(t == 0)
        def _():
            acc[...] = jnp.zeros_like(acc)

        dxb = dx_ref[...].astype(BF)
        dmix_ref[...] = _nt(dxb, w_ref[...])
        for (start, width), ref in zip(MIX_GROUPS, (ya_ref, yc_ref, ym_ref)):
            acc[start:start + width, :] += _tn(ref[...], dxb)

        @pl.when(t == nt - 1)
        def _():
            dw_ref[...] = acc[...].astype(BF)

    cols = lambda w: pl.BlockSpec((tm, w), lambda t: (t, 0))
    full = pl.BlockSpec((D_MODEL, D_MODEL), lambda t: (0, 0))
    body, first, first_specs = _ordered_after(body, after)
    return pl.pallas_call(
        body, out_shape=[jax.ShapeDtypeStruct((t_len, D_MODEL), F32), jax.ShapeDtypeStruct((D_MODEL, D_MODEL), BF)],
        grid=(nt,), in_specs=first_specs + [cols(D_MODEL), cols(Q_COLS), cols(CONV_CH), cols(MQ_COLS), full],
        out_specs=[cols(D_MODEL), full], scratch_shapes=[pltpu.VMEM((D_MODEL, D_MODEL), F32)], name=name,
        compiler_params=_params(1))(*first, dx, ya, yc, ym, w_out)


N_PEERS = N_DEV - 1
HBM_SPEC = pl.BlockSpec(memory_space=pltpu.HBM)
SEM_SPEC = pl.BlockSpec(memory_space=pltpu.SEMAPHORE)
EFFECT = pltpu.SideEffectType.DATAFLOW_SIDE_EFFECTING


def _my_place():
    x, y, c = lax.axis_index("x"), lax.axis_index("y"), lax.axis_index("c")
    return x, y, c, 4 * x + 2 * y + c


def _peer(k):
    x, y, c, _ = _my_place()
    px = 1 - x if k & 4 else x
    py = 1 - y if k & 2 else y
    pc = 1 - c if k & 1 else c
    return (px, py, pc), 4 * px + 2 * py + pc


def _gather_two_level(shards, name):
    n = len(shards)

    def body(*refs):
        ins, outs = refs[:n], refs[n:2 * n]
        send_sems, recv_sems, local_sems = refs[2 * n:]
        x, y, c, me = _my_place()
        sibling = (x, y, 1 - c)
        chips = [(1 - x, y), (x, 1 - y), (1 - x, 1 - y)]
        slot = lambda px, py, pc: 4 * px + 2 * py + pc

        def copy(w, k, src, dst_slot, to):
            return pltpu.make_async_remote_copy(
                src_ref=src, dst_ref=outs[w].at[dst_slot], send_sem=send_sems.at[w * N_PEERS + k],
                recv_sem=recv_sems.at[w * N_PEERS + k], device_id=to, device_id_type=pl.DeviceIdType.MESH)

        local = [pltpu.make_async_copy(ins[w], outs[w].at[me], local_sems.at[w]) for w in range(n)]
        for cp in local:
            cp.start()
        sends = []
        for w in range(n):
            for j, chip in enumerate(chips):
                sends.append(copy(w, 1 + j, ins[w], me, (*chip, c)))
            sends.append(copy(w, 0, ins[w], me, sibling))
        for cp in sends:
            cp.start()
        for w in range(n):
            for j, chip in enumerate(chips):
                got = slot(*chip, c)
                copy(w, 1 + j, ins[w], got, (*chip, c)).wait_recv()
                fwd = copy(w, 4 + j, outs[w].at[got], got, sibling)
                fwd.start()
                sends.append(fwd)
        for w in range(n):
            copy(w, 0, ins[w], slot(x, y, 1 - c), sibling).wait_recv()
            for j, chip in enumerate(chips):
                copy(w, 4 + j, ins[w], slot(*chip, 1 - c), sibling).wait_recv()
        for cp in sends:
            cp.wait_send()
        for cp in local:
            cp.wait()

    hbm = pl.BlockSpec(memory_space=pl.ANY)
    return pl.pallas_call(
        body, out_shape=[jax.ShapeDtypeStruct((N_DEV,) + a.shape, a.dtype) for a in shards], in_specs=[hbm] * n,
        out_specs=[hbm] * n,
        scratch_shapes=[pltpu.SemaphoreType.DMA((n * N_PEERS,)), pltpu.SemaphoreType.DMA((n * N_PEERS,)),
                        pltpu.SemaphoreType.DMA((n,))],
        name=name)(*shards)


def _split_copies(srcs, lands, gather, send_sems, recv_sems):
    _, _, _, me = _my_place()
    pairs = []
    for w in range(len(srcs)):
        for k in range(1, N_DEV):
            dev, idx = _peer(k)
            src = srcs[w] if gather[w] else srcs[w].at[idx]
            sems = dict(send_sem=send_sems.at[w * N_PEERS + k - 1], recv_sem=recv_sems.at[w * N_PEERS + k - 1], device_id=dev,
                        device_id_type=pl.DeviceIdType.MESH)
            pairs.append((pltpu.make_async_remote_copy(src_ref=src, dst_ref=lands[w].at[me], **sems),
                          pltpu.make_async_remote_copy(src_ref=src, dst_ref=lands[w].at[idx], **sems)))
    return pairs


def _exchange_start(srcs, lands, gather, after, name):
    n = len(srcs)

    def body(*refs):
        src_refs, land_refs = refs[:n], refs[n:2 * n]
        send_sems, recv_sems = refs[2 * n + 1], refs[2 * n + 2]
        token = refs[-1]
        for out_going, _ in _split_copies(src_refs, land_refs, gather, send_sems, recv_sems):
            out_going.start()
        token[...] = jnp.zeros_like(token)

    arrays = list(srcs) + list(lands)
    out = pl.pallas_call(
        body, name=name,
        out_shape=(pltpu.SemaphoreType.DMA((n * N_PEERS,)), pltpu.SemaphoreType.DMA((n * N_PEERS,)),
                   *[pltpu.HBM(a.shape, a.dtype) for a in arrays], jax.ShapeDtypeStruct((8, 128), F32)),
        in_specs=[HBM_SPEC] * (2 * n) + [pl.BlockSpec(memory_space=pl.ANY)],
        out_specs=(SEM_SPEC, SEM_SPEC, *[HBM_SPEC] * (2 * n), pl.BlockSpec(memory_space=pltpu.VMEM)),
        input_output_aliases={i: i + 2 for i in range(2 * n)},
        compiler_params=pltpu.CompilerParams(has_side_effects=EFFECT),
    )(*[pltpu.with_memory_space_constraint(a, pltpu.HBM) for a in arrays], after)
    return out[0], out[1], out[2:2 + n], out[2 + n:2 + 2 * n], out[-1]


def _exchange_wait(send_sems, recv_sems, srcs, lands, gather, after, name):
    n = len(srcs)

    def body(*refs):
        src_refs, land_refs = refs[:n], refs[n:2 * n]
        for out_going, arriving in _split_copies(src_refs, land_refs, gather, refs[2 * n], refs[2 * n + 1]):
            out_going.wait_send()
            arriving.wait_recv()

    arrays = list(srcs) + list(lands)
    out = pl.pallas_call(
        body, name=name, out_shape=tuple(pltpu.HBM(a.shape, a.dtype) for a in arrays),
        in_specs=[HBM_SPEC] * (2 * n) + [SEM_SPEC, SEM_SPEC, pl.BlockSpec(memory_space=pl.ANY)],
        out_specs=tuple([HBM_SPEC] * (2 * n)), input_output_aliases={i: i for i in range(2 * n)},
        compiler_params=pltpu.CompilerParams(has_side_effects=EFFECT),
    )(*arrays, send_sems, recv_sems, after)
    return out[n:]


def _own_slot(a, me, gather):
    mine = a if gather else lax.dynamic_index_in_dim(a, me, 0, keepdims=False)
    zone = jnp.zeros((N_DEV,) + mine.shape, mine.dtype)
    return lax.dynamic_update_index_in_dim(zone, mine, me, 0)


def _adamw_math(w, g, m, v):
    m2 = ADAM_B1 * m + (1.0 - ADAM_B1) * g
    v2 = ADAM_B2 * v + (1.0 - ADAM_B2) * (g * g)
    m_hat = m2 / (1.0 - ADAM_B1 ** ADAM_STEP)
    v_hat = v2 / (1.0 - ADAM_B2 ** ADAM_STEP)
    return -ADAM_LR * (m_hat / (jnp.sqrt(v_hat) + ADAM_EPS) + ADAM_WD * w), m2, v2


def _sum_adamw(parts, w, m, v, name):
    rows, cols = w.shape
    tr = rows if rows <= 512 else 256

    def body(p_ref, w_ref, m_ref, v_ref, g_ref, d_ref, m2_ref, v2_ref):
        g = p_ref[0].astype(F32)
        for s in range(1, N_DEV):
            g = g + p_ref[s].astype(F32)
        g_ref[...] = g
        d_ref[...], m2_ref[...], v2_ref[...] = _adamw_math(w_ref[...], g, m_ref[...], v_ref[...])

    blk = pl.BlockSpec((tr, cols), lambda i: (i, 0))
    shape = jax.ShapeDtypeStruct((rows, cols), F32)
    return pl.pallas_call(
        body, out_shape=[shape] * 4, grid=(rows // tr,),
        in_specs=[pl.BlockSpec((N_DEV, tr, cols), lambda i: (0, i, 0)), blk, blk, blk], out_specs=[blk] * 4, name=name,
        compiler_params=_params(1))(parts, w, m, v)


def _slot_sum(parts, name):
    def body(p_ref, g_ref):
        g = p_ref[0]
        for s in range(1, N_DEV):
            g = g + p_ref[s]
        g_ref[...] = g

    return pl.pallas_call(body, out_shape=jax.ShapeDtypeStruct(parts.shape[1:], F32), name=name)(parts)


def _adamw(g, w, m, v, name):
    def body(g_ref, w_ref, m_ref, v_ref, d_ref, m2_ref, v2_ref):
        d_ref[...], m2_ref[...], v2_ref[...] = _adamw_math(w_ref[...], g_ref[...], m_ref[...], v_ref[...])

    shape = jax.ShapeDtypeStruct(w.shape, F32)
    return pl.pallas_call(body, out_shape=[shape] * 3, name=name)(g, w, m, v)


def _pack_small(vals):
    flat = jnp.concatenate([vals[k].reshape(-1) for k in SMALL])
    return jnp.pad(flat, (0, SMALL_ROWS * D_MODEL - flat.shape[0])).reshape(SMALL_ROWS, D_MODEL)


def _unpack_small(packed, like):
    flat, out, at = packed.reshape(-1), {}, 0
    for k in SMALL:
        size = like[k].size
        out[k] = flat[at:at + size].reshape(like[k].shape)
        at += size
    return out


FFN1 = ["w_ffn1_gate", "w_ffn1_up", "w_ffn1_down"]
FFN2 = ["w_ffn2_gate", "w_ffn2_up", "w_ffn2_down"]
MIXER = ["w_in", "w_mem_kv", "w_out"]


def _mixer_fwd(x1, mem, tables, sm, big):
    cos_t, sin_t, gq2, gk2, gmq2, gmk2 = tables
    h2, proj = _in_proj_fwd(x1, sm["g_mix"], big["w_in"], "in_proj_fwd")
    ya, lse = _attn_fwd(proj, cos_t, sin_t, gq2, gk2, sm["sinks"], "attn_fwd")
    yc, cv = _conv_fwd(proj, big["w_dw"], sm["b_dw"], sm["g_conv_ln"], sm["b_conv_ln"], "conv_fwd")
    hm, mkv = _mem_kv_fwd(mem, sm["g_mem"], big["w_mem_kv"], "mem_kv_fwd")
    ym, lse_m = _mem_attn_fwd(proj, mkv, gmq2, gmk2, "mem_attn_fwd")
    x2 = _out_proj_fwd(x1, ya, yc, ym, big["w_out"], "out_proj_fwd")
    return x2, (h2, proj, ya, lse, yc, cv, hm, mkv, ym, lse_m)


def _mixer_bwd(dx2, x1, mem, tables, sm, big, saved, after):
    cos_t, sin_t, gq2, gk2, gmq2, gmk2 = tables
    h2, proj, ya, lse, yc, cv, hm, mkv, ym, lse_m = saved
    g = {}
    dmix, g["w_out"] = _out_proj_bwd(dx2, ya, yc, ym, big["w_out"], "out_proj_bwd", after)
    dq, dgq, g["sinks"] = _attn_bwd_q(proj, cos_t, sin_t, gq2, gk2, sm["sinks"], ya, dmix, lse, "attn_bwd_q")
    dk, dv, dgk = _attn_bwd_kv(proj, cos_t, sin_t, gq2, gk2, ya, dmix, lse, "attn_bwd_kv")
    du, g["w_dw"], g["b_dw"], g["g_conv_ln"], g["b_conv_ln"] = _conv_bwd(
        proj, cv, dmix, big["w_dw"], sm["g_conv_ln"], sm["b_conv_ln"], "conv_bwd")
    dmq, dmkv, dgmq, dgmk = _mem_attn_bwd(proj, mkv, gmq2, gmk2, ym, dmix, lse_m, "mem_attn_bwd")
    g["w_mem_kv"], g["g_mem"] = _mem_kv_bwd(mem, sm["g_mem"], hm, big["w_mem_kv"], dmkv, "mem_kv_bwd")
    dx1, g["g_mix"], g["w_in"] = _in_proj_bwd(dq, dk, dv, du, dmq, big["w_in"], h2, x1, sm["g_mix"], dx2, "in_proj_bwd")
    fold = lambda a: a[:, :HEAD_DIM] + a[:, HEAD_DIM:]
    g["g_q"], g["g_k"], g["g_mq"], g["g_mk"] = fold(dgq), fold(dgk), fold(dgmq), fold(dgmk)
    return dx1, g


def _tables(positions, sm):
    pair = lambda a: jnp.tile(a, (1, 2))
    return _rope_tables(positions) + (pair(sm["g_q"]), pair(sm["g_k"]), pair(sm["g_mq"]), pair(sm["g_mk"]))


def kernel(x, mem, positions, g_ffn1, w_ffn1_gate, w_ffn1_up, w_ffn1_down, g_mix, w_in, g_q, g_k, sinks, w_dw, b_dw, g_conv_ln, b_conv_ln, g_mem, w_mem_kv, g_mq, g_mk, w_out, g_ffn2, w_ffn2_gate, w_ffn2_up, w_ffn2_down, loss_target, m_g_ffn1, m_w_ffn1_gate, m_w_ffn1_up, m_w_ffn1_down, m_g_mix, m_w_in, m_g_q, m_g_k, m_sinks, m_w_dw, m_b_dw, m_g_conv_ln, m_b_conv_ln, m_g_mem, m_w_mem_kv, m_g_mq, m_g_mk, m_w_out, m_g_ffn2, m_w_ffn2_gate, m_w_ffn2_up, m_w_ffn2_down, v_g_ffn1, v_w_ffn1_gate, v_w_ffn1_up, v_w_ffn1_down, v_g_mix, v_w_in, v_g_q, v_g_k, v_sinks, v_w_dw, v_b_dw, v_g_conv_ln, v_b_conv_ln, v_g_mem, v_w_mem_kv, v_g_mq, v_g_mk, v_w_out, v_g_ffn2, v_w_ffn2_gate, v_w_ffn2_up, v_w_ffn2_down):
    w = dict(g_ffn1=g_ffn1, w_ffn1_gate=w_ffn1_gate, w_ffn1_up=w_ffn1_up, w_ffn1_down=w_ffn1_down, g_mix=g_mix, w_in=w_in,
             g_q=g_q, g_k=g_k, sinks=sinks, w_dw=w_dw, b_dw=b_dw, g_conv_ln=g_conv_ln, b_conv_ln=b_conv_ln, g_mem=g_mem,
             w_mem_kv=w_mem_kv, g_mq=g_mq, g_mk=g_mk, w_out=w_out, g_ffn2=g_ffn2, w_ffn2_gate=w_ffn2_gate,
             w_ffn2_up=w_ffn2_up, w_ffn2_down=w_ffn2_down)
    mo = dict(g_ffn1=m_g_ffn1, w_ffn1_gate=m_w_ffn1_gate, w_ffn1_up=m_w_ffn1_up, w_ffn1_down=m_w_ffn1_down, g_mix=m_g_mix,
              w_in=m_w_in, g_q=m_g_q, g_k=m_g_k, sinks=m_sinks, w_dw=m_w_dw, b_dw=m_b_dw, g_conv_ln=m_g_conv_ln,
              b_conv_ln=m_b_conv_ln, g_mem=m_g_mem, w_mem_kv=m_w_mem_kv, g_mq=m_g_mq, g_mk=m_g_mk, w_out=m_w_out,
              g_ffn2=m_g_ffn2, w_ffn2_gate=m_w_ffn2_gate, w_ffn2_up=m_w_ffn2_up, w_ffn2_down=m_w_ffn2_down)
    vo = dict(g_ffn1=v_g_ffn1, w_ffn1_gate=v_w_ffn1_gate, w_ffn1_up=v_w_ffn1_up, w_ffn1_down=v_w_ffn1_down, g_mix=v_g_mix,
              w_in=v_w_in, g_q=v_g_q, g_k=v_g_k, sinks=v_sinks, w_dw=v_w_dw, b_dw=v_b_dw, g_conv_ln=v_g_conv_ln,
              b_conv_ln=v_b_conv_ln, g_mem=v_g_mem, w_mem_kv=v_w_mem_kv, g_mq=v_g_mq, g_mk=v_g_mk, w_out=v_w_out,
              g_ffn2=v_g_ffn2, w_ffn2_gate=v_w_ffn2_gate, w_ffn2_up=v_w_ffn2_up, w_ffn2_down=v_w_ffn2_down)
    me = _my_place()[3]
    sm = {k: w[k] for k in SMALL}
    as_bf16 = lambda names: [w[k][0].astype(BF) for k in names]
    zones = lambda arrays, gather: [_own_slot(a, me, gather) for a in arrays]
    out_g, out_d, out_m, out_v = {}, {}, {}, {}

    def update(names, parts_list):
        for k, parts in zip(names, parts_list):
            g, d, m2, v2 = _sum_adamw(parts, w[k][0], mo[k][0], vo[k][0], "adamw_" + k)
            out_g[k], out_d[k], out_m[k], out_v[k] = g[None], d[None], m2[None], v2[None]

    w1 = _gather_two_level(as_bf16(FFN1), "gather_ffn1")
    mix_src = as_bf16(MIXER) + [w["w_dw"][0]]
    s_mix = _exchange_start(mix_src, zones(mix_src, True), [True] * 4, w1[0], "gather_mixer_start")
    f2_src = as_bf16(FFN2)
    s_f2 = _exchange_start(f2_src, zones(f2_src, True), [True] * 3, s_mix[4], "gather_ffn2_start")

    tables = _tables(positions[0], sm)
    h1, gate1, up1, x1 = _ffn_fwd(x[0], sm["g_ffn1"], *w1, None, "ffn1_fwd", after=s_f2[4])
    got = _exchange_wait(*s_mix[:4], [True] * 4, x1, "gather_mixer_wait")
    big = dict(w_in=got[0].transpose(1, 0, 2).reshape(D_MODEL, IN_COLS), w_mem_kv=got[1].reshape(D_MODEL, 2 * MQ_COLS),
               w_out=got[2].reshape(D_MODEL, D_MODEL), w_dw=got[3].transpose(1, 0, 2).reshape(CONV_WIDTH, CONV_CH))
    x2, saved = _mixer_fwd(x1, mem[0], tables, sm, big)
    w2 = _exchange_wait(*s_f2[:4], [True] * 3, x2, "gather_ffn2_wait")
    h3, gate2, up2, dy, loss_part = _ffn_fwd(x2, sm["g_ffn2"], *w2, loss_target[0], "ffn2_fwd")

    grads = {}
    dyb2, act2, dgate2, dup2, dx2, grads["g_ffn2"] = _ffn_bwd_act(dy, x2, sm["g_ffn2"], gate2, up2, *w2, "ffn2_bwd_act")
    g_f2 = list(_ffn_bwd_w(h3, dyb2, act2, dgate2, dup2, "ffn2_bwd_w"))
    r_f2 = _exchange_start(g_f2, zones(g_f2, False), [False] * 3, dx2, "scatter_ffn2_start")
    dx1, g_mid = _mixer_bwd(dx2, x1, mem[0], tables, sm, big, saved, r_f2[4])
    grads.update(g_mid)
    g_mix = [g_mid["w_in"].reshape(D_MODEL, N_DEV, IN_COLS // N_DEV).transpose(1, 0, 2),
             g_mid["w_mem_kv"].reshape(N_DEV, D_MODEL // N_DEV, 2 * MQ_COLS),
             g_mid["w_out"].reshape(N_DEV, D_MODEL // N_DEV, D_MODEL)]
    r_mix = _exchange_start(g_mix, zones(g_mix, False), [False] * 3, dx1, "scatter_mixer_start")
    dyb1, act1, dgate1, dup1, grad_x, grads["g_ffn1"] = _ffn_bwd_act(dx1, x[0], sm["g_ffn1"], gate1, up1, *w1,
                                                                     "ffn1_bwd_act", after=r_mix[4])
    g_f1 = list(_ffn_bwd_w(h1, dyb1, act1, dgate1, dup1, "ffn1_bwd_w"))
    dw_flat = grads["w_dw"].reshape(-1)
    packed = jnp.concatenate([_pack_small(grads),
                              jnp.pad(dw_flat, (0, SMALL_ROWS * D_MODEL - dw_flat.shape[0])).reshape(SMALL_ROWS, D_MODEL)])
    last_src, last_kind = g_f1 + [packed], [False] * 3 + [True]
    r_f1 = _exchange_start(last_src, zones(g_f1, False) + zones([packed], True), last_kind, grad_x, "scatter_ffn1_start")

    all_done = lambda names: sum(out_d[k][:, :1, :1] for k in names)
    update(FFN2, _exchange_wait(*r_f2[:4], [False] * 3, r_f1[4], "scatter_ffn2_wait"))
    update(MIXER, _exchange_wait(*r_mix[:4], [False] * 3, all_done(FFN2), "scatter_mixer_wait"))
    last = _exchange_wait(*r_f1[:4], last_kind, all_done(MIXER), "scatter_ffn1_wait")
    update(FFN1, last[:3])
    small_sum = _slot_sum(last[3], "small_grad_sum")
    loss_local = loss_part[0, 0]
    g_small = small_sum[:SMALL_ROWS]
    d, m2, v2 = _adamw(g_small, _pack_small(w), _pack_small(mo), _pack_small(vo), "adamw_small")
    for dst, val in ((out_g, g_small), (out_d, d), (out_m, m2), (out_v, v2)):
        dst.update(_unpack_small(val, sm))
    g_dw = small_sum[SMALL_ROWS:].reshape(-1)[:CONV_WIDTH * CONV_CH].reshape(CONV_WIDTH, CONV_CH)
    g_dw = lax.dynamic_slice_in_dim(g_dw, me * (CONV_CH // N_DEV), CONV_CH // N_DEV, axis=1)
    d, m2, v2 = _adamw(g_dw, w["w_dw"][0], mo["w_dw"][0], vo["w_dw"][0], "adamw_w_dw")
    out_g["w_dw"], out_d["w_dw"], out_m["w_dw"], out_v["w_dw"] = g_dw[None], d[None], m2[None], v2[None]

    loss = lax.psum(loss_local, ("x", "y", "c"))
    return (loss, grad_x[None], *[out_g[k] for k in WEIGHTS], *[out_d[k] for k in WEIGHTS], *[out_m[k] for k in WEIGHTS],
            *[out_v[k] for k in WEIGHTS])
```

```python
import jax
import jax.numpy as jnp
from jax import lax
from jax.experimental import pallas as pl
from jax.experimental.pallas import tpu as pltpu

D_MODEL = 1024
N_DEV = 8
FF_BLOCK = 352
HEAD_DIM = 64
PAIR = 2 * HEAD_DIM
N_Q_HEADS = 8
Q_PER_KV = 4
ATTN_BLOCK = 128
Q_COLS = 512
KV_COLS = 128
CONV_CH = 256
MQ_COLS = 256
IN_COLS = 1536
CONV_WIDTH = 31
CONV_PAD = 32
CONV_CHUNK = 256
N_MEM_HEADS = 4
ROPE_THETA = 500000.0
ROPE_HALF = 8
EPS = 1e-6
SCALE = HEAD_DIM ** -0.5
NEG = -1e30
ADAM_LR, ADAM_B1, ADAM_B2, ADAM_EPS, ADAM_WD, ADAM_STEP = 0.001, 0.9, 0.999, 1e-08, 0.01, 10
VMEM_LIMIT_BYTES = 56 * 1024 * 1024
BF = jnp.bfloat16
F32 = jnp.float32

SMALL = ["g_ffn1", "g_mix", "g_mem", "g_ffn2", "g_q", "g_k", "g_mq", "g_mk", "sinks", "b_dw", "g_conv_ln", "b_conv_ln"]
WEIGHTS = ["g_ffn1", "w_ffn1_gate", "w_ffn1_up", "w_ffn1_down", "g_mix", "w_in", "g_q", "g_k", "sinks", "w_dw", "b_dw",
           "g_conv_ln", "b_conv_ln", "g_mem", "w_mem_kv", "g_mq", "g_mk", "w_out", "g_ffn2", "w_ffn2_gate", "w_ffn2_up",
           "w_ffn2_down"]
SMALL_ROWS = 8
PACK_ROWS = 16


def _nn(a, b):
    return jnp.dot(a, b, preferred_element_type=F32)


def _nt(a, b):
    return lax.dot_general(a, b, (((1,), (1,)), ((), ())), preferred_element_type=F32)


def _tn(a, b):
    return lax.dot_general(a, b, (((0,), (0,)), ((), ())), preferred_element_type=F32)


def _params(n_grid):
    return pltpu.CompilerParams(dimension_semantics=("arbitrary",) * n_grid, vmem_limit_bytes=VMEM_LIMIT_BYTES)


def _ordered_after(body, after):
    if after is None:
        return body, (), []

    def body_after(after_ref, *refs):
        body(*refs)

    return body_after, (after,), [pl.BlockSpec(memory_space=pl.ANY)]


def _row_rms(xv):
    return lax.rsqrt(jnp.mean(xv * xv, axis=-1, keepdims=True) + EPS)


def _rms_bwd(dh, xv, r, g):
    u = dh * g
    dx = r * u - xv * (r * r * r) * jnp.mean(u * xv, axis=-1, keepdims=True)
    return dx, jnp.sum(dh * xv * r, axis=0, keepdims=True)


def _sum_all(a):
    return jnp.sum(jnp.sum(a, axis=1, keepdims=True), axis=0, keepdims=True)


def _lane_lo(shape):
    return lax.broadcasted_iota(jnp.int32, shape, 1) < HEAD_DIM


def _pair_sum(v):
    lo = _lane_lo(v.shape)
    s_lo = jnp.sum(jnp.where(lo, v, 0.0), axis=-1, keepdims=True)
    s_hi = jnp.sum(jnp.where(lo, 0.0, v), axis=-1, keepdims=True)
    return jnp.where(lo, s_lo, s_hi)


def _pair_rms(xv):
    return lax.rsqrt(_pair_sum(xv * xv) * (1.0 / HEAD_DIM) + EPS)


def _pair_rms_bwd(dxn, xv, r, g):
    u = dxn * g
    dx = r * u - xv * (r * r * r) * (_pair_sum(u * xv) * (1.0 / HEAD_DIM))
    return dx, jnp.sum(dxn * xv * r, axis=0, keepdims=True)


def _rope_mask(shape):
    lane = lax.broadcasted_iota(jnp.int32, shape, 1)
    return ((lane & (HEAD_DIM - 1)) < 2 * ROPE_HALF).astype(F32)


def _partner(v):
    return pltpu.roll(v, ROPE_HALF, 1) + pltpu.roll(v, PAIR - ROPE_HALF, 1)


def _rope(xn, cos_t, sin_t):
    return xn * cos_t + _partner(xn * _rope_mask(xn.shape)) * sin_t


def _rope_t(d, cos_t, sin_t):
    return d * cos_t + _partner(d * sin_t) * _rope_mask(d.shape)


def _rope_tables(positions):
    inv_freq = ROPE_THETA ** (-jnp.arange(ROPE_HALF, dtype=F32) / ROPE_HALF)
    ang = positions.astype(F32)[:, None] * inv_freq
    cos, sin = jnp.cos(ang), jnp.sin(ang)
    t = positions.shape[0]
    cos_h = jnp.concatenate([cos, cos, jnp.ones((t, HEAD_DIM - 2 * ROPE_HALF), F32)], axis=1)
    sin_h = jnp.concatenate([-sin, sin, jnp.zeros((t, HEAD_DIM - 2 * ROPE_HALF), F32)], axis=1)
    return jnp.tile(cos_h, (1, 2)), jnp.tile(sin_h, (1, 2))


def _ffn_fwd(x, g, wg, wu, wd, target, name, after=None):
    t_len = x.shape[0]
    tm = min(512, t_len)
    with_loss = target is not None

    def body(*refs):
        if with_loss:
            x_ref, g_ref, wg_ref, wu_ref, wd_ref, t_ref, h_ref, gg_ref, uu_ref, dy_ref, loss_ref, acc = refs
        else:
            x_ref, g_ref, wg_ref, wu_ref, wd_ref, h_ref, gg_ref, uu_ref, xo_ref, acc = refs
        t, j = pl.program_id(0), pl.program_id(1)

        @pl.when(j == 0)
        def _():
            xv = x_ref[...]
            h_ref[...] = (xv * _row_rms(xv) * g_ref[...]).astype(BF)
            acc[...] = jnp.zeros_like(acc)

        h = h_ref[...]
        gate = _nt(h, wg_ref[0])
        up = _nt(h, wu_ref[0])
        gg_ref[0] = gate.astype(BF)
        uu_ref[0] = up.astype(BF)
        act = (gate * jax.nn.sigmoid(gate) * up).astype(BF)
        acc[...] += _nn(act, wd_ref[0])

        @pl.when(j == N_DEV - 1)
        def _():
            xo = x_ref[...] + 0.5 * acc[...]
            if with_loss:
                err = xo - t_ref[...]
                dy_ref[...] = err * (1.0 / D_MODEL)

                @pl.when(t == 0)
                def _():
                    loss_ref[...] = jnp.zeros_like(loss_ref)

                loss_ref[...] += _sum_all(err * err) * (0.5 / D_MODEL)
            else:
                xo_ref[...] = xo

    row = pl.BlockSpec((tm, D_MODEL), lambda t, j: (t, 0))
    vec = pl.BlockSpec((1, D_MODEL), lambda t, j: (0, 0))
    w_in_spec = pl.BlockSpec((1, FF_BLOCK, D_MODEL), lambda t, j: (j, 0, 0))
    w_out_spec = pl.BlockSpec((1, FF_BLOCK, D_MODEL), lambda t, j: (j, 0, 0))
    blk = pl.BlockSpec((1, tm, FF_BLOCK), lambda t, j: (j, t, 0))
    in_specs = [row, vec, w_in_spec, w_in_spec, w_out_spec] + ([row] if with_loss else [])
    out_shape = [jax.ShapeDtypeStruct((t_len, D_MODEL), BF),
                 jax.ShapeDtypeStruct((N_DEV, t_len, FF_BLOCK), BF),
                 jax.ShapeDtypeStruct((N_DEV, t_len, FF_BLOCK), BF),
                 jax.ShapeDtypeStruct((t_len, D_MODEL), F32)]
    out_specs = [row, blk, blk, row]
    if with_loss:
        out_shape.append(jax.ShapeDtypeStruct((1, 128), F32))
        out_specs.append(pl.BlockSpec((1, 128), lambda t, j: (0, 0)))
    args = (x, g, wg, wu, wd) + ((target,) if with_loss else ())
    body, first, first_specs = _ordered_after(body, after)
    return pl.pallas_call(body, out_shape=out_shape, grid=(t_len // tm, N_DEV), in_specs=first_specs + in_specs,
                          out_specs=out_specs, scratch_shapes=[pltpu.VMEM((tm, D_MODEL), F32)], name=name,
                          compiler_params=_params(2))(*first, *args)


def _ffn_bwd_act(dy, x, g, gate, up, wg, wu, wd, name, after=None):
    t_len = x.shape[0]
    tm = min(512, t_len)

    def body(dy_ref, x_ref, g_ref, gg_ref, uu_ref, wg_ref, wu_ref, wd_ref,
             dyb_ref, act_ref, dgg_ref, duu_ref, dx_ref, dg_ref, acc):
        t, j = pl.program_id(0), pl.program_id(1)

        @pl.when(j == 0)
        def _():
            dyb_ref[...] = (0.5 * dy_ref[...]).astype(BF)
            acc[...] = jnp.zeros_like(acc)

        d_act = _nt(dyb_ref[...], wd_ref[0])
        gate = gg_ref[0].astype(F32)
        upv = uu_ref[0].astype(F32)
        sig = jax.nn.sigmoid(gate)
        silu = gate * sig
        d_up = (d_act * silu).astype(BF)
        d_gate = (d_act * upv * (sig * (1.0 + gate * (1.0 - sig)))).astype(BF)
        act_ref[0] = (silu * upv).astype(BF)
        dgg_ref[0] = d_gate
        duu_ref[0] = d_up
        acc[...] += _nn(d_gate, wg_ref[0]) + _nn(d_up, wu_ref[0])

        @pl.when(j == N_DEV - 1)
        def _():
            xv = x_ref[...]
            dx, dg = _rms_bwd(acc[...], xv, _row_rms(xv), g_ref[...])
            dx_ref[...] = dy_ref[...] + dx

            @pl.when(t == 0)
            def _():
                dg_ref[...] = jnp.zeros_like(dg_ref)

            dg_ref[...] += dg

    row = pl.BlockSpec((tm, D_MODEL), lambda t, j: (t, 0))
    vec = pl.BlockSpec((1, D_MODEL), lambda t, j: (0, 0))
    w_in_spec = pl.BlockSpec((1, FF_BLOCK, D_MODEL), lambda t, j: (j, 0, 0))
    w_out_spec = pl.BlockSpec((1, FF_BLOCK, D_MODEL), lambda t, j: (j, 0, 0))
    blk = pl.BlockSpec((1, tm, FF_BLOCK), lambda t, j: (j, t, 0))
    blk_shape = jax.ShapeDtypeStruct((N_DEV, t_len, FF_BLOCK), BF)
    body, first, first_specs = _ordered_after(body, after)
    return pl.pallas_call(
        body,
        out_shape=[jax.ShapeDtypeStruct((t_len, D_MODEL), BF), blk_shape, blk_shape, blk_shape,
                   jax.ShapeDtypeStruct((t_len, D_MODEL), F32), jax.ShapeDtypeStruct((1, D_MODEL), F32)],
        grid=(t_len // tm, N_DEV), in_specs=first_specs + [row, row, vec, blk, blk, w_in_spec, w_in_spec, w_out_spec],
        out_specs=[row, blk, blk, blk, row, vec], scratch_shapes=[pltpu.VMEM((tm, D_MODEL), F32)], name=name,
        compiler_params=_params(2))(*first, dy, x, g, gate, up, wg, wu, wd)


def _ffn_bwd_w(h, dyb, act, d_gate, d_up, name):
    t_len = h.shape[0]
    tm = min(512, t_len)
    nt = t_len // tm

    def body(h_ref, dyb_ref, act_ref, dgg_ref, duu_ref, dwg_ref, dwu_ref, dwd_ref, ag, au, ad):
        t = pl.program_id(1)

        @pl.when(t == 0)
        def _():
            ag[...] = jnp.zeros_like(ag)
            au[...] = jnp.zeros_like(au)
            ad[...] = jnp.zeros_like(ad)

        h = h_ref[...]
        ag[...] += _tn(dgg_ref[0], h)
        au[...] += _tn(duu_ref[0], h)
        ad[...] += _tn(act_ref[0], dyb_ref[...])

        @pl.when(t == nt - 1)
        def _():
            dwg_ref[0] = ag[...].astype(BF)
            dwu_ref[0] = au[...].astype(BF)
            dwd_ref[0] = ad[...].astype(BF)

    row = pl.BlockSpec((tm, D_MODEL), lambda j, t: (t, 0))
    blk = pl.BlockSpec((1, tm, FF_BLOCK), lambda j, t: (j, t, 0))
    w_in_spec = pl.BlockSpec((1, FF_BLOCK, D_MODEL), lambda j, t: (j, 0, 0))
    w_out_spec = pl.BlockSpec((1, FF_BLOCK, D_MODEL), lambda j, t: (j, 0, 0))
    return pl.pallas_call(
        body,
        out_shape=[jax.ShapeDtypeStruct((N_DEV, FF_BLOCK, D_MODEL), BF)] * 3,
        grid=(N_DEV, nt), in_specs=[row, row, blk, blk, blk], out_specs=[w_in_spec, w_in_spec, w_out_spec],
        scratch_shapes=[pltpu.VMEM((FF_BLOCK, D_MODEL), F32)] * 3,
        name=name, compiler_params=_params(2))(h, dyb, act, d_gate, d_up)


def _in_proj_fwd(x, g, w_in, name):
    t_len = x.shape[0]
    tm = min(512, t_len)

    def body(x_ref, g_ref, w_ref, h_ref, p_ref):
        xv = x_ref[...]
        h = (xv * _row_rms(xv) * g_ref[...]).astype(BF)
        h_ref[...] = h
        p_ref[...] = _nt(h, w_ref[...])

    row = pl.BlockSpec((tm, D_MODEL), lambda t: (t, 0))
    return pl.pallas_call(
        body, out_shape=[jax.ShapeDtypeStruct((t_len, D_MODEL), BF), jax.ShapeDtypeStruct((t_len, IN_COLS), F32)],
        grid=(t_len // tm,),
        in_specs=[row, pl.BlockSpec((1, D_MODEL), lambda t: (0, 0)), pl.BlockSpec((IN_COLS, D_MODEL), lambda t: (0, 0))],
        out_specs=[row, pl.BlockSpec((tm, IN_COLS), lambda t: (t, 0))], name=name, compiler_params=_params(1))(x, g, w_in)


def _in_proj_bwd(dq, dk, dv, du, dmq, w_in, h, x, g, dres, name):
    t_len = x.shape[0]
    tm = min(512, t_len)
    nt = t_len // tm
    groups = [(0, Q_COLS), (Q_COLS, KV_COLS), (Q_COLS + KV_COLS, KV_COLS), (Q_COLS + 2 * KV_COLS, 2 * CONV_CH),
              (Q_COLS + 2 * KV_COLS + 2 * CONV_CH, MQ_COLS)]

    def body(dq_ref, dk_ref, dv_ref, du_ref, dmq_ref, w_ref, h_ref, x_ref, g_ref, dres_ref, dx_ref, dg_ref, dw_ref, acc):
        t = pl.program_id(0)

        @pl.when(t == 0)
        def _():
            acc[...] = jnp.zeros_like(acc)
            dg_ref[...] = jnp.zeros_like(dg_ref)

        h = h_ref[...]
        dh = jnp.zeros((tm, D_MODEL), F32)
        for (start, width), ref in zip(groups, (dq_ref, dk_ref, dv_ref, du_ref, dmq_ref)):
            piece = ref[...]
            dh = dh + _nn(piece, w_ref[start:start + width, :])
            acc[start:start + width, :] += _tn(piece, h)
        xv = x_ref[...]
        dx, dg = _rms_bwd(dh, xv, _row_rms(xv), g_ref[...])
        dx_ref[...] = dres_ref[...] + dx
        dg_ref[...] += dg

        @pl.when(t == nt - 1)
        def _():
            dw_ref[...] = acc[...].astype(BF)

    def cols(width):
        return pl.BlockSpec((tm, width), lambda t: (t, 0))

    row = cols(D_MODEL)
    vec = pl.BlockSpec((1, D_MODEL), lambda t: (0, 0))
    full = pl.BlockSpec((IN_COLS, D_MODEL), lambda t: (0, 0))
    return pl.pallas_call(
        body,
        out_shape=[jax.ShapeDtypeStruct((t_len, D_MODEL), F32), jax.ShapeDtypeStruct((1, D_MODEL), F32),
                   jax.ShapeDtypeStruct((IN_COLS, D_MODEL), BF)],
        grid=(nt,),
        in_specs=[cols(Q_COLS), cols(KV_COLS), cols(KV_COLS), cols(2 * CONV_CH), cols(MQ_COLS), full, row, row, vec, row],
        out_specs=[row, vec, full], scratch_shapes=[pltpu.VMEM((IN_COLS, D_MODEL), F32)], name=name,
        compiler_params=_params(1))(dq, dk, dv, du, dmq, w_in, h, x, g, dres)


def _attn_fwd(proj, cos_t, sin_t, gq2, gk2, sinks, name):
    t_len = proj.shape[0]
    nb = t_len // ATTN_BLOCK

    def body(q_ref, kvc_ref, kvp_ref, cq_ref, sq_ref, cp_ref, sp_ref, gq_ref, gk_ref, sk_ref, ya_ref, lse_ref):
        n = pl.program_id(0)
        cq, sq = cq_ref[...], sq_ref[...]
        c_all = jnp.concatenate([cp_ref[...], cq], axis=0)
        s_all = jnp.concatenate([sp_ref[...], sq], axis=0)
        kv = jnp.concatenate([kvp_ref[...], kvc_ref[...]], axis=0)
        k_raw, v = kv[:, :PAIR], kv[:, PAIR:]
        kn = _rope(k_raw * _pair_rms(k_raw) * gk_ref[...], c_all, s_all)
        row = lax.broadcasted_iota(jnp.int32, (ATTN_BLOCK, 2 * ATTN_BLOCK), 0)
        col = lax.broadcasted_iota(jnp.int32, (ATTN_BLOCK, 2 * ATTN_BLOCK), 1)
        rel = row + ATTN_BLOCK - col
        valid = (rel >= 0) & (rel < ATTN_BLOCK) & ((col >= ATTN_BLOCK) | (n > 0))
        for pr in range(N_Q_HEADS // 2):
            q_raw = q_ref[:, pr * PAIR:(pr + 1) * PAIR]
            qn = _rope(q_raw * _pair_rms(q_raw) * gq_ref[...], cq, sq)
            kvh = (2 * pr) // Q_PER_KV
            kh = kn[:, kvh * HEAD_DIM:(kvh + 1) * HEAD_DIM].astype(BF)
            vh = v[:, kvh * HEAD_DIM:(kvh + 1) * HEAD_DIM].astype(BF)
            for hh in range(2):
                hd = 2 * pr + hh
                qh = qn[:, hh * HEAD_DIM:(hh + 1) * HEAD_DIM].astype(BF)
                s = jnp.where(valid, _nt(qh, kh) * SCALE, NEG)
                sink = sk_ref[0:1, hd:hd + 1]
                m = jnp.maximum(jnp.max(s, axis=-1, keepdims=True), sink)
                p = jnp.exp(s - m)
                den = jnp.sum(p, axis=-1, keepdims=True) + jnp.exp(sink - m)
                ya_ref[:, hd * HEAD_DIM:(hd + 1) * HEAD_DIM] = _nn((p / den).astype(BF), vh).astype(BF)
                lse_ref[:, hd:hd + 1] = m + jnp.log(den)

    prev = lambda n: (jnp.maximum(n - 1, 0), 0)
    tab = pl.BlockSpec((ATTN_BLOCK, PAIR), lambda n: (n, 0))
    tab_p = pl.BlockSpec((ATTN_BLOCK, PAIR), prev)
    one = lambda w: pl.BlockSpec((1, w), lambda n: (0, 0))
    return pl.pallas_call(
        body, out_shape=[jax.ShapeDtypeStruct((t_len, Q_COLS), BF), jax.ShapeDtypeStruct((t_len, N_Q_HEADS), F32)],
        grid=(nb,),
        in_specs=[pl.BlockSpec((ATTN_BLOCK, Q_COLS), lambda n: (n, 0)),
                  pl.BlockSpec((ATTN_BLOCK, 2 * KV_COLS), lambda n: (n, 2)),
                  pl.BlockSpec((ATTN_BLOCK, 2 * KV_COLS), lambda n: (jnp.maximum(n - 1, 0), 2)),
                  tab, tab, tab_p, tab_p, one(PAIR), one(PAIR), one(N_Q_HEADS)],
        out_specs=[pl.BlockSpec((ATTN_BLOCK, Q_COLS), lambda n: (n, 0)),
                   pl.BlockSpec((ATTN_BLOCK, N_Q_HEADS), lambda n: (n, 0))],
        name=name, compiler_params=_params(1))(proj, proj, proj, cos_t, sin_t, cos_t, sin_t, gq2, gk2, sinks)


def _attn_bwd_q(proj, cos_t, sin_t, gq2, gk2, sinks, ya, dmix, lse, name):
    t_len = proj.shape[0]
    nb = t_len // ATTN_BLOCK

    def body(q_ref, kvc_ref, kvp_ref, cq_ref, sq_ref, cp_ref, sp_ref, gq_ref, gk_ref, sk_ref, ya_ref, do_ref, lse_ref,
             dq_ref, dgq_ref, dsk_ref, scr):
        n = pl.program_id(0)

        @pl.when(n == 0)
        def _():
            dgq_ref[...] = jnp.zeros_like(dgq_ref)
            dsk_ref[...] = jnp.zeros_like(dsk_ref)

        cq, sq = cq_ref[...], sq_ref[...]
        c_all = jnp.concatenate([cp_ref[...], cq], axis=0)
        s_all = jnp.concatenate([sp_ref[...], sq], axis=0)
        kv = jnp.concatenate([kvp_ref[...], kvc_ref[...]], axis=0)
        k_raw, v = kv[:, :PAIR], kv[:, PAIR:]
        kn = _rope(k_raw * _pair_rms(k_raw) * gk_ref[...], c_all, s_all)
        row = lax.broadcasted_iota(jnp.int32, (ATTN_BLOCK, 2 * ATTN_BLOCK), 0)
        col = lax.broadcasted_iota(jnp.int32, (ATTN_BLOCK, 2 * ATTN_BLOCK), 1)
        rel = row + ATTN_BLOCK - col
        valid = (rel >= 0) & (rel < ATTN_BLOCK) & ((col >= ATTN_BLOCK) | (n > 0))
        gq = gq_ref[...]
        for pr in range(N_Q_HEADS // 2):
            q_raw = q_ref[:, pr * PAIR:(pr + 1) * PAIR]
            rq = _pair_rms(q_raw)
            qn = _rope(q_raw * rq * gq, cq, sq)
            kvh = (2 * pr) // Q_PER_KV
            kh = kn[:, kvh * HEAD_DIM:(kvh + 1) * HEAD_DIM].astype(BF)
            vh = v[:, kvh * HEAD_DIM:(kvh + 1) * HEAD_DIM].astype(BF)
            for hh in range(2):
                hd = 2 * pr + hh
                qh = qn[:, hh * HEAD_DIM:(hh + 1) * HEAD_DIM].astype(BF)
                lse_h = lse_ref[:, hd:hd + 1]
                p = jnp.exp(jnp.where(valid, _nt(qh, kh) * SCALE, NEG) - lse_h)
                d_o = do_ref[:, hd * HEAD_DIM:(hd + 1) * HEAD_DIM]
                o = ya_ref[:, hd * HEAD_DIM:(hd + 1) * HEAD_DIM].astype(F32)
                dsum = jnp.sum(d_o * o, axis=-1, keepdims=True)
                ds = p * (_nt(d_o.astype(BF), vh) - dsum)
                scr[:, hh * HEAD_DIM:(hh + 1) * HEAD_DIM] = _nn(ds.astype(BF), kh) * SCALE
                sink = sk_ref[0:1, hd:hd + 1]
                dsk_ref[0:1, hd:hd + 1] += -jnp.sum(jnp.exp(sink - lse_h) * dsum, axis=0, keepdims=True)
            dx, dg = _pair_rms_bwd(_rope_t(scr[...], cq, sq), q_raw, rq, gq)
            dq_ref[:, pr * PAIR:(pr + 1) * PAIR] = dx.astype(BF)
            dgq_ref[...] += dg

    prev = lambda n: (jnp.maximum(n - 1, 0), 0)
    tab = pl.BlockSpec((ATTN_BLOCK, PAIR), lambda n: (n, 0))
    tab_p = pl.BlockSpec((ATTN_BLOCK, PAIR), prev)
    one = lambda w: pl.BlockSpec((1, w), lambda n: (0, 0))
    qblk = pl.BlockSpec((ATTN_BLOCK, Q_COLS), lambda n: (n, 0))
    return pl.pallas_call(
        body,
        out_shape=[jax.ShapeDtypeStruct((t_len, Q_COLS), BF), jax.ShapeDtypeStruct((1, PAIR), F32),
                   jax.ShapeDtypeStruct((1, N_Q_HEADS), F32)],
        grid=(nb,),
        in_specs=[qblk, pl.BlockSpec((ATTN_BLOCK, 2 * KV_COLS), lambda n: (n, 2)),
                  pl.BlockSpec((ATTN_BLOCK, 2 * KV_COLS), lambda n: (jnp.maximum(n - 1, 0), 2)),
                  tab, tab, tab_p, tab_p, one(PAIR), one(PAIR), one(N_Q_HEADS), qblk, qblk,
                  pl.BlockSpec((ATTN_BLOCK, N_Q_HEADS), lambda n: (n, 0))],
        out_specs=[qblk, one(PAIR), one(N_Q_HEADS)], scratch_shapes=[pltpu.VMEM((ATTN_BLOCK, PAIR), F32)],
        name=name, compiler_params=_params(1))(proj, proj, proj, cos_t, sin_t, cos_t, sin_t, gq2, gk2, sinks, ya, dmix, lse)


def _attn_bwd_kv(proj, cos_t, sin_t, gq2, gk2, ya, dmix, lse, name):
    t_len = proj.shape[0]
    nb = t_len // ATTN_BLOCK

    def body(kv_ref, q0_ref, q1_ref, ck_ref, sk_ref, c1_ref, s1_ref, gq_ref, gk_ref, o0_ref, o1_ref, do0_ref, do1_ref,
             l0_ref, l1_ref, dk_ref, dv_ref, dgk_ref, dkn_scr, dv_scr):
        m = pl.program_id(0)

        @pl.when(m == 0)
        def _():
            dgk_ref[...] = jnp.zeros_like(dgk_ref)

        ck, sk = ck_ref[...], sk_ref[...]
        c_all = jnp.concatenate([ck, c1_ref[...]], axis=0)
        s_all = jnp.concatenate([sk, s1_ref[...]], axis=0)
        kvv = kv_ref[...]
        k_raw, v = kvv[:, :PAIR], kvv[:, PAIR:]
        rk = _pair_rms(k_raw)
        gk = gk_ref[...]
        kn = _rope(k_raw * rk * gk, ck, sk)
        row = lax.broadcasted_iota(jnp.int32, (2 * ATTN_BLOCK, ATTN_BLOCK), 0)
        col = lax.broadcasted_iota(jnp.int32, (2 * ATTN_BLOCK, ATTN_BLOCK), 1)
        valid = ((row < ATTN_BLOCK) & (row >= col)) | ((row >= ATTN_BLOCK) & (row - ATTN_BLOCK < col) & (m < nb - 1))
        lse2 = jnp.concatenate([l0_ref[...], l1_ref[...]], axis=0)
        dkn_scr[...] = jnp.zeros_like(dkn_scr)
        dv_scr[...] = jnp.zeros_like(dv_scr)
        for pr in range(N_Q_HEADS // 2):
            lanes = slice(pr * PAIR, (pr + 1) * PAIR)
            q_raw = jnp.concatenate([q0_ref[:, lanes], q1_ref[:, lanes]], axis=0)
            qn = _rope(q_raw * _pair_rms(q_raw) * gq_ref[...], c_all, s_all)
            d_o2 = jnp.concatenate([do0_ref[:, lanes], do1_ref[:, lanes]], axis=0)
            o2 = jnp.concatenate([o0_ref[:, lanes], o1_ref[:, lanes]], axis=0).astype(F32)
            dsum2 = _pair_sum(d_o2 * o2)
            kvh = (2 * pr) // Q_PER_KV
            khs = slice(kvh * HEAD_DIM, (kvh + 1) * HEAD_DIM)
            kh = kn[:, khs].astype(BF)
            vh = v[:, khs].astype(BF)
            for hh in range(2):
                hd = 2 * pr + hh
                hs = slice(hh * HEAD_DIM, (hh + 1) * HEAD_DIM)
                qh = qn[:, hs].astype(BF)
                d_oh = d_o2[:, hs].astype(BF)
                p = jnp.exp(jnp.where(valid, _nt(qh, kh) * SCALE, NEG) - lse2[:, hd:hd + 1])
                dv_scr[:, khs] += _tn(p.astype(BF), d_oh)
                ds = p * (_nt(d_oh, vh) - dsum2[:, hh * HEAD_DIM:hh * HEAD_DIM + 1])
                dkn_scr[:, khs] += _tn(ds.astype(BF), qh) * SCALE
        dx, dg = _pair_rms_bwd(_rope_t(dkn_scr[...], ck, sk), k_raw, rk, gk)
        dk_ref[...] = dx.astype(BF)
        dv_ref[...] = dv_scr[...].astype(BF)
        dgk_ref[...] += dg

    nxt = lambda m: (jnp.minimum(m + 1, nb - 1), 0)
    cur = lambda m: (m, 0)
    tab = lambda f: pl.BlockSpec((ATTN_BLOCK, PAIR), f)
    qb = lambda f: pl.BlockSpec((ATTN_BLOCK, Q_COLS), f)
    lb = lambda f: pl.BlockSpec((ATTN_BLOCK, N_Q_HEADS), f)
    one = pl.BlockSpec((1, PAIR), lambda m: (0, 0))
    return pl.pallas_call(
        body,
        out_shape=[jax.ShapeDtypeStruct((t_len, KV_COLS), BF), jax.ShapeDtypeStruct((t_len, KV_COLS), BF),
                   jax.ShapeDtypeStruct((1, PAIR), F32)],
        grid=(nb,),
        in_specs=[pl.BlockSpec((ATTN_BLOCK, 2 * KV_COLS), lambda m: (m, 2)), qb(cur), qb(nxt), tab(cur), tab(cur), tab(nxt),
                  tab(nxt), one, one, qb(cur), qb(nxt), qb(cur), qb(nxt), lb(cur), lb(nxt)],
        out_specs=[tab(cur), tab(cur), one],
        scratch_shapes=[pltpu.VMEM((ATTN_BLOCK, PAIR), F32), pltpu.VMEM((ATTN_BLOCK, PAIR), F32)],
        name=name, compiler_params=_params(1))(proj, proj, proj, cos_t, sin_t, cos_t, sin_t, gq2, gk2, ya, ya, dmix, dmix,
                                               lse, lse)


def _conv_taps(blk, w, offset_of):
    acc = jnp.zeros((CONV_CHUNK, CONV_CH), F32)
    for k in range(CONV_WIDTH):
        o = offset_of(k)
        acc = acc + w[k:k + 1, :] * blk[o:o + CONV_CHUNK, :]
    return acc


def _conv_fwd(proj, w_dw, b_dw, g_ln, b_ln, name):
    t_len = proj.shape[0]
    nc = t_len // CONV_CHUNK

    def body(a_ref, gt_ref, w_ref, b_ref, g_ref, bl_ref, yc_ref, cv_ref, pad):
        pad[0:CONV_PAD, :] = jnp.zeros((CONV_PAD, CONV_CH), F32)

        def glu(c, carry):
            rows = pl.ds(pl.multiple_of(c * CONV_CHUNK, CONV_CHUNK), CONV_CHUNK)
            dst = pl.ds(pl.multiple_of(c * CONV_CHUNK + CONV_PAD, CONV_PAD), CONV_CHUNK)
            pad[dst, :] = a_ref[rows, :] * jax.nn.sigmoid(gt_ref[rows, :])
            return carry

        lax.fori_loop(0, nc, glu, 0)
        w = w_ref[...]

        def conv(c, carry):
            base = pl.multiple_of(c * CONV_CHUNK, CONV_CHUNK)
            blk = pad[pl.ds(base, CONV_CHUNK + CONV_PAD), :]
            cv = _conv_taps(blk, w, lambda k: k + CONV_PAD - (CONV_WIDTH - 1)) + b_ref[...]
            mu = jnp.mean(cv, axis=-1, keepdims=True)
            xc = cv - mu
            z = xc * lax.rsqrt(jnp.mean(xc * xc, axis=-1, keepdims=True) + EPS) * g_ref[...] + bl_ref[...]
            rows = pl.ds(base, CONV_CHUNK)
            cv_ref[rows, :] = cv
            yc_ref[rows, :] = (z * jax.nn.sigmoid(z)).astype(BF)
            return carry

        lax.fori_loop(0, nc, conv, 0)

    vec = pl.BlockSpec((1, CONV_CH), lambda i: (0, 0))
    full = pl.BlockSpec((t_len, CONV_CH), lambda i: (0, 0))
    return pl.pallas_call(
        body, out_shape=[jax.ShapeDtypeStruct((t_len, CONV_CH), BF), jax.ShapeDtypeStruct((t_len, CONV_CH), F32)],
        grid=(1,),
        in_specs=[pl.BlockSpec((t_len, CONV_CH), lambda i: (0, 3)), pl.BlockSpec((t_len, CONV_CH), lambda i: (0, 4)),
                  pl.BlockSpec((CONV_WIDTH, CONV_CH), lambda i: (0, 0)), vec, vec, vec],
        out_specs=[full, full], scratch_shapes=[pltpu.VMEM((t_len + CONV_PAD, CONV_CH), F32)], name=name,
        compiler_params=_params(1))(proj, proj, w_dw, b_dw, g_ln, b_ln)


def _conv_bwd(proj, cv, dmix, w_dw, g_ln, b_ln, name):
    t_len = proj.shape[0]
    nc = t_len // CONV_CHUNK

    def body(a_ref, gt_ref, cv_ref, dy_ref, w_ref, g_ref, bl_ref, du_ref, dw_ref, db_ref, dgl_ref, dbl_ref, pad, dpad):
        pad[0:CONV_PAD, :] = jnp.zeros((CONV_PAD, CONV_CH), F32)
        dpad[t_len:t_len + CONV_PAD, :] = jnp.zeros((CONV_PAD, CONV_CH), F32)
        dw_ref[...] = jnp.zeros_like(dw_ref)
        db_ref[...] = jnp.zeros_like(db_ref)
        dgl_ref[...] = jnp.zeros_like(dgl_ref)
        dbl_ref[...] = jnp.zeros_like(dbl_ref)

        def norm_bwd(c, carry):
            base = pl.multiple_of(c * CONV_CHUNK, CONV_CHUNK)
            rows = pl.ds(base, CONV_CHUNK)
            dst = pl.ds(pl.multiple_of(c * CONV_CHUNK + CONV_PAD, CONV_PAD), CONV_CHUNK)
            pad[dst, :] = a_ref[rows, :] * jax.nn.sigmoid(gt_ref[rows, :])
            cvv = cv_ref[rows, :]
            xc = cvv - jnp.mean(cvv, axis=-1, keepdims=True)
            rs = lax.rsqrt(jnp.mean(xc * xc, axis=-1, keepdims=True) + EPS)
            xhat = xc * rs
            z = xhat * g_ref[...] + bl_ref[...]
            sg = jax.nn.sigmoid(z)
            dz = dy_ref[rows, :] * (sg * (1.0 + z * (1.0 - sg)))
            dgl_ref[...] += jnp.sum(dz * xhat, axis=0, keepdims=True)
            dbl_ref[...] += jnp.sum(dz, axis=0, keepdims=True)
            dxh = dz * g_ref[...]
            dcv = rs * (dxh - jnp.mean(dxh, axis=-1, keepdims=True) - xhat * jnp.mean(dxh * xhat, axis=-1, keepdims=True))
            db_ref[...] += jnp.sum(dcv, axis=0, keepdims=True)
            dpad[rows, :] = dcv
            return carry

        lax.fori_loop(0, nc, norm_bwd, 0)
        w = w_ref[...]

        def conv_bwd(c, carry):
            base = pl.multiple_of(c * CONV_CHUNK, CONV_CHUNK)
            rows = pl.ds(base, CONV_CHUNK)
            dblk = dpad[pl.ds(base, CONV_CHUNK + CONV_PAD), :]
            dhc = _conv_taps(dblk, w, lambda k: CONV_WIDTH - 1 - k)
            a = a_ref[rows, :]
            sg = jax.nn.sigmoid(gt_ref[rows, :])
            du_ref[rows, 0:CONV_CH] = (dhc * sg).astype(BF)
            du_ref[rows, CONV_CH:2 * CONV_CH] = (dhc * a * sg * (1.0 - sg)).astype(BF)
            hblk = pad[pl.ds(base, CONV_CHUNK + CONV_PAD), :]
            dcv = dblk[0:CONV_CHUNK, :]
            for k in range(CONV_WIDTH):
                o = k + CONV_PAD - (CONV_WIDTH - 1)
                dw_ref[k:k + 1, :] += jnp.sum(dcv * hblk[o:o + CONV_CHUNK, :], axis=0, keepdims=True)
            return carry

        lax.fori_loop(0, nc, conv_bwd, 0)

    vec = pl.BlockSpec((1, CONV_CH), lambda i: (0, 0))
    full = pl.BlockSpec((t_len, CONV_CH), lambda i: (0, 0))
    wspec = pl.BlockSpec((CONV_WIDTH, CONV_CH), lambda i: (0, 0))
    vshape = jax.ShapeDtypeStruct((1, CONV_CH), F32)
    return pl.pallas_call(
        body,
        out_shape=[jax.ShapeDtypeStruct((t_len, 2 * CONV_CH), BF), jax.ShapeDtypeStruct((CONV_WIDTH, CONV_CH), F32),
                   vshape, vshape, vshape],
        grid=(1,),
        in_specs=[pl.BlockSpec((t_len, CONV_CH), lambda i: (0, 3)), pl.BlockSpec((t_len, CONV_CH), lambda i: (0, 4)), full,
                  pl.BlockSpec((t_len, CONV_CH), lambda i: (0, 2)), wspec, vec, vec],
        out_specs=[pl.BlockSpec((t_len, 2 * CONV_CH), lambda i: (0, 0)), wspec, vec, vec, vec],
        scratch_shapes=[pltpu.VMEM((t_len + CONV_PAD, CONV_CH), F32), pltpu.VMEM((t_len + CONV_PAD, CONV_CH), F32)],
        name=name, compiler_params=_params(1))(proj, proj, cv, dmix, w_dw, g_ln, b_ln)


def _mem_kv_fwd(mem, g, w, name):
    def body(m_ref, g_ref, w_ref, h_ref, kv_ref):
        mv = m_ref[...]
        h = (mv * _row_rms(mv) * g_ref[...]).astype(BF)
        h_ref[...] = h
        kv_ref[...] = _nn(h, w_ref[...])

    m_len = mem.shape[0]
    return pl.pallas_call(
        body, out_shape=[jax.ShapeDtypeStruct((m_len, D_MODEL), BF), jax.ShapeDtypeStruct((m_len, 2 * MQ_COLS), F32)],
        name=name, compiler_params=pltpu.CompilerParams(vmem_limit_bytes=VMEM_LIMIT_BYTES))(mem, g, w)


def _mem_kv_bwd(mem, g, h, w, dkv, name):
    def body(m_ref, g_ref, h_ref, w_ref, dkv_ref, dw_ref, dg_ref):
        dkv_b = dkv_ref[...].astype(BF)
        dw_ref[...] = _tn(h_ref[...], dkv_b).astype(BF)
        mv = m_ref[...]
        dg_ref[...] = jnp.sum(_nt(dkv_b, w_ref[...]) * mv * _row_rms(mv), axis=0, keepdims=True)

    return pl.pallas_call(
        body, out_shape=[jax.ShapeDtypeStruct((D_MODEL, 2 * MQ_COLS), BF), jax.ShapeDtypeStruct((1, D_MODEL), F32)],
        name=name, compiler_params=pltpu.CompilerParams(vmem_limit_bytes=VMEM_LIMIT_BYTES))(mem, g, h, w, dkv)


def _mem_attn_fwd(proj, mkv, gq2, gk2, name):
    t_len = proj.shape[0]
    tm = min(256, t_len)

    def body(q_ref, kv_ref, gq_ref, gk_ref, ym_ref, lse_ref):
        kvv = kv_ref[...]
        for pr in range(N_MEM_HEADS // 2):
            lanes = slice(pr * PAIR, (pr + 1) * PAIR)
            k_raw = kvv[:, lanes]
            kn = k_raw * _pair_rms(k_raw) * gk_ref[...]
            v = kvv[:, MQ_COLS + pr * PAIR:MQ_COLS + (pr + 1) * PAIR]
            q_raw = q_ref[:, lanes]
            qn = q_raw * _pair_rms(q_raw) * gq_ref[...]
            for hh in range(2):
                hd = 2 * pr + hh
                hs = slice(hh * HEAD_DIM, (hh + 1) * HEAD_DIM)
                s = _nt(qn[:, hs].astype(BF), kn[:, hs].astype(BF)) * SCALE
                m = jnp.max(s, axis=-1, keepdims=True)
                p = jnp.exp(s - m)
                den = jnp.sum(p, axis=-1, keepdims=True)
                ym_ref[:, hd * HEAD_DIM:(hd + 1) * HEAD_DIM] = _nn((p / den).astype(BF), v[:, hs].astype(BF)).astype(BF)
                lse_ref[:, hd:hd + 1] = m + jnp.log(den)

    m_len = mkv.shape[0]
    one = pl.BlockSpec((1, PAIR), lambda t: (0, 0))
    return pl.pallas_call(
        body, out_shape=[jax.ShapeDtypeStruct((t_len, MQ_COLS), BF), jax.ShapeDtypeStruct((t_len, N_MEM_HEADS), F32)],
        grid=(t_len // tm,),
        in_specs=[pl.BlockSpec((tm, MQ_COLS), lambda t: (t, 5)), pl.BlockSpec((m_len, 2 * MQ_COLS), lambda t: (0, 0)), one, one],
        out_specs=[pl.BlockSpec((tm, MQ_COLS), lambda t: (t, 0)), pl.BlockSpec((tm, N_MEM_HEADS), lambda t: (t, 0))],
        name=name, compiler_params=_params(1))(proj, mkv, gq2, gk2)


def _mem_attn_bwd(proj, mkv, gq2, gk2, ym, dmix, lse, name):
    t_len = proj.shape[0]
    tm = min(256, t_len)
    nt = t_len // tm
    m_len = mkv.shape[0]

    def body(q_ref, kv_ref, gq_ref, gk_ref, ym_ref, do_ref, lse_ref, dq_ref, dkv_ref, dgq_ref, dgk_ref, dkn_scr, dv_scr, scr):
        t = pl.program_id(0)

        @pl.when(t == 0)
        def _():
            dkn_scr[...] = jnp.zeros_like(dkn_scr)
            dv_scr[...] = jnp.zeros_like(dv_scr)
            dgq_ref[...] = jnp.zeros_like(dgq_ref)

        kvv = kv_ref[...]
        for pr in range(N_MEM_HEADS // 2):
            lanes = slice(pr * PAIR, (pr + 1) * PAIR)
            k_raw = kvv[:, lanes]
            kn = k_raw * _pair_rms(k_raw) * gk_ref[...]
            v = kvv[:, MQ_COLS + pr * PAIR:MQ_COLS + (pr + 1) * PAIR]
            q_raw = q_ref[:, lanes]
            rq = _pair_rms(q_raw)
            qn = q_raw * rq * gq_ref[...]
            d_o = do_ref[:, lanes]
            dsum = _pair_sum(d_o * ym_ref[:, lanes].astype(F32))
            for hh in range(2):
                hd = 2 * pr + hh
                hs = slice(hh * HEAD_DIM, (hh + 1) * HEAD_DIM)
                cols = slice(hd * HEAD_DIM, (hd + 1) * HEAD_DIM)
                qh = qn[:, hs].astype(BF)
                kh = kn[:, hs].astype(BF)
                d_oh = d_o[:, hs].astype(BF)
                p = jnp.exp(_nt(qh, kh) * SCALE - lse_ref[:, hd:hd + 1])
                dv_scr[:, cols] += _tn(p.astype(BF), d_oh)
                ds = (p * (_nt(d_oh, v[:, hs].astype(BF)) - dsum[:, hh * HEAD_DIM:hh * HEAD_DIM + 1])).astype(BF)
                scr[:, hs] = _nn(ds, kh) * SCALE
                dkn_scr[:, cols] += _tn(ds, qh) * SCALE
            dx, dg = _pair_rms_bwd(scr[...], q_raw, rq, gq_ref[...])
            dq_ref[:, lanes] = dx.astype(BF)
            dgq_ref[...] += dg

        @pl.when(t == nt - 1)
        def _():
            dgk = jnp.zeros((1, PAIR), F32)
            for pr in range(N_MEM_HEADS // 2):
                lanes = slice(pr * PAIR, (pr + 1) * PAIR)
                k_raw = kvv[:, lanes]
                dx, dg = _pair_rms_bwd(dkn_scr[:, lanes], k_raw, _pair_rms(k_raw), gk_ref[...])
                dkv_ref[:, lanes] = dx
                dgk = dgk + dg
            dkv_ref[:, MQ_COLS:2 * MQ_COLS] = dv_scr[...]
            dgk_ref[...] = dgk

    one = pl.BlockSpec((1, PAIR), lambda t: (0, 0))
    kvspec = pl.BlockSpec((m_len, 2 * MQ_COLS), lambda t: (0, 0))
    qspec = pl.BlockSpec((tm, MQ_COLS), lambda t: (t, 0))
    return pl.pallas_call(
        body,
        out_shape=[jax.ShapeDtypeStruct((t_len, MQ_COLS), BF), jax.ShapeDtypeStruct((m_len, 2 * MQ_COLS), F32),
                   jax.ShapeDtypeStruct((1, PAIR), F32), jax.ShapeDtypeStruct((1, PAIR), F32)],
        grid=(nt,),
        in_specs=[pl.BlockSpec((tm, MQ_COLS), lambda t: (t, 5)), kvspec, one, one, qspec,
                  pl.BlockSpec((tm, MQ_COLS), lambda t: (t, 3)), pl.BlockSpec((tm, N_MEM_HEADS), lambda t: (t, 0))],
        out_specs=[qspec, kvspec, one, one],
        scratch_shapes=[pltpu.VMEM((m_len, MQ_COLS), F32), pltpu.VMEM((m_len, MQ_COLS), F32), pltpu.VMEM((tm, PAIR), F32)],
        name=name, compiler_params=_params(1))(proj, mkv, gq2, gk2, ym, dmix, lse)


MIX_GROUPS = [(0, Q_COLS), (Q_COLS, CONV_CH), (Q_COLS + CONV_CH, MQ_COLS)]


def _out_proj_fwd(x, ya, yc, ym, w_out, name):
    t_len = x.shape[0]
    tm = min(512, t_len)

    def body(x_ref, ya_ref, yc_ref, ym_ref, w_ref, xo_ref):
        y = x_ref[...]
        for (start, width), ref in zip(MIX_GROUPS, (ya_ref, yc_ref, ym_ref)):
            y = y + _nn(ref[...], w_ref[start:start + width, :])
        xo_ref[...] = y

    cols = lambda w: pl.BlockSpec((tm, w), lambda t: (t, 0))
    return pl.pallas_call(
        body, out_shape=jax.ShapeDtypeStruct((t_len, D_MODEL), F32), grid=(t_len // tm,),
        in_specs=[cols(D_MODEL), cols(Q_COLS), cols(CONV_CH), cols(MQ_COLS),
                  pl.BlockSpec((D_MODEL, D_MODEL), lambda t: (0, 0))],
        out_specs=cols(D_MODEL), name=name, compiler_params=_params(1))(x, ya, yc, ym, w_out)


def _out_proj_bwd(dx, ya, yc, ym, w_out, name, after=None):
    t_len = dx.shape[0]
    tm = min(512, t_len)
    nt = t_len // tm

    def body(dx_ref, ya_ref, yc_ref, ym_ref, w_ref, dmix_ref, dw_ref, acc):
        t = pl.program_id(0)

        @pl.when(t == 0)
        def _():
            acc[...] = jnp.zeros_like(acc)

        dxb = dx_ref[...].astype(BF)
        dmix_ref[...] = _nt(dxb, w_ref[...])
        for (start, width), ref in zip(MIX_GROUPS, (ya_ref, yc_ref, ym_ref)):
            acc[start:start + width, :] += _tn(ref[...], dxb)

        @pl.when(t == nt - 1)
        def _():
            dw_ref[...] = acc[...].astype(BF)

    cols = lambda w: pl.BlockSpec((tm, w), lambda t: (t, 0))
    full = pl.BlockSpec((D_MODEL, D_MODEL), lambda t: (0, 0))
    body, first, first_specs = _ordered_after(body, after)
    return pl.pallas_call(
        body, out_shape=[jax.ShapeDtypeStruct((t_len, D_MODEL), F32), jax.ShapeDtypeStruct((D_MODEL, D_MODEL), BF)],
        grid=(nt,), in_specs=first_specs + [cols(D_MODEL), cols(Q_COLS), cols(CONV_CH), cols(MQ_COLS), full],
        out_specs=[cols(D_MODEL), full], scratch_shapes=[pltpu.VMEM((D_MODEL, D_MODEL), F32)], name=name,
        compiler_params=_params(1))(*first, dx, ya, yc, ym, w_out)


N_PEERS = N_DEV - 1
HBM_SPEC = pl.BlockSpec(memory_space=pltpu.HBM)
SEM_SPEC = pl.BlockSpec(memory_space=pltpu.SEMAPHORE)
EFFECT = pltpu.SideEffectType.DATAFLOW_SIDE_EFFECTING


def _my_place():
    x, y, c = lax.axis_index("x"), lax.axis_index("y"), lax.axis_index("c")
    return x, y, c, 4 * x + 2 * y + c


def _peer(k):
    x, y, c, _ = _my_place()
    px = 1 - x if k & 4 else x
    py = 1 - y if k & 2 else y
    pc = 1 - c if k & 1 else c
    return (px, py, pc), 4 * px + 2 * py + pc


def _gather_two_level(shards, name):
    n = len(shards)

    def body(*refs):
        ins, outs = refs[:n], refs[n:2 * n]
        send_sems, recv_sems, local_sems = refs[2 * n:]
        x, y, c, me = _my_place()
        sibling = (x, y, 1 - c)
        chips = [(1 - x, y), (x, 1 - y), (1 - x, 1 - y)]
        slot = lambda px, py, pc: 4 * px + 2 * py + pc

        def copy(w, k, src, dst_slot, to):
            return pltpu.make_async_remote_copy(
                src_ref=src, dst_ref=outs[w].at[dst_slot], send_sem=send_sems.at[w * N_PEERS + k],
                recv_sem=recv_sems.at[w * N_PEERS + k], device_id=to, device_id_type=pl.DeviceIdType.MESH)

        local = [pltpu.make_async_copy(ins[w], outs[w].at[me], local_sems.at[w]) for w in range(n)]
        for cp in local:
            cp.start()
        sends = []
        for w in range(n):
            for j, chip in enumerate(chips):
                sends.append(copy(w, 1 + j, ins[w], me, (*chip, c)))
            sends.append(copy(w, 0, ins[w], me, sibling))
        for cp in sends:
            cp.start()
        for w in range(n):
            for j, chip in enumerate(chips):
                got = slot(*chip, c)
                copy(w, 1 + j, ins[w], got, (*chip, c)).wait_recv()
                fwd = copy(w, 4 + j, outs[w].at[got], got, sibling)
                fwd.start()
                sends.append(fwd)
        for w in range(n):
            copy(w, 0, ins[w], slot(x, y, 1 - c), sibling).wait_recv()
            for j, chip in enumerate(chips):
                copy(w, 4 + j, ins[w], slot(*chip, 1 - c), sibling).wait_recv()
        for cp in sends:
            cp.wait_send()
        for cp in local:
            cp.wait()

    hbm = pl.BlockSpec(memory_space=pl.ANY)
    return pl.pallas_call(
        body, out_shape=[jax.ShapeDtypeStruct((N_DEV,) + a.shape, a.dtype) for a in shards], in_specs=[hbm] * n,
        out_specs=[hbm] * n,
        scratch_shapes=[pltpu.SemaphoreType.DMA((n * N_PEERS,)), pltpu.SemaphoreType.DMA((n * N_PEERS,)),
                        pltpu.SemaphoreType.DMA((n,))],
        name=name)(*shards)


def _split_copies(srcs, lands, gather, send_sems, recv_sems):
    _, _, _, me = _my_place()
    pairs = []
    for w in range(len(srcs)):
        for k in range(1, N_DEV):
            dev, idx = _peer(k)
            src = srcs[w] if gather[w] else srcs[w].at[idx]
            sems = dict(send_sem=send_sems.at[w * N_PEERS + k - 1], recv_sem=recv_sems.at[w * N_PEERS + k - 1], device_id=dev,
                        device_id_type=pl.DeviceIdType.MESH)
            pairs.append((pltpu.make_async_remote_copy(src_ref=src, dst_ref=lands[w].at[me], **sems),
                          pltpu.make_async_remote_copy(src_ref=src, dst_ref=lands[w].at[idx], **sems)))
    return pairs


def _exchange_start(srcs, lands, gather, after, name):
    n = len(srcs)

    def body(*refs):
        src_refs, land_refs = refs[:n], refs[n:2 * n]
        send_sems, recv_sems = refs[2 * n + 1], refs[2 * n + 2]
        token = refs[-1]
        for out_going, _ in _split_copies(src_refs, land_refs, gather, send_sems, recv_sems):
            out_going.start()
        token[...] = jnp.zeros_like(token)

    arrays = list(srcs) + list(lands)
    out = pl.pallas_call(
        body, name=name,
        out_shape=(pltpu.SemaphoreType.DMA((n * N_PEERS,)), pltpu.SemaphoreType.DMA((n * N_PEERS,)),
                   *[pltpu.HBM(a.shape, a.dtype) for a in arrays], jax.ShapeDtypeStruct((8, 128), F32)),
        in_specs=[HBM_SPEC] * (2 * n) + [pl.BlockSpec(memory_space=pl.ANY)],
        out_specs=(SEM_SPEC, SEM_SPEC, *[HBM_SPEC] * (2 * n), pl.BlockSpec(memory_space=pltpu.VMEM)),
        input_output_aliases={i: i + 2 for i in range(2 * n)},
        compiler_params=pltpu.CompilerParams(has_side_effects=EFFECT),
    )(*[pltpu.with_memory_space_constraint(a, pltpu.HBM) for a in arrays], after)
    return out[0], out[1], out[2:2 + n], out[2 + n:2 + 2 * n], out[-1]


def _exchange_wait(send_sems, recv_sems, srcs, lands, gather, after, name):
    n = len(srcs)

    def body(*refs):
        src_refs, land_refs = refs[:n], refs[n:2 * n]
        for out_going, arriving in _split_copies(src_refs, land_refs, gather, refs[2 * n], refs[2 * n + 1]):
            out_going.wait_send()
            arriving.wait_recv()

    arrays = list(srcs) + list(lands)
    out = pl.pallas_call(
        body, name=name, out_shape=tuple(pltpu.HBM(a.shape, a.dtype) for a in arrays),
        in_specs=[HBM_SPEC] * (2 * n) + [SEM_SPEC, SEM_SPEC, pl.BlockSpec(memory_space=pl.ANY)],
        out_specs=tuple([HBM_SPEC] * (2 * n)), input_output_aliases={i: i for i in range(2 * n)},
        compiler_params=pltpu.CompilerParams(has_side_effects=EFFECT),
    )(*arrays, send_sems, recv_sems, after)
    return out[n:]


def _own_slot(a, me, gather):
    mine = a if gather else lax.dynamic_index_in_dim(a, me, 0, keepdims=False)
    return lax.dynamic_update_index_in_dim(lax.empty((N_DEV,) + mine.shape, mine.dtype), mine, me, 0)


def _adamw_math(w, g, m, v):
    m2 = ADAM_B1 * m + (1.0 - ADAM_B1) * g
    v2 = ADAM_B2 * v + (1.0 - ADAM_B2) * (g * g)
    m_hat = m2 / (1.0 - ADAM_B1 ** ADAM_STEP)
    v_hat = v2 / (1.0 - ADAM_B2 ** ADAM_STEP)
    return -ADAM_LR * (m_hat / (jnp.sqrt(v_hat) + ADAM_EPS) + ADAM_WD * w), m2, v2


def _sum_adamw(parts, w, m, v, name):
    rows, cols = w.shape
    tr = rows if rows <= 512 else 256

    def body(p_ref, w_ref, m_ref, v_ref, g_ref, d_ref, m2_ref, v2_ref):
        g = p_ref[0].astype(F32)
        for s in range(1, N_DEV):
            g = g + p_ref[s].astype(F32)
        g_ref[...] = g
        d_ref[...], m2_ref[...], v2_ref[...] = _adamw_math(w_ref[...], g, m_ref[...], v_ref[...])

    blk = pl.BlockSpec((tr, cols), lambda i: (i, 0))
    shape = jax.ShapeDtypeStruct((rows, cols), F32)
    return pl.pallas_call(
        body, out_shape=[shape] * 4, grid=(rows // tr,),
        in_specs=[pl.BlockSpec((N_DEV, tr, cols), lambda i: (0, i, 0)), blk, blk, blk], out_specs=[blk] * 4, name=name,
        compiler_params=_params(1))(parts, w, m, v)


def _slot_sum(parts, name):
    def body(p_ref, g_ref):
        g = p_ref[0]
        for s in range(1, N_DEV):
            g = g + p_ref[s]
        g_ref[...] = g

    return pl.pallas_call(body, out_shape=jax.ShapeDtypeStruct(parts.shape[1:], F32), name=name)(parts)


def _adamw(g, w, m, v, name):
    def body(g_ref, w_ref, m_ref, v_ref, d_ref, m2_ref, v2_ref):
        d_ref[...], m2_ref[...], v2_ref[...] = _adamw_math(w_ref[...], g_ref[...], m_ref[...], v_ref[...])

    shape = jax.ShapeDtypeStruct(w.shape, F32)
    return pl.pallas_call(body, out_shape=[shape] * 3, name=name)(g, w, m, v)


def _pack_small(vals):
    flat = jnp.concatenate([vals[k].reshape(-1) for k in SMALL])
    return jnp.pad(flat, (0, SMALL_ROWS * D_MODEL - flat.shape[0])).reshape(SMALL_ROWS, D_MODEL)


def _unpack_small(packed, like):
    flat, out, at = packed.reshape(-1), {}, 0
    for k in SMALL:
        size = like[k].size
        out[k] = flat[at:at + size].reshape(like[k].shape)
        at += size
    return out


FFN1 = ["w_ffn1_gate", "w_ffn1_up", "w_ffn1_down"]
FFN2 = ["w_ffn2_gate", "w_ffn2_up", "w_ffn2_down"]
MIXER = ["w_in", "w_mem_kv", "w_out"]
TRANSPOSED = ["w_ffn1_gate", "w_ffn1_up", "w_ffn2_gate", "w_ffn2_up", "w_in"]


def _mixer_fwd(x1, mem, tables, sm, big):
    cos_t, sin_t, gq2, gk2, gmq2, gmk2 = tables
    h2, proj = _in_proj_fwd(x1, sm["g_mix"], big["w_in"], "in_proj_fwd")
    ya, lse = _attn_fwd(proj, cos_t, sin_t, gq2, gk2, sm["sinks"], "attn_fwd")
    yc, cv = _conv_fwd(proj, big["w_dw"], sm["b_dw"], sm["g_conv_ln"], sm["b_conv_ln"], "conv_fwd")
    hm, mkv = _mem_kv_fwd(mem, sm["g_mem"], big["w_mem_kv"], "mem_kv_fwd")
    ym, lse_m = _mem_attn_fwd(proj, mkv, gmq2, gmk2, "mem_attn_fwd")
    x2 = _out_proj_fwd(x1, ya, yc, ym, big["w_out"], "out_proj_fwd")
    return x2, (h2, proj, ya, lse, yc, cv, hm, mkv, ym, lse_m)


def _mixer_bwd(dx2, x1, mem, tables, sm, big, saved, after):
    cos_t, sin_t, gq2, gk2, gmq2, gmk2 = tables
    h2, proj, ya, lse, yc, cv, hm, mkv, ym, lse_m = saved
    g = {}
    dmix, g["w_out"] = _out_proj_bwd(dx2, ya, yc, ym, big["w_out"], "out_proj_bwd", after)
    dq, dgq, g["sinks"] = _attn_bwd_q(proj, cos_t, sin_t, gq2, gk2, sm["sinks"], ya, dmix, lse, "attn_bwd_q")
    dk, dv, dgk = _attn_bwd_kv(proj, cos_t, sin_t, gq2, gk2, ya, dmix, lse, "attn_bwd_kv")
    du, g["w_dw"], g["b_dw"], g["g_conv_ln"], g["b_conv_ln"] = _conv_bwd(
        proj, cv, dmix, big["w_dw"], sm["g_conv_ln"], sm["b_conv_ln"], "conv_bwd")
    dmq, dmkv, dgmq, dgmk = _mem_attn_bwd(proj, mkv, gmq2, gmk2, ym, dmix, lse_m, "mem_attn_bwd")
    g["w_mem_kv"], g["g_mem"] = _mem_kv_bwd(mem, sm["g_mem"], hm, big["w_mem_kv"], dmkv, "mem_kv_bwd")
    dx1, g["g_mix"], g["w_in"] = _in_proj_bwd(dq, dk, dv, du, dmq, big["w_in"], h2, x1, sm["g_mix"], dx2, "in_proj_bwd")
    fold = lambda a: a[:, :HEAD_DIM] + a[:, HEAD_DIM:]
    g["g_q"], g["g_k"], g["g_mq"], g["g_mk"] = fold(dgq), fold(dgk), fold(dgmq), fold(dgmk)
    return dx1, g


def _tables(positions, sm):
    pair = lambda a: jnp.tile(a, (1, 2))
    return _rope_tables(positions) + (pair(sm["g_q"]), pair(sm["g_k"]), pair(sm["g_mq"]), pair(sm["g_mk"]))


def kernel(x, mem, positions, g_ffn1, w_ffn1_gate, w_ffn1_up, w_ffn1_down, g_mix, w_in, g_q, g_k, sinks, w_dw, b_dw, g_conv_ln, b_conv_ln, g_mem, w_mem_kv, g_mq, g_mk, w_out, g_ffn2, w_ffn2_gate, w_ffn2_up, w_ffn2_down, loss_target, m_g_ffn1, m_w_ffn1_gate, m_w_ffn1_up, m_w_ffn1_down, m_g_mix, m_w_in, m_g_q, m_g_k, m_sinks, m_w_dw, m_b_dw, m_g_conv_ln, m_b_conv_ln, m_g_mem, m_w_mem_kv, m_g_mq, m_g_mk, m_w_out, m_g_ffn2, m_w_ffn2_gate, m_w_ffn2_up, m_w_ffn2_down, v_g_ffn1, v_w_ffn1_gate, v_w_ffn1_up, v_w_ffn1_down, v_g_mix, v_w_in, v_g_q, v_g_k, v_sinks, v_w_dw, v_b_dw, v_g_conv_ln, v_b_conv_ln, v_g_mem, v_w_mem_kv, v_g_mq, v_g_mk, v_w_out, v_g_ffn2, v_w_ffn2_gate, v_w_ffn2_up, v_w_ffn2_down):
    w = dict(g_ffn1=g_ffn1, w_ffn1_gate=w_ffn1_gate, w_ffn1_up=w_ffn1_up, w_ffn1_down=w_ffn1_down, g_mix=g_mix, w_in=w_in,
             g_q=g_q, g_k=g_k, sinks=sinks, w_dw=w_dw, b_dw=b_dw, g_conv_ln=g_conv_ln, b_conv_ln=b_conv_ln, g_mem=g_mem,
             w_mem_kv=w_mem_kv, g_mq=g_mq, g_mk=g_mk, w_out=w_out, g_ffn2=g_ffn2, w_ffn2_gate=w_ffn2_gate,
             w_ffn2_up=w_ffn2_up, w_ffn2_down=w_ffn2_down)
    mo = dict(g_ffn1=m_g_ffn1, w_ffn1_gate=m_w_ffn1_gate, w_ffn1_up=m_w_ffn1_up, w_ffn1_down=m_w_ffn1_down, g_mix=m_g_mix,
              w_in=m_w_in, g_q=m_g_q, g_k=m_g_k, sinks=m_sinks, w_dw=m_w_dw, b_dw=m_b_dw, g_conv_ln=m_g_conv_ln,
              b_conv_ln=m_b_conv_ln, g_mem=m_g_mem, w_mem_kv=m_w_mem_kv, g_mq=m_g_mq, g_mk=m_g_mk, w_out=m_w_out,
              g_ffn2=m_g_ffn2, w_ffn2_gate=m_w_ffn2_gate, w_ffn2_up=m_w_ffn2_up, w_ffn2_down=m_w_ffn2_down)
    vo = dict(g_ffn1=v_g_ffn1, w_ffn1_gate=v_w_ffn1_gate, w_ffn1_up=v_w_ffn1_up, w_ffn1_down=v_w_ffn1_down, g_mix=v_g_mix,
              w_in=v_w_in, g_q=v_g_q, g_k=v_g_k, sinks=v_sinks, w_dw=v_w_dw, b_dw=v_b_dw, g_conv_ln=v_g_conv_ln,
              b_conv_ln=v_b_conv_ln, g_mem=v_g_mem, w_mem_kv=v_w_mem_kv, g_mq=v_g_mq, g_mk=v_g_mk, w_out=v_w_out,
              g_ffn2=v_g_ffn2, w_ffn2_gate=v_w_ffn2_gate, w_ffn2_up=v_w_ffn2_up, w_ffn2_down=v_w_ffn2_down)
    me = _my_place()[3]
    sm = {k: w[k] for k in SMALL}
    flip = lambda k, a: a.T if k in TRANSPOSED else a
    as_bf16 = lambda names: [flip(k, w[k][0]).astype(BF) for k in names]
    zones = lambda arrays, gather: [_own_slot(a, me, gather) for a in arrays]
    out_g, out_d, out_m, out_v = {}, {}, {}, {}

    def update(names, parts_list):
        for k, parts in zip(names, parts_list):
            new = _sum_adamw(parts, flip(k, w[k][0]), flip(k, mo[k][0]), flip(k, vo[k][0]), "adamw_" + k)
            out_g[k], out_d[k], out_m[k], out_v[k] = [flip(k, a)[None] for a in new]

    w1 = _gather_two_level(as_bf16(FFN1), "gather_ffn1")
    mix_src = as_bf16(MIXER) + [w["w_dw"][0]]
    s_mix = _exchange_start(mix_src, zones(mix_src, True), [True] * 4, w1[0], "gather_mixer_start")
    f2_src = as_bf16(FFN2)
    s_f2 = _exchange_start(f2_src, zones(f2_src, True), [True] * 3, s_mix[4], "gather_ffn2_start")

    tables = _tables(positions[0], sm)
    h1, gate1, up1, x1 = _ffn_fwd(x[0], sm["g_ffn1"], *w1, None, "ffn1_fwd", after=s_f2[4])
    got = _exchange_wait(*s_mix[:4], [True] * 4, x1, "gather_mixer_wait")
    big = dict(w_in=got[0].reshape(IN_COLS, D_MODEL), w_mem_kv=got[1].reshape(D_MODEL, 2 * MQ_COLS),
               w_out=got[2].reshape(D_MODEL, D_MODEL), w_dw=got[3].transpose(1, 0, 2).reshape(CONV_WIDTH, CONV_CH))
    x2, saved = _mixer_fwd(x1, mem[0], tables, sm, big)
    w2 = _exchange_wait(*s_f2[:4], [True] * 3, x2, "gather_ffn2_wait")
    h3, gate2, up2, dy, loss_part = _ffn_fwd(x2, sm["g_ffn2"], *w2, loss_target[0], "ffn2_fwd")

    grads = {}
    dyb2, act2, dgate2, dup2, dx2, grads["g_ffn2"] = _ffn_bwd_act(dy, x2, sm["g_ffn2"], gate2, up2, *w2, "ffn2_bwd_act")
    g_f2 = list(_ffn_bwd_w(h3, dyb2, act2, dgate2, dup2, "ffn2_bwd_w"))
    r_f2 = _exchange_start(g_f2, zones(g_f2, False), [False] * 3, dx2, "scatter_ffn2_start")
    dx1, g_mid = _mixer_bwd(dx2, x1, mem[0], tables, sm, big, saved, r_f2[4])
    grads.update(g_mid)
    g_mix = [g_mid["w_in"].reshape(N_DEV, IN_COLS // N_DEV, D_MODEL),
             g_mid["w_mem_kv"].reshape(N_DEV, D_MODEL // N_DEV, 2 * MQ_COLS),
             g_mid["w_out"].reshape(N_DEV, D_MODEL // N_DEV, D_MODEL)]
    r_mix = _exchange_start(g_mix, zones(g_mix, False), [False] * 3, dx1, "scatter_mixer_start")
    dyb1, act1, dgate1, dup1, grad_x, grads["g_ffn1"] = _ffn_bwd_act(dx1, x[0], sm["g_ffn1"], gate1, up1, *w1,
                                                                     "ffn1_bwd_act", after=r_mix[4])
    g_f1 = list(_ffn_bwd_w(h1, dyb1, act1, dgate1, dup1, "ffn1_bwd_w"))
    dw_flat = grads["w_dw"].reshape(-1)
    packed = jnp.concatenate([_pack_small(grads),
                              jnp.pad(dw_flat, (0, SMALL_ROWS * D_MODEL - dw_flat.shape[0])).reshape(SMALL_ROWS, D_MODEL)])
    last_src, last_kind = g_f1 + [packed], [False] * 3 + [True]
    r_f1 = _exchange_start(last_src, zones(g_f1, False) + zones([packed], True), last_kind, grad_x, "scatter_ffn1_start")

    all_done = lambda names: sum(out_d[k][:, :1, :1] for k in names)
    update(FFN2, _exchange_wait(*r_f2[:4], [False] * 3, r_f1[4], "scatter_ffn2_wait"))
    update(MIXER, _exchange_wait(*r_mix[:4], [False] * 3, all_done(FFN2), "scatter_mixer_wait"))
    last = _exchange_wait(*r_f1[:4], last_kind, all_done(MIXER), "scatter_ffn1_wait")
    update(FFN1, last[:3])
    small_sum = _slot_sum(last[3], "small_grad_sum")
    loss_local = loss_part[0, 0]
    g_small = small_sum[:SMALL_ROWS]
    d, m2, v2 = _adamw(g_small, _pack_small(w), _pack_small(mo), _pack_small(vo), "adamw_small")
    for dst, val in ((out_g, g_small), (out_d, d), (out_m, m2), (out_v, v2)):
        dst.update(_unpack_small(val, sm))
    g_dw = small_sum[SMALL_ROWS:].reshape(-1)[:CONV_WIDTH * CONV_CH].reshape(CONV_WIDTH, CONV_CH)
    g_dw = lax.dynamic_slice_in_dim(g_dw, me * (CONV_CH // N_DEV), CONV_CH // N_DEV, axis=1)
    d, m2, v2 = _adamw(g_dw, w["w_dw"][0], mo["w_dw"][0], vo["w_dw"][0], "adamw_w_dw")
    out_g["w_dw"], out_d["w_dw"], out_m["w_dw"], out_v["w_dw"] = g_dw[None], d[None], m2[None], v2[None]

    loss = lax.psum(loss_local, ("x", "y", "c"))
    return (loss, grad_x[None], *[out_g[k] for k in WEIGHTS], *[out_d[k] for k in WEIGHTS], *[out_m[k] for k in WEIGHTS],
            *[out_v[k] for k in WEIGHTS])
```

```python
import jax
import jax.numpy as jnp
from jax import lax
from jax.experimental import pallas as pl
from jax.experimental.pallas import tpu as pltpu

D_MODEL = 1024
N_DEV = 8
FF_BLOCK = 352
D_FF = N_DEV * FF_BLOCK
FF_TILE = 256
N_FF_TILES = D_FF // FF_TILE
FF_ROWS = 1024
HEAD_DIM = 64
PAIR = 2 * HEAD_DIM
N_Q_HEADS = 8
Q_PER_KV = 4
ATTN_BLOCK = 128
Q_COLS = 512
KV_COLS = 128
CONV_CH = 256
MQ_COLS = 256
IN_COLS = 1536
CONV_WIDTH = 31
CONV_PAD = 32
CONV_CHUNK = 256
N_MEM_HEADS = 4
ROPE_THETA = 500000.0
ROPE_HALF = 8
EPS = 1e-6
SCALE = HEAD_DIM ** -0.5
NEG = -1e30
ADAM_LR, ADAM_B1, ADAM_B2, ADAM_EPS, ADAM_WD, ADAM_STEP = 0.001, 0.9, 0.999, 1e-08, 0.01, 10
VMEM_LIMIT_BYTES = 56 * 1024 * 1024
BF = jnp.bfloat16
F32 = jnp.float32

SMALL = ["g_ffn1", "g_mix", "g_mem", "g_ffn2", "g_q", "g_k", "g_mq", "g_mk", "sinks", "b_dw", "g_conv_ln", "b_conv_ln"]
WEIGHTS = ["g_ffn1", "w_ffn1_gate", "w_ffn1_up", "w_ffn1_down", "g_mix", "w_in", "g_q", "g_k", "sinks", "w_dw", "b_dw",
           "g_conv_ln", "b_conv_ln", "g_mem", "w_mem_kv", "g_mq", "g_mk", "w_out", "g_ffn2", "w_ffn2_gate", "w_ffn2_up",
           "w_ffn2_down"]
SMALL_ROWS = 8
PACK_ROWS = 16


def _nn(a, b):
    return jnp.dot(a, b, preferred_element_type=F32)


def _nt(a, b):
    return lax.dot_general(a, b, (((1,), (1,)), ((), ())), preferred_element_type=F32)


def _tn(a, b):
    return lax.dot_general(a, b, (((0,), (0,)), ((), ())), preferred_element_type=F32)


def _params(n_grid):
    return pltpu.CompilerParams(dimension_semantics=("arbitrary",) * n_grid, vmem_limit_bytes=VMEM_LIMIT_BYTES)


def _ordered_after(body, after):
    if after is None:
        return body, (), []

    def body_after(after_ref, *refs):
        body(*refs)

    return body_after, (after,), [pl.BlockSpec(memory_space=pl.ANY)]


def _row_rms(xv):
    return lax.rsqrt(jnp.mean(xv * xv, axis=-1, keepdims=True) + EPS)


def _rms_bwd(dh, xv, r, g):
    u = dh * g
    dx = r * u - xv * (r * r * r) * jnp.mean(u * xv, axis=-1, keepdims=True)
    return dx, jnp.sum(dh * xv * r, axis=0, keepdims=True)


def _sum_all(a):
    return jnp.sum(jnp.sum(a, axis=1, keepdims=True), axis=0, keepdims=True)


def _lane_lo(shape):
    return lax.broadcasted_iota(jnp.int32, shape, 1) < HEAD_DIM


def _pair_sum(v):
    lo = _lane_lo(v.shape)
    s_lo = jnp.sum(jnp.where(lo, v, 0.0), axis=-1, keepdims=True)
    s_hi = jnp.sum(jnp.where(lo, 0.0, v), axis=-1, keepdims=True)
    return jnp.where(lo, s_lo, s_hi)


def _pair_rms(xv):
    return lax.rsqrt(_pair_sum(xv * xv) * (1.0 / HEAD_DIM) + EPS)


def _pair_rms_bwd(dxn, xv, r, g):
    u = dxn * g
    dx = r * u - xv * (r * r * r) * (_pair_sum(u * xv) * (1.0 / HEAD_DIM))
    return dx, jnp.sum(dxn * xv * r, axis=0, keepdims=True)


def _rope_mask(shape):
    lane = lax.broadcasted_iota(jnp.int32, shape, 1)
    return ((lane & (HEAD_DIM - 1)) < 2 * ROPE_HALF).astype(F32)


def _partner(v):
    return pltpu.roll(v, ROPE_HALF, 1) + pltpu.roll(v, PAIR - ROPE_HALF, 1)


def _rope(xn, cos_t, sin_t):
    return xn * cos_t + _partner(xn * _rope_mask(xn.shape)) * sin_t


def _rope_t(d, cos_t, sin_t):
    return d * cos_t + _partner(d * sin_t) * _rope_mask(d.shape)


def _rope_tables(positions):
    inv_freq = ROPE_THETA ** (-jnp.arange(ROPE_HALF, dtype=F32) / ROPE_HALF)
    ang = positions.astype(F32)[:, None] * inv_freq
    cos, sin = jnp.cos(ang), jnp.sin(ang)
    t = positions.shape[0]
    cos_h = jnp.concatenate([cos, cos, jnp.ones((t, HEAD_DIM - 2 * ROPE_HALF), F32)], axis=1)
    sin_h = jnp.concatenate([-sin, sin, jnp.zeros((t, HEAD_DIM - 2 * ROPE_HALF), F32)], axis=1)
    return jnp.tile(cos_h, (1, 2)), jnp.tile(sin_h, (1, 2))


def _ff_rows(w):
    return w.reshape(D_FF, D_MODEL)


def _ffn_fwd(x, g, wg, wu, wd, target, name, after=None):
    t_len = x.shape[0]
    tm = min(FF_ROWS, t_len)
    with_loss = target is not None

    def body(*refs):
        if with_loss:
            x_ref, g_ref, wg_ref, wu_ref, wd_ref, t_ref, h_ref, gg_ref, uu_ref, dy_ref, loss_ref, acc = refs
        else:
            x_ref, g_ref, wg_ref, wu_ref, wd_ref, h_ref, gg_ref, uu_ref, xo_ref, acc = refs
        t, j = pl.program_id(0), pl.program_id(1)

        @pl.when(j == 0)
        def _():
            xv = x_ref[...]
            h_ref[...] = (xv * _row_rms(xv) * g_ref[...]).astype(BF)
            acc[...] = jnp.zeros_like(acc)

        h = h_ref[...]
        gate = _nt(h, wg_ref[...])
        up = _nt(h, wu_ref[...])
        gg_ref[...] = gate.astype(BF)
        uu_ref[...] = up.astype(BF)
        act = (gate * jax.nn.sigmoid(gate) * up).astype(BF)
        acc[...] += _nn(act, wd_ref[...])

        @pl.when(j == N_FF_TILES - 1)
        def _():
            xo = x_ref[...] + 0.5 * acc[...]
            if with_loss:
                err = xo - t_ref[...]
                dy_ref[...] = err * (1.0 / D_MODEL)

                @pl.when(t == 0)
                def _():
                    loss_ref[...] = jnp.zeros_like(loss_ref)

                loss_ref[...] += _sum_all(err * err) * (0.5 / D_MODEL)
            else:
                xo_ref[...] = xo

    row = pl.BlockSpec((tm, D_MODEL), lambda t, j: (t, 0))
    vec = pl.BlockSpec((1, D_MODEL), lambda t, j: (0, 0))
    w_spec = pl.BlockSpec((FF_TILE, D_MODEL), lambda t, j: (j, 0))
    blk = pl.BlockSpec((tm, FF_TILE), lambda t, j: (t, j))
    in_specs = [row, vec, w_spec, w_spec, w_spec] + ([row] if with_loss else [])
    out_shape = [jax.ShapeDtypeStruct((t_len, D_MODEL), BF),
                 jax.ShapeDtypeStruct((t_len, D_FF), BF),
                 jax.ShapeDtypeStruct((t_len, D_FF), BF),
                 jax.ShapeDtypeStruct((t_len, D_MODEL), F32)]
    out_specs = [row, blk, blk, row]
    if with_loss:
        out_shape.append(jax.ShapeDtypeStruct((1, 128), F32))
        out_specs.append(pl.BlockSpec((1, 128), lambda t, j: (0, 0)))
    args = (x, g, _ff_rows(wg), _ff_rows(wu), _ff_rows(wd)) + ((target,) if with_loss else ())
    body, first, first_specs = _ordered_after(body, after)
    return pl.pallas_call(body, out_shape=out_shape, grid=(t_len // tm, N_FF_TILES), in_specs=first_specs + in_specs,
                          out_specs=out_specs, scratch_shapes=[pltpu.VMEM((tm, D_MODEL), F32)], name=name,
                          compiler_params=_params(2))(*first, *args)


def _ffn_bwd_act(dy, x, g, gate, up, wg, wu, wd, name, after=None):
    t_len = x.shape[0]
    tm = min(FF_ROWS, t_len)

    def body(dy_ref, x_ref, g_ref, gg_ref, uu_ref, wg_ref, wu_ref, wd_ref,
             dyb_ref, act_ref, dgg_ref, duu_ref, dx_ref, dg_ref, acc):
        t, j = pl.program_id(0), pl.program_id(1)

        @pl.when(j == 0)
        def _():
            dyb_ref[...] = (0.5 * dy_ref[...]).astype(BF)
            acc[...] = jnp.zeros_like(acc)

        d_act = _nt(dyb_ref[...], wd_ref[...])
        gate = gg_ref[...].astype(F32)
        upv = uu_ref[...].astype(F32)
        sig = jax.nn.sigmoid(gate)
        silu = gate * sig
        d_up = (d_act * silu).astype(BF)
        d_gate = (d_act * upv * (sig * (1.0 + gate * (1.0 - sig)))).astype(BF)
        act_ref[...] = (silu * upv).astype(BF)
        dgg_ref[...] = d_gate
        duu_ref[...] = d_up
        acc[...] += _nn(d_gate, wg_ref[...]) + _nn(d_up, wu_ref[...])

        @pl.when(j == N_FF_TILES - 1)
        def _():
            xv = x_ref[...]
            dx, dg = _rms_bwd(acc[...], xv, _row_rms(xv), g_ref[...])
            dx_ref[...] = dy_ref[...] + dx

            @pl.when(t == 0)
            def _():
                dg_ref[...] = jnp.zeros_like(dg_ref)

            dg_ref[...] += dg

    row = pl.BlockSpec((tm, D_MODEL), lambda t, j: (t, 0))
    vec = pl.BlockSpec((1, D_MODEL), lambda t, j: (0, 0))
    w_spec = pl.BlockSpec((FF_TILE, D_MODEL), lambda t, j: (j, 0))
    blk = pl.BlockSpec((tm, FF_TILE), lambda t, j: (t, j))
    blk_shape = jax.ShapeDtypeStruct((t_len, D_FF), BF)
    body, first, first_specs = _ordered_after(body, after)
    return pl.pallas_call(
        body,
        out_shape=[jax.ShapeDtypeStruct((t_len, D_MODEL), BF), blk_shape, blk_shape, blk_shape,
                   jax.ShapeDtypeStruct((t_len, D_MODEL), F32), jax.ShapeDtypeStruct((1, D_MODEL), F32)],
        grid=(t_len // tm, N_FF_TILES), in_specs=first_specs + [row, row, vec, blk, blk, w_spec, w_spec, w_spec],
        out_specs=[row, blk, blk, blk, row, vec], scratch_shapes=[pltpu.VMEM((tm, D_MODEL), F32)], name=name,
        compiler_params=_params(2))(*first, dy, x, g, gate, up, _ff_rows(wg), _ff_rows(wu), _ff_rows(wd))


def _ffn_bwd_w(h, dyb, act, d_gate, d_up, name):
    t_len = h.shape[0]
    tm = min(FF_ROWS, t_len)
    nt = t_len // tm

    def body(h_ref, dyb_ref, act_ref, dgg_ref, duu_ref, dwg_ref, dwu_ref, dwd_ref, ag, au, ad):
        t = pl.program_id(1)

        @pl.when(t == 0)
        def _():
            ag[...] = jnp.zeros_like(ag)
            au[...] = jnp.zeros_like(au)
            ad[...] = jnp.zeros_like(ad)

        h = h_ref[...]
        ag[...] += _tn(dgg_ref[...], h)
        au[...] += _tn(duu_ref[...], h)
        ad[...] += _tn(act_ref[...], dyb_ref[...])

        @pl.when(t == nt - 1)
        def _():
            dwg_ref[...] = ag[...].astype(BF)
            dwu_ref[...] = au[...].astype(BF)
            dwd_ref[...] = ad[...].astype(BF)

    row = pl.BlockSpec((tm, D_MODEL), lambda j, t: (t, 0))
    blk = pl.BlockSpec((tm, FF_TILE), lambda j, t: (t, j))
    w_spec = pl.BlockSpec((FF_TILE, D_MODEL), lambda j, t: (j, 0))
    grads = pl.pallas_call(
        body,
        out_shape=[jax.ShapeDtypeStruct((D_FF, D_MODEL), BF)] * 3,
        grid=(N_FF_TILES, nt), in_specs=[row, row, blk, blk, blk], out_specs=[w_spec] * 3,
        scratch_shapes=[pltpu.VMEM((FF_TILE, D_MODEL), F32)] * 3,
        name=name, compiler_params=_params(2))(h, dyb, act, d_gate, d_up)
    return [a.reshape(N_DEV, FF_BLOCK, D_MODEL) for a in grads]


def _in_proj_fwd(x, g, w_in, name):
    t_len = x.shape[0]
    tm = min(512, t_len)

    def body(x_ref, g_ref, w_ref, h_ref, p_ref):
        xv = x_ref[...]
        h = (xv * _row_rms(xv) * g_ref[...]).astype(BF)
        h_ref[...] = h
        p_ref[...] = _nt(h, w_ref[...])

    row = pl.BlockSpec((tm, D_MODEL), lambda t: (t, 0))
    return pl.pallas_call(
        body, out_shape=[jax.ShapeDtypeStruct((t_len, D_MODEL), BF), jax.ShapeDtypeStruct((t_len, IN_COLS), F32)],
        grid=(t_len // tm,),
        in_specs=[row, pl.BlockSpec((1, D_MODEL), lambda t: (0, 0)), pl.BlockSpec((IN_COLS, D_MODEL), lambda t: (0, 0))],
        out_specs=[row, pl.BlockSpec((tm, IN_COLS), lambda t: (t, 0))], name=name, compiler_params=_params(1))(x, g, w_in)


def _in_proj_bwd(dq, dk, dv, du, dmq, w_in, h, x, g, dres, name):
    t_len = x.shape[0]
    tm = min(512, t_len)
    nt = t_len // tm
    groups = [(0, Q_COLS), (Q_COLS, KV_COLS), (Q_COLS + KV_COLS, KV_COLS), (Q_COLS + 2 * KV_COLS, 2 * CONV_CH),
              (Q_COLS + 2 * KV_COLS + 2 * CONV_CH, MQ_COLS)]

    def body(dq_ref, dk_ref, dv_ref, du_ref, dmq_ref, w_ref, h_ref, x_ref, g_ref, dres_ref, dx_ref, dg_ref, dw_ref, acc):
        t = pl.program_id(0)

        @pl.when(t == 0)
        def _():
            acc[...] = jnp.zeros_like(acc)
            dg_ref[...] = jnp.zeros_like(dg_ref)

        h = h_ref[...]
        dh = jnp.zeros((tm, D_MODEL), F32)
        for (start, width), ref in zip(groups, (dq_ref, dk_ref, dv_ref, du_ref, dmq_ref)):
            piece = ref[...]
            dh = dh + _nn(piece, w_ref[start:start + width, :])
            acc[start:start + width, :] += _tn(piece, h)
        xv = x_ref[...]
        dx, dg = _rms_bwd(dh, xv, _row_rms(xv), g_ref[...])
        dx_ref[...] = dres_ref[...] + dx
        dg_ref[...] += dg

        @pl.when(t == nt - 1)
        def _():
            dw_ref[...] = acc[...].astype(BF)

    def cols(width):
        return pl.BlockSpec((tm, width), lambda t: (t, 0))

    row = cols(D_MODEL)
    vec = pl.BlockSpec((1, D_MODEL), lambda t: (0, 0))
    full = pl.BlockSpec((IN_COLS, D_MODEL), lambda t: (0, 0))
    return pl.pallas_call(
        body,
        out_shape=[jax.ShapeDtypeStruct((t_len, D_MODEL), F32), jax.ShapeDtypeStruct((1, D_MODEL), F32),
                   jax.ShapeDtypeStruct((IN_COLS, D_MODEL), BF)],
        grid=(nt,),
        in_specs=[cols(Q_COLS), cols(KV_COLS), cols(KV_COLS), cols(2 * CONV_CH), cols(MQ_COLS), full, row, row, vec, row],
        out_specs=[row, vec, full], scratch_shapes=[pltpu.VMEM((IN_COLS, D_MODEL), F32)], name=name,
        compiler_params=_params(1))(dq, dk, dv, du, dmq, w_in, h, x, g, dres)


def _attn_fwd(proj, cos_t, sin_t, gq2, gk2, sinks, name):
    t_len = proj.shape[0]
    nb = t_len // ATTN_BLOCK

    def body(q_ref, kvc_ref, kvp_ref, cq_ref, sq_ref, cp_ref, sp_ref, gq_ref, gk_ref, sk_ref, ya_ref, lse_ref):
        n = pl.program_id(0)
        cq, sq = cq_ref[...], sq_ref[...]
        c_all = jnp.concatenate([cp_ref[...], cq], axis=0)
        s_all = jnp.concatenate([sp_ref[...], sq], axis=0)
        kv = jnp.concatenate([kvp_ref[...], kvc_ref[...]], axis=0)
        k_raw, v = kv[:, :PAIR], kv[:, PAIR:]
        kn = _rope(k_raw * _pair_rms(k_raw) * gk_ref[...], c_all, s_all)
        row = lax.broadcasted_iota(jnp.int32, (ATTN_BLOCK, 2 * ATTN_BLOCK), 0)
        col = lax.broadcasted_iota(jnp.int32, (ATTN_BLOCK, 2 * ATTN_BLOCK), 1)
        rel = row + ATTN_BLOCK - col
        valid = (rel >= 0) & (rel < ATTN_BLOCK) & ((col >= ATTN_BLOCK) | (n > 0))
        for pr in range(N_Q_HEADS // 2):
            q_raw = q_ref[:, pr * PAIR:(pr + 1) * PAIR]
            qn = _rope(q_raw * _pair_rms(q_raw) * gq_ref[...], cq, sq)
            kvh = (2 * pr) // Q_PER_KV
            kh = kn[:, kvh * HEAD_DIM:(kvh + 1) * HEAD_DIM].astype(BF)
            vh = v[:, kvh * HEAD_DIM:(kvh + 1) * HEAD_DIM].astype(BF)
            for hh in range(2):
                hd = 2 * pr + hh
                qh = qn[:, hh * HEAD_DIM:(hh + 1) * HEAD_DIM].astype(BF)
                s = jnp.where(valid, _nt(qh, kh) * SCALE, NEG)
                sink = sk_ref[0:1, hd:hd + 1]
                m = jnp.maximum(jnp.max(s, axis=-1, keepdims=True), sink)
                p = jnp.exp(s - m)
                den = jnp.sum(p, axis=-1, keepdims=True) + jnp.exp(sink - m)
                ya_ref[:, hd * HEAD_DIM:(hd + 1) * HEAD_DIM] = _nn((p / den).astype(BF), vh).astype(BF)
                lse_ref[:, hd:hd + 1] = m + jnp.log(den)

    prev = lambda n: (jnp.maximum(n - 1, 0), 0)
    tab = pl.BlockSpec((ATTN_BLOCK, PAIR), lambda n: (n, 0))
    tab_p = pl.BlockSpec((ATTN_BLOCK, PAIR), prev)
    one = lambda w: pl.BlockSpec((1, w), lambda n: (0, 0))
    return pl.pallas_call(
        body, out_shape=[jax.ShapeDtypeStruct((t_len, Q_COLS), BF), jax.ShapeDtypeStruct((t_len, N_Q_HEADS), F32)],
        grid=(nb,),
        in_specs=[pl.BlockSpec((ATTN_BLOCK, Q_COLS), lambda n: (n, 0)),
                  pl.BlockSpec((ATTN_BLOCK, 2 * KV_COLS), lambda n: (n, 2)),
                  pl.BlockSpec((ATTN_BLOCK, 2 * KV_COLS), lambda n: (jnp.maximum(n - 1, 0), 2)),
                  tab, tab, tab_p, tab_p, one(PAIR), one(PAIR), one(N_Q_HEADS)],
        out_specs=[pl.BlockSpec((ATTN_BLOCK, Q_COLS), lambda n: (n, 0)),
                   pl.BlockSpec((ATTN_BLOCK, N_Q_HEADS), lambda n: (n, 0))],
        name=name, compiler_params=_params(1))(proj, proj, proj, cos_t, sin_t, cos_t, sin_t, gq2, gk2, sinks)


def _attn_bwd_q(proj, cos_t, sin_t, gq2, gk2, sinks, ya, dmix, lse, name):
    t_len = proj.shape[0]
    nb = t_len // ATTN_BLOCK

    def body(q_ref, kvc_ref, kvp_ref, cq_ref, sq_ref, cp_ref, sp_ref, gq_ref, gk_ref, sk_ref, ya_ref, do_ref, lse_ref,
             dq_ref, dgq_ref, dsk_ref, scr):
        n = pl.program_id(0)

        @pl.when(n == 0)
        def _():
            dgq_ref[...] = jnp.zeros_like(dgq_ref)
            dsk_ref[...] = jnp.zeros_like(dsk_ref)

        cq, sq = cq_ref[...], sq_ref[...]
        c_all = jnp.concatenate([cp_ref[...], cq], axis=0)
        s_all = jnp.concatenate([sp_ref[...], sq], axis=0)
        kv = jnp.concatenate([kvp_ref[...], kvc_ref[...]], axis=0)
        k_raw, v = kv[:, :PAIR], kv[:, PAIR:]
        kn = _rope(k_raw * _pair_rms(k_raw) * gk_ref[...], c_all, s_all)
        row = lax.broadcasted_iota(jnp.int32, (ATTN_BLOCK, 2 * ATTN_BLOCK), 0)
        col = lax.broadcasted_iota(jnp.int32, (ATTN_BLOCK, 2 * ATTN_BLOCK), 1)
        rel = row + ATTN_BLOCK - col
        valid = (rel >= 0) & (rel < ATTN_BLOCK) & ((col >= ATTN_BLOCK) | (n > 0))
        gq = gq_ref[...]
        for pr in range(N_Q_HEADS // 2):
            q_raw = q_ref[:, pr * PAIR:(pr + 1) * PAIR]
            rq = _pair_rms(q_raw)
            qn = _rope(q_raw * rq * gq, cq, sq)
            kvh = (2 * pr) // Q_PER_KV
            kh = kn[:, kvh * HEAD_DIM:(kvh + 1) * HEAD_DIM].astype(BF)
            vh = v[:, kvh * HEAD_DIM:(kvh + 1) * HEAD_DIM].astype(BF)
            for hh in range(2):
                hd = 2 * pr + hh
                qh = qn[:, hh * HEAD_DIM:(hh + 1) * HEAD_DIM].astype(BF)
                lse_h = lse_ref[:, hd:hd + 1]
                p = jnp.exp(jnp.where(valid, _nt(qh, kh) * SCALE, NEG) - lse_h)
                d_o = do_ref[:, hd * HEAD_DIM:(hd + 1) * HEAD_DIM]
                o = ya_ref[:, hd * HEAD_DIM:(hd + 1) * HEAD_DIM].astype(F32)
                dsum = jnp.sum(d_o * o, axis=-1, keepdims=True)
                ds = p * (_nt(d_o.astype(BF), vh) - dsum)
                scr[:, hh * HEAD_DIM:(hh + 1) * HEAD_DIM] = _nn(ds.astype(BF), kh) * SCALE
                sink = sk_ref[0:1, hd:hd + 1]
                dsk_ref[0:1, hd:hd + 1] += -jnp.sum(jnp.exp(sink - lse_h) * dsum, axis=0, keepdims=True)
            dx, dg = _pair_rms_bwd(_rope_t(scr[...], cq, sq), q_raw, rq, gq)
            dq_ref[:, pr * PAIR:(pr + 1) * PAIR] = dx.astype(BF)
            dgq_ref[...] += dg

    prev = lambda n: (jnp.maximum(n - 1, 0), 0)
    tab = pl.BlockSpec((ATTN_BLOCK, PAIR), lambda n: (n, 0))
    tab_p = pl.BlockSpec((ATTN_BLOCK, PAIR), prev)
    one = lambda w: pl.BlockSpec((1, w), lambda n: (0, 0))
    qblk = pl.BlockSpec((ATTN_BLOCK, Q_COLS), lambda n: (n, 0))
    return pl.pallas_call(
        body,
        out_shape=[jax.ShapeDtypeStruct((t_len, Q_COLS), BF), jax.ShapeDtypeStruct((1, PAIR), F32),
                   jax.ShapeDtypeStruct((1, N_Q_HEADS), F32)],
        grid=(nb,),
        in_specs=[qblk, pl.BlockSpec((ATTN_BLOCK, 2 * KV_COLS), lambda n: (n, 2)),
                  pl.BlockSpec((ATTN_BLOCK, 2 * KV_COLS), lambda n: (jnp.maximum(n - 1, 0), 2)),
                  tab, tab, tab_p, tab_p, one(PAIR), one(PAIR), one(N_Q_HEADS), qblk, qblk,
                  pl.BlockSpec((ATTN_BLOCK, N_Q_HEADS), lambda n: (n, 0))],
        out_specs=[qblk, one(PAIR), one(N_Q_HEADS)], scratch_shapes=[pltpu.VMEM((ATTN_BLOCK, PAIR), F32)],
        name=name, compiler_params=_params(1))(proj, proj, proj, cos_t, sin_t, cos_t, sin_t, gq2, gk2, sinks, ya, dmix, lse)


def _attn_bwd_kv(proj, cos_t, sin_t, gq2, gk2, ya, dmix, lse, name):
    t_len = proj.shape[0]
    nb = t_len // ATTN_BLOCK

    def body(kv_ref, q0_ref, q1_ref, ck_ref, sk_ref, c1_ref, s1_ref, gq_ref, gk_ref, o0_ref, o1_ref, do0_ref, do1_ref,
             l0_ref, l1_ref, dk_ref, dv_ref, dgk_ref, dkn_scr, dv_scr):
        m = pl.program_id(0)

        @pl.when(m == 0)
        def _():
            dgk_ref[...] = jnp.zeros_like(dgk_ref)

        ck, sk = ck_ref[...], sk_ref[...]
        c_all = jnp.concatenate([ck, c1_ref[...]], axis=0)
        s_all = jnp.concatenate([sk, s1_ref[...]], axis=0)
        kvv = kv_ref[...]
        k_raw, v = kvv[:, :PAIR], kvv[:, PAIR:]
        rk = _pair_rms(k_raw)
        gk = gk_ref[...]
        kn = _rope(k_raw * rk * gk, ck, sk)
        row = lax.broadcasted_iota(jnp.int32, (2 * ATTN_BLOCK, ATTN_BLOCK), 0)
        col = lax.broadcasted_iota(jnp.int32, (2 * ATTN_BLOCK, ATTN_BLOCK), 1)
        valid = ((row < ATTN_BLOCK) & (row >= col)) | ((row >= ATTN_BLOCK) & (row - ATTN_BLOCK < col) & (m < nb - 1))
        lse2 = jnp.concatenate([l0_ref[...], l1_ref[...]], axis=0)
        dkn_scr[...] = jnp.zeros_like(dkn_scr)
        dv_scr[...] = jnp.zeros_like(dv_scr)
        for pr in range(N_Q_HEADS // 2):
            lanes = slice(pr * PAIR, (pr + 1) * PAIR)
            q_raw = jnp.concatenate([q0_ref[:, lanes], q1_ref[:, lanes]], axis=0)
            qn = _rope(q_raw * _pair_rms(q_raw) * gq_ref[...], c_all, s_all)
            d_o2 = jnp.concatenate([do0_ref[:, lanes], do1_ref[:, lanes]], axis=0)
            o2 = jnp.concatenate([o0_ref[:, lanes], o1_ref[:, lanes]], axis=0).astype(F32)
            dsum2 = _pair_sum(d_o2 * o2)
            kvh = (2 * pr) // Q_PER_KV
            khs = slice(kvh * HEAD_DIM, (kvh + 1) * HEAD_DIM)
            kh = kn[:, khs].astype(BF)
            vh = v[:, khs].astype(BF)
            for hh in range(2):
                hd = 2 * pr + hh
                hs = slice(hh * HEAD_DIM, (hh + 1) * HEAD_DIM)
                qh = qn[:, hs].astype(BF)
                d_oh = d_o2[:, hs].astype(BF)
                p = jnp.exp(jnp.where(valid, _nt(qh, kh) * SCALE, NEG) - lse2[:, hd:hd + 1])
                dv_scr[:, khs] += _tn(p.astype(BF), d_oh)
                ds = p * (_nt(d_oh, vh) - dsum2[:, hh * HEAD_DIM:hh * HEAD_DIM + 1])
                dkn_scr[:, khs] += _tn(ds.astype(BF), qh) * SCALE
        dx, dg = _pair_rms_bwd(_rope_t(dkn_scr[...], ck, sk), k_raw, rk, gk)
        dk_ref[...] = dx.astype(BF)
        dv_ref[...] = dv_scr[...].astype(BF)
        dgk_ref[...] += dg

    nxt = lambda m: (jnp.minimum(m + 1, nb - 1), 0)
    cur = lambda m: (m, 0)
    tab = lambda f: pl.BlockSpec((ATTN_BLOCK, PAIR), f)
    qb = lambda f: pl.BlockSpec((ATTN_BLOCK, Q_COLS), f)
    lb = lambda f: pl.BlockSpec((ATTN_BLOCK, N_Q_HEADS), f)
    one = pl.BlockSpec((1, PAIR), lambda m: (0, 0))
    return pl.pallas_call(
        body,
        out_shape=[jax.ShapeDtypeStruct((t_len, KV_COLS), BF), jax.ShapeDtypeStruct((t_len, KV_COLS), BF),
                   jax.ShapeDtypeStruct((1, PAIR), F32)],
        grid=(nb,),
        in_specs=[pl.BlockSpec((ATTN_BLOCK, 2 * KV_COLS), lambda m: (m, 2)), qb(cur), qb(nxt), tab(cur), tab(cur), tab(nxt),
                  tab(nxt), one, one, qb(cur), qb(nxt), qb(cur), qb(nxt), lb(cur), lb(nxt)],
        out_specs=[tab(cur), tab(cur), one],
        scratch_shapes=[pltpu.VMEM((ATTN_BLOCK, PAIR), F32), pltpu.VMEM((ATTN_BLOCK, PAIR), F32)],
        name=name, compiler_params=_params(1))(proj, proj, proj, cos_t, sin_t, cos_t, sin_t, gq2, gk2, ya, ya, dmix, dmix,
                                               lse, lse)


def _conv_taps(blk, w, offset_of):
    acc = jnp.zeros((CONV_CHUNK, CONV_CH), F32)
    for k in range(CONV_WIDTH):
        o = offset_of(k)
        acc = acc + w[k:k + 1, :] * blk[o:o + CONV_CHUNK, :]
    return acc


def _conv_fwd(proj, w_dw, b_dw, g_ln, b_ln, name):
    t_len = proj.shape[0]
    nc = t_len // CONV_CHUNK

    def body(a_ref, gt_ref, w_ref, b_ref, g_ref, bl_ref, yc_ref, cv_ref, pad):
        pad[0:CONV_PAD, :] = jnp.zeros((CONV_PAD, CONV_CH), F32)

        def glu(c, carry):
            rows = pl.ds(pl.multiple_of(c * CONV_CHUNK, CONV_CHUNK), CONV_CHUNK)
            dst = pl.ds(pl.multiple_of(c * CONV_CHUNK + CONV_PAD, CONV_PAD), CONV_CHUNK)
            pad[dst, :] = a_ref[rows, :] * jax.nn.sigmoid(gt_ref[rows, :])
            return carry

        lax.fori_loop(0, nc, glu, 0)
        w = w_ref[...]

        def conv(c, carry):
            base = pl.multiple_of(c * CONV_CHUNK, CONV_CHUNK)
            blk = pad[pl.ds(base, CONV_CHUNK + CONV_PAD), :]
            cv = _conv_taps(blk, w, lambda k: k + CONV_PAD - (CONV_WIDTH - 1)) + b_ref[...]
            mu = jnp.mean(cv, axis=-1, keepdims=True)
            xc = cv - mu
            z = xc * lax.rsqrt(jnp.mean(xc * xc, axis=-1, keepdims=True) + EPS) * g_ref[...] + bl_ref[...]
            rows = pl.ds(base, CONV_CHUNK)
            cv_ref[rows, :] = cv
            yc_ref[rows, :] = (z * jax.nn.sigmoid(z)).astype(BF)
            return carry

        lax.fori_loop(0, nc, conv, 0)

    vec = pl.BlockSpec((1, CONV_CH), lambda i: (0, 0))
    full = pl.BlockSpec((t_len, CONV_CH), lambda i: (0, 0))
    return pl.pallas_call(
        body, out_shape=[jax.ShapeDtypeStruct((t_len, CONV_CH), BF), jax.ShapeDtypeStruct((t_len, CONV_CH), F32)],
        grid=(1,),
        in_specs=[pl.BlockSpec((t_len, CONV_CH), lambda i: (0, 3)), pl.BlockSpec((t_len, CONV_CH), lambda i: (0, 4)),
                  pl.BlockSpec((CONV_WIDTH, CONV_CH), lambda i: (0, 0)), vec, vec, vec],
        out_specs=[full, full], scratch_shapes=[pltpu.VMEM((t_len + CONV_PAD, CONV_CH), F32)], name=name,
        compiler_params=_params(1))(proj, proj, w_dw, b_dw, g_ln, b_ln)


def _conv_bwd(proj, cv, dmix, w_dw, g_ln, b_ln, name):
    t_len = proj.shape[0]
    nc = t_len // CONV_CHUNK

    def body(a_ref, gt_ref, cv_ref, dy_ref, w_ref, g_ref, bl_ref, du_ref, dw_ref, db_ref, dgl_ref, dbl_ref, pad, dpad):
        pad[0:CONV_PAD, :] = jnp.zeros((CONV_PAD, CONV_CH), F32)
        dpad[t_len:t_len + CONV_PAD, :] = jnp.zeros((CONV_PAD, CONV_CH), F32)
        dw_ref[...] = jnp.zeros_like(dw_ref)
        db_ref[...] = jnp.zeros_like(db_ref)
        dgl_ref[...] = jnp.zeros_like(dgl_ref)
        dbl_ref[...] = jnp.zeros_like(dbl_ref)

        def norm_bwd(c, carry):
            base = pl.multiple_of(c * CONV_CHUNK, CONV_CHUNK)
            rows = pl.ds(base, CONV_CHUNK)
            dst = pl.ds(pl.multiple_of(c * CONV_CHUNK + CONV_PAD, CONV_PAD), CONV_CHUNK)
            pad[dst, :] = a_ref[rows, :] * jax.nn.sigmoid(gt_ref[rows, :])
            cvv = cv_ref[rows, :]
            xc = cvv - jnp.mean(cvv, axis=-1, keepdims=True)
            rs = lax.rsqrt(jnp.mean(xc * xc, axis=-1, keepdims=True) + EPS)
            xhat = xc * rs
            z = xhat * g_ref[...] + bl_ref[...]
            sg = jax.nn.sigmoid(z)
            dz = dy_ref[rows, :] * (sg * (1.0 + z * (1.0 - sg)))
            dgl_ref[...] += jnp.sum(dz * xhat, axis=0, keepdims=True)
            dbl_ref[...] += jnp.sum(dz, axis=0, keepdims=True)
            dxh = dz * g_ref[...]
            dcv = rs * (dxh - jnp.mean(dxh, axis=-1, keepdims=True) - xhat * jnp.mean(dxh * xhat, axis=-1, keepdims=True))
            db_ref[...] += jnp.sum(dcv, axis=0, keepdims=True)
            dpad[rows, :] = dcv
            return carry

        lax.fori_loop(0, nc, norm_bwd, 0)
        w = w_ref[...]

        def conv_bwd(c, carry):
            base = pl.multiple_of(c * CONV_CHUNK, CONV_CHUNK)
            rows = pl.ds(base, CONV_CHUNK)
            dblk = dpad[pl.ds(base, CONV_CHUNK + CONV_PAD), :]
            dhc = _conv_taps(dblk, w, lambda k: CONV_WIDTH - 1 - k)
            a = a_ref[rows, :]
            sg = jax.nn.sigmoid(gt_ref[rows, :])
            du_ref[rows, 0:CONV_CH] = (dhc * sg).astype(BF)
            du_ref[rows, CONV_CH:2 * CONV_CH] = (dhc * a * sg * (1.0 - sg)).astype(BF)
            hblk = pad[pl.ds(base, CONV_CHUNK + CONV_PAD), :]
            dcv = dblk[0:CONV_CHUNK, :]
            for k in range(CONV_WIDTH):
                o = k + CONV_PAD - (CONV_WIDTH - 1)
                dw_ref[k:k + 1, :] += jnp.sum(dcv * hblk[o:o + CONV_CHUNK, :], axis=0, keepdims=True)
            return carry

        lax.fori_loop(0, nc, conv_bwd, 0)

    vec = pl.BlockSpec((1, CONV_CH), lambda i: (0, 0))
    full = pl.BlockSpec((t_len, CONV_CH), lambda i: (0, 0))
    wspec = pl.BlockSpec((CONV_WIDTH, CONV_CH), lambda i: (0, 0))
    vshape = jax.ShapeDtypeStruct((1, CONV_CH), F32)
    return pl.pallas_call(
        body,
        out_shape=[jax.ShapeDtypeStruct((t_len, 2 * CONV_CH), BF), jax.ShapeDtypeStruct((CONV_WIDTH, CONV_CH), F32),
                   vshape, vshape, vshape],
        grid=(1,),
        in_specs=[pl.BlockSpec((t_len, CONV_CH), lambda i: (0, 3)), pl.BlockSpec((t_len, CONV_CH), lambda i: (0, 4)), full,
                  pl.BlockSpec((t_len, CONV_CH), lambda i: (0, 2)), wspec, vec, vec],
        out_specs=[pl.BlockSpec((t_len, 2 * CONV_CH), lambda i: (0, 0)), wspec, vec, vec, vec],
        scratch_shapes=[pltpu.VMEM((t_len + CONV_PAD, CONV_CH), F32), pltpu.VMEM((t_len + CONV_PAD, CONV_CH), F32)],
        name=name, compiler_params=_params(1))(proj, proj, cv, dmix, w_dw, g_ln, b_ln)


def _mem_kv_fwd(mem, g, w, name):
    def body(m_ref, g_ref, w_ref, h_ref, kv_ref):
        mv = m_ref[...]
        h = (mv * _row_rms(mv) * g_ref[...]).astype(BF)
        h_ref[...] = h
        kv_ref[...] = _nn(h, w_ref[...])

    m_len = mem.shape[0]
    return pl.pallas_call(
        body, out_shape=[jax.ShapeDtypeStruct((m_len, D_MODEL), BF), jax.ShapeDtypeStruct((m_len, 2 * MQ_COLS), F32)],
        name=name, compiler_params=pltpu.CompilerParams(vmem_limit_bytes=VMEM_LIMIT_BYTES))(mem, g, w)


def _mem_kv_bwd(mem, g, h, w, dkv, name):
    def body(m_ref, g_ref, h_ref, w_ref, dkv_ref, dw_ref, dg_ref):
        dkv_b = dkv_ref[...].astype(BF)
        dw_ref[...] = _tn(h_ref[...], dkv_b).astype(BF)
        mv = m_ref[...]
        dg_ref[...] = jnp.sum(_nt(dkv_b, w_ref[...]) * mv * _row_rms(mv), axis=0, keepdims=True)

    return pl.pallas_call(
        body, out_shape=[jax.ShapeDtypeStruct((D_MODEL, 2 * MQ_COLS), BF), jax.ShapeDtypeStruct((1, D_MODEL), F32)],
        name=name, compiler_params=pltpu.CompilerParams(vmem_limit_bytes=VMEM_LIMIT_BYTES))(mem, g, h, w, dkv)


def _mem_attn_fwd(proj, mkv, gq2, gk2, name):
    t_len = proj.shape[0]
    tm = min(256, t_len)

    def body(q_ref, kv_ref, gq_ref, gk_ref, ym_ref, lse_ref):
        kvv = kv_ref[...]
        for pr in range(N_MEM_HEADS // 2):
            lanes = slice(pr * PAIR, (pr + 1) * PAIR)
            k_raw = kvv[:, lanes]
            kn = k_raw * _pair_rms(k_raw) * gk_ref[...]
            v = kvv[:, MQ_COLS + pr * PAIR:MQ_COLS + (pr + 1) * PAIR]
            q_raw = q_ref[:, lanes]
            qn = q_raw * _pair_rms(q_raw) * gq_ref[...]
            for hh in range(2):
                hd = 2 * pr + hh
                hs = slice(hh * HEAD_DIM, (hh + 1) * HEAD_DIM)
                s = _nt(qn[:, hs].astype(BF), kn[:, hs].astype(BF)) * SCALE
                m = jnp.max(s, axis=-1, keepdims=True)
                p = jnp.exp(s - m)
                den = jnp.sum(p, axis=-1, keepdims=True)
                ym_ref[:, hd * HEAD_DIM:(hd + 1) * HEAD_DIM] = _nn((p / den).astype(BF), v[:, hs].astype(BF)).astype(BF)
                lse_ref[:, hd:hd + 1] = m + jnp.log(den)

    m_len = mkv.shape[0]
    one = pl.BlockSpec((1, PAIR), lambda t: (0, 0))
    return pl.pallas_call(
        body, out_shape=[jax.ShapeDtypeStruct((t_len, MQ_COLS), BF), jax.ShapeDtypeStruct((t_len, N_MEM_HEADS), F32)],
        grid=(t_len // tm,),
        in_specs=[pl.BlockSpec((tm, MQ_COLS), lambda t: (t, 5)), pl.BlockSpec((m_len, 2 * MQ_COLS), lambda t: (0, 0)), one, one],
        out_specs=[pl.BlockSpec((tm, MQ_COLS), lambda t: (t, 0)), pl.BlockSpec((tm, N_MEM_HEADS), lambda t: (t, 0))],
        name=name, compiler_params=_params(1))(proj, mkv, gq2, gk2)


def _mem_attn_bwd(proj, mkv, gq2, gk2, ym, dmix, lse, name):
    t_len = proj.shape[0]
    tm = min(256, t_len)
    nt = t_len // tm
    m_len = mkv.shape[0]

    def body(q_ref, kv_ref, gq_ref, gk_ref, ym_ref, do_ref, lse_ref, dq_ref, dkv_ref, dgq_ref, dgk_ref, dkn_scr, dv_scr, scr):
        t = pl.program_id(0)

        @pl.when(t == 0)
        def _():
            dkn_scr[...] = jnp.zeros_like(dkn_scr)
            dv_scr[...] = jnp.zeros_like(dv_scr)
            dgq_ref[...] = jnp.zeros_like(dgq_ref)

        kvv = kv_ref[...]
        for pr in range(N_MEM_HEADS // 2):
            lanes = slice(pr * PAIR, (pr + 1) * PAIR)
            k_raw = kvv[:, lanes]
            kn = k_raw * _pair_rms(k_raw) * gk_ref[...]
            v = kvv[:, MQ_COLS + pr * PAIR:MQ_COLS + (pr + 1) * PAIR]
            q_raw = q_ref[:, lanes]
            rq = _pair_rms(q_raw)
            qn = q_raw * rq * gq_ref[...]
            d_o = do_ref[:, lanes]
            dsum = _pair_sum(d_o * ym_ref[:, lanes].astype(F32))
            for hh in range(2):
                hd = 2 * pr + hh
                hs = slice(hh * HEAD_DIM, (hh + 1) * HEAD_DIM)
                cols = slice(hd * HEAD_DIM, (hd + 1) * HEAD_DIM)
                qh = qn[:, hs].astype(BF)
                kh = kn[:, hs].astype(BF)
                d_oh = d_o[:, hs].astype(BF)
                p = jnp.exp(_nt(qh, kh) * SCALE - lse_ref[:, hd:hd + 1])
                dv_scr[:, cols] += _tn(p.astype(BF), d_oh)
                ds = (p * (_nt(d_oh, v[:, hs].astype(BF)) - dsum[:, hh * HEAD_DIM:hh * HEAD_DIM + 1])).astype(BF)
                scr[:, hs] = _nn(ds, kh) * SCALE
                dkn_scr[:, cols] += _tn(ds, qh) * SCALE
            dx, dg = _pair_rms_bwd(scr[...], q_raw, rq, gq_ref[...])
            dq_ref[:, lanes] = dx.astype(BF)
            dgq_ref[...] += dg

        @pl.when(t == nt - 1)
        def _():
            dgk = jnp.zeros((1, PAIR), F32)
            for pr in range(N_MEM_HEADS // 2):
                lanes = slice(pr * PAIR, (pr + 1) * PAIR)
                k_raw = kvv[:, lanes]
                dx, dg = _pair_rms_bwd(dkn_scr[:, lanes], k_raw, _pair_rms(k_raw), gk_ref[...])
                dkv_ref[:, lanes] = dx
                dgk = dgk + dg
            dkv_ref[:, MQ_COLS:2 * MQ_COLS] = dv_scr[...]
            dgk_ref[...] = dgk

    one = pl.BlockSpec((1, PAIR), lambda t: (0, 0))
    kvspec = pl.BlockSpec((m_len, 2 * MQ_COLS), lambda t: (0, 0))
    qspec = pl.BlockSpec((tm, MQ_COLS), lambda t: (t, 0))
    return pl.pallas_call(
        body,
        out_shape=[jax.ShapeDtypeStruct((t_len, MQ_COLS), BF), jax.ShapeDtypeStruct((m_len, 2 * MQ_COLS), F32),
                   jax.ShapeDtypeStruct((1, PAIR), F32), jax.ShapeDtypeStruct((1, PAIR), F32)],
        grid=(nt,),
        in_specs=[pl.BlockSpec((tm, MQ_COLS), lambda t: (t, 5)), kvspec, one, one, qspec,
                  pl.BlockSpec((tm, MQ_COLS), lambda t: (t, 3)), pl.BlockSpec((tm, N_MEM_HEADS), lambda t: (t, 0))],
        out_specs=[qspec, kvspec, one, one],
        scratch_shapes=[pltpu.VMEM((m_len, MQ_COLS), F32), pltpu.VMEM((m_len, MQ_COLS), F32), pltpu.VMEM((tm, PAIR), F32)],
        name=name, compiler_params=_params(1))(proj, mkv, gq2, gk2, ym, dmix, lse)


MIX_GROUPS = [(0, Q_COLS), (Q_COLS, CONV_CH), (Q_COLS + CONV_CH, MQ_COLS)]


def _out_proj_fwd(x, ya, yc, ym, w_out, name, after=None):
    t_len = x.shape[0]
    tm = min(512, t_len)

    def body(x_ref, ya_ref, yc_ref, ym_ref, w_ref, xo_ref):
        y = x_ref[...]
        for (start, width), ref in zip(MIX_GROUPS, (ya_ref, yc_ref, ym_ref)):
            y = y + _nn(ref[...], w_ref[start:start + width, :])
        xo_ref[...] = y

    cols = lambda w: pl.BlockSpec((tm, w), lambda t: (t, 0))
    body, first, first_specs = _ordered_after(body, after)
    return pl.pallas_call(
        body, out_shape=jax.ShapeDtypeStruct((t_len, D_MODEL), F32), grid=(t_len // tm,),
        in_specs=first_specs + [cols(D_MODEL), cols(Q_COLS), cols(CONV_CH), cols(MQ_COLS),
                                pl.BlockSpec((D_MODEL, D_MODEL), lambda t: (0, 0))],
        out_specs=cols(D_MODEL), name=name, compiler_params=_params(1))(*first, x, ya, yc, ym, w_out)


def _out_proj_bwd(dx, ya, yc, ym, w_out, name, after=None):
    t_len = dx.shape[0]
    tm = min(512, t_len)
    nt = t_len // tm

    def body(dx_ref, ya_ref, yc_ref, ym_ref, w_ref, dmix_ref, dw_ref, acc):
        t = pl.program_id(0)

        @pl.when(t == 0)
        def _():
            acc[...] = jnp.zeros_like(acc)

        dxb = dx_ref[...].astype(BF)
        dmix_ref[...] = _nt(dxb, w_ref[...])
        for (start, width), ref in zip(MIX_GROUPS, (ya_ref, yc_ref, ym_ref)):
            acc[start:start + width, :] += _tn(ref[...], dxb)

        @pl.when(t == nt - 1)
        def _():
            dw_ref[...] = acc[...].astype(BF)

    cols = lambda w: pl.BlockSpec((tm, w), lambda t: (t, 0))
    full = pl.BlockSpec((D_MODEL, D_MODEL), lambda t: (0, 0))
    body, first, first_specs = _ordered_after(body, after)
    return pl.pallas_call(
        body, out_shape=[jax.ShapeDtypeStruct((t_len, D_MODEL), F32), jax.ShapeDtypeStruct((D_MODEL, D_MODEL), BF)],
        grid=(nt,), in_specs=first_specs + [cols(D_MODEL), cols(Q_COLS), cols(CONV_CH), cols(MQ_COLS), full],
        out_specs=[cols(D_MODEL), full], scratch_shapes=[pltpu.VMEM((D_MODEL, D_MODEL), F32)], name=name,
        compiler_params=_params(1))(*first, dx, ya, yc, ym, w_out)


N_PEERS = N_DEV - 1
HBM_SPEC = pl.BlockSpec(memory_space=pltpu.HBM)
SEM_SPEC = pl.BlockSpec(memory_space=pltpu.SEMAPHORE)
EFFECT = pltpu.SideEffectType.DATAFLOW_SIDE_EFFECTING


def _my_place():
    x, y, c = lax.axis_index("x"), lax.axis_index("y"), lax.axis_index("c")
    return x, y, c, 4 * x + 2 * y + c


def _peer(k):
    x, y, c, _ = _my_place()
    px = 1 - x if k & 4 else x
    py = 1 - y if k & 2 else y
    pc = 1 - c if k & 1 else c
    return (px, py, pc), 4 * px + 2 * py + pc


def _gather_two_level(shards, name):
    n = len(shards)

    def body(*refs):
        ins, outs = refs[:n], refs[n:2 * n]
        send_sems, recv_sems, local_sems = refs[2 * n:]
        x, y, c, me = _my_place()
        sibling = (x, y, 1 - c)
        chips = [(1 - x, y), (x, 1 - y), (1 - x, 1 - y)]
        slot = lambda px, py, pc: 4 * px + 2 * py + pc

        def copy(w, k, src, dst_slot, to):
            return pltpu.make_async_remote_copy(
                src_ref=src, dst_ref=outs[w].at[dst_slot], send_sem=send_sems.at[w * N_PEERS + k],
                recv_sem=recv_sems.at[w * N_PEERS + k], device_id=to, device_id_type=pl.DeviceIdType.MESH)

        local = [pltpu.make_async_copy(ins[w], outs[w].at[me], local_sems.at[w]) for w in range(n)]
        for cp in local:
            cp.start()
        sends = []
        for w in range(n):
            for j, chip in enumerate(chips):
                sends.append(copy(w, 1 + j, ins[w], me, (*chip, c)))
            sends.append(copy(w, 0, ins[w], me, sibling))
        for cp in sends:
            cp.start()
        for w in range(n):
            for j, chip in enumerate(chips):
                got = slot(*chip, c)
                copy(w, 1 + j, ins[w], got, (*chip, c)).wait_recv()
                fwd = copy(w, 4 + j, outs[w].at[got], got, sibling)
                fwd.start()
                sends.append(fwd)
        for w in range(n):
            copy(w, 0, ins[w], slot(x, y, 1 - c), sibling).wait_recv()
            for j, chip in enumerate(chips):
                copy(w, 4 + j, ins[w], slot(*chip, 1 - c), sibling).wait_recv()
        for cp in sends:
            cp.wait_send()
        for cp in local:
            cp.wait()

    hbm = pl.BlockSpec(memory_space=pl.ANY)
    return pl.pallas_call(
        body, out_shape=[jax.ShapeDtypeStruct((N_DEV,) + a.shape, a.dtype) for a in shards], in_specs=[hbm] * n,
        out_specs=[hbm] * n,
        scratch_shapes=[pltpu.SemaphoreType.DMA((n * N_PEERS,)), pltpu.SemaphoreType.DMA((n * N_PEERS,)),
                        pltpu.SemaphoreType.DMA((n,))],
        name=name)(*shards)


def _split_copies(srcs, lands, gather, send_sems, recv_sems):
    _, _, _, me = _my_place()
    pairs = []
    for w in range(len(srcs)):
        for k in range(1, N_DEV):
            dev, idx = _peer(k)
            src = srcs[w] if gather[w] else srcs[w].at[idx]
            sems = dict(send_sem=send_sems.at[w * N_PEERS + k - 1], recv_sem=recv_sems.at[w * N_PEERS + k - 1], device_id=dev,
                        device_id_type=pl.DeviceIdType.MESH)
            pairs.append((pltpu.make_async_remote_copy(src_ref=src, dst_ref=lands[w].at[me], **sems),
                          pltpu.make_async_remote_copy(src_ref=src, dst_ref=lands[w].at[idx], **sems)))
    return pairs


def _exchange_start(srcs, lands, gather, after, name):
    n = len(srcs)

    def body(*refs):
        src_refs, land_refs = refs[:n], refs[n:2 * n]
        send_sems, recv_sems = refs[2 * n + 1], refs[2 * n + 2]
        token = refs[-1]
        for out_going, _ in _split_copies(src_refs, land_refs, gather, send_sems, recv_sems):
            out_going.start()
        token[...] = jnp.zeros_like(token)

    arrays = list(srcs) + list(lands)
    out = pl.pallas_call(
        body, name=name,
        out_shape=(pltpu.SemaphoreType.DMA((n * N_PEERS,)), pltpu.SemaphoreType.DMA((n * N_PEERS,)),
                   *[pltpu.HBM(a.shape, a.dtype) for a in arrays], jax.ShapeDtypeStruct((8, 128), F32)),
        in_specs=[HBM_SPEC] * (2 * n) + [pl.BlockSpec(memory_space=pl.ANY)],
        out_specs=(SEM_SPEC, SEM_SPEC, *[HBM_SPEC] * (2 * n), pl.BlockSpec(memory_space=pltpu.VMEM)),
        input_output_aliases={i: i + 2 for i in range(2 * n)},
        compiler_params=pltpu.CompilerParams(has_side_effects=EFFECT),
    )(*[pltpu.with_memory_space_constraint(a, pltpu.HBM) for a in arrays], after)
    return out[0], out[1], out[2:2 + n], out[2 + n:2 + 2 * n], out[-1]


def _exchange_wait(send_sems, recv_sems, srcs, lands, gather, after, name):
    n = len(srcs)

    def body(*refs):
        src_refs, land_refs = refs[:n], refs[n:2 * n]
        for out_going, arriving in _split_copies(src_refs, land_refs, gather, refs[2 * n], refs[2 * n + 1]):
            out_going.wait_send()
            arriving.wait_recv()

    arrays = list(srcs) + list(lands)
    out = pl.pallas_call(
        body, name=name, out_shape=tuple(pltpu.HBM(a.shape, a.dtype) for a in arrays),
        in_specs=[HBM_SPEC] * (2 * n) + [SEM_SPEC, SEM_SPEC, pl.BlockSpec(memory_space=pl.ANY)],
        out_specs=tuple([HBM_SPEC] * (2 * n)), input_output_aliases={i: i for i in range(2 * n)},
        compiler_params=pltpu.CompilerParams(has_side_effects=EFFECT),
    )(*arrays, send_sems, recv_sems, after)
    return out[n:]


SIBLING = 1
CHIP_PEERS = (2, 4, 6)
NEAR_PEERS = (SIBLING,) + CHIP_PEERS


def _near_copies(srcs, lands, send_sems, recv_sems):
    _, _, _, me = _my_place()
    pairs = []
    for w in range(len(srcs)):
        for i, k in enumerate(NEAR_PEERS):
            dev, idx = _peer(k)
            sems = dict(send_sem=send_sems.at[w * len(NEAR_PEERS) + i], recv_sem=recv_sems.at[w * len(NEAR_PEERS) + i],
                        device_id=dev, device_id_type=pl.DeviceIdType.MESH)
            pairs.append((pltpu.make_async_remote_copy(src_ref=srcs[w], dst_ref=lands[w].at[me], **sems),
                          pltpu.make_async_remote_copy(src_ref=srcs[w], dst_ref=lands[w].at[idx], **sems)))
    return pairs


def _forward_copies(lands, send_sems, recv_sems):
    sibling, _ = _peer(SIBLING)
    pairs = []
    for w in range(len(lands)):
        for i, k in enumerate(CHIP_PEERS):
            _, got = _peer(k)
            _, gets = _peer(k | SIBLING)
            sems = dict(send_sem=send_sems.at[w * len(CHIP_PEERS) + i], recv_sem=recv_sems.at[w * len(CHIP_PEERS) + i],
                        device_id=sibling, device_id_type=pl.DeviceIdType.MESH)
            pairs.append((pltpu.make_async_remote_copy(src_ref=lands[w].at[got], dst_ref=lands[w].at[got], **sems),
                          pltpu.make_async_remote_copy(src_ref=lands[w].at[got], dst_ref=lands[w].at[gets], **sems)))
    return pairs


def _split_call(body, name, arrays, sems_in, sems_out, after, token):
    n = len(arrays)
    out = pl.pallas_call(
        body, name=name,
        out_shape=(*[pltpu.SemaphoreType.DMA((c,)) for c in sems_out], *[pltpu.HBM(a.shape, a.dtype) for a in arrays],
                   *([jax.ShapeDtypeStruct((8, 128), F32)] if token else [])),
        in_specs=[HBM_SPEC] * n + [SEM_SPEC] * len(sems_in) + [pl.BlockSpec(memory_space=pl.ANY)],
        out_specs=(*[SEM_SPEC] * len(sems_out), *[HBM_SPEC] * n,
                   *([pl.BlockSpec(memory_space=pltpu.VMEM)] if token else [])),
        input_output_aliases={i: i + len(sems_out) for i in range(n)},
        compiler_params=pltpu.CompilerParams(has_side_effects=EFFECT),
    )(*[pltpu.with_memory_space_constraint(a, pltpu.HBM) for a in arrays], *sems_in, after)
    k = len(sems_out)
    return list(out[:k]), list(out[k:k + n]), (out[-1] if token else None)


def _gather_start(srcs, lands, after, name):
    n = len(srcs)

    def body(*refs):
        send1, recv1 = refs[2 * n + 1], refs[2 * n + 2]
        for out_going, _ in _near_copies(refs[:n], refs[n:2 * n], send1, recv1):
            out_going.start()
        refs[-1][...] = jnp.zeros_like(refs[-1])

    sems, arrays, token = _split_call(body, name, list(srcs) + list(lands), [], [n * len(NEAR_PEERS)] * 2, after, True)
    return sems, arrays, token


def _gather_relay(sems, arrays, after, name):
    n = len(arrays) // 2

    def body(*refs):
        send1, recv1 = refs[2 * n], refs[2 * n + 1]
        send2, recv2 = refs[2 * n + 3], refs[2 * n + 4]
        near = _near_copies(refs[:n], refs[n:2 * n], send1, recv1)
        forward = _forward_copies(refs[n:2 * n], send2, recv2)
        for w in range(n):
            for i in range(len(CHIP_PEERS)):
                near[w * len(NEAR_PEERS) + 1 + i][1].wait_recv()
                forward[w * len(CHIP_PEERS) + i][0].start()
        refs[-1][...] = jnp.zeros_like(refs[-1])

    sems2, arrays, token = _split_call(body, name, arrays, sems, [n * len(CHIP_PEERS)] * 2, after, True)
    return sems + sems2, arrays, token


def _gather_finish(sems, arrays, after, name):
    n = len(arrays) // 2

    def body(*refs):
        send1, recv1, send2, recv2 = refs[2 * n:2 * n + 4]
        near = _near_copies(refs[:n], refs[n:2 * n], send1, recv1)
        for out_going, _ in near:
            out_going.wait_send()
        for w in range(n):
            near[w * len(NEAR_PEERS)][1].wait_recv()
        for out_going, arriving in _forward_copies(refs[n:2 * n], send2, recv2):
            out_going.wait_send()
            arriving.wait_recv()

    _, arrays, _ = _split_call(body, name, arrays, sems, [], after, False)
    return arrays[n:]


def _own_slot(a, me, gather):
    mine = a if gather else lax.dynamic_index_in_dim(a, me, 0, keepdims=False)
    return lax.dynamic_update_index_in_dim(lax.empty((N_DEV,) + mine.shape, mine.dtype), mine, me, 0)


def _adamw_math(w, g, m, v):
    m2 = ADAM_B1 * m + (1.0 - ADAM_B1) * g
    v2 = ADAM_B2 * v + (1.0 - ADAM_B2) * (g * g)
    m_hat = m2 / (1.0 - ADAM_B1 ** ADAM_STEP)
    v_hat = v2 / (1.0 - ADAM_B2 ** ADAM_STEP)
    return -ADAM_LR * (m_hat / (jnp.sqrt(v_hat) + ADAM_EPS) + ADAM_WD * w), m2, v2


def _sum_adamw(parts, w, m, v, name):
    rows, cols = w.shape
    tr = rows if rows <= 512 else 256

    def body(p_ref, w_ref, m_ref, v_ref, g_ref, d_ref, m2_ref, v2_ref):
        g = p_ref[0].astype(F32)
        for s in range(1, N_DEV):
            g = g + p_ref[s].astype(F32)
        g_ref[...] = g
        d_ref[...], m2_ref[...], v2_ref[...] = _adamw_math(w_ref[...], g, m_ref[...], v_ref[...])

    blk = pl.BlockSpec((tr, cols), lambda i: (i, 0))
    shape = jax.ShapeDtypeStruct((rows, cols), F32)
    return pl.pallas_call(
        body, out_shape=[shape] * 4, grid=(rows // tr,),
        in_specs=[pl.BlockSpec((N_DEV, tr, cols), lambda i: (0, i, 0)), blk, blk, blk], out_specs=[blk] * 4, name=name,
        compiler_params=_params(1))(parts, w, m, v)


def _slot_sum(parts, name):
    def body(p_ref, g_ref):
        g = p_ref[0]
        for s in range(1, N_DEV):
            g = g + p_ref[s]
        g_ref[...] = g

    return pl.pallas_call(body, out_shape=jax.ShapeDtypeStruct(parts.shape[1:], F32), name=name)(parts)


def _adamw(g, w, m, v, name):
    def body(g_ref, w_ref, m_ref, v_ref, d_ref, m2_ref, v2_ref):
        d_ref[...], m2_ref[...], v2_ref[...] = _adamw_math(w_ref[...], g_ref[...], m_ref[...], v_ref[...])

    shape = jax.ShapeDtypeStruct(w.shape, F32)
    return pl.pallas_call(body, out_shape=[shape] * 3, name=name)(g, w, m, v)


def _pack_small(vals):
    flat = jnp.concatenate([vals[k].reshape(-1) for k in SMALL])
    return jnp.pad(flat, (0, SMALL_ROWS * D_MODEL - flat.shape[0])).reshape(SMALL_ROWS, D_MODEL)


def _unpack_small(packed, like):
    flat, out, at = packed.reshape(-1), {}, 0
    for k in SMALL:
        size = like[k].size
        out[k] = flat[at:at + size].reshape(like[k].shape)
        at += size
    return out


FFN1 = ["w_ffn1_gate", "w_ffn1_up", "w_ffn1_down"]
FFN2 = ["w_ffn2_gate", "w_ffn2_up", "w_ffn2_down"]
MIXER = ["w_in", "w_mem_kv", "w_out"]
TRANSPOSED = ["w_ffn1_gate", "w_ffn1_up", "w_ffn2_gate", "w_ffn2_up", "w_in"]


def _mixer_fwd(x1, mem, tables, sm, big, after_attn=None):
    cos_t, sin_t, gq2, gk2, gmq2, gmk2 = tables
    h2, proj = _in_proj_fwd(x1, sm["g_mix"], big["w_in"], "in_proj_fwd")
    ya, lse = _attn_fwd(proj, cos_t, sin_t, gq2, gk2, sm["sinks"], "attn_fwd")
    token = None if after_attn is None else after_attn(ya)
    yc, cv = _conv_fwd(proj, big["w_dw"], sm["b_dw"], sm["g_conv_ln"], sm["b_conv_ln"], "conv_fwd")
    hm, mkv = _mem_kv_fwd(mem, sm["g_mem"], big["w_mem_kv"], "mem_kv_fwd")
    ym, lse_m = _mem_attn_fwd(proj, mkv, gmq2, gmk2, "mem_attn_fwd")
    x2 = _out_proj_fwd(x1, ya, yc, ym, big["w_out"], "out_proj_fwd", token)
    return x2, (h2, proj, ya, lse, yc, cv, hm, mkv, ym, lse_m)


def _mixer_bwd(dx2, x1, mem, tables, sm, big, saved, after):
    cos_t, sin_t, gq2, gk2, gmq2, gmk2 = tables
    h2, proj, ya, lse, yc, cv, hm, mkv, ym, lse_m = saved
    g = {}
    dmix, g["w_out"] = _out_proj_bwd(dx2, ya, yc, ym, big["w_out"], "out_proj_bwd", after)
    dq, dgq, g["sinks"] = _attn_bwd_q(proj, cos_t, sin_t, gq2, gk2, sm["sinks"], ya, dmix, lse, "attn_bwd_q")
    dk, dv, dgk = _attn_bwd_kv(proj, cos_t, sin_t, gq2, gk2, ya, dmix, lse, "attn_bwd_kv")
    du, g["w_dw"], g["b_dw"], g["g_conv_ln"], g["b_conv_ln"] = _conv_bwd(
        proj, cv, dmix, big["w_dw"], sm["g_conv_ln"], sm["b_conv_ln"], "conv_bwd")
    dmq, dmkv, dgmq, dgmk = _mem_attn_bwd(proj, mkv, gmq2, gmk2, ym, dmix, lse_m, "mem_attn_bwd")
    g["w_mem_kv"], g["g_mem"] = _mem_kv_bwd(mem, sm["g_mem"], hm, big["w_mem_kv"], dmkv, "mem_kv_bwd")
    dx1, g["g_mix"], g["w_in"] = _in_proj_bwd(dq, dk, dv, du, dmq, big["w_in"], h2, x1, sm["g_mix"], dx2, "in_proj_bwd")
    fold = lambda a: a[:, :HEAD_DIM] + a[:, HEAD_DIM:]
    g["g_q"], g["g_k"], g["g_mq"], g["g_mk"] = fold(dgq), fold(dgk), fold(dgmq), fold(dgmk)
    return dx1, g


def _tables(positions, sm):
    pair = lambda a: jnp.tile(a, (1, 2))
    return _rope_tables(positions) + (pair(sm["g_q"]), pair(sm["g_k"]), pair(sm["g_mq"]), pair(sm["g_mk"]))


def kernel(x, mem, positions, g_ffn1, w_ffn1_gate, w_ffn1_up, w_ffn1_down, g_mix, w_in, g_q, g_k, sinks, w_dw, b_dw, g_conv_ln, b_conv_ln, g_mem, w_mem_kv, g_mq, g_mk, w_out, g_ffn2, w_ffn2_gate, w_ffn2_up, w_ffn2_down, loss_target, m_g_ffn1, m_w_ffn1_gate, m_w_ffn1_up, m_w_ffn1_down, m_g_mix, m_w_in, m_g_q, m_g_k, m_sinks, m_w_dw, m_b_dw, m_g_conv_ln, m_b_conv_ln, m_g_mem, m_w_mem_kv, m_g_mq, m_g_mk, m_w_out, m_g_ffn2, m_w_ffn2_gate, m_w_ffn2_up, m_w_ffn2_down, v_g_ffn1, v_w_ffn1_gate, v_w_ffn1_up, v_w_ffn1_down, v_g_mix, v_w_in, v_g_q, v_g_k, v_sinks, v_w_dw, v_b_dw, v_g_conv_ln, v_b_conv_ln, v_g_mem, v_w_mem_kv, v_g_mq, v_g_mk, v_w_out, v_g_ffn2, v_w_ffn2_gate, v_w_ffn2_up, v_w_ffn2_down):
    w = dict(g_ffn1=g_ffn1, w_ffn1_gate=w_ffn1_gate, w_ffn1_up=w_ffn1_up, w_ffn1_down=w_ffn1_down, g_mix=g_mix, w_in=w_in,
             g_q=g_q, g_k=g_k, sinks=sinks, w_dw=w_dw, b_dw=b_dw, g_conv_ln=g_conv_ln, b_conv_ln=b_conv_ln, g_mem=g_mem,
             w_mem_kv=w_mem_kv, g_mq=g_mq, g_mk=g_mk, w_out=w_out, g_ffn2=g_ffn2, w_ffn2_gate=w_ffn2_gate,
             w_ffn2_up=w_ffn2_up, w_ffn2_down=w_ffn2_down)
    mo = dict(g_ffn1=m_g_ffn1, w_ffn1_gate=m_w_ffn1_gate, w_ffn1_up=m_w_ffn1_up, w_ffn1_down=m_w_ffn1_down, g_mix=m_g_mix,
              w_in=m_w_in, g_q=m_g_q, g_k=m_g_k, sinks=m_sinks, w_dw=m_w_dw, b_dw=m_b_dw, g_conv_ln=m_g_conv_ln,
              b_conv_ln=m_b_conv_ln, g_mem=m_g_mem, w_mem_kv=m_w_mem_kv, g_mq=m_g_mq, g_mk=m_g_mk, w_out=m_w_out,
              g_ffn2=m_g_ffn2, w_ffn2_gate=m_w_ffn2_gate, w_ffn2_up=m_w_ffn2_up, w_ffn2_down=m_w_ffn2_down)
    vo = dict(g_ffn1=v_g_ffn1, w_ffn1_gate=v_w_ffn1_gate, w_ffn1_up=v_w_ffn1_up, w_ffn1_down=v_w_ffn1_down, g_mix=v_g_mix,
              w_in=v_w_in, g_q=v_g_q, g_k=v_g_k, sinks=v_sinks, w_dw=v_w_dw, b_dw=v_b_dw, g_conv_ln=v_g_conv_ln,
              b_conv_ln=v_b_conv_ln, g_mem=v_g_mem, w_mem_kv=v_w_mem_kv, g_mq=v_g_mq, g_mk=v_g_mk, w_out=v_w_out,
              g_ffn2=v_g_ffn2, w_ffn2_gate=v_w_ffn2_gate, w_ffn2_up=v_w_ffn2_up, w_ffn2_down=v_w_ffn2_down)
    me = _my_place()[3]
    sm = {k: w[k] for k in SMALL}
    flip = lambda k, a: a.T if k in TRANSPOSED else a
    as_bf16 = lambda names: [flip(k, w[k][0]).astype(BF) for k in names]
    zones = lambda arrays, gather: [_own_slot(a, me, gather) for a in arrays]
    out_g, out_d, out_m, out_v = {}, {}, {}, {}

    def update(names, parts_list):
        for k, parts in zip(names, parts_list):
            new = _sum_adamw(parts, flip(k, w[k][0]), flip(k, mo[k][0]), flip(k, vo[k][0]), "adamw_" + k)
            out_g[k], out_d[k], out_m[k], out_v[k] = [flip(k, a)[None] for a in new]

    w1 = _gather_two_level(as_bf16(FFN1), "gather_ffn1")
    mix_src = as_bf16(MIXER) + [w["w_dw"][0]]
    mix_sems, mix_arrays, mix_token = _gather_start(mix_src, zones(mix_src, True), w1[0], "gather_mixer_start")
    f2_src = as_bf16(FFN2)
    f2_sems, f2_arrays, f2_token = _gather_start(f2_src, zones(f2_src, True), mix_token, "gather_ffn2_start")

    tables = _tables(positions[0], sm)
    h1, gate1, up1, x1 = _ffn_fwd(x[0], sm["g_ffn1"], *w1, None, "ffn1_fwd", after=f2_token)
    mix_sems, mix_arrays, mix_token = _gather_relay(mix_sems, mix_arrays, x1, "gather_mixer_relay")
    got = _gather_finish(mix_sems, mix_arrays, mix_token, "gather_mixer_finish")
    big = dict(w_in=got[0].reshape(IN_COLS, D_MODEL), w_mem_kv=got[1].reshape(D_MODEL, 2 * MQ_COLS),
               w_out=got[2].reshape(D_MODEL, D_MODEL), w_dw=got[3].transpose(1, 0, 2).reshape(CONV_WIDTH, CONV_CH))
    relayed = []

    def relay_ffn2(ya):
        relayed.extend(_gather_relay(f2_sems, f2_arrays, ya, "gather_ffn2_relay"))
        return relayed[2]

    x2, saved = _mixer_fwd(x1, mem[0], tables, sm, big, relay_ffn2)
    w2 = _gather_finish(relayed[0], relayed[1], x2, "gather_ffn2_finish")
    h3, gate2, up2, dy, loss_part = _ffn_fwd(x2, sm["g_ffn2"], *w2, loss_target[0], "ffn2_fwd")

    grads = {}
    dyb2, act2, dgate2, dup2, dx2, grads["g_ffn2"] = _ffn_bwd_act(dy, x2, sm["g_ffn2"], gate2, up2, *w2, "ffn2_bwd_act")
    g_f2 = list(_ffn_bwd_w(h3, dyb2, act2, dgate2, dup2, "ffn2_bwd_w"))
    r_f2 = _exchange_start(g_f2, zones(g_f2, False), [False] * 3, dx2, "scatter_ffn2_start")
    dx1, g_mid = _mixer_bwd(dx2, x1, mem[0], tables, sm, big, saved, r_f2[4])
    grads.update(g_mid)
    g_mix = [g_mid["w_in"].reshape(N_DEV, IN_COLS // N_DEV, D_MODEL),
             g_mid["w_mem_kv"].reshape(N_DEV, D_MODEL // N_DEV, 2 * MQ_COLS),
             g_mid["w_out"].reshape(N_DEV, D_MODEL // N_DEV, D_MODEL)]
    r_mix = _exchange_start(g_mix, zones(g_mix, False), [False] * 3, dx1, "scatter_mixer_start")
    dyb1, act1, dgate1, dup1, grad_x, grads["g_ffn1"] = _ffn_bwd_act(dx1, x[0], sm["g_ffn1"], gate1, up1, *w1,
                                                                     "ffn1_bwd_act", after=r_mix[4])
    g_f1 = list(_ffn_bwd_w(h1, dyb1, act1, dgate1, dup1, "ffn1_bwd_w"))
    dw_flat = grads["w_dw"].reshape(-1)
    packed = jnp.concatenate([_pack_small(grads),
                              jnp.pad(dw_flat, (0, SMALL_ROWS * D_MODEL - dw_flat.shape[0])).reshape(SMALL_ROWS, D_MODEL)])
    last_src, last_kind = g_f1 + [packed], [False] * 3 + [True]
    r_f1 = _exchange_start(last_src, zones(g_f1, False) + zones([packed], True), last_kind, grad_x, "scatter_ffn1_start")

    all_done = lambda names: sum(out_d[k][:, :1, :1] for k in names)
    update(FFN2, _exchange_wait(*r_f2[:4], [False] * 3, r_f1[4], "scatter_ffn2_wait"))
    update(MIXER, _exchange_wait(*r_mix[:4], [False] * 3, all_done(FFN2), "scatter_mixer_wait"))
    last = _exchange_wait(*r_f1[:4], last_kind, all_done(MIXER), "scatter_ffn1_wait")
    update(FFN1, last[:3])
    small_sum = _slot_sum(last[3], "small_grad_sum")
    loss_local = loss_part[0, 0]
    g_small = small_sum[:SMALL_ROWS]
    d, m2, v2 = _adamw(g_small, _pack_small(w), _pack_small(mo), _pack_small(vo), "adamw_small")
    for dst, val in ((out_g, g_small), (out_d, d), (out_m, m2), (out_v, v2)):
        dst.update(_unpack_small(val, sm))
    g_dw = small_sum[SMALL_ROWS:].reshape(-1)[:CONV_WIDTH * CONV_CH].reshape(CONV_WIDTH, CONV_CH)
    g_dw = lax.dynamic_slice_in_dim(g_dw, me * (CONV_CH // N_DEV), CONV_CH // N_DEV, axis=1)
    d, m2, v2 = _adamw(g_dw, w["w_dw"][0], mo["w_dw"][0], vo["w_dw"][0], "adamw_w_dw")
    out_g["w_dw"], out_d["w_dw"], out_m["w_dw"], out_v["w_dw"] = g_dw[None], d[None], m2[None], v2[None]

    loss = lax.psum(loss_local, ("x", "y", "c"))
    return (loss, grad_x[None], *[out_g[k] for k in WEIGHTS], *[out_d[k] for k in WEIGHTS], *[out_m[k] for k in WEIGHTS],
            *[out_v[k] for k in WEIGHTS])
```

```python
import jax
import jax.numpy as jnp
from jax import lax
from jax.experimental import pallas as pl
from jax.experimental.pallas import tpu as pltpu

D_MODEL = 1024
N_DEV = 8
FF_BLOCK = 352
D_FF = N_DEV * FF_BLOCK
FF_TILE = 256
N_FF_TILES = D_FF // FF_TILE
FF_ROWS = 1024
HEAD_DIM = 64
PAIR = 2 * HEAD_DIM
N_Q_HEADS = 8
Q_PER_KV = 4
ATTN_BLOCK = 128
Q_COLS = 512
KV_COLS = 128
CONV_CH = 256
MQ_COLS = 256
IN_COLS = 1536
CONV_WIDTH = 31
CONV_PAD = 32
CONV_CHUNK = 256
N_MEM_HEADS = 4
ROPE_THETA = 500000.0
ROPE_HALF = 8
EPS = 1e-6
SCALE = HEAD_DIM ** -0.5
NEG = -1e30
ADAM_LR, ADAM_B1, ADAM_B2, ADAM_EPS, ADAM_WD, ADAM_STEP = 0.001, 0.9, 0.999, 1e-08, 0.01, 10
VMEM_LIMIT_BYTES = 56 * 1024 * 1024
BF = jnp.bfloat16
F32 = jnp.float32

SMALL = ["g_ffn1", "g_mix", "g_mem", "g_ffn2", "g_q", "g_k", "g_mq", "g_mk", "sinks", "b_dw", "g_conv_ln", "b_conv_ln"]
WEIGHTS = ["g_ffn1", "w_ffn1_gate", "w_ffn1_up", "w_ffn1_down", "g_mix", "w_in", "g_q", "g_k", "sinks", "w_dw", "b_dw",
           "g_conv_ln", "b_conv_ln", "g_mem", "w_mem_kv", "g_mq", "g_mk", "w_out", "g_ffn2", "w_ffn2_gate", "w_ffn2_up",
           "w_ffn2_down"]
SMALL_ROWS = 8
PACK_ROWS = 16


def _nn(a, b):
    return jnp.dot(a, b, preferred_element_type=F32)


def _nt(a, b):
    return lax.dot_general(a, b, (((1,), (1,)), ((), ())), preferred_element_type=F32)


def _tn(a, b):
    return lax.dot_general(a, b, (((0,), (0,)), ((), ())), preferred_element_type=F32)


def _params(n_grid):
    return pltpu.CompilerParams(dimension_semantics=("arbitrary",) * n_grid, vmem_limit_bytes=VMEM_LIMIT_BYTES)


def _ordered_after(body, after):
    if after is None:
        return body, (), []

    def body_after(after_ref, *refs):
        body(*refs)

    return body_after, (after,), [pl.BlockSpec(memory_space=pl.ANY)]


def _row_rms(xv):
    return lax.rsqrt(jnp.mean(xv * xv, axis=-1, keepdims=True) + EPS)


def _rms_bwd(dh, xv, r, g):
    u = dh * g
    dx = r * u - xv * (r * r * r) * jnp.mean(u * xv, axis=-1, keepdims=True)
    return dx, jnp.sum(dh * xv * r, axis=0, keepdims=True)


def _sum_all(a):
    return jnp.sum(jnp.sum(a, axis=1, keepdims=True), axis=0, keepdims=True)


def _pair_sum(v):
    row = lax.broadcasted_iota(jnp.int32, (PAIR, PAIR), 0) >= HEAD_DIM
    col = lax.broadcasted_iota(jnp.int32, (PAIR, PAIR), 1) >= HEAD_DIM
    same_head = (row == col).astype(BF)
    hi = v.astype(BF)
    lo = (v - hi.astype(F32)).astype(BF)
    return _nn(hi, same_head) + _nn(lo, same_head)


def _pair_rms(xv):
    return lax.rsqrt(_pair_sum(xv * xv) * (1.0 / HEAD_DIM) + EPS)


def _pair_rms_bwd(dxn, xv, r, g):
    u = dxn * g
    dx = r * u - xv * (r * r * r) * (_pair_sum(u * xv) * (1.0 / HEAD_DIM))
    return dx, jnp.sum(dxn * xv * r, axis=0, keepdims=True)


def _rope_mask(shape):
    lane = lax.broadcasted_iota(jnp.int32, shape, 1)
    return ((lane & (HEAD_DIM - 1)) < 2 * ROPE_HALF).astype(F32)


def _partner(v):
    return pltpu.roll(v, ROPE_HALF, 1) + pltpu.roll(v, PAIR - ROPE_HALF, 1)


def _rope(xn, cos_t, sin_t):
    return xn * cos_t + _partner(xn * _rope_mask(xn.shape)) * sin_t


def _rope_t(d, cos_t, sin_t):
    return d * cos_t + _partner(d * sin_t) * _rope_mask(d.shape)


def _rope_tables(positions):
    inv_freq = ROPE_THETA ** (-jnp.arange(ROPE_HALF, dtype=F32) / ROPE_HALF)
    ang = positions.astype(F32)[:, None] * inv_freq
    cos, sin = jnp.cos(ang), jnp.sin(ang)
    t = positions.shape[0]
    cos_h = jnp.concatenate([cos, cos, jnp.ones((t, HEAD_DIM - 2 * ROPE_HALF), F32)], axis=1)
    sin_h = jnp.concatenate([-sin, sin, jnp.zeros((t, HEAD_DIM - 2 * ROPE_HALF), F32)], axis=1)
    return jnp.tile(cos_h, (1, 2)), jnp.tile(sin_h, (1, 2))


def _ff_rows(w):
    return w.reshape(D_FF, D_MODEL)


def _ffn_fwd(x, g, wg, wu, wd, target, name, after=None):
    t_len = x.shape[0]
    tm = min(FF_ROWS, t_len)
    with_loss = target is not None

    def body(*refs):
        if with_loss:
            x_ref, g_ref, wg_ref, wu_ref, wd_ref, t_ref, h_ref, gg_ref, uu_ref, dy_ref, loss_ref, acc = refs
        else:
            x_ref, g_ref, wg_ref, wu_ref, wd_ref, h_ref, gg_ref, uu_ref, xo_ref, acc = refs
        t, j = pl.program_id(0), pl.program_id(1)

        @pl.when(j == 0)
        def _():
            xv = x_ref[...]
            h_ref[...] = (xv * _row_rms(xv) * g_ref[...]).astype(BF)
            acc[...] = jnp.zeros_like(acc)

        h = h_ref[...]
        gate = _nt(h, wg_ref[...])
        up = _nt(h, wu_ref[...])
        gg_ref[...] = gate.astype(BF)
        uu_ref[...] = up.astype(BF)
        act = (gate * jax.nn.sigmoid(gate) * up).astype(BF)
        acc[...] += _nn(act, wd_ref[...])

        @pl.when(j == N_FF_TILES - 1)
        def _():
            xo = x_ref[...] + 0.5 * acc[...]
            if with_loss:
                err = xo - t_ref[...]
                dy_ref[...] = err * (1.0 / D_MODEL)

                @pl.when(t == 0)
                def _():
                    loss_ref[...] = jnp.zeros_like(loss_ref)

                loss_ref[...] += _sum_all(err * err) * (0.5 / D_MODEL)
            else:
                xo_ref[...] = xo

    row = pl.BlockSpec((tm, D_MODEL), lambda t, j: (t, 0))
    vec = pl.BlockSpec((1, D_MODEL), lambda t, j: (0, 0))
    w_spec = pl.BlockSpec((FF_TILE, D_MODEL), lambda t, j: (j, 0))
    blk = pl.BlockSpec((tm, FF_TILE), lambda t, j: (t, j))
    in_specs = [row, vec, w_spec, w_spec, w_spec] + ([row] if with_loss else [])
    out_shape = [jax.ShapeDtypeStruct((t_len, D_MODEL), BF),
                 jax.ShapeDtypeStruct((t_len, D_FF), BF),
                 jax.ShapeDtypeStruct((t_len, D_FF), BF),
                 jax.ShapeDtypeStruct((t_len, D_MODEL), F32)]
    out_specs = [row, blk, blk, row]
    if with_loss:
        out_shape.append(jax.ShapeDtypeStruct((1, 128), F32))
        out_specs.append(pl.BlockSpec((1, 128), lambda t, j: (0, 0)))
    args = (x, g, _ff_rows(wg), _ff_rows(wu), _ff_rows(wd)) + ((target,) if with_loss else ())
    body, first, first_specs = _ordered_after(body, after)
    return pl.pallas_call(body, out_shape=out_shape, grid=(t_len // tm, N_FF_TILES), in_specs=first_specs + in_specs,
                          out_specs=out_specs, scratch_shapes=[pltpu.VMEM((tm, D_MODEL), F32)], name=name,
                          compiler_params=_params(2))(*first, *args)


def _ffn_bwd_act(dy, x, g, gate, up, wg, wu, wd, name, after=None):
    t_len = x.shape[0]
    tm = min(FF_ROWS, t_len)

    def body(dy_ref, x_ref, g_ref, gg_ref, uu_ref, wg_ref, wu_ref, wd_ref,
             dyb_ref, act_ref, dgg_ref, duu_ref, dx_ref, dg_ref, acc):
        t, j = pl.program_id(0), pl.program_id(1)

        @pl.when(j == 0)
        def _():
            dyb_ref[...] = (0.5 * dy_ref[...]).astype(BF)
            acc[...] = jnp.zeros_like(acc)

        d_act = _nt(dyb_ref[...], wd_ref[...])
        gate = gg_ref[...].astype(F32)
        upv = uu_ref[...].astype(F32)
        sig = jax.nn.sigmoid(gate)
        silu = gate * sig
        d_up = (d_act * silu).astype(BF)
        d_gate = (d_act * upv * (sig * (1.0 + gate * (1.0 - sig)))).astype(BF)
        act_ref[...] = (silu * upv).astype(BF)
        dgg_ref[...] = d_gate
        duu_ref[...] = d_up
        acc[...] += _nn(jnp.concatenate([d_gate, d_up], axis=1), jnp.concatenate([wg_ref[...], wu_ref[...]], axis=0))

        @pl.when(j == N_FF_TILES - 1)
        def _():
            xv = x_ref[...]
            dx, dg = _rms_bwd(acc[...], xv, _row_rms(xv), g_ref[...])
            dx_ref[...] = dy_ref[...] + dx

            @pl.when(t == 0)
            def _():
                dg_ref[...] = jnp.zeros_like(dg_ref)

            dg_ref[...] += dg

    row = pl.BlockSpec((tm, D_MODEL), lambda t, j: (t, 0))
    vec = pl.BlockSpec((1, D_MODEL), lambda t, j: (0, 0))
    w_spec = pl.BlockSpec((FF_TILE, D_MODEL), lambda t, j: (j, 0))
    blk = pl.BlockSpec((tm, FF_TILE), lambda t, j: (t, j))
    blk_shape = jax.ShapeDtypeStruct((t_len, D_FF), BF)
    body, first, first_specs = _ordered_after(body, after)
    return pl.pallas_call(
        body,
        out_shape=[jax.ShapeDtypeStruct((t_len, D_MODEL), BF), blk_shape, blk_shape, blk_shape,
                   jax.ShapeDtypeStruct((t_len, D_MODEL), F32), jax.ShapeDtypeStruct((1, D_MODEL), F32)],
        grid=(t_len // tm, N_FF_TILES), in_specs=first_specs + [row, row, vec, blk, blk, w_spec, w_spec, w_spec],
        out_specs=[row, blk, blk, blk, row, vec], scratch_shapes=[pltpu.VMEM((tm, D_MODEL), F32)], name=name,
        compiler_params=_params(2))(*first, dy, x, g, gate, up, _ff_rows(wg), _ff_rows(wu), _ff_rows(wd))


def _ffn_bwd_w(h, dyb, act, d_gate, d_up, name):
    t_len = h.shape[0]
    tm = min(FF_ROWS, t_len)
    nt = t_len // tm

    def body(h_ref, dyb_ref, act_ref, dgg_ref, duu_ref, dwg_ref, dwu_ref, dwd_ref, ag, au, ad):
        t = pl.program_id(1)

        @pl.when(t == 0)
        def _():
            ag[...] = jnp.zeros_like(ag)
            au[...] = jnp.zeros_like(au)
            ad[...] = jnp.zeros_like(ad)

        h = h_ref[...]
        ag[...] += _tn(dgg_ref[...], h)
        au[...] += _tn(duu_ref[...], h)
        ad[...] += _tn(act_ref[...], dyb_ref[...])

        @pl.when(t == nt - 1)
        def _():
            dwg_ref[...] = ag[...].astype(BF)
            dwu_ref[...] = au[...].astype(BF)
            dwd_ref[...] = ad[...].astype(BF)

    row = pl.BlockSpec((tm, D_MODEL), lambda j, t: (t, 0))
    blk = pl.BlockSpec((tm, FF_TILE), lambda j, t: (t, j))
    w_spec = pl.BlockSpec((FF_TILE, D_MODEL), lambda j, t: (j, 0))
    grads = pl.pallas_call(
        body,
        out_shape=[jax.ShapeDtypeStruct((D_FF, D_MODEL), BF)] * 3,
        grid=(N_FF_TILES, nt), in_specs=[row, row, blk, blk, blk], out_specs=[w_spec] * 3,
        scratch_shapes=[pltpu.VMEM((FF_TILE, D_MODEL), F32)] * 3,
        name=name, compiler_params=_params(2))(h, dyb, act, d_gate, d_up)
    return [a.reshape(N_DEV, FF_BLOCK, D_MODEL) for a in grads]


def _in_proj_fwd(x, g, w_in, name):
    t_len = x.shape[0]
    tm = min(512, t_len)

    def body(x_ref, g_ref, w_ref, h_ref, p_ref):
        xv = x_ref[...]
        h = (xv * _row_rms(xv) * g_ref[...]).astype(BF)
        h_ref[...] = h
        p_ref[...] = _nt(h, w_ref[...])

    row = pl.BlockSpec((tm, D_MODEL), lambda t: (t, 0))
    return pl.pallas_call(
        body, out_shape=[jax.ShapeDtypeStruct((t_len, D_MODEL), BF), jax.ShapeDtypeStruct((t_len, IN_COLS), F32)],
        grid=(t_len // tm,),
        in_specs=[row, pl.BlockSpec((1, D_MODEL), lambda t: (0, 0)), pl.BlockSpec((IN_COLS, D_MODEL), lambda t: (0, 0))],
        out_specs=[row, pl.BlockSpec((tm, IN_COLS), lambda t: (t, 0))], name=name, compiler_params=_params(1))(x, g, w_in)


def _in_proj_bwd(dq, dk, dv, du, dmq, w_in, h, x, g, dres, name):
    t_len = x.shape[0]
    tm = min(512, t_len)
    nt = t_len // tm
    groups = [(0, Q_COLS), (Q_COLS, KV_COLS), (Q_COLS + KV_COLS, KV_COLS), (Q_COLS + 2 * KV_COLS, 2 * CONV_CH),
              (Q_COLS + 2 * KV_COLS + 2 * CONV_CH, MQ_COLS)]

    def body(dq_ref, dk_ref, dv_ref, du_ref, dmq_ref, w_ref, h_ref, x_ref, g_ref, dres_ref, dx_ref, dg_ref, dw_ref, acc):
        t = pl.program_id(0)

        @pl.when(t == 0)
        def _():
            acc[...] = jnp.zeros_like(acc)
            dg_ref[...] = jnp.zeros_like(dg_ref)

        h = h_ref[...]
        dh = jnp.zeros((tm, D_MODEL), F32)
        for (start, width), ref in zip(groups, (dq_ref, dk_ref, dv_ref, du_ref, dmq_ref)):
            piece = ref[...]
            dh = dh + _nn(piece, w_ref[start:start + width, :])
            acc[start:start + width, :] += _tn(piece, h)
        xv = x_ref[...]
        dx, dg = _rms_bwd(dh, xv, _row_rms(xv), g_ref[...])
        dx_ref[...] = dres_ref[...] + dx
        dg_ref[...] += dg

        @pl.when(t == nt - 1)
        def _():
            dw_ref[...] = acc[...].astype(BF)

    def cols(width):
        return pl.BlockSpec((tm, width), lambda t: (t, 0))

    row = cols(D_MODEL)
    vec = pl.BlockSpec((1, D_MODEL), lambda t: (0, 0))
    full = pl.BlockSpec((IN_COLS, D_MODEL), lambda t: (0, 0))
    return pl.pallas_call(
        body,
        out_shape=[jax.ShapeDtypeStruct((t_len, D_MODEL), F32), jax.ShapeDtypeStruct((1, D_MODEL), F32),
                   jax.ShapeDtypeStruct((IN_COLS, D_MODEL), BF)],
        grid=(nt,),
        in_specs=[cols(Q_COLS), cols(KV_COLS), cols(KV_COLS), cols(2 * CONV_CH), cols(MQ_COLS), full, row, row, vec, row],
        out_specs=[row, vec, full], scratch_shapes=[pltpu.VMEM((IN_COLS, D_MODEL), F32)], name=name,
        compiler_params=_params(1))(dq, dk, dv, du, dmq, w_in, h, x, g, dres)


def _attn_fwd(proj, cos_t, sin_t, gq2, gk2, sinks, name):
    t_len = proj.shape[0]
    nb = t_len // ATTN_BLOCK

    def body(q_ref, kvc_ref, kvp_ref, cq_ref, sq_ref, cp_ref, sp_ref, gq_ref, gk_ref, sk_ref, ya_ref, lse_ref):
        n = pl.program_id(0)
        cq, sq = cq_ref[...], sq_ref[...]
        c_all = jnp.concatenate([cp_ref[...], cq], axis=0)
        s_all = jnp.concatenate([sp_ref[...], sq], axis=0)
        kv = jnp.concatenate([kvp_ref[...], kvc_ref[...]], axis=0)
        k_raw, v = kv[:, :PAIR], kv[:, PAIR:]
        kn = _rope(k_raw * _pair_rms(k_raw) * gk_ref[...], c_all, s_all)
        row = lax.broadcasted_iota(jnp.int32, (ATTN_BLOCK, 2 * ATTN_BLOCK), 0)
        col = lax.broadcasted_iota(jnp.int32, (ATTN_BLOCK, 2 * ATTN_BLOCK), 1)
        rel = row + ATTN_BLOCK - col
        valid = (rel >= 0) & (rel < ATTN_BLOCK) & ((col >= ATTN_BLOCK) | (n > 0))
        for pr in range(N_Q_HEADS // 2):
            q_raw = q_ref[:, pr * PAIR:(pr + 1) * PAIR]
            qn = _rope(q_raw * _pair_rms(q_raw) * gq_ref[...], cq, sq)
            kvh = (2 * pr) // Q_PER_KV
            kh = kn[:, kvh * HEAD_DIM:(kvh + 1) * HEAD_DIM].astype(BF)
            v_ones = jnp.concatenate([v[:, kvh * HEAD_DIM:(kvh + 1) * HEAD_DIM],
                                      jnp.ones((2 * ATTN_BLOCK, HEAD_DIM), F32)], axis=1).astype(BF)
            for hh in range(2):
                hd = 2 * pr + hh
                qh = qn[:, hh * HEAD_DIM:(hh + 1) * HEAD_DIM].astype(BF)
                s = jnp.where(valid, _nt(qh, kh) * SCALE, NEG)
                sink = sk_ref[0:1, hd:hd + 1]
                m = jnp.maximum(jnp.max(s, axis=-1, keepdims=True), sink)
                ov = _nn(jnp.exp(s - m).astype(BF), v_ones)
                den = ov[:, HEAD_DIM:HEAD_DIM + 1] + jnp.exp(sink - m)
                ya_ref[:, hd * HEAD_DIM:(hd + 1) * HEAD_DIM] = (ov[:, :HEAD_DIM] / den).astype(BF)
                lse_ref[:, hd:hd + 1] = m + jnp.log(den)

    prev = lambda n: (jnp.maximum(n - 1, 0), 0)
    tab = pl.BlockSpec((ATTN_BLOCK, PAIR), lambda n: (n, 0))
    tab_p = pl.BlockSpec((ATTN_BLOCK, PAIR), prev)
    one = lambda w: pl.BlockSpec((1, w), lambda n: (0, 0))
    return pl.pallas_call(
        body, out_shape=[jax.ShapeDtypeStruct((t_len, Q_COLS), BF), jax.ShapeDtypeStruct((t_len, N_Q_HEADS), F32)],
        grid=(nb,),
        in_specs=[pl.BlockSpec((ATTN_BLOCK, Q_COLS), lambda n: (n, 0)),
                  pl.BlockSpec((ATTN_BLOCK, 2 * KV_COLS), lambda n: (n, 2)),
                  pl.BlockSpec((ATTN_BLOCK, 2 * KV_COLS), lambda n: (jnp.maximum(n - 1, 0), 2)),
                  tab, tab, tab_p, tab_p, one(PAIR), one(PAIR), one(N_Q_HEADS)],
        out_specs=[pl.BlockSpec((ATTN_BLOCK, Q_COLS), lambda n: (n, 0)),
                   pl.BlockSpec((ATTN_BLOCK, N_Q_HEADS), lambda n: (n, 0))],
        name=name, compiler_params=_params(1))(proj, proj, proj, cos_t, sin_t, cos_t, sin_t, gq2, gk2, sinks)


def _attn_bwd_q(proj, cos_t, sin_t, gq2, gk2, sinks, ya, dmix, lse, name):
    t_len = proj.shape[0]
    nb = t_len // ATTN_BLOCK

    def body(q_ref, kvc_ref, kvp_ref, cq_ref, sq_ref, cp_ref, sp_ref, gq_ref, gk_ref, sk_ref, ya_ref, do_ref, lse_ref,
             dq_ref, dgq_ref, dsk_ref, scr):
        n = pl.program_id(0)

        @pl.when(n == 0)
        def _():
            dgq_ref[...] = jnp.zeros_like(dgq_ref)
            dsk_ref[...] = jnp.zeros_like(dsk_ref)

        cq, sq = cq_ref[...], sq_ref[...]
        c_all = jnp.concatenate([cp_ref[...], cq], axis=0)
        s_all = jnp.concatenate([sp_ref[...], sq], axis=0)
        kv = jnp.concatenate([kvp_ref[...], kvc_ref[...]], axis=0)
        k_raw, v = kv[:, :PAIR], kv[:, PAIR:]
        kn = _rope(k_raw * _pair_rms(k_raw) * gk_ref[...], c_all, s_all)
        row = lax.broadcasted_iota(jnp.int32, (ATTN_BLOCK, 2 * ATTN_BLOCK), 0)
        col = lax.broadcasted_iota(jnp.int32, (ATTN_BLOCK, 2 * ATTN_BLOCK), 1)
        rel = row + ATTN_BLOCK - col
        valid = (rel >= 0) & (rel < ATTN_BLOCK) & ((col >= ATTN_BLOCK) | (n > 0))
        gq = gq_ref[...]
        for pr in range(N_Q_HEADS // 2):
            q_raw = q_ref[:, pr * PAIR:(pr + 1) * PAIR]
            rq = _pair_rms(q_raw)
            qn = _rope(q_raw * rq * gq, cq, sq)
            kvh = (2 * pr) // Q_PER_KV
            kh = kn[:, kvh * HEAD_DIM:(kvh + 1) * HEAD_DIM].astype(BF)
            vh = v[:, kvh * HEAD_DIM:(kvh + 1) * HEAD_DIM].astype(BF)
            lanes = slice(pr * PAIR, (pr + 1) * PAIR)
            dsum2 = _pair_sum(do_ref[:, lanes] * ya_ref[:, lanes].astype(F32))
            for hh in range(2):
                hd = 2 * pr + hh
                qh = qn[:, hh * HEAD_DIM:(hh + 1) * HEAD_DIM].astype(BF)
                lse_h = lse_ref[:, hd:hd + 1]
                p = jnp.exp(jnp.where(valid, _nt(qh, kh) * SCALE, NEG) - lse_h)
                d_o = do_ref[:, hd * HEAD_DIM:(hd + 1) * HEAD_DIM]
                dsum = dsum2[:, hh * HEAD_DIM:hh * HEAD_DIM + 1]
                ds = p * (_nt(d_o.astype(BF), vh) - dsum)
                scr[pr, :, hh * HEAD_DIM:(hh + 1) * HEAD_DIM] = _nn(ds.astype(BF), kh) * SCALE
                sink = sk_ref[0:1, hd:hd + 1]
                dsk_ref[0:1, hd:hd + 1] += -jnp.sum(jnp.exp(sink - lse_h) * dsum, axis=0, keepdims=True)
            dx, dg = _pair_rms_bwd(_rope_t(scr[pr], cq, sq), q_raw, rq, gq)
            dq_ref[:, pr * PAIR:(pr + 1) * PAIR] = dx.astype(BF)
            dgq_ref[...] += dg

    prev = lambda n: (jnp.maximum(n - 1, 0), 0)
    tab = pl.BlockSpec((ATTN_BLOCK, PAIR), lambda n: (n, 0))
    tab_p = pl.BlockSpec((ATTN_BLOCK, PAIR), prev)
    one = lambda w: pl.BlockSpec((1, w), lambda n: (0, 0))
    qblk = pl.BlockSpec((ATTN_BLOCK, Q_COLS), lambda n: (n, 0))
    return pl.pallas_call(
        body,
        out_shape=[jax.ShapeDtypeStruct((t_len, Q_COLS), BF), jax.ShapeDtypeStruct((1, PAIR), F32),
                   jax.ShapeDtypeStruct((1, N_Q_HEADS), F32)],
        grid=(nb,),
        in_specs=[qblk, pl.BlockSpec((ATTN_BLOCK, 2 * KV_COLS), lambda n: (n, 2)),
                  pl.BlockSpec((ATTN_BLOCK, 2 * KV_COLS), lambda n: (jnp.maximum(n - 1, 0), 2)),
                  tab, tab, tab_p, tab_p, one(PAIR), one(PAIR), one(N_Q_HEADS), qblk, qblk,
                  pl.BlockSpec((ATTN_BLOCK, N_Q_HEADS), lambda n: (n, 0))],
        out_specs=[qblk, one(PAIR), one(N_Q_HEADS)], scratch_shapes=[pltpu.VMEM((N_Q_HEADS // 2, ATTN_BLOCK, PAIR), F32)],
        name=name, compiler_params=_params(1))(proj, proj, proj, cos_t, sin_t, cos_t, sin_t, gq2, gk2, sinks, ya, dmix, lse)


def _attn_bwd_kv(proj, cos_t, sin_t, gq2, gk2, ya, dmix, lse, name):
    t_len = proj.shape[0]
    nb = t_len // ATTN_BLOCK

    def body(kv_ref, q0_ref, q1_ref, ck_ref, sk_ref, c1_ref, s1_ref, gq_ref, gk_ref, o0_ref, o1_ref, do0_ref, do1_ref,
             l0_ref, l1_ref, dk_ref, dv_ref, dgk_ref, dkn_scr, dv_scr):
        m = pl.program_id(0)

        @pl.when(m == 0)
        def _():
            dgk_ref[...] = jnp.zeros_like(dgk_ref)

        ck, sk = ck_ref[...], sk_ref[...]
        c_all = jnp.concatenate([ck, c1_ref[...]], axis=0)
        s_all = jnp.concatenate([sk, s1_ref[...]], axis=0)
        kvv = kv_ref[...]
        k_raw, v = kvv[:, :PAIR], kvv[:, PAIR:]
        rk = _pair_rms(k_raw)
        gk = gk_ref[...]
        kn = _rope(k_raw * rk * gk, ck, sk)
        row = lax.broadcasted_iota(jnp.int32, (2 * ATTN_BLOCK, ATTN_BLOCK), 0)
        col = lax.broadcasted_iota(jnp.int32, (2 * ATTN_BLOCK, ATTN_BLOCK), 1)
        valid = ((row < ATTN_BLOCK) & (row >= col)) | ((row >= ATTN_BLOCK) & (row - ATTN_BLOCK < col) & (m < nb - 1))
        lse2 = jnp.concatenate([l0_ref[...], l1_ref[...]], axis=0)
        d_kn, d_v = [None, None], [None, None]
        plus = lambda acc, term: term if acc is None else acc + term
        for pr in range(N_Q_HEADS // 2):
            lanes = slice(pr * PAIR, (pr + 1) * PAIR)
            q_raw = jnp.concatenate([q0_ref[:, lanes], q1_ref[:, lanes]], axis=0)
            qn = _rope(q_raw * _pair_rms(q_raw) * gq_ref[...], c_all, s_all)
            d_o2 = jnp.concatenate([do0_ref[:, lanes], do1_ref[:, lanes]], axis=0)
            o2 = jnp.concatenate([o0_ref[:, lanes], o1_ref[:, lanes]], axis=0).astype(F32)
            dsum2 = _pair_sum(d_o2 * o2)
            kvh = (2 * pr) // Q_PER_KV
            khs = slice(kvh * HEAD_DIM, (kvh + 1) * HEAD_DIM)
            kh = kn[:, khs].astype(BF)
            vh = v[:, khs].astype(BF)
            for hh in range(2):
                hd = 2 * pr + hh
                hs = slice(hh * HEAD_DIM, (hh + 1) * HEAD_DIM)
                qh = qn[:, hs].astype(BF)
                d_oh = d_o2[:, hs].astype(BF)
                p = jnp.exp(jnp.where(valid, _nt(qh, kh) * SCALE, NEG) - lse2[:, hd:hd + 1])
                d_v[kvh] = plus(d_v[kvh], _tn(p.astype(BF), d_oh))
                ds = p * (_nt(d_oh, vh) - dsum2[:, hh * HEAD_DIM:hh * HEAD_DIM + 1])
                d_kn[kvh] = plus(d_kn[kvh], _tn(ds.astype(BF), qh))
        for kvh in range(2):
            khs = slice(kvh * HEAD_DIM, (kvh + 1) * HEAD_DIM)
            dkn_scr[:, khs] = d_kn[kvh] * SCALE
            dv_scr[:, khs] = d_v[kvh]
        dx, dg = _pair_rms_bwd(_rope_t(dkn_scr[...], ck, sk), k_raw, rk, gk)
        dk_ref[...] = dx.astype(BF)
        dv_ref[...] = dv_scr[...].astype(BF)
        dgk_ref[...] += dg

    nxt = lambda m: (jnp.minimum(m + 1, nb - 1), 0)
    cur = lambda m: (m, 0)
    tab = lambda f: pl.BlockSpec((ATTN_BLOCK, PAIR), f)
    qb = lambda f: pl.BlockSpec((ATTN_BLOCK, Q_COLS), f)
    lb = lambda f: pl.BlockSpec((ATTN_BLOCK, N_Q_HEADS), f)
    one = pl.BlockSpec((1, PAIR), lambda m: (0, 0))
    return pl.pallas_call(
        body,
        out_shape=[jax.ShapeDtypeStruct((t_len, KV_COLS), BF), jax.ShapeDtypeStruct((t_len, KV_COLS), BF),
                   jax.ShapeDtypeStruct((1, PAIR), F32)],
        grid=(nb,),
        in_specs=[pl.BlockSpec((ATTN_BLOCK, 2 * KV_COLS), lambda m: (m, 2)), qb(cur), qb(nxt), tab(cur), tab(cur), tab(nxt),
                  tab(nxt), one, one, qb(cur), qb(nxt), qb(cur), qb(nxt), lb(cur), lb(nxt)],
        out_specs=[tab(cur), tab(cur), one],
        scratch_shapes=[pltpu.VMEM((ATTN_BLOCK, PAIR), F32), pltpu.VMEM((ATTN_BLOCK, PAIR), F32)],
        name=name, compiler_params=_params(1))(proj, proj, proj, cos_t, sin_t, cos_t, sin_t, gq2, gk2, ya, ya, dmix, dmix,
                                               lse, lse)


def _conv_taps(blk, w, offset_of):
    acc = jnp.zeros((CONV_CHUNK, CONV_CH), F32)
    for k in range(CONV_WIDTH):
        o = offset_of(k)
        acc = acc + w[k:k + 1, :] * blk[o:o + CONV_CHUNK, :]
    return acc


def _conv_fwd(proj, w_dw, b_dw, g_ln, b_ln, name):
    t_len = proj.shape[0]
    nc = t_len // CONV_CHUNK

    def body(a_ref, gt_ref, w_ref, b_ref, g_ref, bl_ref, yc_ref, cv_ref, pad):
        pad[0:CONV_PAD, :] = jnp.zeros((CONV_PAD, CONV_CH), F32)

        def glu(c, carry):
            rows = pl.ds(pl.multiple_of(c * CONV_CHUNK, CONV_CHUNK), CONV_CHUNK)
            dst = pl.ds(pl.multiple_of(c * CONV_CHUNK + CONV_PAD, CONV_PAD), CONV_CHUNK)
            pad[dst, :] = a_ref[rows, :] * jax.nn.sigmoid(gt_ref[rows, :])
            return carry

        lax.fori_loop(0, nc, glu, 0)
        w = w_ref[...]

        def conv(c, carry):
            base = pl.multiple_of(c * CONV_CHUNK, CONV_CHUNK)
            blk = pad[pl.ds(base, CONV_CHUNK + CONV_PAD), :]
            cv = _conv_taps(blk, w, lambda k: k + CONV_PAD - (CONV_WIDTH - 1)) + b_ref[...]
            mu = jnp.mean(cv, axis=-1, keepdims=True)
            xc = cv - mu
            z = xc * lax.rsqrt(jnp.mean(xc * xc, axis=-1, keepdims=True) + EPS) * g_ref[...] + bl_ref[...]
            rows = pl.ds(base, CONV_CHUNK)
            cv_ref[rows, :] = cv
            yc_ref[rows, :] = (z * jax.nn.sigmoid(z)).astype(BF)
            return carry

        lax.fori_loop(0, nc, conv, 0)

    vec = pl.BlockSpec((1, CONV_CH), lambda i: (0, 0))
    full = pl.BlockSpec((t_len, CONV_CH), lambda i: (0, 0))
    return pl.pallas_call(
        body, out_shape=[jax.ShapeDtypeStruct((t_len, CONV_CH), BF), jax.ShapeDtypeStruct((t_len, CONV_CH), F32)],
        grid=(1,),
        in_specs=[pl.BlockSpec((t_len, CONV_CH), lambda i: (0, 3)), pl.BlockSpec((t_len, CONV_CH), lambda i: (0, 4)),
                  pl.BlockSpec((CONV_WIDTH, CONV_CH), lambda i: (0, 0)), vec, vec, vec],
        out_specs=[full, full], scratch_shapes=[pltpu.VMEM((t_len + CONV_PAD, CONV_CH), F32)], name=name,
        compiler_params=_params(1))(proj, proj, w_dw, b_dw, g_ln, b_ln)


def _conv_bwd(proj, cv, dmix, w_dw, g_ln, b_ln, name):
    t_len = proj.shape[0]
    nc = t_len // CONV_CHUNK

    def body(a_ref, gt_ref, cv_ref, dy_ref, w_ref, g_ref, bl_ref, du_ref, dw_ref, db_ref, dgl_ref, dbl_ref, pad, dpad):
        pad[0:CONV_PAD, :] = jnp.zeros((CONV_PAD, CONV_CH), F32)
        dpad[t_len:t_len + CONV_PAD, :] = jnp.zeros((CONV_PAD, CONV_CH), F32)
        dw_ref[...] = jnp.zeros_like(dw_ref)
        db_ref[...] = jnp.zeros_like(db_ref)
        dgl_ref[...] = jnp.zeros_like(dgl_ref)
        dbl_ref[...] = jnp.zeros_like(dbl_ref)

        def norm_bwd(c, carry):
            base = pl.multiple_of(c * CONV_CHUNK, CONV_CHUNK)
            rows = pl.ds(base, CONV_CHUNK)
            dst = pl.ds(pl.multiple_of(c * CONV_CHUNK + CONV_PAD, CONV_PAD), CONV_CHUNK)
            pad[dst, :] = a_ref[rows, :] * jax.nn.sigmoid(gt_ref[rows, :])
            cvv = cv_ref[rows, :]
            xc = cvv - jnp.mean(cvv, axis=-1, keepdims=True)
            rs = lax.rsqrt(jnp.mean(xc * xc, axis=-1, keepdims=True) + EPS)
            xhat = xc * rs
            z = xhat * g_ref[...] + bl_ref[...]
            sg = jax.nn.sigmoid(z)
            dz = dy_ref[rows, :] * (sg * (1.0 + z * (1.0 - sg)))
            dgl_ref[...] += jnp.sum(dz * xhat, axis=0, keepdims=True)
            dbl_ref[...] += jnp.sum(dz, axis=0, keepdims=True)
            dxh = dz * g_ref[...]
            dcv = rs * (dxh - jnp.mean(dxh, axis=-1, keepdims=True) - xhat * jnp.mean(dxh * xhat, axis=-1, keepdims=True))
            db_ref[...] += jnp.sum(dcv, axis=0, keepdims=True)
            dpad[rows, :] = dcv
            return carry

        lax.fori_loop(0, nc, norm_bwd, 0)
        w = w_ref[...]

        def conv_bwd(c, carry):
            base = pl.multiple_of(c * CONV_CHUNK, CONV_CHUNK)
            rows = pl.ds(base, CONV_CHUNK)
            dblk = dpad[pl.ds(base, CONV_CHUNK + CONV_PAD), :]
            dhc = _conv_taps(dblk, w, lambda k: CONV_WIDTH - 1 - k)
            a = a_ref[rows, :]
            sg = jax.nn.sigmoid(gt_ref[rows, :])
            du_ref[rows, 0:CONV_CH] = (dhc * sg).astype(BF)
            du_ref[rows, CONV_CH:2 * CONV_CH] = (dhc * a * sg * (1.0 - sg)).astype(BF)
            hblk = pad[pl.ds(base, CONV_CHUNK + CONV_PAD), :]
            dcv = dblk[0:CONV_CHUNK, :]
            for k in range(CONV_WIDTH):
                o = k + CONV_PAD - (CONV_WIDTH - 1)
                dw_ref[k:k + 1, :] += jnp.sum(dcv * hblk[o:o + CONV_CHUNK, :], axis=0, keepdims=True)
            return carry

        lax.fori_loop(0, nc, conv_bwd, 0)

    vec = pl.BlockSpec((1, CONV_CH), lambda i: (0, 0))
    full = pl.BlockSpec((t_len, CONV_CH), lambda i: (0, 0))
    wspec = pl.BlockSpec((CONV_WIDTH, CONV_CH), lambda i: (0, 0))
    vshape = jax.ShapeDtypeStruct((1, CONV_CH), F32)
    return pl.pallas_call(
        body,
        out_shape=[jax.ShapeDtypeStruct((t_len, 2 * CONV_CH), BF), jax.ShapeDtypeStruct((CONV_WIDTH, CONV_CH), F32),
                   vshape, vshape, vshape],
        grid=(1,),
        in_specs=[pl.BlockSpec((t_len, CONV_CH), lambda i: (0, 3)), pl.BlockSpec((t_len, CONV_CH), lambda i: (0, 4)), full,
                  pl.BlockSpec((t_len, CONV_CH), lambda i: (0, 2)), wspec, vec, vec],
        out_specs=[pl.BlockSpec((t_len, 2 * CONV_CH), lambda i: (0, 0)), wspec, vec, vec, vec],
        scratch_shapes=[pltpu.VMEM((t_len + CONV_PAD, CONV_CH), F32), pltpu.VMEM((t_len + CONV_PAD, CONV_CH), F32)],
        name=name, compiler_params=_params(1))(proj, proj, cv, dmix, w_dw, g_ln, b_ln)


def _mem_kv_fwd(mem, g, w, name):
    def body(m_ref, g_ref, w_ref, h_ref, kv_ref):
        mv = m_ref[...]
        h = (mv * _row_rms(mv) * g_ref[...]).astype(BF)
        h_ref[...] = h
        kv_ref[...] = _nn(h, w_ref[...])

    m_len = mem.shape[0]
    return pl.pallas_call(
        body, out_shape=[jax.ShapeDtypeStruct((m_len, D_MODEL), BF), jax.ShapeDtypeStruct((m_len, 2 * MQ_COLS), F32)],
        name=name, compiler_params=pltpu.CompilerParams(vmem_limit_bytes=VMEM_LIMIT_BYTES))(mem, g, w)


def _mem_kv_bwd(mem, g, h, w, dkv, name):
    def body(m_ref, g_ref, h_ref, w_ref, dkv_ref, dw_ref, dg_ref):
        dkv_b = dkv_ref[...].astype(BF)
        dw_ref[...] = _tn(h_ref[...], dkv_b).astype(BF)
        mv = m_ref[...]
        dg_ref[...] = jnp.sum(_nt(dkv_b, w_ref[...]) * mv * _row_rms(mv), axis=0, keepdims=True)

    return pl.pallas_call(
        body, out_shape=[jax.ShapeDtypeStruct((D_MODEL, 2 * MQ_COLS), BF), jax.ShapeDtypeStruct((1, D_MODEL), F32)],
        name=name, compiler_params=pltpu.CompilerParams(vmem_limit_bytes=VMEM_LIMIT_BYTES))(mem, g, h, w, dkv)


def _mem_attn_fwd(proj, mkv, gq2, gk2, name):
    t_len = proj.shape[0]
    tm = min(256, t_len)

    def body(q_ref, kv_ref, gq_ref, gk_ref, ym_ref, lse_ref):
        kvv = kv_ref[...]
        for pr in range(N_MEM_HEADS // 2):
            lanes = slice(pr * PAIR, (pr + 1) * PAIR)
            k_raw = kvv[:, lanes]
            kn = k_raw * _pair_rms(k_raw) * gk_ref[...]
            v = kvv[:, MQ_COLS + pr * PAIR:MQ_COLS + (pr + 1) * PAIR]
            q_raw = q_ref[:, lanes]
            qn = q_raw * _pair_rms(q_raw) * gq_ref[...]
            for hh in range(2):
                hd = 2 * pr + hh
                hs = slice(hh * HEAD_DIM, (hh + 1) * HEAD_DIM)
                s = _nt(qn[:, hs].astype(BF), kn[:, hs].astype(BF)) * SCALE
                m = jnp.max(s, axis=-1, keepdims=True)
                p = jnp.exp(s - m)
                den = jnp.sum(p, axis=-1, keepdims=True)
                ym_ref[:, hd * HEAD_DIM:(hd + 1) * HEAD_DIM] = _nn((p / den).astype(BF), v[:, hs].astype(BF)).astype(BF)
                lse_ref[:, hd:hd + 1] = m + jnp.log(den)

    m_len = mkv.shape[0]
    one = pl.BlockSpec((1, PAIR), lambda t: (0, 0))
    return pl.pallas_call(
        body, out_shape=[jax.ShapeDtypeStruct((t_len, MQ_COLS), BF), jax.ShapeDtypeStruct((t_len, N_MEM_HEADS), F32)],
        grid=(t_len // tm,),
        in_specs=[pl.BlockSpec((tm, MQ_COLS), lambda t: (t, 5)), pl.BlockSpec((m_len, 2 * MQ_COLS), lambda t: (0, 0)), one, one],
        out_specs=[pl.BlockSpec((tm, MQ_COLS), lambda t: (t, 0)), pl.BlockSpec((tm, N_MEM_HEADS), lambda t: (t, 0))],
        name=name, compiler_params=_params(1))(proj, mkv, gq2, gk2)


def _mem_attn_bwd(proj, mkv, gq2, gk2, ym, dmix, lse, name):
    t_len = proj.shape[0]
    tm = min(256, t_len)
    nt = t_len // tm
    m_len = mkv.shape[0]

    def body(q_ref, kv_ref, gq_ref, gk_ref, ym_ref, do_ref, lse_ref, dq_ref, dkv_ref, dgq_ref, dgk_ref, dkn_scr, dv_scr, scr):
        t = pl.program_id(0)

        @pl.when(t == 0)
        def _():
            dkn_scr[...] = jnp.zeros_like(dkn_scr)
            dv_scr[...] = jnp.zeros_like(dv_scr)
            dgq_ref[...] = jnp.zeros_like(dgq_ref)

        kvv = kv_ref[...]
        for pr in range(N_MEM_HEADS // 2):
            lanes = slice(pr * PAIR, (pr + 1) * PAIR)
            k_raw = kvv[:, lanes]
            kn = k_raw * _pair_rms(k_raw) * gk_ref[...]
            v = kvv[:, MQ_COLS + pr * PAIR:MQ_COLS + (pr + 1) * PAIR]
            q_raw = q_ref[:, lanes]
            rq = _pair_rms(q_raw)
            qn = q_raw * rq * gq_ref[...]
            d_o = do_ref[:, lanes]
            dsum = _pair_sum(d_o * ym_ref[:, lanes].astype(F32))
            for hh in range(2):
                hd = 2 * pr + hh
                hs = slice(hh * HEAD_DIM, (hh + 1) * HEAD_DIM)
                cols = slice(hd * HEAD_DIM, (hd + 1) * HEAD_DIM)
                qh = qn[:, hs].astype(BF)
                kh = kn[:, hs].astype(BF)
                d_oh = d_o[:, hs].astype(BF)
                p = jnp.exp(_nt(qh, kh) * SCALE - lse_ref[:, hd:hd + 1])
                dv_scr[:, cols] += _tn(p.astype(BF), d_oh)
                ds = (p * (_nt(d_oh, v[:, hs].astype(BF)) - dsum[:, hh * HEAD_DIM:hh * HEAD_DIM + 1])).astype(BF)
                scr[pr, :, hs] = _nn(ds, kh) * SCALE
                dkn_scr[:, cols] += _tn(ds, qh) * SCALE
            dx, dg = _pair_rms_bwd(scr[pr], q_raw, rq, gq_ref[...])
            dq_ref[:, lanes] = dx.astype(BF)
            dgq_ref[...] += dg

        @pl.when(t == nt - 1)
        def _():
            dgk = jnp.zeros((1, PAIR), F32)
            for pr in range(N_MEM_HEADS // 2):
                lanes = slice(pr * PAIR, (pr + 1) * PAIR)
                k_raw = kvv[:, lanes]
                dx, dg = _pair_rms_bwd(dkn_scr[:, lanes], k_raw, _pair_rms(k_raw), gk_ref[...])
                dkv_ref[:, lanes] = dx
                dgk = dgk + dg
            dkv_ref[:, MQ_COLS:2 * MQ_COLS] = dv_scr[...]
            dgk_ref[...] = dgk

    one = pl.BlockSpec((1, PAIR), lambda t: (0, 0))
    kvspec = pl.BlockSpec((m_len, 2 * MQ_COLS), lambda t: (0, 0))
    qspec = pl.BlockSpec((tm, MQ_COLS), lambda t: (t, 0))
    return pl.pallas_call(
        body,
        out_shape=[jax.ShapeDtypeStruct((t_len, MQ_COLS), BF), jax.ShapeDtypeStruct((m_len, 2 * MQ_COLS), F32),
                   jax.ShapeDtypeStruct((1, PAIR), F32), jax.ShapeDtypeStruct((1, PAIR), F32)],
        grid=(nt,),
        in_specs=[pl.BlockSpec((tm, MQ_COLS), lambda t: (t, 5)), kvspec, one, one, qspec,
                  pl.BlockSpec((tm, MQ_COLS), lambda t: (t, 3)), pl.BlockSpec((tm, N_MEM_HEADS), lambda t: (t, 0))],
        out_specs=[qspec, kvspec, one, one],
        scratch_shapes=[pltpu.VMEM((m_len, MQ_COLS), F32), pltpu.VMEM((m_len, MQ_COLS), F32),
                        pltpu.VMEM((N_MEM_HEADS // 2, tm, PAIR), F32)],
        name=name, compiler_params=_params(1))(proj, mkv, gq2, gk2, ym, dmix, lse)


MIX_GROUPS = [(0, Q_COLS), (Q_COLS, CONV_CH), (Q_COLS + CONV_CH, MQ_COLS)]


def _out_proj_fwd(x, ya, yc, ym, w_out, name, after=None):
    t_len = x.shape[0]
    tm = min(512, t_len)

    def body(x_ref, ya_ref, yc_ref, ym_ref, w_ref, xo_ref):
        y = x_ref[...]
        for (start, width), ref in zip(MIX_GROUPS, (ya_ref, yc_ref, ym_ref)):
            y = y + _nn(ref[...], w_ref[start:start + width, :])
        xo_ref[...] = y

    cols = lambda w: pl.BlockSpec((tm, w), lambda t: (t, 0))
    body, first, first_specs = _ordered_after(body, after)
    return pl.pallas_call(
        body, out_shape=jax.ShapeDtypeStruct((t_len, D_MODEL), F32), grid=(t_len // tm,),
        in_specs=first_specs + [cols(D_MODEL), cols(Q_COLS), cols(CONV_CH), cols(MQ_COLS),
                                pl.BlockSpec((D_MODEL, D_MODEL), lambda t: (0, 0))],
        out_specs=cols(D_MODEL), name=name, compiler_params=_params(1))(*first, x, ya, yc, ym, w_out)


def _out_proj_bwd(dx, ya, yc, ym, w_out, name, after=None):
    t_len = dx.shape[0]
    tm = min(512, t_len)
    nt = t_len // tm

    def body(dx_ref, ya_ref, yc_ref, ym_ref, w_ref, dmix_ref, dw_ref, acc):
        t = pl.program_id(0)

        @pl.when(t == 0)
        def _():
            acc[...] = jnp.zeros_like(acc)

        dxb = dx_ref[...].astype(BF)
        dmix_ref[...] = _nt(dxb, w_ref[...])
        for (start, width), ref in zip(MIX_GROUPS, (ya_ref, yc_ref, ym_ref)):
            acc[start:start + width, :] += _tn(ref[...], dxb)

        @pl.when(t == nt - 1)
        def _():
            dw_ref[...] = acc[...].astype(BF)

    cols = lambda w: pl.BlockSpec((tm, w), lambda t: (t, 0))
    full = pl.BlockSpec((D_MODEL, D_MODEL), lambda t: (0, 0))
    body, first, first_specs = _ordered_after(body, after)
    return pl.pallas_call(
        body, out_shape=[jax.ShapeDtypeStruct((t_len, D_MODEL), F32), jax.ShapeDtypeStruct((D_MODEL, D_MODEL), BF)],
        grid=(nt,), in_specs=first_specs + [cols(D_MODEL), cols(Q_COLS), cols(CONV_CH), cols(MQ_COLS), full],
        out_specs=[cols(D_MODEL), full], scratch_shapes=[pltpu.VMEM((D_MODEL, D_MODEL), F32)], name=name,
        compiler_params=_params(1))(*first, dx, ya, yc, ym, w_out)


N_PEERS = N_DEV - 1
HBM_SPEC = pl.BlockSpec(memory_space=pltpu.HBM)
SEM_SPEC = pl.BlockSpec(memory_space=pltpu.SEMAPHORE)
EFFECT = pltpu.SideEffectType.DATAFLOW_SIDE_EFFECTING


def _my_place():
    x, y, c = lax.axis_index("x"), lax.axis_index("y"), lax.axis_index("c")
    return x, y, c, 4 * x + 2 * y + c


def _peer(k):
    x, y, c, _ = _my_place()
    px = 1 - x if k & 4 else x
    py = 1 - y if k & 2 else y
    pc = 1 - c if k & 1 else c
    return (px, py, pc), 4 * px + 2 * py + pc


def _gather_two_level(shards, name):
    n = len(shards)

    def body(*refs):
        ins, outs = refs[:n], refs[n:2 * n]
        send_sems, recv_sems, local_sems = refs[2 * n:]
        x, y, c, me = _my_place()
        sibling = (x, y, 1 - c)
        chips = [(1 - x, y), (x, 1 - y), (1 - x, 1 - y)]
        slot = lambda px, py, pc: 4 * px + 2 * py + pc

        def copy(w, k, src, dst_slot, to):
            return pltpu.make_async_remote_copy(
                src_ref=src, dst_ref=outs[w].at[dst_slot], send_sem=send_sems.at[w * N_PEERS + k],
                recv_sem=recv_sems.at[w * N_PEERS + k], device_id=to, device_id_type=pl.DeviceIdType.MESH)

        local = [pltpu.make_async_copy(ins[w], outs[w].at[me], local_sems.at[w]) for w in range(n)]
        for cp in local:
            cp.start()
        sends = []
        for w in range(n):
            for j, chip in enumerate(chips):
                sends.append(copy(w, 1 + j, ins[w], me, (*chip, c)))
            sends.append(copy(w, 0, ins[w], me, sibling))
        for cp in sends:
            cp.start()
        for w in range(n):
            for j, chip in enumerate(chips):
                got = slot(*chip, c)
                copy(w, 1 + j, ins[w], got, (*chip, c)).wait_recv()
                fwd = copy(w, 4 + j, outs[w].at[got], got, sibling)
                fwd.start()
                sends.append(fwd)
        for w in range(n):
            copy(w, 0, ins[w], slot(x, y, 1 - c), sibling).wait_recv()
            for j, chip in enumerate(chips):
                copy(w, 4 + j, ins[w], slot(*chip, 1 - c), sibling).wait_recv()
        for cp in sends:
            cp.wait_send()
        for cp in local:
            cp.wait()

    hbm = pl.BlockSpec(memory_space=pl.ANY)
    return pl.pallas_call(
        body, out_shape=[jax.ShapeDtypeStruct((N_DEV,) + a.shape, a.dtype) for a in shards], in_specs=[hbm] * n,
        out_specs=[hbm] * n,
        scratch_shapes=[pltpu.SemaphoreType.DMA((n * N_PEERS,)), pltpu.SemaphoreType.DMA((n * N_PEERS,)),
                        pltpu.SemaphoreType.DMA((n,))],
        name=name)(*shards)


def _split_copies(srcs, lands, gather, send_sems, recv_sems):
    _, _, _, me = _my_place()
    pairs = []
    for w in range(len(srcs)):
        for k in range(1, N_DEV):
            dev, idx = _peer(k)
            src = srcs[w] if gather[w] else srcs[w].at[idx]
            sems = dict(send_sem=send_sems.at[w * N_PEERS + k - 1], recv_sem=recv_sems.at[w * N_PEERS + k - 1], device_id=dev,
                        device_id_type=pl.DeviceIdType.MESH)
            pairs.append((pltpu.make_async_remote_copy(src_ref=src, dst_ref=lands[w].at[me], **sems),
                          pltpu.make_async_remote_copy(src_ref=src, dst_ref=lands[w].at[idx], **sems)))
    return pairs


def _exchange_start(srcs, lands, gather, after, name):
    n = len(srcs)

    def body(*refs):
        src_refs, land_refs = refs[:n], refs[n:2 * n]
        send_sems, recv_sems = refs[2 * n + 1], refs[2 * n + 2]
        token = refs[-1]
        for out_going, _ in _split_copies(src_refs, land_refs, gather, send_sems, recv_sems):
            out_going.start()
        token[...] = jnp.zeros_like(token)

    arrays = list(srcs) + list(lands)
    out = pl.pallas_call(
        body, name=name,
        out_shape=(pltpu.SemaphoreType.DMA((n * N_PEERS,)), pltpu.SemaphoreType.DMA((n * N_PEERS,)),
                   *[pltpu.HBM(a.shape, a.dtype) for a in arrays], jax.ShapeDtypeStruct((8, 128), F32)),
        in_specs=[HBM_SPEC] * (2 * n) + [pl.BlockSpec(memory_space=pl.ANY)],
        out_specs=(SEM_SPEC, SEM_SPEC, *[HBM_SPEC] * (2 * n), pl.BlockSpec(memory_space=pltpu.VMEM)),
        input_output_aliases={i: i + 2 for i in range(2 * n)},
        compiler_params=pltpu.CompilerParams(has_side_effects=EFFECT),
    )(*[pltpu.with_memory_space_constraint(a, pltpu.HBM) for a in arrays], after)
    return out[0], out[1], out[2:2 + n], out[2 + n:2 + 2 * n], out[-1]


def _exchange_wait(send_sems, recv_sems, srcs, lands, gather, after, name):
    n = len(srcs)

    def body(*refs):
        src_refs, land_refs = refs[:n], refs[n:2 * n]
        for out_going, arriving in _split_copies(src_refs, land_refs, gather, refs[2 * n], refs[2 * n + 1]):
            out_going.wait_send()
            arriving.wait_recv()

    arrays = list(srcs) + list(lands)
    out = pl.pallas_call(
        body, name=name, out_shape=tuple(pltpu.HBM(a.shape, a.dtype) for a in arrays),
        in_specs=[HBM_SPEC] * (2 * n) + [SEM_SPEC, SEM_SPEC, pl.BlockSpec(memory_space=pl.ANY)],
        out_specs=tuple([HBM_SPEC] * (2 * n)), input_output_aliases={i: i for i in range(2 * n)},
        compiler_params=pltpu.CompilerParams(has_side_effects=EFFECT),
    )(*arrays, send_sems, recv_sems, after)
    return out[n:]


SIBLING = 1
CHIP_PEERS = (2, 4, 6)
NEAR_PEERS = (SIBLING,) + CHIP_PEERS


def _near_copies(srcs, lands, send_sems, recv_sems):
    _, _, _, me = _my_place()
    pairs = []
    for w in range(len(srcs)):
        for i, k in enumerate(NEAR_PEERS):
            dev, idx = _peer(k)
            sems = dict(send_sem=send_sems.at[w * len(NEAR_PEERS) + i], recv_sem=recv_sems.at[w * len(NEAR_PEERS) + i],
                        device_id=dev, device_id_type=pl.DeviceIdType.MESH)
            pairs.append((pltpu.make_async_remote_copy(src_ref=srcs[w], dst_ref=lands[w].at[me], **sems),
                          pltpu.make_async_remote_copy(src_ref=srcs[w], dst_ref=lands[w].at[idx], **sems)))
    return pairs


def _forward_copies(lands, send_sems, recv_sems):
    sibling, _ = _peer(SIBLING)
    pairs = []
    for w in range(len(lands)):
        for i, k in enumerate(CHIP_PEERS):
            _, got = _peer(k)
            _, gets = _peer(k | SIBLING)
            sems = dict(send_sem=send_sems.at[w * len(CHIP_PEERS) + i], recv_sem=recv_sems.at[w * len(CHIP_PEERS) + i],
                        device_id=sibling, device_id_type=pl.DeviceIdType.MESH)
            pairs.append((pltpu.make_async_remote_copy(src_ref=lands[w].at[got], dst_ref=lands[w].at[got], **sems),
                          pltpu.make_async_remote_copy(src_ref=lands[w].at[got], dst_ref=lands[w].at[gets], **sems)))
    return pairs


def _split_call(body, name, arrays, sems_in, sems_out, after, token):
    n = len(arrays)
    out = pl.pallas_call(
        body, name=name,
        out_shape=(*[pltpu.SemaphoreType.DMA((c,)) for c in sems_out], *[pltpu.HBM(a.shape, a.dtype) for a in arrays],
                   *([jax.ShapeDtypeStruct((8, 128), F32)] if token else [])),
        in_specs=[HBM_SPEC] * n + [SEM_SPEC] * len(sems_in) + [pl.BlockSpec(memory_space=pl.ANY)],
        out_specs=(*[SEM_SPEC] * len(sems_out), *[HBM_SPEC] * n,
                   *([pl.BlockSpec(memory_space=pltpu.VMEM)] if token else [])),
        input_output_aliases={i: i + len(sems_out) for i in range(n)},
        compiler_params=pltpu.CompilerParams(has_side_effects=EFFECT),
    )(*[pltpu.with_memory_space_constraint(a, pltpu.HBM) for a in arrays], *sems_in, after)
    k = len(sems_out)
    return list(out[:k]), list(out[k:k + n]), (out[-1] if token else None)


def _gather_start(srcs, lands, after, name):
    n = len(srcs)

    def body(*refs):
        send1, recv1 = refs[2 * n + 1], refs[2 * n + 2]
        for out_going, _ in _near_copies(refs[:n], refs[n:2 * n], send1, recv1):
            out_going.start()
        refs[-1][...] = jnp.zeros_like(refs[-1])

    sems, arrays, token = _split_call(body, name, list(srcs) + list(lands), [], [n * len(NEAR_PEERS)] * 2, after, True)
    return sems, arrays, token


def _gather_relay(sems, arrays, after, name):
    n = len(arrays) // 2

    def body(*refs):
        send1, recv1 = refs[2 * n], refs[2 * n + 1]
        send2, recv2 = refs[2 * n + 3], refs[2 * n + 4]
        near = _near_copies(refs[:n], refs[n:2 * n], send1, recv1)
        forward = _forward_copies(refs[n:2 * n], send2, recv2)
        for w in range(n):
            for i in range(len(CHIP_PEERS)):
                near[w * len(NEAR_PEERS) + 1 + i][1].wait_recv()
                forward[w * len(CHIP_PEERS) + i][0].start()
        refs[-1][...] = jnp.zeros_like(refs[-1])

    sems2, arrays, token = _split_call(body, name, arrays, sems, [n * len(CHIP_PEERS)] * 2, after, True)
    return sems + sems2, arrays, token


def _gather_finish(sems, arrays, after, name):
    n = len(arrays) // 2

    def body(*refs):
        send1, recv1, send2, recv2 = refs[2 * n:2 * n + 4]
        near = _near_copies(refs[:n], refs[n:2 * n], send1, recv1)
        for out_going, _ in near:
            out_going.wait_send()
        for w in range(n):
            near[w * len(NEAR_PEERS)][1].wait_recv()
        for out_going, arriving in _forward_copies(refs[n:2 * n], send2, recv2):
            out_going.wait_send()
            arriving.wait_recv()

    _, arrays, _ = _split_call(body, name, arrays, sems, [], after, False)
    return arrays[n:]


def _own_slot(a, me, gather):
    mine = a if gather else lax.dynamic_index_in_dim(a, me, 0, keepdims=False)
    return lax.dynamic_update_index_in_dim(lax.empty((N_DEV,) + mine.shape, mine.dtype), mine, me, 0)


def _adamw_math(w, g, m, v):
    m2 = ADAM_B1 * m + (1.0 - ADAM_B1) * g
    v2 = ADAM_B2 * v + (1.0 - ADAM_B2) * (g * g)
    m_hat = m2 / (1.0 - ADAM_B1 ** ADAM_STEP)
    v_hat = v2 / (1.0 - ADAM_B2 ** ADAM_STEP)
    return -ADAM_LR * (m_hat / (jnp.sqrt(v_hat) + ADAM_EPS) + ADAM_WD * w), m2, v2


def _sum_adamw(parts, w, m, v, name):
    rows, cols = w.shape
    tr = rows if rows <= 512 else 256

    def body(p_ref, w_ref, m_ref, v_ref, g_ref, d_ref, m2_ref, v2_ref):
        g = p_ref[0].astype(F32)
        for s in range(1, N_DEV):
            g = g + p_ref[s].astype(F32)
        g_ref[...] = g
        d_ref[...], m2_ref[...], v2_ref[...] = _adamw_math(w_ref[...], g, m_ref[...], v_ref[...])

    blk = pl.BlockSpec((tr, cols), lambda i: (i, 0))
    shape = jax.ShapeDtypeStruct((rows, cols), F32)
    return pl.pallas_call(
        body, out_shape=[shape] * 4, grid=(rows // tr,),
        in_specs=[pl.BlockSpec((N_DEV, tr, cols), lambda i: (0, i, 0)), blk, blk, blk], out_specs=[blk] * 4, name=name,
        compiler_params=_params(1))(parts, w, m, v)


def _slot_sum(parts, name):
    def body(p_ref, g_ref):
        g = p_ref[0]
        for s in range(1, N_DEV):
            g = g + p_ref[s]
        g_ref[...] = g

    return pl.pallas_call(body, out_shape=jax.ShapeDtypeStruct(parts.shape[1:], F32), name=name)(parts)


def _adamw(g, w, m, v, name):
    def body(g_ref, w_ref, m_ref, v_ref, d_ref, m2_ref, v2_ref):
        d_ref[...], m2_ref[...], v2_ref[...] = _adamw_math(w_ref[...], g_ref[...], m_ref[...], v_ref[...])

    shape = jax.ShapeDtypeStruct(w.shape, F32)
    return pl.pallas_call(body, out_shape=[shape] * 3, name=name)(g, w, m, v)


def _pack_small(vals):
    flat = jnp.concatenate([vals[k].reshape(-1) for k in SMALL])
    return jnp.pad(flat, (0, SMALL_ROWS * D_MODEL - flat.shape[0])).reshape(SMALL_ROWS, D_MODEL)


def _unpack_small(packed, like):
    flat, out, at = packed.reshape(-1), {}, 0
    for k in SMALL:
        size = like[k].size
        out[k] = flat[at:at + size].reshape(like[k].shape)
        at += size
    return out


FFN1 = ["w_ffn1_gate", "w_ffn1_up", "w_ffn1_down"]
FFN2 = ["w_ffn2_gate", "w_ffn2_up", "w_ffn2_down"]
MIXER = ["w_in", "w_mem_kv", "w_out"]
TRANSPOSED = ["w_ffn1_gate", "w_ffn1_up", "w_ffn2_gate", "w_ffn2_up", "w_in"]


def _mixer_fwd(x1, mem, tables, sm, big, after_attn=None):
    cos_t, sin_t, gq2, gk2, gmq2, gmk2 = tables
    h2, proj = _in_proj_fwd(x1, sm["g_mix"], big["w_in"], "in_proj_fwd")
    ya, lse = _attn_fwd(proj, cos_t, sin_t, gq2, gk2, sm["sinks"], "attn_fwd")
    token = None if after_attn is None else after_attn(ya)
    yc, cv = _conv_fwd(proj, big["w_dw"], sm["b_dw"], sm["g_conv_ln"], sm["b_conv_ln"], "conv_fwd")
    hm, mkv = _mem_kv_fwd(mem, sm["g_mem"], big["w_mem_kv"], "mem_kv_fwd")
    ym, lse_m = _mem_attn_fwd(proj, mkv, gmq2, gmk2, "mem_attn_fwd")
    x2 = _out_proj_fwd(x1, ya, yc, ym, big["w_out"], "out_proj_fwd", token)
    return x2, (h2, proj, ya, lse, yc, cv, hm, mkv, ym, lse_m)


def _mixer_bwd(dx2, x1, mem, tables, sm, big, saved, after):
    cos_t, sin_t, gq2, gk2, gmq2, gmk2 = tables
    h2, proj, ya, lse, yc, cv, hm, mkv, ym, lse_m = saved
    g = {}
    dmix, g["w_out"] = _out_proj_bwd(dx2, ya, yc, ym, big["w_out"], "out_proj_bwd", after)
    dq, dgq, g["sinks"] = _attn_bwd_q(proj, cos_t, sin_t, gq2, gk2, sm["sinks"], ya, dmix, lse, "attn_bwd_q")
    dk, dv, dgk = _attn_bwd_kv(proj, cos_t, sin_t, gq2, gk2, ya, dmix, lse, "attn_bwd_kv")
    du, g["w_dw"], g["b_dw"], g["g_conv_ln"], g["b_conv_ln"] = _conv_bwd(
        proj, cv, dmix, big["w_dw"], sm["g_conv_ln"], sm["b_conv_ln"], "conv_bwd")
    dmq, dmkv, dgmq, dgmk = _mem_attn_bwd(proj, mkv, gmq2, gmk2, ym, dmix, lse_m, "mem_attn_bwd")
    g["w_mem_kv"], g["g_mem"] = _mem_kv_bwd(mem, sm["g_mem"], hm, big["w_mem_kv"], dmkv, "mem_kv_bwd")
    dx1, g["g_mix"], g["w_in"] = _in_proj_bwd(dq, dk, dv, du, dmq, big["w_in"], h2, x1, sm["g_mix"], dx2, "in_proj_bwd")
    fold = lambda a: a[:, :HEAD_DIM] + a[:, HEAD_DIM:]
    g["g_q"], g["g_k"], g["g_mq"], g["g_mk"] = fold(dgq), fold(dgk), fold(dgmq), fold(dgmk)
    return dx1, g


def _tables(positions, sm):
    pair = lambda a: jnp.tile(a, (1, 2))
    return _rope_tables(positions) + (pair(sm["g_q"]), pair(sm["g_k"]), pair(sm["g_mq"]), pair(sm["g_mk"]))


def kernel(x, mem, positions, g_ffn1, w_ffn1_gate, w_ffn1_up, w_ffn1_down, g_mix, w_in, g_q, g_k, sinks, w_dw, b_dw, g_conv_ln, b_conv_ln, g_mem, w_mem_kv, g_mq, g_mk, w_out, g_ffn2, w_ffn2_gate, w_ffn2_up, w_ffn2_down, loss_target, m_g_ffn1, m_w_ffn1_gate, m_w_ffn1_up, m_w_ffn1_down, m_g_mix, m_w_in, m_g_q, m_g_k, m_sinks, m_w_dw, m_b_dw, m_g_conv_ln, m_b_conv_ln, m_g_mem, m_w_mem_kv, m_g_mq, m_g_mk, m_w_out, m_g_ffn2, m_w_ffn2_gate, m_w_ffn2_up, m_w_ffn2_down, v_g_ffn1, v_w_ffn1_gate, v_w_ffn1_up, v_w_ffn1_down, v_g_mix, v_w_in, v_g_q, v_g_k, v_sinks, v_w_dw, v_b_dw, v_g_conv_ln, v_b_conv_ln, v_g_mem, v_w_mem_kv, v_g_mq, v_g_mk, v_w_out, v_g_ffn2, v_w_ffn2_gate, v_w_ffn2_up, v_w_ffn2_down):
    w = dict(g_ffn1=g_ffn1, w_ffn1_gate=w_ffn1_gate, w_ffn1_up=w_ffn1_up, w_ffn1_down=w_ffn1_down, g_mix=g_mix, w_in=w_in,
             g_q=g_q, g_k=g_k, sinks=sinks, w_dw=w_dw, b_dw=b_dw, g_conv_ln=g_conv_ln, b_conv_ln=b_conv_ln, g_mem=g_mem,
             w_mem_kv=w_mem_kv, g_mq=g_mq, g_mk=g_mk, w_out=w_out, g_ffn2=g_ffn2, w_ffn2_gate=w_ffn2_gate,
             w_ffn2_up=w_ffn2_up, w_ffn2_down=w_ffn2_down)
    mo = dict(g_ffn1=m_g_ffn1, w_ffn1_gate=m_w_ffn1_gate, w_ffn1_up=m_w_ffn1_up, w_ffn1_down=m_w_ffn1_down, g_mix=m_g_mix,
              w_in=m_w_in, g_q=m_g_q, g_k=m_g_k, sinks=m_sinks, w_dw=m_w_dw, b_dw=m_b_dw, g_conv_ln=m_g_conv_ln,
              b_conv_ln=m_b_conv_ln, g_mem=m_g_mem, w_mem_kv=m_w_mem_kv, g_mq=m_g_mq, g_mk=m_g_mk, w_out=m_w_out,
              g_ffn2=m_g_ffn2, w_ffn2_gate=m_w_ffn2_gate, w_ffn2_up=m_w_ffn2_up, w_ffn2_down=m_w_ffn2_down)
    vo = dict(g_ffn1=v_g_ffn1, w_ffn1_gate=v_w_ffn1_gate, w_ffn1_up=v_w_ffn1_up, w_ffn1_down=v_w_ffn1_down, g_mix=v_g_mix,
              w_in=v_w_in, g_q=v_g_q, g_k=v_g_k, sinks=v_sinks, w_dw=v_w_dw, b_dw=v_b_dw, g_conv_ln=v_g_conv_ln,
              b_conv_ln=v_b_conv_ln, g_mem=v_g_mem, w_mem_kv=v_w_mem_kv, g_mq=v_g_mq, g_mk=v_g_mk, w_out=v_w_out,
              g_ffn2=v_g_ffn2, w_ffn2_gate=v_w_ffn2_gate, w_ffn2_up=v_w_ffn2_up, w_ffn2_down=v_w_ffn2_down)
    me = _my_place()[3]
    sm = {k: w[k] for k in SMALL}
    flip = lambda k, a: a.T if k in TRANSPOSED else a
    as_bf16 = lambda names: [flip(k, w[k][0]).astype(BF) for k in names]
    zones = lambda arrays, gather: [_own_slot(a, me, gather) for a in arrays]
    out_g, out_d, out_m, out_v = {}, {}, {}, {}

    def update(names, parts_list):
        for k, parts in zip(names, parts_list):
            new = _sum_adamw(parts, flip(k, w[k][0]), flip(k, mo[k][0]), flip(k, vo[k][0]), "adamw_" + k)
            out_g[k], out_d[k], out_m[k], out_v[k] = [flip(k, a)[None] for a in new]

    w1 = _gather_two_level(as_bf16(FFN1), "gather_ffn1")
    mix_src = as_bf16(MIXER) + [w["w_dw"][0]]
    mix_sems, mix_arrays, mix_token = _gather_start(mix_src, zones(mix_src, True), w1[0], "gather_mixer_start")
    f2_src = as_bf16(FFN2)
    f2_sems, f2_arrays, f2_token = _gather_start(f2_src, zones(f2_src, True), mix_token, "gather_ffn2_start")

    tables = _tables(positions[0], sm)
    h1, gate1, up1, x1 = _ffn_fwd(x[0], sm["g_ffn1"], *w1, None, "ffn1_fwd", after=f2_token)
    mix_sems, mix_arrays, mix_token = _gather_relay(mix_sems, mix_arrays, x1, "gather_mixer_relay")
    got = _gather_finish(mix_sems, mix_arrays, mix_token, "gather_mixer_finish")
    big = dict(w_in=got[0].reshape(IN_COLS, D_MODEL), w_mem_kv=got[1].reshape(D_MODEL, 2 * MQ_COLS),
               w_out=got[2].reshape(D_MODEL, D_MODEL), w_dw=got[3].transpose(1, 0, 2).reshape(CONV_WIDTH, CONV_CH))
    relayed = []

    def relay_ffn2(ya):
        relayed.extend(_gather_relay(f2_sems, f2_arrays, ya, "gather_ffn2_relay"))
        return relayed[2]

    x2, saved = _mixer_fwd(x1, mem[0], tables, sm, big, relay_ffn2)
    w2 = _gather_finish(relayed[0], relayed[1], x2, "gather_ffn2_finish")
    h3, gate2, up2, dy, loss_part = _ffn_fwd(x2, sm["g_ffn2"], *w2, loss_target[0], "ffn2_fwd")

    grads = {}
    dyb2, act2, dgate2, dup2, dx2, grads["g_ffn2"] = _ffn_bwd_act(dy, x2, sm["g_ffn2"], gate2, up2, *w2, "ffn2_bwd_act")
    g_f2 = list(_ffn_bwd_w(h3, dyb2, act2, dgate2, dup2, "ffn2_bwd_w"))
    r_f2 = _exchange_start(g_f2, zones(g_f2, False), [False] * 3, dx2, "scatter_ffn2_start")
    dx1, g_mid = _mixer_bwd(dx2, x1, mem[0], tables, sm, big, saved, r_f2[4])
    grads.update(g_mid)
    g_mix = [g_mid["w_in"].reshape(N_DEV, IN_COLS // N_DEV, D_MODEL),
             g_mid["w_mem_kv"].reshape(N_DEV, D_MODEL // N_DEV, 2 * MQ_COLS),
             g_mid["w_out"].reshape(N_DEV, D_MODEL // N_DEV, D_MODEL)]
    r_mix = _exchange_start(g_mix, zones(g_mix, False), [False] * 3, dx1, "scatter_mixer_start")
    dyb1, act1, dgate1, dup1, grad_x, grads["g_ffn1"] = _ffn_bwd_act(dx1, x[0], sm["g_ffn1"], gate1, up1, *w1,
                                                                     "ffn1_bwd_act", after=r_mix[4])
    g_f1 = list(_ffn_bwd_w(h1, dyb1, act1, dgate1, dup1, "ffn1_bwd_w"))
    dw_flat = grads["w_dw"].reshape(-1)
    packed = jnp.concatenate([_pack_small(grads),
                              jnp.pad(dw_flat, (0, SMALL_ROWS * D_MODEL - dw_flat.shape[0])).reshape(SMALL_ROWS, D_MODEL)])
    last_src, last_kind = g_f1 + [packed], [False] * 3 + [True]
    r_f1 = _exchange_start(last_src, zones(g_f1, False) + zones([packed], True), last_kind, grad_x, "scatter_ffn1_start")

    all_done = lambda names: sum(out_d[k][:, :1, :1] for k in names)
    update(FFN2, _exchange_wait(*r_f2[:4], [False] * 3, r_f1[4], "scatter_ffn2_wait"))
    update(MIXER, _exchange_wait(*r_mix[:4], [False] * 3, all_done(FFN2), "scatter_mixer_wait"))
    last = _exchange_wait(*r_f1[:4], last_kind, all_done(MIXER), "scatter_ffn1_wait")
    update(FFN1, last[:3])
    small_sum = _slot_sum(last[3], "small_grad_sum")
    loss_local = loss_part[0, 0]
    g_small = small_sum[:SMALL_ROWS]
    d, m2, v2 = _adamw(g_small, _pack_small(w), _pack_small(mo), _pack_small(vo), "adamw_small")
    for dst, val in ((out_g, g_small), (out_d, d), (out_m, m2), (out_v, v2)):
        dst.update(_unpack_small(val, sm))
    g_dw = small_sum[SMALL_ROWS:].reshape(-1)[:CONV_WIDTH * CONV_CH].reshape(CONV_WIDTH, CONV_CH)
    g_dw = lax.dynamic_slice_in_dim(g_dw, me * (CONV_CH // N_DEV), CONV_CH // N_DEV, axis=1)
    d, m2, v2 = _adamw(g_dw, w["w_dw"][0], mo["w_dw"][0], vo["w_dw"][0], "adamw_w_dw")
    out_g["w_dw"], out_d["w_dw"], out_m["w_dw"], out_v["w_dw"] = g_dw[None], d[None], m2[None], v2[None]

    loss = lax.psum(loss_local, ("x", "y", "c"))
    return (loss, grad_x[None], *[out_g[k] for k in WEIGHTS], *[out_d[k] for k in WEIGHTS], *[out_m[k] for k in WEIGHTS],
            *[out_v[k] for k in WEIGHTS])
```

```python
import jax
import jax.numpy as jnp
from jax import lax
from jax.experimental import pallas as pl
from jax.experimental.pallas import tpu as pltpu

D_MODEL = 1024
N_DEV = 8
FF_BLOCK = 352
D_FF = N_DEV * FF_BLOCK
FF_TILE = 256
N_FF_TILES = D_FF // FF_TILE
FF_ROWS = 1024
HEAD_DIM = 64
PAIR = 2 * HEAD_DIM
N_Q_HEADS = 8
Q_PER_KV = 4
ATTN_BLOCK = 128
Q_COLS = 512
KV_COLS = 128
CONV_CH = 256
MQ_COLS = 256
IN_COLS = 1536
CONV_WIDTH = 31
CONV_PAD = 32
CONV_CHUNK = 256
N_MEM_HEADS = 4
ROPE_THETA = 500000.0
ROPE_HALF = 8
EPS = 1e-6
SCALE = HEAD_DIM ** -0.5
NEG = -1e30
ADAM_LR, ADAM_B1, ADAM_B2, ADAM_EPS, ADAM_WD, ADAM_STEP = 0.001, 0.9, 0.999, 1e-08, 0.01, 10
VMEM_LIMIT_BYTES = 56 * 1024 * 1024
BF = jnp.bfloat16
F32 = jnp.float32

SMALL = ["g_ffn1", "g_mix", "g_mem", "g_ffn2", "g_q", "g_k", "g_mq", "g_mk", "sinks", "b_dw", "g_conv_ln", "b_conv_ln"]
WEIGHTS = ["g_ffn1", "w_ffn1_gate", "w_ffn1_up", "w_ffn1_down", "g_mix", "w_in", "g_q", "g_k", "sinks", "w_dw", "b_dw",
           "g_conv_ln", "b_conv_ln", "g_mem", "w_mem_kv", "g_mq", "g_mk", "w_out", "g_ffn2", "w_ffn2_gate", "w_ffn2_up",
           "w_ffn2_down"]
SMALL_ROWS = 8
PACK_ROWS = 16


def _nn(a, b):
    return jnp.dot(a, b, preferred_element_type=F32)


def _nt(a, b):
    return lax.dot_general(a, b, (((1,), (1,)), ((), ())), preferred_element_type=F32)


def _tn(a, b):
    return lax.dot_general(a, b, (((0,), (0,)), ((), ())), preferred_element_type=F32)


def _params(n_grid):
    return pltpu.CompilerParams(dimension_semantics=("arbitrary",) * n_grid, vmem_limit_bytes=VMEM_LIMIT_BYTES)


def _ordered_after(body, after):
    if after is None:
        return body, (), []

    def body_after(after_ref, *refs):
        body(*refs)

    return body_after, (after,), [pl.BlockSpec(memory_space=pl.ANY)]


def _row_rms(xv):
    return lax.rsqrt(jnp.mean(xv * xv, axis=-1, keepdims=True) + EPS)


def _rms_bwd(dh, xv, r, g):
    u = dh * g
    dx = r * u - xv * (r * r * r) * jnp.mean(u * xv, axis=-1, keepdims=True)
    return dx, jnp.sum(dh * xv * r, axis=0, keepdims=True)


def _sum_all(a):
    return jnp.sum(jnp.sum(a, axis=1, keepdims=True), axis=0, keepdims=True)


def _pair_sums(vs):
    row = lax.broadcasted_iota(jnp.int32, (PAIR, PAIR), 0) >= HEAD_DIM
    col = lax.broadcasted_iota(jnp.int32, (PAIR, PAIR), 1) >= HEAD_DIM
    same_head = (row == col).astype(BF)
    stacked = jnp.concatenate(vs, axis=0) if len(vs) > 1 else vs[0]
    hi = stacked.astype(BF)
    lo = (stacked - hi.astype(F32)).astype(BF)
    n = stacked.shape[0]
    both = _nn(jnp.concatenate([hi, lo], axis=0), same_head)
    total = both[:n] + both[n:]
    out, at = [], 0
    for v in vs:
        out.append(total[at:at + v.shape[0]])
        at += v.shape[0]
    return out


def _pair_sum(v):
    return _pair_sums([v])[0]


def _pair_rms_many(xs):
    return [lax.rsqrt(s * (1.0 / HEAD_DIM) + EPS) for s in _pair_sums([xv * xv for xv in xs])]


def _pair_rms(xv):
    return _pair_rms_many([xv])[0]


def _pair_rms_bwd_many(dxns, xs, rs, g):
    us = [dxn * g for dxn in dxns]
    sums = _pair_sums([u * xv for u, xv in zip(us, xs)])
    return [(r * u - xv * (r * r * r) * (s * (1.0 / HEAD_DIM)), jnp.sum(dxn * xv * r, axis=0, keepdims=True))
            for dxn, xv, r, u, s in zip(dxns, xs, rs, us, sums)]


def _pair_rms_bwd(dxn, xv, r, g):
    return _pair_rms_bwd_many([dxn], [xv], [r], g)[0]


def _rope_mask(shape):
    lane = lax.broadcasted_iota(jnp.int32, shape, 1)
    return ((lane & (HEAD_DIM - 1)) < 2 * ROPE_HALF).astype(F32)


def _partner(v):
    return pltpu.roll(v, ROPE_HALF, 1) + pltpu.roll(v, PAIR - ROPE_HALF, 1)


def _rope(xn, cos_t, sin_t):
    return xn * cos_t + _partner(xn * _rope_mask(xn.shape)) * sin_t


def _rope_t(d, cos_t, sin_t):
    return d * cos_t + _partner(d * sin_t) * _rope_mask(d.shape)


def _rope_tables(positions):
    inv_freq = ROPE_THETA ** (-jnp.arange(ROPE_HALF, dtype=F32) / ROPE_HALF)
    ang = positions.astype(F32)[:, None] * inv_freq
    cos, sin = jnp.cos(ang), jnp.sin(ang)
    t = positions.shape[0]
    cos_h = jnp.concatenate([cos, cos, jnp.ones((t, HEAD_DIM - 2 * ROPE_HALF), F32)], axis=1)
    sin_h = jnp.concatenate([-sin, sin, jnp.zeros((t, HEAD_DIM - 2 * ROPE_HALF), F32)], axis=1)
    return jnp.tile(cos_h, (1, 2)), jnp.tile(sin_h, (1, 2))


def _ff_rows(w):
    return w.reshape(D_FF, D_MODEL)


def _ffn_fwd(x, g, wg, wu, wd, target, name, after=None):
    t_len = x.shape[0]
    tm = min(FF_ROWS, t_len)
    with_loss = target is not None

    def body(*refs):
        if with_loss:
            x_ref, g_ref, wg_ref, wu_ref, wd_ref, t_ref, h_ref, gg_ref, uu_ref, dy_ref, loss_ref, acc = refs
        else:
            x_ref, g_ref, wg_ref, wu_ref, wd_ref, h_ref, gg_ref, uu_ref, xo_ref, acc = refs
        t, j = pl.program_id(0), pl.program_id(1)

        @pl.when(j == 0)
        def _():
            xv = x_ref[...]
            h_ref[...] = (xv * _row_rms(xv) * g_ref[...]).astype(BF)
            acc[...] = jnp.zeros_like(acc)

        h = h_ref[...]
        gate = _nt(h, wg_ref[...])
        up = _nt(h, wu_ref[...])
        gg_ref[...] = gate.astype(BF)
        uu_ref[...] = up.astype(BF)
        act = (gate * jax.nn.sigmoid(gate) * up).astype(BF)
        acc[...] += _nn(act, wd_ref[...])

        @pl.when(j == N_FF_TILES - 1)
        def _():
            xo = x_ref[...] + 0.5 * acc[...]
            if with_loss:
                err = xo - t_ref[...]
                dy_ref[...] = err * (1.0 / D_MODEL)

                @pl.when(t == 0)
                def _():
                    loss_ref[...] = jnp.zeros_like(loss_ref)

                loss_ref[...] += _sum_all(err * err) * (0.5 / D_MODEL)
            else:
                xo_ref[...] = xo

    row = pl.BlockSpec((tm, D_MODEL), lambda t, j: (t, 0))
    vec = pl.BlockSpec((1, D_MODEL), lambda t, j: (0, 0))
    w_spec = pl.BlockSpec((FF_TILE, D_MODEL), lambda t, j: (j, 0))
    blk = pl.BlockSpec((tm, FF_TILE), lambda t, j: (t, j))
    in_specs = [row, vec, w_spec, w_spec, w_spec] + ([row] if with_loss else [])
    out_shape = [jax.ShapeDtypeStruct((t_len, D_MODEL), BF),
                 jax.ShapeDtypeStruct((t_len, D_FF), BF),
                 jax.ShapeDtypeStruct((t_len, D_FF), BF),
                 jax.ShapeDtypeStruct((t_len, D_MODEL), F32)]
    out_specs = [row, blk, blk, row]
    if with_loss:
        out_shape.append(jax.ShapeDtypeStruct((1, 128), F32))
        out_specs.append(pl.BlockSpec((1, 128), lambda t, j: (0, 0)))
    args = (x, g, _ff_rows(wg), _ff_rows(wu), _ff_rows(wd)) + ((target,) if with_loss else ())
    body, first, first_specs = _ordered_after(body, after)
    return pl.pallas_call(body, out_shape=out_shape, grid=(t_len // tm, N_FF_TILES), in_specs=first_specs + in_specs,
                          out_specs=out_specs, scratch_shapes=[pltpu.VMEM((tm, D_MODEL), F32)], name=name,
                          compiler_params=_params(2))(*first, *args)


def _ffn_bwd_act(dy, x, g, gate, up, wg, wu, wd, name, after=None):
    t_len = x.shape[0]
    tm = min(FF_ROWS, t_len)

    def body(dy_ref, x_ref, g_ref, gg_ref, uu_ref, wg_ref, wu_ref, wd_ref,
             dyb_ref, act_ref, dgg_ref, duu_ref, dx_ref, dg_ref, acc):
        t, j = pl.program_id(0), pl.program_id(1)

        @pl.when(j == 0)
        def _():
            dyb_ref[...] = (0.5 * dy_ref[...]).astype(BF)
            acc[...] = jnp.zeros_like(acc)

        d_act = _nt(dyb_ref[...], wd_ref[...])
        gate = gg_ref[...].astype(F32)
        upv = uu_ref[...].astype(F32)
        sig = jax.nn.sigmoid(gate)
        silu = gate * sig
        d_up = (d_act * silu).astype(BF)
        d_gate = (d_act * upv * (sig * (1.0 + gate * (1.0 - sig)))).astype(BF)
        act_ref[...] = (silu * upv).astype(BF)
        dgg_ref[...] = d_gate
        duu_ref[...] = d_up
        acc[...] += _nn(jnp.concatenate([d_gate, d_up], axis=1), jnp.concatenate([wg_ref[...], wu_ref[...]], axis=0))

        @pl.when(j == N_FF_TILES - 1)
        def _():
            xv = x_ref[...]
            dx, dg = _rms_bwd(acc[...], xv, _row_rms(xv), g_ref[...])
            dx_ref[...] = dy_ref[...] + dx

            @pl.when(t == 0)
            def _():
                dg_ref[...] = jnp.zeros_like(dg_ref)

            dg_ref[...] += dg

    row = pl.BlockSpec((tm, D_MODEL), lambda t, j: (t, 0))
    vec = pl.BlockSpec((1, D_MODEL), lambda t, j: (0, 0))
    w_spec = pl.BlockSpec((FF_TILE, D_MODEL), lambda t, j: (j, 0))
    blk = pl.BlockSpec((tm, FF_TILE), lambda t, j: (t, j))
    blk_shape = jax.ShapeDtypeStruct((t_len, D_FF), BF)
    body, first, first_specs = _ordered_after(body, after)
    return pl.pallas_call(
        body,
        out_shape=[jax.ShapeDtypeStruct((t_len, D_MODEL), BF), blk_shape, blk_shape, blk_shape,
                   jax.ShapeDtypeStruct((t_len, D_MODEL), F32), jax.ShapeDtypeStruct((1, D_MODEL), F32)],
        grid=(t_len // tm, N_FF_TILES), in_specs=first_specs + [row, row, vec, blk, blk, w_spec, w_spec, w_spec],
        out_specs=[row, blk, blk, blk, row, vec], scratch_shapes=[pltpu.VMEM((tm, D_MODEL), F32)], name=name,
        compiler_params=_params(2))(*first, dy, x, g, gate, up, _ff_rows(wg), _ff_rows(wu), _ff_rows(wd))


def _ffn_bwd_w(h, dyb, act, d_gate, d_up, name):
    t_len = h.shape[0]
    tm = min(FF_ROWS, t_len)
    nt = t_len // tm

    def body(h_ref, dyb_ref, act_ref, dgg_ref, duu_ref, dwg_ref, dwu_ref, dwd_ref, ag, au, ad):
        t = pl.program_id(1)

        @pl.when(t == 0)
        def _():
            ag[...] = jnp.zeros_like(ag)
            au[...] = jnp.zeros_like(au)
            ad[...] = jnp.zeros_like(ad)

        h = h_ref[...]
        ag[...] += _tn(dgg_ref[...], h)
        au[...] += _tn(duu_ref[...], h)
        ad[...] += _tn(act_ref[...], dyb_ref[...])

        @pl.when(t == nt - 1)
        def _():
            dwg_ref[...] = ag[...].astype(BF)
            dwu_ref[...] = au[...].astype(BF)
            dwd_ref[...] = ad[...].astype(BF)

    row = pl.BlockSpec((tm, D_MODEL), lambda j, t: (t, 0))
    blk = pl.BlockSpec((tm, FF_TILE), lambda j, t: (t, j))
    w_spec = pl.BlockSpec((FF_TILE, D_MODEL), lambda j, t: (j, 0))
    grads = pl.pallas_call(
        body,
        out_shape=[jax.ShapeDtypeStruct((D_FF, D_MODEL), BF)] * 3,
        grid=(N_FF_TILES, nt), in_specs=[row, row, blk, blk, blk], out_specs=[w_spec] * 3,
        scratch_shapes=[pltpu.VMEM((FF_TILE, D_MODEL), F32)] * 3,
        name=name, compiler_params=_params(2))(h, dyb, act, d_gate, d_up)
    return [a.reshape(N_DEV, FF_BLOCK, D_MODEL) for a in grads]


def _in_proj_fwd(x, g, w_in, name):
    t_len = x.shape[0]
    tm = min(512, t_len)

    def body(x_ref, g_ref, w_ref, h_ref, p_ref):
        xv = x_ref[...]
        h = (xv * _row_rms(xv) * g_ref[...]).astype(BF)
        h_ref[...] = h
        p_ref[...] = _nt(h, w_ref[...])

    row = pl.BlockSpec((tm, D_MODEL), lambda t: (t, 0))
    return pl.pallas_call(
        body, out_shape=[jax.ShapeDtypeStruct((t_len, D_MODEL), BF), jax.ShapeDtypeStruct((t_len, IN_COLS), F32)],
        grid=(t_len // tm,),
        in_specs=[row, pl.BlockSpec((1, D_MODEL), lambda t: (0, 0)), pl.BlockSpec((IN_COLS, D_MODEL), lambda t: (0, 0))],
        out_specs=[row, pl.BlockSpec((tm, IN_COLS), lambda t: (t, 0))], name=name, compiler_params=_params(1))(x, g, w_in)


def _in_proj_bwd(dq, dk, dv, du, dmq, w_in, h, x, g, dres, name):
    t_len = x.shape[0]
    tm = min(512, t_len)
    nt = t_len // tm
    groups = [(0, Q_COLS), (Q_COLS, KV_COLS), (Q_COLS + KV_COLS, KV_COLS), (Q_COLS + 2 * KV_COLS, 2 * CONV_CH),
              (Q_COLS + 2 * KV_COLS + 2 * CONV_CH, MQ_COLS)]

    def body(dq_ref, dk_ref, dv_ref, du_ref, dmq_ref, w_ref, h_ref, x_ref, g_ref, dres_ref, dx_ref, dg_ref, dw_ref, acc):
        t = pl.program_id(0)

        @pl.when(t == 0)
        def _():
            acc[...] = jnp.zeros_like(acc)
            dg_ref[...] = jnp.zeros_like(dg_ref)

        h = h_ref[...]
        dh = jnp.zeros((tm, D_MODEL), F32)
        for (start, width), ref in zip(groups, (dq_ref, dk_ref, dv_ref, du_ref, dmq_ref)):
            piece = ref[...]
            dh = dh + _nn(piece, w_ref[start:start + width, :])
            acc[start:start + width, :] += _tn(piece, h)
        xv = x_ref[...]
        dx, dg = _rms_bwd(dh, xv, _row_rms(xv), g_ref[...])
        dx_ref[...] = dres_ref[...] + dx
        dg_ref[...] += dg

        @pl.when(t == nt - 1)
        def _():
            dw_ref[...] = acc[...].astype(BF)

    def cols(width):
        return pl.BlockSpec((tm, width), lambda t: (t, 0))

    row = cols(D_MODEL)
    vec = pl.BlockSpec((1, D_MODEL), lambda t: (0, 0))
    full = pl.BlockSpec((IN_COLS, D_MODEL), lambda t: (0, 0))
    return pl.pallas_call(
        body,
        out_shape=[jax.ShapeDtypeStruct((t_len, D_MODEL), F32), jax.ShapeDtypeStruct((1, D_MODEL), F32),
                   jax.ShapeDtypeStruct((IN_COLS, D_MODEL), BF)],
        grid=(nt,),
        in_specs=[cols(Q_COLS), cols(KV_COLS), cols(KV_COLS), cols(2 * CONV_CH), cols(MQ_COLS), full, row, row, vec, row],
        out_specs=[row, vec, full], scratch_shapes=[pltpu.VMEM((IN_COLS, D_MODEL), F32)], name=name,
        compiler_params=_params(1))(dq, dk, dv, du, dmq, w_in, h, x, g, dres)


def _attn_fwd(proj, cos_t, sin_t, gq2, gk2, sinks, name):
    t_len = proj.shape[0]
    nb = t_len // ATTN_BLOCK

    def body(q_ref, kvc_ref, kvp_ref, cq_ref, sq_ref, cp_ref, sp_ref, gq_ref, gk_ref, sk_ref, ya_ref, lse_ref):
        n = pl.program_id(0)
        cq, sq = cq_ref[...], sq_ref[...]
        c_all = jnp.concatenate([cp_ref[...], cq], axis=0)
        s_all = jnp.concatenate([sp_ref[...], sq], axis=0)
        kv = jnp.concatenate([kvp_ref[...], kvc_ref[...]], axis=0)
        k_raw, v = kv[:, :PAIR], kv[:, PAIR:]
        q_raws = [q_ref[:, pr * PAIR:(pr + 1) * PAIR] for pr in range(N_Q_HEADS // 2)]
        rms = _pair_rms_many(q_raws + [k_raw])
        kn = _rope(k_raw * rms[-1] * gk_ref[...], c_all, s_all)
        row = lax.broadcasted_iota(jnp.int32, (ATTN_BLOCK, 2 * ATTN_BLOCK), 0)
        col = lax.broadcasted_iota(jnp.int32, (ATTN_BLOCK, 2 * ATTN_BLOCK), 1)
        rel = row + ATTN_BLOCK - col
        valid = (rel >= 0) & (rel < ATTN_BLOCK) & ((col >= ATTN_BLOCK) | (n > 0))
        for pr in range(N_Q_HEADS // 2):
            qn = _rope(q_raws[pr] * rms[pr] * gq_ref[...], cq, sq)
            kvh = (2 * pr) // Q_PER_KV
            kh = kn[:, kvh * HEAD_DIM:(kvh + 1) * HEAD_DIM].astype(BF)
            v_ones = jnp.concatenate([v[:, kvh * HEAD_DIM:(kvh + 1) * HEAD_DIM],
                                      jnp.ones((2 * ATTN_BLOCK, HEAD_DIM), F32)], axis=1).astype(BF)
            for hh in range(2):
                hd = 2 * pr + hh
                qh = qn[:, hh * HEAD_DIM:(hh + 1) * HEAD_DIM].astype(BF)
                s = jnp.where(valid, _nt(qh, kh) * SCALE, NEG)
                sink = sk_ref[0:1, hd:hd + 1]
                m = jnp.maximum(jnp.max(s, axis=-1, keepdims=True), sink)
                ov = _nn(jnp.exp(s - m).astype(BF), v_ones)
                den = ov[:, HEAD_DIM:HEAD_DIM + 1] + jnp.exp(sink - m)
                ya_ref[:, hd * HEAD_DIM:(hd + 1) * HEAD_DIM] = (ov[:, :HEAD_DIM] / den).astype(BF)
                lse_ref[:, hd:hd + 1] = m + jnp.log(den)

    prev = lambda n: (jnp.maximum(n - 1, 0), 0)
    tab = pl.BlockSpec((ATTN_BLOCK, PAIR), lambda n: (n, 0))
    tab_p = pl.BlockSpec((ATTN_BLOCK, PAIR), prev)
    one = lambda w: pl.BlockSpec((1, w), lambda n: (0, 0))
    return pl.pallas_call(
        body, out_shape=[jax.ShapeDtypeStruct((t_len, Q_COLS), BF), jax.ShapeDtypeStruct((t_len, N_Q_HEADS), F32)],
        grid=(nb,),
        in_specs=[pl.BlockSpec((ATTN_BLOCK, Q_COLS), lambda n: (n, 0)),
                  pl.BlockSpec((ATTN_BLOCK, 2 * KV_COLS), lambda n: (n, 2)),
                  pl.BlockSpec((ATTN_BLOCK, 2 * KV_COLS), lambda n: (jnp.maximum(n - 1, 0), 2)),
                  tab, tab, tab_p, tab_p, one(PAIR), one(PAIR), one(N_Q_HEADS)],
        out_specs=[pl.BlockSpec((ATTN_BLOCK, Q_COLS), lambda n: (n, 0)),
                   pl.BlockSpec((ATTN_BLOCK, N_Q_HEADS), lambda n: (n, 0))],
        name=name, compiler_params=_params(1))(proj, proj, proj, cos_t, sin_t, cos_t, sin_t, gq2, gk2, sinks)


def _attn_bwd_q(proj, cos_t, sin_t, gq2, gk2, sinks, ya, dmix, lse, name):
    t_len = proj.shape[0]
    nb = t_len // ATTN_BLOCK

    def body(q_ref, kvc_ref, kvp_ref, cq_ref, sq_ref, cp_ref, sp_ref, gq_ref, gk_ref, sk_ref, ya_ref, do_ref, lse_ref,
             dq_ref, dgq_ref, dsk_ref, scr):
        n = pl.program_id(0)

        @pl.when(n == 0)
        def _():
            dgq_ref[...] = jnp.zeros_like(dgq_ref)
            dsk_ref[...] = jnp.zeros_like(dsk_ref)

        cq, sq = cq_ref[...], sq_ref[...]
        c_all = jnp.concatenate([cp_ref[...], cq], axis=0)
        s_all = jnp.concatenate([sp_ref[...], sq], axis=0)
        kv = jnp.concatenate([kvp_ref[...], kvc_ref[...]], axis=0)
        k_raw, v = kv[:, :PAIR], kv[:, PAIR:]
        n_pairs = N_Q_HEADS // 2
        pair_lanes = [slice(pr * PAIR, (pr + 1) * PAIR) for pr in range(n_pairs)]
        q_raws = [q_ref[:, lanes] for lanes in pair_lanes]
        rms = _pair_rms_many(q_raws + [k_raw])
        dsums = _pair_sums([do_ref[:, lanes] * ya_ref[:, lanes].astype(F32) for lanes in pair_lanes])
        kn = _rope(k_raw * rms[-1] * gk_ref[...], c_all, s_all)
        row = lax.broadcasted_iota(jnp.int32, (ATTN_BLOCK, 2 * ATTN_BLOCK), 0)
        col = lax.broadcasted_iota(jnp.int32, (ATTN_BLOCK, 2 * ATTN_BLOCK), 1)
        rel = row + ATTN_BLOCK - col
        valid = (rel >= 0) & (rel < ATTN_BLOCK) & ((col >= ATTN_BLOCK) | (n > 0))
        gq = gq_ref[...]
        for pr in range(n_pairs):
            qn = _rope(q_raws[pr] * rms[pr] * gq, cq, sq)
            kvh = (2 * pr) // Q_PER_KV
            kh = kn[:, kvh * HEAD_DIM:(kvh + 1) * HEAD_DIM].astype(BF)
            vh = v[:, kvh * HEAD_DIM:(kvh + 1) * HEAD_DIM].astype(BF)
            dsum2 = dsums[pr]
            for hh in range(2):
                hd = 2 * pr + hh
                qh = qn[:, hh * HEAD_DIM:(hh + 1) * HEAD_DIM].astype(BF)
                lse_h = lse_ref[:, hd:hd + 1]
                p = jnp.exp(jnp.where(valid, _nt(qh, kh) * SCALE, NEG) - lse_h)
                d_o = do_ref[:, hd * HEAD_DIM:(hd + 1) * HEAD_DIM]
                dsum = dsum2[:, hh * HEAD_DIM:hh * HEAD_DIM + 1]
                ds = p * (_nt(d_o.astype(BF), vh) - dsum)
                scr[pr, :, hh * HEAD_DIM:(hh + 1) * HEAD_DIM] = _nn(ds.astype(BF), kh) * SCALE
                sink = sk_ref[0:1, hd:hd + 1]
                dsk_ref[0:1, hd:hd + 1] += -jnp.sum(jnp.exp(sink - lse_h) * dsum, axis=0, keepdims=True)
        back = _pair_rms_bwd_many([_rope_t(scr[pr], cq, sq) for pr in range(n_pairs)], q_raws, rms[:n_pairs], gq)
        for lanes, (dx, dg) in zip(pair_lanes, back):
            dq_ref[:, lanes] = dx.astype(BF)
            dgq_ref[...] += dg

    prev = lambda n: (jnp.maximum(n - 1, 0), 0)
    tab = pl.BlockSpec((ATTN_BLOCK, PAIR), lambda n: (n, 0))
    tab_p = pl.BlockSpec((ATTN_BLOCK, PAIR), prev)
    one = lambda w: pl.BlockSpec((1, w), lambda n: (0, 0))
    qblk = pl.BlockSpec((ATTN_BLOCK, Q_COLS), lambda n: (n, 0))
    return pl.pallas_call(
        body,
        out_shape=[jax.ShapeDtypeStruct((t_len, Q_COLS), BF), jax.ShapeDtypeStruct((1, PAIR), F32),
                   jax.ShapeDtypeStruct((1, N_Q_HEADS), F32)],
        grid=(nb,),
        in_specs=[qblk, pl.BlockSpec((ATTN_BLOCK, 2 * KV_COLS), lambda n: (n, 2)),
                  pl.BlockSpec((ATTN_BLOCK, 2 * KV_COLS), lambda n: (jnp.maximum(n - 1, 0), 2)),
                  tab, tab, tab_p, tab_p, one(PAIR), one(PAIR), one(N_Q_HEADS), qblk, qblk,
                  pl.BlockSpec((ATTN_BLOCK, N_Q_HEADS), lambda n: (n, 0))],
        out_specs=[qblk, one(PAIR), one(N_Q_HEADS)], scratch_shapes=[pltpu.VMEM((N_Q_HEADS // 2, ATTN_BLOCK, PAIR), F32)],
        name=name, compiler_params=_params(1))(proj, proj, proj, cos_t, sin_t, cos_t, sin_t, gq2, gk2, sinks, ya, dmix, lse)


def _attn_bwd_kv(proj, cos_t, sin_t, gq2, gk2, ya, dmix, lse, name):
    t_len = proj.shape[0]
    nb = t_len // ATTN_BLOCK

    def body(kv_ref, q0_ref, q1_ref, ck_ref, sk_ref, c1_ref, s1_ref, gq_ref, gk_ref, o0_ref, o1_ref, do0_ref, do1_ref,
             l0_ref, l1_ref, dk_ref, dv_ref, dgk_ref, dkn_scr, dv_scr):
        m = pl.program_id(0)

        @pl.when(m == 0)
        def _():
            dgk_ref[...] = jnp.zeros_like(dgk_ref)

        ck, sk = ck_ref[...], sk_ref[...]
        c_all = jnp.concatenate([ck, c1_ref[...]], axis=0)
        s_all = jnp.concatenate([sk, s1_ref[...]], axis=0)
        kvv = kv_ref[...]
        k_raw, v = kvv[:, :PAIR], kvv[:, PAIR:]
        pair_lanes = [slice(pr * PAIR, (pr + 1) * PAIR) for pr in range(N_Q_HEADS // 2)]
        q_raws = [jnp.concatenate([q0_ref[:, lanes], q1_ref[:, lanes]], axis=0) for lanes in pair_lanes]
        d_os = [jnp.concatenate([do0_ref[:, lanes], do1_ref[:, lanes]], axis=0) for lanes in pair_lanes]
        rms = [_pair_rms(a) for a in q_raws + [k_raw]]
        dsums = [_pair_sum(d_o2 * jnp.concatenate([o0_ref[:, lanes], o1_ref[:, lanes]], axis=0).astype(F32))
                 for d_o2, lanes in zip(d_os, pair_lanes)]
        rk = rms[-1]
        gk = gk_ref[...]
        kn = _rope(k_raw * rk * gk, ck, sk)
        row = lax.broadcasted_iota(jnp.int32, (2 * ATTN_BLOCK, ATTN_BLOCK), 0)
        col = lax.broadcasted_iota(jnp.int32, (2 * ATTN_BLOCK, ATTN_BLOCK), 1)
        valid = ((row < ATTN_BLOCK) & (row >= col)) | ((row >= ATTN_BLOCK) & (row - ATTN_BLOCK < col) & (m < nb - 1))
        lse2 = jnp.concatenate([l0_ref[...], l1_ref[...]], axis=0)
        d_kn, d_v = [None, None], [None, None]
        plus = lambda acc, term: term if acc is None else acc + term
        for pr in range(N_Q_HEADS // 2):
            qn = _rope(q_raws[pr] * rms[pr] * gq_ref[...], c_all, s_all)
            d_o2, dsum2 = d_os[pr], dsums[pr]
            kvh = (2 * pr) // Q_PER_KV
            khs = slice(kvh * HEAD_DIM, (kvh + 1) * HEAD_DIM)
            kh = kn[:, khs].astype(BF)
            vh = v[:, khs].astype(BF)
            for hh in range(2):
                hd = 2 * pr + hh
                hs = slice(hh * HEAD_DIM, (hh + 1) * HEAD_DIM)
                qh = qn[:, hs].astype(BF)
                d_oh = d_o2[:, hs].astype(BF)
                p = jnp.exp(jnp.where(valid, _nt(qh, kh) * SCALE, NEG) - lse2[:, hd:hd + 1])
                d_v[kvh] = plus(d_v[kvh], _tn(p.astype(BF), d_oh))
                ds = p * (_nt(d_oh, vh) - dsum2[:, hh * HEAD_DIM:hh * HEAD_DIM + 1])
                d_kn[kvh] = plus(d_kn[kvh], _tn(ds.astype(BF), qh))
        for kvh in range(2):
            khs = slice(kvh * HEAD_DIM, (kvh + 1) * HEAD_DIM)
            dkn_scr[:, khs] = d_kn[kvh] * SCALE
            dv_scr[:, khs] = d_v[kvh]
        dx, dg = _pair_rms_bwd(_rope_t(dkn_scr[...], ck, sk), k_raw, rk, gk)
        dk_ref[...] = dx.astype(BF)
        dv_ref[...] = dv_scr[...].astype(BF)
        dgk_ref[...] += dg

    nxt = lambda m: (jnp.minimum(m + 1, nb - 1), 0)
    cur = lambda m: (m, 0)
    tab = lambda f: pl.BlockSpec((ATTN_BLOCK, PAIR), f)
    qb = lambda f: pl.BlockSpec((ATTN_BLOCK, Q_COLS), f)
    lb = lambda f: pl.BlockSpec((ATTN_BLOCK, N_Q_HEADS), f)
    one = pl.BlockSpec((1, PAIR), lambda m: (0, 0))
    return pl.pallas_call(
        body,
        out_shape=[jax.ShapeDtypeStruct((t_len, KV_COLS), BF), jax.ShapeDtypeStruct((t_len, KV_COLS), BF),
                   jax.ShapeDtypeStruct((1, PAIR), F32)],
        grid=(nb,),
        in_specs=[pl.BlockSpec((ATTN_BLOCK, 2 * KV_COLS), lambda m: (m, 2)), qb(cur), qb(nxt), tab(cur), tab(cur), tab(nxt),
                  tab(nxt), one, one, qb(cur), qb(nxt), qb(cur), qb(nxt), lb(cur), lb(nxt)],
        out_specs=[tab(cur), tab(cur), one],
        scratch_shapes=[pltpu.VMEM((ATTN_BLOCK, PAIR), F32), pltpu.VMEM((ATTN_BLOCK, PAIR), F32)],
        name=name, compiler_params=_params(1))(proj, proj, proj, cos_t, sin_t, cos_t, sin_t, gq2, gk2, ya, ya, dmix, dmix,
                                               lse, lse)


def _conv_taps(blk, w, offset_of):
    acc = jnp.zeros((CONV_CHUNK, CONV_CH), F32)
    for k in range(CONV_WIDTH):
        o = offset_of(k)
        acc = acc + w[k:k + 1, :] * blk[o:o + CONV_CHUNK, :]
    return acc


def _conv_fwd(proj, w_dw, b_dw, g_ln, b_ln, name):
    t_len = proj.shape[0]
    nc = t_len // CONV_CHUNK

    def body(a_ref, gt_ref, w_ref, b_ref, g_ref, bl_ref, yc_ref, cv_ref, pad):
        pad[0:CONV_PAD, :] = jnp.zeros((CONV_PAD, CONV_CH), F32)

        def glu(c, carry):
            rows = pl.ds(pl.multiple_of(c * CONV_CHUNK, CONV_CHUNK), CONV_CHUNK)
            dst = pl.ds(pl.multiple_of(c * CONV_CHUNK + CONV_PAD, CONV_PAD), CONV_CHUNK)
            pad[dst, :] = a_ref[rows, :] * jax.nn.sigmoid(gt_ref[rows, :])
            return carry

        lax.fori_loop(0, nc, glu, 0)
        w = w_ref[...]

        def conv(c, carry):
            base = pl.multiple_of(c * CONV_CHUNK, CONV_CHUNK)
            blk = pad[pl.ds(base, CONV_CHUNK + CONV_PAD), :]
            cv = _conv_taps(blk, w, lambda k: k + CONV_PAD - (CONV_WIDTH - 1)) + b_ref[...]
            mu = jnp.mean(cv, axis=-1, keepdims=True)
            xc = cv - mu
            z = xc * lax.rsqrt(jnp.mean(xc * xc, axis=-1, keepdims=True) + EPS) * g_ref[...] + bl_ref[...]
            rows = pl.ds(base, CONV_CHUNK)
            cv_ref[rows, :] = cv
            yc_ref[rows, :] = (z * jax.nn.sigmoid(z)).astype(BF)
            return carry

        lax.fori_loop(0, nc, conv, 0)

    vec = pl.BlockSpec((1, CONV_CH), lambda i: (0, 0))
    full = pl.BlockSpec((t_len, CONV_CH), lambda i: (0, 0))
    return pl.pallas_call(
        body, out_shape=[jax.ShapeDtypeStruct((t_len, CONV_CH), BF), jax.ShapeDtypeStruct((t_len, CONV_CH), F32)],
        grid=(1,),
        in_specs=[pl.BlockSpec((t_len, CONV_CH), lambda i: (0, 3)), pl.BlockSpec((t_len, CONV_CH), lambda i: (0, 4)),
                  pl.BlockSpec((CONV_WIDTH, CONV_CH), lambda i: (0, 0)), vec, vec, vec],
        out_specs=[full, full], scratch_shapes=[pltpu.VMEM((t_len + CONV_PAD, CONV_CH), F32)], name=name,
        compiler_params=_params(1))(proj, proj, w_dw, b_dw, g_ln, b_ln)


def _conv_bwd(proj, cv, dmix, w_dw, g_ln, b_ln, name):
    t_len = proj.shape[0]
    nc = t_len // CONV_CHUNK

    def body(a_ref, gt_ref, cv_ref, dy_ref, w_ref, g_ref, bl_ref, du_ref, dw_ref, db_ref, dgl_ref, dbl_ref, pad, dpad):
        pad[0:CONV_PAD, :] = jnp.zeros((CONV_PAD, CONV_CH), F32)
        dpad[t_len:t_len + CONV_PAD, :] = jnp.zeros((CONV_PAD, CONV_CH), F32)
        dw_ref[...] = jnp.zeros_like(dw_ref)
        db_ref[...] = jnp.zeros_like(db_ref)
        dgl_ref[...] = jnp.zeros_like(dgl_ref)
        dbl_ref[...] = jnp.zeros_like(dbl_ref)

        def norm_bwd(c, carry):
            base = pl.multiple_of(c * CONV_CHUNK, CONV_CHUNK)
            rows = pl.ds(base, CONV_CHUNK)
            dst = pl.ds(pl.multiple_of(c * CONV_CHUNK + CONV_PAD, CONV_PAD), CONV_CHUNK)
            pad[dst, :] = a_ref[rows, :] * jax.nn.sigmoid(gt_ref[rows, :])
            cvv = cv_ref[rows, :]
            xc = cvv - jnp.mean(cvv, axis=-1, keepdims=True)
            rs = lax.rsqrt(jnp.mean(xc * xc, axis=-1, keepdims=True) + EPS)
            xhat = xc * rs
            z = xhat * g_ref[...] + bl_ref[...]
            sg = jax.nn.sigmoid(z)
            dz = dy_ref[rows, :] * (sg * (1.0 + z * (1.0 - sg)))
            dgl_ref[...] += jnp.sum(dz * xhat, axis=0, keepdims=True)
            dbl_ref[...] += jnp.sum(dz, axis=0, keepdims=True)
            dxh = dz * g_ref[...]
            dcv = rs * (dxh - jnp.mean(dxh, axis=-1, keepdims=True) - xhat * jnp.mean(dxh * xhat, axis=-1, keepdims=True))
            db_ref[...] += jnp.sum(dcv, axis=0, keepdims=True)
            dpad[rows, :] = dcv
            return carry

        lax.fori_loop(0, nc, norm_bwd, 0)
        w = w_ref[...]

        def conv_bwd(c, carry):
            base = pl.multiple_of(c * CONV_CHUNK, CONV_CHUNK)
            rows = pl.ds(base, CONV_CHUNK)
            dblk = dpad[pl.ds(base, CONV_CHUNK + CONV_PAD), :]
            dhc = _conv_taps(dblk, w, lambda k: CONV_WIDTH - 1 - k)
            a = a_ref[rows, :]
            sg = jax.nn.sigmoid(gt_ref[rows, :])
            du_ref[rows, 0:CONV_CH] = (dhc * sg).astype(BF)
            du_ref[rows, CONV_CH:2 * CONV_CH] = (dhc * a * sg * (1.0 - sg)).astype(BF)
            hblk = pad[pl.ds(base, CONV_CHUNK + CONV_PAD), :]
            dcv = dblk[0:CONV_CHUNK, :]
            for k in range(CONV_WIDTH):
                o = k + CONV_PAD - (CONV_WIDTH - 1)
                dw_ref[k:k + 1, :] += jnp.sum(dcv * hblk[o:o + CONV_CHUNK, :], axis=0, keepdims=True)
            return carry

        lax.fori_loop(0, nc, conv_bwd, 0)

    vec = pl.BlockSpec((1, CONV_CH), lambda i: (0, 0))
    full = pl.BlockSpec((t_len, CONV_CH), lambda i: (0, 0))
    wspec = pl.BlockSpec((CONV_WIDTH, CONV_CH), lambda i: (0, 0))
    vshape = jax.ShapeDtypeStruct((1, CONV_CH), F32)
    return pl.pallas_call(
        body,
        out_shape=[jax.ShapeDtypeStruct((t_len, 2 * CONV_CH), BF), jax.ShapeDtypeStruct((CONV_WIDTH, CONV_CH), F32),
                   vshape, vshape, vshape],
        grid=(1,),
        in_specs=[pl.BlockSpec((t_len, CONV_CH), lambda i: (0, 3)), pl.BlockSpec((t_len, CONV_CH), lambda i: (0, 4)), full,
                  pl.BlockSpec((t_len, CONV_CH), lambda i: (0, 2)), wspec, vec, vec],
        out_specs=[pl.BlockSpec((t_len, 2 * CONV_CH), lambda i: (0, 0)), wspec, vec, vec, vec],
        scratch_shapes=[pltpu.VMEM((t_len + CONV_PAD, CONV_CH), F32), pltpu.VMEM((t_len + CONV_PAD, CONV_CH), F32)],
        name=name, compiler_params=_params(1))(proj, proj, cv, dmix, w_dw, g_ln, b_ln)


def _mem_kv_fwd(mem, g, w, name):
    def body(m_ref, g_ref, w_ref, h_ref, kv_ref):
        mv = m_ref[...]
        h = (mv * _row_rms(mv) * g_ref[...]).astype(BF)
        h_ref[...] = h
        kv_ref[...] = _nn(h, w_ref[...])

    m_len = mem.shape[0]
    return pl.pallas_call(
        body, out_shape=[jax.ShapeDtypeStruct((m_len, D_MODEL), BF), jax.ShapeDtypeStruct((m_len, 2 * MQ_COLS), F32)],
        name=name, compiler_params=pltpu.CompilerParams(vmem_limit_bytes=VMEM_LIMIT_BYTES))(mem, g, w)


def _mem_kv_bwd(mem, g, h, w, dkv, name):
    def body(m_ref, g_ref, h_ref, w_ref, dkv_ref, dw_ref, dg_ref):
        dkv_b = dkv_ref[...].astype(BF)
        dw_ref[...] = _tn(h_ref[...], dkv_b).astype(BF)
        mv = m_ref[...]
        dg_ref[...] = jnp.sum(_nt(dkv_b, w_ref[...]) * mv * _row_rms(mv), axis=0, keepdims=True)

    return pl.pallas_call(
        body, out_shape=[jax.ShapeDtypeStruct((D_MODEL, 2 * MQ_COLS), BF), jax.ShapeDtypeStruct((1, D_MODEL), F32)],
        name=name, compiler_params=pltpu.CompilerParams(vmem_limit_bytes=VMEM_LIMIT_BYTES))(mem, g, h, w, dkv)


def _mem_attn_fwd(proj, mkv, gq2, gk2, name):
    t_len = proj.shape[0]
    tm = min(256, t_len)

    def body(q_ref, kv_ref, gq_ref, gk_ref, ym_ref, lse_ref):
        kvv = kv_ref[...]
        for pr in range(N_MEM_HEADS // 2):
            lanes = slice(pr * PAIR, (pr + 1) * PAIR)
            k_raw = kvv[:, lanes]
            kn = k_raw * _pair_rms(k_raw) * gk_ref[...]
            v = kvv[:, MQ_COLS + pr * PAIR:MQ_COLS + (pr + 1) * PAIR]
            q_raw = q_ref[:, lanes]
            qn = q_raw * _pair_rms(q_raw) * gq_ref[...]
            for hh in range(2):
                hd = 2 * pr + hh
                hs = slice(hh * HEAD_DIM, (hh + 1) * HEAD_DIM)
                s = _nt(qn[:, hs].astype(BF), kn[:, hs].astype(BF)) * SCALE
                m = jnp.max(s, axis=-1, keepdims=True)
                p = jnp.exp(s - m)
                den = jnp.sum(p, axis=-1, keepdims=True)
                ym_ref[:, hd * HEAD_DIM:(hd + 1) * HEAD_DIM] = _nn((p / den).astype(BF), v[:, hs].astype(BF)).astype(BF)
                lse_ref[:, hd:hd + 1] = m + jnp.log(den)

    m_len = mkv.shape[0]
    one = pl.BlockSpec((1, PAIR), lambda t: (0, 0))
    return pl.pallas_call(
        body, out_shape=[jax.ShapeDtypeStruct((t_len, MQ_COLS), BF), jax.ShapeDtypeStruct((t_len, N_MEM_HEADS), F32)],
        grid=(t_len // tm,),
        in_specs=[pl.BlockSpec((tm, MQ_COLS), lambda t: (t, 5)), pl.BlockSpec((m_len, 2 * MQ_COLS), lambda t: (0, 0)), one, one],
        out_specs=[pl.BlockSpec((tm, MQ_COLS), lambda t: (t, 0)), pl.BlockSpec((tm, N_MEM_HEADS), lambda t: (t, 0))],
        name=name, compiler_params=_params(1))(proj, mkv, gq2, gk2)


def _mem_attn_bwd(proj, mkv, gq2, gk2, ym, dmix, lse, name):
    t_len = proj.shape[0]
    tm = min(256, t_len)
    nt = t_len // tm
    m_len = mkv.shape[0]

    def body(q_ref, kv_ref, gq_ref, gk_ref, ym_ref, do_ref, lse_ref, dq_ref, dkv_ref, dgq_ref, dgk_ref, dkn_scr, dv_scr, scr):
        t = pl.program_id(0)

        @pl.when(t == 0)
        def _():
            dkn_scr[...] = jnp.zeros_like(dkn_scr)
            dv_scr[...] = jnp.zeros_like(dv_scr)
            dgq_ref[...] = jnp.zeros_like(dgq_ref)

        kvv = kv_ref[...]
        for pr in range(N_MEM_HEADS // 2):
            lanes = slice(pr * PAIR, (pr + 1) * PAIR)
            k_raw = kvv[:, lanes]
            kn = k_raw * _pair_rms(k_raw) * gk_ref[...]
            v = kvv[:, MQ_COLS + pr * PAIR:MQ_COLS + (pr + 1) * PAIR]
            q_raw = q_ref[:, lanes]
            rq = _pair_rms(q_raw)
            qn = q_raw * rq * gq_ref[...]
            d_o = do_ref[:, lanes]
            dsum = _pair_sum(d_o * ym_ref[:, lanes].astype(F32))
            for hh in range(2):
                hd = 2 * pr + hh
                hs = slice(hh * HEAD_DIM, (hh + 1) * HEAD_DIM)
                cols = slice(hd * HEAD_DIM, (hd + 1) * HEAD_DIM)
                qh = qn[:, hs].astype(BF)
                kh = kn[:, hs].astype(BF)
                d_oh = d_o[:, hs].astype(BF)
                p = jnp.exp(_nt(qh, kh) * SCALE - lse_ref[:, hd:hd + 1])
                dv_scr[:, cols] += _tn(p.astype(BF), d_oh)
                ds = (p * (_nt(d_oh, v[:, hs].astype(BF)) - dsum[:, hh * HEAD_DIM:hh * HEAD_DIM + 1])).astype(BF)
                scr[pr, :, hs] = _nn(ds, kh) * SCALE
                dkn_scr[:, cols] += _tn(ds, qh) * SCALE
            dx, dg = _pair_rms_bwd(scr[pr], q_raw, rq, gq_ref[...])
            dq_ref[:, lanes] = dx.astype(BF)
            dgq_ref[...] += dg

        @pl.when(t == nt - 1)
        def _():
            dgk = jnp.zeros((1, PAIR), F32)
            for pr in range(N_MEM_HEADS // 2):
                lanes = slice(pr * PAIR, (pr + 1) * PAIR)
                k_raw = kvv[:, lanes]
                dx, dg = _pair_rms_bwd(dkn_scr[:, lanes], k_raw, _pair_rms(k_raw), gk_ref[...])
                dkv_ref[:, lanes] = dx
                dgk = dgk + dg
            dkv_ref[:, MQ_COLS:2 * MQ_COLS] = dv_scr[...]
            dgk_ref[...] = dgk

    one = pl.BlockSpec((1, PAIR), lambda t: (0, 0))
    kvspec = pl.BlockSpec((m_len, 2 * MQ_COLS), lambda t: (0, 0))
    qspec = pl.BlockSpec((tm, MQ_COLS), lambda t: (t, 0))
    return pl.pallas_call(
        body,
        out_shape=[jax.ShapeDtypeStruct((t_len, MQ_COLS), BF), jax.ShapeDtypeStruct((m_len, 2 * MQ_COLS), F32),
                   jax.ShapeDtypeStruct((1, PAIR), F32), jax.ShapeDtypeStruct((1, PAIR), F32)],
        grid=(nt,),
        in_specs=[pl.BlockSpec((tm, MQ_COLS), lambda t: (t, 5)), kvspec, one, one, qspec,
                  pl.BlockSpec((tm, MQ_COLS), lambda t: (t, 3)), pl.BlockSpec((tm, N_MEM_HEADS), lambda t: (t, 0))],
        out_specs=[qspec, kvspec, one, one],
        scratch_shapes=[pltpu.VMEM((m_len, MQ_COLS), F32), pltpu.VMEM((m_len, MQ_COLS), F32),
                        pltpu.VMEM((N_MEM_HEADS // 2, tm, PAIR), F32)],
        name=name, compiler_params=_params(1))(proj, mkv, gq2, gk2, ym, dmix, lse)


MIX_GROUPS = [(0, Q_COLS), (Q_COLS, CONV_CH), (Q_COLS + CONV_CH, MQ_COLS)]


def _out_proj_fwd(x, ya, yc, ym, w_out, name, after=None):
    t_len = x.shape[0]
    tm = min(512, t_len)

    def body(x_ref, ya_ref, yc_ref, ym_ref, w_ref, xo_ref):
        y = x_ref[...]
        for (start, width), ref in zip(MIX_GROUPS, (ya_ref, yc_ref, ym_ref)):
            y = y + _nn(ref[...], w_ref[start:start + width, :])
        xo_ref[...] = y

    cols = lambda w: pl.BlockSpec((tm, w), lambda t: (t, 0))
    body, first, first_specs = _ordered_after(body, after)
    return pl.pallas_call(
        body, out_shape=jax.ShapeDtypeStruct((t_len, D_MODEL), F32), grid=(t_len // tm,),
        in_specs=first_specs + [cols(D_MODEL), cols(Q_COLS), cols(CONV_CH), cols(MQ_COLS),
                                pl.BlockSpec((D_MODEL, D_MODEL), lambda t: (0, 0))],
        out_specs=cols(D_MODEL), name=name, compiler_params=_params(1))(*first, x, ya, yc, ym, w_out)


def _out_proj_bwd(dx, ya, yc, ym, w_out, name, after=None):
    t_len = dx.shape[0]
    tm = min(512, t_len)
    nt = t_len // tm

    def body(dx_ref, ya_ref, yc_ref, ym_ref, w_ref, dmix_ref, dw_ref, acc):
        t = pl.program_id(0)

        @pl.when(t == 0)
        def _():
            acc[...] = jnp.zeros_like(acc)

        dxb = dx_ref[...].astype(BF)
        dmix_ref[...] = _nt(dxb, w_ref[...])
        for (start, width), ref in zip(MIX_GROUPS, (ya_ref, yc_ref, ym_ref)):
            acc[start:start + width, :] += _tn(ref[...], dxb)

        @pl.when(t == nt - 1)
        def _():
            dw_ref[...] = acc[...].astype(BF)

    cols = lambda w: pl.BlockSpec((tm, w), lambda t: (t, 0))
    full = pl.BlockSpec((D_MODEL, D_MODEL), lambda t: (0, 0))
    body, first, first_specs = _ordered_after(body, after)
    return pl.pallas_call(
        body, out_shape=[jax.ShapeDtypeStruct((t_len, D_MODEL), F32), jax.ShapeDtypeStruct((D_MODEL, D_MODEL), BF)],
        grid=(nt,), in_specs=first_specs + [cols(D_MODEL), cols(Q_COLS), cols(CONV_CH), cols(MQ_COLS), full],
        out_specs=[cols(D_MODEL), full], scratch_shapes=[pltpu.VMEM((D_MODEL, D_MODEL), F32)], name=name,
        compiler_params=_params(1))(*first, dx, ya, yc, ym, w_out)


N_PEERS = N_DEV - 1
HBM_SPEC = pl.BlockSpec(memory_space=pltpu.HBM)
SEM_SPEC = pl.BlockSpec(memory_space=pltpu.SEMAPHORE)
EFFECT = pltpu.SideEffectType.DATAFLOW_SIDE_EFFECTING


def _my_place():
    x, y, c = lax.axis_index("x"), lax.axis_index("y"), lax.axis_index("c")
    return x, y, c, 4 * x + 2 * y + c


def _peer(k):
    x, y, c, _ = _my_place()
    px = 1 - x if k & 4 else x
    py = 1 - y if k & 2 else y
    pc = 1 - c if k & 1 else c
    return (px, py, pc), 4 * px + 2 * py + pc


def _gather_two_level(shards, name):
    n = len(shards)

    def body(*refs):
        ins, outs = refs[:n], refs[n:2 * n]
        send_sems, recv_sems, local_sems = refs[2 * n:]
        x, y, c, me = _my_place()
        sibling = (x, y, 1 - c)
        chips = [(1 - x, y), (x, 1 - y), (1 - x, 1 - y)]
        slot = lambda px, py, pc: 4 * px + 2 * py + pc

        def copy(w, k, src, dst_slot, to):
            return pltpu.make_async_remote_copy(
                src_ref=src, dst_ref=outs[w].at[dst_slot], send_sem=send_sems.at[w * N_PEERS + k],
                recv_sem=recv_sems.at[w * N_PEERS + k], device_id=to, device_id_type=pl.DeviceIdType.MESH)

        local = [pltpu.make_async_copy(ins[w], outs[w].at[me], local_sems.at[w]) for w in range(n)]
        for cp in local:
            cp.start()
        sends = []
        for w in range(n):
            for j, chip in enumerate(chips):
                sends.append(copy(w, 1 + j, ins[w], me, (*chip, c)))
            sends.append(copy(w, 0, ins[w], me, sibling))
        for cp in sends:
            cp.start()
        for w in range(n):
            for j, chip in enumerate(chips):
                got = slot(*chip, c)
                copy(w, 1 + j, ins[w], got, (*chip, c)).wait_recv()
                fwd = copy(w, 4 + j, outs[w].at[got], got, sibling)
                fwd.start()
                sends.append(fwd)
        for w in range(n):
            copy(w, 0, ins[w], slot(x, y, 1 - c), sibling).wait_recv()
            for j, chip in enumerate(chips):
                copy(w, 4 + j, ins[w], slot(*chip, 1 - c), sibling).wait_recv()
        for cp in sends:
            cp.wait_send()
        for cp in local:
            cp.wait()

    hbm = pl.BlockSpec(memory_space=pl.ANY)
    return pl.pallas_call(
        body, out_shape=[jax.ShapeDtypeStruct((N_DEV,) + a.shape, a.dtype) for a in shards], in_specs=[hbm] * n,
        out_specs=[hbm] * n,
        scratch_shapes=[pltpu.SemaphoreType.DMA((n * N_PEERS,)), pltpu.SemaphoreType.DMA((n * N_PEERS,)),
                        pltpu.SemaphoreType.DMA((n,))],
        name=name)(*shards)


def _split_copies(srcs, lands, gather, send_sems, recv_sems):
    _, _, _, me = _my_place()
    pairs = []
    for w in range(len(srcs)):
        for k in range(1, N_DEV):
            dev, idx = _peer(k)
            src = srcs[w] if gather[w] else srcs[w].at[idx]
            sems = dict(send_sem=send_sems.at[w * N_PEERS + k - 1], recv_sem=recv_sems.at[w * N_PEERS + k - 1], device_id=dev,
                        device_id_type=pl.DeviceIdType.MESH)
            pairs.append((pltpu.make_async_remote_copy(src_ref=src, dst_ref=lands[w].at[me], **sems),
                          pltpu.make_async_remote_copy(src_ref=src, dst_ref=lands[w].at[idx], **sems)))
    return pairs


def _exchange_start(srcs, lands, gather, after, name):
    n = len(srcs)

    def body(*refs):
        src_refs, land_refs = refs[:n], refs[n:2 * n]
        send_sems, recv_sems = refs[2 * n + 1], refs[2 * n + 2]
        token = refs[-1]
        for out_going, _ in _split_copies(src_refs, land_refs, gather, send_sems, recv_sems):
            out_going.start()
        token[...] = jnp.zeros_like(token)

    arrays = list(srcs) + list(lands)
    out = pl.pallas_call(
        body, name=name,
        out_shape=(pltpu.SemaphoreType.DMA((n * N_PEERS,)), pltpu.SemaphoreType.DMA((n * N_PEERS,)),
                   *[pltpu.HBM(a.shape, a.dtype) for a in arrays], jax.ShapeDtypeStruct((8, 128), F32)),
        in_specs=[HBM_SPEC] * (2 * n) + [pl.BlockSpec(memory_space=pl.ANY)],
        out_specs=(SEM_SPEC, SEM_SPEC, *[HBM_SPEC] * (2 * n), pl.BlockSpec(memory_space=pltpu.VMEM)),
        input_output_aliases={i: i + 2 for i in range(2 * n)},
        compiler_params=pltpu.CompilerParams(has_side_effects=EFFECT),
    )(*[pltpu.with_memory_space_constraint(a, pltpu.HBM) for a in arrays], after)
    return out[0], out[1], out[2:2 + n], out[2 + n:2 + 2 * n], out[-1]


def _exchange_wait(send_sems, recv_sems, srcs, lands, gather, after, name):
    n = len(srcs)

    def body(*refs):
        src_refs, land_refs = refs[:n], refs[n:2 * n]
        for out_going, arriving in _split_copies(src_refs, land_refs, gather, refs[2 * n], refs[2 * n + 1]):
            out_going.wait_send()
            arriving.wait_recv()

    arrays = list(srcs) + list(lands)
    out = pl.pallas_call(
        body, name=name, out_shape=tuple(pltpu.HBM(a.shape, a.dtype) for a in arrays),
        in_specs=[HBM_SPEC] * (2 * n) + [SEM_SPEC, SEM_SPEC, pl.BlockSpec(memory_space=pl.ANY)],
        out_specs=tuple([HBM_SPEC] * (2 * n)), input_output_aliases={i: i for i in range(2 * n)},
        compiler_params=pltpu.CompilerParams(has_side_effects=EFFECT),
    )(*arrays, send_sems, recv_sems, after)
    return out[n:]


SIBLING = 1
CHIP_PEERS = (2, 4, 6)
NEAR_PEERS = (SIBLING,) + CHIP_PEERS


def _near_copies(srcs, lands, send_sems, recv_sems):
    _, _, _, me = _my_place()
    pairs = []
    for w in range(len(srcs)):
        for i, k in enumerate(NEAR_PEERS):
            dev, idx = _peer(k)
            sems = dict(send_sem=send_sems.at[w * len(NEAR_PEERS) + i], recv_sem=recv_sems.at[w * len(NEAR_PEERS) + i],
                        device_id=dev, device_id_type=pl.DeviceIdType.MESH)
            pairs.append((pltpu.make_async_remote_copy(src_ref=srcs[w], dst_ref=lands[w].at[me], **sems),
                          pltpu.make_async_remote_copy(src_ref=srcs[w], dst_ref=lands[w].at[idx], **sems)))
    return pairs


def _forward_copies(lands, send_sems, recv_sems):
    sibling, _ = _peer(SIBLING)
    pairs = []
    for w in range(len(lands)):
        for i, k in enumerate(CHIP_PEERS):
            _, got = _peer(k)
            _, gets = _peer(k | SIBLING)
            sems = dict(send_sem=send_sems.at[w * len(CHIP_PEERS) + i], recv_sem=recv_sems.at[w * len(CHIP_PEERS) + i],
                        device_id=sibling, device_id_type=pl.DeviceIdType.MESH)
            pairs.append((pltpu.make_async_remote_copy(src_ref=lands[w].at[got], dst_ref=lands[w].at[got], **sems),
                          pltpu.make_async_remote_copy(src_ref=lands[w].at[got], dst_ref=lands[w].at[gets], **sems)))
    return pairs


def _split_call(body, name, arrays, sems_in, sems_out, after, token):
    n = len(arrays)
    out = pl.pallas_call(
        body, name=name,
        out_shape=(*[pltpu.SemaphoreType.DMA((c,)) for c in sems_out], *[pltpu.HBM(a.shape, a.dtype) for a in arrays],
                   *([jax.ShapeDtypeStruct((8, 128), F32)] if token else [])),
        in_specs=[HBM_SPEC] * n + [SEM_SPEC] * len(sems_in) + [pl.BlockSpec(memory_space=pl.ANY)],
        out_specs=(*[SEM_SPEC] * len(sems_out), *[HBM_SPEC] * n,
                   *([pl.BlockSpec(memory_space=pltpu.VMEM)] if token else [])),
        input_output_aliases={i: i + len(sems_out) for i in range(n)},
        compiler_params=pltpu.CompilerParams(has_side_effects=EFFECT),
    )(*[pltpu.with_memory_space_constraint(a, pltpu.HBM) for a in arrays], *sems_in, after)
    k = len(sems_out)
    return list(out[:k]), list(out[k:k + n]), (out[-1] if token else None)


def _gather_start(srcs, lands, after, name):
    n = len(srcs)

    def body(*refs):
        send1, recv1 = refs[2 * n + 1], refs[2 * n + 2]
        for out_going, _ in _near_copies(refs[:n], refs[n:2 * n], send1, recv1):
            out_going.start()
        refs[-1][...] = jnp.zeros_like(refs[-1])

    sems, arrays, token = _split_call(body, name, list(srcs) + list(lands), [], [n * len(NEAR_PEERS)] * 2, after, True)
    return sems, arrays, token


def _gather_relay(sems, arrays, after, name):
    n = len(arrays) // 2

    def body(*refs):
        send1, recv1 = refs[2 * n], refs[2 * n + 1]
        send2, recv2 = refs[2 * n + 3], refs[2 * n + 4]
        near = _near_copies(refs[:n], refs[n:2 * n], send1, recv1)
        forward = _forward_copies(refs[n:2 * n], send2, recv2)
        for w in range(n):
            for i in range(len(CHIP_PEERS)):
                near[w * len(NEAR_PEERS) + 1 + i][1].wait_recv()
                forward[w * len(CHIP_PEERS) + i][0].start()
        refs[-1][...] = jnp.zeros_like(refs[-1])

    sems2, arrays, token = _split_call(body, name, arrays, sems, [n * len(CHIP_PEERS)] * 2, after, True)
    return sems + sems2, arrays, token


def _gather_finish(sems, arrays, after, name):
    n = len(arrays) // 2

    def body(*refs):
        send1, recv1, send2, recv2 = refs[2 * n:2 * n + 4]
        near = _near_copies(refs[:n], refs[n:2 * n], send1, recv1)
        for out_going, _ in near:
            out_going.wait_send()
        for w in range(n):
            near[w * len(NEAR_PEERS)][1].wait_recv()
        for out_going, arriving in _forward_copies(refs[n:2 * n], send2, recv2):
            out_going.wait_send()
            arriving.wait_recv()

    _, arrays, _ = _split_call(body, name, arrays, sems, [], after, False)
    return arrays[n:]


def _own_slot(a, me, gather):
    mine = a if gather else lax.dynamic_index_in_dim(a, me, 0, keepdims=False)
    return lax.dynamic_update_index_in_dim(lax.empty((N_DEV,) + mine.shape, mine.dtype), mine, me, 0)


def _adamw_math(w, g, m, v):
    m2 = ADAM_B1 * m + (1.0 - ADAM_B1) * g
    v2 = ADAM_B2 * v + (1.0 - ADAM_B2) * (g * g)
    m_hat = m2 / (1.0 - ADAM_B1 ** ADAM_STEP)
    v_hat = v2 / (1.0 - ADAM_B2 ** ADAM_STEP)
    return -ADAM_LR * (m_hat / (jnp.sqrt(v_hat) + ADAM_EPS) + ADAM_WD * w), m2, v2


def _sum_adamw(parts, w, m, v, name):
    rows, cols = w.shape
    tr = rows if rows <= 512 else 256

    def body(p_ref, w_ref, m_ref, v_ref, g_ref, d_ref, m2_ref, v2_ref):
        g = p_ref[0].astype(F32)
        for s in range(1, N_DEV):
            g = g + p_ref[s].astype(F32)
        g_ref[...] = g
        d_ref[...], m2_ref[...], v2_ref[...] = _adamw_math(w_ref[...], g, m_ref[...], v_ref[...])

    blk = pl.BlockSpec((tr, cols), lambda i: (i, 0))
    shape = jax.ShapeDtypeStruct((rows, cols), F32)
    return pl.pallas_call(
        body, out_shape=[shape] * 4, grid=(rows // tr,),
        in_specs=[pl.BlockSpec((N_DEV, tr, cols), lambda i: (0, i, 0)), blk, blk, blk], out_specs=[blk] * 4, name=name,
        compiler_params=_params(1))(parts, w, m, v)


def _slot_sum(parts, name):
    def body(p_ref, g_ref):
        g = p_ref[0]
        for s in range(1, N_DEV):
            g = g + p_ref[s]
        g_ref[...] = g

    return pl.pallas_call(body, out_shape=jax.ShapeDtypeStruct(parts.shape[1:], F32), name=name)(parts)


def _adamw(g, w, m, v, name):
    def body(g_ref, w_ref, m_ref, v_ref, d_ref, m2_ref, v2_ref):
        d_ref[...], m2_ref[...], v2_ref[...] = _adamw_math(w_ref[...], g_ref[...], m_ref[...], v_ref[...])

    shape = jax.ShapeDtypeStruct(w.shape, F32)
    return pl.pallas_call(body, out_shape=[shape] * 3, name=name)(g, w, m, v)


def _pack_small(vals, last=None):
    flat = jnp.concatenate([vals[k].reshape(-1) for k in SMALL])
    tail = jnp.zeros((SMALL_ROWS * D_MODEL - flat.shape[0] - 1,), F32)
    last = jnp.zeros((1,), F32) if last is None else last.reshape(1)
    return jnp.concatenate([flat, tail, last]).reshape(SMALL_ROWS, D_MODEL)


def _unpack_small(packed, like):
    flat, out, at = packed.reshape(-1), {}, 0
    for k in SMALL:
        size = like[k].size
        out[k] = flat[at:at + size].reshape(like[k].shape)
        at += size
    return out


FFN1 = ["w_ffn1_gate", "w_ffn1_up", "w_ffn1_down"]
FFN2 = ["w_ffn2_gate", "w_ffn2_up", "w_ffn2_down"]
MIXER = ["w_in", "w_mem_kv", "w_out"]
TRANSPOSED = ["w_ffn1_gate", "w_ffn1_up", "w_ffn2_gate", "w_ffn2_up", "w_in"]


def _mixer_fwd(x1, mem, tables, sm, big, after_attn=None):
    cos_t, sin_t, gq2, gk2, gmq2, gmk2 = tables
    h2, proj = _in_proj_fwd(x1, sm["g_mix"], big["w_in"], "in_proj_fwd")
    ya, lse = _attn_fwd(proj, cos_t, sin_t, gq2, gk2, sm["sinks"], "attn_fwd")
    token = None if after_attn is None else after_attn(ya)
    yc, cv = _conv_fwd(proj, big["w_dw"], sm["b_dw"], sm["g_conv_ln"], sm["b_conv_ln"], "conv_fwd")
    hm, mkv = _mem_kv_fwd(mem, sm["g_mem"], big["w_mem_kv"], "mem_kv_fwd")
    ym, lse_m = _mem_attn_fwd(proj, mkv, gmq2, gmk2, "mem_attn_fwd")
    x2 = _out_proj_fwd(x1, ya, yc, ym, big["w_out"], "out_proj_fwd", token)
    return x2, (h2, proj, ya, lse, yc, cv, hm, mkv, ym, lse_m)


def _mixer_bwd(dx2, x1, mem, tables, sm, big, saved, after):
    cos_t, sin_t, gq2, gk2, gmq2, gmk2 = tables
    h2, proj, ya, lse, yc, cv, hm, mkv, ym, lse_m = saved
    g = {}
    dmix, g["w_out"] = _out_proj_bwd(dx2, ya, yc, ym, big["w_out"], "out_proj_bwd", after)
    dq, dgq, g["sinks"] = _attn_bwd_q(proj, cos_t, sin_t, gq2, gk2, sm["sinks"], ya, dmix, lse, "attn_bwd_q")
    dk, dv, dgk = _attn_bwd_kv(proj, cos_t, sin_t, gq2, gk2, ya, dmix, lse, "attn_bwd_kv")
    du, g["w_dw"], g["b_dw"], g["g_conv_ln"], g["b_conv_ln"] = _conv_bwd(
        proj, cv, dmix, big["w_dw"], sm["g_conv_ln"], sm["b_conv_ln"], "conv_bwd")
    dmq, dmkv, dgmq, dgmk = _mem_attn_bwd(proj, mkv, gmq2, gmk2, ym, dmix, lse_m, "mem_attn_bwd")
    g["w_mem_kv"], g["g_mem"] = _mem_kv_bwd(mem, sm["g_mem"], hm, big["w_mem_kv"], dmkv, "mem_kv_bwd")
    dx1, g["g_mix"], g["w_in"] = _in_proj_bwd(dq, dk, dv, du, dmq, big["w_in"], h2, x1, sm["g_mix"], dx2, "in_proj_bwd")
    fold = lambda a: a[:, :HEAD_DIM] + a[:, HEAD_DIM:]
    g["g_q"], g["g_k"], g["g_mq"], g["g_mk"] = fold(dgq), fold(dgk), fold(dgmq), fold(dgmk)
    return dx1, g


def _tables(positions, sm):
    pair = lambda a: jnp.tile(a, (1, 2))
    return _rope_tables(positions) + (pair(sm["g_q"]), pair(sm["g_k"]), pair(sm["g_mq"]), pair(sm["g_mk"]))


def kernel(x, mem, positions, g_ffn1, w_ffn1_gate, w_ffn1_up, w_ffn1_down, g_mix, w_in, g_q, g_k, sinks, w_dw, b_dw, g_conv_ln, b_conv_ln, g_mem, w_mem_kv, g_mq, g_mk, w_out, g_ffn2, w_ffn2_gate, w_ffn2_up, w_ffn2_down, loss_target, m_g_ffn1, m_w_ffn1_gate, m_w_ffn1_up, m_w_ffn1_down, m_g_mix, m_w_in, m_g_q, m_g_k, m_sinks, m_w_dw, m_b_dw, m_g_conv_ln, m_b_conv_ln, m_g_mem, m_w_mem_kv, m_g_mq, m_g_mk, m_w_out, m_g_ffn2, m_w_ffn2_gate, m_w_ffn2_up, m_w_ffn2_down, v_g_ffn1, v_w_ffn1_gate, v_w_ffn1_up, v_w_ffn1_down, v_g_mix, v_w_in, v_g_q, v_g_k, v_sinks, v_w_dw, v_b_dw, v_g_conv_ln, v_b_conv_ln, v_g_mem, v_w_mem_kv, v_g_mq, v_g_mk, v_w_out, v_g_ffn2, v_w_ffn2_gate, v_w_ffn2_up, v_w_ffn2_down):
    w = dict(g_ffn1=g_ffn1, w_ffn1_gate=w_ffn1_gate, w_ffn1_up=w_ffn1_up, w_ffn1_down=w_ffn1_down, g_mix=g_mix, w_in=w_in,
             g_q=g_q, g_k=g_k, sinks=sinks, w_dw=w_dw, b_dw=b_dw, g_conv_ln=g_conv_ln, b_conv_ln=b_conv_ln, g_mem=g_mem,
             w_mem_kv=w_mem_kv, g_mq=g_mq, g_mk=g_mk, w_out=w_out, g_ffn2=g_ffn2, w_ffn2_gate=w_ffn2_gate,
             w_ffn2_up=w_ffn2_up, w_ffn2_down=w_ffn2_down)
    mo = dict(g_ffn1=m_g_ffn1, w_ffn1_gate=m_w_ffn1_gate, w_ffn1_up=m_w_ffn1_up, w_ffn1_down=m_w_ffn1_down, g_mix=m_g_mix,
              w_in=m_w_in, g_q=m_g_q, g_k=m_g_k, sinks=m_sinks, w_dw=m_w_dw, b_dw=m_b_dw, g_conv_ln=m_g_conv_ln,
              b_conv_ln=m_b_conv_ln, g_mem=m_g_mem, w_mem_kv=m_w_mem_kv, g_mq=m_g_mq, g_mk=m_g_mk, w_out=m_w_out,
              g_ffn2=m_g_ffn2, w_ffn2_gate=m_w_ffn2_gate, w_ffn2_up=m_w_ffn2_up, w_ffn2_down=m_w_ffn2_down)
    vo = dict(g_ffn1=v_g_ffn1, w_ffn1_gate=v_w_ffn1_gate, w_ffn1_up=v_w_ffn1_up, w_ffn1_down=v_w_ffn1_down, g_mix=v_g_mix,
              w_in=v_w_in, g_q=v_g_q, g_k=v_g_k, sinks=v_sinks, w_dw=v_w_dw, b_dw=v_b_dw, g_conv_ln=v_g_conv_ln,
              b_conv_ln=v_b_conv_ln, g_mem=v_g_mem, w_mem_kv=v_w_mem_kv, g_mq=v_g_mq, g_mk=v_g_mk, w_out=v_w_out,
              g_ffn2=v_g_ffn2, w_ffn2_gate=v_w_ffn2_gate, w_ffn2_up=v_w_ffn2_up, w_ffn2_down=v_w_ffn2_down)
    me = _my_place()[3]
    sm = {k: w[k] for k in SMALL}
    flip = lambda k, a: a.T if k in TRANSPOSED else a
    as_bf16 = lambda names: [flip(k, w[k][0]).astype(BF) for k in names]
    zones = lambda arrays, gather: [_own_slot(a, me, gather) for a in arrays]
    out_g, out_d, out_m, out_v = {}, {}, {}, {}

    def update(names, parts_list):
        for k, parts in zip(names, parts_list):
            new = _sum_adamw(parts, flip(k, w[k][0]), flip(k, mo[k][0]), flip(k, vo[k][0]), "adamw_" + k)
            out_g[k], out_d[k], out_m[k], out_v[k] = [flip(k, a)[None] for a in new]

    w1 = _gather_two_level(as_bf16(FFN1), "gather_ffn1")
    mix_src = as_bf16(MIXER) + [w["w_dw"][0]]
    mix_sems, mix_arrays, mix_token = _gather_start(mix_src, zones(mix_src, True), w1[0], "gather_mixer_start")
    f2_src = as_bf16(FFN2)
    f2_sems, f2_arrays, f2_token = _gather_start(f2_src, zones(f2_src, True), mix_token, "gather_ffn2_start")

    tables = _tables(positions[0], sm)
    h1, gate1, up1, x1 = _ffn_fwd(x[0], sm["g_ffn1"], *w1, None, "ffn1_fwd", after=f2_token)
    mix_sems, mix_arrays, mix_token = _gather_relay(mix_sems, mix_arrays, x1, "gather_mixer_relay")
    got = _gather_finish(mix_sems, mix_arrays, mix_token, "gather_mixer_finish")
    big = dict(w_in=got[0].reshape(IN_COLS, D_MODEL), w_mem_kv=got[1].reshape(D_MODEL, 2 * MQ_COLS),
               w_out=got[2].reshape(D_MODEL, D_MODEL), w_dw=got[3].transpose(1, 0, 2).reshape(CONV_WIDTH, CONV_CH))
    relayed = []

    def relay_ffn2(ya):
        relayed.extend(_gather_relay(f2_sems, f2_arrays, ya, "gather_ffn2_relay"))
        return relayed[2]

    x2, saved = _mixer_fwd(x1, mem[0], tables, sm, big, relay_ffn2)
    w2 = _gather_finish(relayed[0], relayed[1], x2, "gather_ffn2_finish")
    h3, gate2, up2, dy, loss_part = _ffn_fwd(x2, sm["g_ffn2"], *w2, loss_target[0], "ffn2_fwd")

    grads = {}
    dyb2, act2, dgate2, dup2, dx2, grads["g_ffn2"] = _ffn_bwd_act(dy, x2, sm["g_ffn2"], gate2, up2, *w2, "ffn2_bwd_act")
    g_f2 = list(_ffn_bwd_w(h3, dyb2, act2, dgate2, dup2, "ffn2_bwd_w"))
    r_f2 = _exchange_start(g_f2, zones(g_f2, False), [False] * 3, dx2, "scatter_ffn2_start")
    dx1, g_mid = _mixer_bwd(dx2, x1, mem[0], tables, sm, big, saved, r_f2[4])
    grads.update(g_mid)
    g_mix = [g_mid["w_in"].reshape(N_DEV, IN_COLS // N_DEV, D_MODEL),
             g_mid["w_mem_kv"].reshape(N_DEV, D_MODEL // N_DEV, 2 * MQ_COLS),
             g_mid["w_out"].reshape(N_DEV, D_MODEL // N_DEV, D_MODEL)]
    r_mix = _exchange_start(g_mix, zones(g_mix, False), [False] * 3, dx1, "scatter_mixer_start")
    dyb1, act1, dgate1, dup1, grad_x, grads["g_ffn1"] = _ffn_bwd_act(dx1, x[0], sm["g_ffn1"], gate1, up1, *w1,
                                                                     "ffn1_bwd_act", after=r_mix[4])
    g_f1 = list(_ffn_bwd_w(h1, dyb1, act1, dgate1, dup1, "ffn1_bwd_w"))
    dw_flat = grads["w_dw"].reshape(-1)
    packed = jnp.concatenate([_pack_small(grads, loss_part[0, 0]),
                              jnp.pad(dw_flat, (0, SMALL_ROWS * D_MODEL - dw_flat.shape[0])).reshape(SMALL_ROWS, D_MODEL)])
    last_src, last_kind = g_f1 + [packed], [False] * 3 + [True]
    r_f1 = _exchange_start(last_src, zones(g_f1, False) + zones([packed], True), last_kind, grad_x, "scatter_ffn1_start")

    all_done = lambda names: sum(out_d[k][:, :1, :1] for k in names)
    update(FFN2, _exchange_wait(*r_f2[:4], [False] * 3, r_f1[4], "scatter_ffn2_wait"))
    update(MIXER, _exchange_wait(*r_mix[:4], [False] * 3, all_done(FFN2), "scatter_mixer_wait"))
    last = _exchange_wait(*r_f1[:4], last_kind, all_done(MIXER), "scatter_ffn1_wait")
    update(FFN1, last[:3])
    small_sum = _slot_sum(last[3], "small_grad_sum")
    loss = small_sum[SMALL_ROWS - 1, D_MODEL - 1]
    g_small = small_sum[:SMALL_ROWS]
    d, m2, v2 = _adamw(g_small, _pack_small(w), _pack_small(mo), _pack_small(vo), "adamw_small")
    for dst, val in ((out_g, g_small), (out_d, d), (out_m, m2), (out_v, v2)):
        dst.update(_unpack_small(val, sm))
    g_dw = small_sum[SMALL_ROWS:].reshape(-1)[:CONV_WIDTH * CONV_CH].reshape(CONV_WIDTH, CONV_CH)
    g_dw = lax.dynamic_slice_in_dim(g_dw, me * (CONV_CH // N_DEV), CONV_CH // N_DEV, axis=1)
    d, m2, v2 = _adamw(g_dw, w["w_dw"][0], mo["w_dw"][0], vo["w_dw"][0], "adamw_w_dw")
    out_g["w_dw"], out_d["w_dw"], out_m["w_dw"], out_v["w_dw"] = g_dw[None], d[None], m2[None], v2[None]

    return (loss, grad_x[None], *[out_g[k] for k in WEIGHTS], *[out_d[k] for k in WEIGHTS], *[out_m[k] for k in WEIGHTS],
            *[out_v[k] for k in WEIGHTS])
```

```python
import jax
import jax.numpy as jnp
from jax import lax
from jax.experimental import pallas as pl
from jax.experimental.pallas import tpu as pltpu

D_MODEL = 1024
N_DEV = 8
FF_BLOCK = 352
D_FF = N_DEV * FF_BLOCK
FF_TILE = 256
N_FF_TILES = D_FF // FF_TILE
FF_ROWS = 1024
FF_CHUNK = 256
HEAD_DIM = 64
PAIR = 2 * HEAD_DIM
N_Q_HEADS = 8
Q_PER_KV = 4
ATTN_BLOCK = 128
Q_COLS = 512
KV_COLS = 128
CONV_CH = 256
MQ_COLS = 256
IN_COLS = 1536
CONV_WIDTH = 31
CONV_PAD = 32
CONV_CHUNK = 256
N_MEM_HEADS = 4
ROPE_THETA = 500000.0
ROPE_HALF = 8
EPS = 1e-6
SCALE = HEAD_DIM ** -0.5
NEG = -1e30
ADAM_LR, ADAM_B1, ADAM_B2, ADAM_EPS, ADAM_WD, ADAM_STEP = 0.001, 0.9, 0.999, 1e-08, 0.01, 10
VMEM_LIMIT_BYTES = 56 * 1024 * 1024
BF = jnp.bfloat16
F32 = jnp.float32

SMALL = ["g_ffn1", "g_mix", "g_mem", "g_ffn2", "g_q", "g_k", "g_mq", "g_mk", "sinks", "b_dw", "g_conv_ln", "b_conv_ln"]
WEIGHTS = ["g_ffn1", "w_ffn1_gate", "w_ffn1_up", "w_ffn1_down", "g_mix", "w_in", "g_q", "g_k", "sinks", "w_dw", "b_dw",
           "g_conv_ln", "b_conv_ln", "g_mem", "w_mem_kv", "g_mq", "g_mk", "w_out", "g_ffn2", "w_ffn2_gate", "w_ffn2_up",
           "w_ffn2_down"]
SMALL_ROWS = 8
PACK_ROWS = 16


def _nn(a, b):
    return jnp.dot(a, b, preferred_element_type=F32)


def _nt(a, b):
    return lax.dot_general(a, b, (((1,), (1,)), ((), ())), preferred_element_type=F32)


def _tn(a, b):
    return lax.dot_general(a, b, (((0,), (0,)), ((), ())), preferred_element_type=F32)


def _params(n_grid):
    return pltpu.CompilerParams(dimension_semantics=("arbitrary",) * n_grid, vmem_limit_bytes=VMEM_LIMIT_BYTES)


def _ordered_after(body, after):
    if after is None:
        return body, (), []

    def body_after(after_ref, *refs):
        body(*refs)

    return body_after, (after,), [pl.BlockSpec(memory_space=pl.ANY)]


def _row_rms(xv):
    return lax.rsqrt(jnp.mean(xv * xv, axis=-1, keepdims=True) + EPS)


def _rms_bwd(dh, xv, r, g):
    u = dh * g
    dx = r * u - xv * (r * r * r) * jnp.mean(u * xv, axis=-1, keepdims=True)
    return dx, jnp.sum(dh * xv * r, axis=0, keepdims=True)


def _sum_all(a):
    return jnp.sum(jnp.sum(a, axis=1, keepdims=True), axis=0, keepdims=True)


def _pair_sums(vs):
    row = lax.broadcasted_iota(jnp.int32, (PAIR, PAIR), 0) >= HEAD_DIM
    col = lax.broadcasted_iota(jnp.int32, (PAIR, PAIR), 1) >= HEAD_DIM
    same_head = (row == col).astype(BF)
    stacked = jnp.concatenate(vs, axis=0) if len(vs) > 1 else vs[0]
    hi = stacked.astype(BF)
    lo = (stacked - hi.astype(F32)).astype(BF)
    n = stacked.shape[0]
    both = _nn(jnp.concatenate([hi, lo], axis=0), same_head)
    total = both[:n] + both[n:]
    out, at = [], 0
    for v in vs:
        out.append(total[at:at + v.shape[0]])
        at += v.shape[0]
    return out


def _pair_sum(v):
    return _pair_sums([v])[0]


def _pair_rms_many(xs):
    return [lax.rsqrt(s * (1.0 / HEAD_DIM) + EPS) for s in _pair_sums([xv * xv for xv in xs])]


def _pair_rms(xv):
    return _pair_rms_many([xv])[0]


def _pair_rms_bwd_many(dxns, xs, rs, g):
    us = [dxn * g for dxn in dxns]
    sums = _pair_sums([u * xv for u, xv in zip(us, xs)])
    return [(r * u - xv * (r * r * r) * (s * (1.0 / HEAD_DIM)), jnp.sum(dxn * xv * r, axis=0, keepdims=True))
            for dxn, xv, r, u, s in zip(dxns, xs, rs, us, sums)]


def _pair_rms_bwd(dxn, xv, r, g):
    return _pair_rms_bwd_many([dxn], [xv], [r], g)[0]


def _rope_mask(shape):
    lane = lax.broadcasted_iota(jnp.int32, shape, 1)
    return ((lane & (HEAD_DIM - 1)) < 2 * ROPE_HALF).astype(F32)


def _partner(v):
    return pltpu.roll(v, ROPE_HALF, 1) + pltpu.roll(v, PAIR - ROPE_HALF, 1)


def _rope(xn, cos_t, sin_t):
    return xn * cos_t + _partner(xn * _rope_mask(xn.shape)) * sin_t


def _rope_t(d, cos_t, sin_t):
    return d * cos_t + _partner(d * sin_t) * _rope_mask(d.shape)


def _rope_tables(positions):
    inv_freq = ROPE_THETA ** (-jnp.arange(ROPE_HALF, dtype=F32) / ROPE_HALF)
    ang = positions.astype(F32)[:, None] * inv_freq
    cos, sin = jnp.cos(ang), jnp.sin(ang)
    t = positions.shape[0]
    cos_h = jnp.concatenate([cos, cos, jnp.ones((t, HEAD_DIM - 2 * ROPE_HALF), F32)], axis=1)
    sin_h = jnp.concatenate([-sin, sin, jnp.zeros((t, HEAD_DIM - 2 * ROPE_HALF), F32)], axis=1)
    return jnp.tile(cos_h, (1, 2)), jnp.tile(sin_h, (1, 2))


def _ff_rows(w):
    return w.reshape(D_FF, D_MODEL)


def _ffn_fwd(x, g, wg, wu, wd, target, name, after=None):
    t_len = x.shape[0]
    tm = min(FF_ROWS, t_len)
    with_loss = target is not None

    def body(*refs):
        if with_loss:
            x_ref, g_ref, wg_ref, wu_ref, wd_ref, t_ref, h_ref, gg_ref, uu_ref, dy_ref, loss_ref, acc = refs
        else:
            x_ref, g_ref, wg_ref, wu_ref, wd_ref, h_ref, gg_ref, uu_ref, xo_ref, acc = refs
        t, j = pl.program_id(0), pl.program_id(1)

        @pl.when(j == 0)
        def _():
            xv = x_ref[...]
            h_ref[...] = (xv * _row_rms(xv) * g_ref[...]).astype(BF)
            acc[...] = jnp.zeros_like(acc)

        h = h_ref[...]
        gate = _nt(h, wg_ref[...])
        up = _nt(h, wu_ref[...])
        gg_ref[...] = gate.astype(BF)
        uu_ref[...] = up.astype(BF)
        act = (gate * jax.nn.sigmoid(gate) * up).astype(BF)
        acc[...] += _nn(act, wd_ref[...])

        @pl.when(j == N_FF_TILES - 1)
        def _():
            xo = x_ref[...] + 0.5 * acc[...]
            if with_loss:
                err = xo - t_ref[...]
                dy_ref[...] = err * (1.0 / D_MODEL)

                @pl.when(t == 0)
                def _():
                    loss_ref[...] = jnp.zeros_like(loss_ref)

                loss_ref[...] += _sum_all(err * err) * (0.5 / D_MODEL)
            else:
                xo_ref[...] = xo

    row = pl.BlockSpec((tm, D_MODEL), lambda t, j: (t, 0))
    vec = pl.BlockSpec((1, D_MODEL), lambda t, j: (0, 0))
    w_spec = pl.BlockSpec((FF_TILE, D_MODEL), lambda t, j: (j, 0))
    blk = pl.BlockSpec((tm, FF_TILE), lambda t, j: (t, j))
    in_specs = [row, vec, w_spec, w_spec, w_spec] + ([row] if with_loss else [])
    out_shape = [jax.ShapeDtypeStruct((t_len, D_MODEL), BF),
                 jax.ShapeDtypeStruct((t_len, D_FF), BF),
                 jax.ShapeDtypeStruct((t_len, D_FF), BF),
                 jax.ShapeDtypeStruct((t_len, D_MODEL), F32)]
    out_specs = [row, blk, blk, row]
    if with_loss:
        out_shape.append(jax.ShapeDtypeStruct((1, 128), F32))
        out_specs.append(pl.BlockSpec((1, 128), lambda t, j: (0, 0)))
    args = (x, g, _ff_rows(wg), _ff_rows(wu), _ff_rows(wd)) + ((target,) if with_loss else ())
    body, first, first_specs = _ordered_after(body, after)
    return pl.pallas_call(body, out_shape=out_shape, grid=(t_len // tm, N_FF_TILES), in_specs=first_specs + in_specs,
                          out_specs=out_specs, scratch_shapes=[pltpu.VMEM((tm, D_MODEL), F32)], name=name,
                          compiler_params=_params(2))(*first, *args)


def _ffn_bwd_act(dy, x, g, gate, up, wg, wu, wd, name, after=None):
    t_len = x.shape[0]
    tm = min(FF_ROWS, t_len)

    def body(dy_ref, x_ref, g_ref, gg_ref, uu_ref, wg_ref, wu_ref, wd_ref,
             dyb_ref, act_ref, dgg_ref, duu_ref, dx_ref, dg_ref, acc):
        t, j = pl.program_id(0), pl.program_id(1)

        @pl.when(j == 0)
        def _():
            dyb_ref[...] = (0.5 * dy_ref[...]).astype(BF)
            acc[...] = jnp.zeros_like(acc)

        w_both = jnp.concatenate([wg_ref[...], wu_ref[...]], axis=0)
        for c in range(tm // FF_CHUNK):
            rows = slice(c * FF_CHUNK, (c + 1) * FF_CHUNK)
            d_act = _nt(dyb_ref[rows, :], wd_ref[...])
            gate = gg_ref[rows, :].astype(F32)
            upv = uu_ref[rows, :].astype(F32)
            sig = jax.nn.sigmoid(gate)
            silu = gate * sig
            d_up = (d_act * silu).astype(BF)
            d_gate = (d_act * upv * (sig * (1.0 + gate * (1.0 - sig)))).astype(BF)
            act_ref[rows, :] = (silu * upv).astype(BF)
            dgg_ref[rows, :] = d_gate
            duu_ref[rows, :] = d_up
            acc[rows, :] += _nn(jnp.concatenate([d_gate, d_up], axis=1), w_both)

        @pl.when(j == N_FF_TILES - 1)
        def _():
            xv = x_ref[...]
            dx, dg = _rms_bwd(acc[...], xv, _row_rms(xv), g_ref[...])
            dx_ref[...] = dy_ref[...] + dx

            @pl.when(t == 0)
            def _():
                dg_ref[...] = jnp.zeros_like(dg_ref)

            dg_ref[...] += dg

    row = pl.BlockSpec((tm, D_MODEL), lambda t, j: (t, 0))
    vec = pl.BlockSpec((1, D_MODEL), lambda t, j: (0, 0))
    w_spec = pl.BlockSpec((FF_TILE, D_MODEL), lambda t, j: (j, 0))
    blk = pl.BlockSpec((tm, FF_TILE), lambda t, j: (t, j))
    blk_shape = jax.ShapeDtypeStruct((t_len, D_FF), BF)
    body, first, first_specs = _ordered_after(body, after)
    return pl.pallas_call(
        body,
        out_shape=[jax.ShapeDtypeStruct((t_len, D_MODEL), BF), blk_shape, blk_shape, blk_shape,
                   jax.ShapeDtypeStruct((t_len, D_MODEL), F32), jax.ShapeDtypeStruct((1, D_MODEL), F32)],
        grid=(t_len // tm, N_FF_TILES), in_specs=first_specs + [row, row, vec, blk, blk, w_spec, w_spec, w_spec],
        out_specs=[row, blk, blk, blk, row, vec], scratch_shapes=[pltpu.VMEM((tm, D_MODEL), F32)], name=name,
        compiler_params=_params(2))(*first, dy, x, g, gate, up, _ff_rows(wg), _ff_rows(wu), _ff_rows(wd))


def _ffn_bwd_w(h, dyb, act, d_gate, d_up, name):
    t_len = h.shape[0]
    tm = min(FF_ROWS, t_len)
    nt = t_len // tm

    def body(h_ref, dyb_ref, act_ref, dgg_ref, duu_ref, dwg_ref, dwu_ref, dwd_ref, ag, au, ad):
        t = pl.program_id(1)

        @pl.when(t == 0)
        def _():
            ag[...] = jnp.zeros_like(ag)
            au[...] = jnp.zeros_like(au)
            ad[...] = jnp.zeros_like(ad)

        h = h_ref[...]
        ag[...] += _tn(dgg_ref[...], h)
        au[...] += _tn(duu_ref[...], h)
        ad[...] += _tn(act_ref[...], dyb_ref[...])

        @pl.when(t == nt - 1)
        def _():
            dwg_ref[...] = ag[...].astype(BF)
            dwu_ref[...] = au[...].astype(BF)
            dwd_ref[...] = ad[...].astype(BF)

    row = pl.BlockSpec((tm, D_MODEL), lambda j, t: (t, 0))
    blk = pl.BlockSpec((tm, FF_TILE), lambda j, t: (t, j))
    w_spec = pl.BlockSpec((FF_TILE, D_MODEL), lambda j, t: (j, 0))
    grads = pl.pallas_call(
        body,
        out_shape=[jax.ShapeDtypeStruct((D_FF, D_MODEL), BF)] * 3,
        grid=(N_FF_TILES, nt), in_specs=[row, row, blk, blk, blk], out_specs=[w_spec] * 3,
        scratch_shapes=[pltpu.VMEM((FF_TILE, D_MODEL), F32)] * 3,
        name=name, compiler_params=_params(2))(h, dyb, act, d_gate, d_up)
    return [a.reshape(N_DEV, FF_BLOCK, D_MODEL) for a in grads]


def _in_proj_fwd(x, g, w_in, name):
    t_len = x.shape[0]
    tm = min(512, t_len)

    def body(x_ref, g_ref, w_ref, h_ref, p_ref):
        xv = x_ref[...]
        h = (xv * _row_rms(xv) * g_ref[...]).astype(BF)
        h_ref[...] = h
        p_ref[...] = _nt(h, w_ref[...])

    row = pl.BlockSpec((tm, D_MODEL), lambda t: (t, 0))
    return pl.pallas_call(
        body, out_shape=[jax.ShapeDtypeStruct((t_len, D_MODEL), BF), jax.ShapeDtypeStruct((t_len, IN_COLS), F32)],
        grid=(t_len // tm,),
        in_specs=[row, pl.BlockSpec((1, D_MODEL), lambda t: (0, 0)), pl.BlockSpec((IN_COLS, D_MODEL), lambda t: (0, 0))],
        out_specs=[row, pl.BlockSpec((tm, IN_COLS), lambda t: (t, 0))], name=name, compiler_params=_params(1))(x, g, w_in)


def _in_proj_bwd(dq, dk, dv, du, dmq, w_in, h, x, g, dres, name):
    t_len = x.shape[0]
    tm = min(512, t_len)
    nt = t_len // tm
    groups = [(0, Q_COLS), (Q_COLS, KV_COLS), (Q_COLS + KV_COLS, KV_COLS), (Q_COLS + 2 * KV_COLS, 2 * CONV_CH),
              (Q_COLS + 2 * KV_COLS + 2 * CONV_CH, MQ_COLS)]

    def body(dq_ref, dk_ref, dv_ref, du_ref, dmq_ref, w_ref, h_ref, x_ref, g_ref, dres_ref, dx_ref, dg_ref, dw_ref, acc):
        t = pl.program_id(0)

        @pl.when(t == 0)
        def _():
            acc[...] = jnp.zeros_like(acc)
            dg_ref[...] = jnp.zeros_like(dg_ref)

        h = h_ref[...]
        dh = jnp.zeros((tm, D_MODEL), F32)
        for (start, width), ref in zip(groups, (dq_ref, dk_ref, dv_ref, du_ref, dmq_ref)):
            piece = ref[...]
            dh = dh + _nn(piece, w_ref[start:start + width, :])
            acc[start:start + width, :] += _tn(piece, h)
        xv = x_ref[...]
        dx, dg = _rms_bwd(dh, xv, _row_rms(xv), g_ref[...])
        dx_ref[...] = dres_ref[...] + dx
        dg_ref[...] += dg

        @pl.when(t == nt - 1)
        def _():
            dw_ref[...] = acc[...].astype(BF)

    def cols(width):
        return pl.BlockSpec((tm, width), lambda t: (t, 0))

    row = cols(D_MODEL)
    vec = pl.BlockSpec((1, D_MODEL), lambda t: (0, 0))
    full = pl.BlockSpec((IN_COLS, D_MODEL), lambda t: (0, 0))
    return pl.pallas_call(
        body,
        out_shape=[jax.ShapeDtypeStruct((t_len, D_MODEL), F32), jax.ShapeDtypeStruct((1, D_MODEL), F32),
                   jax.ShapeDtypeStruct((IN_COLS, D_MODEL), BF)],
        grid=(nt,),
        in_specs=[cols(Q_COLS), cols(KV_COLS), cols(KV_COLS), cols(2 * CONV_CH), cols(MQ_COLS), full, row, row, vec, row],
        out_specs=[row, vec, full], scratch_shapes=[pltpu.VMEM((IN_COLS, D_MODEL), F32)], name=name,
        compiler_params=_params(1))(dq, dk, dv, du, dmq, w_in, h, x, g, dres)


def _attn_fwd(proj, cos_t, sin_t, gq2, gk2, sinks, name):
    t_len = proj.shape[0]
    nb = t_len // ATTN_BLOCK

    def body(q_ref, kvc_ref, kvp_ref, cq_ref, sq_ref, cp_ref, sp_ref, gq_ref, gk_ref, sk_ref, ya_ref, lse_ref):
        n = pl.program_id(0)
        cq, sq = cq_ref[...], sq_ref[...]
        c_all = jnp.concatenate([cp_ref[...], cq], axis=0)
        s_all = jnp.concatenate([sp_ref[...], sq], axis=0)
        kv = jnp.concatenate([kvp_ref[...], kvc_ref[...]], axis=0)
        k_raw, v = kv[:, :PAIR], kv[:, PAIR:]
        q_raws = [q_ref[:, pr * PAIR:(pr + 1) * PAIR] for pr in range(N_Q_HEADS // 2)]
        rms = _pair_rms_many(q_raws + [k_raw])
        kn = _rope(k_raw * rms[-1] * gk_ref[...], c_all, s_all)
        row = lax.broadcasted_iota(jnp.int32, (ATTN_BLOCK, 2 * ATTN_BLOCK), 0)
        col = lax.broadcasted_iota(jnp.int32, (ATTN_BLOCK, 2 * ATTN_BLOCK), 1)
        rel = row + ATTN_BLOCK - col
        valid = (rel >= 0) & (rel < ATTN_BLOCK) & ((col >= ATTN_BLOCK) | (n > 0))
        for pr in range(N_Q_HEADS // 2):
            qn = _rope(q_raws[pr] * rms[pr] * gq_ref[...], cq, sq)
            kvh = (2 * pr) // Q_PER_KV
            kh = kn[:, kvh * HEAD_DIM:(kvh + 1) * HEAD_DIM].astype(BF)
            v_ones = jnp.concatenate([v[:, kvh * HEAD_DIM:(kvh + 1) * HEAD_DIM],
                                      jnp.ones((2 * ATTN_BLOCK, HEAD_DIM), F32)], axis=1).astype(BF)
            for hh in range(2):
                hd = 2 * pr + hh
                qh = qn[:, hh * HEAD_DIM:(hh + 1) * HEAD_DIM].astype(BF)
                s = jnp.where(valid, _nt(qh, kh) * SCALE, NEG)
                sink = sk_ref[0:1, hd:hd + 1]
                m = jnp.maximum(jnp.max(s, axis=-1, keepdims=True), sink)
                ov = _nn(jnp.exp(s - m).astype(BF), v_ones)
                den = ov[:, HEAD_DIM:HEAD_DIM + 1] + jnp.exp(sink - m)
                ya_ref[:, hd * HEAD_DIM:(hd + 1) * HEAD_DIM] = (ov[:, :HEAD_DIM] / den).astype(BF)
                lse_ref[:, hd:hd + 1] = m + jnp.log(den)

    prev = lambda n: (jnp.maximum(n - 1, 0), 0)
    tab = pl.BlockSpec((ATTN_BLOCK, PAIR), lambda n: (n, 0))
    tab_p = pl.BlockSpec((ATTN_BLOCK, PAIR), prev)
    one = lambda w: pl.BlockSpec((1, w), lambda n: (0, 0))
    return pl.pallas_call(
        body, out_shape=[jax.ShapeDtypeStruct((t_len, Q_COLS), BF), jax.ShapeDtypeStruct((t_len, N_Q_HEADS), F32)],
        grid=(nb,),
        in_specs=[pl.BlockSpec((ATTN_BLOCK, Q_COLS), lambda n: (n, 0)),
                  pl.BlockSpec((ATTN_BLOCK, 2 * KV_COLS), lambda n: (n, 2)),
                  pl.BlockSpec((ATTN_BLOCK, 2 * KV_COLS), lambda n: (jnp.maximum(n - 1, 0), 2)),
                  tab, tab, tab_p, tab_p, one(PAIR), one(PAIR), one(N_Q_HEADS)],
        out_specs=[pl.BlockSpec((ATTN_BLOCK, Q_COLS), lambda n: (n, 0)),
                   pl.BlockSpec((ATTN_BLOCK, N_Q_HEADS), lambda n: (n, 0))],
        name=name, compiler_params=_params(1))(proj, proj, proj, cos_t, sin_t, cos_t, sin_t, gq2, gk2, sinks)


def _attn_bwd_q(proj, cos_t, sin_t, gq2, gk2, sinks, ya, dmix, lse, name):
    t_len = proj.shape[0]
    nb = t_len // ATTN_BLOCK

    def body(q_ref, kvc_ref, kvp_ref, cq_ref, sq_ref, cp_ref, sp_ref, gq_ref, gk_ref, sk_ref, ya_ref, do_ref, lse_ref,
             dq_ref, dgq_ref, dsk_ref, scr):
        n = pl.program_id(0)

        @pl.when(n == 0)
        def _():
            dgq_ref[...] = jnp.zeros_like(dgq_ref)
            dsk_ref[...] = jnp.zeros_like(dsk_ref)

        cq, sq = cq_ref[...], sq_ref[...]
        c_all = jnp.concatenate([cp_ref[...], cq], axis=0)
        s_all = jnp.concatenate([sp_ref[...], sq], axis=0)
        kv = jnp.concatenate([kvp_ref[...], kvc_ref[...]], axis=0)
        k_raw, v = kv[:, :PAIR], kv[:, PAIR:]
        n_pairs = N_Q_HEADS // 2
        pair_lanes = [slice(pr * PAIR, (pr + 1) * PAIR) for pr in range(n_pairs)]
        q_raws = [q_ref[:, lanes] for lanes in pair_lanes]
        rms = _pair_rms_many(q_raws + [k_raw])
        dsums = _pair_sums([do_ref[:, lanes] * ya_ref[:, lanes].astype(F32) for lanes in pair_lanes])
        kn = _rope(k_raw * rms[-1] * gk_ref[...], c_all, s_all)
        row = lax.broadcasted_iota(jnp.int32, (ATTN_BLOCK, 2 * ATTN_BLOCK), 0)
        col = lax.broadcasted_iota(jnp.int32, (ATTN_BLOCK, 2 * ATTN_BLOCK), 1)
        rel = row + ATTN_BLOCK - col
        valid = (rel >= 0) & (rel < ATTN_BLOCK) & ((col >= ATTN_BLOCK) | (n > 0))
        gq = gq_ref[...]
        for pr in range(n_pairs):
            qn = _rope(q_raws[pr] * rms[pr] * gq, cq, sq)
            kvh = (2 * pr) // Q_PER_KV
            kh = kn[:, kvh * HEAD_DIM:(kvh + 1) * HEAD_DIM].astype(BF)
            vh = v[:, kvh * HEAD_DIM:(kvh + 1) * HEAD_DIM].astype(BF)
            dsum2 = dsums[pr]
            for hh in range(2):
                hd = 2 * pr + hh
                qh = qn[:, hh * HEAD_DIM:(hh + 1) * HEAD_DIM].astype(BF)
                lse_h = lse_ref[:, hd:hd + 1]
                p = jnp.exp(jnp.where(valid, _nt(qh, kh) * SCALE, NEG) - lse_h)
                d_o = do_ref[:, hd * HEAD_DIM:(hd + 1) * HEAD_DIM]
                dsum = dsum2[:, hh * HEAD_DIM:hh * HEAD_DIM + 1]
                ds = p * (_nt(d_o.astype(BF), vh) - dsum)
                scr[pr, :, hh * HEAD_DIM:(hh + 1) * HEAD_DIM] = _nn(ds.astype(BF), kh) * SCALE
                sink = sk_ref[0:1, hd:hd + 1]
                dsk_ref[0:1, hd:hd + 1] += -jnp.sum(jnp.exp(sink - lse_h) * dsum, axis=0, keepdims=True)
        back = _pair_rms_bwd_many([_rope_t(scr[pr], cq, sq) for pr in range(n_pairs)], q_raws, rms[:n_pairs], gq)
        for lanes, (dx, dg) in zip(pair_lanes, back):
            dq_ref[:, lanes] = dx.astype(BF)
            dgq_ref[...] += dg

    prev = lambda n: (jnp.maximum(n - 1, 0), 0)
    tab = pl.BlockSpec((ATTN_BLOCK, PAIR), lambda n: (n, 0))
    tab_p = pl.BlockSpec((ATTN_BLOCK, PAIR), prev)
    one = lambda w: pl.BlockSpec((1, w), lambda n: (0, 0))
    qblk = pl.BlockSpec((ATTN_BLOCK, Q_COLS), lambda n: (n, 0))
    return pl.pallas_call(
        body,
        out_shape=[jax.ShapeDtypeStruct((t_len, Q_COLS), BF), jax.ShapeDtypeStruct((1, PAIR), F32),
                   jax.ShapeDtypeStruct((1, N_Q_HEADS), F32)],
        grid=(nb,),
        in_specs=[qblk, pl.BlockSpec((ATTN_BLOCK, 2 * KV_COLS), lambda n: (n, 2)),
                  pl.BlockSpec((ATTN_BLOCK, 2 * KV_COLS), lambda n: (jnp.maximum(n - 1, 0), 2)),
                  tab, tab, tab_p, tab_p, one(PAIR), one(PAIR), one(N_Q_HEADS), qblk, qblk,
                  pl.BlockSpec((ATTN_BLOCK, N_Q_HEADS), lambda n: (n, 0))],
        out_specs=[qblk, one(PAIR), one(N_Q_HEADS)], scratch_shapes=[pltpu.VMEM((N_Q_HEADS // 2, ATTN_BLOCK, PAIR), F32)],
        name=name, compiler_params=_params(1))(proj, proj, proj, cos_t, sin_t, cos_t, sin_t, gq2, gk2, sinks, ya, dmix, lse)


def _attn_bwd_kv(proj, cos_t, sin_t, gq2, gk2, ya, dmix, lse, name):
    t_len = proj.shape[0]
    nb = t_len // ATTN_BLOCK

    def body(kv_ref, q0_ref, q1_ref, ck_ref, sk_ref, c1_ref, s1_ref, gq_ref, gk_ref, o0_ref, o1_ref, do0_ref, do1_ref,
             l0_ref, l1_ref, dk_ref, dv_ref, dgk_ref, dkn_scr, dv_scr):
        m = pl.program_id(0)

        @pl.when(m == 0)
        def _():
            dgk_ref[...] = jnp.zeros_like(dgk_ref)

        ck, sk = ck_ref[...], sk_ref[...]
        c_all = jnp.concatenate([ck, c1_ref[...]], axis=0)
        s_all = jnp.concatenate([sk, s1_ref[...]], axis=0)
        kvv = kv_ref[...]
        k_raw, v = kvv[:, :PAIR], kvv[:, PAIR:]
        pair_lanes = [slice(pr * PAIR, (pr + 1) * PAIR) for pr in range(N_Q_HEADS // 2)]
        q_raws = [jnp.concatenate([q0_ref[:, lanes], q1_ref[:, lanes]], axis=0) for lanes in pair_lanes]
        d_os = [jnp.concatenate([do0_ref[:, lanes], do1_ref[:, lanes]], axis=0) for lanes in pair_lanes]
        rms = [_pair_rms(a) for a in q_raws + [k_raw]]
        dsums = [_pair_sum(d_o2 * jnp.concatenate([o0_ref[:, lanes], o1_ref[:, lanes]], axis=0).astype(F32))
                 for d_o2, lanes in zip(d_os, pair_lanes)]
        rk = rms[-1]
        gk = gk_ref[...]
        kn = _rope(k_raw * rk * gk, ck, sk)
        row = lax.broadcasted_iota(jnp.int32, (2 * ATTN_BLOCK, ATTN_BLOCK), 0)
        col = lax.broadcasted_iota(jnp.int32, (2 * ATTN_BLOCK, ATTN_BLOCK), 1)
        valid = ((row < ATTN_BLOCK) & (row >= col)) | ((row >= ATTN_BLOCK) & (row - ATTN_BLOCK < col) & (m < nb - 1))
        lse2 = jnp.concatenate([l0_ref[...], l1_ref[...]], axis=0)
        d_kn, d_v = [None, None], [None, None]
        plus = lambda acc, term: term if acc is None else acc + term
        for pr in range(N_Q_HEADS // 2):
            qn = _rope(q_raws[pr] * rms[pr] * gq_ref[...], c_all, s_all)
            d_o2, dsum2 = d_os[pr], dsums[pr]
            kvh = (2 * pr) // Q_PER_KV
            khs = slice(kvh * HEAD_DIM, (kvh + 1) * HEAD_DIM)
            kh = kn[:, khs].astype(BF)
            vh = v[:, khs].astype(BF)
            for hh in range(2):
                hd = 2 * pr + hh
                hs = slice(hh * HEAD_DIM, (hh + 1) * HEAD_DIM)
                qh = qn[:, hs].astype(BF)
                d_oh = d_o2[:, hs].astype(BF)
                p = jnp.exp(jnp.where(valid, _nt(qh, kh) * SCALE, NEG) - lse2[:, hd:hd + 1])
                d_v[kvh] = plus(d_v[kvh], _tn(p.astype(BF), d_oh))
                ds = p * (_nt(d_oh, vh) - dsum2[:, hh * HEAD_DIM:hh * HEAD_DIM + 1])
                d_kn[kvh] = plus(d_kn[kvh], _tn(ds.astype(BF), qh))
        for kvh in range(2):
            khs = slice(kvh * HEAD_DIM, (kvh + 1) * HEAD_DIM)
            dkn_scr[:, khs] = d_kn[kvh] * SCALE
            dv_scr[:, khs] = d_v[kvh]
        dx, dg = _pair_rms_bwd(_rope_t(dkn_scr[...], ck, sk), k_raw, rk, gk)
        dk_ref[...] = dx.astype(BF)
        dv_ref[...] = dv_scr[...].astype(BF)
        dgk_ref[...] += dg

    nxt = lambda m: (jnp.minimum(m + 1, nb - 1), 0)
    cur = lambda m: (m, 0)
    tab = lambda f: pl.BlockSpec((ATTN_BLOCK, PAIR), f)
    qb = lambda f: pl.BlockSpec((ATTN_BLOCK, Q_COLS), f)
    lb = lambda f: pl.BlockSpec((ATTN_BLOCK, N_Q_HEADS), f)
    one = pl.BlockSpec((1, PAIR), lambda m: (0, 0))
    return pl.pallas_call(
        body,
        out_shape=[jax.ShapeDtypeStruct((t_len, KV_COLS), BF), jax.ShapeDtypeStruct((t_len, KV_COLS), BF),
                   jax.ShapeDtypeStruct((1, PAIR), F32)],
        grid=(nb,),
        in_specs=[pl.BlockSpec((ATTN_BLOCK, 2 * KV_COLS), lambda m: (m, 2)), qb(cur), qb(nxt), tab(cur), tab(cur), tab(nxt),
                  tab(nxt), one, one, qb(cur), qb(nxt), qb(cur), qb(nxt), lb(cur), lb(nxt)],
        out_specs=[tab(cur), tab(cur), one],
        scratch_shapes=[pltpu.VMEM((ATTN_BLOCK, PAIR), F32), pltpu.VMEM((ATTN_BLOCK, PAIR), F32)],
        name=name, compiler_params=_params(1))(proj, proj, proj, cos_t, sin_t, cos_t, sin_t, gq2, gk2, ya, ya, dmix, dmix,
                                               lse, lse)


def _conv_taps(blk, w, offset_of):
    acc = jnp.zeros((CONV_CHUNK, CONV_CH), F32)
    for k in range(CONV_WIDTH):
        o = offset_of(k)
        acc = acc + w[k:k + 1, :] * blk[o:o + CONV_CHUNK, :]
    return acc


def _conv_fwd(proj, w_dw, b_dw, g_ln, b_ln, name):
    t_len = proj.shape[0]
    nc = t_len // CONV_CHUNK

    def body(a_ref, gt_ref, w_ref, b_ref, g_ref, bl_ref, yc_ref, cv_ref, pad):
        pad[0:CONV_PAD, :] = jnp.zeros((CONV_PAD, CONV_CH), F32)

        def glu(c, carry):
            rows = pl.ds(pl.multiple_of(c * CONV_CHUNK, CONV_CHUNK), CONV_CHUNK)
            dst = pl.ds(pl.multiple_of(c * CONV_CHUNK + CONV_PAD, CONV_PAD), CONV_CHUNK)
            pad[dst, :] = a_ref[rows, :] * jax.nn.sigmoid(gt_ref[rows, :])
            return carry

        lax.fori_loop(0, nc, glu, 0)
        w = w_ref[...]

        def conv(c, carry):
            base = pl.multiple_of(c * CONV_CHUNK, CONV_CHUNK)
            blk = pad[pl.ds(base, CONV_CHUNK + CONV_PAD), :]
            cv = _conv_taps(blk, w, lambda k: k + CONV_PAD - (CONV_WIDTH - 1)) + b_ref[...]
            mu = jnp.mean(cv, axis=-1, keepdims=True)
            xc = cv - mu
            z = xc * lax.rsqrt(jnp.mean(xc * xc, axis=-1, keepdims=True) + EPS) * g_ref[...] + bl_ref[...]
            rows = pl.ds(base, CONV_CHUNK)
            cv_ref[rows, :] = cv
            yc_ref[rows, :] = (z * jax.nn.sigmoid(z)).astype(BF)
            return carry

        lax.fori_loop(0, nc, conv, 0)

    vec = pl.BlockSpec((1, CONV_CH), lambda i: (0, 0))
    full = pl.BlockSpec((t_len, CONV_CH), lambda i: (0, 0))
    return pl.pallas_call(
        body, out_shape=[jax.ShapeDtypeStruct((t_len, CONV_CH), BF), jax.ShapeDtypeStruct((t_len, CONV_CH), F32)],
        grid=(1,),
        in_specs=[pl.BlockSpec((t_len, CONV_CH), lambda i: (0, 3)), pl.BlockSpec((t_len, CONV_CH), lambda i: (0, 4)),
                  pl.BlockSpec((CONV_WIDTH, CONV_CH), lambda i: (0, 0)), vec, vec, vec],
        out_specs=[full, full], scratch_shapes=[pltpu.VMEM((t_len + CONV_PAD, CONV_CH), F32)], name=name,
        compiler_params=_params(1))(proj, proj, w_dw, b_dw, g_ln, b_ln)


def _conv_bwd(proj, cv, dmix, w_dw, g_ln, b_ln, name):
    t_len = proj.shape[0]
    nc = t_len // CONV_CHUNK

    def body(a_ref, gt_ref, cv_ref, dy_ref, w_ref, g_ref, bl_ref, du_ref, dw_ref, db_ref, dgl_ref, dbl_ref, pad, dpad):
        pad[0:CONV_PAD, :] = jnp.zeros((CONV_PAD, CONV_CH), F32)
        dpad[t_len:t_len + CONV_PAD, :] = jnp.zeros((CONV_PAD, CONV_CH), F32)
        dw_ref[...] = jnp.zeros_like(dw_ref)
        db_ref[...] = jnp.zeros_like(db_ref)
        dgl_ref[...] = jnp.zeros_like(dgl_ref)
        dbl_ref[...] = jnp.zeros_like(dbl_ref)

        def norm_bwd(c, carry):
            base = pl.multiple_of(c * CONV_CHUNK, CONV_CHUNK)
            rows = pl.ds(base, CONV_CHUNK)
            dst = pl.ds(pl.multiple_of(c * CONV_CHUNK + CONV_PAD, CONV_PAD), CONV_CHUNK)
            pad[dst, :] = a_ref[rows, :] * jax.nn.sigmoid(gt_ref[rows, :])
            cvv = cv_ref[rows, :]
            xc = cvv - jnp.mean(cvv, axis=-1, keepdims=True)
            rs = lax.rsqrt(jnp.mean(xc * xc, axis=-1, keepdims=True) + EPS)
            xhat = xc * rs
            z = xhat * g_ref[...] + bl_ref[...]
            sg = jax.nn.sigmoid(z)
            dz = dy_ref[rows, :] * (sg * (1.0 + z * (1.0 - sg)))
            dgl_ref[...] += jnp.sum(dz * xhat, axis=0, keepdims=True)
            dbl_ref[...] += jnp.sum(dz, axis=0, keepdims=True)
            dxh = dz * g_ref[...]
            dcv = rs * (dxh - jnp.mean(dxh, axis=-1, keepdims=True) - xhat * jnp.mean(dxh * xhat, axis=-1, keepdims=True))
            db_ref[...] += jnp.sum(dcv, axis=0, keepdims=True)
            dpad[rows, :] = dcv
            return carry

        lax.fori_loop(0, nc, norm_bwd, 0)
        w = w_ref[...]

        def conv_bwd(c, carry):
            base = pl.multiple_of(c * CONV_CHUNK, CONV_CHUNK)
            rows = pl.ds(base, CONV_CHUNK)
            dblk = dpad[pl.ds(base, CONV_CHUNK + CONV_PAD), :]
            dhc = _conv_taps(dblk, w, lambda k: CONV_WIDTH - 1 - k)
            a = a_ref[rows, :]
            sg = jax.nn.sigmoid(gt_ref[rows, :])
            du_ref[rows, 0:CONV_CH] = (dhc * sg).astype(BF)
            du_ref[rows, CONV_CH:2 * CONV_CH] = (dhc * a * sg * (1.0 - sg)).astype(BF)
            hblk = pad[pl.ds(base, CONV_CHUNK + CONV_PAD), :]
            dcv = dblk[0:CONV_CHUNK, :]
            for k in range(CONV_WIDTH):
                o = k + CONV_PAD - (CONV_WIDTH - 1)
                dw_ref[k:k + 1, :] += jnp.sum(dcv * hblk[o:o + CONV_CHUNK, :], axis=0, keepdims=True)
            return carry

        lax.fori_loop(0, nc, conv_bwd, 0)

    vec = pl.BlockSpec((1, CONV_CH), lambda i: (0, 0))
    full = pl.BlockSpec((t_len, CONV_CH), lambda i: (0, 0))
    wspec = pl.BlockSpec((CONV_WIDTH, CONV_CH), lambda i: (0, 0))
    vshape = jax.ShapeDtypeStruct((1, CONV_CH), F32)
    return pl.pallas_call(
        body,
        out_shape=[jax.ShapeDtypeStruct((t_len, 2 * CONV_CH), BF), jax.ShapeDtypeStruct((CONV_WIDTH, CONV_CH), F32),
                   vshape, vshape, vshape],
        grid=(1,),
        in_specs=[pl.BlockSpec((t_len, CONV_CH), lambda i: (0, 3)), pl.BlockSpec((t_len, CONV_CH), lambda i: (0, 4)), full,
                  pl.BlockSpec((t_len, CONV_CH), lambda i: (0, 2)), wspec, vec, vec],
        out_specs=[pl.BlockSpec((t_len, 2 * CONV_CH), lambda i: (0, 0)), wspec, vec, vec, vec],
        scratch_shapes=[pltpu.VMEM((t_len + CONV_PAD, CONV_CH), F32), pltpu.VMEM((t_len + CONV_PAD, CONV_CH), F32)],
        name=name, compiler_params=_params(1))(proj, proj, cv, dmix, w_dw, g_ln, b_ln)


def _mem_kv_fwd(mem, g, w, name):
    def body(m_ref, g_ref, w_ref, h_ref, kv_ref):
        mv = m_ref[...]
        h = (mv * _row_rms(mv) * g_ref[...]).astype(BF)
        h_ref[...] = h
        kv_ref[...] = _nn(h, w_ref[...])

    m_len = mem.shape[0]
    return pl.pallas_call(
        body, out_shape=[jax.ShapeDtypeStruct((m_len, D_MODEL), BF), jax.ShapeDtypeStruct((m_len, 2 * MQ_COLS), F32)],
        name=name, compiler_params=pltpu.CompilerParams(vmem_limit_bytes=VMEM_LIMIT_BYTES))(mem, g, w)


def _mem_kv_bwd(mem, g, h, w, dkv, name):
    def body(m_ref, g_ref, h_ref, w_ref, dkv_ref, dw_ref, dg_ref):
        dkv_b = dkv_ref[...].astype(BF)
        dw_ref[...] = _tn(h_ref[...], dkv_b).astype(BF)
        mv = m_ref[...]
        dg_ref[...] = jnp.sum(_nt(dkv_b, w_ref[...]) * mv * _row_rms(mv), axis=0, keepdims=True)

    return pl.pallas_call(
        body, out_shape=[jax.ShapeDtypeStruct((D_MODEL, 2 * MQ_COLS), BF), jax.ShapeDtypeStruct((1, D_MODEL), F32)],
        name=name, compiler_params=pltpu.CompilerParams(vmem_limit_bytes=VMEM_LIMIT_BYTES))(mem, g, h, w, dkv)


def _mem_attn_fwd(proj, mkv, gq2, gk2, name):
    t_len = proj.shape[0]
    tm = min(256, t_len)

    def body(q_ref, kv_ref, gq_ref, gk_ref, ym_ref, lse_ref):
        kvv = kv_ref[...]
        for pr in range(N_MEM_HEADS // 2):
            lanes = slice(pr * PAIR, (pr + 1) * PAIR)
            k_raw = kvv[:, lanes]
            kn = k_raw * _pair_rms(k_raw) * gk_ref[...]
            v = kvv[:, MQ_COLS + pr * PAIR:MQ_COLS + (pr + 1) * PAIR]
            q_raw = q_ref[:, lanes]
            qn = q_raw * _pair_rms(q_raw) * gq_ref[...]
            for hh in range(2):
                hd = 2 * pr + hh
                hs = slice(hh * HEAD_DIM, (hh + 1) * HEAD_DIM)
                s = _nt(qn[:, hs].astype(BF), kn[:, hs].astype(BF)) * SCALE
                m = jnp.max(s, axis=-1, keepdims=True)
                p = jnp.exp(s - m)
                den = jnp.sum(p, axis=-1, keepdims=True)
                ym_ref[:, hd * HEAD_DIM:(hd + 1) * HEAD_DIM] = _nn((p / den).astype(BF), v[:, hs].astype(BF)).astype(BF)
                lse_ref[:, hd:hd + 1] = m + jnp.log(den)

    m_len = mkv.shape[0]
    one = pl.BlockSpec((1, PAIR), lambda t: (0, 0))
    return pl.pallas_call(
        body, out_shape=[jax.ShapeDtypeStruct((t_len, MQ_COLS), BF), jax.ShapeDtypeStruct((t_len, N_MEM_HEADS), F32)],
        grid=(t_len // tm,),
        in_specs=[pl.BlockSpec((tm, MQ_COLS), lambda t: (t, 5)), pl.BlockSpec((m_len, 2 * MQ_COLS), lambda t: (0, 0)), one, one],
        out_specs=[pl.BlockSpec((tm, MQ_COLS), lambda t: (t, 0)), pl.BlockSpec((tm, N_MEM_HEADS), lambda t: (t, 0))],
        name=name, compiler_params=_params(1))(proj, mkv, gq2, gk2)


def _mem_attn_bwd(proj, mkv, gq2, gk2, ym, dmix, lse, name):
    t_len = proj.shape[0]
    tm = min(256, t_len)
    nt = t_len // tm
    m_len = mkv.shape[0]

    def body(q_ref, kv_ref, gq_ref, gk_ref, ym_ref, do_ref, lse_ref, dq_ref, dkv_ref, dgq_ref, dgk_ref, dkn_scr, dv_scr, scr):
        t = pl.program_id(0)

        @pl.when(t == 0)
        def _():
            dkn_scr[...] = jnp.zeros_like(dkn_scr)
            dv_scr[...] = jnp.zeros_like(dv_scr)
            dgq_ref[...] = jnp.zeros_like(dgq_ref)

        kvv = kv_ref[...]
        for pr in range(N_MEM_HEADS // 2):
            lanes = slice(pr * PAIR, (pr + 1) * PAIR)
            k_raw = kvv[:, lanes]
            kn = k_raw * _pair_rms(k_raw) * gk_ref[...]
            v = kvv[:, MQ_COLS + pr * PAIR:MQ_COLS + (pr + 1) * PAIR]
            q_raw = q_ref[:, lanes]
            rq = _pair_rms(q_raw)
            qn = q_raw * rq * gq_ref[...]
            d_o = do_ref[:, lanes]
            dsum = _pair_sum(d_o * ym_ref[:, lanes].astype(F32))
            for hh in range(2):
                hd = 2 * pr + hh
                hs = slice(hh * HEAD_DIM, (hh + 1) * HEAD_DIM)
                cols = slice(hd * HEAD_DIM, (hd + 1) * HEAD_DIM)
                qh = qn[:, hs].astype(BF)
                kh = kn[:, hs].astype(BF)
                d_oh = d_o[:, hs].astype(BF)
                p = jnp.exp(_nt(qh, kh) * SCALE - lse_ref[:, hd:hd + 1])
                dv_scr[:, cols] += _tn(p.astype(BF), d_oh)
                ds = (p * (_nt(d_oh, v[:, hs].astype(BF)) - dsum[:, hh * HEAD_DIM:hh * HEAD_DIM + 1])).astype(BF)
                scr[pr, :, hs] = _nn(ds, kh) * SCALE
                dkn_scr[:, cols] += _tn(ds, qh) * SCALE
            dx, dg = _pair_rms_bwd(scr[pr], q_raw, rq, gq_ref[...])
            dq_ref[:, lanes] = dx.astype(BF)
            dgq_ref[...] += dg

        @pl.when(t == nt - 1)
        def _():
            dgk = jnp.zeros((1, PAIR), F32)
            for pr in range(N_MEM_HEADS // 2):
                lanes = slice(pr * PAIR, (pr + 1) * PAIR)
                k_raw = kvv[:, lanes]
                dx, dg = _pair_rms_bwd(dkn_scr[:, lanes], k_raw, _pair_rms(k_raw), gk_ref[...])
                dkv_ref[:, lanes] = dx
                dgk = dgk + dg
            dkv_ref[:, MQ_COLS:2 * MQ_COLS] = dv_scr[...]
            dgk_ref[...] = dgk

    one = pl.BlockSpec((1, PAIR), lambda t: (0, 0))
    kvspec = pl.BlockSpec((m_len, 2 * MQ_COLS), lambda t: (0, 0))
    qspec = pl.BlockSpec((tm, MQ_COLS), lambda t: (t, 0))
    return pl.pallas_call(
        body,
        out_shape=[jax.ShapeDtypeStruct((t_len, MQ_COLS), BF), jax.ShapeDtypeStruct((m_len, 2 * MQ_COLS), F32),
                   jax.ShapeDtypeStruct((1, PAIR), F32), jax.ShapeDtypeStruct((1, PAIR), F32)],
        grid=(nt,),
        in_specs=[pl.BlockSpec((tm, MQ_COLS), lambda t: (t, 5)), kvspec, one, one, qspec,
                  pl.BlockSpec((tm, MQ_COLS), lambda t: (t, 3)), pl.BlockSpec((tm, N_MEM_HEADS), lambda t: (t, 0))],
        out_specs=[qspec, kvspec, one, one],
        scratch_shapes=[pltpu.VMEM((m_len, MQ_COLS), F32), pltpu.VMEM((m_len, MQ_COLS), F32),
                        pltpu.VMEM((N_MEM_HEADS // 2, tm, PAIR), F32)],
        name=name, compiler_params=_params(1))(proj, mkv, gq2, gk2, ym, dmix, lse)


MIX_GROUPS = [(0, Q_COLS), (Q_COLS, CONV_CH), (Q_COLS + CONV_CH, MQ_COLS)]


def _out_proj_fwd(x, ya, yc, ym, w_out, name, after=None):
    t_len = x.shape[0]
    tm = min(512, t_len)

    def body(x_ref, ya_ref, yc_ref, ym_ref, w_ref, xo_ref):
        y = x_ref[...]
        for (start, width), ref in zip(MIX_GROUPS, (ya_ref, yc_ref, ym_ref)):
            y = y + _nn(ref[...], w_ref[start:start + width, :])
        xo_ref[...] = y

    cols = lambda w: pl.BlockSpec((tm, w), lambda t: (t, 0))
    body, first, first_specs = _ordered_after(body, after)
    return pl.pallas_call(
        body, out_shape=jax.ShapeDtypeStruct((t_len, D_MODEL), F32), grid=(t_len // tm,),
        in_specs=first_specs + [cols(D_MODEL), cols(Q_COLS), cols(CONV_CH), cols(MQ_COLS),
                                pl.BlockSpec((D_MODEL, D_MODEL), lambda t: (0, 0))],
        out_specs=cols(D_MODEL), name=name, compiler_params=_params(1))(*first, x, ya, yc, ym, w_out)


def _out_proj_bwd(dx, ya, yc, ym, w_out, name, after=None):
    t_len = dx.shape[0]
    tm = min(512, t_len)
    nt = t_len // tm

    def body(dx_ref, ya_ref, yc_ref, ym_ref, w_ref, dmix_ref, dw_ref, acc):
        t = pl.program_id(0)

        @pl.when(t == 0)
        def _():
            acc[...] = jnp.zeros_like(acc)

        dxb = dx_ref[...].astype(BF)
        dmix_ref[...] = _nt(dxb, w_ref[...])
        for (start, width), ref in zip(MIX_GROUPS, (ya_ref, yc_ref, ym_ref)):
            acc[start:start + width, :] += _tn(ref[...], dxb)

        @pl.when(t == nt - 1)
        def _():
            dw_ref[...] = acc[...].astype(BF)

    cols = lambda w: pl.BlockSpec((tm, w), lambda t: (t, 0))
    full = pl.BlockSpec((D_MODEL, D_MODEL), lambda t: (0, 0))
    body, first, first_specs = _ordered_after(body, after)
    return pl.pallas_call(
        body, out_shape=[jax.ShapeDtypeStruct((t_len, D_MODEL), F32), jax.ShapeDtypeStruct((D_MODEL, D_MODEL), BF)],
        grid=(nt,), in_specs=first_specs + [cols(D_MODEL), cols(Q_COLS), cols(CONV_CH), cols(MQ_COLS), full],
        out_specs=[cols(D_MODEL), full], scratch_shapes=[pltpu.VMEM((D_MODEL, D_MODEL), F32)], name=name,
        compiler_params=_params(1))(*first, dx, ya, yc, ym, w_out)


N_PEERS = N_DEV - 1
HBM_SPEC = pl.BlockSpec(memory_space=pltpu.HBM)
SEM_SPEC = pl.BlockSpec(memory_space=pltpu.SEMAPHORE)
EFFECT = pltpu.SideEffectType.DATAFLOW_SIDE_EFFECTING


def _my_place():
    x, y, c = lax.axis_index("x"), lax.axis_index("y"), lax.axis_index("c")
    return x, y, c, 4 * x + 2 * y + c


def _peer(k):
    x, y, c, _ = _my_place()
    px = 1 - x if k & 4 else x
    py = 1 - y if k & 2 else y
    pc = 1 - c if k & 1 else c
    return (px, py, pc), 4 * px + 2 * py + pc


def _gather_two_level(shards, name):
    n = len(shards)

    def body(*refs):
        ins, outs = refs[:n], refs[n:2 * n]
        send_sems, recv_sems, local_sems = refs[2 * n:]
        x, y, c, me = _my_place()
        sibling = (x, y, 1 - c)
        chips = [(1 - x, y), (x, 1 - y), (1 - x, 1 - y)]
        slot = lambda px, py, pc: 4 * px + 2 * py + pc

        def copy(w, k, src, dst_slot, to):
            return pltpu.make_async_remote_copy(
                src_ref=src, dst_ref=outs[w].at[dst_slot], send_sem=send_sems.at[w * N_PEERS + k],
                recv_sem=recv_sems.at[w * N_PEERS + k], device_id=to, device_id_type=pl.DeviceIdType.MESH)

        local = [pltpu.make_async_copy(ins[w], outs[w].at[me], local_sems.at[w]) for w in range(n)]
        for cp in local:
            cp.start()
        sends = []
        for w in range(n):
            for j, chip in enumerate(chips):
                sends.append(copy(w, 1 + j, ins[w], me, (*chip, c)))
            sends.append(copy(w, 0, ins[w], me, sibling))
        for cp in sends:
            cp.start()
        for w in range(n):
            for j, chip in enumerate(chips):
                got = slot(*chip, c)
                copy(w, 1 + j, ins[w], got, (*chip, c)).wait_recv()
                fwd = copy(w, 4 + j, outs[w].at[got], got, sibling)
                fwd.start()
                sends.append(fwd)
        for w in range(n):
            copy(w, 0, ins[w], slot(x, y, 1 - c), sibling).wait_recv()
            for j, chip in enumerate(chips):
                copy(w, 4 + j, ins[w], slot(*chip, 1 - c), sibling).wait_recv()
        for cp in sends:
            cp.wait_send()
        for cp in local:
            cp.wait()

    hbm = pl.BlockSpec(memory_space=pl.ANY)
    return pl.pallas_call(
        body, out_shape=[jax.ShapeDtypeStruct((N_DEV,) + a.shape, a.dtype) for a in shards], in_specs=[hbm] * n,
        out_specs=[hbm] * n,
        scratch_shapes=[pltpu.SemaphoreType.DMA((n * N_PEERS,)), pltpu.SemaphoreType.DMA((n * N_PEERS,)),
                        pltpu.SemaphoreType.DMA((n,))],
        name=name)(*shards)


def _split_copies(srcs, lands, gather, send_sems, recv_sems):
    _, _, _, me = _my_place()
    pairs = []
    for w in range(len(srcs)):
        for k in range(1, N_DEV):
            dev, idx = _peer(k)
            src = srcs[w] if gather[w] else srcs[w].at[idx]
            sems = dict(send_sem=send_sems.at[w * N_PEERS + k - 1], recv_sem=recv_sems.at[w * N_PEERS + k - 1], device_id=dev,
                        device_id_type=pl.DeviceIdType.MESH)
            pairs.append((pltpu.make_async_remote_copy(src_ref=src, dst_ref=lands[w].at[me], **sems),
                          pltpu.make_async_remote_copy(src_ref=src, dst_ref=lands[w].at[idx], **sems)))
    return pairs


def _exchange_start(srcs, lands, gather, after, name):
    n = len(srcs)

    def body(*refs):
        src_refs, land_refs = refs[:n], refs[n:2 * n]
        send_sems, recv_sems = refs[2 * n + 1], refs[2 * n + 2]
        token = refs[-1]
        for out_going, _ in _split_copies(src_refs, land_refs, gather, send_sems, recv_sems):
            out_going.start()
        token[...] = jnp.zeros_like(token)

    arrays = list(srcs) + list(lands)
    out = pl.pallas_call(
        body, name=name,
        out_shape=(pltpu.SemaphoreType.DMA((n * N_PEERS,)), pltpu.SemaphoreType.DMA((n * N_PEERS,)),
                   *[pltpu.HBM(a.shape, a.dtype) for a in arrays], jax.ShapeDtypeStruct((8, 128), F32)),
        in_specs=[HBM_SPEC] * (2 * n) + [pl.BlockSpec(memory_space=pl.ANY)],
        out_specs=(SEM_SPEC, SEM_SPEC, *[HBM_SPEC] * (2 * n), pl.BlockSpec(memory_space=pltpu.VMEM)),
        input_output_aliases={i: i + 2 for i in range(2 * n)},
        compiler_params=pltpu.CompilerParams(has_side_effects=EFFECT),
    )(*[pltpu.with_memory_space_constraint(a, pltpu.HBM) for a in arrays], after)
    return out[0], out[1], out[2:2 + n], out[2 + n:2 + 2 * n], out[-1]


def _exchange_wait(send_sems, recv_sems, srcs, lands, gather, after, name):
    n = len(srcs)

    def body(*refs):
        src_refs, land_refs = refs[:n], refs[n:2 * n]
        for out_going, arriving in _split_copies(src_refs, land_refs, gather, refs[2 * n], refs[2 * n + 1]):
            out_going.wait_send()
            arriving.wait_recv()

    arrays = list(srcs) + list(lands)
    out = pl.pallas_call(
        body, name=name, out_shape=tuple(pltpu.HBM(a.shape, a.dtype) for a in arrays),
        in_specs=[HBM_SPEC] * (2 * n) + [SEM_SPEC, SEM_SPEC, pl.BlockSpec(memory_space=pl.ANY)],
        out_specs=tuple([HBM_SPEC] * (2 * n)), input_output_aliases={i: i for i in range(2 * n)},
        compiler_params=pltpu.CompilerParams(has_side_effects=EFFECT),
    )(*arrays, send_sems, recv_sems, after)
    return out[n:]


SIBLING = 1
CHIP_PEERS = (2, 4, 6)
NEAR_PEERS = (SIBLING,) + CHIP_PEERS


def _near_copies(srcs, lands, send_sems, recv_sems):
    _, _, _, me = _my_place()
    pairs = []
    for w in range(len(srcs)):
        for i, k in enumerate(NEAR_PEERS):
            dev, idx = _peer(k)
            sems = dict(send_sem=send_sems.at[w * len(NEAR_PEERS) + i], recv_sem=recv_sems.at[w * len(NEAR_PEERS) + i],
                        device_id=dev, device_id_type=pl.DeviceIdType.MESH)
            pairs.append((pltpu.make_async_remote_copy(src_ref=srcs[w], dst_ref=lands[w].at[me], **sems),
                          pltpu.make_async_remote_copy(src_ref=srcs[w], dst_ref=lands[w].at[idx], **sems)))
    return pairs


def _forward_copies(lands, send_sems, recv_sems):
    sibling, _ = _peer(SIBLING)
    pairs = []
    for w in range(len(lands)):
        for i, k in enumerate(CHIP_PEERS):
            _, got = _peer(k)
            _, gets = _peer(k | SIBLING)
            sems = dict(send_sem=send_sems.at[w * len(CHIP_PEERS) + i], recv_sem=recv_sems.at[w * len(CHIP_PEERS) + i],
                        device_id=sibling, device_id_type=pl.DeviceIdType.MESH)
            pairs.append((pltpu.make_async_remote_copy(src_ref=lands[w].at[got], dst_ref=lands[w].at[got], **sems),
                          pltpu.make_async_remote_copy(src_ref=lands[w].at[got], dst_ref=lands[w].at[gets], **sems)))
    return pairs


def _split_call(body, name, arrays, sems_in, sems_out, after, token):
    n = len(arrays)
    out = pl.pallas_call(
        body, name=name,
        out_shape=(*[pltpu.SemaphoreType.DMA((c,)) for c in sems_out], *[pltpu.HBM(a.shape, a.dtype) for a in arrays],
                   *([jax.ShapeDtypeStruct((8, 128), F32)] if token else [])),
        in_specs=[HBM_SPEC] * n + [SEM_SPEC] * len(sems_in) + [pl.BlockSpec(memory_space=pl.ANY)],
        out_specs=(*[SEM_SPEC] * len(sems_out), *[HBM_SPEC] * n,
                   *([pl.BlockSpec(memory_space=pltpu.VMEM)] if token else [])),
        input_output_aliases={i: i + len(sems_out) for i in range(n)},
        compiler_params=pltpu.CompilerParams(has_side_effects=EFFECT),
    )(*[pltpu.with_memory_space_constraint(a, pltpu.HBM) for a in arrays], *sems_in, after)
    k = len(sems_out)
    return list(out[:k]), list(out[k:k + n]), (out[-1] if token else None)


def _gather_start(srcs, lands, after, name):
    n = len(srcs)

    def body(*refs):
        send1, recv1 = refs[2 * n + 1], refs[2 * n + 2]
        for out_going, _ in _near_copies(refs[:n], refs[n:2 * n], send1, recv1):
            out_going.start()
        refs[-1][...] = jnp.zeros_like(refs[-1])

    sems, arrays, token = _split_call(body, name, list(srcs) + list(lands), [], [n * len(NEAR_PEERS)] * 2, after, True)
    return sems, arrays, token


def _gather_relay(sems, arrays, after, name):
    n = len(arrays) // 2

    def body(*refs):
        send1, recv1 = refs[2 * n], refs[2 * n + 1]
        send2, recv2 = refs[2 * n + 3], refs[2 * n + 4]
        near = _near_copies(refs[:n], refs[n:2 * n], send1, recv1)
        forward = _forward_copies(refs[n:2 * n], send2, recv2)
        for w in range(n):
            for i in range(len(CHIP_PEERS)):
                near[w * len(NEAR_PEERS) + 1 + i][1].wait_recv()
                forward[w * len(CHIP_PEERS) + i][0].start()
        refs[-1][...] = jnp.zeros_like(refs[-1])

    sems2, arrays, token = _split_call(body, name, arrays, sems, [n * len(CHIP_PEERS)] * 2, after, True)
    return sems + sems2, arrays, token


def _gather_finish(sems, arrays, after, name):
    n = len(arrays) // 2

    def body(*refs):
        send1, recv1, send2, recv2 = refs[2 * n:2 * n + 4]
        near = _near_copies(refs[:n], refs[n:2 * n], send1, recv1)
        for out_going, _ in near:
            out_going.wait_send()
        for w in range(n):
            near[w * len(NEAR_PEERS)][1].wait_recv()
        for out_going, arriving in _forward_copies(refs[n:2 * n], send2, recv2):
            out_going.wait_send()
            arriving.wait_recv()

    _, arrays, _ = _split_call(body, name, arrays, sems, [], after, False)
    return arrays[n:]


def _own_slot(a, me, gather):
    mine = a if gather else lax.dynamic_index_in_dim(a, me, 0, keepdims=False)
    return lax.dynamic_update_index_in_dim(lax.empty((N_DEV,) + mine.shape, mine.dtype), mine, me, 0)


def _adamw_math(w, g, m, v):
    m2 = ADAM_B1 * m + (1.0 - ADAM_B1) * g
    v2 = ADAM_B2 * v + (1.0 - ADAM_B2) * (g * g)
    m_hat = m2 / (1.0 - ADAM_B1 ** ADAM_STEP)
    v_hat = v2 / (1.0 - ADAM_B2 ** ADAM_STEP)
    return -ADAM_LR * (m_hat / (jnp.sqrt(v_hat) + ADAM_EPS) + ADAM_WD * w), m2, v2


def _sum_adamw(parts, w, m, v, name):
    rows, cols = w.shape
    tr = rows if rows <= 512 else 256

    def body(p_ref, w_ref, m_ref, v_ref, g_ref, d_ref, m2_ref, v2_ref):
        g = p_ref[0].astype(F32)
        for s in range(1, N_DEV):
            g = g + p_ref[s].astype(F32)
        g_ref[...] = g
        d_ref[...], m2_ref[...], v2_ref[...] = _adamw_math(w_ref[...], g, m_ref[...], v_ref[...])

    blk = pl.BlockSpec((tr, cols), lambda i: (i, 0))
    shape = jax.ShapeDtypeStruct((rows, cols), F32)
    return pl.pallas_call(
        body, out_shape=[shape] * 4, grid=(rows // tr,),
        in_specs=[pl.BlockSpec((N_DEV, tr, cols), lambda i: (0, i, 0)), blk, blk, blk], out_specs=[blk] * 4, name=name,
        compiler_params=_params(1))(parts, w, m, v)


def _slot_sum(parts, name):
    def body(p_ref, g_ref):
        g = p_ref[0]
        for s in range(1, N_DEV):
            g = g + p_ref[s]
        g_ref[...] = g

    return pl.pallas_call(body, out_shape=jax.ShapeDtypeStruct(parts.shape[1:], F32), name=name)(parts)


def _adamw(g, w, m, v, name):
    def body(g_ref, w_ref, m_ref, v_ref, d_ref, m2_ref, v2_ref):
        d_ref[...], m2_ref[...], v2_ref[...] = _adamw_math(w_ref[...], g_ref[...], m_ref[...], v_ref[...])

    shape = jax.ShapeDtypeStruct(w.shape, F32)
    return pl.pallas_call(body, out_shape=[shape] * 3, name=name)(g, w, m, v)


def _pack_small(vals, last=None):
    flat = jnp.concatenate([vals[k].reshape(-1) for k in SMALL])
    tail = jnp.zeros((SMALL_ROWS * D_MODEL - flat.shape[0] - 1,), F32)
    last = jnp.zeros((1,), F32) if last is None else last.reshape(1)
    return jnp.concatenate([flat, tail, last]).reshape(SMALL_ROWS, D_MODEL)


def _unpack_small(packed, like):
    flat, out, at = packed.reshape(-1), {}, 0
    for k in SMALL:
        size = like[k].size
        out[k] = flat[at:at + size].reshape(like[k].shape)
        at += size
    return out


FFN1 = ["w_ffn1_gate", "w_ffn1_up", "w_ffn1_down"]
FFN2 = ["w_ffn2_gate", "w_ffn2_up", "w_ffn2_down"]
MIXER = ["w_in", "w_mem_kv", "w_out"]
TRANSPOSED = ["w_ffn1_gate", "w_ffn1_up", "w_ffn2_gate", "w_ffn2_up", "w_in"]


def _mixer_fwd(x1, mem, tables, sm, big, after_attn=None):
    cos_t, sin_t, gq2, gk2, gmq2, gmk2 = tables
    h2, proj = _in_proj_fwd(x1, sm["g_mix"], big["w_in"], "in_proj_fwd")
    ya, lse = _attn_fwd(proj, cos_t, sin_t, gq2, gk2, sm["sinks"], "attn_fwd")
    token = None if after_attn is None else after_attn(ya)
    yc, cv = _conv_fwd(proj, big["w_dw"], sm["b_dw"], sm["g_conv_ln"], sm["b_conv_ln"], "conv_fwd")
    hm, mkv = _mem_kv_fwd(mem, sm["g_mem"], big["w_mem_kv"], "mem_kv_fwd")
    ym, lse_m = _mem_attn_fwd(proj, mkv, gmq2, gmk2, "mem_attn_fwd")
    x2 = _out_proj_fwd(x1, ya, yc, ym, big["w_out"], "out_proj_fwd", token)
    return x2, (h2, proj, ya, lse, yc, cv, hm, mkv, ym, lse_m)


def _mixer_bwd(dx2, x1, mem, tables, sm, big, saved, after):
    cos_t, sin_t, gq2, gk2, gmq2, gmk2 = tables
    h2, proj, ya, lse, yc, cv, hm, mkv, ym, lse_m = saved
    g = {}
    dmix, g["w_out"] = _out_proj_bwd(dx2, ya, yc, ym, big["w_out"], "out_proj_bwd", after)
    dq, dgq, g["sinks"] = _attn_bwd_q(proj, cos_t, sin_t, gq2, gk2, sm["sinks"], ya, dmix, lse, "attn_bwd_q")
    dk, dv, dgk = _attn_bwd_kv(proj, cos_t, sin_t, gq2, gk2, ya, dmix, lse, "attn_bwd_kv")
    du, g["w_dw"], g["b_dw"], g["g_conv_ln"], g["b_conv_ln"] = _conv_bwd(
        proj, cv, dmix, big["w_dw"], sm["g_conv_ln"], sm["b_conv_ln"], "conv_bwd")
    dmq, dmkv, dgmq, dgmk = _mem_attn_bwd(proj, mkv, gmq2, gmk2, ym, dmix, lse_m, "mem_attn_bwd")
    g["w_mem_kv"], g["g_mem"] = _mem_kv_bwd(mem, sm["g_mem"], hm, big["w_mem_kv"], dmkv, "mem_kv_bwd")
    dx1, g["g_mix"], g["w_in"] = _in_proj_bwd(dq, dk, dv, du, dmq, big["w_in"], h2, x1, sm["g_mix"], dx2, "in_proj_bwd")
    fold = lambda a: a[:, :HEAD_DIM] + a[:, HEAD_DIM:]
    g["g_q"], g["g_k"], g["g_mq"], g["g_mk"] = fold(dgq), fold(dgk), fold(dgmq), fold(dgmk)
    return dx1, g


def _tables(positions, sm):
    pair = lambda a: jnp.tile(a, (1, 2))
    return _rope_tables(positions) + (pair(sm["g_q"]), pair(sm["g_k"]), pair(sm["g_mq"]), pair(sm["g_mk"]))


def kernel(x, mem, positions, g_ffn1, w_ffn1_gate, w_ffn1_up, w_ffn1_down, g_mix, w_in, g_q, g_k, sinks, w_dw, b_dw, g_conv_ln, b_conv_ln, g_mem, w_mem_kv, g_mq, g_mk, w_out, g_ffn2, w_ffn2_gate, w_ffn2_up, w_ffn2_down, loss_target, m_g_ffn1, m_w_ffn1_gate, m_w_ffn1_up, m_w_ffn1_down, m_g_mix, m_w_in, m_g_q, m_g_k, m_sinks, m_w_dw, m_b_dw, m_g_conv_ln, m_b_conv_ln, m_g_mem, m_w_mem_kv, m_g_mq, m_g_mk, m_w_out, m_g_ffn2, m_w_ffn2_gate, m_w_ffn2_up, m_w_ffn2_down, v_g_ffn1, v_w_ffn1_gate, v_w_ffn1_up, v_w_ffn1_down, v_g_mix, v_w_in, v_g_q, v_g_k, v_sinks, v_w_dw, v_b_dw, v_g_conv_ln, v_b_conv_ln, v_g_mem, v_w_mem_kv, v_g_mq, v_g_mk, v_w_out, v_g_ffn2, v_w_ffn2_gate, v_w_ffn2_up, v_w_ffn2_down):
    w = dict(g_ffn1=g_ffn1, w_ffn1_gate=w_ffn1_gate, w_ffn1_up=w_ffn1_up, w_ffn1_down=w_ffn1_down, g_mix=g_mix, w_in=w_in,
             g_q=g_q, g_k=g_k, sinks=sinks, w_dw=w_dw, b_dw=b_dw, g_conv_ln=g_conv_ln, b_conv_ln=b_conv_ln, g_mem=g_mem,
             w_mem_kv=w_mem_kv, g_mq=g_mq, g_mk=g_mk, w_out=w_out, g_ffn2=g_ffn2, w_ffn2_gate=w_ffn2_gate,
             w_ffn2_up=w_ffn2_up, w_ffn2_down=w_ffn2_down)
    mo = dict(g_ffn1=m_g_ffn1, w_ffn1_gate=m_w_ffn1_gate, w_ffn1_up=m_w_ffn1_up, w_ffn1_down=m_w_ffn1_down, g_mix=m_g_mix,
              w_in=m_w_in, g_q=m_g_q, g_k=m_g_k, sinks=m_sinks, w_dw=m_w_dw, b_dw=m_b_dw, g_conv_ln=m_g_conv_ln,
              b_conv_ln=m_b_conv_ln, g_mem=m_g_mem, w_mem_kv=m_w_mem_kv, g_mq=m_g_mq, g_mk=m_g_mk, w_out=m_w_out,
              g_ffn2=m_g_ffn2, w_ffn2_gate=m_w_ffn2_gate, w_ffn2_up=m_w_ffn2_up, w_ffn2_down=m_w_ffn2_down)
    vo = dict(g_ffn1=v_g_ffn1, w_ffn1_gate=v_w_ffn1_gate, w_ffn1_up=v_w_ffn1_up, w_ffn1_down=v_w_ffn1_down, g_mix=v_g_mix,
              w_in=v_w_in, g_q=v_g_q, g_k=v_g_k, sinks=v_sinks, w_dw=v_w_dw, b_dw=v_b_dw, g_conv_ln=v_g_conv_ln,
              b_conv_ln=v_b_conv_ln, g_mem=v_g_mem, w_mem_kv=v_w_mem_kv, g_mq=v_g_mq, g_mk=v_g_mk, w_out=v_w_out,
              g_ffn2=v_g_ffn2, w_ffn2_gate=v_w_ffn2_gate, w_ffn2_up=v_w_ffn2_up, w_ffn2_down=v_w_ffn2_down)
    me = _my_place()[3]
    sm = {k: w[k] for k in SMALL}
    flip = lambda k, a: a.T if k in TRANSPOSED else a
    as_bf16 = lambda names: [flip(k, w[k][0]).astype(BF) for k in names]
    zones = lambda arrays, gather: [_own_slot(a, me, gather) for a in arrays]
    out_g, out_d, out_m, out_v = {}, {}, {}, {}

    def update(names, parts_list):
        for k, parts in zip(names, parts_list):
            new = _sum_adamw(parts, flip(k, w[k][0]), flip(k, mo[k][0]), flip(k, vo[k][0]), "adamw_" + k)
            out_g[k], out_d[k], out_m[k], out_v[k] = [flip(k, a)[None] for a in new]

    w1 = _gather_two_level(as_bf16(FFN1), "gather_ffn1")
    mix_src = as_bf16(MIXER) + [w["w_dw"][0]]
    mix_sems, mix_arrays, mix_token = _gather_start(mix_src, zones(mix_src, True), w1[0], "gather_mixer_start")
    f2_src = as_bf16(FFN2)
    f2_sems, f2_arrays, f2_token = _gather_start(f2_src, zones(f2_src, True), mix_token, "gather_ffn2_start")

    tables = _tables(positions[0], sm)
    h1, gate1, up1, x1 = _ffn_fwd(x[0], sm["g_ffn1"], *w1, None, "ffn1_fwd", after=f2_token)
    mix_sems, mix_arrays, mix_token = _gather_relay(mix_sems, mix_arrays, x1, "gather_mixer_relay")
    got = _gather_finish(mix_sems, mix_arrays, mix_token, "gather_mixer_finish")
    big = dict(w_in=got[0].reshape(IN_COLS, D_MODEL), w_mem_kv=got[1].reshape(D_MODEL, 2 * MQ_COLS),
               w_out=got[2].reshape(D_MODEL, D_MODEL), w_dw=got[3].transpose(1, 0, 2).reshape(CONV_WIDTH, CONV_CH))
    relayed = []

    def relay_ffn2(ya):
        relayed.extend(_gather_relay(f2_sems, f2_arrays, ya, "gather_ffn2_relay"))
        return relayed[2]

    x2, saved = _mixer_fwd(x1, mem[0], tables, sm, big, relay_ffn2)
    w2 = _gather_finish(relayed[0], relayed[1], x2, "gather_ffn2_finish")
    h3, gate2, up2, dy, loss_part = _ffn_fwd(x2, sm["g_ffn2"], *w2, loss_target[0], "ffn2_fwd")

    grads = {}
    dyb2, act2, dgate2, dup2, dx2, grads["g_ffn2"] = _ffn_bwd_act(dy, x2, sm["g_ffn2"], gate2, up2, *w2, "ffn2_bwd_act")
    g_f2 = list(_ffn_bwd_w(h3, dyb2, act2, dgate2, dup2, "ffn2_bwd_w"))
    r_f2 = _exchange_start(g_f2, zones(g_f2, False), [False] * 3, dx2, "scatter_ffn2_start")
    dx1, g_mid = _mixer_bwd(dx2, x1, mem[0], tables, sm, big, saved, r_f2[4])
    grads.update(g_mid)
    g_mix = [g_mid["w_in"].reshape(N_DEV, IN_COLS // N_DEV, D_MODEL),
             g_mid["w_mem_kv"].reshape(N_DEV, D_MODEL // N_DEV, 2 * MQ_COLS),
             g_mid["w_out"].reshape(N_DEV, D_MODEL // N_DEV, D_MODEL)]
    r_mix = _exchange_start(g_mix, zones(g_mix, False), [False] * 3, dx1, "scatter_mixer_start")
    dyb1, act1, dgate1, dup1, grad_x, grads["g_ffn1"] = _ffn_bwd_act(dx1, x[0], sm["g_ffn1"], gate1, up1, *w1,
                                                                     "ffn1_bwd_act", after=r_mix[4])
    g_f1 = list(_ffn_bwd_w(h1, dyb1, act1, dgate1, dup1, "ffn1_bwd_w"))
    dw_flat = grads["w_dw"].reshape(-1)
    packed = jnp.concatenate([_pack_small(grads, loss_part[0, 0]),
                              jnp.pad(dw_flat, (0, SMALL_ROWS * D_MODEL - dw_flat.shape[0])).reshape(SMALL_ROWS, D_MODEL)])
    last_src, last_kind = g_f1 + [packed], [False] * 3 + [True]
    r_f1 = _exchange_start(last_src, zones(g_f1, False) + zones([packed], True), last_kind, grad_x, "scatter_ffn1_start")

    all_done = lambda names: sum(out_d[k][:, :1, :1] for k in names)
    update(FFN2, _exchange_wait(*r_f2[:4], [False] * 3, r_f1[4], "scatter_ffn2_wait"))
    update(MIXER, _exchange_wait(*r_mix[:4], [False] * 3, all_done(FFN2), "scatter_mixer_wait"))
    last = _exchange_wait(*r_f1[:4], last_kind, all_done(MIXER), "scatter_ffn1_wait")
    update(FFN1, last[:3])
    small_sum = _slot_sum(last[3], "small_grad_sum")
    loss = small_sum[SMALL_ROWS - 1, D_MODEL - 1]
    g_small = small_sum[:SMALL_ROWS]
    d, m2, v2 = _adamw(g_small, _pack_small(w), _pack_small(mo), _pack_small(vo), "adamw_small")
    for dst, val in ((out_g, g_small), (out_d, d), (out_m, m2), (out_v, v2)):
        dst.update(_unpack_small(val, sm))
    g_dw = small_sum[SMALL_ROWS:].reshape(-1)[:CONV_WIDTH * CONV_CH].reshape(CONV_WIDTH, CONV_CH)
    g_dw = lax.dynamic_slice_in_dim(g_dw, me * (CONV_CH // N_DEV), CONV_CH // N_DEV, axis=1)
    d, m2, v2 = _adamw(g_dw, w["w_dw"][0], mo["w_dw"][0], vo["w_dw"][0], "adamw_w_dw")
    out_g["w_dw"], out_d["w_dw"], out_m["w_dw"], out_v["w_dw"] = g_dw[None], d[None], m2[None], v2[None]

    return (loss, grad_x[None], *[out_g[k] for k in WEIGHTS], *[out_d[k] for k in WEIGHTS], *[out_m[k] for k in WEIGHTS],
            *[out_v[k] for k in WEIGHTS])
```

```python
import jax
import jax.numpy as jnp
from jax import lax
from jax.experimental import pallas as pl
from jax.experimental.pallas import tpu as pltpu

D_MODEL = 1024
N_DEV = 8
FF_BLOCK = 352
D_FF = N_DEV * FF_BLOCK
FF_TILE = 256
N_FF_TILES = D_FF // FF_TILE
FF_ROWS = 1024
FF_CHUNK = 256
HEAD_DIM = 64
PAIR = 2 * HEAD_DIM
N_Q_HEADS = 8
Q_PER_KV = 4
ATTN_BLOCK = 128
Q_COLS = 512
KV_COLS = 128
CONV_CH = 256
MQ_COLS = 256
IN_COLS = 1536
CONV_WIDTH = 31
CONV_PAD = 32
CONV_CHUNK = 256
N_MEM_HEADS = 4
ROPE_THETA = 500000.0
ROPE_HALF = 8
EPS = 1e-6
SCALE = HEAD_DIM ** -0.5
NEG = -1e30
ADAM_LR, ADAM_B1, ADAM_B2, ADAM_EPS, ADAM_WD, ADAM_STEP = 0.001, 0.9, 0.999, 1e-08, 0.01, 10
VMEM_LIMIT_BYTES = 56 * 1024 * 1024
BF = jnp.bfloat16
F32 = jnp.float32

SMALL = ["g_ffn1", "g_mix", "g_mem", "g_ffn2", "g_q", "g_k", "g_mq", "g_mk", "sinks", "b_dw", "g_conv_ln", "b_conv_ln"]
WEIGHTS = ["g_ffn1", "w_ffn1_gate", "w_ffn1_up", "w_ffn1_down", "g_mix", "w_in", "g_q", "g_k", "sinks", "w_dw", "b_dw",
           "g_conv_ln", "b_conv_ln", "g_mem", "w_mem_kv", "g_mq", "g_mk", "w_out", "g_ffn2", "w_ffn2_gate", "w_ffn2_up",
           "w_ffn2_down"]
SMALL_ROWS = 8
PACK_ROWS = 16


def _nn(a, b):
    return jnp.dot(a, b, preferred_element_type=F32)


def _nt(a, b):
    return lax.dot_general(a, b, (((1,), (1,)), ((), ())), preferred_element_type=F32)


def _tn(a, b):
    return lax.dot_general(a, b, (((0,), (0,)), ((), ())), preferred_element_type=F32)


def _params(n_grid):
    return pltpu.CompilerParams(dimension_semantics=("arbitrary",) * n_grid, vmem_limit_bytes=VMEM_LIMIT_BYTES)


def _ordered_after(body, after):
    if after is None:
        return body, (), []

    def body_after(after_ref, *refs):
        body(*refs)

    return body_after, (after,), [pl.BlockSpec(memory_space=pl.ANY)]


def _row_rms(xv):
    return lax.rsqrt(jnp.mean(xv * xv, axis=-1, keepdims=True) + EPS)


def _rms_bwd(dh, xv, r, g):
    u = dh * g
    dx = r * u - xv * (r * r * r) * jnp.mean(u * xv, axis=-1, keepdims=True)
    return dx, jnp.sum(dh * xv * r, axis=0, keepdims=True)


def _sum_all(a):
    return jnp.sum(jnp.sum(a, axis=1, keepdims=True), axis=0, keepdims=True)


def _pair_sums(vs):
    row = lax.broadcasted_iota(jnp.int32, (PAIR, PAIR), 0) >= HEAD_DIM
    col = lax.broadcasted_iota(jnp.int32, (PAIR, PAIR), 1) >= HEAD_DIM
    same_head = (row == col).astype(BF)
    stacked = jnp.concatenate(vs, axis=0) if len(vs) > 1 else vs[0]
    hi = stacked.astype(BF)
    lo = (stacked - hi.astype(F32)).astype(BF)
    n = stacked.shape[0]
    both = _nn(jnp.concatenate([hi, lo], axis=0), same_head)
    total = both[:n] + both[n:]
    out, at = [], 0
    for v in vs:
        out.append(total[at:at + v.shape[0]])
        at += v.shape[0]
    return out


def _pair_sum(v):
    return _pair_sums([v])[0]


def _bf16_terms(v, n):
    terms, rest = [], v
    for _ in range(n):
        t = rest.astype(BF)
        terms.append(t)
        rest = rest - t.astype(F32)
    return terms


def _as_rows(cols):
    k = cols.shape[1]
    eye = (lax.broadcasted_iota(jnp.int32, (k, k), 0) == lax.broadcasted_iota(jnp.int32, (k, k), 1)).astype(BF)
    return sum(_nt(eye, t) for t in _bf16_terms(cols, 3))


def _head_sums_as_rows(v):
    sel = (lax.broadcasted_iota(jnp.int32, (8, PAIR), 0)
           == (lax.broadcasted_iota(jnp.int32, (8, PAIR), 1) >= HEAD_DIM).astype(jnp.int32)).astype(BF)
    return sum(_nt(sel, t) for t in _bf16_terms(v, 2))


def _pair_rms_many(xs):
    return [lax.rsqrt(s * (1.0 / HEAD_DIM) + EPS) for s in _pair_sums([xv * xv for xv in xs])]


def _pair_rms(xv):
    return _pair_rms_many([xv])[0]


def _pair_rms_bwd_many(dxns, xs, rs, g):
    us = [dxn * g for dxn in dxns]
    sums = _pair_sums([u * xv for u, xv in zip(us, xs)])
    return [(r * u - xv * (r * r * r) * (s * (1.0 / HEAD_DIM)), jnp.sum(dxn * xv * r, axis=0, keepdims=True))
            for dxn, xv, r, u, s in zip(dxns, xs, rs, us, sums)]


def _pair_rms_bwd(dxn, xv, r, g):
    return _pair_rms_bwd_many([dxn], [xv], [r], g)[0]


def _rope_mask(shape):
    lane = lax.broadcasted_iota(jnp.int32, shape, 1)
    return ((lane & (HEAD_DIM - 1)) < 2 * ROPE_HALF).astype(F32)


def _partner(v):
    return pltpu.roll(v, ROPE_HALF, 1) + pltpu.roll(v, PAIR - ROPE_HALF, 1)


def _rope(xn, cos_t, sin_t):
    return xn * cos_t + _partner(xn * _rope_mask(xn.shape)) * sin_t


def _rope_t(d, cos_t, sin_t):
    return d * cos_t + _partner(d * sin_t) * _rope_mask(d.shape)


def _rope_tables(positions):
    inv_freq = ROPE_THETA ** (-jnp.arange(ROPE_HALF, dtype=F32) / ROPE_HALF)
    ang = positions.astype(F32)[:, None] * inv_freq
    cos, sin = jnp.cos(ang), jnp.sin(ang)
    t = positions.shape[0]
    cos_h = jnp.concatenate([cos, cos, jnp.ones((t, HEAD_DIM - 2 * ROPE_HALF), F32)], axis=1)
    sin_h = jnp.concatenate([-sin, sin, jnp.zeros((t, HEAD_DIM - 2 * ROPE_HALF), F32)], axis=1)
    return jnp.tile(cos_h, (1, 2)), jnp.tile(sin_h, (1, 2))


def _ff_rows(w):
    return w.reshape(D_FF, D_MODEL)


def _ffn_fwd(x, g, wg, wu, wd, target, name, after=None):
    t_len = x.shape[0]
    tm = min(FF_ROWS, t_len)
    with_loss = target is not None

    def body(*refs):
        if with_loss:
            x_ref, g_ref, wg_ref, wu_ref, wd_ref, t_ref, h_ref, gg_ref, uu_ref, dy_ref, loss_ref, acc = refs
        else:
            x_ref, g_ref, wg_ref, wu_ref, wd_ref, h_ref, gg_ref, uu_ref, xo_ref, acc = refs
        t, j = pl.program_id(0), pl.program_id(1)

        @pl.when(j == 0)
        def _():
            xv = x_ref[...]
            h_ref[...] = (xv * _row_rms(xv) * g_ref[...]).astype(BF)
            acc[...] = jnp.zeros_like(acc)

        h = h_ref[...]
        gate = _nt(h, wg_ref[...])
        up = _nt(h, wu_ref[...])
        gg_ref[...] = gate.astype(BF)
        uu_ref[...] = up.astype(BF)
        act = (gate * jax.nn.sigmoid(gate) * up).astype(BF)
        acc[...] += _nn(act, wd_ref[...])

        @pl.when(j == N_FF_TILES - 1)
        def _():
            xo = x_ref[...] + 0.5 * acc[...]
            if with_loss:
                err = xo - t_ref[...]
                dy_ref[...] = err * (1.0 / D_MODEL)

                @pl.when(t == 0)
                def _():
                    loss_ref[...] = jnp.zeros_like(loss_ref)

                loss_ref[...] += _sum_all(err * err) * (0.5 / D_MODEL)
            else:
                xo_ref[...] = xo

    row = pl.BlockSpec((tm, D_MODEL), lambda t, j: (t, 0))
    vec = pl.BlockSpec((1, D_MODEL), lambda t, j: (0, 0))
    w_spec = pl.BlockSpec((FF_TILE, D_MODEL), lambda t, j: (j, 0))
    blk = pl.BlockSpec((tm, FF_TILE), lambda t, j: (t, j))
    in_specs = [row, vec, w_spec, w_spec, w_spec] + ([row] if with_loss else [])
    out_shape = [jax.ShapeDtypeStruct((t_len, D_MODEL), BF),
                 jax.ShapeDtypeStruct((t_len, D_FF), BF),
                 jax.ShapeDtypeStruct((t_len, D_FF), BF),
                 jax.ShapeDtypeStruct((t_len, D_MODEL), F32)]
    out_specs = [row, blk, blk, row]
    if with_loss:
        out_shape.append(jax.ShapeDtypeStruct((1, 128), F32))
        out_specs.append(pl.BlockSpec((1, 128), lambda t, j: (0, 0)))
    args = (x, g, _ff_rows(wg), _ff_rows(wu), _ff_rows(wd)) + ((target,) if with_loss else ())
    body, first, first_specs = _ordered_after(body, after)
    return pl.pallas_call(body, out_shape=out_shape, grid=(t_len // tm, N_FF_TILES), in_specs=first_specs + in_specs,
                          out_specs=out_specs, scratch_shapes=[pltpu.VMEM((tm, D_MODEL), F32)], name=name,
                          compiler_params=_params(2))(*first, *args)


def _ffn_bwd_act(dy, x, g, gate, up, wg, wu, wd, name, after=None):
    t_len = x.shape[0]
    tm = min(FF_ROWS, t_len)

    def body(dy_ref, x_ref, g_ref, gg_ref, uu_ref, wg_ref, wu_ref, wd_ref,
             dyb_ref, act_ref, dgg_ref, duu_ref, dx_ref, dg_ref, acc):
        t, j = pl.program_id(0), pl.program_id(1)

        @pl.when(j == 0)
        def _():
            dyb_ref[...] = (0.5 * dy_ref[...]).astype(BF)
            acc[...] = jnp.zeros_like(acc)

        w_both = jnp.concatenate([wg_ref[...], wu_ref[...]], axis=0)
        for c in range(tm // FF_CHUNK):
            rows = slice(c * FF_CHUNK, (c + 1) * FF_CHUNK)
            d_act = _nt(dyb_ref[rows, :], wd_ref[...])
            gate = gg_ref[rows, :].astype(F32)
            upv = uu_ref[rows, :].astype(F32)
            sig = jax.nn.sigmoid(gate)
            silu = gate * sig
            d_up = (d_act * silu).astype(BF)
            d_gate = (d_act * upv * (sig * (1.0 + gate * (1.0 - sig)))).astype(BF)
            act_ref[rows, :] = (silu * upv).astype(BF)
            dgg_ref[rows, :] = d_gate
            duu_ref[rows, :] = d_up
            acc[rows, :] += _nn(jnp.concatenate([d_gate, d_up], axis=1), w_both)

        @pl.when(j == N_FF_TILES - 1)
        def _():
            xv = x_ref[...]
            dx, dg = _rms_bwd(acc[...], xv, _row_rms(xv), g_ref[...])
            dx_ref[...] = dy_ref[...] + dx

            @pl.when(t == 0)
            def _():
                dg_ref[...] = jnp.zeros_like(dg_ref)

            dg_ref[...] += dg

    row = pl.BlockSpec((tm, D_MODEL), lambda t, j: (t, 0))
    vec = pl.BlockSpec((1, D_MODEL), lambda t, j: (0, 0))
    w_spec = pl.BlockSpec((FF_TILE, D_MODEL), lambda t, j: (j, 0))
    blk = pl.BlockSpec((tm, FF_TILE), lambda t, j: (t, j))
    blk_shape = jax.ShapeDtypeStruct((t_len, D_FF), BF)
    body, first, first_specs = _ordered_after(body, after)
    return pl.pallas_call(
        body,
        out_shape=[jax.ShapeDtypeStruct((t_len, D_MODEL), BF), blk_shape, blk_shape, blk_shape,
                   jax.ShapeDtypeStruct((t_len, D_MODEL), F32), jax.ShapeDtypeStruct((1, D_MODEL), F32)],
        grid=(t_len // tm, N_FF_TILES), in_specs=first_specs + [row, row, vec, blk, blk, w_spec, w_spec, w_spec],
        out_specs=[row, blk, blk, blk, row, vec], scratch_shapes=[pltpu.VMEM((tm, D_MODEL), F32)], name=name,
        compiler_params=_params(2))(*first, dy, x, g, gate, up, _ff_rows(wg), _ff_rows(wu), _ff_rows(wd))


def _ffn_bwd_w(h, dyb, act, d_gate, d_up, name):
    t_len = h.shape[0]
    tm = min(FF_ROWS, t_len)
    nt = t_len // tm

    def body(h_ref, dyb_ref, act_ref, dgg_ref, duu_ref, dwg_ref, dwu_ref, dwd_ref, ag, au, ad):
        t = pl.program_id(1)

        @pl.when(t == 0)
        def _():
            ag[...] = jnp.zeros_like(ag)
            au[...] = jnp.zeros_like(au)
            ad[...] = jnp.zeros_like(ad)

        h = h_ref[...]
        ag[...] += _tn(dgg_ref[...], h)
        au[...] += _tn(duu_ref[...], h)
        ad[...] += _tn(act_ref[...], dyb_ref[...])

        @pl.when(t == nt - 1)
        def _():
            dwg_ref[...] = ag[...].astype(BF)
            dwu_ref[...] = au[...].astype(BF)
            dwd_ref[...] = ad[...].astype(BF)

    row = pl.BlockSpec((tm, D_MODEL), lambda j, t: (t, 0))
    blk = pl.BlockSpec((tm, FF_TILE), lambda j, t: (t, j))
    w_spec = pl.BlockSpec((FF_TILE, D_MODEL), lambda j, t: (j, 0))
    grads = pl.pallas_call(
        body,
        out_shape=[jax.ShapeDtypeStruct((D_FF, D_MODEL), BF)] * 3,
        grid=(N_FF_TILES, nt), in_specs=[row, row, blk, blk, blk], out_specs=[w_spec] * 3,
        scratch_shapes=[pltpu.VMEM((FF_TILE, D_MODEL), F32)] * 3,
        name=name, compiler_params=_params(2))(h, dyb, act, d_gate, d_up)
    return [a.reshape(N_DEV, FF_BLOCK, D_MODEL) for a in grads]


def _in_proj_fwd(x, g, w_in, name):
    t_len = x.shape[0]
    tm = min(512, t_len)

    def body(x_ref, g_ref, w_ref, h_ref, p_ref):
        xv = x_ref[...]
        h = (xv * _row_rms(xv) * g_ref[...]).astype(BF)
        h_ref[...] = h
        p_ref[...] = _nt(h, w_ref[...])

    row = pl.BlockSpec((tm, D_MODEL), lambda t: (t, 0))
    return pl.pallas_call(
        body, out_shape=[jax.ShapeDtypeStruct((t_len, D_MODEL), BF), jax.ShapeDtypeStruct((t_len, IN_COLS), F32)],
        grid=(t_len // tm,),
        in_specs=[row, pl.BlockSpec((1, D_MODEL), lambda t: (0, 0)), pl.BlockSpec((IN_COLS, D_MODEL), lambda t: (0, 0))],
        out_specs=[row, pl.BlockSpec((tm, IN_COLS), lambda t: (t, 0))], name=name, compiler_params=_params(1))(x, g, w_in)


def _in_proj_bwd(dq, dk, dv, du, dmq, w_in, h, x, g, dres, name):
    t_len = x.shape[0]
    tm = min(512, t_len)
    nt = t_len // tm
    groups = [(0, Q_COLS), (Q_COLS, KV_COLS), (Q_COLS + KV_COLS, KV_COLS), (Q_COLS + 2 * KV_COLS, 2 * CONV_CH),
              (Q_COLS + 2 * KV_COLS + 2 * CONV_CH, MQ_COLS)]

    def body(dq_ref, dk_ref, dv_ref, du_ref, dmq_ref, w_ref, h_ref, x_ref, g_ref, dres_ref, dx_ref, dg_ref, dw_ref, acc):
        t = pl.program_id(0)

        @pl.when(t == 0)
        def _():
            acc[...] = jnp.zeros_like(acc)
            dg_ref[...] = jnp.zeros_like(dg_ref)

        h = h_ref[...]
        dh = jnp.zeros((tm, D_MODEL), F32)
        for (start, width), ref in zip(groups, (dq_ref, dk_ref, dv_ref, du_ref, dmq_ref)):
            piece = ref[...]
            dh = dh + _nn(piece, w_ref[start:start + width, :])
            acc[start:start + width, :] += _tn(piece, h)
        xv = x_ref[...]
        dx, dg = _rms_bwd(dh, xv, _row_rms(xv), g_ref[...])
        dx_ref[...] = dres_ref[...] + dx
        dg_ref[...] += dg

        @pl.when(t == nt - 1)
        def _():
            dw_ref[...] = acc[...].astype(BF)

    def cols(width):
        return pl.BlockSpec((tm, width), lambda t: (t, 0))

    row = cols(D_MODEL)
    vec = pl.BlockSpec((1, D_MODEL), lambda t: (0, 0))
    full = pl.BlockSpec((IN_COLS, D_MODEL), lambda t: (0, 0))
    return pl.pallas_call(
        body,
        out_shape=[jax.ShapeDtypeStruct((t_len, D_MODEL), F32), jax.ShapeDtypeStruct((1, D_MODEL), F32),
                   jax.ShapeDtypeStruct((IN_COLS, D_MODEL), BF)],
        grid=(nt,),
        in_specs=[cols(Q_COLS), cols(KV_COLS), cols(KV_COLS), cols(2 * CONV_CH), cols(MQ_COLS), full, row, row, vec, row],
        out_specs=[row, vec, full], scratch_shapes=[pltpu.VMEM((IN_COLS, D_MODEL), F32)], name=name,
        compiler_params=_params(1))(dq, dk, dv, du, dmq, w_in, h, x, g, dres)


def _attn_fwd(proj, cos_t, sin_t, gq2, gk2, sinks, name):
    t_len = proj.shape[0]
    nb = t_len // ATTN_BLOCK

    def body(q_ref, kvc_ref, kvp_ref, cq_ref, sq_ref, cp_ref, sp_ref, gq_ref, gk_ref, sk_ref, ya_ref, lse_ref):
        n = pl.program_id(0)
        cq, sq = cq_ref[...], sq_ref[...]
        c_all = jnp.concatenate([cp_ref[...], cq], axis=0)
        s_all = jnp.concatenate([sp_ref[...], sq], axis=0)
        kv = jnp.concatenate([kvp_ref[...], kvc_ref[...]], axis=0)
        k_raw, v = kv[:, :PAIR], kv[:, PAIR:]
        q_raws = [q_ref[:, pr * PAIR:(pr + 1) * PAIR] for pr in range(N_Q_HEADS // 2)]
        rms = _pair_rms_many(q_raws + [k_raw])
        kn = _rope(k_raw * rms[-1] * gk_ref[...], c_all, s_all)
        row = lax.broadcasted_iota(jnp.int32, (ATTN_BLOCK, 2 * ATTN_BLOCK), 0)
        col = lax.broadcasted_iota(jnp.int32, (ATTN_BLOCK, 2 * ATTN_BLOCK), 1)
        rel = row + ATTN_BLOCK - col
        valid = (rel >= 0) & (rel < ATTN_BLOCK) & ((col >= ATTN_BLOCK) | (n > 0))
        for pr in range(N_Q_HEADS // 2):
            qn = _rope(q_raws[pr] * rms[pr] * gq_ref[...], cq, sq)
            kvh = (2 * pr) // Q_PER_KV
            kh = kn[:, kvh * HEAD_DIM:(kvh + 1) * HEAD_DIM].astype(BF)
            v_ones = jnp.concatenate([v[:, kvh * HEAD_DIM:(kvh + 1) * HEAD_DIM],
                                      jnp.ones((2 * ATTN_BLOCK, HEAD_DIM), F32)], axis=1).astype(BF)
            for hh in range(2):
                hd = 2 * pr + hh
                qh = qn[:, hh * HEAD_DIM:(hh + 1) * HEAD_DIM].astype(BF)
                s = jnp.where(valid, _nt(qh, kh) * SCALE, NEG)
                sink = sk_ref[0:1, hd:hd + 1]
                m = jnp.maximum(jnp.max(s, axis=-1, keepdims=True), sink)
                ov = _nn(jnp.exp(s - m).astype(BF), v_ones)
                den = ov[:, HEAD_DIM:HEAD_DIM + 1] + jnp.exp(sink - m)
                ya_ref[:, hd * HEAD_DIM:(hd + 1) * HEAD_DIM] = (ov[:, :HEAD_DIM] / den).astype(BF)
                lse_ref[:, hd:hd + 1] = m + jnp.log(den)

    prev = lambda n: (jnp.maximum(n - 1, 0), 0)
    tab = pl.BlockSpec((ATTN_BLOCK, PAIR), lambda n: (n, 0))
    tab_p = pl.BlockSpec((ATTN_BLOCK, PAIR), prev)
    one = lambda w: pl.BlockSpec((1, w), lambda n: (0, 0))
    return pl.pallas_call(
        body, out_shape=[jax.ShapeDtypeStruct((t_len, Q_COLS), BF), jax.ShapeDtypeStruct((t_len, N_Q_HEADS), F32)],
        grid=(nb,),
        in_specs=[pl.BlockSpec((ATTN_BLOCK, Q_COLS), lambda n: (n, 0)),
                  pl.BlockSpec((ATTN_BLOCK, 2 * KV_COLS), lambda n: (n, 2)),
                  pl.BlockSpec((ATTN_BLOCK, 2 * KV_COLS), lambda n: (jnp.maximum(n - 1, 0), 2)),
                  tab, tab, tab_p, tab_p, one(PAIR), one(PAIR), one(N_Q_HEADS)],
        out_specs=[pl.BlockSpec((ATTN_BLOCK, Q_COLS), lambda n: (n, 0)),
                   pl.BlockSpec((ATTN_BLOCK, N_Q_HEADS), lambda n: (n, 0))],
        name=name, compiler_params=_params(1))(proj, proj, proj, cos_t, sin_t, cos_t, sin_t, gq2, gk2, sinks)


def _attn_bwd_q(proj, cos_t, sin_t, gq2, gk2, sinks, ya, dmix, lse, name):
    t_len = proj.shape[0]
    nb = t_len // ATTN_BLOCK

    def body(q_ref, kvc_ref, kvp_ref, cq_ref, sq_ref, cp_ref, sp_ref, gq_ref, gk_ref, sk_ref, ya_ref, do_ref, lse_ref,
             dq_ref, dgq_ref, dsk_ref, scr):
        n = pl.program_id(0)

        @pl.when(n == 0)
        def _():
            dgq_ref[...] = jnp.zeros_like(dgq_ref)
            dsk_ref[...] = jnp.zeros_like(dsk_ref)

        cq, sq = cq_ref[...], sq_ref[...]
        c_all = jnp.concatenate([cp_ref[...], cq], axis=0)
        s_all = jnp.concatenate([sp_ref[...], sq], axis=0)
        kv = jnp.concatenate([kvp_ref[...], kvc_ref[...]], axis=0)
        k_raw, v = kv[:, :PAIR], kv[:, PAIR:]
        n_pairs = N_Q_HEADS // 2
        pair_lanes = [slice(pr * PAIR, (pr + 1) * PAIR) for pr in range(n_pairs)]
        q_raws = [q_ref[:, lanes] for lanes in pair_lanes]
        rms = _pair_rms_many(q_raws + [k_raw])
        dsums = _pair_sums([do_ref[:, lanes] * ya_ref[:, lanes].astype(F32) for lanes in pair_lanes])
        kn = _rope(k_raw * rms[-1] * gk_ref[...], c_all, s_all)
        row = lax.broadcasted_iota(jnp.int32, (ATTN_BLOCK, 2 * ATTN_BLOCK), 0)
        col = lax.broadcasted_iota(jnp.int32, (ATTN_BLOCK, 2 * ATTN_BLOCK), 1)
        rel = row + ATTN_BLOCK - col
        valid = (rel >= 0) & (rel < ATTN_BLOCK) & ((col >= ATTN_BLOCK) | (n > 0))
        gq = gq_ref[...]
        for pr in range(n_pairs):
            qn = _rope(q_raws[pr] * rms[pr] * gq, cq, sq)
            kvh = (2 * pr) // Q_PER_KV
            kh = kn[:, kvh * HEAD_DIM:(kvh + 1) * HEAD_DIM].astype(BF)
            vh = v[:, kvh * HEAD_DIM:(kvh + 1) * HEAD_DIM].astype(BF)
            dsum2 = dsums[pr]
            for hh in range(2):
                hd = 2 * pr + hh
                qh = qn[:, hh * HEAD_DIM:(hh + 1) * HEAD_DIM].astype(BF)
                lse_h = lse_ref[:, hd:hd + 1]
                p = jnp.exp(jnp.where(valid, _nt(qh, kh) * SCALE, NEG) - lse_h)
                d_o = do_ref[:, hd * HEAD_DIM:(hd + 1) * HEAD_DIM]
                dsum = dsum2[:, hh * HEAD_DIM:hh * HEAD_DIM + 1]
                ds = p * (_nt(d_o.astype(BF), vh) - dsum)
                scr[pr, :, hh * HEAD_DIM:(hh + 1) * HEAD_DIM] = _nn(ds.astype(BF), kh) * SCALE
                sink = sk_ref[0:1, hd:hd + 1]
                dsk_ref[0:1, hd:hd + 1] += -jnp.sum(jnp.exp(sink - lse_h) * dsum, axis=0, keepdims=True)
        back = _pair_rms_bwd_many([_rope_t(scr[pr], cq, sq) for pr in range(n_pairs)], q_raws, rms[:n_pairs], gq)
        for lanes, (dx, dg) in zip(pair_lanes, back):
            dq_ref[:, lanes] = dx.astype(BF)
            dgq_ref[...] += dg

    prev = lambda n: (jnp.maximum(n - 1, 0), 0)
    tab = pl.BlockSpec((ATTN_BLOCK, PAIR), lambda n: (n, 0))
    tab_p = pl.BlockSpec((ATTN_BLOCK, PAIR), prev)
    one = lambda w: pl.BlockSpec((1, w), lambda n: (0, 0))
    qblk = pl.BlockSpec((ATTN_BLOCK, Q_COLS), lambda n: (n, 0))
    return pl.pallas_call(
        body,
        out_shape=[jax.ShapeDtypeStruct((t_len, Q_COLS), BF), jax.ShapeDtypeStruct((1, PAIR), F32),
                   jax.ShapeDtypeStruct((1, N_Q_HEADS), F32)],
        grid=(nb,),
        in_specs=[qblk, pl.BlockSpec((ATTN_BLOCK, 2 * KV_COLS), lambda n: (n, 2)),
                  pl.BlockSpec((ATTN_BLOCK, 2 * KV_COLS), lambda n: (jnp.maximum(n - 1, 0), 2)),
                  tab, tab, tab_p, tab_p, one(PAIR), one(PAIR), one(N_Q_HEADS), qblk, qblk,
                  pl.BlockSpec((ATTN_BLOCK, N_Q_HEADS), lambda n: (n, 0))],
        out_specs=[qblk, one(PAIR), one(N_Q_HEADS)], scratch_shapes=[pltpu.VMEM((N_Q_HEADS // 2, ATTN_BLOCK, PAIR), F32)],
        name=name, compiler_params=_params(1))(proj, proj, proj, cos_t, sin_t, cos_t, sin_t, gq2, gk2, sinks, ya, dmix, lse)


def _attn_bwd_kv(proj, cos_t, sin_t, gq2, gk2, ya, dmix, lse, name):
    t_len = proj.shape[0]
    nb = t_len // ATTN_BLOCK

    def body(kv_ref, q0_ref, q1_ref, ck_ref, sk_ref, c1_ref, s1_ref, gq_ref, gk_ref, o0_ref, o1_ref, do0_ref, do1_ref,
             l0_ref, l1_ref, dk_ref, dv_ref, dgk_ref, dkn_scr, dv_scr):
        m = pl.program_id(0)

        @pl.when(m == 0)
        def _():
            dgk_ref[...] = jnp.zeros_like(dgk_ref)

        ck, sk = ck_ref[...], sk_ref[...]
        c_all = jnp.concatenate([ck, c1_ref[...]], axis=0)
        s_all = jnp.concatenate([sk, s1_ref[...]], axis=0)
        kvv = kv_ref[...]
        k_raw, v = kvv[:, :PAIR], kvv[:, PAIR:]
        pair_lanes = [slice(pr * PAIR, (pr + 1) * PAIR) for pr in range(N_Q_HEADS // 2)]
        q_raws = [jnp.concatenate([q0_ref[:, lanes], q1_ref[:, lanes]], axis=0) for lanes in pair_lanes]
        d_os = [jnp.concatenate([do0_ref[:, lanes], do1_ref[:, lanes]], axis=0) for lanes in pair_lanes]
        rms = [_pair_rms(a) for a in q_raws + [k_raw]]
        rk = rms[-1]
        gk = gk_ref[...]
        kn = _rope(k_raw * rk * gk, ck, sk)
        key = lax.broadcasted_iota(jnp.int32, (ATTN_BLOCK, 2 * ATTN_BLOCK), 0)
        qry = lax.broadcasted_iota(jnp.int32, (ATTN_BLOCK, 2 * ATTN_BLOCK), 1)
        valid = ((qry < ATTN_BLOCK) & (qry >= key)) | ((qry >= ATTN_BLOCK) & (qry - ATTN_BLOCK < key) & (m < nb - 1))
        lse_rows = _as_rows(jnp.concatenate([l0_ref[...], l1_ref[...]], axis=0))
        d_kn, d_v = [None, None], [None, None]
        plus = lambda acc, term: term if acc is None else acc + term
        for pr in range(N_Q_HEADS // 2):
            qn = _rope(q_raws[pr] * rms[pr] * gq_ref[...], c_all, s_all)
            d_o2, lanes = d_os[pr], pair_lanes[pr]
            dsum_rows = _head_sums_as_rows(d_o2 * jnp.concatenate([o0_ref[:, lanes], o1_ref[:, lanes]], axis=0).astype(F32))
            kvh = (2 * pr) // Q_PER_KV
            khs = slice(kvh * HEAD_DIM, (kvh + 1) * HEAD_DIM)
            kh = kn[:, khs].astype(BF)
            vh = v[:, khs].astype(BF)
            for hh in range(2):
                hd = 2 * pr + hh
                hs = slice(hh * HEAD_DIM, (hh + 1) * HEAD_DIM)
                qh = qn[:, hs].astype(BF)
                d_oh = d_o2[:, hs].astype(BF)
                p = jnp.exp(jnp.where(valid, _nt(kh, qh) * SCALE, NEG) - lse_rows[hd:hd + 1, :])
                d_v[kvh] = plus(d_v[kvh], _nn(p.astype(BF), d_oh))
                ds = p * (_nt(vh, d_oh) - dsum_rows[hh:hh + 1, :])
                d_kn[kvh] = plus(d_kn[kvh], _nn(ds.astype(BF), qh))
        for kvh in range(2):
            khs = slice(kvh * HEAD_DIM, (kvh + 1) * HEAD_DIM)
            dkn_scr[:, khs] = d_kn[kvh] * SCALE
            dv_scr[:, khs] = d_v[kvh]
        dx, dg = _pair_rms_bwd(_rope_t(dkn_scr[...], ck, sk), k_raw, rk, gk)
        dk_ref[...] = dx.astype(BF)
        dv_ref[...] = dv_scr[...].astype(BF)
        dgk_ref[...] += dg

    nxt = lambda m: (jnp.minimum(m + 1, nb - 1), 0)
    cur = lambda m: (m, 0)
    tab = lambda f: pl.BlockSpec((ATTN_BLOCK, PAIR), f)
    qb = lambda f: pl.BlockSpec((ATTN_BLOCK, Q_COLS), f)
    lb = lambda f: pl.BlockSpec((ATTN_BLOCK, N_Q_HEADS), f)
    one = pl.BlockSpec((1, PAIR), lambda m: (0, 0))
    return pl.pallas_call(
        body,
        out_shape=[jax.ShapeDtypeStruct((t_len, KV_COLS), BF), jax.ShapeDtypeStruct((t_len, KV_COLS), BF),
                   jax.ShapeDtypeStruct((1, PAIR), F32)],
        grid=(nb,),
        in_specs=[pl.BlockSpec((ATTN_BLOCK, 2 * KV_COLS), lambda m: (m, 2)), qb(cur), qb(nxt), tab(cur), tab(cur), tab(nxt),
                  tab(nxt), one, one, qb(cur), qb(nxt), qb(cur), qb(nxt), lb(cur), lb(nxt)],
        out_specs=[tab(cur), tab(cur), one],
        scratch_shapes=[pltpu.VMEM((ATTN_BLOCK, PAIR), F32), pltpu.VMEM((ATTN_BLOCK, PAIR), F32)],
        name=name, compiler_params=_params(1))(proj, proj, proj, cos_t, sin_t, cos_t, sin_t, gq2, gk2, ya, ya, dmix, dmix,
                                               lse, lse)


def _conv_taps(blk, w, offset_of):
    acc = jnp.zeros((CONV_CHUNK, CONV_CH), F32)
    for k in range(CONV_WIDTH):
        o = offset_of(k)
        acc = acc + w[k:k + 1, :] * blk[o:o + CONV_CHUNK, :]
    return acc


def _conv_fwd(proj, w_dw, b_dw, g_ln, b_ln, name):
    t_len = proj.shape[0]
    nc = t_len // CONV_CHUNK

    def body(a_ref, gt_ref, w_ref, b_ref, g_ref, bl_ref, yc_ref, cv_ref, pad):
        pad[0:CONV_PAD, :] = jnp.zeros((CONV_PAD, CONV_CH), F32)

        def glu(c, carry):
            rows = pl.ds(pl.multiple_of(c * CONV_CHUNK, CONV_CHUNK), CONV_CHUNK)
            dst = pl.ds(pl.multiple_of(c * CONV_CHUNK + CONV_PAD, CONV_PAD), CONV_CHUNK)
            pad[dst, :] = a_ref[rows, :] * jax.nn.sigmoid(gt_ref[rows, :])
            return carry

        lax.fori_loop(0, nc, glu, 0)
        w = w_ref[...]

        def conv(c, carry):
            base = pl.multiple_of(c * CONV_CHUNK, CONV_CHUNK)
            blk = pad[pl.ds(base, CONV_CHUNK + CONV_PAD), :]
            cv = _conv_taps(blk, w, lambda k: k + CONV_PAD - (CONV_WIDTH - 1)) + b_ref[...]
            mu = jnp.mean(cv, axis=-1, keepdims=True)
            xc = cv - mu
            z = xc * lax.rsqrt(jnp.mean(xc * xc, axis=-1, keepdims=True) + EPS) * g_ref[...] + bl_ref[...]
            rows = pl.ds(base, CONV_CHUNK)
            cv_ref[rows, :] = cv
            yc_ref[rows, :] = (z * jax.nn.sigmoid(z)).astype(BF)
            return carry

        lax.fori_loop(0, nc, conv, 0)

    vec = pl.BlockSpec((1, CONV_CH), lambda i: (0, 0))
    full = pl.BlockSpec((t_len, CONV_CH), lambda i: (0, 0))
    return pl.pallas_call(
        body, out_shape=[jax.ShapeDtypeStruct((t_len, CONV_CH), BF), jax.ShapeDtypeStruct((t_len, CONV_CH), F32)],
        grid=(1,),
        in_specs=[pl.BlockSpec((t_len, CONV_CH), lambda i: (0, 3)), pl.BlockSpec((t_len, CONV_CH), lambda i: (0, 4)),
                  pl.BlockSpec((CONV_WIDTH, CONV_CH), lambda i: (0, 0)), vec, vec, vec],
        out_specs=[full, full], scratch_shapes=[pltpu.VMEM((t_len + CONV_PAD, CONV_CH), F32)], name=name,
        compiler_params=_params(1))(proj, proj, w_dw, b_dw, g_ln, b_ln)


def _conv_bwd(proj, cv, dmix, w_dw, g_ln, b_ln, name):
    t_len = proj.shape[0]
    nc = t_len // CONV_CHUNK

    def body(a_ref, gt_ref, cv_ref, dy_ref, w_ref, g_ref, bl_ref, du_ref, dw_ref, db_ref, dgl_ref, dbl_ref, pad, dpad):
        pad[0:CONV_PAD, :] = jnp.zeros((CONV_PAD, CONV_CH), F32)
        dpad[t_len:t_len + CONV_PAD, :] = jnp.zeros((CONV_PAD, CONV_CH), F32)
        dw_ref[...] = jnp.zeros_like(dw_ref)
        db_ref[...] = jnp.zeros_like(db_ref)
        dgl_ref[...] = jnp.zeros_like(dgl_ref)
        dbl_ref[...] = jnp.zeros_like(dbl_ref)

        def norm_bwd(c, carry):
            base = pl.multiple_of(c * CONV_CHUNK, CONV_CHUNK)
            rows = pl.ds(base, CONV_CHUNK)
            dst = pl.ds(pl.multiple_of(c * CONV_CHUNK + CONV_PAD, CONV_PAD), CONV_CHUNK)
            pad[dst, :] = a_ref[rows, :] * jax.nn.sigmoid(gt_ref[rows, :])
            cvv = cv_ref[rows, :]
            xc = cvv - jnp.mean(cvv, axis=-1, keepdims=True)
            rs = lax.rsqrt(jnp.mean(xc * xc, axis=-1, keepdims=True) + EPS)
            xhat = xc * rs
            z = xhat * g_ref[...] + bl_ref[...]
            sg = jax.nn.sigmoid(z)
            dz = dy_ref[rows, :] * (sg * (1.0 + z * (1.0 - sg)))
            dgl_ref[...] += jnp.sum(dz * xhat, axis=0, keepdims=True)
            dbl_ref[...] += jnp.sum(dz, axis=0, keepdims=True)
            dxh = dz * g_ref[...]
            dcv = rs * (dxh - jnp.mean(dxh, axis=-1, keepdims=True) - xhat * jnp.mean(dxh * xhat, axis=-1, keepdims=True))
            db_ref[...] += jnp.sum(dcv, axis=0, keepdims=True)
            dpad[rows, :] = dcv
            return carry

        lax.fori_loop(0, nc, norm_bwd, 0)
        w = w_ref[...]

        def conv_bwd(c, carry):
            base = pl.multiple_of(c * CONV_CHUNK, CONV_CHUNK)
            rows = pl.ds(base, CONV_CHUNK)
            dblk = dpad[pl.ds(base, CONV_CHUNK + CONV_PAD), :]
            dhc = _conv_taps(dblk, w, lambda k: CONV_WIDTH - 1 - k)
            a = a_ref[rows, :]
            sg = jax.nn.sigmoid(gt_ref[rows, :])
            du_ref[rows, 0:CONV_CH] = (dhc * sg).astype(BF)
            du_ref[rows, CONV_CH:2 * CONV_CH] = (dhc * a * sg * (1.0 - sg)).astype(BF)
            hblk = pad[pl.ds(base, CONV_CHUNK + CONV_PAD), :]
            dcv = dblk[0:CONV_CHUNK, :]
            for k in range(CONV_WIDTH):
                o = k + CONV_PAD - (CONV_WIDTH - 1)
                dw_ref[k:k + 1, :] += jnp.sum(dcv * hblk[o:o + CONV_CHUNK, :], axis=0, keepdims=True)
            return carry

        lax.fori_loop(0, nc, conv_bwd, 0)

    vec = pl.BlockSpec((1, CONV_CH), lambda i: (0, 0))
    full = pl.BlockSpec((t_len, CONV_CH), lambda i: (0, 0))
    wspec = pl.BlockSpec((CONV_WIDTH, CONV_CH), lambda i: (0, 0))
    vshape = jax.ShapeDtypeStruct((1, CONV_CH), F32)
    return pl.pallas_call(
        body,
        out_shape=[jax.ShapeDtypeStruct((t_len, 2 * CONV_CH), BF), jax.ShapeDtypeStruct((CONV_WIDTH, CONV_CH), F32),
                   vshape, vshape, vshape],
        grid=(1,),
        in_specs=[pl.BlockSpec((t_len, CONV_CH), lambda i: (0, 3)), pl.BlockSpec((t_len, CONV_CH), lambda i: (0, 4)), full,
                  pl.BlockSpec((t_len, CONV_CH), lambda i: (0, 2)), wspec, vec, vec],
        out_specs=[pl.BlockSpec((t_len, 2 * CONV_CH), lambda i: (0, 0)), wspec, vec, vec, vec],
        scratch_shapes=[pltpu.VMEM((t_len + CONV_PAD, CONV_CH), F32), pltpu.VMEM((t_len + CONV_PAD, CONV_CH), F32)],
        name=name, compiler_params=_params(1))(proj, proj, cv, dmix, w_dw, g_ln, b_ln)


def _mem_kv_fwd(mem, g, w, name):
    def body(m_ref, g_ref, w_ref, h_ref, kv_ref):
        mv = m_ref[...]
        h = (mv * _row_rms(mv) * g_ref[...]).astype(BF)
        h_ref[...] = h
        kv_ref[...] = _nn(h, w_ref[...])

    m_len = mem.shape[0]
    return pl.pallas_call(
        body, out_shape=[jax.ShapeDtypeStruct((m_len, D_MODEL), BF), jax.ShapeDtypeStruct((m_len, 2 * MQ_COLS), F32)],
        name=name, compiler_params=pltpu.CompilerParams(vmem_limit_bytes=VMEM_LIMIT_BYTES))(mem, g, w)


def _mem_kv_bwd(mem, g, h, w, dkv, name):
    def body(m_ref, g_ref, h_ref, w_ref, dkv_ref, dw_ref, dg_ref):
        dkv_b = dkv_ref[...].astype(BF)
        dw_ref[...] = _tn(h_ref[...], dkv_b).astype(BF)
        mv = m_ref[...]
        dg_ref[...] = jnp.sum(_nt(dkv_b, w_ref[...]) * mv * _row_rms(mv), axis=0, keepdims=True)

    return pl.pallas_call(
        body, out_shape=[jax.ShapeDtypeStruct((D_MODEL, 2 * MQ_COLS), BF), jax.ShapeDtypeStruct((1, D_MODEL), F32)],
        name=name, compiler_params=pltpu.CompilerParams(vmem_limit_bytes=VMEM_LIMIT_BYTES))(mem, g, h, w, dkv)


def _mem_attn_fwd(proj, mkv, gq2, gk2, name):
    t_len = proj.shape[0]
    tm = min(256, t_len)

    def body(q_ref, kv_ref, gq_ref, gk_ref, ym_ref, lse_ref):
        kvv = kv_ref[...]
        for pr in range(N_MEM_HEADS // 2):
            lanes = slice(pr * PAIR, (pr + 1) * PAIR)
            k_raw = kvv[:, lanes]
            kn = k_raw * _pair_rms(k_raw) * gk_ref[...]
            v = kvv[:, MQ_COLS + pr * PAIR:MQ_COLS + (pr + 1) * PAIR]
            q_raw = q_ref[:, lanes]
            qn = q_raw * _pair_rms(q_raw) * gq_ref[...]
            for hh in range(2):
                hd = 2 * pr + hh
                hs = slice(hh * HEAD_DIM, (hh + 1) * HEAD_DIM)
                s = _nt(qn[:, hs].astype(BF), kn[:, hs].astype(BF)) * SCALE
                m = jnp.max(s, axis=-1, keepdims=True)
                p = jnp.exp(s - m)
                den = jnp.sum(p, axis=-1, keepdims=True)
                ym_ref[:, hd * HEAD_DIM:(hd + 1) * HEAD_DIM] = _nn((p / den).astype(BF), v[:, hs].astype(BF)).astype(BF)
                lse_ref[:, hd:hd + 1] = m + jnp.log(den)

    m_len = mkv.shape[0]
    one = pl.BlockSpec((1, PAIR), lambda t: (0, 0))
    return pl.pallas_call(
        body, out_shape=[jax.ShapeDtypeStruct((t_len, MQ_COLS), BF), jax.ShapeDtypeStruct((t_len, N_MEM_HEADS), F32)],
        grid=(t_len // tm,),
        in_specs=[pl.BlockSpec((tm, MQ_COLS), lambda t: (t, 5)), pl.BlockSpec((m_len, 2 * MQ_COLS), lambda t: (0, 0)), one, one],
        out_specs=[pl.BlockSpec((tm, MQ_COLS), lambda t: (t, 0)), pl.BlockSpec((tm, N_MEM_HEADS), lambda t: (t, 0))],
        name=name, compiler_params=_params(1))(proj, mkv, gq2, gk2)


def _mem_attn_bwd(proj, mkv, gq2, gk2, ym, dmix, lse, name):
    t_len = proj.shape[0]
    tm = min(256, t_len)
    nt = t_len // tm
    m_len = mkv.shape[0]

    def body(q_ref, kv_ref, gq_ref, gk_ref, ym_ref, do_ref, lse_ref, dq_ref, dkv_ref, dgq_ref, dgk_ref, dkn_scr, dv_scr, scr):
        t = pl.program_id(0)

        @pl.when(t == 0)
        def _():
            dkn_scr[...] = jnp.zeros_like(dkn_scr)
            dv_scr[...] = jnp.zeros_like(dv_scr)
            dgq_ref[...] = jnp.zeros_like(dgq_ref)

        kvv = kv_ref[...]
        for pr in range(N_MEM_HEADS // 2):
            lanes = slice(pr * PAIR, (pr + 1) * PAIR)
            k_raw = kvv[:, lanes]
            kn = k_raw * _pair_rms(k_raw) * gk_ref[...]
            v = kvv[:, MQ_COLS + pr * PAIR:MQ_COLS + (pr + 1) * PAIR]
            q_raw = q_ref[:, lanes]
            rq = _pair_rms(q_raw)
            qn = q_raw * rq * gq_ref[...]
            d_o = do_ref[:, lanes]
            dsum = _pair_sum(d_o * ym_ref[:, lanes].astype(F32))
            for hh in range(2):
                hd = 2 * pr + hh
                hs = slice(hh * HEAD_DIM, (hh + 1) * HEAD_DIM)
                cols = slice(hd * HEAD_DIM, (hd + 1) * HEAD_DIM)
                qh = qn[:, hs].astype(BF)
                kh = kn[:, hs].astype(BF)
                d_oh = d_o[:, hs].astype(BF)
                p = jnp.exp(_nt(qh, kh) * SCALE - lse_ref[:, hd:hd + 1])
                dv_scr[:, cols] += _tn(p.astype(BF), d_oh)
                ds = (p * (_nt(d_oh, v[:, hs].astype(BF)) - dsum[:, hh * HEAD_DIM:hh * HEAD_DIM + 1])).astype(BF)
                scr[pr, :, hs] = _nn(ds, kh) * SCALE
                dkn_scr[:, cols] += _tn(ds, qh) * SCALE
            dx, dg = _pair_rms_bwd(scr[pr], q_raw, rq, gq_ref[...])
            dq_ref[:, lanes] = dx.astype(BF)
            dgq_ref[...] += dg

        @pl.when(t == nt - 1)
        def _():
            dgk = jnp.zeros((1, PAIR), F32)
            for pr in range(N_MEM_HEADS // 2):
                lanes = slice(pr * PAIR, (pr + 1) * PAIR)
                k_raw = kvv[:, lanes]
                dx, dg = _pair_rms_bwd(dkn_scr[:, lanes], k_raw, _pair_rms(k_raw), gk_ref[...])
                dkv_ref[:, lanes] = dx
                dgk = dgk + dg
            dkv_ref[:, MQ_COLS:2 * MQ_COLS] = dv_scr[...]
            dgk_ref[...] = dgk

    one = pl.BlockSpec((1, PAIR), lambda t: (0, 0))
    kvspec = pl.BlockSpec((m_len, 2 * MQ_COLS), lambda t: (0, 0))
    qspec = pl.BlockSpec((tm, MQ_COLS), lambda t: (t, 0))
    return pl.pallas_call(
        body,
        out_shape=[jax.ShapeDtypeStruct((t_len, MQ_COLS), BF), jax.ShapeDtypeStruct((m_len, 2 * MQ_COLS), F32),
                   jax.ShapeDtypeStruct((1, PAIR), F32), jax.ShapeDtypeStruct((1, PAIR), F32)],
        grid=(nt,),
        in_specs=[pl.BlockSpec((tm, MQ_COLS), lambda t: (t, 5)), kvspec, one, one, qspec,
                  pl.BlockSpec((tm, MQ_COLS), lambda t: (t, 3)), pl.BlockSpec((tm, N_MEM_HEADS), lambda t: (t, 0))],
        out_specs=[qspec, kvspec, one, one],
        scratch_shapes=[pltpu.VMEM((m_len, MQ_COLS), F32), pltpu.VMEM((m_len, MQ_COLS), F32),
                        pltpu.VMEM((N_MEM_HEADS // 2, tm, PAIR), F32)],
        name=name, compiler_params=_params(1))(proj, mkv, gq2, gk2, ym, dmix, lse)


MIX_GROUPS = [(0, Q_COLS), (Q_COLS, CONV_CH), (Q_COLS + CONV_CH, MQ_COLS)]


def _out_proj_fwd(x, ya, yc, ym, w_out, name, after=None):
    t_len = x.shape[0]
    tm = min(512, t_len)

    def body(x_ref, ya_ref, yc_ref, ym_ref, w_ref, xo_ref):
        y = x_ref[...]
        for (start, width), ref in zip(MIX_GROUPS, (ya_ref, yc_ref, ym_ref)):
            y = y + _nn(ref[...], w_ref[start:start + width, :])
        xo_ref[...] = y

    cols = lambda w: pl.BlockSpec((tm, w), lambda t: (t, 0))
    body, first, first_specs = _ordered_after(body, after)
    return pl.pallas_call(
        body, out_shape=jax.ShapeDtypeStruct((t_len, D_MODEL), F32), grid=(t_len // tm,),
        in_specs=first_specs + [cols(D_MODEL), cols(Q_COLS), cols(CONV_CH), cols(MQ_COLS),
                                pl.BlockSpec((D_MODEL, D_MODEL), lambda t: (0, 0))],
        out_specs=cols(D_MODEL), name=name, compiler_params=_params(1))(*first, x, ya, yc, ym, w_out)


def _out_proj_bwd(dx, ya, yc, ym, w_out, name, after=None):
    t_len = dx.shape[0]
    tm = min(512, t_len)
    nt = t_len // tm

    def body(dx_ref, ya_ref, yc_ref, ym_ref, w_ref, dmix_ref, dw_ref, acc):
        t = pl.program_id(0)

        @pl.when(t == 0)
        def _():
            acc[...] = jnp.zeros_like(acc)

        dxb = dx_ref[...].astype(BF)
        dmix_ref[...] = _nt(dxb, w_ref[...])
        for (start, width), ref in zip(MIX_GROUPS, (ya_ref, yc_ref, ym_ref)):
            acc[start:start + width, :] += _tn(ref[...], dxb)

        @pl.when(t == nt - 1)
        def _():
            dw_ref[...] = acc[...].astype(BF)

    cols = lambda w: pl.BlockSpec((tm, w), lambda t: (t, 0))
    full = pl.BlockSpec((D_MODEL, D_MODEL), lambda t: (0, 0))
    body, first, first_specs = _ordered_after(body, after)
    return pl.pallas_call(
        body, out_shape=[jax.ShapeDtypeStruct((t_len, D_MODEL), F32), jax.ShapeDtypeStruct((D_MODEL, D_MODEL), BF)],
        grid=(nt,), in_specs=first_specs + [cols(D_MODEL), cols(Q_COLS), cols(CONV_CH), cols(MQ_COLS), full],
        out_specs=[cols(D_MODEL), full], scratch_shapes=[pltpu.VMEM((D_MODEL, D_MODEL), F32)], name=name,
        compiler_params=_params(1))(*first, dx, ya, yc, ym, w_out)


N_PEERS = N_DEV - 1
HBM_SPEC = pl.BlockSpec(memory_space=pltpu.HBM)
SEM_SPEC = pl.BlockSpec(memory_space=pltpu.SEMAPHORE)
EFFECT = pltpu.SideEffectType.DATAFLOW_SIDE_EFFECTING


def _my_place():
    x, y, c = lax.axis_index("x"), lax.axis_index("y"), lax.axis_index("c")
    return x, y, c, 4 * x + 2 * y + c


def _peer(k):
    x, y, c, _ = _my_place()
    px = 1 - x if k & 4 else x
    py = 1 - y if k & 2 else y
    pc = 1 - c if k & 1 else c
    return (px, py, pc), 4 * px + 2 * py + pc


def _gather_two_level(shards, name):
    n = len(shards)

    def body(*refs):
        ins, outs = refs[:n], refs[n:2 * n]
        send_sems, recv_sems, local_sems = refs[2 * n:]
        x, y, c, me = _my_place()
        sibling = (x, y, 1 - c)
        chips = [(1 - x, y), (x, 1 - y), (1 - x, 1 - y)]
        slot = lambda px, py, pc: 4 * px + 2 * py + pc

        def copy(w, k, src, dst_slot, to):
            return pltpu.make_async_remote_copy(
                src_ref=src, dst_ref=outs[w].at[dst_slot], send_sem=send_sems.at[w * N_PEERS + k],
                recv_sem=recv_sems.at[w * N_PEERS + k], device_id=to, device_id_type=pl.DeviceIdType.MESH)

        local = [pltpu.make_async_copy(ins[w], outs[w].at[me], local_sems.at[w]) for w in range(n)]
        for cp in local:
            cp.start()
        sends = []
        for w in range(n):
            for j, chip in enumerate(chips):
                sends.append(copy(w, 1 + j, ins[w], me, (*chip, c)))
            sends.append(copy(w, 0, ins[w], me, sibling))
        for cp in sends:
            cp.start()
        for w in range(n):
            for j, chip in enumerate(chips):
                got = slot(*chip, c)
                copy(w, 1 + j, ins[w], got, (*chip, c)).wait_recv()
                fwd = copy(w, 4 + j, outs[w].at[got], got, sibling)
                fwd.start()
                sends.append(fwd)
        for w in range(n):
            copy(w, 0, ins[w], slot(x, y, 1 - c), sibling).wait_recv()
            for j, chip in enumerate(chips):
                copy(w, 4 + j, ins[w], slot(*chip, 1 - c), sibling).wait_recv()
        for cp in sends:
            cp.wait_send()
        for cp in local:
            cp.wait()

    hbm = pl.BlockSpec(memory_space=pl.ANY)
    return pl.pallas_call(
        body, out_shape=[jax.ShapeDtypeStruct((N_DEV,) + a.shape, a.dtype) for a in shards], in_specs=[hbm] * n,
        out_specs=[hbm] * n,
        scratch_shapes=[pltpu.SemaphoreType.DMA((n * N_PEERS,)), pltpu.SemaphoreType.DMA((n * N_PEERS,)),
                        pltpu.SemaphoreType.DMA((n,))],
        name=name)(*shards)


def _split_copies(srcs, lands, gather, send_sems, recv_sems):
    _, _, _, me = _my_place()
    pairs = []
    for w in range(len(srcs)):
        for k in range(1, N_DEV):
            dev, idx = _peer(k)
            src = srcs[w] if gather[w] else srcs[w].at[idx]
            sems = dict(send_sem=send_sems.at[w * N_PEERS + k - 1], recv_sem=recv_sems.at[w * N_PEERS + k - 1], device_id=dev,
                        device_id_type=pl.DeviceIdType.MESH)
            pairs.append((pltpu.make_async_remote_copy(src_ref=src, dst_ref=lands[w].at[me], **sems),
                          pltpu.make_async_remote_copy(src_ref=src, dst_ref=lands[w].at[idx], **sems)))
    return pairs


def _exchange_start(srcs, lands, gather, after, name):
    n = len(srcs)

    def body(*refs):
        src_refs, land_refs = refs[:n], refs[n:2 * n]
        send_sems, recv_sems = refs[2 * n + 1], refs[2 * n + 2]
        token = refs[-1]
        for out_going, _ in _split_copies(src_refs, land_refs, gather, send_sems, recv_sems):
            out_going.start()
        token[...] = jnp.zeros_like(token)

    arrays = list(srcs) + list(lands)
    out = pl.pallas_call(
        body, name=name,
        out_shape=(pltpu.SemaphoreType.DMA((n * N_PEERS,)), pltpu.SemaphoreType.DMA((n * N_PEERS,)),
                   *[pltpu.HBM(a.shape, a.dtype) for a in arrays], jax.ShapeDtypeStruct((8, 128), F32)),
        in_specs=[HBM_SPEC] * (2 * n) + [pl.BlockSpec(memory_space=pl.ANY)],
        out_specs=(SEM_SPEC, SEM_SPEC, *[HBM_SPEC] * (2 * n), pl.BlockSpec(memory_space=pltpu.VMEM)),
        input_output_aliases={i: i + 2 for i in range(2 * n)},
        compiler_params=pltpu.CompilerParams(has_side_effects=EFFECT),
    )(*[pltpu.with_memory_space_constraint(a, pltpu.HBM) for a in arrays], after)
    return out[0], out[1], out[2:2 + n], out[2 + n:2 + 2 * n], out[-1]


def _exchange_wait(send_sems, recv_sems, srcs, lands, gather, after, name):
    n = len(srcs)

    def body(*refs):
        src_refs, land_refs = refs[:n], refs[n:2 * n]
        for out_going, arriving in _split_copies(src_refs, land_refs, gather, refs[2 * n], refs[2 * n + 1]):
            out_going.wait_send()
            arriving.wait_recv()

    arrays = list(srcs) + list(lands)
    out = pl.pallas_call(
        body, name=name, out_shape=tuple(pltpu.HBM(a.shape, a.dtype) for a in arrays),
        in_specs=[HBM_SPEC] * (2 * n) + [SEM_SPEC, SEM_SPEC, pl.BlockSpec(memory_space=pl.ANY)],
        out_specs=tuple([HBM_SPEC] * (2 * n)), input_output_aliases={i: i for i in range(2 * n)},
        compiler_params=pltpu.CompilerParams(has_side_effects=EFFECT),
    )(*arrays, send_sems, recv_sems, after)
    return out[n:]


SIBLING = 1
CHIP_PEERS = (2, 4, 6)
NEAR_PEERS = (SIBLING,) + CHIP_PEERS


def _near_copies(srcs, lands, send_sems, recv_sems):
    _, _, _, me = _my_place()
    pairs = []
    for w in range(len(srcs)):
        for i, k in enumerate(NEAR_PEERS):
            dev, idx = _peer(k)
            sems = dict(send_sem=send_sems.at[w * len(NEAR_PEERS) + i], recv_sem=recv_sems.at[w * len(NEAR_PEERS) + i],
                        device_id=dev, device_id_type=pl.DeviceIdType.MESH)
            pairs.append((pltpu.make_async_remote_copy(src_ref=srcs[w], dst_ref=lands[w].at[me], **sems),
                          pltpu.make_async_remote_copy(src_ref=srcs[w], dst_ref=lands[w].at[idx], **sems)))
    return pairs


def _forward_copies(lands, send_sems, recv_sems):
    sibling, _ = _peer(SIBLING)
    pairs = []
    for w in range(len(lands)):
        for i, k in enumerate(CHIP_PEERS):
            _, got = _peer(k)
            _, gets = _peer(k | SIBLING)
            sems = dict(send_sem=send_sems.at[w * len(CHIP_PEERS) + i], recv_sem=recv_sems.at[w * len(CHIP_PEERS) + i],
                        device_id=sibling, device_id_type=pl.DeviceIdType.MESH)
            pairs.append((pltpu.make_async_remote_copy(src_ref=lands[w].at[got], dst_ref=lands[w].at[got], **sems),
                          pltpu.make_async_remote_copy(src_ref=lands[w].at[got], dst_ref=lands[w].at[gets], **sems)))
    return pairs


def _split_call(body, name, arrays, sems_in, sems_out, after, token):
    n = len(arrays)
    out = pl.pallas_call(
        body, name=name,
        out_shape=(*[pltpu.SemaphoreType.DMA((c,)) for c in sems_out], *[pltpu.HBM(a.shape, a.dtype) for a in arrays],
                   *([jax.ShapeDtypeStruct((8, 128), F32)] if token else [])),
        in_specs=[HBM_SPEC] * n + [SEM_SPEC] * len(sems_in) + [pl.BlockSpec(memory_space=pl.ANY)],
        out_specs=(*[SEM_SPEC] * len(sems_out), *[HBM_SPEC] * n,
                   *([pl.BlockSpec(memory_space=pltpu.VMEM)] if token else [])),
        input_output_aliases={i: i + len(sems_out) for i in range(n)},
        compiler_params=pltpu.CompilerParams(has_side_effects=EFFECT),
    )(*[pltpu.with_memory_space_constraint(a, pltpu.HBM) for a in arrays], *sems_in, after)
    k = len(sems_out)
    return list(out[:k]), list(out[k:k + n]), (out[-1] if token else None)


def _gather_start(srcs, lands, after, name):
    n = len(srcs)

    def body(*refs):
        send1, recv1 = refs[2 * n + 1], refs[2 * n + 2]
        for out_going, _ in _near_copies(refs[:n], refs[n:2 * n], send1, recv1):
            out_going.start()
        refs[-1][...] = jnp.zeros_like(refs[-1])

    sems, arrays, token = _split_call(body, name, list(srcs) + list(lands), [], [n * len(NEAR_PEERS)] * 2, after, True)
    return sems, arrays, token


def _gather_relay(sems, arrays, after, name):
    n = len(arrays) // 2

    def body(*refs):
        send1, recv1 = refs[2 * n], refs[2 * n + 1]
        send2, recv2 = refs[2 * n + 3], refs[2 * n + 4]
        near = _near_copies(refs[:n], refs[n:2 * n], send1, recv1)
        forward = _forward_copies(refs[n:2 * n], send2, recv2)
        for w in range(n):
            for i in range(len(CHIP_PEERS)):
                near[w * len(NEAR_PEERS) + 1 + i][1].wait_recv()
                forward[w * len(CHIP_PEERS) + i][0].start()
        refs[-1][...] = jnp.zeros_like(refs[-1])

    sems2, arrays, token = _split_call(body, name, arrays, sems, [n * len(CHIP_PEERS)] * 2, after, True)
    return sems + sems2, arrays, token


def _gather_finish(sems, arrays, after, name):
    n = len(arrays) // 2

    def body(*refs):
        send1, recv1, send2, recv2 = refs[2 * n:2 * n + 4]
        near = _near_copies(refs[:n], refs[n:2 * n], send1, recv1)
        for out_going, _ in near:
            out_going.wait_send()
        for w in range(n):
            near[w * len(NEAR_PEERS)][1].wait_recv()
        for out_going, arriving in _forward_copies(refs[n:2 * n], send2, recv2):
            out_going.wait_send()
            arriving.wait_recv()

    _, arrays, _ = _split_call(body, name, arrays, sems, [], after, False)
    return arrays[n:]


def _own_slot(a, me, gather):
    mine = a if gather else lax.dynamic_index_in_dim(a, me, 0, keepdims=False)
    return lax.dynamic_update_index_in_dim(lax.empty((N_DEV,) + mine.shape, mine.dtype), mine, me, 0)


def _adamw_math(w, g, m, v):
    m2 = ADAM_B1 * m + (1.0 - ADAM_B1) * g
    v2 = ADAM_B2 * v + (1.0 - ADAM_B2) * (g * g)
    m_hat = m2 / (1.0 - ADAM_B1 ** ADAM_STEP)
    v_hat = v2 / (1.0 - ADAM_B2 ** ADAM_STEP)
    return -ADAM_LR * (m_hat / (jnp.sqrt(v_hat) + ADAM_EPS) + ADAM_WD * w), m2, v2


def _sum_adamw(parts, w, m, v, name):
    rows, cols = w.shape
    tr = rows if rows <= 512 else 256

    def body(p_ref, w_ref, m_ref, v_ref, g_ref, d_ref, m2_ref, v2_ref):
        g = p_ref[0].astype(F32)
        for s in range(1, N_DEV):
            g = g + p_ref[s].astype(F32)
        g_ref[...] = g
        d_ref[...], m2_ref[...], v2_ref[...] = _adamw_math(w_ref[...], g, m_ref[...], v_ref[...])

    blk = pl.BlockSpec((tr, cols), lambda i: (i, 0))
    shape = jax.ShapeDtypeStruct((rows, cols), F32)
    return pl.pallas_call(
        body, out_shape=[shape] * 4, grid=(rows // tr,),
        in_specs=[pl.BlockSpec((N_DEV, tr, cols), lambda i: (0, i, 0)), blk, blk, blk], out_specs=[blk] * 4, name=name,
        compiler_params=_params(1))(parts, w, m, v)


def _slot_sum(parts, name):
    def body(p_ref, g_ref):
        g = p_ref[0]
        for s in range(1, N_DEV):
            g = g + p_ref[s]
        g_ref[...] = g

    return pl.pallas_call(body, out_shape=jax.ShapeDtypeStruct(parts.shape[1:], F32), name=name)(parts)


def _adamw(g, w, m, v, name):
    def body(g_ref, w_ref, m_ref, v_ref, d_ref, m2_ref, v2_ref):
        d_ref[...], m2_ref[...], v2_ref[...] = _adamw_math(w_ref[...], g_ref[...], m_ref[...], v_ref[...])

    shape = jax.ShapeDtypeStruct(w.shape, F32)
    return pl.pallas_call(body, out_shape=[shape] * 3, name=name)(g, w, m, v)


def _pack_small(vals, last=None):
    flat = jnp.concatenate([vals[k].reshape(-1) for k in SMALL])
    tail = jnp.zeros((SMALL_ROWS * D_MODEL - flat.shape[0] - 1,), F32)
    last = jnp.zeros((1,), F32) if last is None else last.reshape(1)
    return jnp.concatenate([flat, tail, last]).reshape(SMALL_ROWS, D_MODEL)


def _unpack_small(packed, like):
    flat, out, at = packed.reshape(-1), {}, 0
    for k in SMALL:
        size = like[k].size
        out[k] = flat[at:at + size].reshape(like[k].shape)
        at += size
    return out


FFN1 = ["w_ffn1_gate", "w_ffn1_up", "w_ffn1_down"]
FFN2 = ["w_ffn2_gate", "w_ffn2_up", "w_ffn2_down"]
MIXER = ["w_in", "w_mem_kv", "w_out"]
TRANSPOSED = ["w_ffn1_gate", "w_ffn1_up", "w_ffn2_gate", "w_ffn2_up", "w_in"]


def _mixer_fwd(x1, mem, tables, sm, big, after_attn=None):
    cos_t, sin_t, gq2, gk2, gmq2, gmk2 = tables
    h2, proj = _in_proj_fwd(x1, sm["g_mix"], big["w_in"], "in_proj_fwd")
    ya, lse = _attn_fwd(proj, cos_t, sin_t, gq2, gk2, sm["sinks"], "attn_fwd")
    token = None if after_attn is None else after_attn(ya)
    yc, cv = _conv_fwd(proj, big["w_dw"], sm["b_dw"], sm["g_conv_ln"], sm["b_conv_ln"], "conv_fwd")
    hm, mkv = _mem_kv_fwd(mem, sm["g_mem"], big["w_mem_kv"], "mem_kv_fwd")
    ym, lse_m = _mem_attn_fwd(proj, mkv, gmq2, gmk2, "mem_attn_fwd")
    x2 = _out_proj_fwd(x1, ya, yc, ym, big["w_out"], "out_proj_fwd", token)
    return x2, (h2, proj, ya, lse, yc, cv, hm, mkv, ym, lse_m)


def _mixer_bwd(dx2, x1, mem, tables, sm, big, saved, after):
    cos_t, sin_t, gq2, gk2, gmq2, gmk2 = tables
    h2, proj, ya, lse, yc, cv, hm, mkv, ym, lse_m = saved
    g = {}
    dmix, g["w_out"] = _out_proj_bwd(dx2, ya, yc, ym, big["w_out"], "out_proj_bwd", after)
    dq, dgq, g["sinks"] = _attn_bwd_q(proj, cos_t, sin_t, gq2, gk2, sm["sinks"], ya, dmix, lse, "attn_bwd_q")
    dk, dv, dgk = _attn_bwd_kv(proj, cos_t, sin_t, gq2, gk2, ya, dmix, lse, "attn_bwd_kv")
    du, g["w_dw"], g["b_dw"], g["g_conv_ln"], g["b_conv_ln"] = _conv_bwd(
        proj, cv, dmix, big["w_dw"], sm["g_conv_ln"], sm["b_conv_ln"], "conv_bwd")
    dmq, dmkv, dgmq, dgmk = _mem_attn_bwd(proj, mkv, gmq2, gmk2, ym, dmix, lse_m, "mem_attn_bwd")
    g["w_mem_kv"], g["g_mem"] = _mem_kv_bwd(mem, sm["g_mem"], hm, big["w_mem_kv"], dmkv, "mem_kv_bwd")
    dx1, g["g_mix"], g["w_in"] = _in_proj_bwd(dq, dk, dv, du, dmq, big["w_in"], h2, x1, sm["g_mix"], dx2, "in_proj_bwd")
    fold = lambda a: a[:, :HEAD_DIM] + a[:, HEAD_DIM:]
    g["g_q"], g["g_k"], g["g_mq"], g["g_mk"] = fold(dgq), fold(dgk), fold(dgmq), fold(dgmk)
    return dx1, g


def _tables(positions, sm):
    pair = lambda a: jnp.tile(a, (1, 2))
    return _rope_tables(positions) + (pair(sm["g_q"]), pair(sm["g_k"]), pair(sm["g_mq"]), pair(sm["g_mk"]))


def kernel(x, mem, positions, g_ffn1, w_ffn1_gate, w_ffn1_up, w_ffn1_down, g_mix, w_in, g_q, g_k, sinks, w_dw, b_dw, g_conv_ln, b_conv_ln, g_mem, w_mem_kv, g_mq, g_mk, w_out, g_ffn2, w_ffn2_gate, w_ffn2_up, w_ffn2_down, loss_target, m_g_ffn1, m_w_ffn1_gate, m_w_ffn1_up, m_w_ffn1_down, m_g_mix, m_w_in, m_g_q, m_g_k, m_sinks, m_w_dw, m_b_dw, m_g_conv_ln, m_b_conv_ln, m_g_mem, m_w_mem_kv, m_g_mq, m_g_mk, m_w_out, m_g_ffn2, m_w_ffn2_gate, m_w_ffn2_up, m_w_ffn2_down, v_g_ffn1, v_w_ffn1_gate, v_w_ffn1_up, v_w_ffn1_down, v_g_mix, v_w_in, v_g_q, v_g_k, v_sinks, v_w_dw, v_b_dw, v_g_conv_ln, v_b_conv_ln, v_g_mem, v_w_mem_kv, v_g_mq, v_g_mk, v_w_out, v_g_ffn2, v_w_ffn2_gate, v_w_ffn2_up, v_w_ffn2_down):
    w = dict(g_ffn1=g_ffn1, w_ffn1_gate=w_ffn1_gate, w_ffn1_up=w_ffn1_up, w_ffn1_down=w_ffn1_down, g_mix=g_mix, w_in=w_in,
             g_q=g_q, g_k=g_k, sinks=sinks, w_dw=w_dw, b_dw=b_dw, g_conv_ln=g_conv_ln, b_conv_ln=b_conv_ln, g_mem=g_mem,
             w_mem_kv=w_mem_kv, g_mq=g_mq, g_mk=g_mk, w_out=w_out, g_ffn2=g_ffn2, w_ffn2_gate=w_ffn2_gate,
             w_ffn2_up=w_ffn2_up, w_ffn2_down=w_ffn2_down)
    mo = dict(g_ffn1=m_g_ffn1, w_ffn1_gate=m_w_ffn1_gate, w_ffn1_up=m_w_ffn1_up, w_ffn1_down=m_w_ffn1_down, g_mix=m_g_mix,
              w_in=m_w_in, g_q=m_g_q, g_k=m_g_k, sinks=m_sinks, w_dw=m_w_dw, b_dw=m_b_dw, g_conv_ln=m_g_conv_ln,
              b_conv_ln=m_b_conv_ln, g_mem=m_g_mem, w_mem_kv=m_w_mem_kv, g_mq=m_g_mq, g_mk=m_g_mk, w_out=m_w_out,
              g_ffn2=m_g_ffn2, w_ffn2_gate=m_w_ffn2_gate, w_ffn2_up=m_w_ffn2_up, w_ffn2_down=m_w_ffn2_down)
    vo = dict(g_ffn1=v_g_ffn1, w_ffn1_gate=v_w_ffn1_gate, w_ffn1_up=v_w_ffn1_up, w_ffn1_down=v_w_ffn1_down, g_mix=v_g_mix,
              w_in=v_w_in, g_q=v_g_q, g_k=v_g_k, sinks=v_sinks, w_dw=v_w_dw, b_dw=v_b_dw, g_conv_ln=v_g_conv_ln,
              b_conv_ln=v_b_conv_ln, g_mem=v_g_mem, w_mem_kv=v_w_mem_kv, g_mq=v_g_mq, g_mk=v_g_mk, w_out=v_w_out,
              g_ffn2=v_g_ffn2, w_ffn2_gate=v_w_ffn2_gate, w_ffn2_up=v_w_ffn2_up, w_ffn2_down=v_w_ffn2_down)
    me = _my_place()[3]
    sm = {k: w[k] for k in SMALL}
    flip = lambda k, a: a.T if k in TRANSPOSED else a
    as_bf16 = lambda names: [flip(k, w[k][0]).astype(BF) for k in names]
    zones = lambda arrays, gather: [_own_slot(a, me, gather) for a in arrays]
    out_g, out_d, out_m, out_v = {}, {}, {}, {}

    def update(names, parts_list):
        for k, parts in zip(names, parts_list):
            new = _sum_adamw(parts, flip(k, w[k][0]), flip(k, mo[k][0]), flip(k, vo[k][0]), "adamw_" + k)
            out_g[k], out_d[k], out_m[k], out_v[k] = [flip(k, a)[None] for a in new]

    w1 = _gather_two_level(as_bf16(FFN1), "gather_ffn1")
    mix_src = as_bf16(MIXER) + [w["w_dw"][0]]
    mix_sems, mix_arrays, mix_token = _gather_start(mix_src, zones(mix_src, True), w1[0], "gather_mixer_start")
    f2_src = as_bf16(FFN2)
    f2_sems, f2_arrays, f2_token = _gather_start(f2_src, zones(f2_src, True), mix_token, "gather_ffn2_start")

    tables = _tables(positions[0], sm)
    h1, gate1, up1, x1 = _ffn_fwd(x[0], sm["g_ffn1"], *w1, None, "ffn1_fwd", after=f2_token)
    mix_sems, mix_arrays, mix_token = _gather_relay(mix_sems, mix_arrays, x1, "gather_mixer_relay")
    got = _gather_finish(mix_sems, mix_arrays, mix_token, "gather_mixer_finish")
    big = dict(w_in=got[0].reshape(IN_COLS, D_MODEL), w_mem_kv=got[1].reshape(D_MODEL, 2 * MQ_COLS),
               w_out=got[2].reshape(D_MODEL, D_MODEL), w_dw=got[3].transpose(1, 0, 2).reshape(CONV_WIDTH, CONV_CH))
    relayed = []

    def relay_ffn2(ya):
        relayed.extend(_gather_relay(f2_sems, f2_arrays, ya, "gather_ffn2_relay"))
        return relayed[2]

    x2, saved = _mixer_fwd(x1, mem[0], tables, sm, big, relay_ffn2)
    w2 = _gather_finish(relayed[0], relayed[1], x2, "gather_ffn2_finish")
    h3, gate2, up2, dy, loss_part = _ffn_fwd(x2, sm["g_ffn2"], *w2, loss_target[0], "ffn2_fwd")

    grads = {}
    dyb2, act2, dgate2, dup2, dx2, grads["g_ffn2"] = _ffn_bwd_act(dy, x2, sm["g_ffn2"], gate2, up2, *w2, "ffn2_bwd_act")
    g_f2 = list(_ffn_bwd_w(h3, dyb2, act2, dgate2, dup2, "ffn2_bwd_w"))
    r_f2 = _exchange_start(g_f2, zones(g_f2, False), [False] * 3, dx2, "scatter_ffn2_start")
    dx1, g_mid = _mixer_bwd(dx2, x1, mem[0], tables, sm, big, saved, r_f2[4])
    grads.update(g_mid)
    g_mix = [g_mid["w_in"].reshape(N_DEV, IN_COLS // N_DEV, D_MODEL),
             g_mid["w_mem_kv"].reshape(N_DEV, D_MODEL // N_DEV, 2 * MQ_COLS),
             g_mid["w_out"].reshape(N_DEV, D_MODEL // N_DEV, D_MODEL)]
    r_mix = _exchange_start(g_mix, zones(g_mix, False), [False] * 3, dx1, "scatter_mixer_start")
    dyb1, act1, dgate1, dup1, grad_x, grads["g_ffn1"] = _ffn_bwd_act(dx1, x[0], sm["g_ffn1"], gate1, up1, *w1,
                                                                     "ffn1_bwd_act", after=r_mix[4])
    g_f1 = list(_ffn_bwd_w(h1, dyb1, act1, dgate1, dup1, "ffn1_bwd_w"))
    dw_flat = grads["w_dw"].reshape(-1)
    packed = jnp.concatenate([_pack_small(grads, loss_part[0, 0]),
                              jnp.pad(dw_flat, (0, SMALL_ROWS * D_MODEL - dw_flat.shape[0])).reshape(SMALL_ROWS, D_MODEL)])
    last_src, last_kind = g_f1 + [packed], [False] * 3 + [True]
    r_f1 = _exchange_start(last_src, zones(g_f1, False) + zones([packed], True), last_kind, grad_x, "scatter_ffn1_start")

    all_done = lambda names: sum(out_d[k][:, :1, :1] for k in names)
    update(FFN2, _exchange_wait(*r_f2[:4], [False] * 3, r_f1[4], "scatter_ffn2_wait"))
    update(MIXER, _exchange_wait(*r_mix[:4], [False] * 3, all_done(FFN2), "scatter_mixer_wait"))
    last = _exchange_wait(*r_f1[:4], last_kind, all_done(MIXER), "scatter_ffn1_wait")
    update(FFN1, last[:3])
    small_sum = _slot_sum(last[3], "small_grad_sum")
    loss = small_sum[SMALL_ROWS - 1, D_MODEL - 1]
    g_small = small_sum[:SMALL_ROWS]
    d, m2, v2 = _adamw(g_small, _pack_small(w), _pack_small(mo), _pack_small(vo), "adamw_small")
    for dst, val in ((out_g, g_small), (out_d, d), (out_m, m2), (out_v, v2)):
        dst.update(_unpack_small(val, sm))
    g_dw = small_sum[SMALL_ROWS:].reshape(-1)[:CONV_WIDTH * CONV_CH].reshape(CONV_WIDTH, CONV_CH)
    g_dw = lax.dynamic_slice_in_dim(g_dw, me * (CONV_CH // N_DEV), CONV_CH // N_DEV, axis=1)
    d, m2, v2 = _adamw(g_dw, w["w_dw"][0], mo["w_dw"][0], vo["w_dw"][0], "adamw_w_dw")
    out_g["w_dw"], out_d["w_dw"], out_m["w_dw"], out_v["w_dw"] = g_dw[None], d[None], m2[None], v2[None]

    return (loss, grad_x[None], *[out_g[k] for k in WEIGHTS], *[out_d[k] for k in WEIGHTS], *[out_m[k] for k in WEIGHTS],
            *[out_v[k] for k in WEIGHTS])
```

```python
import jax
import jax.numpy as jnp
from jax import lax
from jax.experimental import pallas as pl
from jax.experimental.pallas import tpu as pltpu

D_MODEL = 1024
N_DEV = 8
FF_BLOCK = 352
D_FF = N_DEV * FF_BLOCK
FF_TILE = 256
N_FF_TILES = D_FF // FF_TILE
FF_ROWS = 1024
FF_ROWS_W = 2048
FF_CHUNK = 256
HEAD_DIM = 64
PAIR = 2 * HEAD_DIM
N_Q_HEADS = 8
Q_PER_KV = 4
ATTN_BLOCK = 128
Q_COLS = 512
KV_COLS = 128
CONV_CH = 256
MQ_COLS = 256
IN_COLS = 1536
CONV_WIDTH = 31
CONV_PAD = 32
CONV_CHUNK = 256
N_MEM_HEADS = 4
ROPE_THETA = 500000.0
ROPE_HALF = 8
EPS = 1e-6
SCALE = HEAD_DIM ** -0.5
NEG = -1e30
ADAM_LR, ADAM_B1, ADAM_B2, ADAM_EPS, ADAM_WD, ADAM_STEP = 0.001, 0.9, 0.999, 1e-08, 0.01, 10
VMEM_LIMIT_BYTES = 56 * 1024 * 1024
BF = jnp.bfloat16
F32 = jnp.float32

SMALL = ["g_ffn1", "g_mix", "g_mem", "g_ffn2", "g_q", "g_k", "g_mq", "g_mk", "sinks", "b_dw", "g_conv_ln", "b_conv_ln"]
WEIGHTS = ["g_ffn1", "w_ffn1_gate", "w_ffn1_up", "w_ffn1_down", "g_mix", "w_in", "g_q", "g_k", "sinks", "w_dw", "b_dw",
           "g_conv_ln", "b_conv_ln", "g_mem", "w_mem_kv", "g_mq", "g_mk", "w_out", "g_ffn2", "w_ffn2_gate", "w_ffn2_up",
           "w_ffn2_down"]
SMALL_ROWS = 8
PACK_ROWS = 16


def _nn(a, b):
    return jnp.dot(a, b, preferred_element_type=F32)


def _nt(a, b):
    return lax.dot_general(a, b, (((1,), (1,)), ((), ())), preferred_element_type=F32)


def _tn(a, b):
    return lax.dot_general(a, b, (((0,), (0,)), ((), ())), preferred_element_type=F32)


def _params(n_grid):
    return pltpu.CompilerParams(dimension_semantics=("arbitrary",) * n_grid, vmem_limit_bytes=VMEM_LIMIT_BYTES)


def _ordered_after(body, after):
    if after is None:
        return body, (), []

    def body_after(after_ref, *refs):
        body(*refs)

    return body_after, (after,), [pl.BlockSpec(memory_space=pl.ANY)]


def _row_rms(xv):
    return lax.rsqrt(jnp.mean(xv * xv, axis=-1, keepdims=True) + EPS)


def _rms_bwd(dh, xv, r, g):
    u = dh * g
    dx = r * u - xv * (r * r * r) * jnp.mean(u * xv, axis=-1, keepdims=True)
    return dx, jnp.sum(dh * xv * r, axis=0, keepdims=True)


def _sum_all(a):
    return jnp.sum(jnp.sum(a, axis=1, keepdims=True), axis=0, keepdims=True)


def _pair_sums(vs):
    row = lax.broadcasted_iota(jnp.int32, (PAIR, PAIR), 0) >= HEAD_DIM
    col = lax.broadcasted_iota(jnp.int32, (PAIR, PAIR), 1) >= HEAD_DIM
    same_head = (row == col).astype(BF)
    stacked = jnp.concatenate(vs, axis=0) if len(vs) > 1 else vs[0]
    hi = stacked.astype(BF)
    lo = (stacked - hi.astype(F32)).astype(BF)
    n = stacked.shape[0]
    both = _nn(jnp.concatenate([hi, lo], axis=0), same_head)
    total = both[:n] + both[n:]
    out, at = [], 0
    for v in vs:
        out.append(total[at:at + v.shape[0]])
        at += v.shape[0]
    return out


def _pair_sum(v):
    return _pair_sums([v])[0]


def _bf16_terms(v, n):
    terms, rest = [], v
    for _ in range(n):
        t = rest.astype(BF)
        terms.append(t)
        rest = rest - t.astype(F32)
    return terms


def _as_rows(cols):
    k = cols.shape[1]
    eye = (lax.broadcasted_iota(jnp.int32, (k, k), 0) == lax.broadcasted_iota(jnp.int32, (k, k), 1)).astype(BF)
    return sum(_nt(eye, t) for t in _bf16_terms(cols, 3))


def _head_sums_as_rows(v):
    sel = (lax.broadcasted_iota(jnp.int32, (8, PAIR), 0)
           == (lax.broadcasted_iota(jnp.int32, (8, PAIR), 1) >= HEAD_DIM).astype(jnp.int32)).astype(BF)
    return sum(_nt(sel, t) for t in _bf16_terms(v, 2))


def _pair_rms_many(xs):
    return [lax.rsqrt(s * (1.0 / HEAD_DIM) + EPS) for s in _pair_sums([xv * xv for xv in xs])]


def _pair_rms(xv):
    return _pair_rms_many([xv])[0]


def _pair_rms_bwd_many(dxns, xs, rs, g):
    us = [dxn * g for dxn in dxns]
    sums = _pair_sums([u * xv for u, xv in zip(us, xs)])
    return [(r * u - xv * (r * r * r) * (s * (1.0 / HEAD_DIM)), jnp.sum(dxn * xv * r, axis=0, keepdims=True))
            for dxn, xv, r, u, s in zip(dxns, xs, rs, us, sums)]


def _pair_rms_bwd(dxn, xv, r, g):
    return _pair_rms_bwd_many([dxn], [xv], [r], g)[0]


def _rope_mask(shape):
    lane = lax.broadcasted_iota(jnp.int32, shape, 1)
    return ((lane & (HEAD_DIM - 1)) < 2 * ROPE_HALF).astype(F32)


def _partner(v):
    return pltpu.roll(v, ROPE_HALF, 1) + pltpu.roll(v, PAIR - ROPE_HALF, 1)


def _rope(xn, cos_t, sin_t):
    return xn * cos_t + _partner(xn * _rope_mask(xn.shape)) * sin_t


def _rope_t(d, cos_t, sin_t):
    return d * cos_t + _partner(d * sin_t) * _rope_mask(d.shape)


def _rope_tables(positions):
    inv_freq = ROPE_THETA ** (-jnp.arange(ROPE_HALF, dtype=F32) / ROPE_HALF)
    ang = positions.astype(F32)[:, None] * inv_freq
    cos, sin = jnp.cos(ang), jnp.sin(ang)
    t = positions.shape[0]
    cos_h = jnp.concatenate([cos, cos, jnp.ones((t, HEAD_DIM - 2 * ROPE_HALF), F32)], axis=1)
    sin_h = jnp.concatenate([-sin, sin, jnp.zeros((t, HEAD_DIM - 2 * ROPE_HALF), F32)], axis=1)
    return jnp.tile(cos_h, (1, 2)), jnp.tile(sin_h, (1, 2))


def _ff_rows(w):
    return w.reshape(D_FF, D_MODEL)


def _ffn_fwd(x, g, wg, wu, wd, target, name, after=None):
    t_len = x.shape[0]
    tm = min(FF_ROWS, t_len)
    with_loss = target is not None

    def body(*refs):
        if with_loss:
            x_ref, g_ref, wg_ref, wu_ref, wd_ref, t_ref, h_ref, gg_ref, uu_ref, dy_ref, loss_ref, acc = refs
        else:
            x_ref, g_ref, wg_ref, wu_ref, wd_ref, h_ref, gg_ref, uu_ref, xo_ref, acc = refs
        t, j = pl.program_id(0), pl.program_id(1)

        @pl.when(j == 0)
        def _():
            xv = x_ref[...]
            h_ref[...] = (xv * _row_rms(xv) * g_ref[...]).astype(BF)
            acc[...] = jnp.zeros_like(acc)

        h = h_ref[...]
        gate = _nt(h, wg_ref[...])
        up = _nt(h, wu_ref[...])
        gg_ref[...] = gate.astype(BF)
        uu_ref[...] = up.astype(BF)
        act = (gate * jax.nn.sigmoid(gate) * up).astype(BF)
        acc[...] += _nn(act, wd_ref[...])

        @pl.when(j == N_FF_TILES - 1)
        def _():
            xo = x_ref[...] + 0.5 * acc[...]
            if with_loss:
                err = xo - t_ref[...]
                dy_ref[...] = err * (1.0 / D_MODEL)

                @pl.when(t == 0)
                def _():
                    loss_ref[...] = jnp.zeros_like(loss_ref)

                loss_ref[...] += _sum_all(err * err) * (0.5 / D_MODEL)
            else:
                xo_ref[...] = xo

    row = pl.BlockSpec((tm, D_MODEL), lambda t, j: (t, 0))
    vec = pl.BlockSpec((1, D_MODEL), lambda t, j: (0, 0))
    w_spec = pl.BlockSpec((FF_TILE, D_MODEL), lambda t, j: (j, 0))
    blk = pl.BlockSpec((tm, FF_TILE), lambda t, j: (t, j))
    in_specs = [row, vec, w_spec, w_spec, w_spec] + ([row] if with_loss else [])
    out_shape = [jax.ShapeDtypeStruct((t_len, D_MODEL), BF),
                 jax.ShapeDtypeStruct((t_len, D_FF), BF),
                 jax.ShapeDtypeStruct((t_len, D_FF), BF),
                 jax.ShapeDtypeStruct((t_len, D_MODEL), F32)]
    out_specs = [row, blk, blk, row]
    if with_loss:
        out_shape.append(jax.ShapeDtypeStruct((1, 128), F32))
        out_specs.append(pl.BlockSpec((1, 128), lambda t, j: (0, 0)))
    args = (x, g, _ff_rows(wg), _ff_rows(wu), _ff_rows(wd)) + ((target,) if with_loss else ())
    body, first, first_specs = _ordered_after(body, after)
    return pl.pallas_call(body, out_shape=out_shape, grid=(t_len // tm, N_FF_TILES), in_specs=first_specs + in_specs,
                          out_specs=out_specs, scratch_shapes=[pltpu.VMEM((tm, D_MODEL), F32)], name=name,
                          compiler_params=_params(2))(*first, *args)


def _ffn_bwd_act(dy, x, g, gate, up, wg, wu, wd, name, after=None):
    t_len = x.shape[0]
    tm = min(FF_ROWS, t_len)

    def body(dy_ref, x_ref, g_ref, gg_ref, uu_ref, wg_ref, wu_ref, wd_ref,
             dyb_ref, act_ref, dgg_ref, duu_ref, dx_ref, dg_ref, acc):
        t, j = pl.program_id(0), pl.program_id(1)

        @pl.when(j == 0)
        def _():
            dyb_ref[...] = (0.5 * dy_ref[...]).astype(BF)
            acc[...] = jnp.zeros_like(acc)

        w_both = jnp.concatenate([wg_ref[...], wu_ref[...]], axis=0)
        for c in range(tm // FF_CHUNK):
            rows = slice(c * FF_CHUNK, (c + 1) * FF_CHUNK)
            d_act = _nt(dyb_ref[rows, :], wd_ref[...])
            gate = gg_ref[rows, :].astype(F32)
            upv = uu_ref[rows, :].astype(F32)
            sig = jax.nn.sigmoid(gate)
            silu = gate * sig
            d_up = (d_act * silu).astype(BF)
            d_gate = (d_act * upv * (sig * (1.0 + gate * (1.0 - sig)))).astype(BF)
            act_ref[rows, :] = (silu * upv).astype(BF)
            dgg_ref[rows, :] = d_gate
            duu_ref[rows, :] = d_up
            acc[rows, :] += _nn(jnp.concatenate([d_gate, d_up], axis=1), w_both)

        @pl.when(j == N_FF_TILES - 1)
        def _():
            xv = x_ref[...]
            dx, dg = _rms_bwd(acc[...], xv, _row_rms(xv), g_ref[...])
            dx_ref[...] = dy_ref[...] + dx

            @pl.when(t == 0)
            def _():
                dg_ref[...] = jnp.zeros_like(dg_ref)

            dg_ref[...] += dg

    row = pl.BlockSpec((tm, D_MODEL), lambda t, j: (t, 0))
    vec = pl.BlockSpec((1, D_MODEL), lambda t, j: (0, 0))
    w_spec = pl.BlockSpec((FF_TILE, D_MODEL), lambda t, j: (j, 0))
    blk = pl.BlockSpec((tm, FF_TILE), lambda t, j: (t, j))
    blk_shape = jax.ShapeDtypeStruct((t_len, D_FF), BF)
    body, first, first_specs = _ordered_after(body, after)
    return pl.pallas_call(
        body,
        out_shape=[jax.ShapeDtypeStruct((t_len, D_MODEL), BF), blk_shape, blk_shape, blk_shape,
                   jax.ShapeDtypeStruct((t_len, D_MODEL), F32), jax.ShapeDtypeStruct((1, D_MODEL), F32)],
        grid=(t_len // tm, N_FF_TILES), in_specs=first_specs + [row, row, vec, blk, blk, w_spec, w_spec, w_spec],
        out_specs=[row, blk, blk, blk, row, vec], scratch_shapes=[pltpu.VMEM((tm, D_MODEL), F32)], name=name,
        compiler_params=_params(2))(*first, dy, x, g, gate, up, _ff_rows(wg), _ff_rows(wu), _ff_rows(wd))


def _ffn_bwd_w(h, dyb, act, d_gate, d_up, name):
    t_len = h.shape[0]
    tm = min(FF_ROWS_W, t_len)
    nt = t_len // tm

    def body(h_ref, dyb_ref, act_ref, dgg_ref, duu_ref, dwg_ref, dwu_ref, dwd_ref, ag, au, ad):
        t = pl.program_id(1)

        @pl.when(t == 0)
        def _():
            ag[...] = jnp.zeros_like(ag)
            au[...] = jnp.zeros_like(au)
            ad[...] = jnp.zeros_like(ad)

        h = h_ref[...]
        ag[...] += _tn(dgg_ref[...], h)
        au[...] += _tn(duu_ref[...], h)
        ad[...] += _tn(act_ref[...], dyb_ref[...])

        @pl.when(t == nt - 1)
        def _():
            dwg_ref[...] = ag[...].astype(BF)
            dwu_ref[...] = au[...].astype(BF)
            dwd_ref[...] = ad[...].astype(BF)

    row = pl.BlockSpec((tm, D_MODEL), lambda j, t: (t, 0))
    blk = pl.BlockSpec((tm, FF_TILE), lambda j, t: (t, j))
    w_spec = pl.BlockSpec((FF_TILE, D_MODEL), lambda j, t: (j, 0))
    grads = pl.pallas_call(
        body,
        out_shape=[jax.ShapeDtypeStruct((D_FF, D_MODEL), BF)] * 3,
        grid=(N_FF_TILES, nt), in_specs=[row, row, blk, blk, blk], out_specs=[w_spec] * 3,
        scratch_shapes=[pltpu.VMEM((FF_TILE, D_MODEL), F32)] * 3,
        name=name, compiler_params=_params(2))(h, dyb, act, d_gate, d_up)
    return [a.reshape(N_DEV, FF_BLOCK, D_MODEL) for a in grads]


def _in_proj_fwd(x, g, w_in, name):
    t_len = x.shape[0]
    tm = min(512, t_len)

    def body(x_ref, g_ref, w_ref, h_ref, p_ref):
        xv = x_ref[...]
        h = (xv * _row_rms(xv) * g_ref[...]).astype(BF)
        h_ref[...] = h
        p_ref[...] = _nt(h, w_ref[...])

    row = pl.BlockSpec((tm, D_MODEL), lambda t: (t, 0))
    return pl.pallas_call(
        body, out_shape=[jax.ShapeDtypeStruct((t_len, D_MODEL), BF), jax.ShapeDtypeStruct((t_len, IN_COLS), F32)],
        grid=(t_len // tm,),
        in_specs=[row, pl.BlockSpec((1, D_MODEL), lambda t: (0, 0)), pl.BlockSpec((IN_COLS, D_MODEL), lambda t: (0, 0))],
        out_specs=[row, pl.BlockSpec((tm, IN_COLS), lambda t: (t, 0))], name=name, compiler_params=_params(1))(x, g, w_in)


def _in_proj_bwd(dq, dk, dv, du, dmq, w_in, h, x, g, dres, name):
    t_len = x.shape[0]
    tm = min(512, t_len)
    nt = t_len // tm
    groups = [(0, Q_COLS), (Q_COLS, KV_COLS), (Q_COLS + KV_COLS, KV_COLS), (Q_COLS + 2 * KV_COLS, 2 * CONV_CH),
              (Q_COLS + 2 * KV_COLS + 2 * CONV_CH, MQ_COLS)]

    def body(dq_ref, dk_ref, dv_ref, du_ref, dmq_ref, w_ref, h_ref, x_ref, g_ref, dres_ref, dx_ref, dg_ref, dw_ref, acc):
        t = pl.program_id(0)

        @pl.when(t == 0)
        def _():
            acc[...] = jnp.zeros_like(acc)
            dg_ref[...] = jnp.zeros_like(dg_ref)

        h = h_ref[...]
        dh = jnp.zeros((tm, D_MODEL), F32)
        for (start, width), ref in zip(groups, (dq_ref, dk_ref, dv_ref, du_ref, dmq_ref)):
            piece = ref[...]
            dh = dh + _nn(piece, w_ref[start:start + width, :])
            acc[start:start + width, :] += _tn(piece, h)
        xv = x_ref[...]
        dx, dg = _rms_bwd(dh, xv, _row_rms(xv), g_ref[...])
        dx_ref[...] = dres_ref[...] + dx
        dg_ref[...] += dg

        @pl.when(t == nt - 1)
        def _():
            dw_ref[...] = acc[...].astype(BF)

    def cols(width):
        return pl.BlockSpec((tm, width), lambda t: (t, 0))

    row = cols(D_MODEL)
    vec = pl.BlockSpec((1, D_MODEL), lambda t: (0, 0))
    full = pl.BlockSpec((IN_COLS, D_MODEL), lambda t: (0, 0))
    return pl.pallas_call(
        body,
        out_shape=[jax.ShapeDtypeStruct((t_len, D_MODEL), F32), jax.ShapeDtypeStruct((1, D_MODEL), F32),
                   jax.ShapeDtypeStruct((IN_COLS, D_MODEL), BF)],
        grid=(nt,),
        in_specs=[cols(Q_COLS), cols(KV_COLS), cols(KV_COLS), cols(2 * CONV_CH), cols(MQ_COLS), full, row, row, vec, row],
        out_specs=[row, vec, full], scratch_shapes=[pltpu.VMEM((IN_COLS, D_MODEL), F32)], name=name,
        compiler_params=_params(1))(dq, dk, dv, du, dmq, w_in, h, x, g, dres)


def _attn_fwd(proj, cos_t, sin_t, gq2, gk2, sinks, name):
    t_len = proj.shape[0]
    nb = t_len // ATTN_BLOCK

    def body(q_ref, kvc_ref, kvp_ref, cq_ref, sq_ref, cp_ref, sp_ref, gq_ref, gk_ref, sk_ref, ya_ref, lse_ref):
        n = pl.program_id(0)
        cq, sq = cq_ref[...], sq_ref[...]
        c_all = jnp.concatenate([cp_ref[...], cq], axis=0)
        s_all = jnp.concatenate([sp_ref[...], sq], axis=0)
        kv = jnp.concatenate([kvp_ref[...], kvc_ref[...]], axis=0)
        k_raw, v = kv[:, :PAIR], kv[:, PAIR:]
        q_raws = [q_ref[:, pr * PAIR:(pr + 1) * PAIR] for pr in range(N_Q_HEADS // 2)]
        rms = _pair_rms_many(q_raws + [k_raw])
        kn = _rope(k_raw * rms[-1] * gk_ref[...], c_all, s_all)
        row = lax.broadcasted_iota(jnp.int32, (ATTN_BLOCK, 2 * ATTN_BLOCK), 0)
        col = lax.broadcasted_iota(jnp.int32, (ATTN_BLOCK, 2 * ATTN_BLOCK), 1)
        rel = row + ATTN_BLOCK - col
        valid = (rel >= 0) & (rel < ATTN_BLOCK) & ((col >= ATTN_BLOCK) | (n > 0))
        for pr in range(N_Q_HEADS // 2):
            qn = _rope(q_raws[pr] * rms[pr] * gq_ref[...], cq, sq)
            kvh = (2 * pr) // Q_PER_KV
            kh = kn[:, kvh * HEAD_DIM:(kvh + 1) * HEAD_DIM].astype(BF)
            v_ones = jnp.concatenate([v[:, kvh * HEAD_DIM:(kvh + 1) * HEAD_DIM],
                                      jnp.ones((2 * ATTN_BLOCK, HEAD_DIM), F32)], axis=1).astype(BF)
            for hh in range(2):
                hd = 2 * pr + hh
                qh = qn[:, hh * HEAD_DIM:(hh + 1) * HEAD_DIM].astype(BF)
                s = jnp.where(valid, _nt(qh, kh) * SCALE, NEG)
                sink = sk_ref[0:1, hd:hd + 1]
                m = jnp.maximum(jnp.max(s, axis=-1, keepdims=True), sink)
                ov = _nn(jnp.exp(s - m).astype(BF), v_ones)
                den = ov[:, HEAD_DIM:HEAD_DIM + 1] + jnp.exp(sink - m)
                ya_ref[:, hd * HEAD_DIM:(hd + 1) * HEAD_DIM] = (ov[:, :HEAD_DIM] / den).astype(BF)
                lse_ref[:, hd:hd + 1] = m + jnp.log(den)

    prev = lambda n: (jnp.maximum(n - 1, 0), 0)
    tab = pl.BlockSpec((ATTN_BLOCK, PAIR), lambda n: (n, 0))
    tab_p = pl.BlockSpec((ATTN_BLOCK, PAIR), prev)
    one = lambda w: pl.BlockSpec((1, w), lambda n: (0, 0))
    return pl.pallas_call(
        body, out_shape=[jax.ShapeDtypeStruct((t_len, Q_COLS), BF), jax.ShapeDtypeStruct((t_len, N_Q_HEADS), F32)],
        grid=(nb,),
        in_specs=[pl.BlockSpec((ATTN_BLOCK, Q_COLS), lambda n: (n, 0)),
                  pl.BlockSpec((ATTN_BLOCK, 2 * KV_COLS), lambda n: (n, 2)),
                  pl.BlockSpec((ATTN_BLOCK, 2 * KV_COLS), lambda n: (jnp.maximum(n - 1, 0), 2)),
                  tab, tab, tab_p, tab_p, one(PAIR), one(PAIR), one(N_Q_HEADS)],
        out_specs=[pl.BlockSpec((ATTN_BLOCK, Q_COLS), lambda n: (n, 0)),
                   pl.BlockSpec((ATTN_BLOCK, N_Q_HEADS), lambda n: (n, 0))],
        name=name, compiler_params=_params(1))(proj, proj, proj, cos_t, sin_t, cos_t, sin_t, gq2, gk2, sinks)


def _attn_bwd_q(proj, cos_t, sin_t, gq2, gk2, sinks, ya, dmix, lse, name):
    t_len = proj.shape[0]
    nb = t_len // ATTN_BLOCK

    def body(q_ref, kvc_ref, kvp_ref, cq_ref, sq_ref, cp_ref, sp_ref, gq_ref, gk_ref, sk_ref, ya_ref, do_ref, lse_ref,
             dq_ref, dgq_ref, dsk_ref, scr):
        n = pl.program_id(0)

        @pl.when(n == 0)
        def _():
            dgq_ref[...] = jnp.zeros_like(dgq_ref)
            dsk_ref[...] = jnp.zeros_like(dsk_ref)

        cq, sq = cq_ref[...], sq_ref[...]
        c_all = jnp.concatenate([cp_ref[...], cq], axis=0)
        s_all = jnp.concatenate([sp_ref[...], sq], axis=0)
        kv = jnp.concatenate([kvp_ref[...], kvc_ref[...]], axis=0)
        k_raw, v = kv[:, :PAIR], kv[:, PAIR:]
        n_pairs = N_Q_HEADS // 2
        pair_lanes = [slice(pr * PAIR, (pr + 1) * PAIR) for pr in range(n_pairs)]
        q_raws = [q_ref[:, lanes] for lanes in pair_lanes]
        rms = _pair_rms_many(q_raws + [k_raw])
        dsums = _pair_sums([do_ref[:, lanes] * ya_ref[:, lanes].astype(F32) for lanes in pair_lanes])
        kn = _rope(k_raw * rms[-1] * gk_ref[...], c_all, s_all)
        row = lax.broadcasted_iota(jnp.int32, (ATTN_BLOCK, 2 * ATTN_BLOCK), 0)
        col = lax.broadcasted_iota(jnp.int32, (ATTN_BLOCK, 2 * ATTN_BLOCK), 1)
        rel = row + ATTN_BLOCK - col
        valid = (rel >= 0) & (rel < ATTN_BLOCK) & ((col >= ATTN_BLOCK) | (n > 0))
        gq = gq_ref[...]
        for pr in range(n_pairs):
            qn = _rope(q_raws[pr] * rms[pr] * gq, cq, sq)
            kvh = (2 * pr) // Q_PER_KV
            kh = kn[:, kvh * HEAD_DIM:(kvh + 1) * HEAD_DIM].astype(BF)
            vh = v[:, kvh * HEAD_DIM:(kvh + 1) * HEAD_DIM].astype(BF)
            dsum2 = dsums[pr]
            for hh in range(2):
                hd = 2 * pr + hh
                qh = qn[:, hh * HEAD_DIM:(hh + 1) * HEAD_DIM].astype(BF)
                lse_h = lse_ref[:, hd:hd + 1]
                p = jnp.exp(jnp.where(valid, _nt(qh, kh) * SCALE, NEG) - lse_h)
                d_o = do_ref[:, hd * HEAD_DIM:(hd + 1) * HEAD_DIM]
                dsum = dsum2[:, hh * HEAD_DIM:hh * HEAD_DIM + 1]
                ds = p * (_nt(d_o.astype(BF), vh) - dsum)
                scr[pr, :, hh * HEAD_DIM:(hh + 1) * HEAD_DIM] = _nn(ds.astype(BF), kh) * SCALE
                sink = sk_ref[0:1, hd:hd + 1]
                dsk_ref[0:1, hd:hd + 1] += -jnp.sum(jnp.exp(sink - lse_h) * dsum, axis=0, keepdims=True)
        back = _pair_rms_bwd_many([_rope_t(scr[pr], cq, sq) for pr in range(n_pairs)], q_raws, rms[:n_pairs], gq)
        for lanes, (dx, dg) in zip(pair_lanes, back):
            dq_ref[:, lanes] = dx.astype(BF)
            dgq_ref[...] += dg

    prev = lambda n: (jnp.maximum(n - 1, 0), 0)
    tab = pl.BlockSpec((ATTN_BLOCK, PAIR), lambda n: (n, 0))
    tab_p = pl.BlockSpec((ATTN_BLOCK, PAIR), prev)
    one = lambda w: pl.BlockSpec((1, w), lambda n: (0, 0))
    qblk = pl.BlockSpec((ATTN_BLOCK, Q_COLS), lambda n: (n, 0))
    return pl.pallas_call(
        body,
        out_shape=[jax.ShapeDtypeStruct((t_len, Q_COLS), BF), jax.ShapeDtypeStruct((1, PAIR), F32),
                   jax.ShapeDtypeStruct((1, N_Q_HEADS), F32)],
        grid=(nb,),
        in_specs=[qblk, pl.BlockSpec((ATTN_BLOCK, 2 * KV_COLS), lambda n: (n, 2)),
                  pl.BlockSpec((ATTN_BLOCK, 2 * KV_COLS), lambda n: (jnp.maximum(n - 1, 0), 2)),
                  tab, tab, tab_p, tab_p, one(PAIR), one(PAIR), one(N_Q_HEADS), qblk, qblk,
                  pl.BlockSpec((ATTN_BLOCK, N_Q_HEADS), lambda n: (n, 0))],
        out_specs=[qblk, one(PAIR), one(N_Q_HEADS)], scratch_shapes=[pltpu.VMEM((N_Q_HEADS // 2, ATTN_BLOCK, PAIR), F32)],
        name=name, compiler_params=_params(1))(proj, proj, proj, cos_t, sin_t, cos_t, sin_t, gq2, gk2, sinks, ya, dmix, lse)


def _attn_bwd_kv(proj, cos_t, sin_t, gq2, gk2, ya, dmix, lse, name):
    t_len = proj.shape[0]
    nb = t_len // ATTN_BLOCK

    def body(kv_ref, q0_ref, q1_ref, ck_ref, sk_ref, c1_ref, s1_ref, gq_ref, gk_ref, o0_ref, o1_ref, do0_ref, do1_ref,
             l0_ref, l1_ref, dk_ref, dv_ref, dgk_ref, dkn_scr, dv_scr):
        m = pl.program_id(0)

        @pl.when(m == 0)
        def _():
            dgk_ref[...] = jnp.zeros_like(dgk_ref)

        ck, sk = ck_ref[...], sk_ref[...]
        c_all = jnp.concatenate([ck, c1_ref[...]], axis=0)
        s_all = jnp.concatenate([sk, s1_ref[...]], axis=0)
        kvv = kv_ref[...]
        k_raw, v = kvv[:, :PAIR], kvv[:, PAIR:]
        pair_lanes = [slice(pr * PAIR, (pr + 1) * PAIR) for pr in range(N_Q_HEADS // 2)]
        q_raws = [jnp.concatenate([q0_ref[:, lanes], q1_ref[:, lanes]], axis=0) for lanes in pair_lanes]
        d_os = [jnp.concatenate([do0_ref[:, lanes], do1_ref[:, lanes]], axis=0) for lanes in pair_lanes]
        rms = [_pair_rms(a) for a in q_raws + [k_raw]]
        rk = rms[-1]
        gk = gk_ref[...]
        kn = _rope(k_raw * rk * gk, ck, sk)
        key = lax.broadcasted_iota(jnp.int32, (ATTN_BLOCK, 2 * ATTN_BLOCK), 0)
        qry = lax.broadcasted_iota(jnp.int32, (ATTN_BLOCK, 2 * ATTN_BLOCK), 1)
        valid = ((qry < ATTN_BLOCK) & (qry >= key)) | ((qry >= ATTN_BLOCK) & (qry - ATTN_BLOCK < key) & (m < nb - 1))
        lse_rows = _as_rows(jnp.concatenate([l0_ref[...], l1_ref[...]], axis=0))
        d_kn, d_v = [None, None], [None, None]
        plus = lambda acc, term: term if acc is None else acc + term
        for pr in range(N_Q_HEADS // 2):
            qn = _rope(q_raws[pr] * rms[pr] * gq_ref[...], c_all, s_all)
            d_o2, lanes = d_os[pr], pair_lanes[pr]
            dsum_rows = _head_sums_as_rows(d_o2 * jnp.concatenate([o0_ref[:, lanes], o1_ref[:, lanes]], axis=0).astype(F32))
            kvh = (2 * pr) // Q_PER_KV
            khs = slice(kvh * HEAD_DIM, (kvh + 1) * HEAD_DIM)
            kh = kn[:, khs].astype(BF)
            vh = v[:, khs].astype(BF)
            for hh in range(2):
                hd = 2 * pr + hh
                hs = slice(hh * HEAD_DIM, (hh + 1) * HEAD_DIM)
                qh = qn[:, hs].astype(BF)
                d_oh = d_o2[:, hs].astype(BF)
                p = jnp.exp(jnp.where(valid, _nt(kh, qh) * SCALE, NEG) - lse_rows[hd:hd + 1, :])
                d_v[kvh] = plus(d_v[kvh], _nn(p.astype(BF), d_oh))
                ds = p * (_nt(vh, d_oh) - dsum_rows[hh:hh + 1, :])
                d_kn[kvh] = plus(d_kn[kvh], _nn(ds.astype(BF), qh))
        for kvh in range(2):
            khs = slice(kvh * HEAD_DIM, (kvh + 1) * HEAD_DIM)
            dkn_scr[:, khs] = d_kn[kvh] * SCALE
            dv_scr[:, khs] = d_v[kvh]
        dx, dg = _pair_rms_bwd(_rope_t(dkn_scr[...], ck, sk), k_raw, rk, gk)
        dk_ref[...] = dx.astype(BF)
        dv_ref[...] = dv_scr[...].astype(BF)
        dgk_ref[...] += dg

    nxt = lambda m: (jnp.minimum(m + 1, nb - 1), 0)
    cur = lambda m: (m, 0)
    tab = lambda f: pl.BlockSpec((ATTN_BLOCK, PAIR), f)
    qb = lambda f: pl.BlockSpec((ATTN_BLOCK, Q_COLS), f)
    lb = lambda f: pl.BlockSpec((ATTN_BLOCK, N_Q_HEADS), f)
    one = pl.BlockSpec((1, PAIR), lambda m: (0, 0))
    return pl.pallas_call(
        body,
        out_shape=[jax.ShapeDtypeStruct((t_len, KV_COLS), BF), jax.ShapeDtypeStruct((t_len, KV_COLS), BF),
                   jax.ShapeDtypeStruct((1, PAIR), F32)],
        grid=(nb,),
        in_specs=[pl.BlockSpec((ATTN_BLOCK, 2 * KV_COLS), lambda m: (m, 2)), qb(cur), qb(nxt), tab(cur), tab(cur), tab(nxt),
                  tab(nxt), one, one, qb(cur), qb(nxt), qb(cur), qb(nxt), lb(cur), lb(nxt)],
        out_specs=[tab(cur), tab(cur), one],
        scratch_shapes=[pltpu.VMEM((ATTN_BLOCK, PAIR), F32), pltpu.VMEM((ATTN_BLOCK, PAIR), F32)],
        name=name, compiler_params=_params(1))(proj, proj, proj, cos_t, sin_t, cos_t, sin_t, gq2, gk2, ya, ya, dmix, dmix,
                                               lse, lse)


def _conv_taps(blk, w, offset_of):
    acc = jnp.zeros((CONV_CHUNK, CONV_CH), F32)
    for k in range(CONV_WIDTH):
        o = offset_of(k)
        acc = acc + w[k:k + 1, :] * blk[o:o + CONV_CHUNK, :]
    return acc


def _conv_fwd(proj, w_dw, b_dw, g_ln, b_ln, name):
    t_len = proj.shape[0]
    nc = t_len // CONV_CHUNK

    def body(a_ref, gt_ref, w_ref, b_ref, g_ref, bl_ref, yc_ref, cv_ref, pad):
        pad[0:CONV_PAD, :] = jnp.zeros((CONV_PAD, CONV_CH), F32)

        def glu(c, carry):
            rows = pl.ds(pl.multiple_of(c * CONV_CHUNK, CONV_CHUNK), CONV_CHUNK)
            dst = pl.ds(pl.multiple_of(c * CONV_CHUNK + CONV_PAD, CONV_PAD), CONV_CHUNK)
            pad[dst, :] = a_ref[rows, :] * jax.nn.sigmoid(gt_ref[rows, :])
            return carry

        lax.fori_loop(0, nc, glu, 0)
        w = w_ref[...]

        def conv(c, carry):
            base = pl.multiple_of(c * CONV_CHUNK, CONV_CHUNK)
            blk = pad[pl.ds(base, CONV_CHUNK + CONV_PAD), :]
            cv = _conv_taps(blk, w, lambda k: k + CONV_PAD - (CONV_WIDTH - 1)) + b_ref[...]
            mu = jnp.mean(cv, axis=-1, keepdims=True)
            xc = cv - mu
            z = xc * lax.rsqrt(jnp.mean(xc * xc, axis=-1, keepdims=True) + EPS) * g_ref[...] + bl_ref[...]
            rows = pl.ds(base, CONV_CHUNK)
            cv_ref[rows, :] = cv
            yc_ref[rows, :] = (z * jax.nn.sigmoid(z)).astype(BF)
            return carry

        lax.fori_loop(0, nc, conv, 0)

    vec = pl.BlockSpec((1, CONV_CH), lambda i: (0, 0))
    full = pl.BlockSpec((t_len, CONV_CH), lambda i: (0, 0))
    return pl.pallas_call(
        body, out_shape=[jax.ShapeDtypeStruct((t_len, CONV_CH), BF), jax.ShapeDtypeStruct((t_len, CONV_CH), F32)],
        grid=(1,),
        in_specs=[pl.BlockSpec((t_len, CONV_CH), lambda i: (0, 3)), pl.BlockSpec((t_len, CONV_CH), lambda i: (0, 4)),
                  pl.BlockSpec((CONV_WIDTH, CONV_CH), lambda i: (0, 0)), vec, vec, vec],
        out_specs=[full, full], scratch_shapes=[pltpu.VMEM((t_len + CONV_PAD, CONV_CH), F32)], name=name,
        compiler_params=_params(1))(proj, proj, w_dw, b_dw, g_ln, b_ln)


def _conv_bwd(proj, cv, dmix, w_dw, g_ln, b_ln, name):
    t_len = proj.shape[0]
    nc = t_len // CONV_CHUNK

    def body(a_ref, gt_ref, cv_ref, dy_ref, w_ref, g_ref, bl_ref, du_ref, dw_ref, db_ref, dgl_ref, dbl_ref, pad, dpad):
        pad[0:CONV_PAD, :] = jnp.zeros((CONV_PAD, CONV_CH), F32)
        dpad[t_len:t_len + CONV_PAD, :] = jnp.zeros((CONV_PAD, CONV_CH), F32)
        dw_ref[...] = jnp.zeros_like(dw_ref)
        db_ref[...] = jnp.zeros_like(db_ref)
        dgl_ref[...] = jnp.zeros_like(dgl_ref)
        dbl_ref[...] = jnp.zeros_like(dbl_ref)

        def norm_bwd(c, carry):
            base = pl.multiple_of(c * CONV_CHUNK, CONV_CHUNK)
            rows = pl.ds(base, CONV_CHUNK)
            dst = pl.ds(pl.multiple_of(c * CONV_CHUNK + CONV_PAD, CONV_PAD), CONV_CHUNK)
            pad[dst, :] = a_ref[rows, :] * jax.nn.sigmoid(gt_ref[rows, :])
            cvv = cv_ref[rows, :]
            xc = cvv - jnp.mean(cvv, axis=-1, keepdims=True)
            rs = lax.rsqrt(jnp.mean(xc * xc, axis=-1, keepdims=True) + EPS)
            xhat = xc * rs
            z = xhat * g_ref[...] + bl_ref[...]
            sg = jax.nn.sigmoid(z)
            dz = dy_ref[rows, :] * (sg * (1.0 + z * (1.0 - sg)))
            dgl_ref[...] += jnp.sum(dz * xhat, axis=0, keepdims=True)
            dbl_ref[...] += jnp.sum(dz, axis=0, keepdims=True)
            dxh = dz * g_ref[...]
            dcv = rs * (dxh - jnp.mean(dxh, axis=-1, keepdims=True) - xhat * jnp.mean(dxh * xhat, axis=-1, keepdims=True))
            db_ref[...] += jnp.sum(dcv, axis=0, keepdims=True)
            dpad[rows, :] = dcv
            return carry

        lax.fori_loop(0, nc, norm_bwd, 0)
        w = w_ref[...]

        def conv_bwd(c, carry):
            base = pl.multiple_of(c * CONV_CHUNK, CONV_CHUNK)
            rows = pl.ds(base, CONV_CHUNK)
            dblk = dpad[pl.ds(base, CONV_CHUNK + CONV_PAD), :]
            dhc = _conv_taps(dblk, w, lambda k: CONV_WIDTH - 1 - k)
            a = a_ref[rows, :]
            sg = jax.nn.sigmoid(gt_ref[rows, :])
            du_ref[rows, 0:CONV_CH] = (dhc * sg).astype(BF)
            du_ref[rows, CONV_CH:2 * CONV_CH] = (dhc * a * sg * (1.0 - sg)).astype(BF)
            hblk = pad[pl.ds(base, CONV_CHUNK + CONV_PAD), :]
            dcv = dblk[0:CONV_CHUNK, :]
            for k in range(CONV_WIDTH):
                o = k + CONV_PAD - (CONV_WIDTH - 1)
                dw_ref[k:k + 1, :] += jnp.sum(dcv * hblk[o:o + CONV_CHUNK, :], axis=0, keepdims=True)
            return carry

        lax.fori_loop(0, nc, conv_bwd, 0)

    vec = pl.BlockSpec((1, CONV_CH), lambda i: (0, 0))
    full = pl.BlockSpec((t_len, CONV_CH), lambda i: (0, 0))
    wspec = pl.BlockSpec((CONV_WIDTH, CONV_CH), lambda i: (0, 0))
    vshape = jax.ShapeDtypeStruct((1, CONV_CH), F32)
    return pl.pallas_call(
        body,
        out_shape=[jax.ShapeDtypeStruct((t_len, 2 * CONV_CH), BF), jax.ShapeDtypeStruct((CONV_WIDTH, CONV_CH), F32),
                   vshape, vshape, vshape],
        grid=(1,),
        in_specs=[pl.BlockSpec((t_len, CONV_CH), lambda i: (0, 3)), pl.BlockSpec((t_len, CONV_CH), lambda i: (0, 4)), full,
                  pl.BlockSpec((t_len, CONV_CH), lambda i: (0, 2)), wspec, vec, vec],
        out_specs=[pl.BlockSpec((t_len, 2 * CONV_CH), lambda i: (0, 0)), wspec, vec, vec, vec],
        scratch_shapes=[pltpu.VMEM((t_len + CONV_PAD, CONV_CH), F32), pltpu.VMEM((t_len + CONV_PAD, CONV_CH), F32)],
        name=name, compiler_params=_params(1))(proj, proj, cv, dmix, w_dw, g_ln, b_ln)


def _mem_kv_fwd(mem, g, w, name):
    def body(m_ref, g_ref, w_ref, h_ref, kv_ref):
        mv = m_ref[...]
        h = (mv * _row_rms(mv) * g_ref[...]).astype(BF)
        h_ref[...] = h
        kv_ref[...] = _nn(h, w_ref[...])

    m_len = mem.shape[0]
    return pl.pallas_call(
        body, out_shape=[jax.ShapeDtypeStruct((m_len, D_MODEL), BF), jax.ShapeDtypeStruct((m_len, 2 * MQ_COLS), F32)],
        name=name, compiler_params=pltpu.CompilerParams(vmem_limit_bytes=VMEM_LIMIT_BYTES))(mem, g, w)


def _mem_kv_bwd(mem, g, h, w, dkv, name):
    def body(m_ref, g_ref, h_ref, w_ref, dkv_ref, dw_ref, dg_ref):
        dkv_b = dkv_ref[...].astype(BF)
        dw_ref[...] = _tn(h_ref[...], dkv_b).astype(BF)
        mv = m_ref[...]
        dg_ref[...] = jnp.sum(_nt(dkv_b, w_ref[...]) * mv * _row_rms(mv), axis=0, keepdims=True)

    return pl.pallas_call(
        body, out_shape=[jax.ShapeDtypeStruct((D_MODEL, 2 * MQ_COLS), BF), jax.ShapeDtypeStruct((1, D_MODEL), F32)],
        name=name, compiler_params=pltpu.CompilerParams(vmem_limit_bytes=VMEM_LIMIT_BYTES))(mem, g, h, w, dkv)


def _mem_attn_fwd(proj, mkv, gq2, gk2, name):
    t_len = proj.shape[0]
    tm = min(256, t_len)

    def body(q_ref, kv_ref, gq_ref, gk_ref, ym_ref, lse_ref):
        kvv = kv_ref[...]
        for pr in range(N_MEM_HEADS // 2):
            lanes = slice(pr * PAIR, (pr + 1) * PAIR)
            k_raw = kvv[:, lanes]
            kn = k_raw * _pair_rms(k_raw) * gk_ref[...]
            v = kvv[:, MQ_COLS + pr * PAIR:MQ_COLS + (pr + 1) * PAIR]
            q_raw = q_ref[:, lanes]
            qn = q_raw * _pair_rms(q_raw) * gq_ref[...]
            for hh in range(2):
                hd = 2 * pr + hh
                hs = slice(hh * HEAD_DIM, (hh + 1) * HEAD_DIM)
                s = _nt(qn[:, hs].astype(BF), kn[:, hs].astype(BF)) * SCALE
                m = jnp.max(s, axis=-1, keepdims=True)
                p = jnp.exp(s - m)
                den = jnp.sum(p, axis=-1, keepdims=True)
                ym_ref[:, hd * HEAD_DIM:(hd + 1) * HEAD_DIM] = _nn((p / den).astype(BF), v[:, hs].astype(BF)).astype(BF)
                lse_ref[:, hd:hd + 1] = m + jnp.log(den)

    m_len = mkv.shape[0]
    one = pl.BlockSpec((1, PAIR), lambda t: (0, 0))
    return pl.pallas_call(
        body, out_shape=[jax.ShapeDtypeStruct((t_len, MQ_COLS), BF), jax.ShapeDtypeStruct((t_len, N_MEM_HEADS), F32)],
        grid=(t_len // tm,),
        in_specs=[pl.BlockSpec((tm, MQ_COLS), lambda t: (t, 5)), pl.BlockSpec((m_len, 2 * MQ_COLS), lambda t: (0, 0)), one, one],
        out_specs=[pl.BlockSpec((tm, MQ_COLS), lambda t: (t, 0)), pl.BlockSpec((tm, N_MEM_HEADS), lambda t: (t, 0))],
        name=name, compiler_params=_params(1))(proj, mkv, gq2, gk2)


def _mem_attn_bwd(proj, mkv, gq2, gk2, ym, dmix, lse, name):
    t_len = proj.shape[0]
    tm = min(256, t_len)
    nt = t_len // tm
    m_len = mkv.shape[0]

    def body(q_ref, kv_ref, gq_ref, gk_ref, ym_ref, do_ref, lse_ref, dq_ref, dkv_ref, dgq_ref, dgk_ref, dkn_scr, dv_scr, scr):
        t = pl.program_id(0)

        @pl.when(t == 0)
        def _():
            dkn_scr[...] = jnp.zeros_like(dkn_scr)
            dv_scr[...] = jnp.zeros_like(dv_scr)
            dgq_ref[...] = jnp.zeros_like(dgq_ref)

        kvv = kv_ref[...]
        for pr in range(N_MEM_HEADS // 2):
            lanes = slice(pr * PAIR, (pr + 1) * PAIR)
            k_raw = kvv[:, lanes]
            kn = k_raw * _pair_rms(k_raw) * gk_ref[...]
            v = kvv[:, MQ_COLS + pr * PAIR:MQ_COLS + (pr + 1) * PAIR]
            q_raw = q_ref[:, lanes]
            rq = _pair_rms(q_raw)
            qn = q_raw * rq * gq_ref[...]
            d_o = do_ref[:, lanes]
            dsum = _pair_sum(d_o * ym_ref[:, lanes].astype(F32))
            for hh in range(2):
                hd = 2 * pr + hh
                hs = slice(hh * HEAD_DIM, (hh + 1) * HEAD_DIM)
                cols = slice(hd * HEAD_DIM, (hd + 1) * HEAD_DIM)
                qh = qn[:, hs].astype(BF)
                kh = kn[:, hs].astype(BF)
                d_oh = d_o[:, hs].astype(BF)
                p = jnp.exp(_nt(qh, kh) * SCALE - lse_ref[:, hd:hd + 1])
                dv_scr[:, cols] += _tn(p.astype(BF), d_oh)
                ds = (p * (_nt(d_oh, v[:, hs].astype(BF)) - dsum[:, hh * HEAD_DIM:hh * HEAD_DIM + 1])).astype(BF)
                scr[pr, :, hs] = _nn(ds, kh) * SCALE
                dkn_scr[:, cols] += _tn(ds, qh) * SCALE
            dx, dg = _pair_rms_bwd(scr[pr], q_raw, rq, gq_ref[...])
            dq_ref[:, lanes] = dx.astype(BF)
            dgq_ref[...] += dg

        @pl.when(t == nt - 1)
        def _():
            dgk = jnp.zeros((1, PAIR), F32)
            for pr in range(N_MEM_HEADS // 2):
                lanes = slice(pr * PAIR, (pr + 1) * PAIR)
                k_raw = kvv[:, lanes]
                dx, dg = _pair_rms_bwd(dkn_scr[:, lanes], k_raw, _pair_rms(k_raw), gk_ref[...])
                dkv_ref[:, lanes] = dx
                dgk = dgk + dg
            dkv_ref[:, MQ_COLS:2 * MQ_COLS] = dv_scr[...]
            dgk_ref[...] = dgk

    one = pl.BlockSpec((1, PAIR), lambda t: (0, 0))
    kvspec = pl.BlockSpec((m_len, 2 * MQ_COLS), lambda t: (0, 0))
    qspec = pl.BlockSpec((tm, MQ_COLS), lambda t: (t, 0))
    return pl.pallas_call(
        body,
        out_shape=[jax.ShapeDtypeStruct((t_len, MQ_COLS), BF), jax.ShapeDtypeStruct((m_len, 2 * MQ_COLS), F32),
                   jax.ShapeDtypeStruct((1, PAIR), F32), jax.ShapeDtypeStruct((1, PAIR), F32)],
        grid=(nt,),
        in_specs=[pl.BlockSpec((tm, MQ_COLS), lambda t: (t, 5)), kvspec, one, one, qspec,
                  pl.BlockSpec((tm, MQ_COLS), lambda t: (t, 3)), pl.BlockSpec((tm, N_MEM_HEADS), lambda t: (t, 0))],
        out_specs=[qspec, kvspec, one, one],
        scratch_shapes=[pltpu.VMEM((m_len, MQ_COLS), F32), pltpu.VMEM((m_len, MQ_COLS), F32),
                        pltpu.VMEM((N_MEM_HEADS // 2, tm, PAIR), F32)],
        name=name, compiler_params=_params(1))(proj, mkv, gq2, gk2, ym, dmix, lse)


MIX_GROUPS = [(0, Q_COLS), (Q_COLS, CONV_CH), (Q_COLS + CONV_CH, MQ_COLS)]


def _out_proj_fwd(x, ya, yc, ym, w_out, name, after=None):
    t_len = x.shape[0]
    tm = min(512, t_len)

    def body(x_ref, ya_ref, yc_ref, ym_ref, w_ref, xo_ref):
        y = x_ref[...]
        for (start, width), ref in zip(MIX_GROUPS, (ya_ref, yc_ref, ym_ref)):
            y = y + _nn(ref[...], w_ref[start:start + width, :])
        xo_ref[...] = y

    cols = lambda w: pl.BlockSpec((tm, w), lambda t: (t, 0))
    body, first, first_specs = _ordered_after(body, after)
    return pl.pallas_call(
        body, out_shape=jax.ShapeDtypeStruct((t_len, D_MODEL), F32), grid=(t_len // tm,),
        in_specs=first_specs + [cols(D_MODEL), cols(Q_COLS), cols(CONV_CH), cols(MQ_COLS),
                                pl.BlockSpec((D_MODEL, D_MODEL), lambda t: (0, 0))],
        out_specs=cols(D_MODEL), name=name, compiler_params=_params(1))(*first, x, ya, yc, ym, w_out)


def _out_proj_bwd(dx, ya, yc, ym, w_out, name, after=None):
    t_len = dx.shape[0]
    tm = min(512, t_len)
    nt = t_len // tm

    def body(dx_ref, ya_ref, yc_ref, ym_ref, w_ref, dmix_ref, dw_ref, acc):
        t = pl.program_id(0)

        @pl.when(t == 0)
        def _():
            acc[...] = jnp.zeros_like(acc)

        dxb = dx_ref[...].astype(BF)
        dmix_ref[...] = _nt(dxb, w_ref[...])
        for (start, width), ref in zip(MIX_GROUPS, (ya_ref, yc_ref, ym_ref)):
            acc[start:start + width, :] += _tn(ref[...], dxb)

        @pl.when(t == nt - 1)
        def _():
            dw_ref[...] = acc[...].astype(BF)

    cols = lambda w: pl.BlockSpec((tm, w), lambda t: (t, 0))
    full = pl.BlockSpec((D_MODEL, D_MODEL), lambda t: (0, 0))
    body, first, first_specs = _ordered_after(body, after)
    return pl.pallas_call(
        body, out_shape=[jax.ShapeDtypeStruct((t_len, D_MODEL), F32), jax.ShapeDtypeStruct((D_MODEL, D_MODEL), BF)],
        grid=(nt,), in_specs=first_specs + [cols(D_MODEL), cols(Q_COLS), cols(CONV_CH), cols(MQ_COLS), full],
        out_specs=[cols(D_MODEL), full], scratch_shapes=[pltpu.VMEM((D_MODEL, D_MODEL), F32)], name=name,
        compiler_params=_params(1))(*first, dx, ya, yc, ym, w_out)


N_PEERS = N_DEV - 1
HBM_SPEC = pl.BlockSpec(memory_space=pltpu.HBM)
SEM_SPEC = pl.BlockSpec(memory_space=pltpu.SEMAPHORE)
EFFECT = pltpu.SideEffectType.DATAFLOW_SIDE_EFFECTING


def _my_place():
    x, y, c = lax.axis_index("x"), lax.axis_index("y"), lax.axis_index("c")
    return x, y, c, 4 * x + 2 * y + c


def _peer(k):
    x, y, c, _ = _my_place()
    px = 1 - x if k & 4 else x
    py = 1 - y if k & 2 else y
    pc = 1 - c if k & 1 else c
    return (px, py, pc), 4 * px + 2 * py + pc


def _gather_two_level(shards, name):
    n = len(shards)

    def body(*refs):
        ins, outs = refs[:n], refs[n:2 * n]
        send_sems, recv_sems, local_sems = refs[2 * n:]
        x, y, c, me = _my_place()
        sibling = (x, y, 1 - c)
        chips = [(1 - x, y), (x, 1 - y), (1 - x, 1 - y)]
        slot = lambda px, py, pc: 4 * px + 2 * py + pc

        def copy(w, k, src, dst_slot, to):
            return pltpu.make_async_remote_copy(
                src_ref=src, dst_ref=outs[w].at[dst_slot], send_sem=send_sems.at[w * N_PEERS + k],
                recv_sem=recv_sems.at[w * N_PEERS + k], device_id=to, device_id_type=pl.DeviceIdType.MESH)

        local = [pltpu.make_async_copy(ins[w], outs[w].at[me], local_sems.at[w]) for w in range(n)]
        for cp in local:
            cp.start()
        sends = []
        for w in range(n):
            for j, chip in enumerate(chips):
                sends.append(copy(w, 1 + j, ins[w], me, (*chip, c)))
            sends.append(copy(w, 0, ins[w], me, sibling))
        for cp in sends:
            cp.start()
        for w in range(n):
            for j, chip in enumerate(chips):
                got = slot(*chip, c)
                copy(w, 1 + j, ins[w], got, (*chip, c)).wait_recv()
                fwd = copy(w, 4 + j, outs[w].at[got], got, sibling)
                fwd.start()
                sends.append(fwd)
        for w in range(n):
            copy(w, 0, ins[w], slot(x, y, 1 - c), sibling).wait_recv()
            for j, chip in enumerate(chips):
                copy(w, 4 + j, ins[w], slot(*chip, 1 - c), sibling).wait_recv()
        for cp in sends:
            cp.wait_send()
        for cp in local:
            cp.wait()

    hbm = pl.BlockSpec(memory_space=pl.ANY)
    return pl.pallas_call(
        body, out_shape=[jax.ShapeDtypeStruct((N_DEV,) + a.shape, a.dtype) for a in shards], in_specs=[hbm] * n,
        out_specs=[hbm] * n,
        scratch_shapes=[pltpu.SemaphoreType.DMA((n * N_PEERS,)), pltpu.SemaphoreType.DMA((n * N_PEERS,)),
                        pltpu.SemaphoreType.DMA((n,))],
        name=name)(*shards)


def _split_copies(srcs, lands, gather, send_sems, recv_sems):
    _, _, _, me = _my_place()
    pairs = []
    for w in range(len(srcs)):
        for k in range(1, N_DEV):
            dev, idx = _peer(k)
            src = srcs[w] if gather[w] else srcs[w].at[idx]
            sems = dict(send_sem=send_sems.at[w * N_PEERS + k - 1], recv_sem=recv_sems.at[w * N_PEERS + k - 1], device_id=dev,
                        device_id_type=pl.DeviceIdType.MESH)
            pairs.append((pltpu.make_async_remote_copy(src_ref=src, dst_ref=lands[w].at[me], **sems),
                          pltpu.make_async_remote_copy(src_ref=src, dst_ref=lands[w].at[idx], **sems)))
    return pairs


def _exchange_start(srcs, lands, gather, after, name):
    n = len(srcs)

    def body(*refs):
        src_refs, land_refs = refs[:n], refs[n:2 * n]
        send_sems, recv_sems = refs[2 * n + 1], refs[2 * n + 2]
        token = refs[-1]
        for out_going, _ in _split_copies(src_refs, land_refs, gather, send_sems, recv_sems):
            out_going.start()
        token[...] = jnp.zeros_like(token)

    arrays = list(srcs) + list(lands)
    out = pl.pallas_call(
        body, name=name,
        out_shape=(pltpu.SemaphoreType.DMA((n * N_PEERS,)), pltpu.SemaphoreType.DMA((n * N_PEERS,)),
                   *[pltpu.HBM(a.shape, a.dtype) for a in arrays], jax.ShapeDtypeStruct((8, 128), F32)),
        in_specs=[HBM_SPEC] * (2 * n) + [pl.BlockSpec(memory_space=pl.ANY)],
        out_specs=(SEM_SPEC, SEM_SPEC, *[HBM_SPEC] * (2 * n), pl.BlockSpec(memory_space=pltpu.VMEM)),
        input_output_aliases={i: i + 2 for i in range(2 * n)},
        compiler_params=pltpu.CompilerParams(has_side_effects=EFFECT),
    )(*[pltpu.with_memory_space_constraint(a, pltpu.HBM) for a in arrays], after)
    return out[0], out[1], out[2:2 + n], out[2 + n:2 + 2 * n], out[-1]


def _exchange_wait(send_sems, recv_sems, srcs, lands, gather, after, name):
    n = len(srcs)

    def body(*refs):
        src_refs, land_refs = refs[:n], refs[n:2 * n]
        for out_going, arriving in _split_copies(src_refs, land_refs, gather, refs[2 * n], refs[2 * n + 1]):
            out_going.wait_send()
            arriving.wait_recv()

    arrays = list(srcs) + list(lands)
    out = pl.pallas_call(
        body, name=name, out_shape=tuple(pltpu.HBM(a.shape, a.dtype) for a in arrays),
        in_specs=[HBM_SPEC] * (2 * n) + [SEM_SPEC, SEM_SPEC, pl.BlockSpec(memory_space=pl.ANY)],
        out_specs=tuple([HBM_SPEC] * (2 * n)), input_output_aliases={i: i for i in range(2 * n)},
        compiler_params=pltpu.CompilerParams(has_side_effects=EFFECT),
    )(*arrays, send_sems, recv_sems, after)
    return out[n:]


SIBLING = 1
CHIP_PEERS = (2, 4, 6)
NEAR_PEERS = (SIBLING,) + CHIP_PEERS


def _near_copies(srcs, lands, send_sems, recv_sems):
    _, _, _, me = _my_place()
    pairs = []
    for w in range(len(srcs)):
        for i, k in enumerate(NEAR_PEERS):
            dev, idx = _peer(k)
            sems = dict(send_sem=send_sems.at[w * len(NEAR_PEERS) + i], recv_sem=recv_sems.at[w * len(NEAR_PEERS) + i],
                        device_id=dev, device_id_type=pl.DeviceIdType.MESH)
            pairs.append((pltpu.make_async_remote_copy(src_ref=srcs[w], dst_ref=lands[w].at[me], **sems),
                          pltpu.make_async_remote_copy(src_ref=srcs[w], dst_ref=lands[w].at[idx], **sems)))
    return pairs


def _forward_copies(lands, send_sems, recv_sems):
    sibling, _ = _peer(SIBLING)
    pairs = []
    for w in range(len(lands)):
        for i, k in enumerate(CHIP_PEERS):
            _, got = _peer(k)
            _, gets = _peer(k | SIBLING)
            sems = dict(send_sem=send_sems.at[w * len(CHIP_PEERS) + i], recv_sem=recv_sems.at[w * len(CHIP_PEERS) + i],
                        device_id=sibling, device_id_type=pl.DeviceIdType.MESH)
            pairs.append((pltpu.make_async_remote_copy(src_ref=lands[w].at[got], dst_ref=lands[w].at[got], **sems),
                          pltpu.make_async_remote_copy(src_ref=lands[w].at[got], dst_ref=lands[w].at[gets], **sems)))
    return pairs


def _split_call(body, name, arrays, sems_in, sems_out, after, token):
    n = len(arrays)
    out = pl.pallas_call(
        body, name=name,
        out_shape=(*[pltpu.SemaphoreType.DMA((c,)) for c in sems_out], *[pltpu.HBM(a.shape, a.dtype) for a in arrays],
                   *([jax.ShapeDtypeStruct((8, 128), F32)] if token else [])),
        in_specs=[HBM_SPEC] * n + [SEM_SPEC] * len(sems_in) + [pl.BlockSpec(memory_space=pl.ANY)],
        out_specs=(*[SEM_SPEC] * len(sems_out), *[HBM_SPEC] * n,
                   *([pl.BlockSpec(memory_space=pltpu.VMEM)] if token else [])),
        input_output_aliases={i: i + len(sems_out) for i in range(n)},
        compiler_params=pltpu.CompilerParams(has_side_effects=EFFECT),
    )(*[pltpu.with_memory_space_constraint(a, pltpu.HBM) for a in arrays], *sems_in, after)
    k = len(sems_out)
    return list(out[:k]), list(out[k:k + n]), (out[-1] if token else None)


def _gather_start(srcs, lands, after, name):
    n = len(srcs)

    def body(*refs):
        send1, recv1 = refs[2 * n + 1], refs[2 * n + 2]
        for out_going, _ in _near_copies(refs[:n], refs[n:2 * n], send1, recv1):
            out_going.start()
        refs[-1][...] = jnp.zeros_like(refs[-1])

    sems, arrays, token = _split_call(body, name, list(srcs) + list(lands), [], [n * len(NEAR_PEERS)] * 2, after, True)
    return sems, arrays, token


def _gather_relay(sems, arrays, after, name):
    n = len(arrays) // 2

    def body(*refs):
        send1, recv1 = refs[2 * n], refs[2 * n + 1]
        send2, recv2 = refs[2 * n + 3], refs[2 * n + 4]
        near = _near_copies(refs[:n], refs[n:2 * n], send1, recv1)
        forward = _forward_copies(refs[n:2 * n], send2, recv2)
        for w in range(n):
            for i in range(len(CHIP_PEERS)):
                near[w * len(NEAR_PEERS) + 1 + i][1].wait_recv()
                forward[w * len(CHIP_PEERS) + i][0].start()
        refs[-1][...] = jnp.zeros_like(refs[-1])

    sems2, arrays, token = _split_call(body, name, arrays, sems, [n * len(CHIP_PEERS)] * 2, after, True)
    return sems + sems2, arrays, token


def _gather_finish(sems, arrays, after, name):
    n = len(arrays) // 2

    def body(*refs):
        send1, recv1, send2, recv2 = refs[2 * n:2 * n + 4]
        near = _near_copies(refs[:n], refs[n:2 * n], send1, recv1)
        for out_going, _ in near:
            out_going.wait_send()
        for w in range(n):
            near[w * len(NEAR_PEERS)][1].wait_recv()
        for out_going, arriving in _forward_copies(refs[n:2 * n], send2, recv2):
            out_going.wait_send()
            arriving.wait_recv()

    _, arrays, _ = _split_call(body, name, arrays, sems, [], after, False)
    return arrays[n:]


def _own_slot(a, me, gather):
    mine = a if gather else lax.dynamic_index_in_dim(a, me, 0, keepdims=False)
    return lax.dynamic_update_index_in_dim(lax.empty((N_DEV,) + mine.shape, mine.dtype), mine, me, 0)


def _adamw_math(w, g, m, v):
    m2 = ADAM_B1 * m + (1.0 - ADAM_B1) * g
    v2 = ADAM_B2 * v + (1.0 - ADAM_B2) * (g * g)
    m_hat = m2 / (1.0 - ADAM_B1 ** ADAM_STEP)
    v_hat = v2 / (1.0 - ADAM_B2 ** ADAM_STEP)
    return -ADAM_LR * (m_hat / (jnp.sqrt(v_hat) + ADAM_EPS) + ADAM_WD * w), m2, v2


def _sum_adamw(parts, w, m, v, name):
    rows, cols = w.shape
    tr = rows if rows <= 512 else 256

    def body(p_ref, w_ref, m_ref, v_ref, g_ref, d_ref, m2_ref, v2_ref):
        g = p_ref[0].astype(F32)
        for s in range(1, N_DEV):
            g = g + p_ref[s].astype(F32)
        g_ref[...] = g
        d_ref[...], m2_ref[...], v2_ref[...] = _adamw_math(w_ref[...], g, m_ref[...], v_ref[...])

    blk = pl.BlockSpec((tr, cols), lambda i: (i, 0))
    shape = jax.ShapeDtypeStruct((rows, cols), F32)
    return pl.pallas_call(
        body, out_shape=[shape] * 4, grid=(rows // tr,),
        in_specs=[pl.BlockSpec((N_DEV, tr, cols), lambda i: (0, i, 0)), blk, blk, blk], out_specs=[blk] * 4, name=name,
        compiler_params=_params(1))(parts, w, m, v)


def _slot_sum(parts, name):
    def body(p_ref, g_ref):
        g = p_ref[0]
        for s in range(1, N_DEV):
            g = g + p_ref[s]
        g_ref[...] = g

    return pl.pallas_call(body, out_shape=jax.ShapeDtypeStruct(parts.shape[1:], F32), name=name)(parts)


def _adamw(g, w, m, v, name):
    def body(g_ref, w_ref, m_ref, v_ref, d_ref, m2_ref, v2_ref):
        d_ref[...], m2_ref[...], v2_ref[...] = _adamw_math(w_ref[...], g_ref[...], m_ref[...], v_ref[...])

    shape = jax.ShapeDtypeStruct(w.shape, F32)
    return pl.pallas_call(body, out_shape=[shape] * 3, name=name)(g, w, m, v)


def _pack_small(vals, last=None):
    flat = jnp.concatenate([vals[k].reshape(-1) for k in SMALL])
    tail = jnp.zeros((SMALL_ROWS * D_MODEL - flat.shape[0] - 1,), F32)
    last = jnp.zeros((1,), F32) if last is None else last.reshape(1)
    return jnp.concatenate([flat, tail, last]).reshape(SMALL_ROWS, D_MODEL)


def _unpack_small(packed, like):
    flat, out, at = packed.reshape(-1), {}, 0
    for k in SMALL:
        size = like[k].size
        out[k] = flat[at:at + size].reshape(like[k].shape)
        at += size
    return out


FFN1 = ["w_ffn1_gate", "w_ffn1_up", "w_ffn1_down"]
FFN2 = ["w_ffn2_gate", "w_ffn2_up", "w_ffn2_down"]
MIXER = ["w_in", "w_mem_kv", "w_out"]
TRANSPOSED = ["w_ffn1_gate", "w_ffn1_up", "w_ffn2_gate", "w_ffn2_up", "w_in"]


def _mixer_fwd(x1, mem, tables, sm, big, after_attn=None):
    cos_t, sin_t, gq2, gk2, gmq2, gmk2 = tables
    h2, proj = _in_proj_fwd(x1, sm["g_mix"], big["w_in"], "in_proj_fwd")
    ya, lse = _attn_fwd(proj, cos_t, sin_t, gq2, gk2, sm["sinks"], "attn_fwd")
    token = None if after_attn is None else after_attn(ya)
    yc, cv = _conv_fwd(proj, big["w_dw"], sm["b_dw"], sm["g_conv_ln"], sm["b_conv_ln"], "conv_fwd")
    hm, mkv = _mem_kv_fwd(mem, sm["g_mem"], big["w_mem_kv"], "mem_kv_fwd")
    ym, lse_m = _mem_attn_fwd(proj, mkv, gmq2, gmk2, "mem_attn_fwd")
    x2 = _out_proj_fwd(x1, ya, yc, ym, big["w_out"], "out_proj_fwd", token)
    return x2, (h2, proj, ya, lse, yc, cv, hm, mkv, ym, lse_m)


def _mixer_bwd(dx2, x1, mem, tables, sm, big, saved, after):
    cos_t, sin_t, gq2, gk2, gmq2, gmk2 = tables
    h2, proj, ya, lse, yc, cv, hm, mkv, ym, lse_m = saved
    g = {}
    dmix, g["w_out"] = _out_proj_bwd(dx2, ya, yc, ym, big["w_out"], "out_proj_bwd", after)
    dq, dgq, g["sinks"] = _attn_bwd_q(proj, cos_t, sin_t, gq2, gk2, sm["sinks"], ya, dmix, lse, "attn_bwd_q")
    dk, dv, dgk = _attn_bwd_kv(proj, cos_t, sin_t, gq2, gk2, ya, dmix, lse, "attn_bwd_kv")
    du, g["w_dw"], g["b_dw"], g["g_conv_ln"], g["b_conv_ln"] = _conv_bwd(
        proj, cv, dmix, big["w_dw"], sm["g_conv_ln"], sm["b_conv_ln"], "conv_bwd")
    dmq, dmkv, dgmq, dgmk = _mem_attn_bwd(proj, mkv, gmq2, gmk2, ym, dmix, lse_m, "mem_attn_bwd")
    g["w_mem_kv"], g["g_mem"] = _mem_kv_bwd(mem, sm["g_mem"], hm, big["w_mem_kv"], dmkv, "mem_kv_bwd")
    dx1, g["g_mix"], g["w_in"] = _in_proj_bwd(dq, dk, dv, du, dmq, big["w_in"], h2, x1, sm["g_mix"], dx2, "in_proj_bwd")
    fold = lambda a: a[:, :HEAD_DIM] + a[:, HEAD_DIM:]
    g["g_q"], g["g_k"], g["g_mq"], g["g_mk"] = fold(dgq), fold(dgk), fold(dgmq), fold(dgmk)
    return dx1, g


def _tables(positions, sm):
    pair = lambda a: jnp.tile(a, (1, 2))
    return _rope_tables(positions) + (pair(sm["g_q"]), pair(sm["g_k"]), pair(sm["g_mq"]), pair(sm["g_mk"]))


def kernel(x, mem, positions, g_ffn1, w_ffn1_gate, w_ffn1_up, w_ffn1_down, g_mix, w_in, g_q, g_k, sinks, w_dw, b_dw, g_conv_ln, b_conv_ln, g_mem, w_mem_kv, g_mq, g_mk, w_out, g_ffn2, w_ffn2_gate, w_ffn2_up, w_ffn2_down, loss_target, m_g_ffn1, m_w_ffn1_gate, m_w_ffn1_up, m_w_ffn1_down, m_g_mix, m_w_in, m_g_q, m_g_k, m_sinks, m_w_dw, m_b_dw, m_g_conv_ln, m_b_conv_ln, m_g_mem, m_w_mem_kv, m_g_mq, m_g_mk, m_w_out, m_g_ffn2, m_w_ffn2_gate, m_w_ffn2_up, m_w_ffn2_down, v_g_ffn1, v_w_ffn1_gate, v_w_ffn1_up, v_w_ffn1_down, v_g_mix, v_w_in, v_g_q, v_g_k, v_sinks, v_w_dw, v_b_dw, v_g_conv_ln, v_b_conv_ln, v_g_mem, v_w_mem_kv, v_g_mq, v_g_mk, v_w_out, v_g_ffn2, v_w_ffn2_gate, v_w_ffn2_up, v_w_ffn2_down):
    w = dict(g_ffn1=g_ffn1, w_ffn1_gate=w_ffn1_gate, w_ffn1_up=w_ffn1_up, w_ffn1_down=w_ffn1_down, g_mix=g_mix, w_in=w_in,
             g_q=g_q, g_k=g_k, sinks=sinks, w_dw=w_dw, b_dw=b_dw, g_conv_ln=g_conv_ln, b_conv_ln=b_conv_ln, g_mem=g_mem,
             w_mem_kv=w_mem_kv, g_mq=g_mq, g_mk=g_mk, w_out=w_out, g_ffn2=g_ffn2, w_ffn2_gate=w_ffn2_gate,
             w_ffn2_up=w_ffn2_up, w_ffn2_down=w_ffn2_down)
    mo = dict(g_ffn1=m_g_ffn1, w_ffn1_gate=m_w_ffn1_gate, w_ffn1_up=m_w_ffn1_up, w_ffn1_down=m_w_ffn1_down, g_mix=m_g_mix,
              w_in=m_w_in, g_q=m_g_q, g_k=m_g_k, sinks=m_sinks, w_dw=m_w_dw, b_dw=m_b_dw, g_conv_ln=m_g_conv_ln,
              b_conv_ln=m_b_conv_ln, g_mem=m_g_mem, w_mem_kv=m_w_mem_kv, g_mq=m_g_mq, g_mk=m_g_mk, w_out=m_w_out,
              g_ffn2=m_g_ffn2, w_ffn2_gate=m_w_ffn2_gate, w_ffn2_up=m_w_ffn2_up, w_ffn2_down=m_w_ffn2_down)
    vo = dict(g_ffn1=v_g_ffn1, w_ffn1_gate=v_w_ffn1_gate, w_ffn1_up=v_w_ffn1_up, w_ffn1_down=v_w_ffn1_down, g_mix=v_g_mix,
              w_in=v_w_in, g_q=v_g_q, g_k=v_g_k, sinks=v_sinks, w_dw=v_w_dw, b_dw=v_b_dw, g_conv_ln=v_g_conv_ln,
              b_conv_ln=v_b_conv_ln, g_mem=v_g_mem, w_mem_kv=v_w_mem_kv, g_mq=v_g_mq, g_mk=v_g_mk, w_out=v_w_out,
              g_ffn2=v_g_ffn2, w_ffn2_gate=v_w_ffn2_gate, w_ffn2_up=v_w_ffn2_up, w_ffn2_down=v_w_ffn2_down)
    me = _my_place()[3]
    sm = {k: w[k] for k in SMALL}
    flip = lambda k, a: a.T if k in TRANSPOSED else a
    as_bf16 = lambda names: [flip(k, w[k][0]).astype(BF) for k in names]
    zones = lambda arrays, gather: [_own_slot(a, me, gather) for a in arrays]
    out_g, out_d, out_m, out_v = {}, {}, {}, {}

    def update(names, parts_list):
        for k, parts in zip(names, parts_list):
            new = _sum_adamw(parts, flip(k, w[k][0]), flip(k, mo[k][0]), flip(k, vo[k][0]), "adamw_" + k)
            out_g[k], out_d[k], out_m[k], out_v[k] = [flip(k, a)[None] for a in new]

    w1 = _gather_two_level(as_bf16(FFN1), "gather_ffn1")
    mix_src = as_bf16(MIXER) + [w["w_dw"][0]]
    mix_sems, mix_arrays, mix_token = _gather_start(mix_src, zones(mix_src, True), w1[0], "gather_mixer_start")
    f2_src = as_bf16(FFN2)
    f2_sems, f2_arrays, f2_token = _gather_start(f2_src, zones(f2_src, True), mix_token, "gather_ffn2_start")

    tables = _tables(positions[0], sm)
    h1, gate1, up1, x1 = _ffn_fwd(x[0], sm["g_ffn1"], *w1, None, "ffn1_fwd", after=f2_token)
    mix_sems, mix_arrays, mix_token = _gather_relay(mix_sems, mix_arrays, x1, "gather_mixer_relay")
    got = _gather_finish(mix_sems, mix_arrays, mix_token, "gather_mixer_finish")
    big = dict(w_in=got[0].reshape(IN_COLS, D_MODEL), w_mem_kv=got[1].reshape(D_MODEL, 2 * MQ_COLS),
               w_out=got[2].reshape(D_MODEL, D_MODEL), w_dw=got[3].transpose(1, 0, 2).reshape(CONV_WIDTH, CONV_CH))
    relayed = []

    def relay_ffn2(ya):
        relayed.extend(_gather_relay(f2_sems, f2_arrays, ya, "gather_ffn2_relay"))
        return relayed[2]

    x2, saved = _mixer_fwd(x1, mem[0], tables, sm, big, relay_ffn2)
    w2 = _gather_finish(relayed[0], relayed[1], x2, "gather_ffn2_finish")
    h3, gate2, up2, dy, loss_part = _ffn_fwd(x2, sm["g_ffn2"], *w2, loss_target[0], "ffn2_fwd")

    grads = {}
    dyb2, act2, dgate2, dup2, dx2, grads["g_ffn2"] = _ffn_bwd_act(dy, x2, sm["g_ffn2"], gate2, up2, *w2, "ffn2_bwd_act")
    g_f2 = list(_ffn_bwd_w(h3, dyb2, act2, dgate2, dup2, "ffn2_bwd_w"))
    r_f2 = _exchange_start(g_f2, zones(g_f2, False), [False] * 3, dx2, "scatter_ffn2_start")
    dx1, g_mid = _mixer_bwd(dx2, x1, mem[0], tables, sm, big, saved, r_f2[4])
    grads.update(g_mid)
    g_mix = [g_mid["w_in"].reshape(N_DEV, IN_COLS // N_DEV, D_MODEL),
             g_mid["w_mem_kv"].reshape(N_DEV, D_MODEL // N_DEV, 2 * MQ_COLS),
             g_mid["w_out"].reshape(N_DEV, D_MODEL // N_DEV, D_MODEL)]
    r_mix = _exchange_start(g_mix, zones(g_mix, False), [False] * 3, dx1, "scatter_mixer_start")
    dyb1, act1, dgate1, dup1, grad_x, grads["g_ffn1"] = _ffn_bwd_act(dx1, x[0], sm["g_ffn1"], gate1, up1, *w1,
                                                                     "ffn1_bwd_act", after=r_mix[4])
    g_f1 = list(_ffn_bwd_w(h1, dyb1, act1, dgate1, dup1, "ffn1_bwd_w"))
    dw_flat = grads["w_dw"].reshape(-1)
    packed = jnp.concatenate([_pack_small(grads, loss_part[0, 0]),
                              jnp.pad(dw_flat, (0, SMALL_ROWS * D_MODEL - dw_flat.shape[0])).reshape(SMALL_ROWS, D_MODEL)])
    last_src, last_kind = g_f1 + [packed], [False] * 3 + [True]
    r_f1 = _exchange_start(last_src, zones(g_f1, False) + zones([packed], True), last_kind, grad_x, "scatter_ffn1_start")

    all_done = lambda names: sum(out_d[k][:, :1, :1] for k in names)
    update(FFN2, _exchange_wait(*r_f2[:4], [False] * 3, r_f1[4], "scatter_ffn2_wait"))
    update(MIXER, _exchange_wait(*r_mix[:4], [False] * 3, all_done(FFN2), "scatter_mixer_wait"))
    last = _exchange_wait(*r_f1[:4], last_kind, all_done(MIXER), "scatter_ffn1_wait")
    update(FFN1, last[:3])
    small_sum = _slot_sum(last[3], "small_grad_sum")
    loss = small_sum[SMALL_ROWS - 1, D_MODEL - 1]
    g_small = small_sum[:SMALL_ROWS]
    d, m2, v2 = _adamw(g_small, _pack_small(w), _pack_small(mo), _pack_small(vo), "adamw_small")
    for dst, val in ((out_g, g_small), (out_d, d), (out_m, m2), (out_v, v2)):
        dst.update(_unpack_small(val, sm))
    g_dw = small_sum[SMALL_ROWS:].reshape(-1)[:CONV_WIDTH * CONV_CH].reshape(CONV_WIDTH, CONV_CH)
    g_dw = lax.dynamic_slice_in_dim(g_dw, me * (CONV_CH // N_DEV), CONV_CH // N_DEV, axis=1)
    d, m2, v2 = _adamw(g_dw, w["w_dw"][0], mo["w_dw"][0], vo["w_dw"][0], "adamw_w_dw")
    out_g["w_dw"], out_d["w_dw"], out_m["w_dw"], out_v["w_dw"] = g_dw[None], d[None], m2[None], v2[None]

    return (loss, grad_x[None], *[out_g[k] for k in WEIGHTS], *[out_d[k] for k in WEIGHTS], *[out_m[k] for k in WEIGHTS],
            *[out_v[k] for k in WEIGHTS])
```

```python
import jax
import jax.numpy as jnp
from jax import lax
from jax.experimental import pallas as pl
from jax.experimental.pallas import tpu as pltpu

D_MODEL = 1024
N_DEV = 8
FF_BLOCK = 352
D_FF = N_DEV * FF_BLOCK
FF_TILE = 256
N_FF_TILES = D_FF // FF_TILE
FF_ROWS = 1024
FF_ROWS_W = 2048
FF_CHUNK = 256
HEAD_DIM = 64
PAIR = 2 * HEAD_DIM
N_Q_HEADS = 8
Q_PER_KV = 4
ATTN_BLOCK = 128
Q_COLS = 512
KV_COLS = 128
CONV_CH = 256
MQ_COLS = 256
IN_COLS = 1536
CONV_WIDTH = 31
CONV_PAD = 32
CONV_CHUNK = 256
N_MEM_HEADS = 4
ROPE_THETA = 500000.0
ROPE_HALF = 8
EPS = 1e-6
SCALE = HEAD_DIM ** -0.5
NEG = -1e30
ADAM_LR, ADAM_B1, ADAM_B2, ADAM_EPS, ADAM_WD, ADAM_STEP = 0.001, 0.9, 0.999, 1e-08, 0.01, 10
VMEM_LIMIT_BYTES = 56 * 1024 * 1024
BF = jnp.bfloat16
F32 = jnp.float32

SMALL = ["g_ffn1", "g_mix", "g_mem", "g_ffn2", "g_q", "g_k", "g_mq", "g_mk", "sinks", "b_dw", "g_conv_ln", "b_conv_ln"]
WEIGHTS = ["g_ffn1", "w_ffn1_gate", "w_ffn1_up", "w_ffn1_down", "g_mix", "w_in", "g_q", "g_k", "sinks", "w_dw", "b_dw",
           "g_conv_ln", "b_conv_ln", "g_mem", "w_mem_kv", "g_mq", "g_mk", "w_out", "g_ffn2", "w_ffn2_gate", "w_ffn2_up",
           "w_ffn2_down"]
SMALL_ROWS = 8
PACK_ROWS = 16


def _nn(a, b):
    return jnp.dot(a, b, preferred_element_type=F32)


def _nt(a, b):
    return lax.dot_general(a, b, (((1,), (1,)), ((), ())), preferred_element_type=F32)


def _tn(a, b):
    return lax.dot_general(a, b, (((0,), (0,)), ((), ())), preferred_element_type=F32)


def _params(n_grid):
    return pltpu.CompilerParams(dimension_semantics=("arbitrary",) * n_grid, vmem_limit_bytes=VMEM_LIMIT_BYTES)


def _ordered_after(body, after):
    if after is None:
        return body, (), []

    def body_after(after_ref, *refs):
        body(*refs)

    return body_after, (after,), [pl.BlockSpec(memory_space=pl.ANY)]


def _row_rms(xv):
    return lax.rsqrt(jnp.mean(xv * xv, axis=-1, keepdims=True) + EPS)


def _rms_bwd(dh, xv, r, g):
    u = dh * g
    dx = r * u - xv * (r * r * r) * jnp.mean(u * xv, axis=-1, keepdims=True)
    return dx, jnp.sum(dh * xv * r, axis=0, keepdims=True)


def _sum_all(a):
    return jnp.sum(jnp.sum(a, axis=1, keepdims=True), axis=0, keepdims=True)


def _pair_sums(vs):
    row = lax.broadcasted_iota(jnp.int32, (PAIR, PAIR), 0) >= HEAD_DIM
    col = lax.broadcasted_iota(jnp.int32, (PAIR, PAIR), 1) >= HEAD_DIM
    same_head = (row == col).astype(BF)
    stacked = jnp.concatenate(vs, axis=0) if len(vs) > 1 else vs[0]
    hi = stacked.astype(BF)
    lo = (stacked - hi.astype(F32)).astype(BF)
    n = stacked.shape[0]
    both = _nn(jnp.concatenate([hi, lo], axis=0), same_head)
    total = both[:n] + both[n:]
    out, at = [], 0
    for v in vs:
        out.append(total[at:at + v.shape[0]])
        at += v.shape[0]
    return out


def _pair_sum(v):
    return _pair_sums([v])[0]


def _bf16_terms(v, n):
    terms, rest = [], v
    for _ in range(n):
        t = rest.astype(BF)
        terms.append(t)
        rest = rest - t.astype(F32)
    return terms


def _as_rows(cols):
    k = cols.shape[1]
    eye = (lax.broadcasted_iota(jnp.int32, (k, k), 0) == lax.broadcasted_iota(jnp.int32, (k, k), 1)).astype(BF)
    return sum(_nt(eye, t) for t in _bf16_terms(cols, 3))


def _head_sums_as_rows(v):
    sel = (lax.broadcasted_iota(jnp.int32, (8, PAIR), 0)
           == (lax.broadcasted_iota(jnp.int32, (8, PAIR), 1) >= HEAD_DIM).astype(jnp.int32)).astype(BF)
    return sum(_nt(sel, t) for t in _bf16_terms(v, 2))


def _pair_rms_many(xs):
    return [lax.rsqrt(s * (1.0 / HEAD_DIM) + EPS) for s in _pair_sums([xv * xv for xv in xs])]


def _pair_rms(xv):
    return _pair_rms_many([xv])[0]


def _pair_rms_bwd_many(dxns, xs, rs, g):
    us = [dxn * g for dxn in dxns]
    sums = _pair_sums([u * xv for u, xv in zip(us, xs)])
    return [(r * u - xv * (r * r * r) * (s * (1.0 / HEAD_DIM)), jnp.sum(dxn * xv * r, axis=0, keepdims=True))
            for dxn, xv, r, u, s in zip(dxns, xs, rs, us, sums)]


def _pair_rms_bwd(dxn, xv, r, g):
    return _pair_rms_bwd_many([dxn], [xv], [r], g)[0]


def _rope_mask(shape):
    lane = lax.broadcasted_iota(jnp.int32, shape, 1)
    return ((lane & (HEAD_DIM - 1)) < 2 * ROPE_HALF).astype(F32)


def _partner(v):
    return pltpu.roll(v, ROPE_HALF, 1) + pltpu.roll(v, PAIR - ROPE_HALF, 1)


def _rope(xn, cos_t, sin_t):
    return xn * cos_t + _partner(xn * _rope_mask(xn.shape)) * sin_t


def _rope_t(d, cos_t, sin_t):
    return d * cos_t + _partner(d * sin_t) * _rope_mask(d.shape)


def _rope_tables(positions):
    inv_freq = ROPE_THETA ** (-jnp.arange(ROPE_HALF, dtype=F32) / ROPE_HALF)
    ang = positions.astype(F32)[:, None] * inv_freq
    cos, sin = jnp.cos(ang), jnp.sin(ang)
    t = positions.shape[0]
    cos_h = jnp.concatenate([cos, cos, jnp.ones((t, HEAD_DIM - 2 * ROPE_HALF), F32)], axis=1)
    sin_h = jnp.concatenate([-sin, sin, jnp.zeros((t, HEAD_DIM - 2 * ROPE_HALF), F32)], axis=1)
    return jnp.tile(cos_h, (1, 2)), jnp.tile(sin_h, (1, 2))


def _ff_rows(w):
    return w.reshape(D_FF, D_MODEL)


def _ffn_fwd(x, g, wg, wu, wd, target, name, after=None):
    t_len = x.shape[0]
    tm = min(FF_ROWS, t_len)
    with_loss = target is not None

    def body(*refs):
        if with_loss:
            x_ref, g_ref, wg_ref, wu_ref, wd_ref, t_ref, h_ref, gg_ref, uu_ref, dy_ref, loss_ref, acc = refs
        else:
            x_ref, g_ref, wg_ref, wu_ref, wd_ref, h_ref, gg_ref, uu_ref, xo_ref, acc = refs
        t, j = pl.program_id(0), pl.program_id(1)

        @pl.when(j == 0)
        def _():
            xv = x_ref[...]
            h_ref[...] = (xv * _row_rms(xv) * g_ref[...]).astype(BF)
            acc[...] = jnp.zeros_like(acc)

        h = h_ref[...]
        gate = _nt(h, wg_ref[...])
        up = _nt(h, wu_ref[...])
        gg_ref[...] = gate.astype(BF)
        uu_ref[...] = up.astype(BF)
        act = (gate * jax.nn.sigmoid(gate) * up).astype(BF)
        acc[...] += _nn(act, wd_ref[...])

        @pl.when(j == N_FF_TILES - 1)
        def _():
            xo = x_ref[...] + 0.5 * acc[...]
            if with_loss:
                err = xo - t_ref[...]
                dy_ref[...] = err * (1.0 / D_MODEL)

                @pl.when(t == 0)
                def _():
                    loss_ref[...] = jnp.zeros_like(loss_ref)

                loss_ref[...] += _sum_all(err * err) * (0.5 / D_MODEL)
            else:
                xo_ref[...] = xo

    row = pl.BlockSpec((tm, D_MODEL), lambda t, j: (t, 0))
    vec = pl.BlockSpec((1, D_MODEL), lambda t, j: (0, 0))
    w_spec = pl.BlockSpec((FF_TILE, D_MODEL), lambda t, j: (j, 0))
    blk = pl.BlockSpec((tm, FF_TILE), lambda t, j: (t, j))
    in_specs = [row, vec, w_spec, w_spec, w_spec] + ([row] if with_loss else [])
    out_shape = [jax.ShapeDtypeStruct((t_len, D_MODEL), BF),
                 jax.ShapeDtypeStruct((t_len, D_FF), BF),
                 jax.ShapeDtypeStruct((t_len, D_FF), BF),
                 jax.ShapeDtypeStruct((t_len, D_MODEL), F32)]
    out_specs = [row, blk, blk, row]
    if with_loss:
        out_shape.append(jax.ShapeDtypeStruct((1, 128), F32))
        out_specs.append(pl.BlockSpec((1, 128), lambda t, j: (0, 0)))
    args = (x, g, _ff_rows(wg), _ff_rows(wu), _ff_rows(wd)) + ((target,) if with_loss else ())
    body, first, first_specs = _ordered_after(body, after)
    return pl.pallas_call(body, out_shape=out_shape, grid=(t_len // tm, N_FF_TILES), in_specs=first_specs + in_specs,
                          out_specs=out_specs, scratch_shapes=[pltpu.VMEM((tm, D_MODEL), F32)], name=name,
                          compiler_params=_params(2))(*first, *args)


def _ffn_bwd_act(dy, x, g, gate, up, wg, wu, wd, name, after=None):
    t_len = x.shape[0]
    tm = min(FF_ROWS, t_len)

    def body(dy_ref, x_ref, g_ref, gg_ref, uu_ref, wg_ref, wu_ref, wd_ref,
             dyb_ref, act_ref, dgg_ref, duu_ref, dx_ref, dg_ref, acc):
        t, j = pl.program_id(0), pl.program_id(1)

        @pl.when(j == 0)
        def _():
            dyb_ref[...] = (0.5 * dy_ref[...]).astype(BF)
            acc[...] = jnp.zeros_like(acc)

        w_both = jnp.concatenate([wg_ref[...], wu_ref[...]], axis=0)
        for c in range(tm // FF_CHUNK):
            rows = slice(c * FF_CHUNK, (c + 1) * FF_CHUNK)
            d_act = _nt(dyb_ref[rows, :], wd_ref[...])
            gate = gg_ref[rows, :].astype(F32)
            upv = uu_ref[rows, :].astype(F32)
            sig = jax.nn.sigmoid(gate)
            silu = gate * sig
            d_up = (d_act * silu).astype(BF)
            d_gate = (d_act * upv * (sig * (1.0 + gate * (1.0 - sig)))).astype(BF)
            act_ref[rows, :] = (silu * upv).astype(BF)
            dgg_ref[rows, :] = d_gate
            duu_ref[rows, :] = d_up
            acc[rows, :] += _nn(jnp.concatenate([d_gate, d_up], axis=1), w_both)

        @pl.when(j == N_FF_TILES - 1)
        def _():
            xv = x_ref[...]
            dx, dg = _rms_bwd(acc[...], xv, _row_rms(xv), g_ref[...])
            dx_ref[...] = dy_ref[...] + dx

            @pl.when(t == 0)
            def _():
                dg_ref[...] = jnp.zeros_like(dg_ref)

            dg_ref[...] += dg

    row = pl.BlockSpec((tm, D_MODEL), lambda t, j: (t, 0))
    vec = pl.BlockSpec((1, D_MODEL), lambda t, j: (0, 0))
    w_spec = pl.BlockSpec((FF_TILE, D_MODEL), lambda t, j: (j, 0))
    blk = pl.BlockSpec((tm, FF_TILE), lambda t, j: (t, j))
    blk_shape = jax.ShapeDtypeStruct((t_len, D_FF), BF)
    body, first, first_specs = _ordered_after(body, after)
    return pl.pallas_call(
        body,
        out_shape=[jax.ShapeDtypeStruct((t_len, D_MODEL), BF), blk_shape, blk_shape, blk_shape,
                   jax.ShapeDtypeStruct((t_len, D_MODEL), F32), jax.ShapeDtypeStruct((1, D_MODEL), F32)],
        grid=(t_len // tm, N_FF_TILES), in_specs=first_specs + [row, row, vec, blk, blk, w_spec, w_spec, w_spec],
        out_specs=[row, blk, blk, blk, row, vec], scratch_shapes=[pltpu.VMEM((tm, D_MODEL), F32)], name=name,
        compiler_params=_params(2))(*first, dy, x, g, gate, up, _ff_rows(wg), _ff_rows(wu), _ff_rows(wd))


def _ffn_bwd_w(h, dyb, act, d_gate, d_up, name):
    t_len = h.shape[0]
    tm = min(FF_ROWS_W, t_len)
    nt = t_len // tm

    def body(h_ref, dyb_ref, act_ref, dgg_ref, duu_ref, dwg_ref, dwu_ref, dwd_ref, ag, au, ad):
        t = pl.program_id(1)

        @pl.when(t == 0)
        def _():
            ag[...] = jnp.zeros_like(ag)
            au[...] = jnp.zeros_like(au)
            ad[...] = jnp.zeros_like(ad)

        h = h_ref[...]
        ag[...] += _tn(dgg_ref[...], h)
        au[...] += _tn(duu_ref[...], h)
        ad[...] += _tn(act_ref[...], dyb_ref[...])

        @pl.when(t == nt - 1)
        def _():
            dwg_ref[...] = ag[...].astype(BF)
            dwu_ref[...] = au[...].astype(BF)
            dwd_ref[...] = ad[...].astype(BF)

    row = pl.BlockSpec((tm, D_MODEL), lambda j, t: (t, 0))
    blk = pl.BlockSpec((tm, FF_TILE), lambda j, t: (t, j))
    w_spec = pl.BlockSpec((FF_TILE, D_MODEL), lambda j, t: (j, 0))
    grads = pl.pallas_call(
        body,
        out_shape=[jax.ShapeDtypeStruct((D_FF, D_MODEL), BF)] * 3,
        grid=(N_FF_TILES, nt), in_specs=[row, row, blk, blk, blk], out_specs=[w_spec] * 3,
        scratch_shapes=[pltpu.VMEM((FF_TILE, D_MODEL), F32)] * 3,
        name=name, compiler_params=_params(2))(h, dyb, act, d_gate, d_up)
    return [a.reshape(N_DEV, FF_BLOCK, D_MODEL) for a in grads]


def _in_proj_fwd(x, g, w_in, name):
    t_len = x.shape[0]
    tm = min(512, t_len)

    def body(x_ref, g_ref, w_ref, h_ref, p_ref):
        xv = x_ref[...]
        h = (xv * _row_rms(xv) * g_ref[...]).astype(BF)
        h_ref[...] = h
        p_ref[...] = _nt(h, w_ref[...])

    row = pl.BlockSpec((tm, D_MODEL), lambda t: (t, 0))
    return pl.pallas_call(
        body, out_shape=[jax.ShapeDtypeStruct((t_len, D_MODEL), BF), jax.ShapeDtypeStruct((t_len, IN_COLS), F32)],
        grid=(t_len // tm,),
        in_specs=[row, pl.BlockSpec((1, D_MODEL), lambda t: (0, 0)), pl.BlockSpec((IN_COLS, D_MODEL), lambda t: (0, 0))],
        out_specs=[row, pl.BlockSpec((tm, IN_COLS), lambda t: (t, 0))], name=name, compiler_params=_params(1))(x, g, w_in)


def _in_proj_bwd(dq, dk, dv, du, dmq, w_in, h, x, g, dres, name):
    t_len = x.shape[0]
    tm = min(512, t_len)
    nt = t_len // tm
    groups = [(0, Q_COLS), (Q_COLS, KV_COLS), (Q_COLS + KV_COLS, KV_COLS), (Q_COLS + 2 * KV_COLS, 2 * CONV_CH),
              (Q_COLS + 2 * KV_COLS + 2 * CONV_CH, MQ_COLS)]

    def body(dq_ref, dk_ref, dv_ref, du_ref, dmq_ref, w_ref, h_ref, x_ref, g_ref, dres_ref, dx_ref, dg_ref, dw_ref, acc):
        t = pl.program_id(0)

        @pl.when(t == 0)
        def _():
            acc[...] = jnp.zeros_like(acc)
            dg_ref[...] = jnp.zeros_like(dg_ref)

        h = h_ref[...]
        dh = jnp.zeros((tm, D_MODEL), F32)
        for (start, width), ref in zip(groups, (dq_ref, dk_ref, dv_ref, du_ref, dmq_ref)):
            piece = ref[...]
            dh = dh + _nn(piece, w_ref[start:start + width, :])
            acc[start:start + width, :] += _tn(piece, h)
        xv = x_ref[...]
        dx, dg = _rms_bwd(dh, xv, _row_rms(xv), g_ref[...])
        dx_ref[...] = dres_ref[...] + dx
        dg_ref[...] += dg

        @pl.when(t == nt - 1)
        def _():
            dw_ref[...] = acc[...].astype(BF)

    def cols(width):
        return pl.BlockSpec((tm, width), lambda t: (t, 0))

    row = cols(D_MODEL)
    vec = pl.BlockSpec((1, D_MODEL), lambda t: (0, 0))
    full = pl.BlockSpec((IN_COLS, D_MODEL), lambda t: (0, 0))
    return pl.pallas_call(
        body,
        out_shape=[jax.ShapeDtypeStruct((t_len, D_MODEL), F32), jax.ShapeDtypeStruct((1, D_MODEL), F32),
                   jax.ShapeDtypeStruct((IN_COLS, D_MODEL), BF)],
        grid=(nt,),
        in_specs=[cols(Q_COLS), cols(KV_COLS), cols(KV_COLS), cols(2 * CONV_CH), cols(MQ_COLS), full, row, row, vec, row],
        out_specs=[row, vec, full], scratch_shapes=[pltpu.VMEM((IN_COLS, D_MODEL), F32)], name=name,
        compiler_params=_params(1))(dq, dk, dv, du, dmq, w_in, h, x, g, dres)


def _attn_fwd(proj, cos_t, sin_t, gq2, gk2, sinks, name):
    t_len = proj.shape[0]
    nb = t_len // ATTN_BLOCK

    def body(q_ref, kvc_ref, kvp_ref, cq_ref, sq_ref, cp_ref, sp_ref, gq_ref, gk_ref, sk_ref, ya_ref, lse_ref):
        n = pl.program_id(0)
        cq, sq = cq_ref[...], sq_ref[...]
        c_all = jnp.concatenate([cp_ref[...], cq], axis=0)
        s_all = jnp.concatenate([sp_ref[...], sq], axis=0)
        kv = jnp.concatenate([kvp_ref[...], kvc_ref[...]], axis=0)
        k_raw, v = kv[:, :PAIR], kv[:, PAIR:]
        q_raws = [q_ref[:, pr * PAIR:(pr + 1) * PAIR] for pr in range(N_Q_HEADS // 2)]
        rms = _pair_rms_many(q_raws + [k_raw])
        kn = _rope(k_raw * rms[-1] * gk_ref[...], c_all, s_all)
        row = lax.broadcasted_iota(jnp.int32, (ATTN_BLOCK, 2 * ATTN_BLOCK), 0)
        col = lax.broadcasted_iota(jnp.int32, (ATTN_BLOCK, 2 * ATTN_BLOCK), 1)
        rel = row + ATTN_BLOCK - col
        valid = (rel >= 0) & (rel < ATTN_BLOCK) & ((col >= ATTN_BLOCK) | (n > 0))
        for pr in range(N_Q_HEADS // 2):
            qn = _rope(q_raws[pr] * rms[pr] * gq_ref[...], cq, sq)
            kvh = (2 * pr) // Q_PER_KV
            kh = kn[:, kvh * HEAD_DIM:(kvh + 1) * HEAD_DIM].astype(BF)
            v_ones = jnp.concatenate([v[:, kvh * HEAD_DIM:(kvh + 1) * HEAD_DIM],
                                      jnp.ones((2 * ATTN_BLOCK, HEAD_DIM), F32)], axis=1).astype(BF)
            for hh in range(2):
                hd = 2 * pr + hh
                qh = qn[:, hh * HEAD_DIM:(hh + 1) * HEAD_DIM].astype(BF)
                s = jnp.where(valid, _nt(qh, kh) * SCALE, NEG)
                sink = sk_ref[0:1, hd:hd + 1]
                m = jnp.maximum(jnp.max(s, axis=-1, keepdims=True), sink)
                ov = _nn(jnp.exp(s - m).astype(BF), v_ones)
                den = ov[:, HEAD_DIM:HEAD_DIM + 1] + jnp.exp(sink - m)
                ya_ref[:, hd * HEAD_DIM:(hd + 1) * HEAD_DIM] = (ov[:, :HEAD_DIM] / den).astype(BF)
                lse_ref[:, hd:hd + 1] = m + jnp.log(den)

    prev = lambda n: (jnp.maximum(n - 1, 0), 0)
    tab = pl.BlockSpec((ATTN_BLOCK, PAIR), lambda n: (n, 0))
    tab_p = pl.BlockSpec((ATTN_BLOCK, PAIR), prev)
    one = lambda w: pl.BlockSpec((1, w), lambda n: (0, 0))
    return pl.pallas_call(
        body, out_shape=[jax.ShapeDtypeStruct((t_len, Q_COLS), BF), jax.ShapeDtypeStruct((t_len, N_Q_HEADS), F32)],
        grid=(nb,),
        in_specs=[pl.BlockSpec((ATTN_BLOCK, Q_COLS), lambda n: (n, 0)),
                  pl.BlockSpec((ATTN_BLOCK, 2 * KV_COLS), lambda n: (n, 2)),
                  pl.BlockSpec((ATTN_BLOCK, 2 * KV_COLS), lambda n: (jnp.maximum(n - 1, 0), 2)),
                  tab, tab, tab_p, tab_p, one(PAIR), one(PAIR), one(N_Q_HEADS)],
        out_specs=[pl.BlockSpec((ATTN_BLOCK, Q_COLS), lambda n: (n, 0)),
                   pl.BlockSpec((ATTN_BLOCK, N_Q_HEADS), lambda n: (n, 0))],
        name=name, compiler_params=_params(1))(proj, proj, proj, cos_t, sin_t, cos_t, sin_t, gq2, gk2, sinks)


def _attn_bwd_q(proj, cos_t, sin_t, gq2, gk2, sinks, ya, dmix, lse, name):
    t_len = proj.shape[0]
    nb = t_len // ATTN_BLOCK

    def body(q_ref, kvc_ref, kvp_ref, cq_ref, sq_ref, cp_ref, sp_ref, gq_ref, gk_ref, sk_ref, ya_ref, do_ref, lse_ref,
             dq_ref, dgq_ref, dsk_ref, scr):
        n = pl.program_id(0)

        @pl.when(n == 0)
        def _():
            dgq_ref[...] = jnp.zeros_like(dgq_ref)
            dsk_ref[...] = jnp.zeros_like(dsk_ref)

        cq, sq = cq_ref[...], sq_ref[...]
        c_all = jnp.concatenate([cp_ref[...], cq], axis=0)
        s_all = jnp.concatenate([sp_ref[...], sq], axis=0)
        kv = jnp.concatenate([kvp_ref[...], kvc_ref[...]], axis=0)
        k_raw, v = kv[:, :PAIR], kv[:, PAIR:]
        n_pairs = N_Q_HEADS // 2
        pair_lanes = [slice(pr * PAIR, (pr + 1) * PAIR) for pr in range(n_pairs)]
        q_raws = [q_ref[:, lanes] for lanes in pair_lanes]
        rms = _pair_rms_many(q_raws + [k_raw])
        dsums = _pair_sums([do_ref[:, lanes] * ya_ref[:, lanes].astype(F32) for lanes in pair_lanes])
        kn = _rope(k_raw * rms[-1] * gk_ref[...], c_all, s_all)
        row = lax.broadcasted_iota(jnp.int32, (ATTN_BLOCK, 2 * ATTN_BLOCK), 0)
        col = lax.broadcasted_iota(jnp.int32, (ATTN_BLOCK, 2 * ATTN_BLOCK), 1)
        rel = row + ATTN_BLOCK - col
        valid = (rel >= 0) & (rel < ATTN_BLOCK) & ((col >= ATTN_BLOCK) | (n > 0))
        gq = gq_ref[...]
        for pr in range(n_pairs):
            qn = _rope(q_raws[pr] * rms[pr] * gq, cq, sq)
            kvh = (2 * pr) // Q_PER_KV
            kh = kn[:, kvh * HEAD_DIM:(kvh + 1) * HEAD_DIM].astype(BF)
            vh = v[:, kvh * HEAD_DIM:(kvh + 1) * HEAD_DIM].astype(BF)
            dsum2 = dsums[pr]
            for hh in range(2):
                hd = 2 * pr + hh
                qh = qn[:, hh * HEAD_DIM:(hh + 1) * HEAD_DIM].astype(BF)
                lse_h = lse_ref[:, hd:hd + 1]
                p = jnp.exp(jnp.where(valid, _nt(qh, kh) * SCALE, NEG) - lse_h)
                d_o = do_ref[:, hd * HEAD_DIM:(hd + 1) * HEAD_DIM]
                dsum = dsum2[:, hh * HEAD_DIM:hh * HEAD_DIM + 1]
                ds = p * (_nt(d_o.astype(BF), vh) - dsum)
                scr[pr, :, hh * HEAD_DIM:(hh + 1) * HEAD_DIM] = _nn(ds.astype(BF), kh) * SCALE
                sink = sk_ref[0:1, hd:hd + 1]
                dsk_ref[0:1, hd:hd + 1] += -jnp.sum(jnp.exp(sink - lse_h) * dsum, axis=0, keepdims=True)
        back = _pair_rms_bwd_many([_rope_t(scr[pr], cq, sq) for pr in range(n_pairs)], q_raws, rms[:n_pairs], gq)
        for lanes, (dx, dg) in zip(pair_lanes, back):
            dq_ref[:, lanes] = dx.astype(BF)
            dgq_ref[...] += dg

    prev = lambda n: (jnp.maximum(n - 1, 0), 0)
    tab = pl.BlockSpec((ATTN_BLOCK, PAIR), lambda n: (n, 0))
    tab_p = pl.BlockSpec((ATTN_BLOCK, PAIR), prev)
    one = lambda w: pl.BlockSpec((1, w), lambda n: (0, 0))
    qblk = pl.BlockSpec((ATTN_BLOCK, Q_COLS), lambda n: (n, 0))
    return pl.pallas_call(
        body,
        out_shape=[jax.ShapeDtypeStruct((t_len, Q_COLS), BF), jax.ShapeDtypeStruct((1, PAIR), F32),
                   jax.ShapeDtypeStruct((1, N_Q_HEADS), F32)],
        grid=(nb,),
        in_specs=[qblk, pl.BlockSpec((ATTN_BLOCK, 2 * KV_COLS), lambda n: (n, 2)),
                  pl.BlockSpec((ATTN_BLOCK, 2 * KV_COLS), lambda n: (jnp.maximum(n - 1, 0), 2)),
                  tab, tab, tab_p, tab_p, one(PAIR), one(PAIR), one(N_Q_HEADS), qblk, qblk,
                  pl.BlockSpec((ATTN_BLOCK, N_Q_HEADS), lambda n: (n, 0))],
        out_specs=[qblk, one(PAIR), one(N_Q_HEADS)], scratch_shapes=[pltpu.VMEM((N_Q_HEADS // 2, ATTN_BLOCK, PAIR), F32)],
        name=name, compiler_params=_params(1))(proj, proj, proj, cos_t, sin_t, cos_t, sin_t, gq2, gk2, sinks, ya, dmix, lse)


def _attn_bwd_kv(proj, cos_t, sin_t, gq2, gk2, ya, dmix, lse, name):
    t_len = proj.shape[0]
    nb = t_len // ATTN_BLOCK

    def body(kv_ref, q0_ref, q1_ref, ck_ref, sk_ref, c1_ref, s1_ref, gq_ref, gk_ref, o0_ref, o1_ref, do0_ref, do1_ref,
             l0_ref, l1_ref, dk_ref, dv_ref, dgk_ref, dkn_scr, dv_scr):
        m = pl.program_id(0)

        @pl.when(m == 0)
        def _():
            dgk_ref[...] = jnp.zeros_like(dgk_ref)

        ck, sk = ck_ref[...], sk_ref[...]
        c_all = jnp.concatenate([ck, c1_ref[...]], axis=0)
        s_all = jnp.concatenate([sk, s1_ref[...]], axis=0)
        kvv = kv_ref[...]
        k_raw, v = kvv[:, :PAIR], kvv[:, PAIR:]
        pair_lanes = [slice(pr * PAIR, (pr + 1) * PAIR) for pr in range(N_Q_HEADS // 2)]
        q_raws = [jnp.concatenate([q0_ref[:, lanes], q1_ref[:, lanes]], axis=0) for lanes in pair_lanes]
        d_os = [jnp.concatenate([do0_ref[:, lanes], do1_ref[:, lanes]], axis=0) for lanes in pair_lanes]
        rms = [_pair_rms(a) for a in q_raws + [k_raw]]
        rk = rms[-1]
        gk = gk_ref[...]
        kn = _rope(k_raw * rk * gk, ck, sk)
        key = lax.broadcasted_iota(jnp.int32, (ATTN_BLOCK, 2 * ATTN_BLOCK), 0)
        qry = lax.broadcasted_iota(jnp.int32, (ATTN_BLOCK, 2 * ATTN_BLOCK), 1)
        valid = ((qry < ATTN_BLOCK) & (qry >= key)) | ((qry >= ATTN_BLOCK) & (qry - ATTN_BLOCK < key) & (m < nb - 1))
        lse_rows = _as_rows(jnp.concatenate([l0_ref[...], l1_ref[...]], axis=0))
        d_kn, d_v = [None, None], [None, None]
        plus = lambda acc, term: term if acc is None else acc + term
        for pr in range(N_Q_HEADS // 2):
            qn = _rope(q_raws[pr] * rms[pr] * gq_ref[...], c_all, s_all)
            d_o2, lanes = d_os[pr], pair_lanes[pr]
            dsum_rows = _head_sums_as_rows(d_o2 * jnp.concatenate([o0_ref[:, lanes], o1_ref[:, lanes]], axis=0).astype(F32))
            kvh = (2 * pr) // Q_PER_KV
            khs = slice(kvh * HEAD_DIM, (kvh + 1) * HEAD_DIM)
            kh = kn[:, khs].astype(BF)
            vh = v[:, khs].astype(BF)
            for hh in range(2):
                hd = 2 * pr + hh
                hs = slice(hh * HEAD_DIM, (hh + 1) * HEAD_DIM)
                qh = qn[:, hs].astype(BF)
                d_oh = d_o2[:, hs].astype(BF)
                p = jnp.exp(jnp.where(valid, _nt(kh, qh) * SCALE, NEG) - lse_rows[hd:hd + 1, :])
                d_v[kvh] = plus(d_v[kvh], _nn(p.astype(BF), d_oh))
                ds = p * (_nt(vh, d_oh) - dsum_rows[hh:hh + 1, :])
                d_kn[kvh] = plus(d_kn[kvh], _nn(ds.astype(BF), qh))
        for kvh in range(2):
            khs = slice(kvh * HEAD_DIM, (kvh + 1) * HEAD_DIM)
            dkn_scr[:, khs] = d_kn[kvh] * SCALE
            dv_scr[:, khs] = d_v[kvh]
        dx, dg = _pair_rms_bwd(_rope_t(dkn_scr[...], ck, sk), k_raw, rk, gk)
        dk_ref[...] = dx.astype(BF)
        dv_ref[...] = dv_scr[...].astype(BF)
        dgk_ref[...] += dg

    nxt = lambda m: (jnp.minimum(m + 1, nb - 1), 0)
    cur = lambda m: (m, 0)
    tab = lambda f: pl.BlockSpec((ATTN_BLOCK, PAIR), f)
    qb = lambda f: pl.BlockSpec((ATTN_BLOCK, Q_COLS), f)
    lb = lambda f: pl.BlockSpec((ATTN_BLOCK, N_Q_HEADS), f)
    one = pl.BlockSpec((1, PAIR), lambda m: (0, 0))
    return pl.pallas_call(
        body,
        out_shape=[jax.ShapeDtypeStruct((t_len, KV_COLS), BF), jax.ShapeDtypeStruct((t_len, KV_COLS), BF),
                   jax.ShapeDtypeStruct((1, PAIR), F32)],
        grid=(nb,),
        in_specs=[pl.BlockSpec((ATTN_BLOCK, 2 * KV_COLS), lambda m: (m, 2)), qb(cur), qb(nxt), tab(cur), tab(cur), tab(nxt),
                  tab(nxt), one, one, qb(cur), qb(nxt), qb(cur), qb(nxt), lb(cur), lb(nxt)],
        out_specs=[tab(cur), tab(cur), one],
        scratch_shapes=[pltpu.VMEM((ATTN_BLOCK, PAIR), F32), pltpu.VMEM((ATTN_BLOCK, PAIR), F32)],
        name=name, compiler_params=_params(1))(proj, proj, proj, cos_t, sin_t, cos_t, sin_t, gq2, gk2, ya, ya, dmix, dmix,
                                               lse, lse)


def _conv_taps(blk, w, offset_of):
    acc = jnp.zeros((CONV_CHUNK, CONV_CH), F32)
    for k in range(CONV_WIDTH):
        o = offset_of(k)
        acc = acc + w[k:k + 1, :] * blk[o:o + CONV_CHUNK, :]
    return acc


def _conv_fwd(proj, w_dw, b_dw, g_ln, b_ln, name):
    t_len = proj.shape[0]
    nc = t_len // CONV_CHUNK

    def body(a_ref, gt_ref, w_ref, b_ref, g_ref, bl_ref, yc_ref, cv_ref, pad):
        pad[0:CONV_PAD, :] = jnp.zeros((CONV_PAD, CONV_CH), F32)

        def glu(c, carry):
            rows = pl.ds(pl.multiple_of(c * CONV_CHUNK, CONV_CHUNK), CONV_CHUNK)
            dst = pl.ds(pl.multiple_of(c * CONV_CHUNK + CONV_PAD, CONV_PAD), CONV_CHUNK)
            pad[dst, :] = a_ref[rows, :] * jax.nn.sigmoid(gt_ref[rows, :])
            return carry

        lax.fori_loop(0, nc, glu, 0)
        w = w_ref[...]

        def conv(c, carry):
            base = pl.multiple_of(c * CONV_CHUNK, CONV_CHUNK)
            blk = pad[pl.ds(base, CONV_CHUNK + CONV_PAD), :]
            cv = _conv_taps(blk, w, lambda k: k + CONV_PAD - (CONV_WIDTH - 1)) + b_ref[...]
            mu = jnp.mean(cv, axis=-1, keepdims=True)
            xc = cv - mu
            z = xc * lax.rsqrt(jnp.mean(xc * xc, axis=-1, keepdims=True) + EPS) * g_ref[...] + bl_ref[...]
            rows = pl.ds(base, CONV_CHUNK)
            cv_ref[rows, :] = cv
            yc_ref[rows, :] = (z * jax.nn.sigmoid(z)).astype(BF)
            return carry

        lax.fori_loop(0, nc, conv, 0)

    vec = pl.BlockSpec((1, CONV_CH), lambda i: (0, 0))
    full = pl.BlockSpec((t_len, CONV_CH), lambda i: (0, 0))
    return pl.pallas_call(
        body, out_shape=[jax.ShapeDtypeStruct((t_len, CONV_CH), BF), jax.ShapeDtypeStruct((t_len, CONV_CH), F32)],
        grid=(1,),
        in_specs=[pl.BlockSpec((t_len, CONV_CH), lambda i: (0, 3)), pl.BlockSpec((t_len, CONV_CH), lambda i: (0, 4)),
                  pl.BlockSpec((CONV_WIDTH, CONV_CH), lambda i: (0, 0)), vec, vec, vec],
        out_specs=[full, full], scratch_shapes=[pltpu.VMEM((t_len + CONV_PAD, CONV_CH), F32)], name=name,
        compiler_params=_params(1))(proj, proj, w_dw, b_dw, g_ln, b_ln)


def _conv_bwd(proj, cv, dmix, w_dw, g_ln, b_ln, name):
    t_len = proj.shape[0]
    nc = t_len // CONV_CHUNK

    def body(a_ref, gt_ref, cv_ref, dy_ref, w_ref, g_ref, bl_ref, du_ref, dw_ref, db_ref, dgl_ref, dbl_ref, pad, dpad):
        pad[0:CONV_PAD, :] = jnp.zeros((CONV_PAD, CONV_CH), F32)
        dpad[t_len:t_len + CONV_PAD, :] = jnp.zeros((CONV_PAD, CONV_CH), F32)
        dw_ref[...] = jnp.zeros_like(dw_ref)
        db_ref[...] = jnp.zeros_like(db_ref)
        dgl_ref[...] = jnp.zeros_like(dgl_ref)
        dbl_ref[...] = jnp.zeros_like(dbl_ref)

        def norm_bwd(c, carry):
            base = pl.multiple_of(c * CONV_CHUNK, CONV_CHUNK)
            rows = pl.ds(base, CONV_CHUNK)
            dst = pl.ds(pl.multiple_of(c * CONV_CHUNK + CONV_PAD, CONV_PAD), CONV_CHUNK)
            pad[dst, :] = a_ref[rows, :] * jax.nn.sigmoid(gt_ref[rows, :])
            cvv = cv_ref[rows, :]
            xc = cvv - jnp.mean(cvv, axis=-1, keepdims=True)
            rs = lax.rsqrt(jnp.mean(xc * xc, axis=-1, keepdims=True) + EPS)
            xhat = xc * rs
            z = xhat * g_ref[...] + bl_ref[...]
            sg = jax.nn.sigmoid(z)
            dz = dy_ref[rows, :] * (sg * (1.0 + z * (1.0 - sg)))
            dgl_ref[...] += jnp.sum(dz * xhat, axis=0, keepdims=True)
            dbl_ref[...] += jnp.sum(dz, axis=0, keepdims=True)
            dxh = dz * g_ref[...]
            dcv = rs * (dxh - jnp.mean(dxh, axis=-1, keepdims=True) - xhat * jnp.mean(dxh * xhat, axis=-1, keepdims=True))
            db_ref[...] += jnp.sum(dcv, axis=0, keepdims=True)
            dpad[rows, :] = dcv
            return carry

        lax.fori_loop(0, nc, norm_bwd, 0)
        w = w_ref[...]

        def conv_bwd(c, carry):
            base = pl.multiple_of(c * CONV_CHUNK, CONV_CHUNK)
            rows = pl.ds(base, CONV_CHUNK)
            dblk = dpad[pl.ds(base, CONV_CHUNK + CONV_PAD), :]
            dhc = _conv_taps(dblk, w, lambda k: CONV_WIDTH - 1 - k)
            a = a_ref[rows, :]
            sg = jax.nn.sigmoid(gt_ref[rows, :])
            du_ref[rows, 0:CONV_CH] = (dhc * sg).astype(BF)
            du_ref[rows, CONV_CH:2 * CONV_CH] = (dhc * a * sg * (1.0 - sg)).astype(BF)
            hblk = pad[pl.ds(base, CONV_CHUNK + CONV_PAD), :]
            dcv = dblk[0:CONV_CHUNK, :]
            for k in range(CONV_WIDTH):
                o = k + CONV_PAD - (CONV_WIDTH - 1)
                dw_ref[k:k + 1, :] += jnp.sum(dcv * hblk[o:o + CONV_CHUNK, :], axis=0, keepdims=True)
            return carry

        lax.fori_loop(0, nc, conv_bwd, 0)

    vec = pl.BlockSpec((1, CONV_CH), lambda i: (0, 0))
    full = pl.BlockSpec((t_len, CONV_CH), lambda i: (0, 0))
    wspec = pl.BlockSpec((CONV_WIDTH, CONV_CH), lambda i: (0, 0))
    vshape = jax.ShapeDtypeStruct((1, CONV_CH), F32)
    return pl.pallas_call(
        body,
        out_shape=[jax.ShapeDtypeStruct((t_len, 2 * CONV_CH), BF), jax.ShapeDtypeStruct((CONV_WIDTH, CONV_CH), F32),
                   vshape, vshape, vshape],
        grid=(1,),
        in_specs=[pl.BlockSpec((t_len, CONV_CH), lambda i: (0, 3)), pl.BlockSpec((t_len, CONV_CH), lambda i: (0, 4)), full,
                  pl.BlockSpec((t_len, CONV_CH), lambda i: (0, 2)), wspec, vec, vec],
        out_specs=[pl.BlockSpec((t_len, 2 * CONV_CH), lambda i: (0, 0)), wspec, vec, vec, vec],
        scratch_shapes=[pltpu.VMEM((t_len + CONV_PAD, CONV_CH), F32), pltpu.VMEM((t_len + CONV_PAD, CONV_CH), F32)],
        name=name, compiler_params=_params(1))(proj, proj, cv, dmix, w_dw, g_ln, b_ln)


def _mem_kv_fwd(mem, g, w, name):
    def body(m_ref, g_ref, w_ref, h_ref, kv_ref):
        mv = m_ref[...]
        h = (mv * _row_rms(mv) * g_ref[...]).astype(BF)
        h_ref[...] = h
        kv_ref[...] = _nn(h, w_ref[...])

    m_len = mem.shape[0]
    return pl.pallas_call(
        body, out_shape=[jax.ShapeDtypeStruct((m_len, D_MODEL), BF), jax.ShapeDtypeStruct((m_len, 2 * MQ_COLS), F32)],
        name=name, compiler_params=pltpu.CompilerParams(vmem_limit_bytes=VMEM_LIMIT_BYTES))(mem, g, w)


def _mem_kv_bwd(mem, g, h, w, dkv, name):
    def body(m_ref, g_ref, h_ref, w_ref, dkv_ref, dw_ref, dg_ref):
        dkv_b = dkv_ref[...].astype(BF)
        dw_ref[...] = _tn(h_ref[...], dkv_b).astype(BF)
        mv = m_ref[...]
        dg_ref[...] = jnp.sum(_nt(dkv_b, w_ref[...]) * mv * _row_rms(mv), axis=0, keepdims=True)

    return pl.pallas_call(
        body, out_shape=[jax.ShapeDtypeStruct((D_MODEL, 2 * MQ_COLS), BF), jax.ShapeDtypeStruct((1, D_MODEL), F32)],
        name=name, compiler_params=pltpu.CompilerParams(vmem_limit_bytes=VMEM_LIMIT_BYTES))(mem, g, h, w, dkv)


def _mem_attn_fwd(proj, mkv, gq2, gk2, name):
    t_len = proj.shape[0]
    tm = min(256, t_len)

    def body(q_ref, kv_ref, gq_ref, gk_ref, ym_ref, lse_ref):
        kvv = kv_ref[...]
        for pr in range(N_MEM_HEADS // 2):
            lanes = slice(pr * PAIR, (pr + 1) * PAIR)
            k_raw = kvv[:, lanes]
            kn = k_raw * _pair_rms(k_raw) * gk_ref[...]
            v = kvv[:, MQ_COLS + pr * PAIR:MQ_COLS + (pr + 1) * PAIR]
            q_raw = q_ref[:, lanes]
            qn = q_raw * _pair_rms(q_raw) * gq_ref[...]
            for hh in range(2):
                hd = 2 * pr + hh
                hs = slice(hh * HEAD_DIM, (hh + 1) * HEAD_DIM)
                s = _nt(qn[:, hs].astype(BF), kn[:, hs].astype(BF)) * SCALE
                m = jnp.max(s, axis=-1, keepdims=True)
                p = jnp.exp(s - m)
                den = jnp.sum(p, axis=-1, keepdims=True)
                ym_ref[:, hd * HEAD_DIM:(hd + 1) * HEAD_DIM] = _nn((p / den).astype(BF), v[:, hs].astype(BF)).astype(BF)
                lse_ref[:, hd:hd + 1] = m + jnp.log(den)

    m_len = mkv.shape[0]
    one = pl.BlockSpec((1, PAIR), lambda t: (0, 0))
    return pl.pallas_call(
        body, out_shape=[jax.ShapeDtypeStruct((t_len, MQ_COLS), BF), jax.ShapeDtypeStruct((t_len, N_MEM_HEADS), F32)],
        grid=(t_len // tm,),
        in_specs=[pl.BlockSpec((tm, MQ_COLS), lambda t: (t, 5)), pl.BlockSpec((m_len, 2 * MQ_COLS), lambda t: (0, 0)), one, one],
        out_specs=[pl.BlockSpec((tm, MQ_COLS), lambda t: (t, 0)), pl.BlockSpec((tm, N_MEM_HEADS), lambda t: (t, 0))],
        name=name, compiler_params=_params(1))(proj, mkv, gq2, gk2)


def _mem_attn_bwd(proj, mkv, gq2, gk2, ym, dmix, lse, name):
    t_len = proj.shape[0]
    tm = min(256, t_len)
    nt = t_len // tm
    m_len = mkv.shape[0]

    def body(q_ref, kv_ref, gq_ref, gk_ref, ym_ref, do_ref, lse_ref, dq_ref, dkv_ref, dgq_ref, dgk_ref, dkn_scr, dv_scr, scr):
        t = pl.program_id(0)

        @pl.when(t == 0)
        def _():
            dkn_scr[...] = jnp.zeros_like(dkn_scr)
            dv_scr[...] = jnp.zeros_like(dv_scr)
            dgq_ref[...] = jnp.zeros_like(dgq_ref)

        kvv = kv_ref[...]
        for pr in range(N_MEM_HEADS // 2):
            lanes = slice(pr * PAIR, (pr + 1) * PAIR)
            k_raw = kvv[:, lanes]
            kn = k_raw * _pair_rms(k_raw) * gk_ref[...]
            v = kvv[:, MQ_COLS + pr * PAIR:MQ_COLS + (pr + 1) * PAIR]
            q_raw = q_ref[:, lanes]
            rq = _pair_rms(q_raw)
            qn = q_raw * rq * gq_ref[...]
            d_o = do_ref[:, lanes]
            dsum = _pair_sum(d_o * ym_ref[:, lanes].astype(F32))
            for hh in range(2):
                hd = 2 * pr + hh
                hs = slice(hh * HEAD_DIM, (hh + 1) * HEAD_DIM)
                cols = slice(hd * HEAD_DIM, (hd + 1) * HEAD_DIM)
                qh = qn[:, hs].astype(BF)
                kh = kn[:, hs].astype(BF)
                d_oh = d_o[:, hs].astype(BF)
                p = jnp.exp(_nt(qh, kh) * SCALE - lse_ref[:, hd:hd + 1])
                dv_scr[:, cols] += _tn(p.astype(BF), d_oh)
                ds = (p * (_nt(d_oh, v[:, hs].astype(BF)) - dsum[:, hh * HEAD_DIM:hh * HEAD_DIM + 1])).astype(BF)
                scr[pr, :, hs] = _nn(ds, kh) * SCALE
                dkn_scr[:, cols] += _tn(ds, qh) * SCALE
            dx, dg = _pair_rms_bwd(scr[pr], q_raw, rq, gq_ref[...])
            dq_ref[:, lanes] = dx.astype(BF)
            dgq_ref[...] += dg

        @pl.when(t == nt - 1)
        def _():
            dgk = jnp.zeros((1, PAIR), F32)
            for pr in range(N_MEM_HEADS // 2):
                lanes = slice(pr * PAIR, (pr + 1) * PAIR)
                k_raw = kvv[:, lanes]
                dx, dg = _pair_rms_bwd(dkn_scr[:, lanes], k_raw, _pair_rms(k_raw), gk_ref[...])
                dkv_ref[:, lanes] = dx
                dgk = dgk + dg
            dkv_ref[:, MQ_COLS:2 * MQ_COLS] = dv_scr[...]
            dgk_ref[...] = dgk

    one = pl.BlockSpec((1, PAIR), lambda t: (0, 0))
    kvspec = pl.BlockSpec((m_len, 2 * MQ_COLS), lambda t: (0, 0))
    qspec = pl.BlockSpec((tm, MQ_COLS), lambda t: (t, 0))
    return pl.pallas_call(
        body,
        out_shape=[jax.ShapeDtypeStruct((t_len, MQ_COLS), BF), jax.ShapeDtypeStruct((m_len, 2 * MQ_COLS), F32),
                   jax.ShapeDtypeStruct((1, PAIR), F32), jax.ShapeDtypeStruct((1, PAIR), F32)],
        grid=(nt,),
        in_specs=[pl.BlockSpec((tm, MQ_COLS), lambda t: (t, 5)), kvspec, one, one, qspec,
                  pl.BlockSpec((tm, MQ_COLS), lambda t: (t, 3)), pl.BlockSpec((tm, N_MEM_HEADS), lambda t: (t, 0))],
        out_specs=[qspec, kvspec, one, one],
        scratch_shapes=[pltpu.VMEM((m_len, MQ_COLS), F32), pltpu.VMEM((m_len, MQ_COLS), F32),
                        pltpu.VMEM((N_MEM_HEADS // 2, tm, PAIR), F32)],
        name=name, compiler_params=_params(1))(proj, mkv, gq2, gk2, ym, dmix, lse)


MIX_GROUPS = [(0, Q_COLS), (Q_COLS, CONV_CH), (Q_COLS + CONV_CH, MQ_COLS)]


def _out_proj_fwd(x, ya, yc, ym, w_out, name, after=None):
    t_len = x.shape[0]
    tm = min(512, t_len)

    def body(x_ref, ya_ref, yc_ref, ym_ref, w_ref, xo_ref):
        y = x_ref[...]
        for (start, width), ref in zip(MIX_GROUPS, (ya_ref, yc_ref, ym_ref)):
            y = y + _nn(ref[...], w_ref[start:start + width, :])
        xo_ref[...] = y

    cols = lambda w: pl.BlockSpec((tm, w), lambda t: (t, 0))
    body, first, first_specs = _ordered_after(body, after)
    return pl.pallas_call(
        body, out_shape=jax.ShapeDtypeStruct((t_len, D_MODEL), F32), grid=(t_len // tm,),
        in_specs=first_specs + [cols(D_MODEL), cols(Q_COLS), cols(CONV_CH), cols(MQ_COLS),
                                pl.BlockSpec((D_MODEL, D_MODEL), lambda t: (0, 0))],
        out_specs=cols(D_MODEL), name=name, compiler_params=_params(1))(*first, x, ya, yc, ym, w_out)


def _out_proj_bwd(dx, ya, yc, ym, w_out, name, after=None):
    t_len = dx.shape[0]
    tm = min(512, t_len)
    nt = t_len // tm

    def body(dx_ref, ya_ref, yc_ref, ym_ref, w_ref, dmix_ref, dw_ref, acc):
        t = pl.program_id(0)

        @pl.when(t == 0)
        def _():
            acc[...] = jnp.zeros_like(acc)

        dxb = dx_ref[...].astype(BF)
        dmix_ref[...] = _nt(dxb, w_ref[...])
        for (start, width), ref in zip(MIX_GROUPS, (ya_ref, yc_ref, ym_ref)):
            acc[start:start + width, :] += _tn(ref[...], dxb)

        @pl.when(t == nt - 1)
        def _():
            dw_ref[...] = acc[...].astype(BF)

    cols = lambda w: pl.BlockSpec((tm, w), lambda t: (t, 0))
    full = pl.BlockSpec((D_MODEL, D_MODEL), lambda t: (0, 0))
    body, first, first_specs = _ordered_after(body, after)
    return pl.pallas_call(
        body, out_shape=[jax.ShapeDtypeStruct((t_len, D_MODEL), F32), jax.ShapeDtypeStruct((D_MODEL, D_MODEL), BF)],
        grid=(nt,), in_specs=first_specs + [cols(D_MODEL), cols(Q_COLS), cols(CONV_CH), cols(MQ_COLS), full],
        out_specs=[cols(D_MODEL), full], scratch_shapes=[pltpu.VMEM((D_MODEL, D_MODEL), F32)], name=name,
        compiler_params=_params(1))(*first, dx, ya, yc, ym, w_out)


N_PEERS = N_DEV - 1
HBM_SPEC = pl.BlockSpec(memory_space=pltpu.HBM)
SEM_SPEC = pl.BlockSpec(memory_space=pltpu.SEMAPHORE)
EFFECT = pltpu.SideEffectType.DATAFLOW_SIDE_EFFECTING


def _my_place():
    x, y, c = lax.axis_index("x"), lax.axis_index("y"), lax.axis_index("c")
    return x, y, c, 4 * x + 2 * y + c


def _peer(k):
    x, y, c, _ = _my_place()
    px = 1 - x if k & 4 else x
    py = 1 - y if k & 2 else y
    pc = 1 - c if k & 1 else c
    return (px, py, pc), 4 * px + 2 * py + pc


def _gather_two_level(shards, name):
    n = len(shards)

    def body(*refs):
        ins, outs = refs[:n], refs[n:2 * n]
        send_sems, recv_sems, local_sems = refs[2 * n:]
        x, y, c, me = _my_place()
        sibling = (x, y, 1 - c)
        chips = [(1 - x, y), (x, 1 - y), (1 - x, 1 - y)]
        slot = lambda px, py, pc: 4 * px + 2 * py + pc

        def copy(w, k, src, dst_slot, to):
            return pltpu.make_async_remote_copy(
                src_ref=src, dst_ref=outs[w].at[dst_slot], send_sem=send_sems.at[w * N_PEERS + k],
                recv_sem=recv_sems.at[w * N_PEERS + k], device_id=to, device_id_type=pl.DeviceIdType.MESH)

        local = [pltpu.make_async_copy(ins[w], outs[w].at[me], local_sems.at[w]) for w in range(n)]
        for cp in local:
            cp.start()
        sends = []
        for w in range(n):
            for j, chip in enumerate(chips):
                sends.append(copy(w, 1 + j, ins[w], me, (*chip, c)))
            sends.append(copy(w, 0, ins[w], me, sibling))
        for cp in sends:
            cp.start()
        for w in range(n):
            for j, chip in enumerate(chips):
                got = slot(*chip, c)
                copy(w, 1 + j, ins[w], got, (*chip, c)).wait_recv()
                fwd = copy(w, 4 + j, outs[w].at[got], got, sibling)
                fwd.start()
                sends.append(fwd)
        for w in range(n):
            copy(w, 0, ins[w], slot(x, y, 1 - c), sibling).wait_recv()
            for j, chip in enumerate(chips):
                copy(w, 4 + j, ins[w], slot(*chip, 1 - c), sibling).wait_recv()
        for cp in sends:
            cp.wait_send()
        for cp in local:
            cp.wait()

    hbm = pl.BlockSpec(memory_space=pl.ANY)
    return pl.pallas_call(
        body, out_shape=[jax.ShapeDtypeStruct((N_DEV,) + a.shape, a.dtype) for a in shards], in_specs=[hbm] * n,
        out_specs=[hbm] * n,
        scratch_shapes=[pltpu.SemaphoreType.DMA((n * N_PEERS,)), pltpu.SemaphoreType.DMA((n * N_PEERS,)),
                        pltpu.SemaphoreType.DMA((n,))],
        name=name)(*shards)


def _split_copies(srcs, lands, gather, send_sems, recv_sems, first=0):
    _, _, _, me = _my_place()
    pairs = []
    for w in range(len(srcs)):
        for k in range(1, N_DEV):
            dev, idx = _peer(k)
            src = srcs[w] if gather[w] else srcs[w].at[idx]
            at = (first + w) * N_PEERS + k - 1
            sems = dict(send_sem=send_sems.at[at], recv_sem=recv_sems.at[at], device_id=dev,
                        device_id_type=pl.DeviceIdType.MESH)
            pairs.append((pltpu.make_async_remote_copy(src_ref=src, dst_ref=lands[w].at[me], **sems),
                          pltpu.make_async_remote_copy(src_ref=src, dst_ref=lands[w].at[idx], **sems)))
    return pairs


def _exchange_start(srcs, lands, gather, after, name):
    n = len(srcs)

    def body(*refs):
        src_refs, land_refs = refs[:n], refs[n:2 * n]
        send_sems, recv_sems = refs[2 * n + 1], refs[2 * n + 2]
        token = refs[-1]
        for out_going, _ in _split_copies(src_refs, land_refs, gather, send_sems, recv_sems):
            out_going.start()
        token[...] = jnp.zeros_like(token)

    arrays = list(srcs) + list(lands)
    out = pl.pallas_call(
        body, name=name,
        out_shape=(pltpu.SemaphoreType.DMA((n * N_PEERS,)), pltpu.SemaphoreType.DMA((n * N_PEERS,)),
                   *[pltpu.HBM(a.shape, a.dtype) for a in arrays], jax.ShapeDtypeStruct((8, 128), F32)),
        in_specs=[HBM_SPEC] * (2 * n) + [pl.BlockSpec(memory_space=pl.ANY)],
        out_specs=(SEM_SPEC, SEM_SPEC, *[HBM_SPEC] * (2 * n), pl.BlockSpec(memory_space=pltpu.VMEM)),
        input_output_aliases={i: i + 2 for i in range(2 * n)},
        compiler_params=pltpu.CompilerParams(has_side_effects=EFFECT),
    )(*[pltpu.with_memory_space_constraint(a, pltpu.HBM) for a in arrays], after)
    return out[0], out[1], out[2:2 + n], out[2 + n:2 + 2 * n], out[-1]


def _exchange_wait(send_sems, recv_sems, srcs, lands, gather, after, name, first=0, count=None):
    n = len(srcs)
    count = n if count is None else count

    def body(*refs):
        src_refs, land_refs = refs[:n], refs[n:2 * n]
        pairs = _split_copies(src_refs, land_refs, gather, refs[2 * n], refs[2 * n + 1], first)
        for out_going, arriving in pairs[:count * N_PEERS]:
            out_going.wait_send()
            arriving.wait_recv()

    arrays = list(srcs) + list(lands)
    out = pl.pallas_call(
        body, name=name, out_shape=tuple(pltpu.HBM(a.shape, a.dtype) for a in arrays),
        in_specs=[HBM_SPEC] * (2 * n) + [SEM_SPEC, SEM_SPEC, pl.BlockSpec(memory_space=pl.ANY)],
        out_specs=tuple([HBM_SPEC] * (2 * n)), input_output_aliases={i: i for i in range(2 * n)},
        compiler_params=pltpu.CompilerParams(has_side_effects=EFFECT),
    )(*arrays, send_sems, recv_sems, after)
    return list(out[:n]), list(out[n:])


SIBLING = 1
CHIP_PEERS = (2, 4, 6)
NEAR_PEERS = (SIBLING,) + CHIP_PEERS


def _near_copies(srcs, lands, send_sems, recv_sems):
    _, _, _, me = _my_place()
    pairs = []
    for w in range(len(srcs)):
        for i, k in enumerate(NEAR_PEERS):
            dev, idx = _peer(k)
            sems = dict(send_sem=send_sems.at[w * len(NEAR_PEERS) + i], recv_sem=recv_sems.at[w * len(NEAR_PEERS) + i],
                        device_id=dev, device_id_type=pl.DeviceIdType.MESH)
            pairs.append((pltpu.make_async_remote_copy(src_ref=srcs[w], dst_ref=lands[w].at[me], **sems),
                          pltpu.make_async_remote_copy(src_ref=srcs[w], dst_ref=lands[w].at[idx], **sems)))
    return pairs


def _forward_copies(lands, send_sems, recv_sems):
    sibling, _ = _peer(SIBLING)
    pairs = []
    for w in range(len(lands)):
        for i, k in enumerate(CHIP_PEERS):
            _, got = _peer(k)
            _, gets = _peer(k | SIBLING)
            sems = dict(send_sem=send_sems.at[w * len(CHIP_PEERS) + i], recv_sem=recv_sems.at[w * len(CHIP_PEERS) + i],
                        device_id=sibling, device_id_type=pl.DeviceIdType.MESH)
            pairs.append((pltpu.make_async_remote_copy(src_ref=lands[w].at[got], dst_ref=lands[w].at[got], **sems),
                          pltpu.make_async_remote_copy(src_ref=lands[w].at[got], dst_ref=lands[w].at[gets], **sems)))
    return pairs


def _split_call(body, name, arrays, sems_in, sems_out, after, token):
    n = len(arrays)
    out = pl.pallas_call(
        body, name=name,
        out_shape=(*[pltpu.SemaphoreType.DMA((c,)) for c in sems_out], *[pltpu.HBM(a.shape, a.dtype) for a in arrays],
                   *([jax.ShapeDtypeStruct((8, 128), F32)] if token else [])),
        in_specs=[HBM_SPEC] * n + [SEM_SPEC] * len(sems_in) + [pl.BlockSpec(memory_space=pl.ANY)],
        out_specs=(*[SEM_SPEC] * len(sems_out), *[HBM_SPEC] * n,
                   *([pl.BlockSpec(memory_space=pltpu.VMEM)] if token else [])),
        input_output_aliases={i: i + len(sems_out) for i in range(n)},
        compiler_params=pltpu.CompilerParams(has_side_effects=EFFECT),
    )(*[pltpu.with_memory_space_constraint(a, pltpu.HBM) for a in arrays], *sems_in, after)
    k = len(sems_out)
    return list(out[:k]), list(out[k:k + n]), (out[-1] if token else None)


def _gather_start(srcs, lands, after, name):
    n = len(srcs)

    def body(*refs):
        send1, recv1 = refs[2 * n + 1], refs[2 * n + 2]
        for out_going, _ in _near_copies(refs[:n], refs[n:2 * n], send1, recv1):
            out_going.start()
        refs[-1][...] = jnp.zeros_like(refs[-1])

    sems, arrays, token = _split_call(body, name, list(srcs) + list(lands), [], [n * len(NEAR_PEERS)] * 2, after, True)
    return sems, arrays, token


def _gather_relay(sems, arrays, after, name):
    n = len(arrays) // 2

    def body(*refs):
        send1, recv1 = refs[2 * n], refs[2 * n + 1]
        send2, recv2 = refs[2 * n + 3], refs[2 * n + 4]
        near = _near_copies(refs[:n], refs[n:2 * n], send1, recv1)
        forward = _forward_copies(refs[n:2 * n], send2, recv2)
        for w in range(n):
            for i in range(len(CHIP_PEERS)):
                near[w * len(NEAR_PEERS) + 1 + i][1].wait_recv()
                forward[w * len(CHIP_PEERS) + i][0].start()
        refs[-1][...] = jnp.zeros_like(refs[-1])

    sems2, arrays, token = _split_call(body, name, arrays, sems, [n * len(CHIP_PEERS)] * 2, after, True)
    return sems + sems2, arrays, token


def _gather_finish(sems, arrays, after, name):
    n = len(arrays) // 2

    def body(*refs):
        send1, recv1, send2, recv2 = refs[2 * n:2 * n + 4]
        near = _near_copies(refs[:n], refs[n:2 * n], send1, recv1)
        for out_going, _ in near:
            out_going.wait_send()
        for w in range(n):
            near[w * len(NEAR_PEERS)][1].wait_recv()
        for out_going, arriving in _forward_copies(refs[n:2 * n], send2, recv2):
            out_going.wait_send()
            arriving.wait_recv()

    _, arrays, _ = _split_call(body, name, arrays, sems, [], after, False)
    return arrays[n:]


def _own_slot(a, me, gather):
    mine = a if gather else lax.dynamic_index_in_dim(a, me, 0, keepdims=False)
    return lax.dynamic_update_index_in_dim(lax.empty((N_DEV,) + mine.shape, mine.dtype), mine, me, 0)


def _adamw_math(w, g, m, v):
    m2 = ADAM_B1 * m + (1.0 - ADAM_B1) * g
    v2 = ADAM_B2 * v + (1.0 - ADAM_B2) * (g * g)
    m_hat = m2 / (1.0 - ADAM_B1 ** ADAM_STEP)
    v_hat = v2 / (1.0 - ADAM_B2 ** ADAM_STEP)
    return -ADAM_LR * (m_hat / (jnp.sqrt(v_hat) + ADAM_EPS) + ADAM_WD * w), m2, v2


def _sum_adamw(parts, w, m, v, name):
    rows, cols = w.shape
    tr = rows if rows <= 512 else 256

    def body(p_ref, w_ref, m_ref, v_ref, g_ref, d_ref, m2_ref, v2_ref):
        g = p_ref[0].astype(F32)
        for s in range(1, N_DEV):
            g = g + p_ref[s].astype(F32)
        g_ref[...] = g
        d_ref[...], m2_ref[...], v2_ref[...] = _adamw_math(w_ref[...], g, m_ref[...], v_ref[...])

    blk = pl.BlockSpec((tr, cols), lambda i: (i, 0))
    shape = jax.ShapeDtypeStruct((rows, cols), F32)
    return pl.pallas_call(
        body, out_shape=[shape] * 4, grid=(rows // tr,),
        in_specs=[pl.BlockSpec((N_DEV, tr, cols), lambda i: (0, i, 0)), blk, blk, blk], out_specs=[blk] * 4, name=name,
        compiler_params=_params(1))(parts, w, m, v)


def _slot_sum(parts, name):
    def body(p_ref, g_ref):
        g = p_ref[0]
        for s in range(1, N_DEV):
            g = g + p_ref[s]
        g_ref[...] = g

    return pl.pallas_call(body, out_shape=jax.ShapeDtypeStruct(parts.shape[1:], F32), name=name)(parts)


def _adamw(g, w, m, v, name):
    def body(g_ref, w_ref, m_ref, v_ref, d_ref, m2_ref, v2_ref):
        d_ref[...], m2_ref[...], v2_ref[...] = _adamw_math(w_ref[...], g_ref[...], m_ref[...], v_ref[...])

    shape = jax.ShapeDtypeStruct(w.shape, F32)
    return pl.pallas_call(body, out_shape=[shape] * 3, name=name)(g, w, m, v)


def _pack_small(vals, last=None):
    flat = jnp.concatenate([vals[k].reshape(-1) for k in SMALL])
    tail = jnp.zeros((SMALL_ROWS * D_MODEL - flat.shape[0] - 1,), F32)
    last = jnp.zeros((1,), F32) if last is None else last.reshape(1)
    return jnp.concatenate([flat, tail, last]).reshape(SMALL_ROWS, D_MODEL)


def _unpack_small(packed, like):
    flat, out, at = packed.reshape(-1), {}, 0
    for k in SMALL:
        size = like[k].size
        out[k] = flat[at:at + size].reshape(like[k].shape)
        at += size
    return out


FFN1 = ["w_ffn1_gate", "w_ffn1_up", "w_ffn1_down"]
FFN2 = ["w_ffn2_gate", "w_ffn2_up", "w_ffn2_down"]
MIXER = ["w_in", "w_mem_kv", "w_out"]
TRANSPOSED = ["w_ffn1_gate", "w_ffn1_up", "w_ffn2_gate", "w_ffn2_up", "w_in"]


def _mixer_fwd(x1, mem, tables, sm, big, after_attn=None):
    cos_t, sin_t, gq2, gk2, gmq2, gmk2 = tables
    h2, proj = _in_proj_fwd(x1, sm["g_mix"], big["w_in"], "in_proj_fwd")
    ya, lse = _attn_fwd(proj, cos_t, sin_t, gq2, gk2, sm["sinks"], "attn_fwd")
    token = None if after_attn is None else after_attn(ya)
    yc, cv = _conv_fwd(proj, big["w_dw"], sm["b_dw"], sm["g_conv_ln"], sm["b_conv_ln"], "conv_fwd")
    hm, mkv = _mem_kv_fwd(mem, sm["g_mem"], big["w_mem_kv"], "mem_kv_fwd")
    ym, lse_m = _mem_attn_fwd(proj, mkv, gmq2, gmk2, "mem_attn_fwd")
    x2 = _out_proj_fwd(x1, ya, yc, ym, big["w_out"], "out_proj_fwd", token)
    return x2, (h2, proj, ya, lse, yc, cv, hm, mkv, ym, lse_m)


def _mixer_bwd(dx2, x1, mem, tables, sm, big, saved, after):
    cos_t, sin_t, gq2, gk2, gmq2, gmk2 = tables
    h2, proj, ya, lse, yc, cv, hm, mkv, ym, lse_m = saved
    g = {}
    dmix, g["w_out"] = _out_proj_bwd(dx2, ya, yc, ym, big["w_out"], "out_proj_bwd", after)
    dq, dgq, g["sinks"] = _attn_bwd_q(proj, cos_t, sin_t, gq2, gk2, sm["sinks"], ya, dmix, lse, "attn_bwd_q")
    dk, dv, dgk = _attn_bwd_kv(proj, cos_t, sin_t, gq2, gk2, ya, dmix, lse, "attn_bwd_kv")
    du, g["w_dw"], g["b_dw"], g["g_conv_ln"], g["b_conv_ln"] = _conv_bwd(
        proj, cv, dmix, big["w_dw"], sm["g_conv_ln"], sm["b_conv_ln"], "conv_bwd")
    dmq, dmkv, dgmq, dgmk = _mem_attn_bwd(proj, mkv, gmq2, gmk2, ym, dmix, lse_m, "mem_attn_bwd")
    g["w_mem_kv"], g["g_mem"] = _mem_kv_bwd(mem, sm["g_mem"], hm, big["w_mem_kv"], dmkv, "mem_kv_bwd")
    dx1, g["g_mix"], g["w_in"] = _in_proj_bwd(dq, dk, dv, du, dmq, big["w_in"], h2, x1, sm["g_mix"], dx2, "in_proj_bwd")
    fold = lambda a: a[:, :HEAD_DIM] + a[:, HEAD_DIM:]
    g["g_q"], g["g_k"], g["g_mq"], g["g_mk"] = fold(dgq), fold(dgk), fold(dgmq), fold(dgmk)
    return dx1, g


def _tables(positions, sm):
    pair = lambda a: jnp.tile(a, (1, 2))
    return _rope_tables(positions) + (pair(sm["g_q"]), pair(sm["g_k"]), pair(sm["g_mq"]), pair(sm["g_mk"]))


def kernel(x, mem, positions, g_ffn1, w_ffn1_gate, w_ffn1_up, w_ffn1_down, g_mix, w_in, g_q, g_k, sinks, w_dw, b_dw, g_conv_ln, b_conv_ln, g_mem, w_mem_kv, g_mq, g_mk, w_out, g_ffn2, w_ffn2_gate, w_ffn2_up, w_ffn2_down, loss_target, m_g_ffn1, m_w_ffn1_gate, m_w_ffn1_up, m_w_ffn1_down, m_g_mix, m_w_in, m_g_q, m_g_k, m_sinks, m_w_dw, m_b_dw, m_g_conv_ln, m_b_conv_ln, m_g_mem, m_w_mem_kv, m_g_mq, m_g_mk, m_w_out, m_g_ffn2, m_w_ffn2_gate, m_w_ffn2_up, m_w_ffn2_down, v_g_ffn1, v_w_ffn1_gate, v_w_ffn1_up, v_w_ffn1_down, v_g_mix, v_w_in, v_g_q, v_g_k, v_sinks, v_w_dw, v_b_dw, v_g_conv_ln, v_b_conv_ln, v_g_mem, v_w_mem_kv, v_g_mq, v_g_mk, v_w_out, v_g_ffn2, v_w_ffn2_gate, v_w_ffn2_up, v_w_ffn2_down):
    w = dict(g_ffn1=g_ffn1, w_ffn1_gate=w_ffn1_gate, w_ffn1_up=w_ffn1_up, w_ffn1_down=w_ffn1_down, g_mix=g_mix, w_in=w_in,
             g_q=g_q, g_k=g_k, sinks=sinks, w_dw=w_dw, b_dw=b_dw, g_conv_ln=g_conv_ln, b_conv_ln=b_conv_ln, g_mem=g_mem,
             w_mem_kv=w_mem_kv, g_mq=g_mq, g_mk=g_mk, w_out=w_out, g_ffn2=g_ffn2, w_ffn2_gate=w_ffn2_gate,
             w_ffn2_up=w_ffn2_up, w_ffn2_down=w_ffn2_down)
    mo = dict(g_ffn1=m_g_ffn1, w_ffn1_gate=m_w_ffn1_gate, w_ffn1_up=m_w_ffn1_up, w_ffn1_down=m_w_ffn1_down, g_mix=m_g_mix,
              w_in=m_w_in, g_q=m_g_q, g_k=m_g_k, sinks=m_sinks, w_dw=m_w_dw, b_dw=m_b_dw, g_conv_ln=m_g_conv_ln,
              b_conv_ln=m_b_conv_ln, g_mem=m_g_mem, w_mem_kv=m_w_mem_kv, g_mq=m_g_mq, g_mk=m_g_mk, w_out=m_w_out,
              g_ffn2=m_g_ffn2, w_ffn2_gate=m_w_ffn2_gate, w_ffn2_up=m_w_ffn2_up, w_ffn2_down=m_w_ffn2_down)
    vo = dict(g_ffn1=v_g_ffn1, w_ffn1_gate=v_w_ffn1_gate, w_ffn1_up=v_w_ffn1_up, w_ffn1_down=v_w_ffn1_down, g_mix=v_g_mix,
              w_in=v_w_in, g_q=v_g_q, g_k=v_g_k, sinks=v_sinks, w_dw=v_w_dw, b_dw=v_b_dw, g_conv_ln=v_g_conv_ln,
              b_conv_ln=v_b_conv_ln, g_mem=v_g_mem, w_mem_kv=v_w_mem_kv, g_mq=v_g_mq, g_mk=v_g_mk, w_out=v_w_out,
              g_ffn2=v_g_ffn2, w_ffn2_gate=v_w_ffn2_gate, w_ffn2_up=v_w_ffn2_up, w_ffn2_down=v_w_ffn2_down)
    me = _my_place()[3]
    sm = {k: w[k] for k in SMALL}
    flip = lambda k, a: a.T if k in TRANSPOSED else a
    as_bf16 = lambda names: [flip(k, w[k][0]).astype(BF) for k in names]
    zones = lambda arrays, gather: [_own_slot(a, me, gather) for a in arrays]
    out_g, out_d, out_m, out_v = {}, {}, {}, {}

    def update(names, parts_list):
        for k, parts in zip(names, parts_list):
            new = _sum_adamw(parts, flip(k, w[k][0]), flip(k, mo[k][0]), flip(k, vo[k][0]), "adamw_" + k)
            out_g[k], out_d[k], out_m[k], out_v[k] = [flip(k, a)[None] for a in new]

    w1 = _gather_two_level(as_bf16(FFN1), "gather_ffn1")
    mix_src = as_bf16(MIXER) + [w["w_dw"][0]]
    mix_sems, mix_arrays, mix_token = _gather_start(mix_src, zones(mix_src, True), w1[0], "gather_mixer_start")
    f2_src = as_bf16(FFN2)
    f2_sems, f2_arrays, f2_token = _gather_start(f2_src, zones(f2_src, True), mix_token, "gather_ffn2_start")

    tables = _tables(positions[0], sm)
    h1, gate1, up1, x1 = _ffn_fwd(x[0], sm["g_ffn1"], *w1, None, "ffn1_fwd", after=f2_token)
    mix_sems, mix_arrays, mix_token = _gather_relay(mix_sems, mix_arrays, x1, "gather_mixer_relay")
    got = _gather_finish(mix_sems, mix_arrays, mix_token, "gather_mixer_finish")
    big = dict(w_in=got[0].reshape(IN_COLS, D_MODEL), w_mem_kv=got[1].reshape(D_MODEL, 2 * MQ_COLS),
               w_out=got[2].reshape(D_MODEL, D_MODEL), w_dw=got[3].transpose(1, 0, 2).reshape(CONV_WIDTH, CONV_CH))
    relayed = []

    def relay_ffn2(ya):
        relayed.extend(_gather_relay(f2_sems, f2_arrays, ya, "gather_ffn2_relay"))
        return relayed[2]

    x2, saved = _mixer_fwd(x1, mem[0], tables, sm, big, relay_ffn2)
    w2 = _gather_finish(relayed[0], relayed[1], x2, "gather_ffn2_finish")
    h3, gate2, up2, dy, loss_part = _ffn_fwd(x2, sm["g_ffn2"], *w2, loss_target[0], "ffn2_fwd")

    grads = {}
    dyb2, act2, dgate2, dup2, dx2, grads["g_ffn2"] = _ffn_bwd_act(dy, x2, sm["g_ffn2"], gate2, up2, *w2, "ffn2_bwd_act")
    g_f2 = list(_ffn_bwd_w(h3, dyb2, act2, dgate2, dup2, "ffn2_bwd_w"))
    r_f2 = _exchange_start(g_f2, zones(g_f2, False), [False] * 3, dx2, "scatter_ffn2_start")
    dx1, g_mid = _mixer_bwd(dx2, x1, mem[0], tables, sm, big, saved, r_f2[4])
    grads.update(g_mid)
    g_mix = [g_mid["w_in"].reshape(N_DEV, IN_COLS // N_DEV, D_MODEL),
             g_mid["w_mem_kv"].reshape(N_DEV, D_MODEL // N_DEV, 2 * MQ_COLS),
             g_mid["w_out"].reshape(N_DEV, D_MODEL // N_DEV, D_MODEL)]
    r_mix = _exchange_start(g_mix, zones(g_mix, False), [False] * 3, dx1, "scatter_mixer_start")
    dyb1, act1, dgate1, dup1, grad_x, grads["g_ffn1"] = _ffn_bwd_act(dx1, x[0], sm["g_ffn1"], gate1, up1, *w1,
                                                                     "ffn1_bwd_act", after=r_mix[4])
    g_f1 = list(_ffn_bwd_w(h1, dyb1, act1, dgate1, dup1, "ffn1_bwd_w"))
    dw_flat = grads["w_dw"].reshape(-1)
    packed = jnp.concatenate([_pack_small(grads, loss_part[0, 0]),
                              jnp.pad(dw_flat, (0, SMALL_ROWS * D_MODEL - dw_flat.shape[0])).reshape(SMALL_ROWS, D_MODEL)])
    last_src, last_kind = g_f1 + [packed], [False] * 3 + [True]
    r_f1 = _exchange_start(last_src, zones(g_f1, False) + zones([packed], True), last_kind, grad_x, "scatter_ffn1_start")

    all_done = lambda names: sum(out_d[k][:, :1, :1] for k in names)
    update(FFN2, _exchange_wait(*r_f2[:4], [False] * 3, r_f1[4], "scatter_ffn2_wait")[1])
    update(MIXER, _exchange_wait(*r_mix[:4], [False] * 3, all_done(FFN2), "scatter_mixer_wait")[1])
    srcs, lands = _exchange_wait(*r_f1[:4], last_kind, all_done(MIXER), "scatter_ffn1_wait_gate", 0, 1)
    update(FFN1[:1], lands[:1])
    srcs, lands = _exchange_wait(r_f1[0], r_f1[1], srcs[1:], lands[1:], last_kind[1:], out_d[FFN1[0]],
                                 "scatter_ffn1_wait_up", 1, 1)
    update(FFN1[1:2], lands[:1])
    srcs, lands = _exchange_wait(r_f1[0], r_f1[1], srcs[1:], lands[1:], last_kind[2:], out_d[FFN1[1]],
                                 "scatter_ffn1_wait_down", 2)
    update(FFN1[2:], lands[:1])
    small_sum = _slot_sum(lands[1], "small_grad_sum")
    loss = small_sum[SMALL_ROWS - 1, D_MODEL - 1]
    g_small = small_sum[:SMALL_ROWS]
    d, m2, v2 = _adamw(g_small, _pack_small(w), _pack_small(mo), _pack_small(vo), "adamw_small")
    for dst, val in ((out_g, g_small), (out_d, d), (out_m, m2), (out_v, v2)):
        dst.update(_unpack_small(val, sm))
    g_dw = small_sum[SMALL_ROWS:].reshape(-1)[:CONV_WIDTH * CONV_CH].reshape(CONV_WIDTH, CONV_CH)
    g_dw = lax.dynamic_slice_in_dim(g_dw, me * (CONV_CH // N_DEV), CONV_CH // N_DEV, axis=1)
    d, m2, v2 = _adamw(g_dw, w["w_dw"][0], mo["w_dw"][0], vo["w_dw"][0], "adamw_w_dw")
    out_g["w_dw"], out_d["w_dw"], out_m["w_dw"], out_v["w_dw"] = g_dw[None], d[None], m2[None], v2[None]

    return (loss, grad_x[None], *[out_g[k] for k in WEIGHTS], *[out_d[k] for k in WEIGHTS], *[out_m[k] for k in WEIGHTS],
            *[out_v[k] for k in WEIGHTS])
```

```python
import jax
import jax.numpy as jnp
from jax import lax
from jax.experimental import pallas as pl
from jax.experimental.pallas import tpu as pltpu

D_MODEL = 1024
N_DEV = 8
FF_BLOCK = 352
D_FF = N_DEV * FF_BLOCK
FF_TILE = 256
N_FF_TILES = D_FF // FF_TILE
FF_ROWS = 1024
FF_ROWS_W = 2048
FF_CHUNK = 256
HEAD_DIM = 64
PAIR = 2 * HEAD_DIM
N_Q_HEADS = 8
Q_PER_KV = 4
ATTN_BLOCK = 128
Q_COLS = 512
KV_COLS = 128
CONV_CH = 256
MQ_COLS = 256
IN_COLS = 1536
CONV_WIDTH = 31
CONV_PAD = 32
CONV_CHUNK = 256
N_MEM_HEADS = 4
ROPE_THETA = 500000.0
ROPE_HALF = 8
EPS = 1e-6
SCALE = HEAD_DIM ** -0.5
NEG = -1e30
ADAM_LR, ADAM_B1, ADAM_B2, ADAM_EPS, ADAM_WD, ADAM_STEP = 0.001, 0.9, 0.999, 1e-08, 0.01, 10
VMEM_LIMIT_BYTES = 56 * 1024 * 1024
BF = jnp.bfloat16
F32 = jnp.float32

SMALL = ["g_ffn1", "g_mix", "g_mem", "g_ffn2", "g_q", "g_k", "g_mq", "g_mk", "sinks", "b_dw", "g_conv_ln", "b_conv_ln"]
WEIGHTS = ["g_ffn1", "w_ffn1_gate", "w_ffn1_up", "w_ffn1_down", "g_mix", "w_in", "g_q", "g_k", "sinks", "w_dw", "b_dw",
           "g_conv_ln", "b_conv_ln", "g_mem", "w_mem_kv", "g_mq", "g_mk", "w_out", "g_ffn2", "w_ffn2_gate", "w_ffn2_up",
           "w_ffn2_down"]
SMALL_ROWS = 8
PACK_ROWS = 16


def _nn(a, b):
    return jnp.dot(a, b, preferred_element_type=F32)


def _nt(a, b):
    return lax.dot_general(a, b, (((1,), (1,)), ((), ())), preferred_element_type=F32)


def _tn(a, b):
    return lax.dot_general(a, b, (((0,), (0,)), ((), ())), preferred_element_type=F32)


def _params(n_grid):
    return pltpu.CompilerParams(dimension_semantics=("arbitrary",) * n_grid, vmem_limit_bytes=VMEM_LIMIT_BYTES)


def _ordered_after(body, after):
    if after is None:
        return body, (), []

    def body_after(after_ref, *refs):
        body(*refs)

    return body_after, (after,), [pl.BlockSpec(memory_space=pl.ANY)]


def _row_rms(xv):
    return lax.rsqrt(jnp.mean(xv * xv, axis=-1, keepdims=True) + EPS)


def _rms_bwd(dh, xv, r, g):
    u = dh * g
    dx = r * u - xv * (r * r * r) * jnp.mean(u * xv, axis=-1, keepdims=True)
    return dx, jnp.sum(dh * xv * r, axis=0, keepdims=True)


def _sum_all(a):
    return jnp.sum(jnp.sum(a, axis=1, keepdims=True), axis=0, keepdims=True)


def _pair_sums(vs):
    row = lax.broadcasted_iota(jnp.int32, (PAIR, PAIR), 0) >= HEAD_DIM
    col = lax.broadcasted_iota(jnp.int32, (PAIR, PAIR), 1) >= HEAD_DIM
    same_head = (row == col).astype(BF)
    stacked = jnp.concatenate(vs, axis=0) if len(vs) > 1 else vs[0]
    hi = stacked.astype(BF)
    lo = (stacked - hi.astype(F32)).astype(BF)
    n = stacked.shape[0]
    both = _nn(jnp.concatenate([hi, lo], axis=0), same_head)
    total = both[:n] + both[n:]
    out, at = [], 0
    for v in vs:
        out.append(total[at:at + v.shape[0]])
        at += v.shape[0]
    return out


def _pair_sum(v):
    return _pair_sums([v])[0]


def _bf16_terms(v, n):
    terms, rest = [], v
    for _ in range(n):
        t = rest.astype(BF)
        terms.append(t)
        rest = rest - t.astype(F32)
    return terms


def _as_rows(cols):
    k = cols.shape[1]
    eye = (lax.broadcasted_iota(jnp.int32, (k, k), 0) == lax.broadcasted_iota(jnp.int32, (k, k), 1)).astype(BF)
    return sum(_nt(eye, t) for t in _bf16_terms(cols, 3))


def _head_sums_as_rows(v):
    sel = (lax.broadcasted_iota(jnp.int32, (8, PAIR), 0)
           == (lax.broadcasted_iota(jnp.int32, (8, PAIR), 1) >= HEAD_DIM).astype(jnp.int32)).astype(BF)
    return sum(_nt(sel, t) for t in _bf16_terms(v, 2))


def _pair_rms_many(xs):
    return [lax.rsqrt(s * (1.0 / HEAD_DIM) + EPS) for s in _pair_sums([xv * xv for xv in xs])]


def _pair_rms(xv):
    return _pair_rms_many([xv])[0]


def _pair_rms_bwd_many(dxns, xs, rs, g):
    us = [dxn * g for dxn in dxns]
    sums = _pair_sums([u * xv for u, xv in zip(us, xs)])
    return [(r * u - xv * (r * r * r) * (s * (1.0 / HEAD_DIM)), jnp.sum(dxn * xv * r, axis=0, keepdims=True))
            for dxn, xv, r, u, s in zip(dxns, xs, rs, us, sums)]


def _pair_rms_bwd(dxn, xv, r, g):
    return _pair_rms_bwd_many([dxn], [xv], [r], g)[0]


def _rope_mask(shape):
    lane = lax.broadcasted_iota(jnp.int32, shape, 1)
    return ((lane & (HEAD_DIM - 1)) < 2 * ROPE_HALF).astype(F32)


def _partner(v):
    return pltpu.roll(v, ROPE_HALF, 1) + pltpu.roll(v, PAIR - ROPE_HALF, 1)


def _rope(xn, cos_t, sin_t):
    return xn * cos_t + _partner(xn * _rope_mask(xn.shape)) * sin_t


def _rope_t(d, cos_t, sin_t):
    return d * cos_t + _partner(d * sin_t) * _rope_mask(d.shape)


def _rope_tables(positions):
    inv_freq = ROPE_THETA ** (-jnp.arange(ROPE_HALF, dtype=F32) / ROPE_HALF)
    ang = positions.astype(F32)[:, None] * inv_freq
    cos, sin = jnp.cos(ang), jnp.sin(ang)
    t = positions.shape[0]
    cos_h = jnp.concatenate([cos, cos, jnp.ones((t, HEAD_DIM - 2 * ROPE_HALF), F32)], axis=1)
    sin_h = jnp.concatenate([-sin, sin, jnp.zeros((t, HEAD_DIM - 2 * ROPE_HALF), F32)], axis=1)
    return jnp.tile(cos_h, (1, 2)), jnp.tile(sin_h, (1, 2))


def _ff_rows(w):
    return w.reshape(D_FF, D_MODEL)


def _ffn_fwd(x, g, wg, wu, wd, target, name, after=None):
    t_len = x.shape[0]
    tm = min(FF_ROWS, t_len)
    with_loss = target is not None

    def body(*refs):
        if with_loss:
            x_ref, g_ref, wg_ref, wu_ref, wd_ref, t_ref, h_ref, gg_ref, uu_ref, dy_ref, loss_ref, acc = refs
        else:
            x_ref, g_ref, wg_ref, wu_ref, wd_ref, h_ref, gg_ref, uu_ref, xo_ref, acc = refs
        t, j = pl.program_id(0), pl.program_id(1)

        @pl.when(j == 0)
        def _():
            xv = x_ref[...]
            h_ref[...] = (xv * _row_rms(xv) * g_ref[...]).astype(BF)
            acc[...] = jnp.zeros_like(acc)

        h = h_ref[...]
        gate = _nt(h, wg_ref[...])
        up = _nt(h, wu_ref[...])
        gg_ref[...] = gate.astype(BF)
        uu_ref[...] = up.astype(BF)
        act = (gate * jax.nn.sigmoid(gate) * up).astype(BF)
        acc[...] += _nn(act, wd_ref[...])

        @pl.when(j == N_FF_TILES - 1)
        def _():
            xo = x_ref[...] + 0.5 * acc[...]
            if with_loss:
                err = xo - t_ref[...]
                dy_ref[...] = err * (1.0 / D_MODEL)

                @pl.when(t == 0)
                def _():
                    loss_ref[...] = jnp.zeros_like(loss_ref)

                loss_ref[...] += _sum_all(err * err) * (0.5 / D_MODEL)
            else:
                xo_ref[...] = xo

    row = pl.BlockSpec((tm, D_MODEL), lambda t, j: (t, 0))
    vec = pl.BlockSpec((1, D_MODEL), lambda t, j: (0, 0))
    w_spec = pl.BlockSpec((FF_TILE, D_MODEL), lambda t, j: (j, 0))
    blk = pl.BlockSpec((tm, FF_TILE), lambda t, j: (t, j))
    in_specs = [row, vec, w_spec, w_spec, w_spec] + ([row] if with_loss else [])
    out_shape = [jax.ShapeDtypeStruct((t_len, D_MODEL), BF),
                 jax.ShapeDtypeStruct((t_len, D_FF), BF),
                 jax.ShapeDtypeStruct((t_len, D_FF), BF),
                 jax.ShapeDtypeStruct((t_len, D_MODEL), F32)]
    out_specs = [row, blk, blk, row]
    if with_loss:
        out_shape.append(jax.ShapeDtypeStruct((1, 128), F32))
        out_specs.append(pl.BlockSpec((1, 128), lambda t, j: (0, 0)))
    args = (x, g, _ff_rows(wg), _ff_rows(wu), _ff_rows(wd)) + ((target,) if with_loss else ())
    body, first, first_specs = _ordered_after(body, after)
    return pl.pallas_call(body, out_shape=out_shape, grid=(t_len // tm, N_FF_TILES), in_specs=first_specs + in_specs,
                          out_specs=out_specs, scratch_shapes=[pltpu.VMEM((tm, D_MODEL), F32)], name=name,
                          compiler_params=_params(2))(*first, *args)


def _ffn_bwd_act(dy, x, g, gate, up, wg, wu, wd, name, after=None):
    t_len = x.shape[0]
    tm = min(FF_ROWS, t_len)

    def body(dy_ref, x_ref, g_ref, gg_ref, uu_ref, wg_ref, wu_ref, wd_ref,
             dyb_ref, act_ref, dgg_ref, duu_ref, dx_ref, dg_ref, acc):
        t, j = pl.program_id(0), pl.program_id(1)

        @pl.when(j == 0)
        def _():
            dyb_ref[...] = (0.5 * dy_ref[...]).astype(BF)
            acc[...] = jnp.zeros_like(acc)

        w_both = jnp.concatenate([wg_ref[...], wu_ref[...]], axis=0)
        for c in range(tm // FF_CHUNK):
            rows = slice(c * FF_CHUNK, (c + 1) * FF_CHUNK)
            d_act = _nt(dyb_ref[rows, :], wd_ref[...])
            gate = gg_ref[rows, :].astype(F32)
            upv = uu_ref[rows, :].astype(F32)
            sig = jax.nn.sigmoid(gate)
            silu = gate * sig
            d_up = (d_act * silu).astype(BF)
            d_gate = (d_act * upv * (sig * (1.0 + gate * (1.0 - sig)))).astype(BF)
            act_ref[rows, :] = (silu * upv).astype(BF)
            dgg_ref[rows, :] = d_gate
            duu_ref[rows, :] = d_up
            acc[rows, :] += _nn(jnp.concatenate([d_gate, d_up], axis=1), w_both)

        @pl.when(j == N_FF_TILES - 1)
        def _():
            xv = x_ref[...]
            dx, dg = _rms_bwd(acc[...], xv, _row_rms(xv), g_ref[...])
            dx_ref[...] = dy_ref[...] + dx

            @pl.when(t == 0)
            def _():
                dg_ref[...] = jnp.zeros_like(dg_ref)

            dg_ref[...] += dg

    row = pl.BlockSpec((tm, D_MODEL), lambda t, j: (t, 0))
    vec = pl.BlockSpec((1, D_MODEL), lambda t, j: (0, 0))
    w_spec = pl.BlockSpec((FF_TILE, D_MODEL), lambda t, j: (j, 0))
    blk = pl.BlockSpec((tm, FF_TILE), lambda t, j: (t, j))
    blk_shape = jax.ShapeDtypeStruct((t_len, D_FF), BF)
    body, first, first_specs = _ordered_after(body, after)
    return pl.pallas_call(
        body,
        out_shape=[jax.ShapeDtypeStruct((t_len, D_MODEL), BF), blk_shape, blk_shape, blk_shape,
                   jax.ShapeDtypeStruct((t_len, D_MODEL), F32), jax.ShapeDtypeStruct((1, D_MODEL), F32)],
        grid=(t_len // tm, N_FF_TILES), in_specs=first_specs + [row, row, vec, blk, blk, w_spec, w_spec, w_spec],
        out_specs=[row, blk, blk, blk, row, vec], scratch_shapes=[pltpu.VMEM((tm, D_MODEL), F32)], name=name,
        compiler_params=_params(2))(*first, dy, x, g, gate, up, _ff_rows(wg), _ff_rows(wu), _ff_rows(wd))


def _ffn_bwd_w(h, dyb, act, d_gate, d_up, name):
    t_len = h.shape[0]
    tm = min(FF_ROWS_W, t_len)
    nt = t_len // tm

    def body(h_ref, dyb_ref, act_ref, dgg_ref, duu_ref, dwg_ref, dwu_ref, dwd_ref, ag, au, ad):
        t = pl.program_id(1)

        @pl.when(t == 0)
        def _():
            ag[...] = jnp.zeros_like(ag)
            au[...] = jnp.zeros_like(au)
            ad[...] = jnp.zeros_like(ad)

        h = h_ref[...]
        ag[...] += _tn(dgg_ref[...], h)
        au[...] += _tn(duu_ref[...], h)
        ad[...] += _tn(act_ref[...], dyb_ref[...])

        @pl.when(t == nt - 1)
        def _():
            dwg_ref[...] = ag[...].astype(BF)
            dwu_ref[...] = au[...].astype(BF)
            dwd_ref[...] = ad[...].astype(BF)

    row = pl.BlockSpec((tm, D_MODEL), lambda j, t: (t, 0))
    blk = pl.BlockSpec((tm, FF_TILE), lambda j, t: (t, j))
    w_spec = pl.BlockSpec((FF_TILE, D_MODEL), lambda j, t: (j, 0))
    grads = pl.pallas_call(
        body,
        out_shape=[jax.ShapeDtypeStruct((D_FF, D_MODEL), BF)] * 3,
        grid=(N_FF_TILES, nt), in_specs=[row, row, blk, blk, blk], out_specs=[w_spec] * 3,
        scratch_shapes=[pltpu.VMEM((FF_TILE, D_MODEL), F32)] * 3,
        name=name, compiler_params=_params(2))(h, dyb, act, d_gate, d_up)
    return [a.reshape(N_DEV, FF_BLOCK, D_MODEL) for a in grads]


def _in_proj_fwd(x, g, w_in, name):
    t_len = x.shape[0]
    tm = min(512, t_len)

    def body(x_ref, g_ref, w_ref, h_ref, p_ref):
        xv = x_ref[...]
        h = (xv * _row_rms(xv) * g_ref[...]).astype(BF)
        h_ref[...] = h
        p_ref[...] = _nt(h, w_ref[...])

    row = pl.BlockSpec((tm, D_MODEL), lambda t: (t, 0))
    return pl.pallas_call(
        body, out_shape=[jax.ShapeDtypeStruct((t_len, D_MODEL), BF), jax.ShapeDtypeStruct((t_len, IN_COLS), F32)],
        grid=(t_len // tm,),
        in_specs=[row, pl.BlockSpec((1, D_MODEL), lambda t: (0, 0)), pl.BlockSpec((IN_COLS, D_MODEL), lambda t: (0, 0))],
        out_specs=[row, pl.BlockSpec((tm, IN_COLS), lambda t: (t, 0))], name=name, compiler_params=_params(1))(x, g, w_in)


def _in_proj_bwd(dq, dk, dv, du, dmq, w_in, h, x, g, dres, name):
    t_len = x.shape[0]
    tm = min(512, t_len)
    nt = t_len // tm
    groups = [(0, Q_COLS), (Q_COLS, KV_COLS), (Q_COLS + KV_COLS, KV_COLS), (Q_COLS + 2 * KV_COLS, 2 * CONV_CH),
              (Q_COLS + 2 * KV_COLS + 2 * CONV_CH, MQ_COLS)]

    def body(dq_ref, dk_ref, dv_ref, du_ref, dmq_ref, w_ref, h_ref, x_ref, g_ref, dres_ref, dx_ref, dg_ref, dw_ref, acc):
        t = pl.program_id(0)

        @pl.when(t == 0)
        def _():
            acc[...] = jnp.zeros_like(acc)
            dg_ref[...] = jnp.zeros_like(dg_ref)

        h = h_ref[...]
        dh = jnp.zeros((tm, D_MODEL), F32)
        for (start, width), ref in zip(groups, (dq_ref, dk_ref, dv_ref, du_ref, dmq_ref)):
            piece = ref[...]
            dh = dh + _nn(piece, w_ref[start:start + width, :])
            acc[start:start + width, :] += _tn(piece, h)
        xv = x_ref[...]
        dx, dg = _rms_bwd(dh, xv, _row_rms(xv), g_ref[...])
        dx_ref[...] = dres_ref[...] + dx
        dg_ref[...] += dg

        @pl.when(t == nt - 1)
        def _():
            dw_ref[...] = acc[...].astype(BF)

    def cols(width):
        return pl.BlockSpec((tm, width), lambda t: (t, 0))

    row = cols(D_MODEL)
    vec = pl.BlockSpec((1, D_MODEL), lambda t: (0, 0))
    full = pl.BlockSpec((IN_COLS, D_MODEL), lambda t: (0, 0))
    return pl.pallas_call(
        body,
        out_shape=[jax.ShapeDtypeStruct((t_len, D_MODEL), F32), jax.ShapeDtypeStruct((1, D_MODEL), F32),
                   jax.ShapeDtypeStruct((IN_COLS, D_MODEL), BF)],
        grid=(nt,),
        in_specs=[cols(Q_COLS), cols(KV_COLS), cols(KV_COLS), cols(2 * CONV_CH), cols(MQ_COLS), full, row, row, vec, row],
        out_specs=[row, vec, full], scratch_shapes=[pltpu.VMEM((IN_COLS, D_MODEL), F32)], name=name,
        compiler_params=_params(1))(dq, dk, dv, du, dmq, w_in, h, x, g, dres)


def _attn_fwd(proj, cos_t, sin_t, gq2, gk2, sinks, name):
    t_len = proj.shape[0]
    nb = t_len // ATTN_BLOCK

    def body(q_ref, kvc_ref, kvp_ref, cq_ref, sq_ref, cp_ref, sp_ref, gq_ref, gk_ref, sk_ref, ya_ref, lse_ref):
        n = pl.program_id(0)
        cq, sq = cq_ref[...], sq_ref[...]
        c_all = jnp.concatenate([cp_ref[...], cq], axis=0)
        s_all = jnp.concatenate([sp_ref[...], sq], axis=0)
        kv = jnp.concatenate([kvp_ref[...], kvc_ref[...]], axis=0)
        k_raw, v = kv[:, :PAIR], kv[:, PAIR:]
        q_raws = [q_ref[:, pr * PAIR:(pr + 1) * PAIR] for pr in range(N_Q_HEADS // 2)]
        rms = _pair_rms_many(q_raws + [k_raw])
        kn = _rope(k_raw * rms[-1] * gk_ref[...], c_all, s_all)
        row = lax.broadcasted_iota(jnp.int32, (ATTN_BLOCK, 2 * ATTN_BLOCK), 0)
        col = lax.broadcasted_iota(jnp.int32, (ATTN_BLOCK, 2 * ATTN_BLOCK), 1)
        rel = row + ATTN_BLOCK - col
        valid = (rel >= 0) & (rel < ATTN_BLOCK) & ((col >= ATTN_BLOCK) | (n > 0))
        for pr in range(N_Q_HEADS // 2):
            qn = _rope(q_raws[pr] * rms[pr] * gq_ref[...], cq, sq)
            kvh = (2 * pr) // Q_PER_KV
            kh = kn[:, kvh * HEAD_DIM:(kvh + 1) * HEAD_DIM].astype(BF)
            v_ones = jnp.concatenate([v[:, kvh * HEAD_DIM:(kvh + 1) * HEAD_DIM],
                                      jnp.ones((2 * ATTN_BLOCK, HEAD_DIM), F32)], axis=1).astype(BF)
            for hh in range(2):
                hd = 2 * pr + hh
                qh = qn[:, hh * HEAD_DIM:(hh + 1) * HEAD_DIM].astype(BF)
                s = jnp.where(valid, _nt(qh, kh) * SCALE, NEG)
                sink = sk_ref[0:1, hd:hd + 1]
                m = jnp.maximum(jnp.max(s, axis=-1, keepdims=True), sink)
                ov = _nn(jnp.exp(s - m).astype(BF), v_ones)
                den = ov[:, HEAD_DIM:HEAD_DIM + 1] + jnp.exp(sink - m)
                ya_ref[:, hd * HEAD_DIM:(hd + 1) * HEAD_DIM] = (ov[:, :HEAD_DIM] / den).astype(BF)
                lse_ref[:, hd:hd + 1] = m + jnp.log(den)

    prev = lambda n: (jnp.maximum(n - 1, 0), 0)
    tab = pl.BlockSpec((ATTN_BLOCK, PAIR), lambda n: (n, 0))
    tab_p = pl.BlockSpec((ATTN_BLOCK, PAIR), prev)
    one = lambda w: pl.BlockSpec((1, w), lambda n: (0, 0))
    return pl.pallas_call(
        body, out_shape=[jax.ShapeDtypeStruct((t_len, Q_COLS), BF), jax.ShapeDtypeStruct((t_len, N_Q_HEADS), F32)],
        grid=(nb,),
        in_specs=[pl.BlockSpec((ATTN_BLOCK, Q_COLS), lambda n: (n, 0)),
                  pl.BlockSpec((ATTN_BLOCK, 2 * KV_COLS), lambda n: (n, 2)),
                  pl.BlockSpec((ATTN_BLOCK, 2 * KV_COLS), lambda n: (jnp.maximum(n - 1, 0), 2)),
                  tab, tab, tab_p, tab_p, one(PAIR), one(PAIR), one(N_Q_HEADS)],
        out_specs=[pl.BlockSpec((ATTN_BLOCK, Q_COLS), lambda n: (n, 0)),
                   pl.BlockSpec((ATTN_BLOCK, N_Q_HEADS), lambda n: (n, 0))],
        name=name, compiler_params=_params(1))(proj, proj, proj, cos_t, sin_t, cos_t, sin_t, gq2, gk2, sinks)


def _attn_bwd_q(proj, cos_t, sin_t, gq2, gk2, sinks, ya, dmix, lse, name):
    t_len = proj.shape[0]
    nb = t_len // ATTN_BLOCK

    def body(q_ref, kvc_ref, kvp_ref, cq_ref, sq_ref, cp_ref, sp_ref, gq_ref, gk_ref, sk_ref, ya_ref, do_ref, lse_ref,
             dq_ref, dgq_ref, dsk_ref, scr):
        n = pl.program_id(0)

        @pl.when(n == 0)
        def _():
            dgq_ref[...] = jnp.zeros_like(dgq_ref)
            dsk_ref[...] = jnp.zeros_like(dsk_ref)

        cq, sq = cq_ref[...], sq_ref[...]
        c_all = jnp.concatenate([cp_ref[...], cq], axis=0)
        s_all = jnp.concatenate([sp_ref[...], sq], axis=0)
        kv = jnp.concatenate([kvp_ref[...], kvc_ref[...]], axis=0)
        k_raw, v = kv[:, :PAIR], kv[:, PAIR:]
        n_pairs = N_Q_HEADS // 2
        pair_lanes = [slice(pr * PAIR, (pr + 1) * PAIR) for pr in range(n_pairs)]
        q_raws = [q_ref[:, lanes] for lanes in pair_lanes]
        rms = _pair_rms_many(q_raws + [k_raw])
        dsums = _pair_sums([do_ref[:, lanes] * ya_ref[:, lanes].astype(F32) for lanes in pair_lanes])
        kn = _rope(k_raw * rms[-1] * gk_ref[...], c_all, s_all)
        row = lax.broadcasted_iota(jnp.int32, (ATTN_BLOCK, 2 * ATTN_BLOCK), 0)
        col = lax.broadcasted_iota(jnp.int32, (ATTN_BLOCK, 2 * ATTN_BLOCK), 1)
        rel = row + ATTN_BLOCK - col
        valid = (rel >= 0) & (rel < ATTN_BLOCK) & ((col >= ATTN_BLOCK) | (n > 0))
        gq = gq_ref[...]
        for pr in range(n_pairs):
            qn = _rope(q_raws[pr] * rms[pr] * gq, cq, sq)
            kvh = (2 * pr) // Q_PER_KV
            kh = kn[:, kvh * HEAD_DIM:(kvh + 1) * HEAD_DIM].astype(BF)
            vh = v[:, kvh * HEAD_DIM:(kvh + 1) * HEAD_DIM].astype(BF)
            dsum2 = dsums[pr]
            for hh in range(2):
                hd = 2 * pr + hh
                qh = qn[:, hh * HEAD_DIM:(hh + 1) * HEAD_DIM].astype(BF)
                lse_h = lse_ref[:, hd:hd + 1]
                p = jnp.exp(jnp.where(valid, _nt(qh, kh) * SCALE, NEG) - lse_h)
                d_o = do_ref[:, hd * HEAD_DIM:(hd + 1) * HEAD_DIM]
                dsum = dsum2[:, hh * HEAD_DIM:hh * HEAD_DIM + 1]
                ds = p * (_nt(d_o.astype(BF), vh) - dsum)
                scr[pr, :, hh * HEAD_DIM:(hh + 1) * HEAD_DIM] = _nn(ds.astype(BF), kh) * SCALE
                sink = sk_ref[0:1, hd:hd + 1]
                dsk_ref[0:1, hd:hd + 1] += -jnp.sum(jnp.exp(sink - lse_h) * dsum, axis=0, keepdims=True)
        back = _pair_rms_bwd_many([_rope_t(scr[pr], cq, sq) for pr in range(n_pairs)], q_raws, rms[:n_pairs], gq)
        for lanes, (dx, dg) in zip(pair_lanes, back):
            dq_ref[:, lanes] = dx.astype(BF)
            dgq_ref[...] += dg

    prev = lambda n: (jnp.maximum(n - 1, 0), 0)
    tab = pl.BlockSpec((ATTN_BLOCK, PAIR), lambda n: (n, 0))
    tab_p = pl.BlockSpec((ATTN_BLOCK, PAIR), prev)
    one = lambda w: pl.BlockSpec((1, w), lambda n: (0, 0))
    qblk = pl.BlockSpec((ATTN_BLOCK, Q_COLS), lambda n: (n, 0))
    return pl.pallas_call(
        body,
        out_shape=[jax.ShapeDtypeStruct((t_len, Q_COLS), BF), jax.ShapeDtypeStruct((1, PAIR), F32),
                   jax.ShapeDtypeStruct((1, N_Q_HEADS), F32)],
        grid=(nb,),
        in_specs=[qblk, pl.BlockSpec((ATTN_BLOCK, 2 * KV_COLS), lambda n: (n, 2)),
                  pl.BlockSpec((ATTN_BLOCK, 2 * KV_COLS), lambda n: (jnp.maximum(n - 1, 0), 2)),
                  tab, tab, tab_p, tab_p, one(PAIR), one(PAIR), one(N_Q_HEADS), qblk, qblk,
                  pl.BlockSpec((ATTN_BLOCK, N_Q_HEADS), lambda n: (n, 0))],
        out_specs=[qblk, one(PAIR), one(N_Q_HEADS)], scratch_shapes=[pltpu.VMEM((N_Q_HEADS // 2, ATTN_BLOCK, PAIR), F32)],
        name=name, compiler_params=_params(1))(proj, proj, proj, cos_t, sin_t, cos_t, sin_t, gq2, gk2, sinks, ya, dmix, lse)


def _attn_bwd_kv(proj, cos_t, sin_t, gq2, gk2, ya, dmix, lse, name):
    t_len = proj.shape[0]
    nb = t_len // ATTN_BLOCK

    def body(kv_ref, q0_ref, q1_ref, ck_ref, sk_ref, c1_ref, s1_ref, gq_ref, gk_ref, o0_ref, o1_ref, do0_ref, do1_ref,
             l0_ref, l1_ref, dk_ref, dv_ref, dgk_ref, dkn_scr, dv_scr):
        m = pl.program_id(0)

        @pl.when(m == 0)
        def _():
            dgk_ref[...] = jnp.zeros_like(dgk_ref)

        ck, sk = ck_ref[...], sk_ref[...]
        c_all = jnp.concatenate([ck, c1_ref[...]], axis=0)
        s_all = jnp.concatenate([sk, s1_ref[...]], axis=0)
        kvv = kv_ref[...]
        k_raw, v = kvv[:, :PAIR], kvv[:, PAIR:]
        pair_lanes = [slice(pr * PAIR, (pr + 1) * PAIR) for pr in range(N_Q_HEADS // 2)]
        q_raws = [jnp.concatenate([q0_ref[:, lanes], q1_ref[:, lanes]], axis=0) for lanes in pair_lanes]
        d_os = [jnp.concatenate([do0_ref[:, lanes], do1_ref[:, lanes]], axis=0) for lanes in pair_lanes]
        rms = [_pair_rms(a) for a in q_raws + [k_raw]]
        rk = rms[-1]
        gk = gk_ref[...]
        kn = _rope(k_raw * rk * gk, ck, sk)
        key = lax.broadcasted_iota(jnp.int32, (ATTN_BLOCK, 2 * ATTN_BLOCK), 0)
        qry = lax.broadcasted_iota(jnp.int32, (ATTN_BLOCK, 2 * ATTN_BLOCK), 1)
        valid = ((qry < ATTN_BLOCK) & (qry >= key)) | ((qry >= ATTN_BLOCK) & (qry - ATTN_BLOCK < key) & (m < nb - 1))
        lse_rows = _as_rows(jnp.concatenate([l0_ref[...], l1_ref[...]], axis=0))
        d_kn, d_v = [None, None], [None, None]
        plus = lambda acc, term: term if acc is None else acc + term
        for pr in range(N_Q_HEADS // 2):
            qn = _rope(q_raws[pr] * rms[pr] * gq_ref[...], c_all, s_all)
            d_o2, lanes = d_os[pr], pair_lanes[pr]
            dsum_rows = _head_sums_as_rows(d_o2 * jnp.concatenate([o0_ref[:, lanes], o1_ref[:, lanes]], axis=0).astype(F32))
            kvh = (2 * pr) // Q_PER_KV
            khs = slice(kvh * HEAD_DIM, (kvh + 1) * HEAD_DIM)
            kh = kn[:, khs].astype(BF)
            vh = v[:, khs].astype(BF)
            for hh in range(2):
                hd = 2 * pr + hh
                hs = slice(hh * HEAD_DIM, (hh + 1) * HEAD_DIM)
                qh = qn[:, hs].astype(BF)
                d_oh = d_o2[:, hs].astype(BF)
                p = jnp.exp(jnp.where(valid, _nt(kh, qh) * SCALE, NEG) - lse_rows[hd:hd + 1, :])
                d_v[kvh] = plus(d_v[kvh], _nn(p.astype(BF), d_oh))
                ds = p * (_nt(vh, d_oh) - dsum_rows[hh:hh + 1, :])
                d_kn[kvh] = plus(d_kn[kvh], _nn(ds.astype(BF), qh))
        for kvh in range(2):
            khs = slice(kvh * HEAD_DIM, (kvh + 1) * HEAD_DIM)
            dkn_scr[:, khs] = d_kn[kvh] * SCALE
            dv_scr[:, khs] = d_v[kvh]
        dx, dg = _pair_rms_bwd(_rope_t(dkn_scr[...], ck, sk), k_raw, rk, gk)
        dk_ref[...] = dx.astype(BF)
        dv_ref[...] = dv_scr[...].astype(BF)
        dgk_ref[...] += dg

    nxt = lambda m: (jnp.minimum(m + 1, nb - 1), 0)
    cur = lambda m: (m, 0)
    tab = lambda f: pl.BlockSpec((ATTN_BLOCK, PAIR), f)
    qb = lambda f: pl.BlockSpec((ATTN_BLOCK, Q_COLS), f)
    lb = lambda f: pl.BlockSpec((ATTN_BLOCK, N_Q_HEADS), f)
    one = pl.BlockSpec((1, PAIR), lambda m: (0, 0))
    return pl.pallas_call(
        body,
        out_shape=[jax.ShapeDtypeStruct((t_len, KV_COLS), BF), jax.ShapeDtypeStruct((t_len, KV_COLS), BF),
                   jax.ShapeDtypeStruct((1, PAIR), F32)],
        grid=(nb,),
        in_specs=[pl.BlockSpec((ATTN_BLOCK, 2 * KV_COLS), lambda m: (m, 2)), qb(cur), qb(nxt), tab(cur), tab(cur), tab(nxt),
                  tab(nxt), one, one, qb(cur), qb(nxt), qb(cur), qb(nxt), lb(cur), lb(nxt)],
        out_specs=[tab(cur), tab(cur), one],
        scratch_shapes=[pltpu.VMEM((ATTN_BLOCK, PAIR), F32), pltpu.VMEM((ATTN_BLOCK, PAIR), F32)],
        name=name, compiler_params=_params(1))(proj, proj, proj, cos_t, sin_t, cos_t, sin_t, gq2, gk2, ya, ya, dmix, dmix,
                                               lse, lse)


def _conv_taps(blk, w, offset_of):
    acc = jnp.zeros((CONV_CHUNK, CONV_CH), F32)
    for k in range(CONV_WIDTH):
        o = offset_of(k)
        acc = acc + w[k:k + 1, :] * blk[o:o + CONV_CHUNK, :]
    return acc


def _conv_fwd(proj, w_dw, b_dw, g_ln, b_ln, name):
    t_len = proj.shape[0]
    nc = t_len // CONV_CHUNK

    def body(a_ref, gt_ref, w_ref, b_ref, g_ref, bl_ref, yc_ref, cv_ref, pad):
        pad[0:CONV_PAD, :] = jnp.zeros((CONV_PAD, CONV_CH), F32)

        def glu(c, carry):
            rows = pl.ds(pl.multiple_of(c * CONV_CHUNK, CONV_CHUNK), CONV_CHUNK)
            dst = pl.ds(pl.multiple_of(c * CONV_CHUNK + CONV_PAD, CONV_PAD), CONV_CHUNK)
            pad[dst, :] = a_ref[rows, :] * jax.nn.sigmoid(gt_ref[rows, :])
            return carry

        lax.fori_loop(0, nc, glu, 0)
        w = w_ref[...]

        def conv(c, carry):
            base = pl.multiple_of(c * CONV_CHUNK, CONV_CHUNK)
            blk = pad[pl.ds(base, CONV_CHUNK + CONV_PAD), :]
            cv = _conv_taps(blk, w, lambda k: k + CONV_PAD - (CONV_WIDTH - 1)) + b_ref[...]
            mu = jnp.mean(cv, axis=-1, keepdims=True)
            xc = cv - mu
            z = xc * lax.rsqrt(jnp.mean(xc * xc, axis=-1, keepdims=True) + EPS) * g_ref[...] + bl_ref[...]
            rows = pl.ds(base, CONV_CHUNK)
            cv_ref[rows, :] = cv
            yc_ref[rows, :] = (z * jax.nn.sigmoid(z)).astype(BF)
            return carry

        lax.fori_loop(0, nc, conv, 0)

    vec = pl.BlockSpec((1, CONV_CH), lambda i: (0, 0))
    full = pl.BlockSpec((t_len, CONV_CH), lambda i: (0, 0))
    return pl.pallas_call(
        body, out_shape=[jax.ShapeDtypeStruct((t_len, CONV_CH), BF), jax.ShapeDtypeStruct((t_len, CONV_CH), F32)],
        grid=(1,),
        in_specs=[pl.BlockSpec((t_len, CONV_CH), lambda i: (0, 3)), pl.BlockSpec((t_len, CONV_CH), lambda i: (0, 4)),
                  pl.BlockSpec((CONV_WIDTH, CONV_CH), lambda i: (0, 0)), vec, vec, vec],
        out_specs=[full, full], scratch_shapes=[pltpu.VMEM((t_len + CONV_PAD, CONV_CH), F32)], name=name,
        compiler_params=_params(1))(proj, proj, w_dw, b_dw, g_ln, b_ln)


def _conv_bwd(proj, cv, dmix, w_dw, g_ln, b_ln, name):
    t_len = proj.shape[0]
    nc = t_len // CONV_CHUNK

    def body(a_ref, gt_ref, cv_ref, dy_ref, w_ref, g_ref, bl_ref, du_ref, dw_ref, db_ref, dgl_ref, dbl_ref, pad, dpad):
        pad[0:CONV_PAD, :] = jnp.zeros((CONV_PAD, CONV_CH), F32)
        dpad[t_len:t_len + CONV_PAD, :] = jnp.zeros((CONV_PAD, CONV_CH), F32)
        dw_ref[...] = jnp.zeros_like(dw_ref)
        db_ref[...] = jnp.zeros_like(db_ref)
        dgl_ref[...] = jnp.zeros_like(dgl_ref)
        dbl_ref[...] = jnp.zeros_like(dbl_ref)

        def norm_bwd(c, carry):
            base = pl.multiple_of(c * CONV_CHUNK, CONV_CHUNK)
            rows = pl.ds(base, CONV_CHUNK)
            dst = pl.ds(pl.multiple_of(c * CONV_CHUNK + CONV_PAD, CONV_PAD), CONV_CHUNK)
            pad[dst, :] = a_ref[rows, :] * jax.nn.sigmoid(gt_ref[rows, :])
            cvv = cv_ref[rows, :]
            xc = cvv - jnp.mean(cvv, axis=-1, keepdims=True)
            rs = lax.rsqrt(jnp.mean(xc * xc, axis=-1, keepdims=True) + EPS)
            xhat = xc * rs
            z = xhat * g_ref[...] + bl_ref[...]
            sg = jax.nn.sigmoid(z)
            dz = dy_ref[rows, :] * (sg * (1.0 + z * (1.0 - sg)))
            dgl_ref[...] += jnp.sum(dz * xhat, axis=0, keepdims=True)
            dbl_ref[...] += jnp.sum(dz, axis=0, keepdims=True)
            dxh = dz * g_ref[...]
            dcv = rs * (dxh - jnp.mean(dxh, axis=-1, keepdims=True) - xhat * jnp.mean(dxh * xhat, axis=-1, keepdims=True))
            db_ref[...] += jnp.sum(dcv, axis=0, keepdims=True)
            dpad[rows, :] = dcv
            return carry

        lax.fori_loop(0, nc, norm_bwd, 0)
        w = w_ref[...]

        def conv_bwd(c, carry):
            base = pl.multiple_of(c * CONV_CHUNK, CONV_CHUNK)
            rows = pl.ds(base, CONV_CHUNK)
            dblk = dpad[pl.ds(base, CONV_CHUNK + CONV_PAD), :]
            dhc = _conv_taps(dblk, w, lambda k: CONV_WIDTH - 1 - k)
            a = a_ref[rows, :]
            sg = jax.nn.sigmoid(gt_ref[rows, :])
            du_ref[rows, 0:CONV_CH] = (dhc * sg).astype(BF)
            du_ref[rows, CONV_CH:2 * CONV_CH] = (dhc * a * sg * (1.0 - sg)).astype(BF)
            hblk = pad[pl.ds(base, CONV_CHUNK + CONV_PAD), :]
            dcv = dblk[0:CONV_CHUNK, :]
            for k in range(CONV_WIDTH):
                o = k + CONV_PAD - (CONV_WIDTH - 1)
                dw_ref[k:k + 1, :] += jnp.sum(dcv * hblk[o:o + CONV_CHUNK, :], axis=0, keepdims=True)
            return carry

        lax.fori_loop(0, nc, conv_bwd, 0)

    vec = pl.BlockSpec((1, CONV_CH), lambda i: (0, 0))
    full = pl.BlockSpec((t_len, CONV_CH), lambda i: (0, 0))
    wspec = pl.BlockSpec((CONV_WIDTH, CONV_CH), lambda i: (0, 0))
    vshape = jax.ShapeDtypeStruct((1, CONV_CH), F32)
    return pl.pallas_call(
        body,
        out_shape=[jax.ShapeDtypeStruct((t_len, 2 * CONV_CH), BF), jax.ShapeDtypeStruct((CONV_WIDTH, CONV_CH), F32),
                   vshape, vshape, vshape],
        grid=(1,),
        in_specs=[pl.BlockSpec((t_len, CONV_CH), lambda i: (0, 3)), pl.BlockSpec((t_len, CONV_CH), lambda i: (0, 4)), full,
                  pl.BlockSpec((t_len, CONV_CH), lambda i: (0, 2)), wspec, vec, vec],
        out_specs=[pl.BlockSpec((t_len, 2 * CONV_CH), lambda i: (0, 0)), wspec, vec, vec, vec],
        scratch_shapes=[pltpu.VMEM((t_len + CONV_PAD, CONV_CH), F32), pltpu.VMEM((t_len + CONV_PAD, CONV_CH), F32)],
        name=name, compiler_params=_params(1))(proj, proj, cv, dmix, w_dw, g_ln, b_ln)


def _mem_kv_fwd(mem, g, w, name):
    def body(m_ref, g_ref, w_ref, h_ref, kv_ref):
        mv = m_ref[...]
        h = (mv * _row_rms(mv) * g_ref[...]).astype(BF)
        h_ref[...] = h
        kv_ref[...] = _nn(h, w_ref[...])

    m_len = mem.shape[0]
    return pl.pallas_call(
        body, out_shape=[jax.ShapeDtypeStruct((m_len, D_MODEL), BF), jax.ShapeDtypeStruct((m_len, 2 * MQ_COLS), F32)],
        name=name, compiler_params=pltpu.CompilerParams(vmem_limit_bytes=VMEM_LIMIT_BYTES))(mem, g, w)


def _mem_kv_bwd(mem, g, h, w, dkv, name):
    def body(m_ref, g_ref, h_ref, w_ref, dkv_ref, dw_ref, dg_ref):
        dkv_b = dkv_ref[...].astype(BF)
        dw_ref[...] = _tn(h_ref[...], dkv_b).astype(BF)
        mv = m_ref[...]
        dg_ref[...] = jnp.sum(_nt(dkv_b, w_ref[...]) * mv * _row_rms(mv), axis=0, keepdims=True)

    return pl.pallas_call(
        body, out_shape=[jax.ShapeDtypeStruct((D_MODEL, 2 * MQ_COLS), BF), jax.ShapeDtypeStruct((1, D_MODEL), F32)],
        name=name, compiler_params=pltpu.CompilerParams(vmem_limit_bytes=VMEM_LIMIT_BYTES))(mem, g, h, w, dkv)


def _mem_attn_fwd(proj, mkv, gq2, gk2, name):
    t_len = proj.shape[0]
    tm = min(256, t_len)

    def body(q_ref, kv_ref, gq_ref, gk_ref, ym_ref, lse_ref):
        kvv = kv_ref[...]
        for pr in range(N_MEM_HEADS // 2):
            lanes = slice(pr * PAIR, (pr + 1) * PAIR)
            k_raw = kvv[:, lanes]
            kn = k_raw * _pair_rms(k_raw) * gk_ref[...]
            v = kvv[:, MQ_COLS + pr * PAIR:MQ_COLS + (pr + 1) * PAIR]
            q_raw = q_ref[:, lanes]
            qn = q_raw * _pair_rms(q_raw) * gq_ref[...]
            for hh in range(2):
                hd = 2 * pr + hh
                hs = slice(hh * HEAD_DIM, (hh + 1) * HEAD_DIM)
                s = _nt(qn[:, hs].astype(BF), kn[:, hs].astype(BF)) * SCALE
                m = jnp.max(s, axis=-1, keepdims=True)
                p = jnp.exp(s - m)
                den = jnp.sum(p, axis=-1, keepdims=True)
                ym_ref[:, hd * HEAD_DIM:(hd + 1) * HEAD_DIM] = _nn((p / den).astype(BF), v[:, hs].astype(BF)).astype(BF)
                lse_ref[:, hd:hd + 1] = m + jnp.log(den)

    m_len = mkv.shape[0]
    one = pl.BlockSpec((1, PAIR), lambda t: (0, 0))
    return pl.pallas_call(
        body, out_shape=[jax.ShapeDtypeStruct((t_len, MQ_COLS), BF), jax.ShapeDtypeStruct((t_len, N_MEM_HEADS), F32)],
        grid=(t_len // tm,),
        in_specs=[pl.BlockSpec((tm, MQ_COLS), lambda t: (t, 5)), pl.BlockSpec((m_len, 2 * MQ_COLS), lambda t: (0, 0)), one, one],
        out_specs=[pl.BlockSpec((tm, MQ_COLS), lambda t: (t, 0)), pl.BlockSpec((tm, N_MEM_HEADS), lambda t: (t, 0))],
        name=name, compiler_params=_params(1))(proj, mkv, gq2, gk2)


def _mem_attn_bwd(proj, mkv, gq2, gk2, ym, dmix, lse, name):
    t_len = proj.shape[0]
    tm = min(256, t_len)
    nt = t_len // tm
    m_len = mkv.shape[0]

    def body(q_ref, kv_ref, gq_ref, gk_ref, ym_ref, do_ref, lse_ref, dq_ref, dkv_ref, dgq_ref, dgk_ref, dkn_scr, dv_scr, scr):
        t = pl.program_id(0)

        @pl.when(t == 0)
        def _():
            dkn_scr[...] = jnp.zeros_like(dkn_scr)
            dv_scr[...] = jnp.zeros_like(dv_scr)
            dgq_ref[...] = jnp.zeros_like(dgq_ref)

        kvv = kv_ref[...]
        for pr in range(N_MEM_HEADS // 2):
            lanes = slice(pr * PAIR, (pr + 1) * PAIR)
            k_raw = kvv[:, lanes]
            kn = k_raw * _pair_rms(k_raw) * gk_ref[...]
            v = kvv[:, MQ_COLS + pr * PAIR:MQ_COLS + (pr + 1) * PAIR]
            q_raw = q_ref[:, lanes]
            rq = _pair_rms(q_raw)
            qn = q_raw * rq * gq_ref[...]
            d_o = do_ref[:, lanes]
            dsum = _pair_sum(d_o * ym_ref[:, lanes].astype(F32))
            for hh in range(2):
                hd = 2 * pr + hh
                hs = slice(hh * HEAD_DIM, (hh + 1) * HEAD_DIM)
                cols = slice(hd * HEAD_DIM, (hd + 1) * HEAD_DIM)
                qh = qn[:, hs].astype(BF)
                kh = kn[:, hs].astype(BF)
                d_oh = d_o[:, hs].astype(BF)
                p = jnp.exp(_nt(qh, kh) * SCALE - lse_ref[:, hd:hd + 1])
                dv_scr[:, cols] += _tn(p.astype(BF), d_oh)
                ds = (p * (_nt(d_oh, v[:, hs].astype(BF)) - dsum[:, hh * HEAD_DIM:hh * HEAD_DIM + 1])).astype(BF)
                scr[pr, :, hs] = _nn(ds, kh) * SCALE
                dkn_scr[:, cols] += _tn(ds, qh) * SCALE
            dx, dg = _pair_rms_bwd(scr[pr], q_raw, rq, gq_ref[...])
            dq_ref[:, lanes] = dx.astype(BF)
            dgq_ref[...] += dg

        @pl.when(t == nt - 1)
        def _():
            dgk = jnp.zeros((1, PAIR), F32)
            for pr in range(N_MEM_HEADS // 2):
                lanes = slice(pr * PAIR, (pr + 1) * PAIR)
                k_raw = kvv[:, lanes]
                dx, dg = _pair_rms_bwd(dkn_scr[:, lanes], k_raw, _pair_rms(k_raw), gk_ref[...])
                dkv_ref[:, lanes] = dx
                dgk = dgk + dg
            dkv_ref[:, MQ_COLS:2 * MQ_COLS] = dv_scr[...]
            dgk_ref[...] = dgk

    one = pl.BlockSpec((1, PAIR), lambda t: (0, 0))
    kvspec = pl.BlockSpec((m_len, 2 * MQ_COLS), lambda t: (0, 0))
    qspec = pl.BlockSpec((tm, MQ_COLS), lambda t: (t, 0))
    return pl.pallas_call(
        body,
        out_shape=[jax.ShapeDtypeStruct((t_len, MQ_COLS), BF), jax.ShapeDtypeStruct((m_len, 2 * MQ_COLS), F32),
                   jax.ShapeDtypeStruct((1, PAIR), F32), jax.ShapeDtypeStruct((1, PAIR), F32)],
        grid=(nt,),
        in_specs=[pl.BlockSpec((tm, MQ_COLS), lambda t: (t, 5)), kvspec, one, one, qspec,
                  pl.BlockSpec((tm, MQ_COLS), lambda t: (t, 3)), pl.BlockSpec((tm, N_MEM_HEADS), lambda t: (t, 0))],
        out_specs=[qspec, kvspec, one, one],
        scratch_shapes=[pltpu.VMEM((m_len, MQ_COLS), F32), pltpu.VMEM((m_len, MQ_COLS), F32),
                        pltpu.VMEM((N_MEM_HEADS // 2, tm, PAIR), F32)],
        name=name, compiler_params=_params(1))(proj, mkv, gq2, gk2, ym, dmix, lse)


MIX_GROUPS = [(0, Q_COLS), (Q_COLS, CONV_CH), (Q_COLS + CONV_CH, MQ_COLS)]


def _out_proj_fwd(x, ya, yc, ym, w_out, name, after=None):
    t_len = x.shape[0]
    tm = min(512, t_len)

    def body(x_ref, ya_ref, yc_ref, ym_ref, w_ref, xo_ref):
        y = x_ref[...]
        for (start, width), ref in zip(MIX_GROUPS, (ya_ref, yc_ref, ym_ref)):
            y = y + _nn(ref[...], w_ref[start:start + width, :])
        xo_ref[...] = y

    cols = lambda w: pl.BlockSpec((tm, w), lambda t: (t, 0))
    body, first, first_specs = _ordered_after(body, after)
    return pl.pallas_call(
        body, out_shape=jax.ShapeDtypeStruct((t_len, D_MODEL), F32), grid=(t_len // tm,),
        in_specs=first_specs + [cols(D_MODEL), cols(Q_COLS), cols(CONV_CH), cols(MQ_COLS),
                                pl.BlockSpec((D_MODEL, D_MODEL), lambda t: (0, 0))],
        out_specs=cols(D_MODEL), name=name, compiler_params=_params(1))(*first, x, ya, yc, ym, w_out)


def _out_proj_bwd(dx, ya, yc, ym, w_out, name, after=None):
    t_len = dx.shape[0]
    tm = min(512, t_len)
    nt = t_len // tm

    def body(dx_ref, ya_ref, yc_ref, ym_ref, w_ref, dmix_ref, dw_ref, acc):
        t = pl.program_id(0)

        @pl.when(t == 0)
        def _():
            acc[...] = jnp.zeros_like(acc)

        dxb = dx_ref[...].astype(BF)
        dmix_ref[...] = _nt(dxb, w_ref[...])
        for (start, width), ref in zip(MIX_GROUPS, (ya_ref, yc_ref, ym_ref)):
            acc[start:start + width, :] += _tn(ref[...], dxb)

        @pl.when(t == nt - 1)
        def _():
            dw_ref[...] = acc[...].astype(BF)

    cols = lambda w: pl.BlockSpec((tm, w), lambda t: (t, 0))
    full = pl.BlockSpec((D_MODEL, D_MODEL), lambda t: (0, 0))
    body, first, first_specs = _ordered_after(body, after)
    return pl.pallas_call(
        body, out_shape=[jax.ShapeDtypeStruct((t_len, D_MODEL), F32), jax.ShapeDtypeStruct((D_MODEL, D_MODEL), BF)],
        grid=(nt,), in_specs=first_specs + [cols(D_MODEL), cols(Q_COLS), cols(CONV_CH), cols(MQ_COLS), full],
        out_specs=[cols(D_MODEL), full], scratch_shapes=[pltpu.VMEM((D_MODEL, D_MODEL), F32)], name=name,
        compiler_params=_params(1))(*first, dx, ya, yc, ym, w_out)


N_PEERS = N_DEV - 1
HBM_SPEC = pl.BlockSpec(memory_space=pltpu.HBM)
SEM_SPEC = pl.BlockSpec(memory_space=pltpu.SEMAPHORE)
EFFECT = pltpu.SideEffectType.DATAFLOW_SIDE_EFFECTING


def _my_place():
    x, y, c = lax.axis_index("x"), lax.axis_index("y"), lax.axis_index("c")
    return x, y, c, 4 * x + 2 * y + c


def _peer(k):
    x, y, c, _ = _my_place()
    px = 1 - x if k & 4 else x
    py = 1 - y if k & 2 else y
    pc = 1 - c if k & 1 else c
    return (px, py, pc), 4 * px + 2 * py + pc


def _gather_two_level(shards, name):
    n = len(shards)

    def body(*refs):
        ins, outs = refs[:n], refs[n:2 * n]
        send_sems, recv_sems, local_sems = refs[2 * n:]
        x, y, c, me = _my_place()
        sibling = (x, y, 1 - c)
        chips = [(1 - x, y), (x, 1 - y), (1 - x, 1 - y)]
        slot = lambda px, py, pc: 4 * px + 2 * py + pc

        def copy(w, k, src, dst_slot, to):
            return pltpu.make_async_remote_copy(
                src_ref=src, dst_ref=outs[w].at[dst_slot], send_sem=send_sems.at[w * N_PEERS + k],
                recv_sem=recv_sems.at[w * N_PEERS + k], device_id=to, device_id_type=pl.DeviceIdType.MESH)

        local = [pltpu.make_async_copy(ins[w], outs[w].at[me], local_sems.at[w]) for w in range(n)]
        for cp in local:
            cp.start()
        sends = []
        for w in range(n):
            for j, chip in enumerate(chips):
                sends.append(copy(w, 1 + j, ins[w], me, (*chip, c)))
            sends.append(copy(w, 0, ins[w], me, sibling))
        for cp in sends:
            cp.start()
        for w in range(n):
            for j, chip in enumerate(chips):
                got = slot(*chip, c)
                copy(w, 1 + j, ins[w], got, (*chip, c)).wait_recv()
                fwd = copy(w, 4 + j, outs[w].at[got], got, sibling)
                fwd.start()
                sends.append(fwd)
        for w in range(n):
            copy(w, 0, ins[w], slot(x, y, 1 - c), sibling).wait_recv()
            for j, chip in enumerate(chips):
                copy(w, 4 + j, ins[w], slot(*chip, 1 - c), sibling).wait_recv()
        for cp in sends:
            cp.wait_send()
        for cp in local:
            cp.wait()

    hbm = pl.BlockSpec(memory_space=pl.ANY)
    return pl.pallas_call(
        body, out_shape=[jax.ShapeDtypeStruct((N_DEV,) + a.shape, a.dtype) for a in shards], in_specs=[hbm] * n,
        out_specs=[hbm] * n,
        scratch_shapes=[pltpu.SemaphoreType.DMA((n * N_PEERS,)), pltpu.SemaphoreType.DMA((n * N_PEERS,)),
                        pltpu.SemaphoreType.DMA((n,))],
        name=name)(*shards)


def _split_copies(srcs, lands, gather, send_sems, recv_sems, first=0):
    _, _, _, me = _my_place()
    pairs = []
    for w in range(len(srcs)):
        for k in range(1, N_DEV):
            dev, idx = _peer(k)
            src = srcs[w] if gather[w] else srcs[w].at[idx]
            at = (first + w) * N_PEERS + k - 1
            sems = dict(send_sem=send_sems.at[at], recv_sem=recv_sems.at[at], device_id=dev,
                        device_id_type=pl.DeviceIdType.MESH)
            pairs.append((pltpu.make_async_remote_copy(src_ref=src, dst_ref=lands[w].at[me], **sems),
                          pltpu.make_async_remote_copy(src_ref=src, dst_ref=lands[w].at[idx], **sems)))
    return pairs


def _exchange_start(srcs, lands, gather, after, name):
    n = len(srcs)

    def body(*refs):
        src_refs, land_refs = refs[:n], refs[n:2 * n]
        send_sems, recv_sems = refs[2 * n + 1], refs[2 * n + 2]
        token = refs[-1]
        for out_going, _ in _split_copies(src_refs, land_refs, gather, send_sems, recv_sems):
            out_going.start()
        token[...] = jnp.zeros_like(token)

    arrays = list(srcs) + list(lands)
    out = pl.pallas_call(
        body, name=name,
        out_shape=(pltpu.SemaphoreType.DMA((n * N_PEERS,)), pltpu.SemaphoreType.DMA((n * N_PEERS,)),
                   *[pltpu.HBM(a.shape, a.dtype) for a in arrays], jax.ShapeDtypeStruct((8, 128), F32)),
        in_specs=[HBM_SPEC] * (2 * n) + [pl.BlockSpec(memory_space=pl.ANY)],
        out_specs=(SEM_SPEC, SEM_SPEC, *[HBM_SPEC] * (2 * n), pl.BlockSpec(memory_space=pltpu.VMEM)),
        input_output_aliases={i: i + 2 for i in range(2 * n)},
        compiler_params=pltpu.CompilerParams(has_side_effects=EFFECT),
    )(*[pltpu.with_memory_space_constraint(a, pltpu.HBM) for a in arrays], after)
    return out[0], out[1], out[2:2 + n], out[2 + n:2 + 2 * n], out[-1]


def _exchange_wait(send_sems, recv_sems, srcs, lands, gather, after, name, first=0, count=None):
    n = len(srcs)
    count = n if count is None else count

    def body(*refs):
        src_refs, land_refs = refs[:n], refs[n:2 * n]
        pairs = _split_copies(src_refs, land_refs, gather, refs[2 * n], refs[2 * n + 1], first)
        for out_going, arriving in pairs[:count * N_PEERS]:
            out_going.wait_send()
            arriving.wait_recv()

    arrays = list(srcs) + list(lands)
    out = pl.pallas_call(
        body, name=name, out_shape=tuple(pltpu.HBM(a.shape, a.dtype) for a in arrays),
        in_specs=[HBM_SPEC] * (2 * n) + [SEM_SPEC, SEM_SPEC, pl.BlockSpec(memory_space=pl.ANY)],
        out_specs=tuple([HBM_SPEC] * (2 * n)), input_output_aliases={i: i for i in range(2 * n)},
        compiler_params=pltpu.CompilerParams(has_side_effects=EFFECT),
    )(*arrays, send_sems, recv_sems, after)
    return list(out[:n]), list(out[n:])


SIBLING = 1
CHIP_PEERS = (2, 4, 6)
NEAR_PEERS = (SIBLING,) + CHIP_PEERS


def _near_copies(srcs, lands, send_sems, recv_sems):
    _, _, _, me = _my_place()
    pairs = []
    for w in range(len(srcs)):
        for i, k in enumerate(NEAR_PEERS):
            dev, idx = _peer(k)
            sems = dict(send_sem=send_sems.at[w * len(NEAR_PEERS) + i], recv_sem=recv_sems.at[w * len(NEAR_PEERS) + i],
                        device_id=dev, device_id_type=pl.DeviceIdType.MESH)
            pairs.append((pltpu.make_async_remote_copy(src_ref=srcs[w], dst_ref=lands[w].at[me], **sems),
                          pltpu.make_async_remote_copy(src_ref=srcs[w], dst_ref=lands[w].at[idx], **sems)))
    return pairs


def _forward_copies(lands, send_sems, recv_sems):
    sibling, _ = _peer(SIBLING)
    pairs = []
    for w in range(len(lands)):
        for i, k in enumerate(CHIP_PEERS):
            _, got = _peer(k)
            _, gets = _peer(k | SIBLING)
            sems = dict(send_sem=send_sems.at[w * len(CHIP_PEERS) + i], recv_sem=recv_sems.at[w * len(CHIP_PEERS) + i],
                        device_id=sibling, device_id_type=pl.DeviceIdType.MESH)
            pairs.append((pltpu.make_async_remote_copy(src_ref=lands[w].at[got], dst_ref=lands[w].at[got], **sems),
                          pltpu.make_async_remote_copy(src_ref=lands[w].at[got], dst_ref=lands[w].at[gets], **sems)))
    return pairs


def _split_call(body, name, arrays, sems_in, sems_out, after, token):
    n = len(arrays)
    out = pl.pallas_call(
        body, name=name,
        out_shape=(*[pltpu.SemaphoreType.DMA((c,)) for c in sems_out], *[pltpu.HBM(a.shape, a.dtype) for a in arrays],
                   *([jax.ShapeDtypeStruct((8, 128), F32)] if token else [])),
        in_specs=[HBM_SPEC] * n + [SEM_SPEC] * len(sems_in) + [pl.BlockSpec(memory_space=pl.ANY)],
        out_specs=(*[SEM_SPEC] * len(sems_out), *[HBM_SPEC] * n,
                   *([pl.BlockSpec(memory_space=pltpu.VMEM)] if token else [])),
        input_output_aliases={i: i + len(sems_out) for i in range(n)},
        compiler_params=pltpu.CompilerParams(has_side_effects=EFFECT),
    )(*[pltpu.with_memory_space_constraint(a, pltpu.HBM) for a in arrays], *sems_in, after)
    k = len(sems_out)
    return list(out[:k]), list(out[k:k + n]), (out[-1] if token else None)


def _gather_start(srcs, lands, after, name):
    n = len(srcs)

    def body(*refs):
        send1, recv1 = refs[2 * n + 1], refs[2 * n + 2]
        for out_going, _ in _near_copies(refs[:n], refs[n:2 * n], send1, recv1):
            out_going.start()
        refs[-1][...] = jnp.zeros_like(refs[-1])

    sems, arrays, token = _split_call(body, name, list(srcs) + list(lands), [], [n * len(NEAR_PEERS)] * 2, after, True)
    return sems, arrays, token


def _gather_relay(sems, arrays, after, name):
    n = len(arrays) // 2

    def body(*refs):
        send1, recv1 = refs[2 * n], refs[2 * n + 1]
        send2, recv2 = refs[2 * n + 3], refs[2 * n + 4]
        near = _near_copies(refs[:n], refs[n:2 * n], send1, recv1)
        forward = _forward_copies(refs[n:2 * n], send2, recv2)
        for w in range(n):
            for i in range(len(CHIP_PEERS)):
                near[w * len(NEAR_PEERS) + 1 + i][1].wait_recv()
                forward[w * len(CHIP_PEERS) + i][0].start()
        refs[-1][...] = jnp.zeros_like(refs[-1])

    sems2, arrays, token = _split_call(body, name, arrays, sems, [n * len(CHIP_PEERS)] * 2, after, True)
    return sems + sems2, arrays, token


def _gather_finish(sems, arrays, after, name):
    n = len(arrays) // 2

    def body(*refs):
        send1, recv1, send2, recv2 = refs[2 * n:2 * n + 4]
        near = _near_copies(refs[:n], refs[n:2 * n], send1, recv1)
        for out_going, _ in near:
            out_going.wait_send()
        for w in range(n):
            near[w * len(NEAR_PEERS)][1].wait_recv()
        for out_going, arriving in _forward_copies(refs[n:2 * n], send2, recv2):
            out_going.wait_send()
            arriving.wait_recv()

    _, arrays, _ = _split_call(body, name, arrays, sems, [], after, False)
    return arrays[n:]


def _own_slot(a, me, gather):
    mine = a if gather else lax.dynamic_index_in_dim(a, me, 0, keepdims=False)
    return lax.dynamic_update_index_in_dim(lax.empty((N_DEV,) + mine.shape, mine.dtype), mine, me, 0)


def _adamw_math(w, g, m, v):
    m2 = ADAM_B1 * m + (1.0 - ADAM_B1) * g
    v2 = ADAM_B2 * v + (1.0 - ADAM_B2) * (g * g)
    m_hat = m2 / (1.0 - ADAM_B1 ** ADAM_STEP)
    v_hat = v2 / (1.0 - ADAM_B2 ** ADAM_STEP)
    return -ADAM_LR * (m_hat / (jnp.sqrt(v_hat) + ADAM_EPS) + ADAM_WD * w), m2, v2


def _sum_adamw(parts, w, m, v, name):
    rows, cols = w.shape
    tr = rows if rows <= 512 else 256

    def body(p_ref, w_ref, m_ref, v_ref, g_ref, d_ref, m2_ref, v2_ref):
        g = p_ref[0].astype(F32)
        for s in range(1, N_DEV):
            g = g + p_ref[s].astype(F32)
        g_ref[...] = g
        d_ref[...], m2_ref[...], v2_ref[...] = _adamw_math(w_ref[...], g, m_ref[...], v_ref[...])

    blk = pl.BlockSpec((tr, cols), lambda i: (i, 0))
    shape = jax.ShapeDtypeStruct((rows, cols), F32)
    return pl.pallas_call(
        body, out_shape=[shape] * 4, grid=(rows // tr,),
        in_specs=[pl.BlockSpec((N_DEV, tr, cols), lambda i: (0, i, 0)), blk, blk, blk], out_specs=[blk] * 4, name=name,
        compiler_params=_params(1))(parts, w, m, v)


def _slot_sum(parts, name):
    def body(p_ref, g_ref):
        g = p_ref[0]
        for s in range(1, N_DEV):
            g = g + p_ref[s]
        g_ref[...] = g

    return pl.pallas_call(body, out_shape=jax.ShapeDtypeStruct(parts.shape[1:], F32), name=name)(parts)


def _adamw(g, w, m, v, name):
    def body(g_ref, w_ref, m_ref, v_ref, d_ref, m2_ref, v2_ref):
        d_ref[...], m2_ref[...], v2_ref[...] = _adamw_math(w_ref[...], g_ref[...], m_ref[...], v_ref[...])

    shape = jax.ShapeDtypeStruct(w.shape, F32)
    return pl.pallas_call(body, out_shape=[shape] * 3, name=name)(g, w, m, v)


def _pack_small(vals, last=None):
    flat = jnp.concatenate([vals[k].reshape(-1) for k in SMALL])
    tail = jnp.zeros((SMALL_ROWS * D_MODEL - flat.shape[0] - 1,), F32)
    last = jnp.zeros((1,), F32) if last is None else last.reshape(1)
    return jnp.concatenate([flat, tail, last]).reshape(SMALL_ROWS, D_MODEL)


def _unpack_small(packed, like):
    flat, out, at = packed.reshape(-1), {}, 0
    for k in SMALL:
        size = like[k].size
        out[k] = flat[at:at + size].reshape(like[k].shape)
        at += size
    return out


FFN1 = ["w_ffn1_gate", "w_ffn1_up", "w_ffn1_down"]
FFN2 = ["w_ffn2_gate", "w_ffn2_up", "w_ffn2_down"]
MIXER = ["w_in", "w_mem_kv", "w_out"]
TRANSPOSED = ["w_ffn1_gate", "w_ffn1_up", "w_ffn2_gate", "w_ffn2_up", "w_in"]


def _mixer_fwd(x1, mem, tables, sm, big, after_attn=None):
    cos_t, sin_t, gq2, gk2, gmq2, gmk2 = tables
    h2, proj = _in_proj_fwd(x1, sm["g_mix"], big["w_in"], "in_proj_fwd")
    ya, lse = _attn_fwd(proj, cos_t, sin_t, gq2, gk2, sm["sinks"], "attn_fwd")
    token = None if after_attn is None else after_attn(ya)
    yc, cv = _conv_fwd(proj, big["w_dw"], sm["b_dw"], sm["g_conv_ln"], sm["b_conv_ln"], "conv_fwd")
    hm, mkv = _mem_kv_fwd(mem, sm["g_mem"], big["w_mem_kv"], "mem_kv_fwd")
    ym, lse_m = _mem_attn_fwd(proj, mkv, gmq2, gmk2, "mem_attn_fwd")
    x2 = _out_proj_fwd(x1, ya, yc, ym, big["w_out"], "out_proj_fwd", token)
    return x2, (h2, proj, ya, lse, yc, cv, hm, mkv, ym, lse_m)


def _mixer_bwd(dx2, x1, mem, tables, sm, big, saved, after):
    cos_t, sin_t, gq2, gk2, gmq2, gmk2 = tables
    h2, proj, ya, lse, yc, cv, hm, mkv, ym, lse_m = saved
    g = {}
    dmix, g["w_out"] = _out_proj_bwd(dx2, ya, yc, ym, big["w_out"], "out_proj_bwd", after)
    dq, dgq, g["sinks"] = _attn_bwd_q(proj, cos_t, sin_t, gq2, gk2, sm["sinks"], ya, dmix, lse, "attn_bwd_q")
    dk, dv, dgk = _attn_bwd_kv(proj, cos_t, sin_t, gq2, gk2, ya, dmix, lse, "attn_bwd_kv")
    du, g["w_dw"], g["b_dw"], g["g_conv_ln"], g["b_conv_ln"] = _conv_bwd(
        proj, cv, dmix, big["w_dw"], sm["g_conv_ln"], sm["b_conv_ln"], "conv_bwd")
    dmq, dmkv, dgmq, dgmk = _mem_attn_bwd(proj, mkv, gmq2, gmk2, ym, dmix, lse_m, "mem_attn_bwd")
    g["w_mem_kv"], g["g_mem"] = _mem_kv_bwd(mem, sm["g_mem"], hm, big["w_mem_kv"], dmkv, "mem_kv_bwd")
    dx1, g["g_mix"], g["w_in"] = _in_proj_bwd(dq, dk, dv, du, dmq, big["w_in"], h2, x1, sm["g_mix"], dx2, "in_proj_bwd")
    fold = lambda a: a[:, :HEAD_DIM] + a[:, HEAD_DIM:]
    g["g_q"], g["g_k"], g["g_mq"], g["g_mk"] = fold(dgq), fold(dgk), fold(dgmq), fold(dgmk)
    return dx1, g


def _tables(positions, sm):
    pair = lambda a: jnp.tile(a, (1, 2))
    return _rope_tables(positions) + (pair(sm["g_q"]), pair(sm["g_k"]), pair(sm["g_mq"]), pair(sm["g_mk"]))


def kernel(x, mem, positions, g_ffn1, w_ffn1_gate, w_ffn1_up, w_ffn1_down, g_mix, w_in, g_q, g_k, sinks, w_dw, b_dw, g_conv_ln, b_conv_ln, g_mem, w_mem_kv, g_mq, g_mk, w_out, g_ffn2, w_ffn2_gate, w_ffn2_up, w_ffn2_down, loss_target, m_g_ffn1, m_w_ffn1_gate, m_w_ffn1_up, m_w_ffn1_down, m_g_mix, m_w_in, m_g_q, m_g_k, m_sinks, m_w_dw, m_b_dw, m_g_conv_ln, m_b_conv_ln, m_g_mem, m_w_mem_kv, m_g_mq, m_g_mk, m_w_out, m_g_ffn2, m_w_ffn2_gate, m_w_ffn2_up, m_w_ffn2_down, v_g_ffn1, v_w_ffn1_gate, v_w_ffn1_up, v_w_ffn1_down, v_g_mix, v_w_in, v_g_q, v_g_k, v_sinks, v_w_dw, v_b_dw, v_g_conv_ln, v_b_conv_ln, v_g_mem, v_w_mem_kv, v_g_mq, v_g_mk, v_w_out, v_g_ffn2, v_w_ffn2_gate, v_w_ffn2_up, v_w_ffn2_down):
    w = dict(g_ffn1=g_ffn1, w_ffn1_gate=w_ffn1_gate, w_ffn1_up=w_ffn1_up, w_ffn1_down=w_ffn1_down, g_mix=g_mix, w_in=w_in,
             g_q=g_q, g_k=g_k, sinks=sinks, w_dw=w_dw, b_dw=b_dw, g_conv_ln=g_conv_ln, b_conv_ln=b_conv_ln, g_mem=g_mem,
             w_mem_kv=w_mem_kv, g_mq=g_mq, g_mk=g_mk, w_out=w_out, g_ffn2=g_ffn2, w_ffn2_gate=w_ffn2_gate,
             w_ffn2_up=w_ffn2_up, w_ffn2_down=w_ffn2_down)
    mo = dict(g_ffn1=m_g_ffn1, w_ffn1_gate=m_w_ffn1_gate, w_ffn1_up=m_w_ffn1_up, w_ffn1_down=m_w_ffn1_down, g_mix=m_g_mix,
              w_in=m_w_in, g_q=m_g_q, g_k=m_g_k, sinks=m_sinks, w_dw=m_w_dw, b_dw=m_b_dw, g_conv_ln=m_g_conv_ln,
              b_conv_ln=m_b_conv_ln, g_mem=m_g_mem, w_mem_kv=m_w_mem_kv, g_mq=m_g_mq, g_mk=m_g_mk, w_out=m_w_out,
              g_ffn2=m_g_ffn2, w_ffn2_gate=m_w_ffn2_gate, w_ffn2_up=m_w_ffn2_up, w_ffn2_down=m_w_ffn2_down)
    vo = dict(g_ffn1=v_g_ffn1, w_ffn1_gate=v_w_ffn1_gate, w_ffn1_up=v_w_ffn1_up, w_ffn1_down=v_w_ffn1_down, g_mix=v_g_mix,
              w_in=v_w_in, g_q=v_g_q, g_k=v_g_k, sinks=v_sinks, w_dw=v_w_dw, b_dw=v_b_dw, g_conv_ln=v_g_conv_ln,
              b_conv_ln=v_b_conv_ln, g_mem=v_g_mem, w_mem_kv=v_w_mem_kv, g_mq=v_g_mq, g_mk=v_g_mk, w_out=v_w_out,
              g_ffn2=v_g_ffn2, w_ffn2_gate=v_w_ffn2_gate, w_ffn2_up=v_w_ffn2_up, w_ffn2_down=v_w_ffn2_down)
    me = _my_place()[3]
    sm = {k: w[k] for k in SMALL}
    flip = lambda k, a: a.T if k in TRANSPOSED else a
    as_bf16 = lambda names: [flip(k, w[k][0]).astype(BF) for k in names]
    zones = lambda arrays, gather: [_own_slot(a, me, gather) for a in arrays]
    out_g, out_d, out_m, out_v = {}, {}, {}, {}

    def update(names, parts_list):
        for k, parts in zip(names, parts_list):
            new = _sum_adamw(parts, flip(k, w[k][0]), flip(k, mo[k][0]), flip(k, vo[k][0]), "adamw_" + k)
            out_g[k], out_d[k], out_m[k], out_v[k] = [flip(k, a)[None] for a in new]

    w1 = _gather_two_level(as_bf16(FFN1), "gather_ffn1")
    mix_src = as_bf16(MIXER) + [w["w_dw"][0]]
    mix_sems, mix_arrays, mix_token = _gather_start(mix_src, zones(mix_src, True), w1[0], "gather_mixer_start")
    f2_src = as_bf16(FFN2)
    f2_sems, f2_arrays, f2_token = _gather_start(f2_src, zones(f2_src, True), mix_token, "gather_ffn2_start")

    tables = _tables(positions[0], sm)
    h1, gate1, up1, x1 = _ffn_fwd(x[0], sm["g_ffn1"], *w1, None, "ffn1_fwd", after=f2_token)
    mix_sems, mix_arrays, mix_token = _gather_relay(mix_sems, mix_arrays, x1, "gather_mixer_relay")
    got = _gather_finish(mix_sems, mix_arrays, mix_token, "gather_mixer_finish")
    big = dict(w_in=got[0].reshape(IN_COLS, D_MODEL), w_mem_kv=got[1].reshape(D_MODEL, 2 * MQ_COLS),
               w_out=got[2].reshape(D_MODEL, D_MODEL), w_dw=got[3].transpose(1, 0, 2).reshape(CONV_WIDTH, CONV_CH))
    relayed = []

    def relay_ffn2(ya):
        relayed.extend(_gather_relay(f2_sems, f2_arrays, ya, "gather_ffn2_relay"))
        return relayed[2]

    x2, saved = _mixer_fwd(x1, mem[0], tables, sm, big, relay_ffn2)
    w2 = _gather_finish(relayed[0], relayed[1], x2, "gather_ffn2_finish")
    h3, gate2, up2, dy, loss_part = _ffn_fwd(x2, sm["g_ffn2"], *w2, loss_target[0], "ffn2_fwd")

    grads = {}
    dyb2, act2, dgate2, dup2, dx2, grads["g_ffn2"] = _ffn_bwd_act(dy, x2, sm["g_ffn2"], gate2, up2, *w2, "ffn2_bwd_act")
    g_f2 = list(_ffn_bwd_w(h3, dyb2, act2, dgate2, dup2, "ffn2_bwd_w"))
    r_f2 = _exchange_start(g_f2, zones(g_f2, False), [False] * 3, dx2, "scatter_ffn2_start")
    dx1, g_mid = _mixer_bwd(dx2, x1, mem[0], tables, sm, big, saved, r_f2[4])
    grads.update(g_mid)
    g_mix = [g_mid["w_in"].reshape(N_DEV, IN_COLS // N_DEV, D_MODEL),
             g_mid["w_mem_kv"].reshape(N_DEV, D_MODEL // N_DEV, 2 * MQ_COLS),
             g_mid["w_out"].reshape(N_DEV, D_MODEL // N_DEV, D_MODEL)]
    r_mix = _exchange_start(g_mix, zones(g_mix, False), [False] * 3, dx1, "scatter_mixer_start")
    dyb1, act1, dgate1, dup1, grad_x, grads["g_ffn1"] = _ffn_bwd_act(dx1, x[0], sm["g_ffn1"], gate1, up1, *w1,
                                                                     "ffn1_bwd_act", after=r_mix[4])
    dw_flat = grads["w_dw"].reshape(-1)
    packed = jnp.concatenate([_pack_small(grads, loss_part[0, 0]),
                              jnp.pad(dw_flat, (0, SMALL_ROWS * D_MODEL - dw_flat.shape[0])).reshape(SMALL_ROWS, D_MODEL)])
    r_small = _exchange_start([packed], zones([packed], True), [True], grad_x, "gather_small_start")
    g_f1 = list(_ffn_bwd_w(h1, dyb1, act1, dgate1, dup1, "ffn1_bwd_w"))
    last_kind = [False] * 3
    r_f1 = _exchange_start(g_f1, zones(g_f1, False), last_kind, r_small[4], "scatter_ffn1_start")
    small_sum = _slot_sum(_exchange_wait(*r_small[:4], [True], r_f1[4], "gather_small_wait")[1][0], "small_grad_sum")
    loss = small_sum[SMALL_ROWS - 1, D_MODEL - 1]
    g_small = small_sum[:SMALL_ROWS]
    d_small, m2, v2 = _adamw(g_small, _pack_small(w), _pack_small(mo), _pack_small(vo), "adamw_small")
    for dst, val in ((out_g, g_small), (out_d, d_small), (out_m, m2), (out_v, v2)):
        dst.update(_unpack_small(val, sm))
    g_dw = small_sum[SMALL_ROWS:].reshape(-1)[:CONV_WIDTH * CONV_CH].reshape(CONV_WIDTH, CONV_CH)
    g_dw = lax.dynamic_slice_in_dim(g_dw, me * (CONV_CH // N_DEV), CONV_CH // N_DEV, axis=1)
    d, m2, v2 = _adamw(g_dw, w["w_dw"][0], mo["w_dw"][0], vo["w_dw"][0], "adamw_w_dw")
    out_g["w_dw"], out_d["w_dw"], out_m["w_dw"], out_v["w_dw"] = g_dw[None], d[None], m2[None], v2[None]

    all_done = lambda names: sum(out_d[k][:, :1, :1] for k in names)
    update(FFN2, _exchange_wait(*r_f2[:4], [False] * 3, d_small, "scatter_ffn2_wait")[1])
    update(MIXER, _exchange_wait(*r_mix[:4], [False] * 3, all_done(FFN2), "scatter_mixer_wait")[1])
    srcs, lands = _exchange_wait(*r_f1[:4], last_kind, all_done(MIXER), "scatter_ffn1_wait_gate", 0, 1)
    update(FFN1[:1], lands[:1])
    srcs, lands = _exchange_wait(r_f1[0], r_f1[1], srcs[1:], lands[1:], last_kind[1:], out_d[FFN1[0]],
                                 "scatter_ffn1_wait_up", 1, 1)
    update(FFN1[1:2], lands[:1])
    srcs, lands = _exchange_wait(r_f1[0], r_f1[1], srcs[1:], lands[1:], last_kind[2:], out_d[FFN1[1]],
                                 "scatter_ffn1_wait_down", 2)
    update(FFN1[2:], lands[:1])

    return (loss, grad_x[None], *[out_g[k] for k in WEIGHTS], *[out_d[k] for k in WEIGHTS], *[out_m[k] for k in WEIGHTS],
            *[out_v[k] for k in WEIGHTS])
```
